```python
import jax, jax.numpy as jnp
from jax import lax
import numpy as np

D_MODEL = 1024
BATCH = 4
SEQ = 4096
DEPTH = 2

CHUNK = 64
N_A_LAYERS = DEPTH // 2
N_B_LAYERS = DEPTH - N_A_LAYERS
EPS = 1e-6
SGU_BLOCK = 128
SGU_WIDTH = 2 * D_MODEL
SGU_GROUPS = 8
SGU_GROUP_DIM = SGU_WIDTH // SGU_GROUPS
N_HEADS = 16
HEAD_DIM = D_MODEL // N_HEADS
Q_BLOCK = 128
N_GROUPS = 4
EXPERTS_PER_GROUP = 8
N_EXPERTS = N_GROUPS * EXPERTS_PER_GROUP
TOP_K = 2
D_EXPERT = D_MODEL // 2
EXPERT_BLOCK = 128

kernel_name = "yoco_gmlp_fox_hmoe_trunk"


def rmsnorm(x, g):
    xf = x.astype(jnp.float32)
    y = xf * lax.rsqrt(jnp.mean(xf * xf, axis=-1, keepdims=True) + EPS)
    return (y * g.astype(jnp.float32)).astype(x.dtype)


def sgu_mixer(h, w_in, b_in, v_norm, w_s, b_s, w_out):
    B, S, _ = h.shape
    z = jax.nn.gelu(h @ w_in + b_in)
    u, v = jnp.split(z, 2, axis=-1)
    v = rmsnorm(v, v_norm)
    nb = S // SGU_BLOCK
    v = v.reshape(B, nb, SGU_BLOCK, SGU_GROUPS, SGU_GROUP_DIM)
    chunk_id = jnp.arange(SGU_BLOCK) // CHUNK
    mask = chunk_id[:, None] >= chunk_id[None, :]
    w_sm = jnp.where(mask[None], w_s, jnp.zeros_like(w_s))
    v = jnp.einsum('gts,bnsgc->bntgc', w_sm, v) + b_s.T[None, None, :, :, None]
    gated = u * v.reshape(B, S, SGU_WIDTH)
    return gated @ w_out


def shared_kv(x_kv, kv_norm, kv_w, kv_b_f, k_norm):
    B, S, _ = x_kv.shape
    h = rmsnorm(x_kv, kv_norm)
    kvf = h @ kv_w
    k, v, f = jnp.split(kvf, [D_MODEL, 2 * D_MODEL], axis=-1)
    k = rmsnorm(k.reshape(B, S, N_HEADS, HEAD_DIM), k_norm)
    v = v.reshape(B, S, N_HEADS, HEAD_DIM)
    log_f = jax.nn.log_sigmoid((f + kv_b_f).astype(jnp.float32))
    cum = jnp.cumsum(log_f, axis=1).transpose(0, 2, 1)
    return k, v, cum


def forgetting_attention(h, k, v, cum, w_qg, q_norm, w_out):
    B, S, _ = h.shape
    q, gate = jnp.split(h @ w_qg, 2, axis=-1)
    q = rmsnorm(q.reshape(B, S, N_HEADS, HEAD_DIM), q_norm)
    scale = HEAD_DIM ** -0.5
    outs = []
    for i in range(S // Q_BLOCK):
        q0, q1 = i * Q_BLOCK, (i + 1) * Q_BLOCK
        kb, vb = k[:, :q1], v[:, :q1]
        s = jnp.einsum('bqhd,bkhd->bhqk', q[:, q0:q1], kb).astype(jnp.float32) * scale
        decay = cum[:, :, q0:q1, None] - cum[:, :, None, :q1]
        mask = (q0 + jnp.arange(Q_BLOCK))[:, None] >= jnp.arange(q1)[None, :]
        s = jnp.where(mask, s + decay, -jnp.inf)
        p = jax.nn.softmax(s, axis=-1)
        outs.append(jnp.einsum('bhqk,bkhd->bqhd', p.astype(vb.dtype), vb))
    o = jnp.concatenate(outs, axis=1).reshape(B, S, D_MODEL)
    return (o * jax.nn.sigmoid(gate)) @ w_out


def hierarchical_moe(h, w_group, b_group, w_expert, b_expert, w1, w3, w2):
    B, S, D = h.shape
    T = B * S
    xf = h.reshape(T, D)
    tok = jnp.arange(T)
    g_logits = (xf @ w_group + b_group).astype(jnp.float32)
    g_prob = jax.nn.softmax(g_logits, axis=-1)
    g_sel = jnp.argmax(g_logits, axis=-1).astype(jnp.int32)
    g_w = g_prob[tok, g_sel][:, None]
    e_logits = (xf @ w_expert + b_expert).astype(jnp.float32)
    e_logits = e_logits.reshape(T, N_GROUPS, EXPERTS_PER_GROUP)[tok, g_sel]
    e_prob = jax.nn.softmax(e_logits, axis=-1)
    top_p, top_i = lax.top_k(e_prob, TOP_K)
    top_p = top_p / jnp.sum(top_p, axis=-1, keepdims=True)
    weights = g_w * top_p
    expert_id = g_sel[:, None] * EXPERTS_PER_GROUP + top_i.astype(jnp.int32)
    A = T * TOP_K
    flat_e = expert_id.reshape(A)
    flat_tok = jnp.arange(A, dtype=jnp.int32) // TOP_K
    flat_w = weights.reshape(A)
    order = jnp.argsort(flat_e)
    sorted_e = flat_e[order]
    counts = jnp.zeros((N_EXPERTS,), jnp.int32).at[flat_e].add(1)
    padded = (counts + EXPERT_BLOCK - 1) // EXPERT_BLOCK * EXPERT_BLOCK
    start = jnp.cumsum(counts) - counts
    pend = jnp.cumsum(padded)
    pstart = pend - padded
    dest = pstart[sorted_e] + (jnp.arange(A, dtype=jnp.int32) - start[sorted_e])
    n_blocks = -(-A // EXPERT_BLOCK) + N_EXPERTS
    n_rows = n_blocks * EXPERT_BLOCK
    tok_buf = jnp.full((n_rows,), T, jnp.int32).at[dest].set(flat_tok[order])
    w_buf = jnp.zeros((n_rows,), jnp.float32).at[dest].set(flat_w[order])
    block_start = jnp.arange(n_blocks, dtype=jnp.int32) * EXPERT_BLOCK
    block_e = jnp.minimum(jnp.searchsorted(pend, block_start, side='right'),
                          N_EXPERTS - 1).astype(jnp.int32)
    x_pad = jnp.concatenate([xf, jnp.zeros((1, D), xf.dtype)], axis=0)
    xs = x_pad[tok_buf].reshape(n_blocks, EXPERT_BLOCK, D)

    def expert_block(args):
        xb, e = args
        hid = jax.nn.silu(xb @ w1[e]) * (xb @ w3[e])
        return hid @ w2[e]

    ys = lax.map(expert_block, (xs, block_e)).reshape(n_rows, D)
    ys = ys * w_buf[:, None].astype(ys.dtype)
    out = jnp.zeros((T + 1, D), ys.dtype).at[tok_buf].add(ys)[:T]
    return out.reshape(B, S, D)


def setup_inputs(seed: int = 0) -> dict:
    key = jax.random.key(seed)
    ks = jax.random.split(key, 32)

    def nrm(k, shape, scale):
        return jax.random.normal(k, shape, jnp.float32) * scale

    def gain(k, shape):
        return 1.0 + 0.02 * jax.random.normal(k, shape, jnp.float32)

    NA, NB, L = N_A_LAYERS, N_B_LAYERS, DEPTH
    return {
        "x": nrm(ks[0], (BATCH, SEQ, D_MODEL), 1.0),
        "a_norm": gain(ks[1], (NA, D_MODEL)),
        "a_w_in": nrm(ks[2], (NA, D_MODEL, 2 * SGU_WIDTH), D_MODEL ** -0.5),
        "a_b_in": nrm(ks[3], (NA, 2 * SGU_WIDTH), 0.01),
        "a_v_norm": gain(ks[4], (NA, SGU_WIDTH)),
        "a_w_s": nrm(ks[5], (NA, SGU_GROUPS, SGU_BLOCK, SGU_BLOCK), SGU_BLOCK ** -0.5),
        "a_b_s": gain(ks[6], (NA, SGU_GROUPS, SGU_BLOCK)),
        "a_w_out": nrm(ks[7], (NA, SGU_WIDTH, D_MODEL), SGU_WIDTH ** -0.5),
        "kv_norm": gain(ks[8], (D_MODEL,)),
        "kv_w": nrm(ks[9], (D_MODEL, 2 * D_MODEL + N_HEADS), D_MODEL ** -0.5),
        "kv_b_f": nrm(ks[10], (N_HEADS,), 0.01),
        "k_norm": gain(ks[11], (HEAD_DIM,)),
        "b_norm": gain(ks[12], (NB, D_MODEL)),
        "b_w_qg": nrm(ks[13], (NB, D_MODEL, 2 * D_MODEL), D_MODEL ** -0.5),
        "q_norm": gain(ks[14], (NB, HEAD_DIM)),
        "b_w_out": nrm(ks[15], (NB, D_MODEL, D_MODEL), D_MODEL ** -0.5),
        "m_norm": gain(ks[16], (L, D_MODEL)),
        "m_w_group": nrm(ks[17], (L, D_MODEL, N_GROUPS), D_MODEL ** -0.5),
        "m_b_group": nrm(ks[18], (L, N_GROUPS), 0.01),
        "m_w_expert": nrm(ks[19], (L, D_MODEL, N_EXPERTS), D_MODEL ** -0.5),
        "m_b_expert": nrm(ks[20], (L, N_EXPERTS), 0.01),
        "m_w1": nrm(ks[21], (L, N_EXPERTS, D_MODEL, D_EXPERT), D_MODEL ** -0.5),
        "m_w3": nrm(ks[22], (L, N_EXPERTS, D_MODEL, D_EXPERT), D_MODEL ** -0.5),
        "m_w2": nrm(ks[23], (L, N_EXPERTS, D_EXPERT, D_MODEL), D_EXPERT ** -0.5),
    }


def reference(x, a_norm, a_w_in, a_b_in, a_v_norm, a_w_s, a_b_s, a_w_out,
              kv_norm, kv_w, kv_b_f, k_norm,
              b_norm, b_w_qg, q_norm, b_w_out,
              m_norm, m_w_group, m_b_group, m_w_expert, m_b_expert, m_w1, m_w3, m_w2):
    k = v = cum = None
    for layer in range(DEPTH):
        if layer < N_A_LAYERS:
            i = layer
            x = x + sgu_mixer(rmsnorm(x, a_norm[i]), a_w_in[i], a_b_in[i], a_v_norm[i],
                              a_w_s[i], a_b_s[i], a_w_out[i])
        else:
            if layer == N_A_LAYERS:
                k, v, cum = shared_kv(x, kv_norm, kv_w, kv_b_f, k_norm)
            j = layer - N_A_LAYERS
            x = x + forgetting_attention(rmsnorm(x, b_norm[j]), k, v, cum,
                                         b_w_qg[j], q_norm[j], b_w_out[j])
        x = x + hierarchical_moe(rmsnorm(x, m_norm[layer]), m_w_group[layer], m_b_group[layer],
                                 m_w_expert[layer], m_b_expert[layer],
                                 m_w1[layer], m_w3[layer], m_w2[layer])
    return x
```

```python
import functools
import math

import jax
import jax.numpy as jnp
from jax import lax
from jax.experimental import pallas as pl
from jax.experimental.pallas import tpu as pltpu

D_MODEL = 1024
EPS = 1e-6
SGU_BLOCK = 128
SGU_CHUNK = 64
CHUNK_SHIFT = SGU_CHUNK.bit_length() - 1
SGU_WIDTH = 2 * D_MODEL
SGU_GROUPS = 8
SGU_GROUP_DIM = SGU_WIDTH // SGU_GROUPS
N_HEADS = 16
HEAD_DIM = D_MODEL // N_HEADS
N_GROUPS = 4
EXPERTS_PER_GROUP = 8
N_EXPERTS = N_GROUPS * EXPERTS_PER_GROUP
GROUP_SHIFT = EXPERTS_PER_GROUP.bit_length() - 1
TOP_K = 2
D_EXPERT = D_MODEL // 2

LANES = 128
VMEM_LIMIT_BYTES = 56 * 1024 * 1024

ROW_TILE = 256
ROUTE_TILE = 512
EXPERT_ROWS = 256
MOVE_TILE = 256
ATT_BLOCK = 256
HEAD_PAD = 2 * HEAD_DIM
ROUTER_ROWS = 40
AUX_ONE0 = HEAD_DIM
AUX_CUM0 = HEAD_DIM + 3

bf16 = jnp.bfloat16
f32 = jnp.float32


def _dot(a, b, precision=None):
    return jnp.dot(a, b, preferred_element_type=f32, precision=precision)


def _dot_nt(a, b, precision=None):
    return lax.dot_general(a, b, (((1,), (1,)), ((), ())), preferred_element_type=f32,
                           precision=precision)


def _rms(x, g):
    ms = jnp.mean(x * x, axis=-1, keepdims=True)
    return x * lax.rsqrt(ms + EPS) * g


def _gelu_tanh(z):
    c = math.sqrt(2.0 / math.pi)
    return z * (0.5 * (1.0 + jnp.tanh(c * (z + 0.044715 * (z * z * z)))))


def _split3(c):
    hi = c.astype(bf16).astype(f32)
    r = c - hi
    mid = r.astype(bf16).astype(f32)
    lo = r - mid
    return hi, mid, lo


def _const_spec(shape):
    nd = len(shape)
    return pl.BlockSpec(shape, lambda *_: (0,) * nd)


def _params(sem):
    return pltpu.CompilerParams(dimension_semantics=sem, vmem_limit_bytes=VMEM_LIMIT_BYTES)


def _router_logits_t(x_new, mn, wr):
    hm = _rms(x_new, mn)
    return _dot_nt(wr, hm, precision=lax.Precision.HIGHEST)


def _sgu_kernel(x_ref, an_ref, win_ref, bin_ref, vn_ref, ws_ref, bst_ref, wout_ref, mn_ref,
                wr_ref, x1_ref, lg_ref, u_scr, v_scr, gated_scr):
    x = x_ref[...]
    h = _rms(x, an_ref[...]).astype(bf16)
    u_scr[...] = _gelu_tanh(_dot(h, win_ref[:, :SGU_WIDTH]) + bin_ref[:, :SGU_WIDTH])
    v = _gelu_tanh(_dot(h, win_ref[:, SGU_WIDTH:]) + bin_ref[:, SGU_WIDTH:])
    v_scr[...] = _rms(v, vn_ref[...]).astype(bf16)

    t_chunk = lax.broadcasted_iota(jnp.int32, (SGU_BLOCK, SGU_BLOCK), 0) >> CHUNK_SHIFT
    s_chunk = lax.broadcasted_iota(jnp.int32, (SGU_BLOCK, SGU_BLOCK), 1) >> CHUNK_SHIFT
    causal = t_chunk >= s_chunk
    for g in range(SGU_GROUPS):
        wsm = jnp.where(causal, ws_ref[g], 0.0).astype(bf16)
        cols = slice(g * SGU_GROUP_DIM, (g + 1) * SGU_GROUP_DIM)
        for sb in range(ROW_TILE // SGU_BLOCK):
            rows = slice(sb * SGU_BLOCK, (sb + 1) * SGU_BLOCK)
            mixed = _dot(wsm, v_scr[rows, cols]) + bst_ref[:, g:g + 1]
            gated_scr[rows, cols] = (u_scr[rows, cols] * mixed).astype(bf16)

    x1 = x + _dot(gated_scr[...], wout_ref[...])
    x1_ref[...] = x1
    lg_ref[...] = _router_logits_t(x1, mn_ref[...], wr_ref[...])


def _sgu(x2d, a_norm, w_in, b_in, v_norm, w_s, b_s_t, w_out, m_norm, w_router):
    T = x2d.shape[0]
    return pl.pallas_call(
        _sgu_kernel,
        grid=(T // ROW_TILE,),
        in_specs=[
            pl.BlockSpec((ROW_TILE, D_MODEL), lambda i: (i, 0)),
            _const_spec((1, D_MODEL)),
            _const_spec((D_MODEL, 2 * SGU_WIDTH)),
            _const_spec((1, 2 * SGU_WIDTH)),
            _const_spec((1, SGU_WIDTH)),
            _const_spec((SGU_GROUPS, SGU_BLOCK, SGU_BLOCK)),
            _const_spec((SGU_BLOCK, SGU_GROUPS)),
            _const_spec((SGU_WIDTH, D_MODEL)),
            _const_spec((1, D_MODEL)),
            _const_spec((ROUTER_ROWS, D_MODEL)),
        ],
        out_specs=[
            pl.BlockSpec((ROW_TILE, D_MODEL), lambda i: (i, 0)),
            pl.BlockSpec((ROUTER_ROWS, ROW_TILE), lambda i: (0, i)),
        ],
        out_shape=[
            jax.ShapeDtypeStruct((T, D_MODEL), f32),
            jax.ShapeDtypeStruct((ROUTER_ROWS, T), f32),
        ],
        scratch_shapes=[
            pltpu.VMEM((ROW_TILE, SGU_WIDTH), f32),
            pltpu.VMEM((ROW_TILE, SGU_WIDTH), bf16),
            pltpu.VMEM((ROW_TILE, SGU_WIDTH), bf16),
        ],
        compiler_params=_params(("parallel",)),
        name="sgu",
    )(x2d, a_norm, w_in, b_in, v_norm, w_s, b_s_t, w_out, m_norm, w_router)


def _route_kernel(lg_ref, be_ref, bg_ref, oi_ref, ow_ref, cnt_ref, carry_scr):
    step = pl.program_id(0)

    @pl.when(step == 0)
    def _():
        carry_scr[...] = jnp.zeros_like(carry_scr)

    lg = lg_ref[...]
    e_l = lg[0:N_EXPERTS, :] + be_ref[...]
    g_l = lg[N_EXPERTS:N_EXPERTS + 8, :] + bg_ref[...]
    g_row = lax.broadcasted_iota(jnp.int32, g_l.shape, 0).astype(f32)
    g_l = jnp.where(g_row < N_GROUPS, g_l, -jnp.inf)
    g_max = jnp.max(g_l, axis=0, keepdims=True)
    g_sel = jnp.min(jnp.where(g_l == g_max, g_row, 8.0), axis=0, keepdims=True)
    g_den = jnp.sum(jnp.exp(g_l - g_max), axis=0, keepdims=True)
    g_w = 1.0 / g_den

    e_row_i = lax.broadcasted_iota(jnp.int32, e_l.shape, 0)
    e_row = e_row_i.astype(f32)
    e_grp = (e_row_i >> GROUP_SHIFT).astype(f32)
    e_in = jnp.where(e_grp == g_sel, e_l, -jnp.inf)
    m1 = jnp.max(e_in, axis=0, keepdims=True)
    i1 = jnp.min(jnp.where(e_in == m1, e_row, float(N_EXPERTS)), axis=0, keepdims=True)
    e_in2 = jnp.where(e_row == i1, -jnp.inf, e_in)
    m2 = jnp.max(e_in2, axis=0, keepdims=True)
    i2 = jnp.min(jnp.where(e_in2 == m2, e_row, float(N_EXPERTS)), axis=0, keepdims=True)
    t = jnp.exp(m2 - m1)
    p1 = 1.0 / (1.0 + t)
    p2 = t / (1.0 + t)

    hit1 = e_row == i1
    hit2 = e_row == i2
    member = jnp.logical_or(hit1, hit2)
    tt = lg.shape[1]
    before = (lax.broadcasted_iota(jnp.int32, (tt, tt), 0)
              < lax.broadcasted_iota(jnp.int32, (tt, tt), 1))
    prefix = _dot(member.astype(bf16), before.astype(bf16))
    rank_full = prefix + carry_scr[...]
    rank1 = jnp.sum(jnp.where(hit1, rank_full, 0.0), axis=0, keepdims=True)
    rank2 = jnp.sum(jnp.where(hit2, rank_full, 0.0), axis=0, keepdims=True)
    carry_new = carry_scr[...] + jnp.sum(member.astype(f32), axis=1, keepdims=True)
    carry_scr[...] = carry_new

    oi_ref[...] = jnp.zeros(oi_ref.shape, jnp.int32)
    oi_ref[0:1, :] = i1.astype(jnp.int32)
    oi_ref[1:2, :] = i2.astype(jnp.int32)
    oi_ref[2:3, :] = rank1.astype(jnp.int32)
    oi_ref[3:4, :] = rank2.astype(jnp.int32)
    ow_ref[...] = jnp.zeros(ow_ref.shape, f32)
    ow_ref[0:1, :] = g_w * p1
    ow_ref[1:2, :] = g_w * p2
    cnt_ref[...] = jnp.broadcast_to(carry_new, cnt_ref.shape)


def _route(logits_t, b_expert, b_group):
    T = logits_t.shape[1]
    return pl.pallas_call(
        _route_kernel,
        grid=(T // ROUTE_TILE,),
        in_specs=[
            pl.BlockSpec((ROUTER_ROWS, ROUTE_TILE), lambda i: (0, i)),
            _const_spec((N_EXPERTS, 1)),
            _const_spec((8, 1)),
        ],
        out_specs=[
            pl.BlockSpec((8, ROUTE_TILE), lambda i: (0, i)),
            pl.BlockSpec((8, ROUTE_TILE), lambda i: (0, i)),
            _const_spec((N_EXPERTS, LANES)),
        ],
        out_shape=[
            jax.ShapeDtypeStruct((8, T), jnp.int32),
            jax.ShapeDtypeStruct((8, T), f32),
            jax.ShapeDtypeStruct((N_EXPERTS, LANES), f32),
        ],
        scratch_shapes=[pltpu.VMEM((N_EXPERTS, 1), f32)],
        compiler_params=_params(("arbitrary",)),
        name="route",
    )(logits_t, b_expert, b_group)


def _row_copy(src, s, dst, d, sem):
    return pltpu.make_async_copy(src.at[pl.ds(s, 1)], dst.at[pl.ds(d, 1)], sem)


def _dispatch_kernel(dest_ref, x_hbm, xs_in_hbm, xs_hbm, sem):
    del xs_in_hbm
    base = pl.program_id(0) * MOVE_TILE

    def start(j, c):
        for k in range(TOP_K):
            _row_copy(x_hbm, base + j, xs_hbm, dest_ref[k, j], sem).start()
        return c

    lax.fori_loop(0, MOVE_TILE, start, 0)

    def wait(j, c):
        for k in range(TOP_K):
            _row_copy(x_hbm, 0, xs_hbm, 0, sem).wait()
        return c

    lax.fori_loop(0, MOVE_TILE, wait, 0)


def _dispatch(dest_tiles, x2d, n_rows):
    T = x2d.shape[0]
    xs0 = jnp.zeros((n_rows, D_MODEL), f32)
    return pl.pallas_call(
        _dispatch_kernel,
        grid=(T // MOVE_TILE,),
        in_specs=[
            pl.BlockSpec((None, TOP_K, MOVE_TILE), lambda i: (i, 0, 0),
                         memory_space=pltpu.SMEM),
            pl.BlockSpec(memory_space=pl.ANY),
            pl.BlockSpec(memory_space=pl.ANY),
        ],
        out_specs=pl.BlockSpec(memory_space=pl.ANY),
        out_shape=jax.ShapeDtypeStruct((n_rows, D_MODEL), f32),
        scratch_shapes=[pltpu.SemaphoreType.DMA(())],
        input_output_aliases={2: 0},
        compiler_params=_params(("arbitrary",)),
        name="dispatch",
    )(dest_tiles, x2d, xs0)


def _expert_kernel(be_ref, nu_ref, xs_ref, mn_ref, w1_ref, w3_ref, w2_ref, ys_ref,
                   w1_scr, w3_scr, w2_scr):
    b = pl.program_id(0)

    @pl.when(b < nu_ref[0])
    def _():
        new_expert = jnp.logical_or(b == 0, be_ref[b] != be_ref[jnp.maximum(b - 1, 0)])

        @pl.when(new_expert)
        def _():
            w1_scr[...] = w1_ref[0].astype(bf16)
            w3_scr[...] = w3_ref[0].astype(bf16)
            w2_scr[...] = w2_ref[0].astype(bf16)

        h = _rms(xs_ref[...], mn_ref[...]).astype(bf16)
        a = _dot(h, w1_scr[...])
        c = _dot(h, w3_scr[...])
        hid = (a * jax.nn.sigmoid(a)) * c
        ys_ref[...] = _dot(hid.astype(bf16), w2_scr[...])

    @pl.when(b >= nu_ref[0])
    def _():
        ys_ref[...] = jnp.zeros_like(ys_ref)


def _experts(block_e, n_used, xs, m_norm, w1, w3, w2):
    n_rows = xs.shape[0]
    n_blocks = n_rows // EXPERT_ROWS

    def row_map(b, be, nu):
        return (jnp.minimum(b, nu[0] - 1), 0)

    def w_map(b, be, nu):
        return (be[b], 0, 0)

    grid_spec = pltpu.PrefetchScalarGridSpec(
        num_scalar_prefetch=2,
        grid=(n_blocks,),
        in_specs=[
            pl.BlockSpec((EXPERT_ROWS, D_MODEL), row_map),
            pl.BlockSpec((1, D_MODEL), lambda b, be, nu: (0, 0)),
            pl.BlockSpec((1, D_MODEL, D_EXPERT), w_map),
            pl.BlockSpec((1, D_MODEL, D_EXPERT), w_map),
            pl.BlockSpec((1, D_EXPERT, D_MODEL), w_map),
        ],
        out_specs=pl.BlockSpec((EXPERT_ROWS, D_MODEL), lambda b, be, nu: (b, 0)),
        scratch_shapes=[
            pltpu.VMEM((D_MODEL, D_EXPERT), bf16),
            pltpu.VMEM((D_MODEL, D_EXPERT), bf16),
            pltpu.VMEM((D_EXPERT, D_MODEL), bf16),
        ],
    )
    return pl.pallas_call(
        _expert_kernel,
        grid_spec=grid_spec,
        out_shape=jax.ShapeDtypeStruct((n_rows, D_MODEL), f32),
        compiler_params=_params(("arbitrary",)),
        name="experts",
    )(block_e, n_used, xs, m_norm, w1, w3, w2)


def _combine_kernel(dest_ref, x_ref, w_ref, ys_hbm, out_ref, buf, sem):
    def start(j, c):
        for k in range(TOP_K):
            pltpu.make_async_copy(ys_hbm.at[pl.ds(dest_ref[k, j], 1)],
                                  buf.at[k, pl.ds(j, 1)], sem).start()
        return c

    lax.fori_loop(0, MOVE_TILE, start, 0)

    def wait(j, c):
        for k in range(TOP_K):
            pltpu.make_async_copy(ys_hbm.at[pl.ds(0, 1)], buf.at[k, pl.ds(0, 1)], sem).wait()
        return c

    lax.fori_loop(0, MOVE_TILE, wait, 0)
    w = w_ref[...]
    out_ref[...] = x_ref[...] + (w[:, 0:1] * buf[0] + w[:, 1:2] * buf[1])


def _combine(dest_tiles, x2d, w_cols, ys):
    T = x2d.shape[0]
    return pl.pallas_call(
        _combine_kernel,
        grid=(T // MOVE_TILE,),
        in_specs=[
            pl.BlockSpec((None, TOP_K, MOVE_TILE), lambda i: (i, 0, 0),
                         memory_space=pltpu.SMEM),
            pl.BlockSpec((MOVE_TILE, D_MODEL), lambda i: (i, 0)),
            pl.BlockSpec((MOVE_TILE, 8), lambda i: (i, 0)),
            pl.BlockSpec(memory_space=pl.ANY),
        ],
        out_specs=pl.BlockSpec((MOVE_TILE, D_MODEL), lambda i: (i, 0)),
        out_shape=jax.ShapeDtypeStruct((T, D_MODEL), f32),
        scratch_shapes=[
            pltpu.VMEM((TOP_K, MOVE_TILE, D_MODEL), f32),
            pltpu.SemaphoreType.DMA(()),
        ],
        compiler_params=_params(("arbitrary",)),
        name="combine",
    )(dest_tiles, x2d, w_cols, ys)


def _moe(x2d, logits_t, m_norm, b_group, b_expert, w1, w3, w2):
    T = x2d.shape[0]
    n_blocks = (T * TOP_K) // EXPERT_ROWS + N_EXPERTS
    n_rows = n_blocks * EXPERT_ROWS

    be = b_expert.reshape(N_EXPERTS, 1)
    bg = jnp.concatenate([b_group, jnp.zeros((8 - N_GROUPS,), f32)]).reshape(8, 1)
    oi, ow, cnt = _route(logits_t, be, bg)

    counts = cnt[:, 0].astype(jnp.int32)
    padded = (counts + EXPERT_ROWS - 1) // EXPERT_ROWS * EXPERT_ROWS
    pend = jnp.cumsum(padded)
    pstart = pend - padded
    n_used = (pend[-1] // EXPERT_ROWS).astype(jnp.int32).reshape(1)
    block_start = jnp.arange(n_blocks, dtype=jnp.int32) * EXPERT_ROWS
    block_e = jnp.minimum(jnp.searchsorted(pend, block_start, side="right"),
                          N_EXPERTS - 1).astype(jnp.int32)
    dest = pstart[oi[0:2]] + oi[2:4]
    dest_tiles = dest.reshape(TOP_K, T // MOVE_TILE, MOVE_TILE).transpose(1, 0, 2)
    w_cols = ow.T

    xs = _dispatch(dest_tiles, x2d, n_rows)
    ys = _experts(block_e, n_used, xs, m_norm, w1, w3, w2)
    return _combine(dest_tiles, x2d, w_cols, ys)


def _log_sigmoid(x):
    return jnp.minimum(x, 0.0) - jnp.log1p(jnp.exp(-jnp.abs(x)))


def _proj_kernel(tiles_per_seq, x_ref, kvn_ref, bn_ref, wkt_ref, wv_ref, wf_ref, bf_ref,
                 wq_ref, wg_ref, gq_ref, gk_ref, q_ref, kt_ref, v_ref, sg_ref, carry_scr):
    i = pl.program_id(0)
    x = x_ref[...]
    hkv = _rms(x, kvn_ref[...]).astype(bf16)
    hq = _rms(x, bn_ref[...]).astype(bf16)

    v_ref[...] = _dot(hkv, wv_ref[...]).astype(bf16)
    sg_ref[...] = jax.nn.sigmoid(_dot(hq, wg_ref[...])).astype(bf16)

    logf = _log_sigmoid(_dot(hkv, wf_ref[...]) + bf_ref[...])
    tm = x.shape[0]
    incl = (lax.broadcasted_iota(jnp.int32, (tm, tm), 0)
            >= lax.broadcasted_iota(jnp.int32, (tm, tm), 1)).astype(f32)
    @pl.when(i % tiles_per_seq == 0)
    def _():
        carry_scr[...] = jnp.zeros_like(carry_scr)

    cum = _dot(incl, logf, precision=lax.Precision.HIGHEST) + carry_scr[...]
    carry_scr[...] = cum[tm - 1:tm, :]
    cum_t = cum.T

    lane = lax.broadcasted_iota(jnp.int32, (tm, HEAD_PAD), 1)
    sub = lax.broadcasted_iota(jnp.int32, (HEAD_PAD, tm), 0)
    q_raw = _dot(hq, wq_ref[...])
    k_raw_t = _dot_nt(wkt_ref[...], hkv)
    scale = HEAD_DIM ** -0.5
    for h in range(N_HEADS):
        sl = slice(h * HEAD_PAD, (h + 1) * HEAD_PAD)
        qb = q_raw[:, sl]
        q_ms = jnp.sum(qb * qb, axis=-1, keepdims=True) * (1.0 / HEAD_DIM)
        qn = qb * lax.rsqrt(q_ms + EPS) * gq_ref[...] * scale
        c_hi, c_mid, c_lo = _split3(cum[:, h:h + 1])
        qa = jnp.where(lane < HEAD_DIM, qn,
             jnp.where(lane < AUX_CUM0, 1.0,
             jnp.where(lane == AUX_CUM0, c_hi,
             jnp.where(lane == AUX_CUM0 + 1, c_mid,
             jnp.where(lane == AUX_CUM0 + 2, c_lo, 0.0)))))
        q_ref[:, sl] = qa.astype(bf16)

        kb = k_raw_t[sl, :]
        k_ms = jnp.sum(kb * kb, axis=0, keepdims=True) * (1.0 / HEAD_DIM)
        kn = kb * lax.rsqrt(k_ms + EPS) * gk_ref[...]
        t_hi, t_mid, t_lo = _split3(cum_t[h:h + 1, :])
        ka = jnp.where(sub < HEAD_DIM, kn,
             jnp.where(sub == AUX_ONE0, -t_hi,
             jnp.where(sub == AUX_ONE0 + 1, -t_mid,
             jnp.where(sub == AUX_ONE0 + 2, -t_lo,
             jnp.where(sub < AUX_CUM0 + 3, 1.0, 0.0)))))
        kt_ref[0, sl, :] = ka.astype(bf16)


def _proj(x2d, seq, kv_norm, b_norm, wkt, wv, wf, bfv, wq, wg, gq, gk):
    T = x2d.shape[0]
    n_tiles = T // ROW_TILE
    qw = N_HEADS * HEAD_PAD
    return pl.pallas_call(
        functools.partial(_proj_kernel, seq // ROW_TILE),
        grid=(n_tiles,),
        in_specs=[
            pl.BlockSpec((ROW_TILE, D_MODEL), lambda i: (i, 0)),
            _const_spec((1, D_MODEL)),
            _const_spec((1, D_MODEL)),
            _const_spec((qw, D_MODEL)),
            _const_spec((D_MODEL, D_MODEL)),
            _const_spec((D_MODEL, LANES)),
            _const_spec((1, LANES)),
            _const_spec((D_MODEL, qw)),
            _const_spec((D_MODEL, D_MODEL)),
            _const_spec((1, HEAD_PAD)),
            _const_spec((HEAD_PAD, 1)),
        ],
        out_specs=[
            pl.BlockSpec((ROW_TILE, qw), lambda i: (i, 0)),
            pl.BlockSpec((1, qw, ROW_TILE), lambda i: (i, 0, 0)),
            pl.BlockSpec((ROW_TILE, D_MODEL), lambda i: (i, 0)),
            pl.BlockSpec((ROW_TILE, D_MODEL), lambda i: (i, 0)),
        ],
        out_shape=[
            jax.ShapeDtypeStruct((T, qw), bf16),
            jax.ShapeDtypeStruct((n_tiles, qw, ROW_TILE), bf16),
            jax.ShapeDtypeStruct((T, D_MODEL), bf16),
            jax.ShapeDtypeStruct((T, D_MODEL), bf16),
        ],
        scratch_shapes=[pltpu.VMEM((1, LANES), f32)],
        compiler_params=_params(("arbitrary",)),
        name="proj",
    )(x2d, kv_norm, b_norm, wkt, wv, wf, bfv, wq, wg, gq, gk)


def _attn_kernel(q_ref, kt_ref, v_ref, o_ref, m_scr, l_scr, acc_scr):
    qi = pl.program_id(2)
    m_scr[...] = jnp.full(m_scr.shape, -jnp.inf, f32)
    l_scr[...] = jnp.zeros(l_scr.shape, f32)
    acc_scr[...] = jnp.zeros(acc_scr.shape, f32)

    def block(kb, masked):
        row0 = pl.multiple_of(kb * ATT_BLOCK, ATT_BLOCK)
        vb = v_ref[pl.ds(row0, ATT_BLOCK), :]
        for hh in range(2):
            sl = slice(hh * HEAD_PAD, (hh + 1) * HEAD_PAD)
            s = _dot(q_ref[:, sl], kt_ref[kb, sl, :])
            if masked:
                r = lax.broadcasted_iota(jnp.int32, s.shape, 0)
                c = lax.broadcasted_iota(jnp.int32, s.shape, 1)
                s = jnp.where(c <= r, s, -jnp.inf)
            m_prev = m_scr[hh]
            m_new = jnp.maximum(m_prev, jnp.max(s, axis=-1, keepdims=True))
            alpha = jnp.exp(m_prev - m_new)
            p = jnp.exp(s - m_new)
            l_scr[hh] = alpha * l_scr[hh] + jnp.sum(p, axis=-1, keepdims=True)
            acc_scr[hh] = alpha * acc_scr[hh] + _dot(p.astype(bf16), vb)
            m_scr[hh] = m_new

    def body(kb, c):
        block(kb, False)
        return c

    lax.fori_loop(0, qi, body, 0)
    block(qi, True)

    lane = lax.broadcasted_iota(jnp.int32, (ATT_BLOCK, 2 * HEAD_DIM), 1)
    o0 = acc_scr[0] / l_scr[0]
    o1 = acc_scr[1] / l_scr[1]
    o_ref[...] = jnp.where(lane < HEAD_DIM, o0, o1).astype(bf16)


def _attention(q, kt, v, batch, seq):
    T = q.shape[0]
    nq = seq // ATT_BLOCK
    pairs = N_HEADS // 2
    return pl.pallas_call(
        _attn_kernel,
        grid=(batch, pairs, nq),
        in_specs=[
            pl.BlockSpec((ATT_BLOCK, 2 * HEAD_PAD), lambda b, p, i: (b * nq + i, p)),
            pl.BlockSpec((nq, 2 * HEAD_PAD, ATT_BLOCK), lambda b, p, i: (b, p, 0)),
            pl.BlockSpec((seq, 2 * HEAD_DIM), lambda b, p, i: (b, p)),
        ],
        out_specs=pl.BlockSpec((ATT_BLOCK, 2 * HEAD_DIM), lambda b, p, i: (b * nq + i, p)),
        out_shape=jax.ShapeDtypeStruct((T, D_MODEL), bf16),
        scratch_shapes=[
            pltpu.VMEM((2, ATT_BLOCK, 1), f32),
            pltpu.VMEM((2, ATT_BLOCK, 1), f32),
            pltpu.VMEM((2, ATT_BLOCK, 2 * HEAD_DIM), f32),
        ],
        compiler_params=_params(("parallel", "parallel", "parallel")),
        name="attn",
    )(q, kt, v)


def _attn_out_kernel(x_ref, o_ref, sg_ref, wo_ref, mn_ref, wr_ref, x3_ref, lg_ref):
    gated = (o_ref[...].astype(f32) * sg_ref[...].astype(f32)).astype(bf16)
    x3 = x_ref[...] + _dot(gated, wo_ref[...])
    x3_ref[...] = x3
    lg_ref[...] = _router_logits_t(x3, mn_ref[...], wr_ref[...])


def _attn_out(x2d, o, sg, wo, m_norm, w_router):
    T = x2d.shape[0]
    return pl.pallas_call(
        _attn_out_kernel,
        grid=(T // ROW_TILE,),
        in_specs=[
            pl.BlockSpec((ROW_TILE, D_MODEL), lambda i: (i, 0)),
            pl.BlockSpec((ROW_TILE, D_MODEL), lambda i: (i, 0)),
            pl.BlockSpec((ROW_TILE, D_MODEL), lambda i: (i, 0)),
            _const_spec((D_MODEL, D_MODEL)),
            _const_spec((1, D_MODEL)),
            _const_spec((ROUTER_ROWS, D_MODEL)),
        ],
        out_specs=[
            pl.BlockSpec((ROW_TILE, D_MODEL), lambda i: (i, 0)),
            pl.BlockSpec((ROUTER_ROWS, ROW_TILE), lambda i: (0, i)),
        ],
        out_shape=[
            jax.ShapeDtypeStruct((T, D_MODEL), f32),
            jax.ShapeDtypeStruct((ROUTER_ROWS, T), f32),
        ],
        compiler_params=_params(("parallel",)),
        name="attn_out",
    )(x2d, o, sg, wo, m_norm, w_router)


def _router_weight(w_group, w_expert):
    pad = jnp.zeros((D_MODEL, ROUTER_ROWS - N_EXPERTS - N_GROUPS), f32)
    return jnp.concatenate([w_expert, w_group, pad], axis=1).T


def _pad_heads(w):
    w = w.reshape(D_MODEL, N_HEADS, HEAD_DIM)
    w = jnp.pad(w, ((0, 0), (0, 0), (0, HEAD_PAD - HEAD_DIM)))
    return w.reshape(D_MODEL, N_HEADS * HEAD_PAD)


def kernel(x, a_norm, a_w_in, a_b_in, a_v_norm, a_w_s, a_b_s, a_w_out, kv_norm, kv_w, kv_b_f,
           k_norm, b_norm, b_w_qg, q_norm, b_w_out, m_norm, m_w_group, m_b_group, m_w_expert,
           m_b_expert, m_w1, m_w3, m_w2):
    batch, seq, _ = x.shape
    T = batch * seq
    x2d = x.reshape(T, D_MODEL)

    x1, lg0 = _sgu(
        x2d, a_norm[0].reshape(1, -1), a_w_in[0].astype(bf16), a_b_in[0].reshape(1, -1),
        a_v_norm[0].reshape(1, -1), a_w_s[0], a_b_s[0].T, a_w_out[0].astype(bf16),
        m_norm[0].reshape(1, -1), _router_weight(m_w_group[0], m_w_expert[0]))
    x2 = _moe(x1, lg0, m_norm[0].reshape(1, -1), m_b_group[0], m_b_expert[0],
              m_w1[0], m_w3[0], m_w2[0])

    wk = kv_w[:, :D_MODEL]
    wv = kv_w[:, D_MODEL:2 * D_MODEL]
    wf = jnp.pad(kv_w[:, 2 * D_MODEL:], ((0, 0), (0, LANES - N_HEADS)))
    bfv = jnp.pad(kv_b_f, (0, LANES - N_HEADS)).reshape(1, LANES)
    wq = b_w_qg[0][:, :D_MODEL]
    wg = b_w_qg[0][:, D_MODEL:]
    gq = jnp.pad(q_norm[0], (0, HEAD_PAD - HEAD_DIM)).reshape(1, HEAD_PAD)
    gk = jnp.pad(k_norm, (0, HEAD_PAD - HEAD_DIM)).reshape(HEAD_PAD, 1)
    q, kt, v, sg = _proj(
        x2, seq, kv_norm.reshape(1, -1), b_norm[0].reshape(1, -1),
        _pad_heads(wk).T.astype(bf16), wv.astype(bf16), wf.astype(bf16), bfv,
        _pad_heads(wq).astype(bf16), wg.astype(bf16), gq, gk)
    o = _attention(q, kt, v, batch, seq)
    x3, lg1 = _attn_out(x2, o, sg, b_w_out[0].astype(bf16), m_norm[1].reshape(1, -1),
                        _router_weight(m_w_group[1], m_w_expert[1]))
    x4 = _moe(x3, lg1, m_norm[1].reshape(1, -1), m_b_group[1], m_b_expert[1],
              m_w1[1], m_w3[1], m_w2[1])
    return x4.reshape(batch, seq, D_MODEL)
```

```python
import functools
import math

import jax
import jax.numpy as jnp
from jax import lax
from jax.experimental import pallas as pl
from jax.experimental.pallas import tpu as pltpu

D_MODEL = 1024
EPS = 1e-6
SGU_BLOCK = 128
SGU_CHUNK = 64
CHUNK_SHIFT = SGU_CHUNK.bit_length() - 1
SGU_WIDTH = 2 * D_MODEL
SGU_GROUPS = 8
SGU_GROUP_DIM = SGU_WIDTH // SGU_GROUPS
N_HEADS = 16
HEAD_DIM = D_MODEL // N_HEADS
N_GROUPS = 4
EXPERTS_PER_GROUP = 8
N_EXPERTS = N_GROUPS * EXPERTS_PER_GROUP
GROUP_SHIFT = EXPERTS_PER_GROUP.bit_length() - 1
TOP_K = 2
D_EXPERT = D_MODEL // 2

LANES = 128
VMEM_LIMIT_BYTES = 56 * 1024 * 1024

ROW_TILE = 256
ROUTE_TILE = 512
EXPERT_ROWS = 256
MOVE_TILE = 256
ATT_Q = 512
ATT_K = ROW_TILE
HEAD_PAD = 2 * HEAD_DIM
ROUTER_ROWS = 40
AUX_ONE0 = HEAD_DIM
AUX_CUM0 = HEAD_DIM + 3

LOG2E = math.log2(math.e)

bf16 = jnp.bfloat16
f32 = jnp.float32


def _dot(a, b, precision=None):
    return jnp.dot(a, b, preferred_element_type=f32, precision=precision)


def _dot_nt(a, b, precision=None):
    return lax.dot_general(a, b, (((1,), (1,)), ((), ())), preferred_element_type=f32,
                           precision=precision)


def _rms(x, g):
    ms = jnp.mean(x * x, axis=-1, keepdims=True)
    return x * lax.rsqrt(ms + EPS) * g


def _gelu_tanh(z):
    c = math.sqrt(2.0 / math.pi)
    return z * (0.5 * (1.0 + jnp.tanh(c * (z + 0.044715 * (z * z * z)))))


def _split3(c):
    hi = c.astype(bf16).astype(f32)
    r = c - hi
    mid = r.astype(bf16).astype(f32)
    lo = r - mid
    return hi, mid, lo


def _const_spec(shape):
    nd = len(shape)
    return pl.BlockSpec(shape, lambda *_: (0,) * nd)


def _params(sem):
    return pltpu.CompilerParams(dimension_semantics=sem, vmem_limit_bytes=VMEM_LIMIT_BYTES)


def _router_logits_t(x_new, mn, wr):
    hm = _rms(x_new, mn)
    return _dot_nt(wr, hm, precision=lax.Precision.HIGHEST)


def _sgu_kernel(x_ref, an_ref, win_ref, bin_ref, vn_ref, ws_ref, bst_ref, wout_ref, mn_ref,
                wr_ref, x1_ref, lg_ref, u_scr, v_scr, gated_scr):
    x = x_ref[...]
    h = _rms(x, an_ref[...]).astype(bf16)
    u_scr[...] = _gelu_tanh(_dot(h, win_ref[:, :SGU_WIDTH]) + bin_ref[:, :SGU_WIDTH])
    v = _gelu_tanh(_dot(h, win_ref[:, SGU_WIDTH:]) + bin_ref[:, SGU_WIDTH:])
    v_scr[...] = _rms(v, vn_ref[...]).astype(bf16)

    t_chunk = lax.broadcasted_iota(jnp.int32, (SGU_BLOCK, SGU_BLOCK), 0) >> CHUNK_SHIFT
    s_chunk = lax.broadcasted_iota(jnp.int32, (SGU_BLOCK, SGU_BLOCK), 1) >> CHUNK_SHIFT
    causal = t_chunk >= s_chunk
    for g in range(SGU_GROUPS):
        wsm = jnp.where(causal, ws_ref[g], 0.0).astype(bf16)
        cols = slice(g * SGU_GROUP_DIM, (g + 1) * SGU_GROUP_DIM)
        for sb in range(ROW_TILE // SGU_BLOCK):
            rows = slice(sb * SGU_BLOCK, (sb + 1) * SGU_BLOCK)
            mixed = _dot(wsm, v_scr[rows, cols]) + bst_ref[:, g:g + 1]
            gated_scr[rows, cols] = (u_scr[rows, cols] * mixed).astype(bf16)

    x1 = x + _dot(gated_scr[...], wout_ref[...])
    x1_ref[...] = x1
    lg_ref[...] = _router_logits_t(x1, mn_ref[...], wr_ref[...])


def _sgu(x2d, a_norm, w_in, b_in, v_norm, w_s, b_s_t, w_out, m_norm, w_router):
    T = x2d.shape[0]
    return pl.pallas_call(
        _sgu_kernel,
        grid=(T // ROW_TILE,),
        in_specs=[
            pl.BlockSpec((ROW_TILE, D_MODEL), lambda i: (i, 0)),
            _const_spec((1, D_MODEL)),
            _const_spec((D_MODEL, 2 * SGU_WIDTH)),
            _const_spec((1, 2 * SGU_WIDTH)),
            _const_spec((1, SGU_WIDTH)),
            _const_spec((SGU_GROUPS, SGU_BLOCK, SGU_BLOCK)),
            _const_spec((SGU_BLOCK, SGU_GROUPS)),
            _const_spec((SGU_WIDTH, D_MODEL)),
            _const_spec((1, D_MODEL)),
            _const_spec((ROUTER_ROWS, D_MODEL)),
        ],
        out_specs=[
            pl.BlockSpec((ROW_TILE, D_MODEL), lambda i: (i, 0)),
            pl.BlockSpec((ROUTER_ROWS, ROW_TILE), lambda i: (0, i)),
        ],
        out_shape=[
            jax.ShapeDtypeStruct((T, D_MODEL), f32),
            jax.ShapeDtypeStruct((ROUTER_ROWS, T), f32),
        ],
        scratch_shapes=[
            pltpu.VMEM((ROW_TILE, SGU_WIDTH), f32),
            pltpu.VMEM((ROW_TILE, SGU_WIDTH), bf16),
            pltpu.VMEM((ROW_TILE, SGU_WIDTH), bf16),
        ],
        compiler_params=_params(("parallel",)),
        name="sgu",
    )(x2d, a_norm, w_in, b_in, v_norm, w_s, b_s_t, w_out, m_norm, w_router)


def _route_kernel(lg_ref, be_ref, bg_ref, oi_ref, ow_ref, cnt_ref, carry_scr):
    step = pl.program_id(0)

    @pl.when(step == 0)
    def _():
        carry_scr[...] = jnp.zeros_like(carry_scr)

    lg = lg_ref[...]
    e_l = lg[0:N_EXPERTS, :] + be_ref[...]
    g_l = lg[N_EXPERTS:N_EXPERTS + 8, :] + bg_ref[...]
    g_row = lax.broadcasted_iota(jnp.int32, g_l.shape, 0).astype(f32)
    g_l = jnp.where(g_row < N_GROUPS, g_l, -jnp.inf)
    g_max = jnp.max(g_l, axis=0, keepdims=True)
    g_sel = jnp.min(jnp.where(g_l == g_max, g_row, 8.0), axis=0, keepdims=True)
    g_den = jnp.sum(jnp.exp(g_l - g_max), axis=0, keepdims=True)
    g_w = 1.0 / g_den

    e_row_i = lax.broadcasted_iota(jnp.int32, e_l.shape, 0)
    e_row = e_row_i.astype(f32)
    e_grp = (e_row_i >> GROUP_SHIFT).astype(f32)
    e_in = jnp.where(e_grp == g_sel, e_l, -jnp.inf)
    m1 = jnp.max(e_in, axis=0, keepdims=True)
    i1 = jnp.min(jnp.where(e_in == m1, e_row, float(N_EXPERTS)), axis=0, keepdims=True)
    e_in2 = jnp.where(e_row == i1, -jnp.inf, e_in)
    m2 = jnp.max(e_in2, axis=0, keepdims=True)
    i2 = jnp.min(jnp.where(e_in2 == m2, e_row, float(N_EXPERTS)), axis=0, keepdims=True)
    t = jnp.exp(m2 - m1)
    p1 = 1.0 / (1.0 + t)
    p2 = t / (1.0 + t)

    hit1 = e_row == i1
    hit2 = e_row == i2
    member = jnp.logical_or(hit1, hit2)
    tt = lg.shape[1]
    before = (lax.broadcasted_iota(jnp.int32, (tt, tt), 0)
              < lax.broadcasted_iota(jnp.int32, (tt, tt), 1))
    prefix = _dot(member.astype(bf16), before.astype(bf16))
    rank_full = prefix + carry_scr[...]
    rank1 = jnp.sum(jnp.where(hit1, rank_full, 0.0), axis=0, keepdims=True)
    rank2 = jnp.sum(jnp.where(hit2, rank_full, 0.0), axis=0, keepdims=True)
    carry_new = carry_scr[...] + jnp.sum(member.astype(f32), axis=1, keepdims=True)
    carry_scr[...] = carry_new

    oi_ref[...] = jnp.zeros(oi_ref.shape, jnp.int32)
    oi_ref[0:1, :] = i1.astype(jnp.int32)
    oi_ref[1:2, :] = i2.astype(jnp.int32)
    oi_ref[2:3, :] = rank1.astype(jnp.int32)
    oi_ref[3:4, :] = rank2.astype(jnp.int32)
    ow_ref[...] = jnp.zeros(ow_ref.shape, f32)
    ow_ref[0:1, :] = g_w * p1
    ow_ref[1:2, :] = g_w * p2
    cnt_ref[...] = jnp.broadcast_to(carry_new, cnt_ref.shape)


def _route(logits_t, b_expert, b_group):
    T = logits_t.shape[1]
    return pl.pallas_call(
        _route_kernel,
        grid=(T // ROUTE_TILE,),
        in_specs=[
            pl.BlockSpec((ROUTER_ROWS, ROUTE_TILE), lambda i: (0, i)),
            _const_spec((N_EXPERTS, 1)),
            _const_spec((8, 1)),
        ],
        out_specs=[
            pl.BlockSpec((8, ROUTE_TILE), lambda i: (0, i)),
            pl.BlockSpec((8, ROUTE_TILE), lambda i: (0, i)),
            _const_spec((N_EXPERTS, LANES)),
        ],
        out_shape=[
            jax.ShapeDtypeStruct((8, T), jnp.int32),
            jax.ShapeDtypeStruct((8, T), f32),
            jax.ShapeDtypeStruct((N_EXPERTS, LANES), f32),
        ],
        scratch_shapes=[pltpu.VMEM((N_EXPERTS, 1), f32)],
        compiler_params=_params(("arbitrary",)),
        name="route",
    )(logits_t, b_expert, b_group)


def _row_copy(src, s, dst, d, sem):
    return pltpu.make_async_copy(src.at[pl.ds(s, 1)], dst.at[pl.ds(d, 1)], sem)


def _dispatch_kernel(n_blocks, pend_ref, padded_ref, nu_ref, dest_ref, x_ref, xs_hbm,
                     zero_scr, zsem, sem):
    @pl.when(pl.program_id(0) == 0)
    def _():
        zero_scr[...] = jnp.zeros_like(zero_scr)

        def fill(row0):
            row0 = pl.multiple_of(row0, EXPERT_ROWS)
            return pltpu.make_async_copy(zero_scr, xs_hbm.at[pl.ds(row0, EXPERT_ROWS)], zsem)

        for e in range(N_EXPERTS):
            @pl.when(padded_ref[e] > 0)
            def _():
                fill(pend_ref[e] - EXPERT_ROWS).start()

            @pl.when(nu_ref[0] + e < n_blocks)
            def _():
                fill((nu_ref[0] + e) * EXPERT_ROWS).start()

        for e in range(N_EXPERTS):
            @pl.when(padded_ref[e] > 0)
            def _():
                fill(0).wait()

            @pl.when(nu_ref[0] + e < n_blocks)
            def _():
                fill(0).wait()

    def start(j, c):
        for k in range(TOP_K):
            _row_copy(x_ref, j, xs_hbm, dest_ref[k, j], sem).start()
        return c

    lax.fori_loop(0, MOVE_TILE, start, 0)

    def wait(j, c):
        for k in range(TOP_K):
            _row_copy(x_ref, 0, xs_hbm, 0, sem).wait()
        return c

    lax.fori_loop(0, MOVE_TILE, wait, 0)


def _dispatch(pend, padded, n_used, dest_tiles, x2d, n_rows):
    T = x2d.shape[0]
    n_blocks = n_rows // EXPERT_ROWS
    grid_spec = pltpu.PrefetchScalarGridSpec(
        num_scalar_prefetch=3,
        grid=(T // MOVE_TILE,),
        in_specs=[
            pl.BlockSpec((None, TOP_K, MOVE_TILE), lambda i, *_: (i, 0, 0),
                         memory_space=pltpu.SMEM),
            pl.BlockSpec((MOVE_TILE, D_MODEL), lambda i, *_: (i, 0)),
        ],
        out_specs=pl.BlockSpec(memory_space=pl.ANY),
        scratch_shapes=[
            pltpu.VMEM((EXPERT_ROWS, D_MODEL), f32),
            pltpu.SemaphoreType.DMA(()),
            pltpu.SemaphoreType.DMA(()),
        ],
    )
    return pl.pallas_call(
        functools.partial(_dispatch_kernel, n_blocks),
        grid_spec=grid_spec,
        out_shape=jax.ShapeDtypeStruct((n_rows, D_MODEL), f32),
        compiler_params=_params(("arbitrary",)),
        name="dispatch",
    )(pend, padded, n_used, dest_tiles, x2d)


def _expert_kernel(be_ref, nu_ref, xs_ref, mn_ref, w1_ref, w3_ref, w2_ref, ys_ref,
                   w1_scr, w3_scr, w2_scr):
    b = pl.program_id(0)

    @pl.when(b < nu_ref[0])
    def _():
        new_expert = jnp.logical_or(b == 0, be_ref[b] != be_ref[jnp.maximum(b - 1, 0)])

        @pl.when(new_expert)
        def _():
            w1_scr[...] = w1_ref[...].astype(bf16)
            w3_scr[...] = w3_ref[...].astype(bf16)
            w2_scr[...] = w2_ref[...].astype(bf16)

        h = _rms(xs_ref[...], mn_ref[...]).astype(bf16)
        a = _dot(h, w1_scr[...])
        c = _dot(h, w3_scr[...])
        hid = (a * jax.nn.sigmoid(a)) * c
        ys_ref[...] = _dot(hid.astype(bf16), w2_scr[...])

    @pl.when(b >= nu_ref[0])
    def _():
        ys_ref[...] = jnp.zeros_like(ys_ref)


def _experts(layer, block_e, n_used, xs, m_norm, w1, w3, w2):
    n_rows = xs.shape[0]
    n_blocks = n_rows // EXPERT_ROWS

    def row_map(b, be, nu):
        return (jnp.minimum(b, nu[0] - 1), 0)

    def w_map(b, be, nu):
        return (layer, be[b], 0, 0)

    grid_spec = pltpu.PrefetchScalarGridSpec(
        num_scalar_prefetch=2,
        grid=(n_blocks,),
        in_specs=[
            pl.BlockSpec((EXPERT_ROWS, D_MODEL), row_map),
            pl.BlockSpec((1, D_MODEL), lambda b, be, nu: (0, 0)),
            pl.BlockSpec((None, None, D_MODEL, D_EXPERT), w_map),
            pl.BlockSpec((None, None, D_MODEL, D_EXPERT), w_map),
            pl.BlockSpec((None, None, D_EXPERT, D_MODEL), w_map),
        ],
        out_specs=pl.BlockSpec((EXPERT_ROWS, D_MODEL), lambda b, be, nu: (b, 0)),
        scratch_shapes=[
            pltpu.VMEM((D_MODEL, D_EXPERT), bf16),
            pltpu.VMEM((D_MODEL, D_EXPERT), bf16),
            pltpu.VMEM((D_EXPERT, D_MODEL), bf16),
        ],
    )
    return pl.pallas_call(
        _expert_kernel,
        grid_spec=grid_spec,
        out_shape=jax.ShapeDtypeStruct((n_rows, D_MODEL), f32),
        compiler_params=_params(("arbitrary",)),
        name="experts",
    )(block_e, n_used, xs, m_norm, w1, w3, w2)


def _combine_kernel(dest_ref, x_ref, w_ref, ys_hbm, out_ref, buf, sem):
    def start(j, c):
        for k in range(TOP_K):
            pltpu.make_async_copy(ys_hbm.at[pl.ds(dest_ref[k, j], 1)],
                                  buf.at[k, pl.ds(j, 1)], sem).start()
        return c

    lax.fori_loop(0, MOVE_TILE, start, 0)

    def wait(j, c):
        for k in range(TOP_K):
            pltpu.make_async_copy(ys_hbm.at[pl.ds(0, 1)], buf.at[k, pl.ds(0, 1)], sem).wait()
        return c

    lax.fori_loop(0, MOVE_TILE, wait, 0)
    w = w_ref[...]
    out_ref[...] = x_ref[...] + (w[:, 0:1] * buf[0] + w[:, 1:2] * buf[1])


def _combine(dest_tiles, x2d, w_cols, ys):
    T = x2d.shape[0]
    return pl.pallas_call(
        _combine_kernel,
        grid=(T // MOVE_TILE,),
        in_specs=[
            pl.BlockSpec((None, TOP_K, MOVE_TILE), lambda i: (i, 0, 0),
                         memory_space=pltpu.SMEM),
            pl.BlockSpec((MOVE_TILE, D_MODEL), lambda i: (i, 0)),
            pl.BlockSpec((MOVE_TILE, 8), lambda i: (i, 0)),
            pl.BlockSpec(memory_space=pl.ANY),
        ],
        out_specs=pl.BlockSpec((MOVE_TILE, D_MODEL), lambda i: (i, 0)),
        out_shape=jax.ShapeDtypeStruct((T, D_MODEL), f32),
        scratch_shapes=[
            pltpu.VMEM((TOP_K, MOVE_TILE, D_MODEL), f32),
            pltpu.SemaphoreType.DMA(()),
        ],
        compiler_params=_params(("arbitrary",)),
        name="combine",
    )(dest_tiles, x2d, w_cols, ys)


def _moe(layer, x2d, logits_t, m_norm, b_group, b_expert, w1, w3, w2):
    T = x2d.shape[0]
    n_blocks = (T * TOP_K) // EXPERT_ROWS + N_EXPERTS
    n_rows = n_blocks * EXPERT_ROWS

    be = b_expert.reshape(N_EXPERTS, 1)
    bg = jnp.concatenate([b_group, jnp.zeros((8 - N_GROUPS,), f32)]).reshape(8, 1)
    oi, ow, cnt = _route(logits_t, be, bg)

    counts = cnt[:, 0].astype(jnp.int32)
    padded = (counts + EXPERT_ROWS - 1) // EXPERT_ROWS * EXPERT_ROWS
    pend = jnp.cumsum(padded)
    pstart = pend - padded
    n_used = (pend[-1] // EXPERT_ROWS).astype(jnp.int32).reshape(1)
    block_start = jnp.arange(n_blocks, dtype=jnp.int32) * EXPERT_ROWS
    block_e = jnp.minimum(jnp.sum(block_start[:, None] >= pend[None, :], axis=1),
                          N_EXPERTS - 1).astype(jnp.int32)
    e_ids = jnp.arange(N_EXPERTS, dtype=jnp.int32)[:, None, None]
    seg_start = jnp.sum(jnp.where(oi[None, 0:2] == e_ids, pstart[:, None, None], 0), axis=0)
    dest = seg_start + oi[2:4]
    dest_tiles = dest.reshape(TOP_K, T // MOVE_TILE, MOVE_TILE).transpose(1, 0, 2)
    w_cols = ow.T

    xs = _dispatch(pend.astype(jnp.int32), padded, n_used, dest_tiles, x2d, n_rows)
    ys = _experts(layer, block_e, n_used, xs, m_norm, w1, w3, w2)
    return _combine(dest_tiles, x2d, w_cols, ys)


def _log_sigmoid(x):
    return jnp.minimum(x, 0.0) - jnp.log1p(jnp.exp(-jnp.abs(x)))


def _proj_kernel(tiles_per_seq, x_ref, kvn_ref, bn_ref, wkt_ref, wv_ref, wf_ref, bf_ref,
                 wq_ref, wg_ref, gq_ref, gk_ref, q_ref, kt_ref, v_ref, sg_ref, carry_scr):
    i = pl.program_id(0)
    x = x_ref[...]
    hkv = _rms(x, kvn_ref[...]).astype(bf16)
    hq = _rms(x, bn_ref[...]).astype(bf16)

    v_ref[...] = _dot(hkv, wv_ref[...]).astype(bf16)
    sg_ref[...] = jax.nn.sigmoid(_dot(hq, wg_ref[...])).astype(bf16)

    logf = _log_sigmoid(_dot(hkv, wf_ref[...]) + bf_ref[...])
    tm = x.shape[0]
    incl = (lax.broadcasted_iota(jnp.int32, (tm, tm), 0)
            >= lax.broadcasted_iota(jnp.int32, (tm, tm), 1)).astype(f32)
    @pl.when(i % tiles_per_seq == 0)
    def _():
        carry_scr[...] = jnp.zeros_like(carry_scr)

    cum = _dot(incl, logf, precision=lax.Precision.HIGHEST) + carry_scr[...]
    carry_scr[...] = cum[tm - 1:tm, :]
    cum = cum * LOG2E
    cum_t = cum.T

    lane = lax.broadcasted_iota(jnp.int32, (tm, HEAD_PAD), 1)
    sub = lax.broadcasted_iota(jnp.int32, (HEAD_PAD, tm), 0)
    q_raw = _dot(hq, wq_ref[...])
    k_raw_t = _dot_nt(wkt_ref[...], hkv)
    scale = HEAD_DIM ** -0.5 * LOG2E
    for h in range(N_HEADS):
        sl = slice(h * HEAD_PAD, (h + 1) * HEAD_PAD)
        qb = q_raw[:, sl]
        q_ms = jnp.sum(qb * qb, axis=-1, keepdims=True) * (1.0 / HEAD_DIM)
        qn = qb * lax.rsqrt(q_ms + EPS) * gq_ref[...] * scale
        c_hi, c_mid, c_lo = _split3(cum[:, h:h + 1])
        qa = jnp.where(lane < HEAD_DIM, qn,
             jnp.where(lane < AUX_CUM0, 1.0,
             jnp.where(lane == AUX_CUM0, c_hi,
             jnp.where(lane == AUX_CUM0 + 1, c_mid,
             jnp.where(lane == AUX_CUM0 + 2, c_lo, 0.0)))))
        q_ref[:, sl] = qa.astype(bf16)

        kb = k_raw_t[sl, :]
        k_ms = jnp.sum(kb * kb, axis=0, keepdims=True) * (1.0 / HEAD_DIM)
        kn = kb * lax.rsqrt(k_ms + EPS) * gk_ref[...]
        t_hi, t_mid, t_lo = _split3(cum_t[h:h + 1, :])
        ka = jnp.where(sub < HEAD_DIM, kn,
             jnp.where(sub == AUX_ONE0, -t_hi,
             jnp.where(sub == AUX_ONE0 + 1, -t_mid,
             jnp.where(sub == AUX_ONE0 + 2, -t_lo,
             jnp.where(sub < AUX_CUM0 + 3, 1.0, 0.0)))))
        kt_ref[0, sl, :] = ka.astype(bf16)


def _proj(x2d, seq, kv_norm, b_norm, wkt, wv, wf, bfv, wq, wg, gq, gk):
    T = x2d.shape[0]
    n_tiles = T // ROW_TILE
    qw = N_HEADS * HEAD_PAD
    return pl.pallas_call(
        functools.partial(_proj_kernel, seq // ROW_TILE),
        grid=(n_tiles,),
        in_specs=[
            pl.BlockSpec((ROW_TILE, D_MODEL), lambda i: (i, 0)),
            _const_spec((1, D_MODEL)),
            _const_spec((1, D_MODEL)),
            _const_spec((qw, D_MODEL)),
            _const_spec((D_MODEL, D_MODEL)),
            _const_spec((D_MODEL, LANES)),
            _const_spec((1, LANES)),
            _const_spec((D_MODEL, qw)),
            _const_spec((D_MODEL, D_MODEL)),
            _const_spec((1, HEAD_PAD)),
            _const_spec((HEAD_PAD, 1)),
        ],
        out_specs=[
            pl.BlockSpec((ROW_TILE, qw), lambda i: (i, 0)),
            pl.BlockSpec((1, qw, ROW_TILE), lambda i: (i, 0, 0)),
            pl.BlockSpec((ROW_TILE, D_MODEL), lambda i: (i, 0)),
            pl.BlockSpec((ROW_TILE, D_MODEL), lambda i: (i, 0)),
        ],
        out_shape=[
            jax.ShapeDtypeStruct((T, qw), bf16),
            jax.ShapeDtypeStruct((n_tiles, qw, ROW_TILE), bf16),
            jax.ShapeDtypeStruct((T, D_MODEL), bf16),
            jax.ShapeDtypeStruct((T, D_MODEL), bf16),
        ],
        scratch_shapes=[pltpu.VMEM((1, LANES), f32)],
        compiler_params=_params(("arbitrary",)),
        name="proj",
    )(x2d, kv_norm, b_norm, wkt, wv, wf, bfv, wq, wg, gq, gk)


def _attn_kernel(q_ref, kt_ref, v_ref, o_ref, s_scr, m_scr, l_scr, acc_scr):
    qi = pl.program_id(2)
    tiles = ATT_Q // ATT_K
    groups = ATT_K // LANES
    m_scr[...] = jnp.full(m_scr.shape, -jnp.inf, f32)
    col_minus_row = (lax.broadcasted_iota(jnp.int32, (ATT_Q, ATT_K), 1)
                     - lax.broadcasted_iota(jnp.int32, (ATT_Q, ATT_K), 0))

    def sweep_a(trip, masked):
        for t in range(tiles):
            kt_idx = trip * tiles + t
            for hh in range(2):
                sl = slice(hh * HEAD_PAD, (hh + 1) * HEAD_PAD)
                s = _dot(q_ref[:, sl], kt_ref[kt_idx, sl, :])
                if masked:
                    s = jnp.where(col_minus_row <= -t * ATT_K, s, -jnp.inf)
                s_scr[hh, kt_idx] = s
                mx = m_scr[hh]
                for g in range(groups):
                    mx = jnp.maximum(mx, s[:, g * LANES:(g + 1) * LANES])
                m_scr[hh] = mx

    def body_a(trip, c):
        sweep_a(trip, False)
        return c

    lax.fori_loop(0, qi, body_a, 0)
    sweep_a(qi, True)

    for hh in range(2):
        row_max = jnp.max(m_scr[hh], axis=-1, keepdims=True)
        m_scr[hh] = jnp.broadcast_to(row_max, (ATT_Q, LANES))
    l_scr[...] = jnp.zeros(l_scr.shape, f32)
    acc_scr[...] = jnp.zeros(acc_scr.shape, f32)

    def body_b(trip, c):
        for t in range(tiles):
            kt_idx = trip * tiles + t
            row0 = pl.multiple_of(kt_idx * ATT_K, ATT_K)
            vb = v_ref[pl.ds(row0, ATT_K), :]
            for hh in range(2):
                mb = m_scr[hh]
                s = s_scr[hh, kt_idx]
                ps = [jnp.exp2(s[:, g * LANES:(g + 1) * LANES] - mb) for g in range(groups)]
                l_scr[hh] += functools.reduce(lambda a, b: a + b, ps)
                p = jnp.concatenate(ps, axis=1).astype(bf16)
                acc_scr[hh] += _dot(p, vb)
        return c

    lax.fori_loop(0, qi + 1, body_b, 0)

    lane = lax.broadcasted_iota(jnp.int32, (ATT_Q, 2 * HEAD_DIM), 1)
    o0 = acc_scr[0] / jnp.sum(l_scr[0], axis=-1, keepdims=True)
    o1 = acc_scr[1] / jnp.sum(l_scr[1], axis=-1, keepdims=True)
    o_ref[...] = jnp.where(lane < HEAD_DIM, o0, o1).astype(bf16)


def _attention(q, kt, v, batch, seq):
    T = q.shape[0]
    nq = seq // ATT_Q
    nk = seq // ATT_K
    pairs = N_HEADS // 2
    return pl.pallas_call(
        _attn_kernel,
        grid=(batch, pairs, nq),
        in_specs=[
            pl.BlockSpec((ATT_Q, 2 * HEAD_PAD), lambda b, p, i: (b * nq + i, p)),
            pl.BlockSpec((nk, 2 * HEAD_PAD, ATT_K), lambda b, p, i: (b, p, 0)),
            pl.BlockSpec((seq, 2 * HEAD_DIM), lambda b, p, i: (b, p)),
        ],
        out_specs=pl.BlockSpec((ATT_Q, 2 * HEAD_DIM), lambda b, p, i: (b * nq + i, p)),
        out_shape=jax.ShapeDtypeStruct((T, D_MODEL), bf16),
        scratch_shapes=[
            pltpu.VMEM((2, nk, ATT_Q, ATT_K), f32),
            pltpu.VMEM((2, ATT_Q, LANES), f32),
            pltpu.VMEM((2, ATT_Q, LANES), f32),
            pltpu.VMEM((2, ATT_Q, 2 * HEAD_DIM), f32),
        ],
        compiler_params=_params(("parallel", "parallel", "parallel")),
        name="attn",
    )(q, kt, v)


def _attn_out_kernel(x_ref, o_ref, sg_ref, wo_ref, mn_ref, wr_ref, x3_ref, lg_ref):
    gated = (o_ref[...].astype(f32) * sg_ref[...].astype(f32)).astype(bf16)
    x3 = x_ref[...] + _dot(gated, wo_ref[...])
    x3_ref[...] = x3
    lg_ref[...] = _router_logits_t(x3, mn_ref[...], wr_ref[...])


def _attn_out(x2d, o, sg, wo, m_norm, w_router):
    T = x2d.shape[0]
    return pl.pallas_call(
        _attn_out_kernel,
        grid=(T // ROW_TILE,),
        in_specs=[
            pl.BlockSpec((ROW_TILE, D_MODEL), lambda i: (i, 0)),
            pl.BlockSpec((ROW_TILE, D_MODEL), lambda i: (i, 0)),
            pl.BlockSpec((ROW_TILE, D_MODEL), lambda i: (i, 0)),
            _const_spec((D_MODEL, D_MODEL)),
            _const_spec((1, D_MODEL)),
            _const_spec((ROUTER_ROWS, D_MODEL)),
        ],
        out_specs=[
            pl.BlockSpec((ROW_TILE, D_MODEL), lambda i: (i, 0)),
            pl.BlockSpec((ROUTER_ROWS, ROW_TILE), lambda i: (0, i)),
        ],
        out_shape=[
            jax.ShapeDtypeStruct((T, D_MODEL), f32),
            jax.ShapeDtypeStruct((ROUTER_ROWS, T), f32),
        ],
        compiler_params=_params(("parallel",)),
        name="attn_out",
    )(x2d, o, sg, wo, m_norm, w_router)


def _router_weight(w_group, w_expert):
    pad = jnp.zeros((D_MODEL, ROUTER_ROWS - N_EXPERTS - N_GROUPS), f32)
    return jnp.concatenate([w_expert, w_group, pad], axis=1).T


def _pad_heads(w):
    w = w.reshape(D_MODEL, N_HEADS, HEAD_DIM)
    w = jnp.pad(w, ((0, 0), (0, 0), (0, HEAD_PAD - HEAD_DIM)))
    return w.reshape(D_MODEL, N_HEADS * HEAD_PAD)


def kernel(x, a_norm, a_w_in, a_b_in, a_v_norm, a_w_s, a_b_s, a_w_out, kv_norm, kv_w, kv_b_f,
           k_norm, b_norm, b_w_qg, q_norm, b_w_out, m_norm, m_w_group, m_b_group, m_w_expert,
           m_b_expert, m_w1, m_w3, m_w2):
    batch, seq, _ = x.shape
    T = batch * seq
    x2d = x.reshape(T, D_MODEL)

    x1, lg0 = _sgu(
        x2d, a_norm[0].reshape(1, -1), a_w_in[0].astype(bf16), a_b_in[0].reshape(1, -1),
        a_v_norm[0].reshape(1, -1), a_w_s[0], a_b_s[0].T, a_w_out[0].astype(bf16),
        m_norm[0].reshape(1, -1), _router_weight(m_w_group[0], m_w_expert[0]))
    x2 = _moe(0, x1, lg0, m_norm[0].reshape(1, -1), m_b_group[0], m_b_expert[0],
              m_w1, m_w3, m_w2)

    wk = kv_w[:, :D_MODEL]
    wv = kv_w[:, D_MODEL:2 * D_MODEL]
    wf = jnp.pad(kv_w[:, 2 * D_MODEL:], ((0, 0), (0, LANES - N_HEADS)))
    bfv = jnp.pad(kv_b_f, (0, LANES - N_HEADS)).reshape(1, LANES)
    wq = b_w_qg[0][:, :D_MODEL]
    wg = b_w_qg[0][:, D_MODEL:]
    gq = jnp.pad(q_norm[0], (0, HEAD_PAD - HEAD_DIM)).reshape(1, HEAD_PAD)
    gk = jnp.pad(k_norm, (0, HEAD_PAD - HEAD_DIM)).reshape(HEAD_PAD, 1)
    q, kt, v, sg = _proj(
        x2, seq, kv_norm.reshape(1, -1), b_norm[0].reshape(1, -1),
        _pad_heads(wk).T.astype(bf16), wv.astype(bf16), wf.astype(bf16), bfv,
        _pad_heads(wq).astype(bf16), wg.astype(bf16), gq, gk)
    o = _attention(q, kt, v, batch, seq)
    x3, lg1 = _attn_out(x2, o, sg, b_w_out[0].astype(bf16), m_norm[1].reshape(1, -1),
                        _router_weight(m_w_group[1], m_w_expert[1]))
    x4 = _moe(1, x3, lg1, m_norm[1].reshape(1, -1), m_b_group[1], m_b_expert[1],
              m_w1, m_w3, m_w2)
    return x4.reshape(batch, seq, D_MODEL)
```

```python
import functools
import math

import jax
import jax.numpy as jnp
from jax import lax
from jax.experimental import pallas as pl
from jax.experimental.pallas import tpu as pltpu

D_MODEL = 1024
EPS = 1e-6
SGU_BLOCK = 128
SGU_CHUNK = 64
CHUNK_SHIFT = SGU_CHUNK.bit_length() - 1
SGU_WIDTH = 2 * D_MODEL
SGU_GROUPS = 8
SGU_GROUP_DIM = SGU_WIDTH // SGU_GROUPS
N_HEADS = 16
HEAD_DIM = D_MODEL // N_HEADS
N_GROUPS = 4
EXPERTS_PER_GROUP = 8
N_EXPERTS = N_GROUPS * EXPERTS_PER_GROUP
GROUP_SHIFT = EXPERTS_PER_GROUP.bit_length() - 1
TOP_K = 2
D_EXPERT = D_MODEL // 2

LANES = 128
ROW_SUBLANES = D_MODEL // LANES
ISSUE_UNROLL = 8
VMEM_LIMIT_BYTES = 56 * 1024 * 1024

ROW_TILE = 256
ROUTE_TILE = 512
EXPERT_ROWS = 256
MOVE_TILE = 256
ATT_Q = 512
ATT_K = ROW_TILE
HEAD_PAD = 2 * HEAD_DIM
ROUTER_ROWS = 40
AUX_ONE0 = HEAD_DIM
AUX_CUM0 = HEAD_DIM + 3

LOG2E = math.log2(math.e)

bf16 = jnp.bfloat16
f32 = jnp.float32


def _dot(a, b, precision=None):
    return jnp.dot(a, b, preferred_element_type=f32, precision=precision)


def _dot_nt(a, b, precision=None):
    return lax.dot_general(a, b, (((1,), (1,)), ((), ())), preferred_element_type=f32,
                           precision=precision)


def _rms(x, g):
    ms = jnp.mean(x * x, axis=-1, keepdims=True)
    return x * lax.rsqrt(ms + EPS) * g


def _gelu_tanh(z):
    c = math.sqrt(2.0 / math.pi)
    return z * (0.5 * (1.0 + jnp.tanh(c * (z + 0.044715 * (z * z * z)))))


def _split3(c):
    hi = c.astype(bf16).astype(f32)
    r = c - hi
    mid = r.astype(bf16).astype(f32)
    lo = r - mid
    return hi, mid, lo


def _const_spec(shape):
    nd = len(shape)
    return pl.BlockSpec(shape, lambda *_: (0,) * nd)


def _params(sem):
    return pltpu.CompilerParams(dimension_semantics=sem, vmem_limit_bytes=VMEM_LIMIT_BYTES)


def _router_logits_t(x_new, mn, wr3):
    hm = _rms(x_new, mn)
    h_hi = hm.astype(bf16)
    h_lo = (hm - h_hi.astype(f32)).astype(bf16)
    lg = _dot(jnp.concatenate([h_hi, h_hi, h_lo], axis=1), wr3)
    return lg.T[:ROUTER_ROWS, :]


def _sgu_kernel(x_ref, an_ref, win_ref, bin_ref, vn_ref, ws_ref, bst_ref, wout_ref, mn_ref,
                wr_ref, x1_ref, lg_ref, u_scr, v_scr, gated_scr):
    x = x_ref[...]
    h = _rms(x, an_ref[...]).astype(bf16)
    u_scr[...] = _gelu_tanh(_dot(h, win_ref[:, :SGU_WIDTH]) + bin_ref[:, :SGU_WIDTH])
    v = _gelu_tanh(_dot(h, win_ref[:, SGU_WIDTH:]) + bin_ref[:, SGU_WIDTH:])
    v_scr[...] = _rms(v, vn_ref[...]).astype(bf16)

    t_chunk = lax.broadcasted_iota(jnp.int32, (SGU_BLOCK, SGU_BLOCK), 0) >> CHUNK_SHIFT
    s_chunk = lax.broadcasted_iota(jnp.int32, (SGU_BLOCK, SGU_BLOCK), 1) >> CHUNK_SHIFT
    causal = t_chunk >= s_chunk
    for g in range(SGU_GROUPS):
        wsm = jnp.where(causal, ws_ref[g], 0.0).astype(bf16)
        cols = slice(g * SGU_GROUP_DIM, (g + 1) * SGU_GROUP_DIM)
        for sb in range(ROW_TILE // SGU_BLOCK):
            rows = slice(sb * SGU_BLOCK, (sb + 1) * SGU_BLOCK)
            mixed = _dot(wsm, v_scr[rows, cols]) + bst_ref[:, g:g + 1]
            gated_scr[rows, cols] = (u_scr[rows, cols] * mixed).astype(bf16)

    x1 = x + _dot(gated_scr[...], wout_ref[...])
    x1_ref[...] = x1
    lg_ref[...] = _router_logits_t(x1, mn_ref[...], wr_ref[...])


def _sgu(x2d, a_norm, w_in, b_in, v_norm, w_s, b_s_t, w_out, m_norm, w_router):
    T = x2d.shape[0]
    return pl.pallas_call(
        _sgu_kernel,
        grid=(T // ROW_TILE,),
        in_specs=[
            pl.BlockSpec((ROW_TILE, D_MODEL), lambda i: (i, 0)),
            _const_spec((1, D_MODEL)),
            _const_spec((D_MODEL, 2 * SGU_WIDTH)),
            _const_spec((1, 2 * SGU_WIDTH)),
            _const_spec((1, SGU_WIDTH)),
            _const_spec((SGU_GROUPS, SGU_BLOCK, SGU_BLOCK)),
            _const_spec((SGU_BLOCK, SGU_GROUPS)),
            _const_spec((SGU_WIDTH, D_MODEL)),
            _const_spec((1, D_MODEL)),
            _const_spec((3 * D_MODEL, LANES)),
        ],
        out_specs=[
            pl.BlockSpec((ROW_TILE, D_MODEL), lambda i: (i, 0)),
            pl.BlockSpec((ROUTER_ROWS, ROW_TILE), lambda i: (0, i)),
        ],
        out_shape=[
            jax.ShapeDtypeStruct((T, D_MODEL), f32),
            jax.ShapeDtypeStruct((ROUTER_ROWS, T), f32),
        ],
        scratch_shapes=[
            pltpu.VMEM((ROW_TILE, SGU_WIDTH), f32),
            pltpu.VMEM((ROW_TILE, SGU_WIDTH), bf16),
            pltpu.VMEM((ROW_TILE, SGU_WIDTH), bf16),
        ],
        compiler_params=_params(("parallel",)),
        name="sgu",
    )(x2d, a_norm, w_in, b_in, v_norm, w_s, b_s_t, w_out, m_norm, w_router)


def _route_kernel(lg_ref, be_ref, bg_ref, oi_ref, ow_ref, cnt_ref, carry_scr):
    step = pl.program_id(0)

    @pl.when(step == 0)
    def _():
        carry_scr[...] = jnp.zeros_like(carry_scr)

    lg = lg_ref[...]
    e_l = lg[0:N_EXPERTS, :] + be_ref[...]
    g_l = lg[N_EXPERTS:N_EXPERTS + 8, :] + bg_ref[...]
    g_row = lax.broadcasted_iota(jnp.int32, g_l.shape, 0).astype(f32)
    g_l = jnp.where(g_row < N_GROUPS, g_l, -jnp.inf)
    g_max = jnp.max(g_l, axis=0, keepdims=True)
    g_sel = jnp.min(jnp.where(g_l == g_max, g_row, 8.0), axis=0, keepdims=True)
    g_den = jnp.sum(jnp.exp(g_l - g_max), axis=0, keepdims=True)
    g_w = 1.0 / g_den

    e_row_i = lax.broadcasted_iota(jnp.int32, e_l.shape, 0)
    e_row = e_row_i.astype(f32)
    e_grp = (e_row_i >> GROUP_SHIFT).astype(f32)
    e_in = jnp.where(e_grp == g_sel, e_l, -jnp.inf)
    m1 = jnp.max(e_in, axis=0, keepdims=True)
    i1 = jnp.min(jnp.where(e_in == m1, e_row, float(N_EXPERTS)), axis=0, keepdims=True)
    e_in2 = jnp.where(e_row == i1, -jnp.inf, e_in)
    m2 = jnp.max(e_in2, axis=0, keepdims=True)
    i2 = jnp.min(jnp.where(e_in2 == m2, e_row, float(N_EXPERTS)), axis=0, keepdims=True)
    t = jnp.exp(m2 - m1)
    p1 = 1.0 / (1.0 + t)
    p2 = t / (1.0 + t)

    hit1 = e_row == i1
    hit2 = e_row == i2
    member = jnp.logical_or(hit1, hit2)
    tt = lg.shape[1]
    before = (lax.broadcasted_iota(jnp.int32, (tt, tt), 0)
              < lax.broadcasted_iota(jnp.int32, (tt, tt), 1))
    prefix = _dot(member.astype(bf16), before.astype(bf16))
    rank_full = prefix + carry_scr[...]
    rank1 = jnp.sum(jnp.where(hit1, rank_full, 0.0), axis=0, keepdims=True)
    rank2 = jnp.sum(jnp.where(hit2, rank_full, 0.0), axis=0, keepdims=True)
    carry_new = carry_scr[...] + jnp.sum(member.astype(f32), axis=1, keepdims=True)
    carry_scr[...] = carry_new

    oi_ref[...] = jnp.zeros(oi_ref.shape, jnp.int32)
    oi_ref[0:1, :] = i1.astype(jnp.int32)
    oi_ref[1:2, :] = i2.astype(jnp.int32)
    oi_ref[2:3, :] = rank1.astype(jnp.int32)
    oi_ref[3:4, :] = rank2.astype(jnp.int32)
    ow_ref[...] = jnp.zeros(ow_ref.shape, f32)
    ow_ref[0:1, :] = g_w * p1
    ow_ref[1:2, :] = g_w * p2
    cnt_ref[...] = jnp.broadcast_to(carry_new, cnt_ref.shape)


def _route(logits_t, b_expert, b_group):
    T = logits_t.shape[1]
    return pl.pallas_call(
        _route_kernel,
        grid=(T // ROUTE_TILE,),
        in_specs=[
            pl.BlockSpec((ROUTER_ROWS, ROUTE_TILE), lambda i: (0, i)),
            _const_spec((N_EXPERTS, 1)),
            _const_spec((8, 1)),
        ],
        out_specs=[
            pl.BlockSpec((8, ROUTE_TILE), lambda i: (0, i)),
            pl.BlockSpec((8, ROUTE_TILE), lambda i: (0, i)),
            _const_spec((N_EXPERTS, LANES)),
        ],
        out_shape=[
            jax.ShapeDtypeStruct((8, T), jnp.int32),
            jax.ShapeDtypeStruct((8, T), f32),
            jax.ShapeDtypeStruct((N_EXPERTS, LANES), f32),
        ],
        scratch_shapes=[pltpu.VMEM((N_EXPERTS, 1), f32)],
        compiler_params=_params(("arbitrary",)),
        name="route",
    )(logits_t, b_expert, b_group)


def _row_copy(src, s, dst, d, sem):
    return pltpu.make_async_copy(src.at[s], dst.at[d], sem)


def _wait_rows(buf, sem):
    pltpu.make_async_copy(buf, buf, sem).wait()


def _dispatch_kernel(n_blocks, pend_ref, padded_ref, nu_ref, dest_ref, x_ref, xs_hbm,
                     zero_scr, x3_scr, zsem, sem):
    @pl.when(pl.program_id(0) == 0)
    def _():
        zero_scr[...] = jnp.zeros_like(zero_scr)

        def fill(row0):
            return pltpu.make_async_copy(zero_scr, xs_hbm.at[pl.ds(row0, EXPERT_ROWS)], zsem)

        for e in range(N_EXPERTS):
            @pl.when(padded_ref[e] > 0)
            def _():
                fill(pend_ref[e] - EXPERT_ROWS).start()

            @pl.when(nu_ref[0] + e < n_blocks)
            def _():
                fill((nu_ref[0] + e) * EXPERT_ROWS).start()

        for e in range(N_EXPERTS):
            @pl.when(padded_ref[e] > 0)
            def _():
                fill(0).wait()

            @pl.when(nu_ref[0] + e < n_blocks)
            def _():
                fill(0).wait()

    x3_scr[...] = x_ref[...].reshape(x3_scr.shape)

    def start(j, c):
        for k in range(TOP_K):
            _row_copy(x3_scr, j, xs_hbm, dest_ref[k, j], sem).start(priority=k)
        return c

    lax.fori_loop(0, MOVE_TILE, start, 0, unroll=ISSUE_UNROLL)
    for k in range(TOP_K):
        _wait_rows(x3_scr, sem)


def _dispatch(pend, padded, n_used, dest_tiles, x2d, n_rows):
    T = x2d.shape[0]
    n_blocks = n_rows // EXPERT_ROWS
    grid_spec = pltpu.PrefetchScalarGridSpec(
        num_scalar_prefetch=3,
        grid=(T // MOVE_TILE,),
        in_specs=[
            pl.BlockSpec((None, TOP_K, MOVE_TILE), lambda i, *_: (i, 0, 0),
                         memory_space=pltpu.SMEM),
            pl.BlockSpec((MOVE_TILE, D_MODEL), lambda i, *_: (i, 0)),
        ],
        out_specs=pl.BlockSpec(memory_space=pl.ANY),
        scratch_shapes=[
            pltpu.VMEM((EXPERT_ROWS, ROW_SUBLANES, LANES), f32),
            pltpu.VMEM((MOVE_TILE, ROW_SUBLANES, LANES), f32),
            pltpu.SemaphoreType.DMA(()),
            pltpu.SemaphoreType.DMA(()),
        ],
    )
    return pl.pallas_call(
        functools.partial(_dispatch_kernel, n_blocks),
        grid_spec=grid_spec,
        out_shape=jax.ShapeDtypeStruct((n_rows, ROW_SUBLANES, LANES), f32),
        compiler_params=_params(("arbitrary",)),
        name="dispatch",
    )(pend, padded, n_used, dest_tiles, x2d)


def _expert_kernel(be_ref, nu_ref, xs_ref, mn_ref, w1_ref, w3_ref, w2_ref, ys_ref,
                   w1_scr, w3_scr, w2_scr):
    b = pl.program_id(0)

    @pl.when(b < nu_ref[0])
    def _():
        new_expert = jnp.logical_or(b == 0, be_ref[b] != be_ref[jnp.maximum(b - 1, 0)])

        @pl.when(new_expert)
        def _():
            w1_scr[...] = w1_ref[...].astype(bf16)
            w3_scr[...] = w3_ref[...].astype(bf16)
            w2_scr[...] = w2_ref[...].astype(bf16)

        x = xs_ref[...].reshape(EXPERT_ROWS, D_MODEL)
        h = _rms(x, mn_ref[...]).astype(bf16)
        a = _dot(h, w1_scr[...])
        c = _dot(h, w3_scr[...])
        hid = (a * jax.nn.sigmoid(a)) * c
        ys_ref[...] = _dot(hid.astype(bf16), w2_scr[...]).reshape(ys_ref.shape)

    @pl.when(b >= nu_ref[0])
    def _():
        ys_ref[...] = jnp.zeros_like(ys_ref)


def _experts(layer, block_e, n_used, xs, m_norm, w1, w3, w2):
    n_rows = xs.shape[0]
    n_blocks = n_rows // EXPERT_ROWS

    def row_map(b, be, nu):
        return (jnp.minimum(b, nu[0] - 1), 0, 0)

    def w_map(b, be, nu):
        return (layer, be[b], 0, 0)

    grid_spec = pltpu.PrefetchScalarGridSpec(
        num_scalar_prefetch=2,
        grid=(n_blocks,),
        in_specs=[
            pl.BlockSpec((EXPERT_ROWS, ROW_SUBLANES, LANES), row_map),
            pl.BlockSpec((1, D_MODEL), lambda b, be, nu: (0, 0)),
            pl.BlockSpec((None, None, D_MODEL, D_EXPERT), w_map),
            pl.BlockSpec((None, None, D_MODEL, D_EXPERT), w_map),
            pl.BlockSpec((None, None, D_EXPERT, D_MODEL), w_map),
        ],
        out_specs=pl.BlockSpec((EXPERT_ROWS, ROW_SUBLANES, LANES), lambda b, be, nu: (b, 0, 0)),
        scratch_shapes=[
            pltpu.VMEM((D_MODEL, D_EXPERT), bf16),
            pltpu.VMEM((D_MODEL, D_EXPERT), bf16),
            pltpu.VMEM((D_EXPERT, D_MODEL), bf16),
        ],
    )
    return pl.pallas_call(
        _expert_kernel,
        grid_spec=grid_spec,
        out_shape=jax.ShapeDtypeStruct((n_rows, ROW_SUBLANES, LANES), f32),
        compiler_params=_params(("arbitrary",)),
        name="experts",
    )(block_e, n_used, xs, m_norm, w1, w3, w2)


def _combine_kernel(dest_ref, x_ref, w_ref, ys_hbm, out_ref, buf, sem):
    def start(j, c):
        for k in range(TOP_K):
            _row_copy(ys_hbm, dest_ref[k, j], buf.at[k], j, sem).start(priority=k)
        return c

    lax.fori_loop(0, MOVE_TILE, start, 0, unroll=ISSUE_UNROLL)
    for k in range(TOP_K):
        _wait_rows(buf.at[k], sem)
    w = w_ref[...]
    y0 = buf[0].reshape(MOVE_TILE, D_MODEL)
    y1 = buf[1].reshape(MOVE_TILE, D_MODEL)
    out_ref[...] = x_ref[...] + (w[:, 0:1] * y0 + w[:, 1:2] * y1)


def _combine(dest_tiles, x2d, w_cols, ys):
    T = x2d.shape[0]
    return pl.pallas_call(
        _combine_kernel,
        grid=(T // MOVE_TILE,),
        in_specs=[
            pl.BlockSpec((None, TOP_K, MOVE_TILE), lambda i: (i, 0, 0),
                         memory_space=pltpu.SMEM),
            pl.BlockSpec((MOVE_TILE, D_MODEL), lambda i: (i, 0)),
            pl.BlockSpec((MOVE_TILE, 8), lambda i: (i, 0)),
            pl.BlockSpec(memory_space=pl.ANY),
        ],
        out_specs=pl.BlockSpec((MOVE_TILE, D_MODEL), lambda i: (i, 0)),
        out_shape=jax.ShapeDtypeStruct((T, D_MODEL), f32),
        scratch_shapes=[
            pltpu.VMEM((TOP_K, MOVE_TILE, ROW_SUBLANES, LANES), f32),
            pltpu.SemaphoreType.DMA(()),
        ],
        compiler_params=_params(("arbitrary",)),
        name="combine",
    )(dest_tiles, x2d, w_cols, ys)


def _moe(layer, x2d, logits_t, m_norm, b_group, b_expert, w1, w3, w2):
    T = x2d.shape[0]
    n_blocks = (T * TOP_K) // EXPERT_ROWS + N_EXPERTS
    n_rows = n_blocks * EXPERT_ROWS

    be = b_expert.reshape(N_EXPERTS, 1)
    bg = jnp.concatenate([b_group, jnp.zeros((8 - N_GROUPS,), f32)]).reshape(8, 1)
    oi, ow, cnt = _route(logits_t, be, bg)

    counts = cnt[:, 0].astype(jnp.int32)
    padded = (counts + EXPERT_ROWS - 1) // EXPERT_ROWS * EXPERT_ROWS
    pend = jnp.cumsum(padded)
    pstart = pend - padded
    n_used = (pend[-1] // EXPERT_ROWS).astype(jnp.int32).reshape(1)
    block_start = jnp.arange(n_blocks, dtype=jnp.int32) * EXPERT_ROWS
    block_e = jnp.minimum(jnp.sum(block_start[:, None] >= pend[None, :], axis=1),
                          N_EXPERTS - 1).astype(jnp.int32)
    e_ids = jnp.arange(N_EXPERTS, dtype=jnp.int32)[:, None, None]
    seg_start = jnp.sum(jnp.where(oi[None, 0:2] == e_ids, pstart[:, None, None], 0), axis=0)
    dest = seg_start + oi[2:4]
    dest_tiles = dest.reshape(TOP_K, T // MOVE_TILE, MOVE_TILE).transpose(1, 0, 2)
    w_cols = ow.T

    xs = _dispatch(pend.astype(jnp.int32), padded, n_used, dest_tiles, x2d, n_rows)
    ys = _experts(layer, block_e, n_used, xs, m_norm, w1, w3, w2)
    return _combine(dest_tiles, x2d, w_cols, ys)


def _log_sigmoid(x):
    return jnp.minimum(x, 0.0) - jnp.log1p(jnp.exp(-jnp.abs(x)))


def _proj_kernel(tiles_per_seq, x_ref, kvn_ref, bn_ref, wkt_ref, wv_ref, wf_ref, bf_ref,
                 wq_ref, wg_ref, gq_ref, gk_ref, q_ref, kt_ref, v_ref, sg_ref, carry_scr):
    i = pl.program_id(0)
    x = x_ref[...]
    hkv = _rms(x, kvn_ref[...]).astype(bf16)
    hq = _rms(x, bn_ref[...]).astype(bf16)

    v_ref[...] = _dot(hkv, wv_ref[...]).astype(bf16)
    sg_ref[...] = jax.nn.sigmoid(_dot(hq, wg_ref[...])).astype(bf16)

    logf = _log_sigmoid(_dot(hkv, wf_ref[...]) + bf_ref[...])
    tm = x.shape[0]
    incl = (lax.broadcasted_iota(jnp.int32, (tm, tm), 0)
            >= lax.broadcasted_iota(jnp.int32, (tm, tm), 1)).astype(f32)
    @pl.when(i % tiles_per_seq == 0)
    def _():
        carry_scr[...] = jnp.zeros_like(carry_scr)

    cum = _dot(incl, logf, precision=lax.Precision.HIGHEST) + carry_scr[...]
    carry_scr[...] = cum[tm - 1:tm, :]
    cum = cum * LOG2E
    cum_t = cum.T

    lane = lax.broadcasted_iota(jnp.int32, (tm, HEAD_PAD), 1)
    sub = lax.broadcasted_iota(jnp.int32, (HEAD_PAD, tm), 0)
    q_raw = _dot(hq, wq_ref[...])
    k_raw_t = _dot_nt(wkt_ref[...], hkv)
    scale = HEAD_DIM ** -0.5 * LOG2E
    for h in range(N_HEADS):
        sl = slice(h * HEAD_PAD, (h + 1) * HEAD_PAD)
        qb = q_raw[:, sl]
        q_ms = jnp.sum(qb * qb, axis=-1, keepdims=True) * (1.0 / HEAD_DIM)
        qn = qb * lax.rsqrt(q_ms + EPS) * gq_ref[...] * scale
        c_hi, c_mid, c_lo = _split3(cum[:, h:h + 1])
        qa = jnp.where(lane < HEAD_DIM, qn,
             jnp.where(lane < AUX_CUM0, 1.0,
             jnp.where(lane == AUX_CUM0, c_hi,
             jnp.where(lane == AUX_CUM0 + 1, c_mid,
             jnp.where(lane == AUX_CUM0 + 2, c_lo, 0.0)))))
        q_ref[:, sl] = qa.astype(bf16)

        kb = k_raw_t[sl, :]
        k_ms = jnp.sum(kb * kb, axis=0, keepdims=True) * (1.0 / HEAD_DIM)
        kn = kb * lax.rsqrt(k_ms + EPS) * gk_ref[...]
        t_hi, t_mid, t_lo = _split3(cum_t[h:h + 1, :])
        ka = jnp.where(sub < HEAD_DIM, kn,
             jnp.where(sub == AUX_ONE0, -t_hi,
             jnp.where(sub == AUX_ONE0 + 1, -t_mid,
             jnp.where(sub == AUX_ONE0 + 2, -t_lo,
             jnp.where(sub < AUX_CUM0 + 3, 1.0, 0.0)))))
        kt_ref[0, sl, :] = ka.astype(bf16)


def _proj(x2d, seq, kv_norm, b_norm, wkt, wv, wf, bfv, wq, wg, gq, gk):
    T = x2d.shape[0]
    n_tiles = T // ROW_TILE
    qw = N_HEADS * HEAD_PAD
    return pl.pallas_call(
        functools.partial(_proj_kernel, seq // ROW_TILE),
        grid=(n_tiles,),
        in_specs=[
            pl.BlockSpec((ROW_TILE, D_MODEL), lambda i: (i, 0)),
            _const_spec((1, D_MODEL)),
            _const_spec((1, D_MODEL)),
            _const_spec((qw, D_MODEL)),
            _const_spec((D_MODEL, D_MODEL)),
            _const_spec((D_MODEL, LANES)),
            _const_spec((1, LANES)),
            _const_spec((D_MODEL, qw)),
            _const_spec((D_MODEL, D_MODEL)),
            _const_spec((1, HEAD_PAD)),
            _const_spec((HEAD_PAD, 1)),
        ],
        out_specs=[
            pl.BlockSpec((ROW_TILE, qw), lambda i: (i, 0)),
            pl.BlockSpec((1, qw, ROW_TILE), lambda i: (i, 0, 0)),
            pl.BlockSpec((ROW_TILE, D_MODEL), lambda i: (i, 0)),
            pl.BlockSpec((ROW_TILE, D_MODEL), lambda i: (i, 0)),
        ],
        out_shape=[
            jax.ShapeDtypeStruct((T, qw), bf16),
            jax.ShapeDtypeStruct((n_tiles, qw, ROW_TILE), bf16),
            jax.ShapeDtypeStruct((T, D_MODEL), bf16),
            jax.ShapeDtypeStruct((T, D_MODEL), bf16),
        ],
        scratch_shapes=[pltpu.VMEM((1, LANES), f32)],
        compiler_params=_params(("arbitrary",)),
        name="proj",
    )(x2d, kv_norm, b_norm, wkt, wv, wf, bfv, wq, wg, gq, gk)


def _attn_kernel(q_ref, kt_ref, v_ref, o_ref, s_scr, m_scr, l_scr, acc_scr):
    qi = pl.program_id(2)
    tiles = ATT_Q // ATT_K
    groups = ATT_K // LANES
    m_scr[...] = jnp.full(m_scr.shape, -jnp.inf, f32)
    col_minus_row = (lax.broadcasted_iota(jnp.int32, (ATT_Q, ATT_K), 1)
                     - lax.broadcasted_iota(jnp.int32, (ATT_Q, ATT_K), 0))

    def sweep_a(trip, masked):
        for t in range(tiles):
            kt_idx = trip * tiles + t
            for hh in range(2):
                sl = slice(hh * HEAD_PAD, (hh + 1) * HEAD_PAD)
                s = _dot(q_ref[:, sl], kt_ref[kt_idx, sl, :])
                if masked:
                    s = jnp.where(col_minus_row <= -t * ATT_K, s, -jnp.inf)
                s_scr[hh, kt_idx] = s
                mx = m_scr[hh]
                for g in range(groups):
                    mx = jnp.maximum(mx, s[:, g * LANES:(g + 1) * LANES])
                m_scr[hh] = mx

    def body_a(trip, c):
        sweep_a(trip, False)
        return c

    lax.fori_loop(0, qi, body_a, 0)
    sweep_a(qi, True)

    for hh in range(2):
        row_max = jnp.max(m_scr[hh], axis=-1, keepdims=True)
        m_scr[hh] = jnp.broadcast_to(row_max, (ATT_Q, LANES))
    l_scr[...] = jnp.zeros(l_scr.shape, f32)
    acc_scr[...] = jnp.zeros(acc_scr.shape, f32)

    def body_b(trip, c):
        for t in range(tiles):
            kt_idx = trip * tiles + t
            row0 = pl.multiple_of(kt_idx * ATT_K, ATT_K)
            vb = v_ref[pl.ds(row0, ATT_K), :]
            for hh in range(2):
                mb = m_scr[hh]
                s = s_scr[hh, kt_idx]
                ps = [jnp.exp2(s[:, g * LANES:(g + 1) * LANES] - mb) for g in range(groups)]
                l_scr[hh] += functools.reduce(lambda a, b: a + b, ps)
                p = jnp.concatenate(ps, axis=1).astype(bf16)
                acc_scr[hh] += _dot(p, vb)
        return c

    lax.fori_loop(0, qi + 1, body_b, 0)

    lane = lax.broadcasted_iota(jnp.int32, (ATT_Q, 2 * HEAD_DIM), 1)
    o0 = acc_scr[0] / jnp.sum(l_scr[0], axis=-1, keepdims=True)
    o1 = acc_scr[1] / jnp.sum(l_scr[1], axis=-1, keepdims=True)
    o_ref[...] = jnp.where(lane < HEAD_DIM, o0, o1).astype(bf16)


def _attention(q, kt, v, batch, seq):
    T = q.shape[0]
    nq = seq // ATT_Q
    nk = seq // ATT_K
    pairs = N_HEADS // 2
    return pl.pallas_call(
        _attn_kernel,
        grid=(batch, pairs, nq),
        in_specs=[
            pl.BlockSpec((ATT_Q, 2 * HEAD_PAD), lambda b, p, i: (b * nq + i, p)),
            pl.BlockSpec((nk, 2 * HEAD_PAD, ATT_K), lambda b, p, i: (b, p, 0)),
            pl.BlockSpec((seq, 2 * HEAD_DIM), lambda b, p, i: (b, p)),
        ],
        out_specs=pl.BlockSpec((ATT_Q, 2 * HEAD_DIM), lambda b, p, i: (b * nq + i, p)),
        out_shape=jax.ShapeDtypeStruct((T, D_MODEL), bf16),
        scratch_shapes=[
            pltpu.VMEM((2, nk, ATT_Q, ATT_K), f32),
            pltpu.VMEM((2, ATT_Q, LANES), f32),
            pltpu.VMEM((2, ATT_Q, LANES), f32),
            pltpu.VMEM((2, ATT_Q, 2 * HEAD_DIM), f32),
        ],
        compiler_params=_params(("parallel", "parallel", "parallel")),
        name="attn",
    )(q, kt, v)


def _attn_out_kernel(x_ref, o_ref, sg_ref, wo_ref, mn_ref, wr_ref, x3_ref, lg_ref):
    gated = (o_ref[...].astype(f32) * sg_ref[...].astype(f32)).astype(bf16)
    x3 = x_ref[...] + _dot(gated, wo_ref[...])
    x3_ref[...] = x3
    lg_ref[...] = _router_logits_t(x3, mn_ref[...], wr_ref[...])


def _attn_out(x2d, o, sg, wo, m_norm, w_router):
    T = x2d.shape[0]
    return pl.pallas_call(
        _attn_out_kernel,
        grid=(T // ROW_TILE,),
        in_specs=[
            pl.BlockSpec((ROW_TILE, D_MODEL), lambda i: (i, 0)),
            pl.BlockSpec((ROW_TILE, D_MODEL), lambda i: (i, 0)),
            pl.BlockSpec((ROW_TILE, D_MODEL), lambda i: (i, 0)),
            _const_spec((D_MODEL, D_MODEL)),
            _const_spec((1, D_MODEL)),
            _const_spec((3 * D_MODEL, LANES)),
        ],
        out_specs=[
            pl.BlockSpec((ROW_TILE, D_MODEL), lambda i: (i, 0)),
            pl.BlockSpec((ROUTER_ROWS, ROW_TILE), lambda i: (0, i)),
        ],
        out_shape=[
            jax.ShapeDtypeStruct((T, D_MODEL), f32),
            jax.ShapeDtypeStruct((ROUTER_ROWS, T), f32),
        ],
        compiler_params=_params(("parallel",)),
        name="attn_out",
    )(x2d, o, sg, wo, m_norm, w_router)


def _router_weight(w_group, w_expert):
    pad = jnp.zeros((D_MODEL, LANES - N_EXPERTS - N_GROUPS), f32)
    w = jnp.concatenate([w_expert, w_group, pad], axis=1)
    w_hi = w.astype(bf16)
    w_lo = (w - w_hi.astype(f32)).astype(bf16)
    return jnp.concatenate([w_hi, w_lo, w_hi], axis=0)


def _pad_heads(w):
    w = w.reshape(D_MODEL, N_HEADS, HEAD_DIM)
    w = jnp.pad(w, ((0, 0), (0, 0), (0, HEAD_PAD - HEAD_DIM)))
    return w.reshape(D_MODEL, N_HEADS * HEAD_PAD)


def kernel(x, a_norm, a_w_in, a_b_in, a_v_norm, a_w_s, a_b_s, a_w_out, kv_norm, kv_w, kv_b_f,
           k_norm, b_norm, b_w_qg, q_norm, b_w_out, m_norm, m_w_group, m_b_group, m_w_expert,
           m_b_expert, m_w1, m_w3, m_w2):
    batch, seq, _ = x.shape
    T = batch * seq
    x2d = x.reshape(T, D_MODEL)

    x1, lg0 = _sgu(
        x2d, a_norm[0].reshape(1, -1), a_w_in[0].astype(bf16), a_b_in[0].reshape(1, -1),
        a_v_norm[0].reshape(1, -1), a_w_s[0], a_b_s[0].T, a_w_out[0].astype(bf16),
        m_norm[0].reshape(1, -1), _router_weight(m_w_group[0], m_w_expert[0]))
    x2 = _moe(0, x1, lg0, m_norm[0].reshape(1, -1), m_b_group[0], m_b_expert[0],
              m_w1, m_w3, m_w2)

    wk = kv_w[:, :D_MODEL]
    wv = kv_w[:, D_MODEL:2 * D_MODEL]
    wf = jnp.pad(kv_w[:, 2 * D_MODEL:], ((0, 0), (0, LANES - N_HEADS)))
    bfv = jnp.pad(kv_b_f, (0, LANES - N_HEADS)).reshape(1, LANES)
    wq = b_w_qg[0][:, :D_MODEL]
    wg = b_w_qg[0][:, D_MODEL:]
    gq = jnp.pad(q_norm[0], (0, HEAD_PAD - HEAD_DIM)).reshape(1, HEAD_PAD)
    gk = jnp.pad(k_norm, (0, HEAD_PAD - HEAD_DIM)).reshape(HEAD_PAD, 1)
    q, kt, v, sg = _proj(
        x2, seq, kv_norm.reshape(1, -1), b_norm[0].reshape(1, -1),
        _pad_heads(wk).T.astype(bf16), wv.astype(bf16), wf.astype(bf16), bfv,
        _pad_heads(wq).astype(bf16), wg.astype(bf16), gq, gk)
    o = _attention(q, kt, v, batch, seq)
    x3, lg1 = _attn_out(x2, o, sg, b_w_out[0].astype(bf16), m_norm[1].reshape(1, -1),
                        _router_weight(m_w_group[1], m_w_expert[1]))
    x4 = _moe(1, x3, lg1, m_norm[1].reshape(1, -1), m_b_group[1], m_b_expert[1],
              m_w1, m_w3, m_w2)
    return x4.reshape(batch, seq, D_MODEL)
```

```python
import functools
import math

import jax
import jax.numpy as jnp
from jax import lax
from jax.experimental import pallas as pl
from jax.experimental.pallas import tpu as pltpu

D_MODEL = 1024
EPS = 1e-6
SGU_BLOCK = 128
SGU_CHUNK = 64
CHUNK_SHIFT = SGU_CHUNK.bit_length() - 1
SGU_WIDTH = 2 * D_MODEL
SGU_GROUPS = 8
SGU_GROUP_DIM = SGU_WIDTH // SGU_GROUPS
N_HEADS = 16
HEAD_DIM = D_MODEL // N_HEADS
N_GROUPS = 4
EXPERTS_PER_GROUP = 8
N_EXPERTS = N_GROUPS * EXPERTS_PER_GROUP
GROUP_SHIFT = EXPERTS_PER_GROUP.bit_length() - 1
TOP_K = 2
D_EXPERT = D_MODEL // 2

LANES = 128
ROW_SUBLANES = D_MODEL // LANES
ISSUE_UNROLL = 8
VMEM_LIMIT_BYTES = 56 * 1024 * 1024

ROW_TILE = 256
ROUTE_TILE = 512
EXPERT_ROWS = 256
MOVE_TILE = 512
ATT_Q = 512
ATT_K = ROW_TILE
HEAD_PAD = 2 * HEAD_DIM
ROUTER_ROWS = 40
AUX_ONE0 = HEAD_DIM
AUX_CUM0 = HEAD_DIM + 3

LOG2E = math.log2(math.e)

bf16 = jnp.bfloat16
f32 = jnp.float32


def _dot(a, b, precision=None):
    return jnp.dot(a, b, preferred_element_type=f32, precision=precision)


def _dot_nt(a, b, precision=None):
    return lax.dot_general(a, b, (((1,), (1,)), ((), ())), preferred_element_type=f32,
                           precision=precision)


def _rms(x, g):
    ms = jnp.mean(x * x, axis=-1, keepdims=True)
    return x * lax.rsqrt(ms + EPS) * g


def _gelu_tanh(z):
    c = math.sqrt(2.0 / math.pi)
    return z * (0.5 * (1.0 + jnp.tanh(c * (z + 0.044715 * (z * z * z)))))


def _split3(c):
    hi = c.astype(bf16).astype(f32)
    r = c - hi
    mid = r.astype(bf16).astype(f32)
    lo = r - mid
    return hi, mid, lo


def _const_spec(shape):
    nd = len(shape)
    return pl.BlockSpec(shape, lambda *_: (0,) * nd)


def _params(sem):
    return pltpu.CompilerParams(dimension_semantics=sem, vmem_limit_bytes=VMEM_LIMIT_BYTES)


def _router_logits_t(x_new, mn, wr3):
    hm = _rms(x_new, mn)
    h_hi = hm.astype(bf16)
    h_lo = (hm - h_hi.astype(f32)).astype(bf16)
    lg = _dot(jnp.concatenate([h_hi, h_hi, h_lo], axis=1), wr3)
    return lg.T[:ROUTER_ROWS, :]


def _sgu_kernel(x_ref, an_ref, win_ref, bin_ref, vn_ref, ws_ref, bst_ref, wout_ref, mn_ref,
                wr_ref, x1_ref, lg_ref, u_scr, v_scr, gated_scr):
    x = x_ref[...]
    h = _rms(x, an_ref[...]).astype(bf16)
    u_scr[...] = _gelu_tanh(_dot(h, win_ref[:, :SGU_WIDTH]) + bin_ref[:, :SGU_WIDTH])
    v = _gelu_tanh(_dot(h, win_ref[:, SGU_WIDTH:]) + bin_ref[:, SGU_WIDTH:])
    v_scr[...] = _rms(v, vn_ref[...]).astype(bf16)

    t_chunk = lax.broadcasted_iota(jnp.int32, (SGU_BLOCK, SGU_BLOCK), 0) >> CHUNK_SHIFT
    s_chunk = lax.broadcasted_iota(jnp.int32, (SGU_BLOCK, SGU_BLOCK), 1) >> CHUNK_SHIFT
    causal = t_chunk >= s_chunk
    for g in range(SGU_GROUPS):
        wsm = jnp.where(causal, ws_ref[g], 0.0).astype(bf16)
        cols = slice(g * SGU_GROUP_DIM, (g + 1) * SGU_GROUP_DIM)
        for sb in range(ROW_TILE // SGU_BLOCK):
            rows = slice(sb * SGU_BLOCK, (sb + 1) * SGU_BLOCK)
            mixed = _dot(wsm, v_scr[rows, cols]) + bst_ref[:, g:g + 1]
            gated_scr[rows, cols] = (u_scr[rows, cols] * mixed).astype(bf16)

    x1 = x + _dot(gated_scr[...], wout_ref[...])
    x1_ref[...] = x1
    lg_ref[...] = _router_logits_t(x1, mn_ref[...], wr_ref[...])


def _sgu(x2d, a_norm, w_in, b_in, v_norm, w_s, b_s_t, w_out, m_norm, w_router):
    T = x2d.shape[0]
    return pl.pallas_call(
        _sgu_kernel,
        grid=(T // ROW_TILE,),
        in_specs=[
            pl.BlockSpec((ROW_TILE, D_MODEL), lambda i: (i, 0)),
            _const_spec((1, D_MODEL)),
            _const_spec((D_MODEL, 2 * SGU_WIDTH)),
            _const_spec((1, 2 * SGU_WIDTH)),
            _const_spec((1, SGU_WIDTH)),
            _const_spec((SGU_GROUPS, SGU_BLOCK, SGU_BLOCK)),
            _const_spec((SGU_BLOCK, SGU_GROUPS)),
            _const_spec((SGU_WIDTH, D_MODEL)),
            _const_spec((1, D_MODEL)),
            _const_spec((3 * D_MODEL, LANES)),
        ],
        out_specs=[
            pl.BlockSpec((ROW_TILE, D_MODEL), lambda i: (i, 0)),
            pl.BlockSpec((ROUTER_ROWS, ROW_TILE), lambda i: (0, i)),
        ],
        out_shape=[
            jax.ShapeDtypeStruct((T, D_MODEL), f32),
            jax.ShapeDtypeStruct((ROUTER_ROWS, T), f32),
        ],
        scratch_shapes=[
            pltpu.VMEM((ROW_TILE, SGU_WIDTH), f32),
            pltpu.VMEM((ROW_TILE, SGU_WIDTH), bf16),
            pltpu.VMEM((ROW_TILE, SGU_WIDTH), bf16),
        ],
        compiler_params=_params(("parallel",)),
        name="sgu",
    )(x2d, a_norm, w_in, b_in, v_norm, w_s, b_s_t, w_out, m_norm, w_router)


def _route_kernel(lg_ref, be_ref, bg_ref, oi_ref, ow_ref, cnt_ref, carry_scr):
    step = pl.program_id(0)

    @pl.when(step == 0)
    def _():
        carry_scr[...] = jnp.zeros_like(carry_scr)

    lg = lg_ref[...]
    e_l = lg[0:N_EXPERTS, :] + be_ref[...]
    g_l = lg[N_EXPERTS:N_EXPERTS + 8, :] + bg_ref[...]
    g_row = lax.broadcasted_iota(jnp.int32, g_l.shape, 0).astype(f32)
    g_l = jnp.where(g_row < N_GROUPS, g_l, -jnp.inf)
    g_max = jnp.max(g_l, axis=0, keepdims=True)
    g_sel = jnp.min(jnp.where(g_l == g_max, g_row, 8.0), axis=0, keepdims=True)
    g_den = jnp.sum(jnp.exp(g_l - g_max), axis=0, keepdims=True)
    g_w = 1.0 / g_den

    e_row_i = lax.broadcasted_iota(jnp.int32, e_l.shape, 0)
    e_row = e_row_i.astype(f32)
    e_grp = (e_row_i >> GROUP_SHIFT).astype(f32)
    e_in = jnp.where(e_grp == g_sel, e_l, -jnp.inf)
    m1 = jnp.max(e_in, axis=0, keepdims=True)
    i1 = jnp.min(jnp.where(e_in == m1, e_row, float(N_EXPERTS)), axis=0, keepdims=True)
    e_in2 = jnp.where(e_row == i1, -jnp.inf, e_in)
    m2 = jnp.max(e_in2, axis=0, keepdims=True)
    i2 = jnp.min(jnp.where(e_in2 == m2, e_row, float(N_EXPERTS)), axis=0, keepdims=True)
    t = jnp.exp(m2 - m1)
    p1 = 1.0 / (1.0 + t)
    p2 = t / (1.0 + t)

    hit1 = e_row == i1
    hit2 = e_row == i2
    member = jnp.logical_or(hit1, hit2)
    tt = lg.shape[1]
    before = (lax.broadcasted_iota(jnp.int32, (tt, tt), 0)
              < lax.broadcasted_iota(jnp.int32, (tt, tt), 1))
    prefix = _dot(member.astype(bf16), before.astype(bf16))
    rank_full = prefix + carry_scr[...]
    rank1 = jnp.sum(jnp.where(hit1, rank_full, 0.0), axis=0, keepdims=True)
    rank2 = jnp.sum(jnp.where(hit2, rank_full, 0.0), axis=0, keepdims=True)
    carry_new = carry_scr[...] + jnp.sum(member.astype(f32), axis=1, keepdims=True)
    carry_scr[...] = carry_new

    oi_ref[...] = jnp.zeros(oi_ref.shape, jnp.int32)
    oi_ref[0:1, :] = i1.astype(jnp.int32)
    oi_ref[1:2, :] = i2.astype(jnp.int32)
    oi_ref[2:3, :] = rank1.astype(jnp.int32)
    oi_ref[3:4, :] = rank2.astype(jnp.int32)
    ow_ref[...] = jnp.zeros(ow_ref.shape, f32)
    ow_ref[0:1, :] = g_w * p1
    ow_ref[1:2, :] = g_w * p2
    cnt_ref[...] = jnp.broadcast_to(carry_new, cnt_ref.shape)


def _route(logits_t, b_expert, b_group):
    T = logits_t.shape[1]
    return pl.pallas_call(
        _route_kernel,
        grid=(T // ROUTE_TILE,),
        in_specs=[
            pl.BlockSpec((ROUTER_ROWS, ROUTE_TILE), lambda i: (0, i)),
            _const_spec((N_EXPERTS, 1)),
            _const_spec((8, 1)),
        ],
        out_specs=[
            pl.BlockSpec((8, ROUTE_TILE), lambda i: (0, i)),
            pl.BlockSpec((8, ROUTE_TILE), lambda i: (0, i)),
            _const_spec((N_EXPERTS, LANES)),
        ],
        out_shape=[
            jax.ShapeDtypeStruct((8, T), jnp.int32),
            jax.ShapeDtypeStruct((8, T), f32),
            jax.ShapeDtypeStruct((N_EXPERTS, LANES), f32),
        ],
        scratch_shapes=[pltpu.VMEM((N_EXPERTS, 1), f32)],
        compiler_params=_params(("arbitrary",)),
        name="route",
    )(logits_t, b_expert, b_group)


def _row_copy(src, s, dst, d, sem):
    return pltpu.make_async_copy(src.at[s], dst.at[d], sem)


def _wait_rows(buf, sem):
    pltpu.make_async_copy(buf, buf, sem).wait()


def _dispatch_kernel(n_blocks, pend_ref, padded_ref, nu_ref, dest_ref, x_ref, xs_hbm,
                     zero_scr, x3_scr, zsem, sem):
    @pl.when(pl.program_id(0) == 0)
    def _():
        zero_scr[...] = jnp.zeros_like(zero_scr)

        def fill(row0):
            return pltpu.make_async_copy(zero_scr, xs_hbm.at[pl.ds(row0, EXPERT_ROWS)], zsem)

        for e in range(N_EXPERTS):
            @pl.when(padded_ref[e] > 0)
            def _():
                fill(pend_ref[e] - EXPERT_ROWS).start()

            @pl.when(nu_ref[0] + e < n_blocks)
            def _():
                fill((nu_ref[0] + e) * EXPERT_ROWS).start()

        for e in range(N_EXPERTS):
            @pl.when(padded_ref[e] > 0)
            def _():
                fill(0).wait()

            @pl.when(nu_ref[0] + e < n_blocks)
            def _():
                fill(0).wait()

    i = pl.program_id(0)
    slot = i % 2
    stage = x3_scr.at[slot]
    stage[...] = x_ref[...].reshape(stage.shape)

    def start(j, c):
        for k in range(TOP_K):
            _row_copy(stage, j, xs_hbm, dest_ref[k, j], sem.at[slot]).start(priority=k)
        return c

    lax.fori_loop(0, MOVE_TILE, start, 0, unroll=ISSUE_UNROLL)

    @pl.when(i > 0)
    def _():
        for k in range(TOP_K):
            _wait_rows(x3_scr.at[1 - slot], sem.at[1 - slot])

    @pl.when(i == pl.num_programs(0) - 1)
    def _():
        for k in range(TOP_K):
            _wait_rows(stage, sem.at[slot])


def _dispatch(pend, padded, n_used, dest_tiles, x2d, n_rows):
    T = x2d.shape[0]
    n_blocks = n_rows // EXPERT_ROWS
    grid_spec = pltpu.PrefetchScalarGridSpec(
        num_scalar_prefetch=3,
        grid=(T // MOVE_TILE,),
        in_specs=[
            pl.BlockSpec((None, TOP_K, MOVE_TILE), lambda i, *_: (i, 0, 0),
                         memory_space=pltpu.SMEM),
            pl.BlockSpec((MOVE_TILE, D_MODEL), lambda i, *_: (i, 0)),
        ],
        out_specs=pl.BlockSpec(memory_space=pl.ANY),
        scratch_shapes=[
            pltpu.VMEM((EXPERT_ROWS, ROW_SUBLANES, LANES), f32),
            pltpu.VMEM((2, MOVE_TILE, ROW_SUBLANES, LANES), f32),
            pltpu.SemaphoreType.DMA(()),
            pltpu.SemaphoreType.DMA((2,)),
        ],
    )
    return pl.pallas_call(
        functools.partial(_dispatch_kernel, n_blocks),
        grid_spec=grid_spec,
        out_shape=jax.ShapeDtypeStruct((n_rows, ROW_SUBLANES, LANES), f32),
        compiler_params=_params(("arbitrary",)),
        name="dispatch",
    )(pend, padded, n_used, dest_tiles, x2d)


def _expert_kernel(be_ref, nu_ref, xs_ref, mn_ref, w1_ref, w3_ref, w2_ref, ys_ref,
                   h_even, h_odd, w1_scr, w3_scr, w2_scr):
    b = pl.program_id(0)
    nu = nu_ref[0]

    @pl.when(b == 0)
    def _():
        h_odd[...] = jnp.zeros_like(h_odd)

    def step(h_in, h_out):
        h_out[...] = _rms(xs_ref[...].reshape(EXPERT_ROWS, D_MODEL), mn_ref[...]).astype(bf16)
        h = h_in[...]
        a = _dot(h, w1_scr[...])
        c = _dot(h, w3_scr[...])
        hid = (a * jax.nn.sigmoid(a)) * c
        ys_ref[...] = _dot(hid.astype(bf16), w2_scr[...]).reshape(ys_ref.shape)

    @pl.when(b <= nu)
    def _():
        prev = jnp.maximum(b - 1, 0)
        new_expert = jnp.logical_or(
            b == 0, jnp.logical_and(b >= 2, be_ref[prev] != be_ref[jnp.maximum(b - 2, 0)]))

        @pl.when(new_expert)
        def _():
            w1_scr[...] = w1_ref[...].astype(bf16)
            w3_scr[...] = w3_ref[...].astype(bf16)
            w2_scr[...] = w2_ref[...].astype(bf16)

        @pl.when(b % 2 == 0)
        def _():
            step(h_odd, h_even)

        @pl.when(b % 2 == 1)
        def _():
            step(h_even, h_odd)

    @pl.when(b > nu)
    def _():
        ys_ref[...] = jnp.zeros_like(ys_ref)


def _experts(layer, block_e, n_used, xs, m_norm, w1, w3, w2):
    n_rows = xs.shape[0]
    n_blocks = n_rows // EXPERT_ROWS

    def row_map(b, be, nu):
        return (jnp.minimum(b, nu[0] - 1), 0, 0)

    def w_map(b, be, nu):
        return (layer, be[jnp.maximum(b - 1, 0)], 0, 0)

    grid_spec = pltpu.PrefetchScalarGridSpec(
        num_scalar_prefetch=2,
        grid=(n_blocks + 1,),
        in_specs=[
            pl.BlockSpec((EXPERT_ROWS, ROW_SUBLANES, LANES), row_map),
            pl.BlockSpec((1, D_MODEL), lambda b, be, nu: (0, 0)),
            pl.BlockSpec((None, None, D_MODEL, D_EXPERT), w_map),
            pl.BlockSpec((None, None, D_MODEL, D_EXPERT), w_map),
            pl.BlockSpec((None, None, D_EXPERT, D_MODEL), w_map),
        ],
        out_specs=pl.BlockSpec((EXPERT_ROWS, ROW_SUBLANES, LANES),
                               lambda b, be, nu: (jnp.maximum(b - 1, 0), 0, 0)),
        scratch_shapes=[
            pltpu.VMEM((EXPERT_ROWS, D_MODEL), bf16),
            pltpu.VMEM((EXPERT_ROWS, D_MODEL), bf16),
            pltpu.VMEM((D_MODEL, D_EXPERT), bf16),
            pltpu.VMEM((D_MODEL, D_EXPERT), bf16),
            pltpu.VMEM((D_EXPERT, D_MODEL), bf16),
        ],
    )
    return pl.pallas_call(
        _expert_kernel,
        grid_spec=grid_spec,
        out_shape=jax.ShapeDtypeStruct((n_rows, ROW_SUBLANES, LANES), f32),
        compiler_params=_params(("arbitrary",)),
        name="experts",
    )(block_e, n_used, xs, m_norm, w1, w3, w2)


def _combine_kernel(dest_ref, next_ref, x_ref, w_ref, ys_hbm, out_ref, buf, sem):
    i = pl.program_id(0)
    slot = i % 2

    def gather(idx_ref, s):
        def start(j, c):
            for k in range(TOP_K):
                _row_copy(ys_hbm, idx_ref[k, j], buf.at[s, k], j, sem.at[s]).start(priority=k)
            return c

        lax.fori_loop(0, MOVE_TILE, start, 0, unroll=ISSUE_UNROLL)

    @pl.when(i == 0)
    def _():
        gather(dest_ref, slot)

    @pl.when(i + 1 < pl.num_programs(0))
    def _():
        gather(next_ref, 1 - slot)

    for k in range(TOP_K):
        _wait_rows(buf.at[slot, k], sem.at[slot])
    w = w_ref[...]
    y0 = buf[slot, 0].reshape(MOVE_TILE, D_MODEL)
    y1 = buf[slot, 1].reshape(MOVE_TILE, D_MODEL)
    out_ref[...] = x_ref[...] + (w[:, 0:1] * y0 + w[:, 1:2] * y1)


def _combine(dest_tiles, x2d, w_cols, ys):
    T = x2d.shape[0]
    n_tiles = T // MOVE_TILE
    return pl.pallas_call(
        _combine_kernel,
        grid=(n_tiles,),
        in_specs=[
            pl.BlockSpec((None, TOP_K, MOVE_TILE), lambda i: (i, 0, 0),
                         memory_space=pltpu.SMEM),
            pl.BlockSpec((None, TOP_K, MOVE_TILE),
                         lambda i: (jnp.minimum(i + 1, n_tiles - 1), 0, 0),
                         memory_space=pltpu.SMEM),
            pl.BlockSpec((MOVE_TILE, D_MODEL), lambda i: (i, 0)),
            pl.BlockSpec((MOVE_TILE, 8), lambda i: (i, 0)),
            pl.BlockSpec(memory_space=pl.ANY),
        ],
        out_specs=pl.BlockSpec((MOVE_TILE, D_MODEL), lambda i: (i, 0)),
        out_shape=jax.ShapeDtypeStruct((T, D_MODEL), f32),
        scratch_shapes=[
            pltpu.VMEM((2, TOP_K, MOVE_TILE, ROW_SUBLANES, LANES), f32),
            pltpu.SemaphoreType.DMA((2,)),
        ],
        compiler_params=_params(("arbitrary",)),
        name="combine",
    )(dest_tiles, dest_tiles, x2d, w_cols, ys)


def _moe(layer, x2d, logits_t, m_norm, b_group, b_expert, w1, w3, w2):
    T = x2d.shape[0]
    n_blocks = (T * TOP_K) // EXPERT_ROWS + N_EXPERTS
    n_rows = n_blocks * EXPERT_ROWS

    be = b_expert.reshape(N_EXPERTS, 1)
    bg = jnp.concatenate([b_group, jnp.zeros((8 - N_GROUPS,), f32)]).reshape(8, 1)
    oi, ow, cnt = _route(logits_t, be, bg)

    counts = cnt[:, 0].astype(jnp.int32)
    padded = (counts + EXPERT_ROWS - 1) // EXPERT_ROWS * EXPERT_ROWS
    pend = jnp.cumsum(padded)
    pstart = pend - padded
    n_used = (pend[-1] // EXPERT_ROWS).astype(jnp.int32).reshape(1)
    block_start = jnp.arange(n_blocks, dtype=jnp.int32) * EXPERT_ROWS
    block_e = jnp.minimum(jnp.sum(block_start[:, None] >= pend[None, :], axis=1),
                          N_EXPERTS - 1).astype(jnp.int32)
    e_ids = jnp.arange(N_EXPERTS, dtype=jnp.int32)[:, None, None]
    seg_start = jnp.sum(jnp.where(oi[None, 0:2] == e_ids, pstart[:, None, None], 0), axis=0)
    dest = seg_start + oi[2:4]
    dest_tiles = dest.reshape(TOP_K, T // MOVE_TILE, MOVE_TILE).transpose(1, 0, 2)
    w_cols = ow.T

    xs = _dispatch(pend.astype(jnp.int32), padded, n_used, dest_tiles, x2d, n_rows)
    ys = _experts(layer, block_e, n_used, xs, m_norm, w1, w3, w2)
    return _combine(dest_tiles, x2d, w_cols, ys)


def _log_sigmoid(x):
    return jnp.minimum(x, 0.0) - jnp.log1p(jnp.exp(-jnp.abs(x)))


def _proj_kernel(tiles_per_seq, x_ref, kvn_ref, bn_ref, wkt_ref, wv_ref, wf_ref, bf_ref,
                 wq_ref, wg_ref, gq_ref, gk_ref, q_ref, kt_ref, v_ref, sg_ref, carry_scr):
    i = pl.program_id(0)
    x = x_ref[...]
    hkv = _rms(x, kvn_ref[...]).astype(bf16)
    hq = _rms(x, bn_ref[...]).astype(bf16)

    v_ref[...] = _dot(hkv, wv_ref[...]).astype(bf16)
    sg_ref[...] = jax.nn.sigmoid(_dot(hq, wg_ref[...])).astype(bf16)

    logf = _log_sigmoid(_dot(hkv, wf_ref[...]) + bf_ref[...])
    tm = x.shape[0]
    incl = (lax.broadcasted_iota(jnp.int32, (tm, tm), 0)
            >= lax.broadcasted_iota(jnp.int32, (tm, tm), 1)).astype(f32)
    @pl.when(i % tiles_per_seq == 0)
    def _():
        carry_scr[...] = jnp.zeros_like(carry_scr)

    cum = _dot(incl, logf, precision=lax.Precision.HIGHEST) + carry_scr[...]
    carry_scr[...] = cum[tm - 1:tm, :]
    cum = cum * LOG2E
    cum_t = cum.T

    lane = lax.broadcasted_iota(jnp.int32, (tm, HEAD_PAD), 1)
    sub = lax.broadcasted_iota(jnp.int32, (HEAD_PAD, tm), 0)
    q_raw = _dot(hq, wq_ref[...])
    k_raw_t = _dot_nt(wkt_ref[...], hkv)
    scale = HEAD_DIM ** -0.5 * LOG2E
    for h in range(N_HEADS):
        sl = slice(h * HEAD_PAD, (h + 1) * HEAD_PAD)
        qb = q_raw[:, sl]
        q_ms = jnp.sum(qb * qb, axis=-1, keepdims=True) * (1.0 / HEAD_DIM)
        qn = qb * lax.rsqrt(q_ms + EPS) * gq_ref[...] * scale
        c_hi, c_mid, c_lo = _split3(cum[:, h:h + 1])
        qa = jnp.where(lane < HEAD_DIM, qn,
             jnp.where(lane < AUX_CUM0, 1.0,
             jnp.where(lane == AUX_CUM0, c_hi,
             jnp.where(lane == AUX_CUM0 + 1, c_mid,
             jnp.where(lane == AUX_CUM0 + 2, c_lo, 0.0)))))
        q_ref[:, sl] = qa.astype(bf16)

        kb = k_raw_t[sl, :]
        k_ms = jnp.sum(kb * kb, axis=0, keepdims=True) * (1.0 / HEAD_DIM)
        kn = kb * lax.rsqrt(k_ms + EPS) * gk_ref[...]
        t_hi, t_mid, t_lo = _split3(cum_t[h:h + 1, :])
        ka = jnp.where(sub < HEAD_DIM, kn,
             jnp.where(sub == AUX_ONE0, -t_hi,
             jnp.where(sub == AUX_ONE0 + 1, -t_mid,
             jnp.where(sub == AUX_ONE0 + 2, -t_lo,
             jnp.where(sub < AUX_CUM0 + 3, 1.0, 0.0)))))
        kt_ref[0, sl, :] = ka.astype(bf16)


def _proj(x2d, seq, kv_norm, b_norm, wkt, wv, wf, bfv, wq, wg, gq, gk):
    T = x2d.shape[0]
    n_tiles = T // ROW_TILE
    qw = N_HEADS * HEAD_PAD
    return pl.pallas_call(
        functools.partial(_proj_kernel, seq // ROW_TILE),
        grid=(n_tiles,),
        in_specs=[
            pl.BlockSpec((ROW_TILE, D_MODEL), lambda i: (i, 0)),
            _const_spec((1, D_MODEL)),
            _const_spec((1, D_MODEL)),
            _const_spec((qw, D_MODEL)),
            _const_spec((D_MODEL, D_MODEL)),
            _const_spec((D_MODEL, LANES)),
            _const_spec((1, LANES)),
            _const_spec((D_MODEL, qw)),
            _const_spec((D_MODEL, D_MODEL)),
            _const_spec((1, HEAD_PAD)),
            _const_spec((HEAD_PAD, 1)),
        ],
        out_specs=[
            pl.BlockSpec((ROW_TILE, qw), lambda i: (i, 0)),
            pl.BlockSpec((1, qw, ROW_TILE), lambda i: (i, 0, 0)),
            pl.BlockSpec((ROW_TILE, D_MODEL), lambda i: (i, 0)),
            pl.BlockSpec((ROW_TILE, D_MODEL), lambda i: (i, 0)),
        ],
        out_shape=[
            jax.ShapeDtypeStruct((T, qw), bf16),
            jax.ShapeDtypeStruct((n_tiles, qw, ROW_TILE), bf16),
            jax.ShapeDtypeStruct((T, D_MODEL), bf16),
            jax.ShapeDtypeStruct((T, D_MODEL), bf16),
        ],
        scratch_shapes=[pltpu.VMEM((1, LANES), f32)],
        compiler_params=_params(("arbitrary",)),
        name="proj",
    )(x2d, kv_norm, b_norm, wkt, wv, wf, bfv, wq, wg, gq, gk)


def _attn_kernel(q_ref, kt_ref, v_ref, o_ref, s_scr, m_scr, l_scr, acc_scr):
    qi = pl.program_id(2)
    tiles = ATT_Q // ATT_K
    groups = ATT_K // LANES
    heads = [slice(hh * HEAD_PAD, (hh + 1) * HEAD_PAD) for hh in range(2)]
    m_scr[...] = jnp.full(m_scr.shape, -jnp.inf, f32)

    def lane_max(mx, s):
        for g in range(groups):
            mx = jnp.maximum(mx, s[:, g * LANES:(g + 1) * LANES])
        return mx

    def probs(s, mb):
        ps = [jnp.exp2(s[:, g * LANES:(g + 1) * LANES] - mb) for g in range(groups)]
        return ps, functools.reduce(lambda a, b: a + b, ps)

    def body_a(trip, c):
        for t in range(tiles):
            kt_idx = trip * tiles + t
            for hh in range(2):
                s = _dot(q_ref[:, heads[hh]], kt_ref[kt_idx, heads[hh], :])
                s_scr[hh, kt_idx] = s
                m_scr[hh] = lane_max(m_scr[hh], s)
        return c

    lax.fori_loop(0, qi, body_a, 0)
    for t in range(tiles):
        kt_idx = qi * tiles + t
        rows = ATT_Q - t * ATT_K
        visible = (lax.broadcasted_iota(jnp.int32, (rows, ATT_K), 1)
                   <= lax.broadcasted_iota(jnp.int32, (rows, ATT_K), 0))
        for hh in range(2):
            s = _dot(q_ref[t * ATT_K:, heads[hh]], kt_ref[kt_idx, heads[hh], :])
            s = jnp.where(visible, s, -jnp.inf)
            s_scr[hh, kt_idx, t * ATT_K:, :] = s
            m_scr[hh, t * ATT_K:, :] = lane_max(m_scr[hh, t * ATT_K:, :], s)

    for hh in range(2):
        row_max = jnp.max(m_scr[hh], axis=-1, keepdims=True)
        m_scr[hh] = jnp.broadcast_to(row_max, (ATT_Q, LANES))
    l_scr[...] = jnp.zeros(l_scr.shape, f32)
    acc_scr[...] = jnp.zeros(acc_scr.shape, f32)

    def body_b(trip, c):
        row0 = pl.multiple_of(trip * ATT_Q, ATT_Q)
        vb = v_ref[pl.ds(row0, ATT_Q), :]
        for hh in range(2):
            mb = m_scr[hh]
            parts, lsum = [], l_scr[hh]
            for t in range(tiles):
                ps, psum = probs(s_scr[hh, trip * tiles + t], mb)
                parts += ps
                lsum = lsum + psum
            l_scr[hh] = lsum
            acc_scr[hh] += _dot(jnp.concatenate(parts, axis=1).astype(bf16), vb)
        return c

    lax.fori_loop(0, qi, body_b, 0)
    for t in range(tiles):
        kt_idx = qi * tiles + t
        row0 = pl.multiple_of(kt_idx * ATT_K, ATT_K)
        vb = v_ref[pl.ds(row0, ATT_K), :]
        for hh in range(2):
            ps, psum = probs(s_scr[hh, kt_idx, t * ATT_K:, :], m_scr[hh, t * ATT_K:, :])
            l_scr[hh, t * ATT_K:, :] += psum
            acc_scr[hh, t * ATT_K:, :] += _dot(jnp.concatenate(ps, axis=1).astype(bf16), vb)

    lane = lax.broadcasted_iota(jnp.int32, (ATT_Q, 2 * HEAD_DIM), 1)
    o0 = acc_scr[0] / jnp.sum(l_scr[0], axis=-1, keepdims=True)
    o1 = acc_scr[1] / jnp.sum(l_scr[1], axis=-1, keepdims=True)
    o_ref[...] = jnp.where(lane < HEAD_DIM, o0, o1).astype(bf16)


def _attention(q, kt, v, batch, seq):
    T = q.shape[0]
    nq = seq // ATT_Q
    nk = seq // ATT_K
    pairs = N_HEADS // 2
    return pl.pallas_call(
        _attn_kernel,
        grid=(batch, pairs, nq),
        in_specs=[
            pl.BlockSpec((ATT_Q, 2 * HEAD_PAD), lambda b, p, i: (b * nq + i, p)),
            pl.BlockSpec((nk, 2 * HEAD_PAD, ATT_K), lambda b, p, i: (b, p, 0)),
            pl.BlockSpec((seq, 2 * HEAD_DIM), lambda b, p, i: (b, p)),
        ],
        out_specs=pl.BlockSpec((ATT_Q, 2 * HEAD_DIM), lambda b, p, i: (b * nq + i, p)),
        out_shape=jax.ShapeDtypeStruct((T, D_MODEL), bf16),
        scratch_shapes=[
            pltpu.VMEM((2, nk, ATT_Q, ATT_K), f32),
            pltpu.VMEM((2, ATT_Q, LANES), f32),
            pltpu.VMEM((2, ATT_Q, LANES), f32),
            pltpu.VMEM((2, ATT_Q, 2 * HEAD_DIM), f32),
        ],
        compiler_params=_params(("parallel", "parallel", "parallel")),
        name="attn",
    )(q, kt, v)


def _attn_out_kernel(x_ref, o_ref, sg_ref, wo_ref, mn_ref, wr_ref, x3_ref, lg_ref):
    gated = (o_ref[...].astype(f32) * sg_ref[...].astype(f32)).astype(bf16)
    x3 = x_ref[...] + _dot(gated, wo_ref[...])
    x3_ref[...] = x3
    lg_ref[...] = _router_logits_t(x3, mn_ref[...], wr_ref[...])


def _attn_out(x2d, o, sg, wo, m_norm, w_router):
    T = x2d.shape[0]
    return pl.pallas_call(
        _attn_out_kernel,
        grid=(T // ROW_TILE,),
        in_specs=[
            pl.BlockSpec((ROW_TILE, D_MODEL), lambda i: (i, 0)),
            pl.BlockSpec((ROW_TILE, D_MODEL), lambda i: (i, 0)),
            pl.BlockSpec((ROW_TILE, D_MODEL), lambda i: (i, 0)),
            _const_spec((D_MODEL, D_MODEL)),
            _const_spec((1, D_MODEL)),
            _const_spec((3 * D_MODEL, LANES)),
        ],
        out_specs=[
            pl.BlockSpec((ROW_TILE, D_MODEL), lambda i: (i, 0)),
            pl.BlockSpec((ROUTER_ROWS, ROW_TILE), lambda i: (0, i)),
        ],
        out_shape=[
            jax.ShapeDtypeStruct((T, D_MODEL), f32),
            jax.ShapeDtypeStruct((ROUTER_ROWS, T), f32),
        ],
        compiler_params=_params(("parallel",)),
        name="attn_out",
    )(x2d, o, sg, wo, m_norm, w_router)


def _router_weight(w_group, w_expert):
    pad = jnp.zeros((D_MODEL, LANES - N_EXPERTS - N_GROUPS), f32)
    w = jnp.concatenate([w_expert, w_group, pad], axis=1)
    w_hi = w.astype(bf16)
    w_lo = (w - w_hi.astype(f32)).astype(bf16)
    return jnp.concatenate([w_hi, w_lo, w_hi], axis=0)


def _pad_heads(w):
    w = w.reshape(D_MODEL, N_HEADS, HEAD_DIM)
    w = jnp.pad(w, ((0, 0), (0, 0), (0, HEAD_PAD - HEAD_DIM)))
    return w.reshape(D_MODEL, N_HEADS * HEAD_PAD)


def kernel(x, a_norm, a_w_in, a_b_in, a_v_norm, a_w_s, a_b_s, a_w_out, kv_norm, kv_w, kv_b_f,
           k_norm, b_norm, b_w_qg, q_norm, b_w_out, m_norm, m_w_group, m_b_group, m_w_expert,
           m_b_expert, m_w1, m_w3, m_w2):
    batch, seq, _ = x.shape
    T = batch * seq
    x2d = x.reshape(T, D_MODEL)

    x1, lg0 = _sgu(
        x2d, a_norm[0].reshape(1, -1), a_w_in[0].astype(bf16), a_b_in[0].reshape(1, -1),
        a_v_norm[0].reshape(1, -1), a_w_s[0], a_b_s[0].T, a_w_out[0].astype(bf16),
        m_norm[0].reshape(1, -1), _router_weight(m_w_group[0], m_w_expert[0]))
    x2 = _moe(0, x1, lg0, m_norm[0].reshape(1, -1), m_b_group[0], m_b_expert[0],
              m_w1, m_w3, m_w2)

    wk = kv_w[:, :D_MODEL]
    wv = kv_w[:, D_MODEL:2 * D_MODEL]
    wf = jnp.pad(kv_w[:, 2 * D_MODEL:], ((0, 0), (0, LANES - N_HEADS)))
    bfv = jnp.pad(kv_b_f, (0, LANES - N_HEADS)).reshape(1, LANES)
    wq = b_w_qg[0][:, :D_MODEL]
    wg = b_w_qg[0][:, D_MODEL:]
    gq = jnp.pad(q_norm[0], (0, HEAD_PAD - HEAD_DIM)).reshape(1, HEAD_PAD)
    gk = jnp.pad(k_norm, (0, HEAD_PAD - HEAD_DIM)).reshape(HEAD_PAD, 1)
    q, kt, v, sg = _proj(
        x2, seq, kv_norm.reshape(1, -1), b_norm[0].reshape(1, -1),
        _pad_heads(wk).T.astype(bf16), wv.astype(bf16), wf.astype(bf16), bfv,
        _pad_heads(wq).astype(bf16), wg.astype(bf16), gq, gk)
    o = _attention(q, kt, v, batch, seq)
    x3, lg1 = _attn_out(x2, o, sg, b_w_out[0].astype(bf16), m_norm[1].reshape(1, -1),
                        _router_weight(m_w_group[1], m_w_expert[1]))
    x4 = _moe(1, x3, lg1, m_norm[1].reshape(1, -1), m_b_group[1], m_b_expert[1],
              m_w1, m_w3, m_w2)
    return x4.reshape(batch, seq, D_MODEL)
```

```python
import functools
import math

import jax
import jax.numpy as jnp
from jax import lax
from jax.experimental import pallas as pl
from jax.experimental.pallas import tpu as pltpu

D_MODEL = 1024
EPS = 1e-6
SGU_BLOCK = 128
SGU_CHUNK = 64
CHUNK_SHIFT = SGU_CHUNK.bit_length() - 1
SGU_WIDTH = 2 * D_MODEL
SGU_GROUPS = 8
SGU_GROUP_DIM = SGU_WIDTH // SGU_GROUPS
N_HEADS = 16
HEAD_DIM = D_MODEL // N_HEADS
N_GROUPS = 4
EXPERTS_PER_GROUP = 8
N_EXPERTS = N_GROUPS * EXPERTS_PER_GROUP
GROUP_SHIFT = EXPERTS_PER_GROUP.bit_length() - 1
TOP_K = 2
D_EXPERT = D_MODEL // 2

LANES = 128
ROW_SUBLANES = D_MODEL // LANES
ISSUE_UNROLL = 8
VMEM_LIMIT_BYTES = 56 * 1024 * 1024

ROW_TILE = 256
ROUTE_TILE = 512
EXPERT_ROWS = 256
MOVE_TILE = 512
ATT_Q = 1024
ATT_K = ROW_TILE
PV_TILES = 2
HEAD_PAD = 2 * HEAD_DIM
ROUTER_ROWS = 40
AUX_ONE0 = HEAD_DIM
AUX_CUM0 = HEAD_DIM + 3

LOG2E = math.log2(math.e)

bf16 = jnp.bfloat16
f32 = jnp.float32


def _dot(a, b, precision=None):
    return jnp.dot(a, b, preferred_element_type=f32, precision=precision)


def _dot_nt(a, b, precision=None):
    return lax.dot_general(a, b, (((1,), (1,)), ((), ())), preferred_element_type=f32,
                           precision=precision)


def _rms(x, g):
    ms = jnp.mean(x * x, axis=-1, keepdims=True)
    return x * lax.rsqrt(ms + EPS) * g


def _gelu_tanh(z):
    c = math.sqrt(2.0 / math.pi)
    return z * (0.5 * (1.0 + jnp.tanh(c * (z + 0.044715 * (z * z * z)))))


def _split3(c):
    hi = c.astype(bf16).astype(f32)
    r = c - hi
    mid = r.astype(bf16).astype(f32)
    lo = r - mid
    return hi, mid, lo


def _const_spec(shape):
    nd = len(shape)
    return pl.BlockSpec(shape, lambda *_: (0,) * nd)


def _params(sem):
    return pltpu.CompilerParams(dimension_semantics=sem, vmem_limit_bytes=VMEM_LIMIT_BYTES)


def _router_logits_t(x_new, mn, wr3):
    hm = _rms(x_new, mn)
    h_hi = hm.astype(bf16)
    h_lo = (hm - h_hi.astype(f32)).astype(bf16)
    lg = _dot(jnp.concatenate([h_hi, h_hi, h_lo], axis=1), wr3)
    return lg.T[:ROUTER_ROWS, :]


def _sgu_kernel(x_ref, an_ref, win_ref, bin_ref, vn_ref, ws_ref, bst_ref, wout_ref, mn_ref,
                wr_ref, x1_ref, lg_ref, u_scr, v_scr, gated_scr):
    x = x_ref[...]
    h = _rms(x, an_ref[...]).astype(bf16)
    u_scr[...] = _gelu_tanh(_dot(h, win_ref[:, :SGU_WIDTH]) + bin_ref[:, :SGU_WIDTH])
    v = _gelu_tanh(_dot(h, win_ref[:, SGU_WIDTH:]) + bin_ref[:, SGU_WIDTH:])
    v_scr[...] = _rms(v, vn_ref[...]).astype(bf16)

    t_chunk = lax.broadcasted_iota(jnp.int32, (SGU_BLOCK, SGU_BLOCK), 0) >> CHUNK_SHIFT
    s_chunk = lax.broadcasted_iota(jnp.int32, (SGU_BLOCK, SGU_BLOCK), 1) >> CHUNK_SHIFT
    causal = t_chunk >= s_chunk
    for g in range(SGU_GROUPS):
        wsm = jnp.where(causal, ws_ref[g], 0.0).astype(bf16)
        cols = slice(g * SGU_GROUP_DIM, (g + 1) * SGU_GROUP_DIM)
        for sb in range(ROW_TILE // SGU_BLOCK):
            rows = slice(sb * SGU_BLOCK, (sb + 1) * SGU_BLOCK)
            mixed = _dot(wsm, v_scr[rows, cols]) + bst_ref[:, g:g + 1]
            gated_scr[rows, cols] = (u_scr[rows, cols] * mixed).astype(bf16)

    x1 = x + _dot(gated_scr[...], wout_ref[...])
    x1_ref[...] = x1
    lg_ref[...] = _router_logits_t(x1, mn_ref[...], wr_ref[...])


def _sgu(x2d, a_norm, w_in, b_in, v_norm, w_s, b_s_t, w_out, m_norm, w_router):
    T = x2d.shape[0]
    return pl.pallas_call(
        _sgu_kernel,
        grid=(T // ROW_TILE,),
        in_specs=[
            pl.BlockSpec((ROW_TILE, D_MODEL), lambda i: (i, 0)),
            _const_spec((1, D_MODEL)),
            _const_spec((D_MODEL, 2 * SGU_WIDTH)),
            _const_spec((1, 2 * SGU_WIDTH)),
            _const_spec((1, SGU_WIDTH)),
            _const_spec((SGU_GROUPS, SGU_BLOCK, SGU_BLOCK)),
            _const_spec((SGU_BLOCK, SGU_GROUPS)),
            _const_spec((SGU_WIDTH, D_MODEL)),
            _const_spec((1, D_MODEL)),
            _const_spec((3 * D_MODEL, LANES)),
        ],
        out_specs=[
            pl.BlockSpec((ROW_TILE, D_MODEL), lambda i: (i, 0)),
            pl.BlockSpec((ROUTER_ROWS, ROW_TILE), lambda i: (0, i)),
        ],
        out_shape=[
            jax.ShapeDtypeStruct((T, D_MODEL), f32),
            jax.ShapeDtypeStruct((ROUTER_ROWS, T), f32),
        ],
        scratch_shapes=[
            pltpu.VMEM((ROW_TILE, SGU_WIDTH), f32),
            pltpu.VMEM((ROW_TILE, SGU_WIDTH), bf16),
            pltpu.VMEM((ROW_TILE, SGU_WIDTH), bf16),
        ],
        compiler_params=_params(("parallel",)),
        name="sgu",
    )(x2d, a_norm, w_in, b_in, v_norm, w_s, b_s_t, w_out, m_norm, w_router)


def _route_kernel(lg_ref, be_ref, bg_ref, oi_ref, ow_ref, cnt_ref, carry_scr):
    step = pl.program_id(0)

    @pl.when(step == 0)
    def _():
        carry_scr[...] = jnp.zeros_like(carry_scr)

    lg = lg_ref[...]
    e_l = lg[0:N_EXPERTS, :] + be_ref[...]
    g_l = lg[N_EXPERTS:N_EXPERTS + 8, :] + bg_ref[...]
    g_row = lax.broadcasted_iota(jnp.int32, g_l.shape, 0).astype(f32)
    g_l = jnp.where(g_row < N_GROUPS, g_l, -jnp.inf)
    g_max = jnp.max(g_l, axis=0, keepdims=True)
    g_sel = jnp.min(jnp.where(g_l == g_max, g_row, 8.0), axis=0, keepdims=True)
    g_den = jnp.sum(jnp.exp(g_l - g_max), axis=0, keepdims=True)
    g_w = 1.0 / g_den

    e_row_i = lax.broadcasted_iota(jnp.int32, e_l.shape, 0)
    e_row = e_row_i.astype(f32)
    e_grp = (e_row_i >> GROUP_SHIFT).astype(f32)
    e_in = jnp.where(e_grp == g_sel, e_l, -jnp.inf)
    m1 = jnp.max(e_in, axis=0, keepdims=True)
    i1 = jnp.min(jnp.where(e_in == m1, e_row, float(N_EXPERTS)), axis=0, keepdims=True)
    e_in2 = jnp.where(e_row == i1, -jnp.inf, e_in)
    m2 = jnp.max(e_in2, axis=0, keepdims=True)
    i2 = jnp.min(jnp.where(e_in2 == m2, e_row, float(N_EXPERTS)), axis=0, keepdims=True)
    t = jnp.exp(m2 - m1)
    p1 = 1.0 / (1.0 + t)
    p2 = t / (1.0 + t)

    hit1 = e_row == i1
    hit2 = e_row == i2
    member = jnp.logical_or(hit1, hit2)
    tt = lg.shape[1]
    before = (lax.broadcasted_iota(jnp.int32, (tt, tt), 0)
              < lax.broadcasted_iota(jnp.int32, (tt, tt), 1))
    prefix = _dot(member.astype(bf16), before.astype(bf16))
    rank_full = prefix + carry_scr[...]
    rank1 = jnp.sum(jnp.where(hit1, rank_full, 0.0), axis=0, keepdims=True)
    rank2 = jnp.sum(jnp.where(hit2, rank_full, 0.0), axis=0, keepdims=True)
    carry_new = carry_scr[...] + jnp.sum(member.astype(f32), axis=1, keepdims=True)
    carry_scr[...] = carry_new

    oi_ref[...] = jnp.zeros(oi_ref.shape, jnp.int32)
    oi_ref[0:1, :] = i1.astype(jnp.int32)
    oi_ref[1:2, :] = i2.astype(jnp.int32)
    oi_ref[2:3, :] = rank1.astype(jnp.int32)
    oi_ref[3:4, :] = rank2.astype(jnp.int32)
    ow_ref[...] = jnp.zeros(ow_ref.shape, f32)
    ow_ref[0:1, :] = g_w * p1
    ow_ref[1:2, :] = g_w * p2
    cnt_ref[...] = jnp.broadcast_to(carry_new, cnt_ref.shape)


def _route(logits_t, b_expert, b_group):
    T = logits_t.shape[1]
    return pl.pallas_call(
        _route_kernel,
        grid=(T // ROUTE_TILE,),
        in_specs=[
            pl.BlockSpec((ROUTER_ROWS, ROUTE_TILE), lambda i: (0, i)),
            _const_spec((N_EXPERTS, 1)),
            _const_spec((8, 1)),
        ],
        out_specs=[
            pl.BlockSpec((8, ROUTE_TILE), lambda i: (0, i)),
            pl.BlockSpec((8, ROUTE_TILE), lambda i: (0, i)),
            _const_spec((N_EXPERTS, LANES)),
        ],
        out_shape=[
            jax.ShapeDtypeStruct((8, T), jnp.int32),
            jax.ShapeDtypeStruct((8, T), f32),
            jax.ShapeDtypeStruct((N_EXPERTS, LANES), f32),
        ],
        scratch_shapes=[pltpu.VMEM((N_EXPERTS, 1), f32)],
        compiler_params=_params(("arbitrary",)),
        name="route",
    )(logits_t, b_expert, b_group)


def _row_copy(src, s, dst, d, sem):
    return pltpu.make_async_copy(src.at[s], dst.at[d], sem)


def _wait_rows(buf, sem):
    pltpu.make_async_copy(buf, buf, sem).wait()


def _dispatch_kernel(n_blocks, pend_ref, padded_ref, nu_ref, dest_ref, x_ref, xs_hbm,
                     zero_scr, x3_scr, zsem, sem):
    @pl.when(pl.program_id(0) == 0)
    def _():
        zero_scr[...] = jnp.zeros_like(zero_scr)

        def fill(row0):
            return pltpu.make_async_copy(zero_scr, xs_hbm.at[pl.ds(row0, EXPERT_ROWS)], zsem)

        for e in range(N_EXPERTS):
            @pl.when(padded_ref[e] > 0)
            def _():
                fill(pend_ref[e] - EXPERT_ROWS).start()

            @pl.when(nu_ref[0] + e < n_blocks)
            def _():
                fill((nu_ref[0] + e) * EXPERT_ROWS).start()

        for e in range(N_EXPERTS):
            @pl.when(padded_ref[e] > 0)
            def _():
                fill(0).wait()

            @pl.when(nu_ref[0] + e < n_blocks)
            def _():
                fill(0).wait()

    i = pl.program_id(0)
    slot = i % 2
    stage = x3_scr.at[slot]
    stage[...] = x_ref[...].reshape(stage.shape)

    def start(j, c):
        for k in range(TOP_K):
            _row_copy(stage, j, xs_hbm, dest_ref[k, j], sem.at[slot]).start(priority=k)
        return c

    lax.fori_loop(0, MOVE_TILE, start, 0, unroll=ISSUE_UNROLL)

    @pl.when(i > 0)
    def _():
        for k in range(TOP_K):
            _wait_rows(x3_scr.at[1 - slot], sem.at[1 - slot])

    @pl.when(i == pl.num_programs(0) - 1)
    def _():
        for k in range(TOP_K):
            _wait_rows(stage, sem.at[slot])


def _dispatch(pend, padded, n_used, dest_tiles, x2d, n_rows):
    T = x2d.shape[0]
    n_blocks = n_rows // EXPERT_ROWS
    grid_spec = pltpu.PrefetchScalarGridSpec(
        num_scalar_prefetch=3,
        grid=(T // MOVE_TILE,),
        in_specs=[
            pl.BlockSpec((None, TOP_K, MOVE_TILE), lambda i, *_: (i, 0, 0),
                         memory_space=pltpu.SMEM),
            pl.BlockSpec((MOVE_TILE, D_MODEL), lambda i, *_: (i, 0)),
        ],
        out_specs=pl.BlockSpec(memory_space=pl.ANY),
        scratch_shapes=[
            pltpu.VMEM((EXPERT_ROWS, ROW_SUBLANES, LANES), f32),
            pltpu.VMEM((2, MOVE_TILE, ROW_SUBLANES, LANES), f32),
            pltpu.SemaphoreType.DMA(()),
            pltpu.SemaphoreType.DMA((2,)),
        ],
    )
    return pl.pallas_call(
        functools.partial(_dispatch_kernel, n_blocks),
        grid_spec=grid_spec,
        out_shape=jax.ShapeDtypeStruct((n_rows, ROW_SUBLANES, LANES), f32),
        compiler_params=_params(("arbitrary",)),
        name="dispatch",
    )(pend, padded, n_used, dest_tiles, x2d)


def _expert_kernel(be_ref, nu_ref, xs_ref, mn_ref, w1_ref, w3_ref, w2_ref, ys_ref,
                   h_even, h_odd, w1_scr, w3_scr, w2_scr):
    b = pl.program_id(0)
    nu = nu_ref[0]

    @pl.when(b == 0)
    def _():
        h_odd[...] = jnp.zeros_like(h_odd)

    def step(h_in, h_out):
        h_out[...] = _rms(xs_ref[...].reshape(EXPERT_ROWS, D_MODEL), mn_ref[...]).astype(bf16)
        h = h_in[...]
        a = _dot(h, w1_scr[...])
        c = _dot(h, w3_scr[...])
        hid = (a * jax.nn.sigmoid(a)) * c
        ys_ref[...] = _dot(hid.astype(bf16), w2_scr[...]).reshape(ys_ref.shape)

    @pl.when(b <= nu)
    def _():
        prev = jnp.maximum(b - 1, 0)
        new_expert = jnp.logical_or(
            b == 0, jnp.logical_and(b >= 2, be_ref[prev] != be_ref[jnp.maximum(b - 2, 0)]))

        @pl.when(new_expert)
        def _():
            w1_scr[...] = w1_ref[...].astype(bf16)
            w3_scr[...] = w3_ref[...].astype(bf16)
            w2_scr[...] = w2_ref[...].astype(bf16)

        @pl.when(b % 2 == 0)
        def _():
            step(h_odd, h_even)

        @pl.when(b % 2 == 1)
        def _():
            step(h_even, h_odd)

    @pl.when(b > nu)
    def _():
        ys_ref[...] = jnp.zeros_like(ys_ref)


def _experts(layer, block_e, n_used, xs, m_norm, w1, w3, w2):
    n_rows = xs.shape[0]
    n_blocks = n_rows // EXPERT_ROWS

    def row_map(b, be, nu):
        return (jnp.minimum(b, nu[0] - 1), 0, 0)

    def w_map(b, be, nu):
        return (layer, be[jnp.maximum(b - 1, 0)], 0, 0)

    grid_spec = pltpu.PrefetchScalarGridSpec(
        num_scalar_prefetch=2,
        grid=(n_blocks + 1,),
        in_specs=[
            pl.BlockSpec((EXPERT_ROWS, ROW_SUBLANES, LANES), row_map),
            pl.BlockSpec((1, D_MODEL), lambda b, be, nu: (0, 0)),
            pl.BlockSpec((None, None, D_MODEL, D_EXPERT), w_map),
            pl.BlockSpec((None, None, D_MODEL, D_EXPERT), w_map),
            pl.BlockSpec((None, None, D_EXPERT, D_MODEL), w_map),
        ],
        out_specs=pl.BlockSpec((EXPERT_ROWS, ROW_SUBLANES, LANES),
                               lambda b, be, nu: (jnp.maximum(b - 1, 0), 0, 0)),
        scratch_shapes=[
            pltpu.VMEM((EXPERT_ROWS, D_MODEL), bf16),
            pltpu.VMEM((EXPERT_ROWS, D_MODEL), bf16),
            pltpu.VMEM((D_MODEL, D_EXPERT), bf16),
            pltpu.VMEM((D_MODEL, D_EXPERT), bf16),
            pltpu.VMEM((D_EXPERT, D_MODEL), bf16),
        ],
    )
    return pl.pallas_call(
        _expert_kernel,
        grid_spec=grid_spec,
        out_shape=jax.ShapeDtypeStruct((n_rows, ROW_SUBLANES, LANES), f32),
        compiler_params=_params(("arbitrary",)),
        name="experts",
    )(block_e, n_used, xs, m_norm, w1, w3, w2)


def _combine_kernel(dest_ref, next_ref, x_ref, w_ref, ys_hbm, out_ref, buf, sem):
    i = pl.program_id(0)
    slot = i % 2

    def gather(idx_ref, s):
        def start(j, c):
            for k in range(TOP_K):
                _row_copy(ys_hbm, idx_ref[k, j], buf.at[s, k], j, sem.at[s]).start(priority=k)
            return c

        lax.fori_loop(0, MOVE_TILE, start, 0, unroll=ISSUE_UNROLL)

    @pl.when(i == 0)
    def _():
        gather(dest_ref, slot)

    @pl.when(i + 1 < pl.num_programs(0))
    def _():
        gather(next_ref, 1 - slot)

    for k in range(TOP_K):
        _wait_rows(buf.at[slot, k], sem.at[slot])
    w = w_ref[...]
    y0 = buf[slot, 0].reshape(MOVE_TILE, D_MODEL)
    y1 = buf[slot, 1].reshape(MOVE_TILE, D_MODEL)
    out_ref[...] = x_ref[...] + (w[:, 0:1] * y0 + w[:, 1:2] * y1)


def _combine(dest_tiles, x2d, w_cols, ys):
    T = x2d.shape[0]
    n_tiles = T // MOVE_TILE
    return pl.pallas_call(
        _combine_kernel,
        grid=(n_tiles,),
        in_specs=[
            pl.BlockSpec((None, TOP_K, MOVE_TILE), lambda i: (i, 0, 0),
                         memory_space=pltpu.SMEM),
            pl.BlockSpec((None, TOP_K, MOVE_TILE),
                         lambda i: (jnp.minimum(i + 1, n_tiles - 1), 0, 0),
                         memory_space=pltpu.SMEM),
            pl.BlockSpec((MOVE_TILE, D_MODEL), lambda i: (i, 0)),
            pl.BlockSpec((MOVE_TILE, 8), lambda i: (i, 0)),
            pl.BlockSpec(memory_space=pl.ANY),
        ],
        out_specs=pl.BlockSpec((MOVE_TILE, D_MODEL), lambda i: (i, 0)),
        out_shape=jax.ShapeDtypeStruct((T, D_MODEL), f32),
        scratch_shapes=[
            pltpu.VMEM((2, TOP_K, MOVE_TILE, ROW_SUBLANES, LANES), f32),
            pltpu.SemaphoreType.DMA((2,)),
        ],
        compiler_params=_params(("arbitrary",)),
        name="combine",
    )(dest_tiles, dest_tiles, x2d, w_cols, ys)


def _moe(layer, x2d, logits_t, m_norm, b_group, b_expert, w1, w3, w2):
    T = x2d.shape[0]
    n_blocks = (T * TOP_K) // EXPERT_ROWS + N_EXPERTS
    n_rows = n_blocks * EXPERT_ROWS

    be = b_expert.reshape(N_EXPERTS, 1)
    bg = jnp.concatenate([b_group, jnp.zeros((8 - N_GROUPS,), f32)]).reshape(8, 1)
    oi, ow, cnt = _route(logits_t, be, bg)

    counts = cnt[:, 0].astype(jnp.int32)
    padded = (counts + EXPERT_ROWS - 1) // EXPERT_ROWS * EXPERT_ROWS
    pend = jnp.cumsum(padded)
    pstart = pend - padded
    n_used = (pend[-1] // EXPERT_ROWS).astype(jnp.int32).reshape(1)
    block_start = jnp.arange(n_blocks, dtype=jnp.int32) * EXPERT_ROWS
    block_e = jnp.minimum(jnp.sum(block_start[:, None] >= pend[None, :], axis=1),
                          N_EXPERTS - 1).astype(jnp.int32)
    e_ids = jnp.arange(N_EXPERTS, dtype=jnp.int32)[:, None, None]
    seg_start = jnp.sum(jnp.where(oi[None, 0:2] == e_ids, pstart[:, None, None], 0), axis=0)
    dest = seg_start + oi[2:4]
    dest_tiles = dest.reshape(TOP_K, T // MOVE_TILE, MOVE_TILE).transpose(1, 0, 2)
    w_cols = ow.T

    xs = _dispatch(pend.astype(jnp.int32), padded, n_used, dest_tiles, x2d, n_rows)
    ys = _experts(layer, block_e, n_used, xs, m_norm, w1, w3, w2)
    return _combine(dest_tiles, x2d, w_cols, ys)


def _log_sigmoid(x):
    return jnp.minimum(x, 0.0) - jnp.log1p(jnp.exp(-jnp.abs(x)))


def _proj_kernel(tiles_per_seq, x_ref, kvn_ref, bn_ref, wkt_ref, wv_ref, wf_ref, bf_ref,
                 wq_ref, wg_ref, gq_ref, gk_ref, q_ref, kt_ref, v_ref, sg_ref, carry_scr):
    i = pl.program_id(0)
    x = x_ref[...]
    hkv = _rms(x, kvn_ref[...]).astype(bf16)
    hq = _rms(x, bn_ref[...]).astype(bf16)

    v_ref[...] = _dot(hkv, wv_ref[...]).astype(bf16)
    sg_ref[...] = jax.nn.sigmoid(_dot(hq, wg_ref[...])).astype(bf16)

    logf = _log_sigmoid(_dot(hkv, wf_ref[...]) + bf_ref[...])
    tm = x.shape[0]
    incl = (lax.broadcasted_iota(jnp.int32, (tm, tm), 0)
            >= lax.broadcasted_iota(jnp.int32, (tm, tm), 1)).astype(f32)
    @pl.when(i % tiles_per_seq == 0)
    def _():
        carry_scr[...] = jnp.zeros_like(carry_scr)

    cum = _dot(incl, logf, precision=lax.Precision.HIGHEST) + carry_scr[...]
    carry_scr[...] = cum[tm - 1:tm, :]
    cum = cum * LOG2E
    cum_t = cum.T

    lane = lax.broadcasted_iota(jnp.int32, (tm, HEAD_PAD), 1)
    sub = lax.broadcasted_iota(jnp.int32, (HEAD_PAD, tm), 0)
    q_raw = _dot(hq, wq_ref[...])
    k_raw_t = _dot_nt(wkt_ref[...], hkv)
    scale = HEAD_DIM ** -0.5 * LOG2E
    for h in range(N_HEADS):
        sl = slice(h * HEAD_PAD, (h + 1) * HEAD_PAD)
        qb = q_raw[:, sl]
        q_ms = jnp.sum(qb * qb, axis=-1, keepdims=True) * (1.0 / HEAD_DIM)
        qn = qb * lax.rsqrt(q_ms + EPS) * gq_ref[...] * scale
        c_hi, c_mid, c_lo = _split3(cum[:, h:h + 1])
        qa = jnp.where(lane < HEAD_DIM, qn,
             jnp.where(lane < AUX_CUM0, 1.0,
             jnp.where(lane == AUX_CUM0, c_hi,
             jnp.where(lane == AUX_CUM0 + 1, c_mid,
             jnp.where(lane == AUX_CUM0 + 2, c_lo, 0.0)))))
        q_ref[:, sl] = qa.astype(bf16)

        kb = k_raw_t[sl, :]
        k_ms = jnp.sum(kb * kb, axis=0, keepdims=True) * (1.0 / HEAD_DIM)
        kn = kb * lax.rsqrt(k_ms + EPS) * gk_ref[...]
        t_hi, t_mid, t_lo = _split3(cum_t[h:h + 1, :])
        ka = jnp.where(sub < HEAD_DIM, kn,
             jnp.where(sub == AUX_ONE0, -t_hi,
             jnp.where(sub == AUX_ONE0 + 1, -t_mid,
             jnp.where(sub == AUX_ONE0 + 2, -t_lo,
             jnp.where(sub < AUX_CUM0 + 3, 1.0, 0.0)))))
        kt_ref[0, sl, :] = ka.astype(bf16)


def _proj(x2d, seq, kv_norm, b_norm, wkt, wv, wf, bfv, wq, wg, gq, gk):
    T = x2d.shape[0]
    n_tiles = T // ROW_TILE
    qw = N_HEADS * HEAD_PAD
    return pl.pallas_call(
        functools.partial(_proj_kernel, seq // ROW_TILE),
        grid=(n_tiles,),
        in_specs=[
            pl.BlockSpec((ROW_TILE, D_MODEL), lambda i: (i, 0)),
            _const_spec((1, D_MODEL)),
            _const_spec((1, D_MODEL)),
            _const_spec((qw, D_MODEL)),
            _const_spec((D_MODEL, D_MODEL)),
            _const_spec((D_MODEL, LANES)),
            _const_spec((1, LANES)),
            _const_spec((D_MODEL, qw)),
            _const_spec((D_MODEL, D_MODEL)),
            _const_spec((1, HEAD_PAD)),
            _const_spec((HEAD_PAD, 1)),
        ],
        out_specs=[
            pl.BlockSpec((ROW_TILE, qw), lambda i: (i, 0)),
            pl.BlockSpec((1, qw, ROW_TILE), lambda i: (i, 0, 0)),
            pl.BlockSpec((ROW_TILE, D_MODEL), lambda i: (i, 0)),
            pl.BlockSpec((ROW_TILE, D_MODEL), lambda i: (i, 0)),
        ],
        out_shape=[
            jax.ShapeDtypeStruct((T, qw), bf16),
            jax.ShapeDtypeStruct((n_tiles, qw, ROW_TILE), bf16),
            jax.ShapeDtypeStruct((T, D_MODEL), bf16),
            jax.ShapeDtypeStruct((T, D_MODEL), bf16),
        ],
        scratch_shapes=[pltpu.VMEM((1, LANES), f32)],
        compiler_params=_params(("arbitrary",)),
        name="proj",
    )(x2d, kv_norm, b_norm, wkt, wv, wf, bfv, wq, wg, gq, gk)


def _attn_kernel(q_ref, kt_ref, v_ref, o_ref, s_scr, m_scr, l_scr, acc_scr):
    qi = pl.program_id(2)
    tiles = ATT_Q // ATT_K
    groups = ATT_K // LANES
    heads = [slice(hh * HEAD_PAD, (hh + 1) * HEAD_PAD) for hh in range(2)]
    m_scr[...] = jnp.full(m_scr.shape, -jnp.inf, f32)

    def lane_max(mx, s):
        for g in range(groups):
            mx = jnp.maximum(mx, s[:, g * LANES:(g + 1) * LANES])
        return mx

    def probs(s, mb):
        ps = [jnp.exp2(s[:, g * LANES:(g + 1) * LANES] - mb) for g in range(groups)]
        return ps, functools.reduce(lambda a, b: a + b, ps)

    def body_a(trip, c):
        for t in range(tiles):
            kt_idx = trip * tiles + t
            for hh in range(2):
                s = _dot(q_ref[:, heads[hh]], kt_ref[kt_idx, heads[hh], :])
                s_scr[hh, kt_idx] = s
                m_scr[hh] = lane_max(m_scr[hh], s)
        return c

    lax.fori_loop(0, qi, body_a, 0)
    for t in range(tiles):
        kt_idx = qi * tiles + t
        rows = ATT_Q - t * ATT_K
        visible = (lax.broadcasted_iota(jnp.int32, (rows, ATT_K), 1)
                   <= lax.broadcasted_iota(jnp.int32, (rows, ATT_K), 0))
        for hh in range(2):
            s = _dot(q_ref[t * ATT_K:, heads[hh]], kt_ref[kt_idx, heads[hh], :])
            s = jnp.where(visible, s, -jnp.inf)
            s_scr[hh, kt_idx, t * ATT_K:, :] = s
            m_scr[hh, t * ATT_K:, :] = lane_max(m_scr[hh, t * ATT_K:, :], s)

    for hh in range(2):
        row_max = jnp.max(m_scr[hh], axis=-1, keepdims=True)
        m_scr[hh] = jnp.broadcast_to(row_max, (ATT_Q, LANES))
    l_scr[...] = jnp.zeros(l_scr.shape, f32)
    acc_scr[...] = jnp.zeros(acc_scr.shape, f32)

    def body_b(trip, c):
        for hh in range(2):
            mb = m_scr[hh]
            lsum, acc = l_scr[hh], acc_scr[hh]
            for t0 in range(0, tiles, PV_TILES):
                kt_idx = trip * tiles + t0
                row0 = pl.multiple_of(kt_idx * ATT_K, PV_TILES * ATT_K)
                vb = v_ref[pl.ds(row0, PV_TILES * ATT_K), :]
                parts = []
                for t in range(PV_TILES):
                    ps, psum = probs(s_scr[hh, kt_idx + t], mb)
                    parts += ps
                    lsum = lsum + psum
                acc = acc + _dot(jnp.concatenate(parts, axis=1).astype(bf16), vb)
            l_scr[hh] = lsum
            acc_scr[hh] = acc
        return c

    lax.fori_loop(0, qi, body_b, 0)
    for t in range(tiles):
        kt_idx = qi * tiles + t
        row0 = pl.multiple_of(kt_idx * ATT_K, ATT_K)
        vb = v_ref[pl.ds(row0, ATT_K), :]
        for hh in range(2):
            ps, psum = probs(s_scr[hh, kt_idx, t * ATT_K:, :], m_scr[hh, t * ATT_K:, :])
            l_scr[hh, t * ATT_K:, :] += psum
            acc_scr[hh, t * ATT_K:, :] += _dot(jnp.concatenate(ps, axis=1).astype(bf16), vb)

    lane = lax.broadcasted_iota(jnp.int32, (ATT_Q, 2 * HEAD_DIM), 1)
    o0 = acc_scr[0] / jnp.sum(l_scr[0], axis=-1, keepdims=True)
    o1 = acc_scr[1] / jnp.sum(l_scr[1], axis=-1, keepdims=True)
    o_ref[...] = jnp.where(lane < HEAD_DIM, o0, o1).astype(bf16)


def _attention(q, kt, v, batch, seq):
    T = q.shape[0]
    nq = seq // ATT_Q
    nk = seq // ATT_K
    pairs = N_HEADS // 2
    return pl.pallas_call(
        _attn_kernel,
        grid=(batch, pairs, nq),
        in_specs=[
            pl.BlockSpec((ATT_Q, 2 * HEAD_PAD), lambda b, p, i: (b * nq + i, p)),
            pl.BlockSpec((nk, 2 * HEAD_PAD, ATT_K), lambda b, p, i: (b, p, 0)),
            pl.BlockSpec((seq, 2 * HEAD_DIM), lambda b, p, i: (b, p)),
        ],
        out_specs=pl.BlockSpec((ATT_Q, 2 * HEAD_DIM), lambda b, p, i: (b * nq + i, p)),
        out_shape=jax.ShapeDtypeStruct((T, D_MODEL), bf16),
        scratch_shapes=[
            pltpu.VMEM((2, nk, ATT_Q, ATT_K), f32),
            pltpu.VMEM((2, ATT_Q, LANES), f32),
            pltpu.VMEM((2, ATT_Q, LANES), f32),
            pltpu.VMEM((2, ATT_Q, 2 * HEAD_DIM), f32),
        ],
        compiler_params=_params(("parallel", "parallel", "parallel")),
        name="attn",
    )(q, kt, v)


def _attn_out_kernel(x_ref, o_ref, sg_ref, wo_ref, mn_ref, wr_ref, x3_ref, lg_ref):
    gated = (o_ref[...].astype(f32) * sg_ref[...].astype(f32)).astype(bf16)
    x3 = x_ref[...] + _dot(gated, wo_ref[...])
    x3_ref[...] = x3
    lg_ref[...] = _router_logits_t(x3, mn_ref[...], wr_ref[...])


def _attn_out(x2d, o, sg, wo, m_norm, w_router):
    T = x2d.shape[0]
    return pl.pallas_call(
        _attn_out_kernel,
        grid=(T // ROW_TILE,),
        in_specs=[
            pl.BlockSpec((ROW_TILE, D_MODEL), lambda i: (i, 0)),
            pl.BlockSpec((ROW_TILE, D_MODEL), lambda i: (i, 0)),
            pl.BlockSpec((ROW_TILE, D_MODEL), lambda i: (i, 0)),
            _const_spec((D_MODEL, D_MODEL)),
            _const_spec((1, D_MODEL)),
            _const_spec((3 * D_MODEL, LANES)),
        ],
        out_specs=[
            pl.BlockSpec((ROW_TILE, D_MODEL), lambda i: (i, 0)),
            pl.BlockSpec((ROUTER_ROWS, ROW_TILE), lambda i: (0, i)),
        ],
        out_shape=[
            jax.ShapeDtypeStruct((T, D_MODEL), f32),
            jax.ShapeDtypeStruct((ROUTER_ROWS, T), f32),
        ],
        compiler_params=_params(("parallel",)),
        name="attn_out",
    )(x2d, o, sg, wo, m_norm, w_router)


def _router_weight(w_group, w_expert):
    pad = jnp.zeros((D_MODEL, LANES - N_EXPERTS - N_GROUPS), f32)
    w = jnp.concatenate([w_expert, w_group, pad], axis=1)
    w_hi = w.astype(bf16)
    w_lo = (w - w_hi.astype(f32)).astype(bf16)
    return jnp.concatenate([w_hi, w_lo, w_hi], axis=0)


def _pad_heads(w):
    w = w.reshape(D_MODEL, N_HEADS, HEAD_DIM)
    w = jnp.pad(w, ((0, 0), (0, 0), (0, HEAD_PAD - HEAD_DIM)))
    return w.reshape(D_MODEL, N_HEADS * HEAD_PAD)


def kernel(x, a_norm, a_w_in, a_b_in, a_v_norm, a_w_s, a_b_s, a_w_out, kv_norm, kv_w, kv_b_f,
           k_norm, b_norm, b_w_qg, q_norm, b_w_out, m_norm, m_w_group, m_b_group, m_w_expert,
           m_b_expert, m_w1, m_w3, m_w2):
    batch, seq, _ = x.shape
    T = batch * seq
    x2d = x.reshape(T, D_MODEL)

    x1, lg0 = _sgu(
        x2d, a_norm[0].reshape(1, -1), a_w_in[0].astype(bf16), a_b_in[0].reshape(1, -1),
        a_v_norm[0].reshape(1, -1), a_w_s[0], a_b_s[0].T, a_w_out[0].astype(bf16),
        m_norm[0].reshape(1, -1), _router_weight(m_w_group[0], m_w_expert[0]))
    x2 = _moe(0, x1, lg0, m_norm[0].reshape(1, -1), m_b_group[0], m_b_expert[0],
              m_w1, m_w3, m_w2)

    wk = kv_w[:, :D_MODEL]
    wv = kv_w[:, D_MODEL:2 * D_MODEL]
    wf = jnp.pad(kv_w[:, 2 * D_MODEL:], ((0, 0), (0, LANES - N_HEADS)))
    bfv = jnp.pad(kv_b_f, (0, LANES - N_HEADS)).reshape(1, LANES)
    wq = b_w_qg[0][:, :D_MODEL]
    wg = b_w_qg[0][:, D_MODEL:]
    gq = jnp.pad(q_norm[0], (0, HEAD_PAD - HEAD_DIM)).reshape(1, HEAD_PAD)
    gk = jnp.pad(k_norm, (0, HEAD_PAD - HEAD_DIM)).reshape(HEAD_PAD, 1)
    q, kt, v, sg = _proj(
        x2, seq, kv_norm.reshape(1, -1), b_norm[0].reshape(1, -1),
        _pad_heads(wk).T.astype(bf16), wv.astype(bf16), wf.astype(bf16), bfv,
        _pad_heads(wq).astype(bf16), wg.astype(bf16), gq, gk)
    o = _attention(q, kt, v, batch, seq)
    x3, lg1 = _attn_out(x2, o, sg, b_w_out[0].astype(bf16), m_norm[1].reshape(1, -1),
                        _router_weight(m_w_group[1], m_w_expert[1]))
    x4 = _moe(1, x3, lg1, m_norm[1].reshape(1, -1), m_b_group[1], m_b_expert[1],
              m_w1, m_w3, m_w2)
    return x4.reshape(batch, seq, D_MODEL)
```

```python
import functools
import math

import jax
import jax.numpy as jnp
from jax import lax
from jax.experimental import pallas as pl
from jax.experimental.pallas import tpu as pltpu

D_MODEL = 1024
EPS = 1e-6
SGU_BLOCK = 128
SGU_CHUNK = 64
CHUNK_SHIFT = SGU_CHUNK.bit_length() - 1
SGU_WIDTH = 2 * D_MODEL
SGU_GROUPS = 8
SGU_GROUP_DIM = SGU_WIDTH // SGU_GROUPS
N_HEADS = 16
HEAD_DIM = D_MODEL // N_HEADS
N_GROUPS = 4
EXPERTS_PER_GROUP = 8
N_EXPERTS = N_GROUPS * EXPERTS_PER_GROUP
GROUP_SHIFT = EXPERTS_PER_GROUP.bit_length() - 1
TOP_K = 2
D_EXPERT = D_MODEL // 2

LANES = 128
ROW_SUBLANES = D_MODEL // LANES
ISSUE_UNROLL = 8
VMEM_LIMIT_BYTES = 56 * 1024 * 1024

ROW_TILE = 256
ROUTE_TILE = 512
EXPERT_ROWS = 256
MOVE_TILE = 512
ATT_Q = 1024
ATT_K = ROW_TILE
PV_TILES = 2
HEAD_PAD = 2 * HEAD_DIM
ROUTER_ROWS = 40
AUX_ONE0 = HEAD_DIM
AUX_CUM0 = HEAD_DIM + 3

LOG2E = math.log2(math.e)

bf16 = jnp.bfloat16
f32 = jnp.float32


def _dot(a, b, precision=None):
    return jnp.dot(a, b, preferred_element_type=f32, precision=precision)


def _dot_nt(a, b, precision=None):
    return lax.dot_general(a, b, (((1,), (1,)), ((), ())), preferred_element_type=f32,
                           precision=precision)


def _rms(x, g):
    ms = jnp.mean(x * x, axis=-1, keepdims=True)
    return x * lax.rsqrt(ms + EPS) * g


def _gelu_tanh(z):
    c = math.sqrt(2.0 / math.pi)
    return z * (0.5 * (1.0 + jnp.tanh(c * (z + 0.044715 * (z * z * z)))))


def _split3(c):
    hi = c.astype(bf16).astype(f32)
    r = c - hi
    mid = r.astype(bf16).astype(f32)
    lo = r - mid
    return hi, mid, lo


def _const_spec(shape):
    nd = len(shape)
    return pl.BlockSpec(shape, lambda *_: (0,) * nd)


def _params(sem):
    return pltpu.CompilerParams(dimension_semantics=sem, vmem_limit_bytes=VMEM_LIMIT_BYTES)


def _router_logits_t(x_new, mn, wr3):
    hm = _rms(x_new, mn)
    h_hi = hm.astype(bf16)
    h_lo = (hm - h_hi.astype(f32)).astype(bf16)
    lg = _dot(jnp.concatenate([h_hi, h_hi, h_lo], axis=1), wr3)
    return lg.T[:ROUTER_ROWS, :]


def _sgu_kernel(x_ref, an_ref, win_ref, bin_ref, vn_ref, ws_ref, bst_ref, wout_ref, mn_ref,
                wr_ref, x1_ref, lg_ref, u_scr, v_scr, gated_scr):
    x = x_ref[...]
    h = _rms(x, an_ref[...]).astype(bf16)
    u_scr[...] = _gelu_tanh(_dot(h, win_ref[:, :SGU_WIDTH]) + bin_ref[:, :SGU_WIDTH])
    v = _gelu_tanh(_dot(h, win_ref[:, SGU_WIDTH:]) + bin_ref[:, SGU_WIDTH:])
    v_scr[...] = _rms(v, vn_ref[...]).astype(bf16)

    t_chunk = lax.broadcasted_iota(jnp.int32, (SGU_BLOCK, SGU_BLOCK), 0) >> CHUNK_SHIFT
    s_chunk = lax.broadcasted_iota(jnp.int32, (SGU_BLOCK, SGU_BLOCK), 1) >> CHUNK_SHIFT
    causal = t_chunk >= s_chunk
    for g in range(SGU_GROUPS):
        wsm = jnp.where(causal, ws_ref[g], 0.0).astype(bf16)
        cols = slice(g * SGU_GROUP_DIM, (g + 1) * SGU_GROUP_DIM)
        for sb in range(ROW_TILE // SGU_BLOCK):
            rows = slice(sb * SGU_BLOCK, (sb + 1) * SGU_BLOCK)
            mixed = _dot(wsm, v_scr[rows, cols]) + bst_ref[:, g:g + 1]
            gated_scr[rows, cols] = (u_scr[rows, cols] * mixed).astype(bf16)

    x1 = x + _dot(gated_scr[...], wout_ref[...])
    x1_ref[...] = x1
    lg_ref[...] = _router_logits_t(x1, mn_ref[...], wr_ref[...])


def _sgu(x2d, a_norm, w_in, b_in, v_norm, w_s, b_s_t, w_out, m_norm, w_router):
    T = x2d.shape[0]
    return pl.pallas_call(
        _sgu_kernel,
        grid=(T // ROW_TILE,),
        in_specs=[
            pl.BlockSpec((ROW_TILE, D_MODEL), lambda i: (i, 0)),
            _const_spec((1, D_MODEL)),
            _const_spec((D_MODEL, 2 * SGU_WIDTH)),
            _const_spec((1, 2 * SGU_WIDTH)),
            _const_spec((1, SGU_WIDTH)),
            _const_spec((SGU_GROUPS, SGU_BLOCK, SGU_BLOCK)),
            _const_spec((SGU_BLOCK, SGU_GROUPS)),
            _const_spec((SGU_WIDTH, D_MODEL)),
            _const_spec((1, D_MODEL)),
            _const_spec((3 * D_MODEL, LANES)),
        ],
        out_specs=[
            pl.BlockSpec((ROW_TILE, D_MODEL), lambda i: (i, 0)),
            pl.BlockSpec((ROUTER_ROWS, ROW_TILE), lambda i: (0, i)),
        ],
        out_shape=[
            jax.ShapeDtypeStruct((T, D_MODEL), f32),
            jax.ShapeDtypeStruct((ROUTER_ROWS, T), f32),
        ],
        scratch_shapes=[
            pltpu.VMEM((ROW_TILE, SGU_WIDTH), f32),
            pltpu.VMEM((ROW_TILE, SGU_WIDTH), bf16),
            pltpu.VMEM((ROW_TILE, SGU_WIDTH), bf16),
        ],
        compiler_params=_params(("parallel",)),
        name="sgu",
    )(x2d, a_norm, w_in, b_in, v_norm, w_s, b_s_t, w_out, m_norm, w_router)


def _route_kernel(lg_ref, be_ref, bg_ref, oi_ref, ow_ref, cnt_ref, carry_scr):
    step = pl.program_id(0)

    @pl.when(step == 0)
    def _():
        carry_scr[...] = jnp.zeros_like(carry_scr)

    lg = lg_ref[...]
    e_l = lg[0:N_EXPERTS, :] + be_ref[...]
    g_l = lg[N_EXPERTS:N_EXPERTS + 8, :] + bg_ref[...]
    g_row = lax.broadcasted_iota(jnp.int32, g_l.shape, 0).astype(f32)
    g_l = jnp.where(g_row < N_GROUPS, g_l, -jnp.inf)
    g_max = jnp.max(g_l, axis=0, keepdims=True)
    g_sel = jnp.min(jnp.where(g_l == g_max, g_row, 8.0), axis=0, keepdims=True)
    g_den = jnp.sum(jnp.exp(g_l - g_max), axis=0, keepdims=True)
    g_w = 1.0 / g_den

    e_row_i = lax.broadcasted_iota(jnp.int32, e_l.shape, 0)
    e_row = e_row_i.astype(f32)
    e_grp = (e_row_i >> GROUP_SHIFT).astype(f32)
    e_in = jnp.where(e_grp == g_sel, e_l, -jnp.inf)
    m1 = jnp.max(e_in, axis=0, keepdims=True)
    i1 = jnp.min(jnp.where(e_in == m1, e_row, float(N_EXPERTS)), axis=0, keepdims=True)
    e_in2 = jnp.where(e_row == i1, -jnp.inf, e_in)
    m2 = jnp.max(e_in2, axis=0, keepdims=True)
    i2 = jnp.min(jnp.where(e_in2 == m2, e_row, float(N_EXPERTS)), axis=0, keepdims=True)
    t = jnp.exp(m2 - m1)
    p1 = 1.0 / (1.0 + t)
    p2 = t / (1.0 + t)

    hit1 = e_row == i1
    hit2 = e_row == i2
    member = jnp.logical_or(hit1, hit2)
    tt = lg.shape[1]
    before = (lax.broadcasted_iota(jnp.int32, (tt, tt), 0)
              < lax.broadcasted_iota(jnp.int32, (tt, tt), 1))
    prefix = _dot(member.astype(bf16), before.astype(bf16))
    rank_full = prefix + carry_scr[...]
    rank1 = jnp.sum(jnp.where(hit1, rank_full, 0.0), axis=0, keepdims=True)
    rank2 = jnp.sum(jnp.where(hit2, rank_full, 0.0), axis=0, keepdims=True)
    carry_new = carry_scr[...] + jnp.sum(member.astype(f32), axis=1, keepdims=True)
    carry_scr[...] = carry_new

    oi_ref[...] = jnp.zeros(oi_ref.shape, jnp.int32)
    oi_ref[0:1, :] = i1.astype(jnp.int32)
    oi_ref[1:2, :] = i2.astype(jnp.int32)
    oi_ref[2:3, :] = rank1.astype(jnp.int32)
    oi_ref[3:4, :] = rank2.astype(jnp.int32)
    ow_ref[...] = jnp.zeros(ow_ref.shape, f32)
    ow_ref[0:1, :] = g_w * p1
    ow_ref[1:2, :] = g_w * p2
    cnt_ref[...] = jnp.broadcast_to(carry_new, cnt_ref.shape)


def _route(logits_t, b_expert, b_group):
    T = logits_t.shape[1]
    return pl.pallas_call(
        _route_kernel,
        grid=(T // ROUTE_TILE,),
        in_specs=[
            pl.BlockSpec((ROUTER_ROWS, ROUTE_TILE), lambda i: (0, i)),
            _const_spec((N_EXPERTS, 1)),
            _const_spec((8, 1)),
        ],
        out_specs=[
            pl.BlockSpec((8, ROUTE_TILE), lambda i: (0, i)),
            pl.BlockSpec((8, ROUTE_TILE), lambda i: (0, i)),
            _const_spec((N_EXPERTS, LANES)),
        ],
        out_shape=[
            jax.ShapeDtypeStruct((8, T), jnp.int32),
            jax.ShapeDtypeStruct((8, T), f32),
            jax.ShapeDtypeStruct((N_EXPERTS, LANES), f32),
        ],
        scratch_shapes=[pltpu.VMEM((N_EXPERTS, 1), f32)],
        compiler_params=_params(("arbitrary",)),
        name="route",
    )(logits_t, b_expert, b_group)


def _row_copy(src, s, dst, d, sem):
    return pltpu.make_async_copy(src.at[s], dst.at[d], sem)


def _wait_rows(buf, sem):
    pltpu.make_async_copy(buf, buf, sem).wait()


def _dispatch_kernel(n_blocks, pend_ref, padded_ref, nu_ref, dest_ref, x_ref, xs_hbm,
                     zero_scr, x3_scr, zsem, sem):
    @pl.when(pl.program_id(0) == 0)
    def _():
        zero_scr[...] = jnp.zeros_like(zero_scr)

        def fill(row0):
            return pltpu.make_async_copy(zero_scr, xs_hbm.at[pl.ds(row0, EXPERT_ROWS)], zsem)

        for e in range(N_EXPERTS):
            @pl.when(padded_ref[e] > 0)
            def _():
                fill(pend_ref[e] - EXPERT_ROWS).start()

            @pl.when(nu_ref[0] + e < n_blocks)
            def _():
                fill((nu_ref[0] + e) * EXPERT_ROWS).start()

        for e in range(N_EXPERTS):
            @pl.when(padded_ref[e] > 0)
            def _():
                fill(0).wait()

            @pl.when(nu_ref[0] + e < n_blocks)
            def _():
                fill(0).wait()

    i = pl.program_id(0)
    slot = i % 2
    stage = x3_scr.at[slot]
    stage[...] = x_ref[...].reshape(stage.shape)

    def start(j, c):
        for k in range(TOP_K):
            _row_copy(stage, j, xs_hbm, dest_ref[k, j], sem.at[slot]).start(priority=k)
        return c

    lax.fori_loop(0, MOVE_TILE, start, 0, unroll=ISSUE_UNROLL)

    @pl.when(i > 0)
    def _():
        for k in range(TOP_K):
            _wait_rows(x3_scr.at[1 - slot], sem.at[1 - slot])

    @pl.when(i == pl.num_programs(0) - 1)
    def _():
        for k in range(TOP_K):
            _wait_rows(stage, sem.at[slot])


def _dispatch(pend, padded, n_used, dest_tiles, x2d, n_rows):
    T = x2d.shape[0]
    n_blocks = n_rows // EXPERT_ROWS
    grid_spec = pltpu.PrefetchScalarGridSpec(
        num_scalar_prefetch=3,
        grid=(T // MOVE_TILE,),
        in_specs=[
            pl.BlockSpec((None, TOP_K, MOVE_TILE), lambda i, *_: (i, 0, 0),
                         memory_space=pltpu.SMEM),
            pl.BlockSpec((MOVE_TILE, D_MODEL), lambda i, *_: (i, 0)),
        ],
        out_specs=pl.BlockSpec(memory_space=pl.ANY),
        scratch_shapes=[
            pltpu.VMEM((EXPERT_ROWS, ROW_SUBLANES, LANES), f32),
            pltpu.VMEM((2, MOVE_TILE, ROW_SUBLANES, LANES), f32),
            pltpu.SemaphoreType.DMA(()),
            pltpu.SemaphoreType.DMA((2,)),
        ],
    )
    return pl.pallas_call(
        functools.partial(_dispatch_kernel, n_blocks),
        grid_spec=grid_spec,
        out_shape=jax.ShapeDtypeStruct((n_rows, ROW_SUBLANES, LANES), f32),
        compiler_params=_params(("arbitrary",)),
        name="dispatch",
    )(pend, padded, n_used, dest_tiles, x2d)


def _expert_kernel(n_blocks, ps_ref, nb_ref, nu_ref, xs_hbm, mn_ref, w1_ref, w3_ref, w2_ref,
                   ys_hbm, xbuf, ybuf, w1_scr, w3_scr, w2_scr, xsem, ysem):
    e = pl.program_id(0)
    nblk = nb_ref[e]
    row0 = ps_ref[e]

    def rows(i):
        return pl.ds(pl.multiple_of(row0 + i * EXPERT_ROWS, EXPERT_ROWS), EXPERT_ROWS)

    def x_copy(i, slot):
        return pltpu.make_async_copy(xs_hbm.at[rows(i)], xbuf.at[slot], xsem.at[slot])

    def y_copy(i, slot):
        return pltpu.make_async_copy(ybuf.at[slot], ys_hbm.at[rows(i)], ysem.at[slot])

    def normed(slot):
        x = xbuf[slot].reshape(EXPERT_ROWS, D_MODEL)
        return _rms(x, mn_ref[...]).astype(bf16)

    @pl.when(e == 0)
    def _():
        xbuf[1] = jnp.zeros(xbuf.shape[1:], f32)

    @pl.when(nblk > 0)
    def _():
        x_copy(0, 0).start()

        @pl.when(nblk > 1)
        def _():
            x_copy(1, 1).start()

        w1_scr[...] = w1_ref[...].astype(bf16)
        w3_scr[...] = w3_ref[...].astype(bf16)
        w2_scr[...] = w2_ref[...].astype(bf16)
        x_copy(0, 0).wait()

        def body(i, h):
            slot = i % 2
            nxt = 1 - slot

            @pl.when(i + 2 < nblk)
            def _():
                x_copy(i + 2, slot).start()

            @pl.when(i + 1 < nblk)
            def _():
                x_copy(i + 1, nxt).wait()

            @pl.when(i >= 2)
            def _():
                y_copy(i - 2, slot).wait()

            h_next = normed(nxt)
            a = _dot(h, w1_scr[...])
            c = _dot(h, w3_scr[...])
            hid = (a * jax.nn.sigmoid(a)) * c
            ybuf[slot] = _dot(hid.astype(bf16), w2_scr[...]).reshape(ybuf.shape[1:])
            y_copy(i, slot).start()
            return h_next

        lax.fori_loop(0, nblk, body, normed(0))

        @pl.when(nblk > 1)
        def _():
            y_copy(nblk - 2, nblk % 2).wait()

        y_copy(nblk - 1, (nblk - 1) % 2).wait()

    @pl.when(e == pl.num_programs(0) - 1)
    def _():
        ybuf[0] = jnp.zeros(ybuf.shape[1:], f32)

        def fill(b, c):
            r = pl.ds(pl.multiple_of(b * EXPERT_ROWS, EXPERT_ROWS), EXPERT_ROWS)
            cp = pltpu.make_async_copy(ybuf.at[0], ys_hbm.at[r], ysem.at[0])
            cp.start()
            cp.wait()
            return c

        lax.fori_loop(nu_ref[0], n_blocks, fill, 0)


def _experts(layer, pstart, seg_blocks, n_used, xs, m_norm, w1, w3, w2):
    n_rows = xs.shape[0]
    n_blocks = n_rows // EXPERT_ROWS

    def w_map(e, *_):
        return (layer, e, 0, 0)

    block_buf = pltpu.VMEM((2, EXPERT_ROWS, ROW_SUBLANES, LANES), f32)
    grid_spec = pltpu.PrefetchScalarGridSpec(
        num_scalar_prefetch=3,
        grid=(N_EXPERTS,),
        in_specs=[
            pl.BlockSpec(memory_space=pl.ANY),
            pl.BlockSpec((1, D_MODEL), lambda e, *_: (0, 0)),
            pl.BlockSpec((None, None, D_MODEL, D_EXPERT), w_map),
            pl.BlockSpec((None, None, D_MODEL, D_EXPERT), w_map),
            pl.BlockSpec((None, None, D_EXPERT, D_MODEL), w_map),
        ],
        out_specs=pl.BlockSpec(memory_space=pl.ANY),
        scratch_shapes=[
            block_buf,
            block_buf,
            pltpu.VMEM((D_MODEL, D_EXPERT), bf16),
            pltpu.VMEM((D_MODEL, D_EXPERT), bf16),
            pltpu.VMEM((D_EXPERT, D_MODEL), bf16),
            pltpu.SemaphoreType.DMA((2,)),
            pltpu.SemaphoreType.DMA((2,)),
        ],
    )
    return pl.pallas_call(
        functools.partial(_expert_kernel, n_blocks),
        grid_spec=grid_spec,
        out_shape=jax.ShapeDtypeStruct((n_rows, ROW_SUBLANES, LANES), f32),
        compiler_params=_params(("arbitrary",)),
        name="experts",
    )(pstart, seg_blocks, n_used, xs, m_norm, w1, w3, w2)


def _combine_kernel(dest_ref, next_ref, x_ref, w_ref, ys_hbm, out_ref, buf, sem):
    i = pl.program_id(0)
    slot = i % 2

    def gather(idx_ref, s):
        def start(j, c):
            for k in range(TOP_K):
                _row_copy(ys_hbm, idx_ref[k, j], buf.at[s, k], j, sem.at[s]).start(priority=k)
            return c

        lax.fori_loop(0, MOVE_TILE, start, 0, unroll=ISSUE_UNROLL)

    @pl.when(i == 0)
    def _():
        gather(dest_ref, slot)

    @pl.when(i + 1 < pl.num_programs(0))
    def _():
        gather(next_ref, 1 - slot)

    for k in range(TOP_K):
        _wait_rows(buf.at[slot, k], sem.at[slot])
    w = w_ref[...]
    y0 = buf[slot, 0].reshape(MOVE_TILE, D_MODEL)
    y1 = buf[slot, 1].reshape(MOVE_TILE, D_MODEL)
    out_ref[...] = x_ref[...] + (w[:, 0:1] * y0 + w[:, 1:2] * y1)


def _combine(dest_tiles, x2d, w_cols, ys):
    T = x2d.shape[0]
    n_tiles = T // MOVE_TILE
    return pl.pallas_call(
        _combine_kernel,
        grid=(n_tiles,),
        in_specs=[
            pl.BlockSpec((None, TOP_K, MOVE_TILE), lambda i: (i, 0, 0),
                         memory_space=pltpu.SMEM),
            pl.BlockSpec((None, TOP_K, MOVE_TILE),
                         lambda i: (jnp.minimum(i + 1, n_tiles - 1), 0, 0),
                         memory_space=pltpu.SMEM),
            pl.BlockSpec((MOVE_TILE, D_MODEL), lambda i: (i, 0)),
            pl.BlockSpec((MOVE_TILE, 8), lambda i: (i, 0)),
            pl.BlockSpec(memory_space=pl.ANY),
        ],
        out_specs=pl.BlockSpec((MOVE_TILE, D_MODEL), lambda i: (i, 0)),
        out_shape=jax.ShapeDtypeStruct((T, D_MODEL), f32),
        scratch_shapes=[
            pltpu.VMEM((2, TOP_K, MOVE_TILE, ROW_SUBLANES, LANES), f32),
            pltpu.SemaphoreType.DMA((2,)),
        ],
        compiler_params=_params(("arbitrary",)),
        name="combine",
    )(dest_tiles, dest_tiles, x2d, w_cols, ys)


def _moe(layer, x2d, logits_t, m_norm, b_group, b_expert, w1, w3, w2):
    T = x2d.shape[0]
    n_blocks = (T * TOP_K) // EXPERT_ROWS + N_EXPERTS
    n_rows = n_blocks * EXPERT_ROWS

    be = b_expert.reshape(N_EXPERTS, 1)
    bg = jnp.concatenate([b_group, jnp.zeros((8 - N_GROUPS,), f32)]).reshape(8, 1)
    oi, ow, cnt = _route(logits_t, be, bg)

    counts = cnt[:, 0].astype(jnp.int32)
    padded = (counts + EXPERT_ROWS - 1) // EXPERT_ROWS * EXPERT_ROWS
    pend = jnp.cumsum(padded)
    pstart = pend - padded
    n_used = (pend[-1] // EXPERT_ROWS).astype(jnp.int32).reshape(1)
    e_ids = jnp.arange(N_EXPERTS, dtype=jnp.int32)[:, None, None]
    seg_start = jnp.sum(jnp.where(oi[None, 0:2] == e_ids, pstart[:, None, None], 0), axis=0)
    dest = seg_start + oi[2:4]
    dest_tiles = dest.reshape(TOP_K, T // MOVE_TILE, MOVE_TILE).transpose(1, 0, 2)
    w_cols = ow.T

    xs = _dispatch(pend.astype(jnp.int32), padded, n_used, dest_tiles, x2d, n_rows)
    ys = _experts(layer, pstart.astype(jnp.int32), padded // EXPERT_ROWS, n_used, xs, m_norm,
                  w1, w3, w2)
    return _combine(dest_tiles, x2d, w_cols, ys)


def _log_sigmoid(x):
    return jnp.minimum(x, 0.0) - jnp.log1p(jnp.exp(-jnp.abs(x)))


def _proj_kernel(tiles_per_seq, x_ref, kvn_ref, bn_ref, wkt_ref, wv_ref, wf_ref, bf_ref,
                 wq_ref, wg_ref, gq_ref, gk_ref, q_ref, kt_ref, v_ref, sg_ref, carry_scr):
    i = pl.program_id(0)
    x = x_ref[...]
    hkv = _rms(x, kvn_ref[...]).astype(bf16)
    hq = _rms(x, bn_ref[...]).astype(bf16)

    v_ref[...] = _dot(hkv, wv_ref[...]).astype(bf16)
    sg_ref[...] = jax.nn.sigmoid(_dot(hq, wg_ref[...])).astype(bf16)

    logf = _log_sigmoid(_dot(hkv, wf_ref[...]) + bf_ref[...])
    tm = x.shape[0]
    incl = (lax.broadcasted_iota(jnp.int32, (tm, tm), 0)
            >= lax.broadcasted_iota(jnp.int32, (tm, tm), 1)).astype(f32)
    @pl.when(i % tiles_per_seq == 0)
    def _():
        carry_scr[...] = jnp.zeros_like(carry_scr)

    cum = _dot(incl, logf, precision=lax.Precision.HIGHEST) + carry_scr[...]
    carry_scr[...] = cum[tm - 1:tm, :]
    cum = cum * LOG2E
    cum_t = cum.T

    lane = lax.broadcasted_iota(jnp.int32, (tm, HEAD_PAD), 1)
    sub = lax.broadcasted_iota(jnp.int32, (HEAD_PAD, tm), 0)
    q_raw = _dot(hq, wq_ref[...])
    k_raw_t = _dot_nt(wkt_ref[...], hkv)
    scale = HEAD_DIM ** -0.5 * LOG2E
    for h in range(N_HEADS):
        sl = slice(h * HEAD_PAD, (h + 1) * HEAD_PAD)
        qb = q_raw[:, sl]
        q_ms = jnp.sum(qb * qb, axis=-1, keepdims=True) * (1.0 / HEAD_DIM)
        qn = qb * lax.rsqrt(q_ms + EPS) * gq_ref[...] * scale
        c_hi, c_mid, c_lo = _split3(cum[:, h:h + 1])
        qa = jnp.where(lane < HEAD_DIM, qn,
             jnp.where(lane < AUX_CUM0, 1.0,
             jnp.where(lane == AUX_CUM0, c_hi,
             jnp.where(lane == AUX_CUM0 + 1, c_mid,
             jnp.where(lane == AUX_CUM0 + 2, c_lo, 0.0)))))
        q_ref[:, sl] = qa.astype(bf16)

        kb = k_raw_t[sl, :]
        k_ms = jnp.sum(kb * kb, axis=0, keepdims=True) * (1.0 / HEAD_DIM)
        kn = kb * lax.rsqrt(k_ms + EPS) * gk_ref[...]
        t_hi, t_mid, t_lo = _split3(cum_t[h:h + 1, :])
        ka = jnp.where(sub < HEAD_DIM, kn,
             jnp.where(sub == AUX_ONE0, -t_hi,
             jnp.where(sub == AUX_ONE0 + 1, -t_mid,
             jnp.where(sub == AUX_ONE0 + 2, -t_lo,
             jnp.where(sub < AUX_CUM0 + 3, 1.0, 0.0)))))
        kt_ref[0, sl, :] = ka.astype(bf16)


def _proj(x2d, seq, kv_norm, b_norm, wkt, wv, wf, bfv, wq, wg, gq, gk):
    T = x2d.shape[0]
    n_tiles = T // ROW_TILE
    qw = N_HEADS * HEAD_PAD
    return pl.pallas_call(
        functools.partial(_proj_kernel, seq // ROW_TILE),
        grid=(n_tiles,),
        in_specs=[
            pl.BlockSpec((ROW_TILE, D_MODEL), lambda i: (i, 0)),
            _const_spec((1, D_MODEL)),
            _const_spec((1, D_MODEL)),
            _const_spec((qw, D_MODEL)),
            _const_spec((D_MODEL, D_MODEL)),
            _const_spec((D_MODEL, LANES)),
            _const_spec((1, LANES)),
            _const_spec((D_MODEL, qw)),
            _const_spec((D_MODEL, D_MODEL)),
            _const_spec((1, HEAD_PAD)),
            _const_spec((HEAD_PAD, 1)),
        ],
        out_specs=[
            pl.BlockSpec((ROW_TILE, qw), lambda i: (i, 0)),
            pl.BlockSpec((1, qw, ROW_TILE), lambda i: (i, 0, 0)),
            pl.BlockSpec((ROW_TILE, D_MODEL), lambda i: (i, 0)),
            pl.BlockSpec((ROW_TILE, D_MODEL), lambda i: (i, 0)),
        ],
        out_shape=[
            jax.ShapeDtypeStruct((T, qw), bf16),
            jax.ShapeDtypeStruct((n_tiles, qw, ROW_TILE), bf16),
            jax.ShapeDtypeStruct((T, D_MODEL), bf16),
            jax.ShapeDtypeStruct((T, D_MODEL), bf16),
        ],
        scratch_shapes=[pltpu.VMEM((1, LANES), f32)],
        compiler_params=_params(("arbitrary",)),
        name="proj",
    )(x2d, kv_norm, b_norm, wkt, wv, wf, bfv, wq, wg, gq, gk)


def _attn_kernel(q_ref, kt_ref, v_ref, o_ref, s_scr, m_scr, l_scr, acc_scr):
    qi = pl.program_id(2)
    tiles = ATT_Q // ATT_K
    groups = ATT_K // LANES
    heads = [slice(hh * HEAD_PAD, (hh + 1) * HEAD_PAD) for hh in range(2)]
    m_scr[...] = jnp.full(m_scr.shape, -jnp.inf, f32)

    def lane_max(mx, s):
        for g in range(groups):
            mx = jnp.maximum(mx, s[:, g * LANES:(g + 1) * LANES])
        return mx

    def probs(s, mb):
        ps = [jnp.exp2(s[:, g * LANES:(g + 1) * LANES] - mb) for g in range(groups)]
        return ps, functools.reduce(lambda a, b: a + b, ps)

    def body_a(trip, c):
        for t in range(tiles):
            kt_idx = trip * tiles + t
            for hh in range(2):
                s = _dot(q_ref[:, heads[hh]], kt_ref[kt_idx, heads[hh], :])
                s_scr[hh, kt_idx] = s
                m_scr[hh] = lane_max(m_scr[hh], s)
        return c

    lax.fori_loop(0, qi, body_a, 0)
    for t in range(tiles):
        kt_idx = qi * tiles + t
        rows = ATT_Q - t * ATT_K
        visible = (lax.broadcasted_iota(jnp.int32, (rows, ATT_K), 1)
                   <= lax.broadcasted_iota(jnp.int32, (rows, ATT_K), 0))
        for hh in range(2):
            s = _dot(q_ref[t * ATT_K:, heads[hh]], kt_ref[kt_idx, heads[hh], :])
            s = jnp.where(visible, s, -jnp.inf)
            s_scr[hh, kt_idx, t * ATT_K:, :] = s
            m_scr[hh, t * ATT_K:, :] = lane_max(m_scr[hh, t * ATT_K:, :], s)

    for hh in range(2):
        row_max = jnp.max(m_scr[hh], axis=-1, keepdims=True)
        m_scr[hh] = jnp.broadcast_to(row_max, (ATT_Q, LANES))
    l_scr[...] = jnp.zeros(l_scr.shape, f32)
    acc_scr[...] = jnp.zeros(acc_scr.shape, f32)

    def body_b(trip, c):
        for hh in range(2):
            mb = m_scr[hh]
            lsum, acc = l_scr[hh], acc_scr[hh]
            for t0 in range(0, tiles, PV_TILES):
                kt_idx = trip * tiles + t0
                row0 = pl.multiple_of(kt_idx * ATT_K, PV_TILES * ATT_K)
                vb = v_ref[pl.ds(row0, PV_TILES * ATT_K), :]
                parts = []
                for t in range(PV_TILES):
                    ps, psum = probs(s_scr[hh, kt_idx + t], mb)
                    parts += ps
                    lsum = lsum + psum
                acc = acc + _dot(jnp.concatenate(parts, axis=1).astype(bf16), vb)
            l_scr[hh] = lsum
            acc_scr[hh] = acc
        return c

    lax.fori_loop(0, qi, body_b, 0)
    for t in range(tiles):
        kt_idx = qi * tiles + t
        row0 = pl.multiple_of(kt_idx * ATT_K, ATT_K)
        vb = v_ref[pl.ds(row0, ATT_K), :]
        for hh in range(2):
            ps, psum = probs(s_scr[hh, kt_idx, t * ATT_K:, :], m_scr[hh, t * ATT_K:, :])
            l_scr[hh, t * ATT_K:, :] += psum
            acc_scr[hh, t * ATT_K:, :] += _dot(jnp.concatenate(ps, axis=1).astype(bf16), vb)

    lane = lax.broadcasted_iota(jnp.int32, (ATT_Q, 2 * HEAD_DIM), 1)
    o0 = acc_scr[0] / jnp.sum(l_scr[0], axis=-1, keepdims=True)
    o1 = acc_scr[1] / jnp.sum(l_scr[1], axis=-1, keepdims=True)
    o_ref[...] = jnp.where(lane < HEAD_DIM, o0, o1).astype(bf16)


def _attention(q, kt, v, batch, seq):
    T = q.shape[0]
    nq = seq // ATT_Q
    nk = seq // ATT_K
    pairs = N_HEADS // 2
    return pl.pallas_call(
        _attn_kernel,
        grid=(batch, pairs, nq),
        in_specs=[
            pl.BlockSpec((ATT_Q, 2 * HEAD_PAD), lambda b, p, i: (b * nq + i, p)),
            pl.BlockSpec((nk, 2 * HEAD_PAD, ATT_K), lambda b, p, i: (b, p, 0)),
            pl.BlockSpec((seq, 2 * HEAD_DIM), lambda b, p, i: (b, p)),
        ],
        out_specs=pl.BlockSpec((ATT_Q, 2 * HEAD_DIM), lambda b, p, i: (b * nq + i, p)),
        out_shape=jax.ShapeDtypeStruct((T, D_MODEL), bf16),
        scratch_shapes=[
            pltpu.VMEM((2, nk, ATT_Q, ATT_K), f32),
            pltpu.VMEM((2, ATT_Q, LANES), f32),
            pltpu.VMEM((2, ATT_Q, LANES), f32),
            pltpu.VMEM((2, ATT_Q, 2 * HEAD_DIM), f32),
        ],
        compiler_params=_params(("parallel", "parallel", "parallel")),
        name="attn",
    )(q, kt, v)


def _attn_out_kernel(x_ref, o_ref, sg_ref, wo_ref, mn_ref, wr_ref, x3_ref, lg_ref):
    gated = (o_ref[...].astype(f32) * sg_ref[...].astype(f32)).astype(bf16)
    x3 = x_ref[...] + _dot(gated, wo_ref[...])
    x3_ref[...] = x3
    lg_ref[...] = _router_logits_t(x3, mn_ref[...], wr_ref[...])


def _attn_out(x2d, o, sg, wo, m_norm, w_router):
    T = x2d.shape[0]
    return pl.pallas_call(
        _attn_out_kernel,
        grid=(T // ROW_TILE,),
        in_specs=[
            pl.BlockSpec((ROW_TILE, D_MODEL), lambda i: (i, 0)),
            pl.BlockSpec((ROW_TILE, D_MODEL), lambda i: (i, 0)),
            pl.BlockSpec((ROW_TILE, D_MODEL), lambda i: (i, 0)),
            _const_spec((D_MODEL, D_MODEL)),
            _const_spec((1, D_MODEL)),
            _const_spec((3 * D_MODEL, LANES)),
        ],
        out_specs=[
            pl.BlockSpec((ROW_TILE, D_MODEL), lambda i: (i, 0)),
            pl.BlockSpec((ROUTER_ROWS, ROW_TILE), lambda i: (0, i)),
        ],
        out_shape=[
            jax.ShapeDtypeStruct((T, D_MODEL), f32),
            jax.ShapeDtypeStruct((ROUTER_ROWS, T), f32),
        ],
        compiler_params=_params(("parallel",)),
        name="attn_out",
    )(x2d, o, sg, wo, m_norm, w_router)


def _router_weight(w_group, w_expert):
    pad = jnp.zeros((D_MODEL, LANES - N_EXPERTS - N_GROUPS), f32)
    w = jnp.concatenate([w_expert, w_group, pad], axis=1)
    w_hi = w.astype(bf16)
    w_lo = (w - w_hi.astype(f32)).astype(bf16)
    return jnp.concatenate([w_hi, w_lo, w_hi], axis=0)


def _pad_heads(w):
    w = w.reshape(D_MODEL, N_HEADS, HEAD_DIM)
    w = jnp.pad(w, ((0, 0), (0, 0), (0, HEAD_PAD - HEAD_DIM)))
    return w.reshape(D_MODEL, N_HEADS * HEAD_PAD)


def kernel(x, a_norm, a_w_in, a_b_in, a_v_norm, a_w_s, a_b_s, a_w_out, kv_norm, kv_w, kv_b_f,
           k_norm, b_norm, b_w_qg, q_norm, b_w_out, m_norm, m_w_group, m_b_group, m_w_expert,
           m_b_expert, m_w1, m_w3, m_w2):
    batch, seq, _ = x.shape
    T = batch * seq
    x2d = x.reshape(T, D_MODEL)

    x1, lg0 = _sgu(
        x2d, a_norm[0].reshape(1, -1), a_w_in[0].astype(bf16), a_b_in[0].reshape(1, -1),
        a_v_norm[0].reshape(1, -1), a_w_s[0], a_b_s[0].T, a_w_out[0].astype(bf16),
        m_norm[0].reshape(1, -1), _router_weight(m_w_group[0], m_w_expert[0]))
    x2 = _moe(0, x1, lg0, m_norm[0].reshape(1, -1), m_b_group[0], m_b_expert[0],
              m_w1, m_w3, m_w2)

    wk = kv_w[:, :D_MODEL]
    wv = kv_w[:, D_MODEL:2 * D_MODEL]
    wf = jnp.pad(kv_w[:, 2 * D_MODEL:], ((0, 0), (0, LANES - N_HEADS)))
    bfv = jnp.pad(kv_b_f, (0, LANES - N_HEADS)).reshape(1, LANES)
    wq = b_w_qg[0][:, :D_MODEL]
    wg = b_w_qg[0][:, D_MODEL:]
    gq = jnp.pad(q_norm[0], (0, HEAD_PAD - HEAD_DIM)).reshape(1, HEAD_PAD)
    gk = jnp.pad(k_norm, (0, HEAD_PAD - HEAD_DIM)).reshape(HEAD_PAD, 1)
    q, kt, v, sg = _proj(
        x2, seq, kv_norm.reshape(1, -1), b_norm[0].reshape(1, -1),
        _pad_heads(wk).T.astype(bf16), wv.astype(bf16), wf.astype(bf16), bfv,
        _pad_heads(wq).astype(bf16), wg.astype(bf16), gq, gk)
    o = _attention(q, kt, v, batch, seq)
    x3, lg1 = _attn_out(x2, o, sg, b_w_out[0].astype(bf16), m_norm[1].reshape(1, -1),
                        _router_weight(m_w_group[1], m_w_expert[1]))
    x4 = _moe(1, x3, lg1, m_norm[1].reshape(1, -1), m_b_group[1], m_b_expert[1],
              m_w1, m_w3, m_w2)
    return x4.reshape(batch, seq, D_MODEL)
```

```python
import functools
import math

import jax
import jax.numpy as jnp
from jax import lax
from jax.experimental import pallas as pl
from jax.experimental.pallas import tpu as pltpu

D_MODEL = 1024
EPS = 1e-6
SGU_BLOCK = 128
SGU_CHUNK = 64
CHUNK_SHIFT = SGU_CHUNK.bit_length() - 1
SGU_WIDTH = 2 * D_MODEL
SGU_GROUPS = 8
SGU_GROUP_DIM = SGU_WIDTH // SGU_GROUPS
N_HEADS = 16
HEAD_DIM = D_MODEL // N_HEADS
N_GROUPS = 4
EXPERTS_PER_GROUP = 8
N_EXPERTS = N_GROUPS * EXPERTS_PER_GROUP
GROUP_SHIFT = EXPERTS_PER_GROUP.bit_length() - 1
TOP_K = 2
D_EXPERT = D_MODEL // 2

LANES = 128
ROW_SUBLANES = D_MODEL // LANES
ISSUE_UNROLL = 8
VMEM_LIMIT_BYTES = 56 * 1024 * 1024

ROW_TILE = 256
ROUTE_TILE = 512
EXPERT_ROWS = 256
MOVE_TILE = 512
ATT_Q = 1024
ATT_K = ROW_TILE
PV_TILES = 2
HEAD_PAD = 2 * HEAD_DIM
ROUTER_ROWS = 40
AUX_ONE0 = HEAD_DIM
AUX_CUM0 = HEAD_DIM + 3

LOG2E = math.log2(math.e)

bf16 = jnp.bfloat16
f32 = jnp.float32


def _dot(a, b, precision=None):
    return jnp.dot(a, b, preferred_element_type=f32, precision=precision)


def _dot_nt(a, b, precision=None):
    return lax.dot_general(a, b, (((1,), (1,)), ((), ())), preferred_element_type=f32,
                           precision=precision)


def _rms(x, g):
    ms = jnp.mean(x * x, axis=-1, keepdims=True)
    return x * lax.rsqrt(ms + EPS) * g


def _gelu_tanh(z):
    c = math.sqrt(2.0 / math.pi)
    return z * (0.5 * (1.0 + jnp.tanh(c * (z + 0.044715 * (z * z * z)))))


def _split3(c):
    hi = c.astype(bf16).astype(f32)
    r = c - hi
    mid = r.astype(bf16).astype(f32)
    lo = r - mid
    return hi, mid, lo


def _const_spec(shape):
    nd = len(shape)
    return pl.BlockSpec(shape, lambda *_: (0,) * nd)


def _params(sem):
    return pltpu.CompilerParams(dimension_semantics=sem, vmem_limit_bytes=VMEM_LIMIT_BYTES)


def _router_logits_t(x_new, mn, wr3):
    hm = _rms(x_new, mn)
    h_hi = hm.astype(bf16)
    h_lo = (hm - h_hi.astype(f32)).astype(bf16)
    lg = _dot(jnp.concatenate([h_hi, h_hi, h_lo], axis=1), wr3)
    return lg.T[:ROUTER_ROWS, :]


def _sgu_kernel(x_ref, an_ref, win_ref, bin_ref, vn_ref, ws_ref, bst_ref, wout_ref, mn_ref,
                wr_ref, x1_ref, lg_ref, u_scr, v_scr, gated_scr):
    x = x_ref[...]
    h = _rms(x, an_ref[...]).astype(bf16)
    u_scr[...] = _gelu_tanh(_dot(h, win_ref[:, :SGU_WIDTH]) + bin_ref[:, :SGU_WIDTH])
    v = _gelu_tanh(_dot(h, win_ref[:, SGU_WIDTH:]) + bin_ref[:, SGU_WIDTH:])
    v_scr[...] = _rms(v, vn_ref[...]).astype(bf16)

    t_chunk = lax.broadcasted_iota(jnp.int32, (SGU_BLOCK, SGU_BLOCK), 0) >> CHUNK_SHIFT
    s_chunk = lax.broadcasted_iota(jnp.int32, (SGU_BLOCK, SGU_BLOCK), 1) >> CHUNK_SHIFT
    causal = t_chunk >= s_chunk
    for g in range(SGU_GROUPS):
        wsm = jnp.where(causal, ws_ref[g], 0.0).astype(bf16)
        cols = slice(g * SGU_GROUP_DIM, (g + 1) * SGU_GROUP_DIM)
        for sb in range(ROW_TILE // SGU_BLOCK):
            rows = slice(sb * SGU_BLOCK, (sb + 1) * SGU_BLOCK)
            mixed = _dot(wsm, v_scr[rows, cols]) + bst_ref[:, g:g + 1]
            gated_scr[rows, cols] = (u_scr[rows, cols] * mixed).astype(bf16)

    x1 = x + _dot(gated_scr[...], wout_ref[...])
    x1_ref[...] = x1
    lg_ref[...] = _router_logits_t(x1, mn_ref[...], wr_ref[...])


def _sgu(x2d, a_norm, w_in, b_in, v_norm, w_s, b_s_t, w_out, m_norm, w_router):
    T = x2d.shape[0]
    return pl.pallas_call(
        _sgu_kernel,
        grid=(T // ROW_TILE,),
        in_specs=[
            pl.BlockSpec((ROW_TILE, D_MODEL), lambda i: (i, 0)),
            _const_spec((1, D_MODEL)),
            _const_spec((D_MODEL, 2 * SGU_WIDTH)),
            _const_spec((1, 2 * SGU_WIDTH)),
            _const_spec((1, SGU_WIDTH)),
            _const_spec((SGU_GROUPS, SGU_BLOCK, SGU_BLOCK)),
            _const_spec((SGU_BLOCK, SGU_GROUPS)),
            _const_spec((SGU_WIDTH, D_MODEL)),
            _const_spec((1, D_MODEL)),
            _const_spec((3 * D_MODEL, LANES)),
        ],
        out_specs=[
            pl.BlockSpec((ROW_TILE, D_MODEL), lambda i: (i, 0)),
            pl.BlockSpec((ROUTER_ROWS, ROW_TILE), lambda i: (0, i)),
        ],
        out_shape=[
            jax.ShapeDtypeStruct((T, D_MODEL), f32),
            jax.ShapeDtypeStruct((ROUTER_ROWS, T), f32),
        ],
        scratch_shapes=[
            pltpu.VMEM((ROW_TILE, SGU_WIDTH), f32),
            pltpu.VMEM((ROW_TILE, SGU_WIDTH), bf16),
            pltpu.VMEM((ROW_TILE, SGU_WIDTH), bf16),
        ],
        compiler_params=_params(("parallel",)),
        name="sgu",
    )(x2d, a_norm, w_in, b_in, v_norm, w_s, b_s_t, w_out, m_norm, w_router)


def _route_kernel(lg_ref, be_ref, bg_ref, oi_ref, ow_ref, cnt_ref, carry_scr):
    step = pl.program_id(0)

    @pl.when(step == 0)
    def _():
        carry_scr[...] = jnp.zeros_like(carry_scr)

    lg = lg_ref[...]
    e_l = lg[0:N_EXPERTS, :] + be_ref[...]
    g_l = lg[N_EXPERTS:N_EXPERTS + 8, :] + bg_ref[...]
    g_row = lax.broadcasted_iota(jnp.int32, g_l.shape, 0).astype(f32)
    g_l = jnp.where(g_row < N_GROUPS, g_l, -jnp.inf)
    g_max = jnp.max(g_l, axis=0, keepdims=True)
    g_sel = jnp.min(jnp.where(g_l == g_max, g_row, 8.0), axis=0, keepdims=True)
    g_den = jnp.sum(jnp.exp(g_l - g_max), axis=0, keepdims=True)
    g_w = 1.0 / g_den

    e_row_i = lax.broadcasted_iota(jnp.int32, e_l.shape, 0)
    e_row = e_row_i.astype(f32)
    e_grp = (e_row_i >> GROUP_SHIFT).astype(f32)
    e_in = jnp.where(e_grp == g_sel, e_l, -jnp.inf)
    m1 = jnp.max(e_in, axis=0, keepdims=True)
    i1 = jnp.min(jnp.where(e_in == m1, e_row, float(N_EXPERTS)), axis=0, keepdims=True)
    e_in2 = jnp.where(e_row == i1, -jnp.inf, e_in)
    m2 = jnp.max(e_in2, axis=0, keepdims=True)
    i2 = jnp.min(jnp.where(e_in2 == m2, e_row, float(N_EXPERTS)), axis=0, keepdims=True)
    t = jnp.exp(m2 - m1)
    p1 = 1.0 / (1.0 + t)
    p2 = t / (1.0 + t)

    hit1 = e_row == i1
    hit2 = e_row == i2
    member = jnp.logical_or(hit1, hit2)
    tt = lg.shape[1]
    before = (lax.broadcasted_iota(jnp.int32, (tt, tt), 0)
              < lax.broadcasted_iota(jnp.int32, (tt, tt), 1))
    prefix = _dot(member.astype(bf16), before.astype(bf16))
    rank_full = prefix + carry_scr[...]
    rank1 = jnp.sum(jnp.where(hit1, rank_full, 0.0), axis=0, keepdims=True)
    rank2 = jnp.sum(jnp.where(hit2, rank_full, 0.0), axis=0, keepdims=True)
    carry_new = carry_scr[...] + jnp.sum(member.astype(f32), axis=1, keepdims=True)
    carry_scr[...] = carry_new

    oi_ref[...] = jnp.zeros(oi_ref.shape, jnp.int32)
    oi_ref[0:1, :] = i1.astype(jnp.int32)
    oi_ref[1:2, :] = i2.astype(jnp.int32)
    oi_ref[2:3, :] = rank1.astype(jnp.int32)
    oi_ref[3:4, :] = rank2.astype(jnp.int32)
    ow_ref[...] = jnp.zeros(ow_ref.shape, f32)
    ow_ref[0:1, :] = g_w * p1
    ow_ref[1:2, :] = g_w * p2
    cnt_ref[...] = jnp.broadcast_to(carry_new, cnt_ref.shape)


def _route(logits_t, b_expert, b_group):
    T = logits_t.shape[1]
    return pl.pallas_call(
        _route_kernel,
        grid=(T // ROUTE_TILE,),
        in_specs=[
            pl.BlockSpec((ROUTER_ROWS, ROUTE_TILE), lambda i: (0, i)),
            _const_spec((N_EXPERTS, 1)),
            _const_spec((8, 1)),
        ],
        out_specs=[
            pl.BlockSpec((8, ROUTE_TILE), lambda i: (0, i)),
            pl.BlockSpec((8, ROUTE_TILE), lambda i: (0, i)),
            _const_spec((N_EXPERTS, LANES)),
        ],
        out_shape=[
            jax.ShapeDtypeStruct((8, T), jnp.int32),
            jax.ShapeDtypeStruct((8, T), f32),
            jax.ShapeDtypeStruct((N_EXPERTS, LANES), f32),
        ],
        scratch_shapes=[pltpu.VMEM((N_EXPERTS, 1), f32)],
        compiler_params=_params(("arbitrary",)),
        name="route",
    )(logits_t, b_expert, b_group)


def _row_copy(src, s, dst, d, sem):
    return pltpu.make_async_copy(src.at[s], dst.at[d], sem)


def _wait_rows(buf, sem):
    pltpu.make_async_copy(buf, buf, sem).wait()


def _dispatch_kernel(n_blocks, pend_ref, padded_ref, nu_ref, dest_ref, x_ref, xs_hbm,
                     zero_scr, x3_scr, zsem, sem):
    @pl.when(pl.program_id(0) == 0)
    def _():
        zero_scr[...] = jnp.zeros_like(zero_scr)

        def fill(row0):
            return pltpu.make_async_copy(zero_scr, xs_hbm.at[pl.ds(row0, EXPERT_ROWS)], zsem)

        for e in range(N_EXPERTS):
            @pl.when(padded_ref[e] > 0)
            def _():
                fill(pend_ref[e] - EXPERT_ROWS).start()

            @pl.when(nu_ref[0] + e < n_blocks)
            def _():
                fill((nu_ref[0] + e) * EXPERT_ROWS).start()

        for e in range(N_EXPERTS):
            @pl.when(padded_ref[e] > 0)
            def _():
                fill(0).wait()

            @pl.when(nu_ref[0] + e < n_blocks)
            def _():
                fill(0).wait()

    i = pl.program_id(0)
    slot = i % 2
    stage = x3_scr.at[slot]
    stage[...] = x_ref[...].reshape(stage.shape)

    def start(j, c):
        for k in range(TOP_K):
            _row_copy(stage, j, xs_hbm, dest_ref[k, j], sem.at[slot]).start(priority=k)
        return c

    lax.fori_loop(0, MOVE_TILE, start, 0, unroll=ISSUE_UNROLL)

    @pl.when(i > 0)
    def _():
        for k in range(TOP_K):
            _wait_rows(x3_scr.at[1 - slot], sem.at[1 - slot])

    @pl.when(i == pl.num_programs(0) - 1)
    def _():
        for k in range(TOP_K):
            _wait_rows(stage, sem.at[slot])


def _dispatch(pend, padded, n_used, dest_tiles, x2d, n_rows):
    T = x2d.shape[0]
    n_blocks = n_rows // EXPERT_ROWS
    grid_spec = pltpu.PrefetchScalarGridSpec(
        num_scalar_prefetch=3,
        grid=(T // MOVE_TILE,),
        in_specs=[
            pl.BlockSpec((None, TOP_K, MOVE_TILE), lambda i, *_: (i, 0, 0),
                         memory_space=pltpu.SMEM),
            pl.BlockSpec((MOVE_TILE, D_MODEL), lambda i, *_: (i, 0)),
        ],
        out_specs=pl.BlockSpec(memory_space=pl.ANY),
        scratch_shapes=[
            pltpu.VMEM((EXPERT_ROWS, ROW_SUBLANES, LANES), f32),
            pltpu.VMEM((2, MOVE_TILE, ROW_SUBLANES, LANES), f32),
            pltpu.SemaphoreType.DMA(()),
            pltpu.SemaphoreType.DMA((2,)),
        ],
    )
    return pl.pallas_call(
        functools.partial(_dispatch_kernel, n_blocks),
        grid_spec=grid_spec,
        out_shape=jax.ShapeDtypeStruct((n_rows, ROW_SUBLANES, LANES), f32),
        compiler_params=_params(("arbitrary",)),
        name="dispatch",
    )(pend, padded, n_used, dest_tiles, x2d)


def _expert_kernel(n_blocks, ps_ref, nb_ref, nu_ref, xs_hbm, mn_ref, w1_ref, w3_ref, w2_ref,
                   ys_hbm, xbuf, ybuf, h_scr, w1_scr, w3_scr, w2_scr, xsem, ysem):
    e = pl.program_id(0)
    nblk = nb_ref[e]
    g0 = ps_ref[e]
    n_used = nu_ref[0]

    def rows(g):
        return pl.ds(pl.multiple_of(g * EXPERT_ROWS, EXPERT_ROWS), EXPERT_ROWS)

    def x_copy(g):
        return pltpu.make_async_copy(xs_hbm.at[rows(g)], xbuf.at[g % 2], xsem.at[g % 2])

    def y_copy(g):
        return pltpu.make_async_copy(ybuf.at[g % 2], ys_hbm.at[rows(g)], ysem.at[g % 2])

    def normed(slot):
        x = xbuf[slot].reshape(EXPERT_ROWS, D_MODEL)
        return _rms(x, mn_ref[...]).astype(bf16)

    @pl.when(e == 0)
    def _():
        x_copy(0).start(priority=1)
        x_copy(1).start(priority=1)
        x_copy(0).wait()
        h_scr[...] = normed(0)

    @pl.when(nblk > 0)
    def _():
        w1_scr[...] = w1_ref[...].astype(bf16)
        w3_scr[...] = w3_ref[...].astype(bf16)
        w2_scr[...] = w2_ref[...].astype(bf16)

        def body(g, h):
            @pl.when(g + 2 < n_used)
            def _():
                x_copy(g + 2).start(priority=1)

            @pl.when(g + 1 < n_used)
            def _():
                x_copy(g + 1).wait()

            @pl.when(g >= 2)
            def _():
                y_copy(g - 2).wait()

            h_next = normed((g + 1) % 2)
            a = _dot(h, w1_scr[...])
            c = _dot(h, w3_scr[...])
            hid = (a * jax.nn.sigmoid(a)) * c
            ybuf[g % 2] = _dot(hid.astype(bf16), w2_scr[...]).reshape(ybuf.shape[1:])
            y_copy(g).start(priority=1)
            return h_next

        h_scr[...] = lax.fori_loop(g0, g0 + nblk, body, h_scr[...])

    @pl.when(e == pl.num_programs(0) - 1)
    def _():
        y_copy(n_used - 2).wait()
        y_copy(n_used - 1).wait()
        ybuf[0] = jnp.zeros(ybuf.shape[1:], f32)

        def fill(b, c):
            r = pl.ds(pl.multiple_of(b * EXPERT_ROWS, EXPERT_ROWS), EXPERT_ROWS)
            cp = pltpu.make_async_copy(ybuf.at[0], ys_hbm.at[r], ysem.at[0])
            cp.start()
            cp.wait()
            return c

        lax.fori_loop(nu_ref[0], n_blocks, fill, 0)


def _experts(layer, pstart, seg_blocks, n_used, xs, m_norm, w1, w3, w2):
    n_rows = xs.shape[0]
    n_blocks = n_rows // EXPERT_ROWS

    def w_map(e, *_):
        return (layer, e, 0, 0)

    block_buf = pltpu.VMEM((2, EXPERT_ROWS, ROW_SUBLANES, LANES), f32)
    grid_spec = pltpu.PrefetchScalarGridSpec(
        num_scalar_prefetch=3,
        grid=(N_EXPERTS,),
        in_specs=[
            pl.BlockSpec(memory_space=pl.ANY),
            pl.BlockSpec((1, D_MODEL), lambda e, *_: (0, 0)),
            pl.BlockSpec((None, None, D_MODEL, D_EXPERT), w_map),
            pl.BlockSpec((None, None, D_MODEL, D_EXPERT), w_map),
            pl.BlockSpec((None, None, D_EXPERT, D_MODEL), w_map),
        ],
        out_specs=pl.BlockSpec(memory_space=pl.ANY),
        scratch_shapes=[
            block_buf,
            block_buf,
            pltpu.VMEM((EXPERT_ROWS, D_MODEL), bf16),
            pltpu.VMEM((D_MODEL, D_EXPERT), bf16),
            pltpu.VMEM((D_MODEL, D_EXPERT), bf16),
            pltpu.VMEM((D_EXPERT, D_MODEL), bf16),
            pltpu.SemaphoreType.DMA((2,)),
            pltpu.SemaphoreType.DMA((2,)),
        ],
    )
    return pl.pallas_call(
        functools.partial(_expert_kernel, n_blocks),
        grid_spec=grid_spec,
        out_shape=jax.ShapeDtypeStruct((n_rows, ROW_SUBLANES, LANES), f32),
        compiler_params=_params(("arbitrary",)),
        name="experts",
    )(pstart, seg_blocks, n_used, xs, m_norm, w1, w3, w2)


def _combine_kernel(dest_ref, next_ref, x_ref, w_ref, ys_hbm, out_ref, buf, sem):
    i = pl.program_id(0)
    slot = i % 2

    def gather(idx_ref, s):
        def start(j, c):
            for k in range(TOP_K):
                _row_copy(ys_hbm, idx_ref[k, j], buf.at[s, k], j, sem.at[s]).start(priority=k)
            return c

        lax.fori_loop(0, MOVE_TILE, start, 0, unroll=ISSUE_UNROLL)

    @pl.when(i == 0)
    def _():
        gather(dest_ref, slot)

    @pl.when(i + 1 < pl.num_programs(0))
    def _():
        gather(next_ref, 1 - slot)

    for k in range(TOP_K):
        _wait_rows(buf.at[slot, k], sem.at[slot])
    w = w_ref[...]
    y0 = buf[slot, 0].reshape(MOVE_TILE, D_MODEL)
    y1 = buf[slot, 1].reshape(MOVE_TILE, D_MODEL)
    out_ref[...] = x_ref[...] + (w[:, 0:1] * y0 + w[:, 1:2] * y1)


def _combine(dest_tiles, x2d, w_cols, ys):
    T = x2d.shape[0]
    n_tiles = T // MOVE_TILE
    return pl.pallas_call(
        _combine_kernel,
        grid=(n_tiles,),
        in_specs=[
            pl.BlockSpec((None, TOP_K, MOVE_TILE), lambda i: (i, 0, 0),
                         memory_space=pltpu.SMEM),
            pl.BlockSpec((None, TOP_K, MOVE_TILE),
                         lambda i: (jnp.minimum(i + 1, n_tiles - 1), 0, 0),
                         memory_space=pltpu.SMEM),
            pl.BlockSpec((MOVE_TILE, D_MODEL), lambda i: (i, 0)),
            pl.BlockSpec((MOVE_TILE, 8), lambda i: (i, 0)),
            pl.BlockSpec(memory_space=pl.ANY),
        ],
        out_specs=pl.BlockSpec((MOVE_TILE, D_MODEL), lambda i: (i, 0)),
        out_shape=jax.ShapeDtypeStruct((T, D_MODEL), f32),
        scratch_shapes=[
            pltpu.VMEM((2, TOP_K, MOVE_TILE, ROW_SUBLANES, LANES), f32),
            pltpu.SemaphoreType.DMA((2,)),
        ],
        compiler_params=_params(("arbitrary",)),
        name="combine",
    )(dest_tiles, dest_tiles, x2d, w_cols, ys)


def _moe(layer, x2d, logits_t, m_norm, b_group, b_expert, w1, w3, w2):
    T = x2d.shape[0]
    n_blocks = (T * TOP_K) // EXPERT_ROWS + N_EXPERTS
    n_rows = n_blocks * EXPERT_ROWS

    be = b_expert.reshape(N_EXPERTS, 1)
    bg = jnp.concatenate([b_group, jnp.zeros((8 - N_GROUPS,), f32)]).reshape(8, 1)
    oi, ow, cnt = _route(logits_t, be, bg)

    counts = cnt[:, 0].astype(jnp.int32)
    padded = (counts + EXPERT_ROWS - 1) // EXPERT_ROWS * EXPERT_ROWS
    pend = jnp.cumsum(padded)
    pstart = pend - padded
    n_used = (pend[-1] // EXPERT_ROWS).astype(jnp.int32).reshape(1)
    e_ids = jnp.arange(N_EXPERTS, dtype=jnp.int32)[:, None, None]
    seg_start = jnp.sum(jnp.where(oi[None, 0:2] == e_ids, pstart[:, None, None], 0), axis=0)
    dest = seg_start + oi[2:4]
    dest_tiles = dest.reshape(TOP_K, T // MOVE_TILE, MOVE_TILE).transpose(1, 0, 2)
    w_cols = ow.T

    xs = _dispatch(pend.astype(jnp.int32), padded, n_used, dest_tiles, x2d, n_rows)
    ys = _experts(layer, (pstart // EXPERT_ROWS).astype(jnp.int32), padded // EXPERT_ROWS,
                  n_used, xs, m_norm, w1, w3, w2)
    return _combine(dest_tiles, x2d, w_cols, ys)


def _log_sigmoid(x):
    return jnp.minimum(x, 0.0) - jnp.log1p(jnp.exp(-jnp.abs(x)))


def _proj_kernel(tiles_per_seq, x_ref, kvn_ref, bn_ref, wkt_ref, wv_ref, wf_ref, bf_ref,
                 wq_ref, wg_ref, gq_ref, gk_ref, q_ref, kt_ref, v_ref, sg_ref, carry_scr):
    i = pl.program_id(0)
    x = x_ref[...]
    hkv = _rms(x, kvn_ref[...]).astype(bf16)
    hq = _rms(x, bn_ref[...]).astype(bf16)

    v_ref[...] = _dot(hkv, wv_ref[...]).astype(bf16)
    sg_ref[...] = jax.nn.sigmoid(_dot(hq, wg_ref[...])).astype(bf16)

    logf = _log_sigmoid(_dot(hkv, wf_ref[...]) + bf_ref[...])
    tm = x.shape[0]
    incl = (lax.broadcasted_iota(jnp.int32, (tm, tm), 0)
            >= lax.broadcasted_iota(jnp.int32, (tm, tm), 1)).astype(f32)
    @pl.when(i % tiles_per_seq == 0)
    def _():
        carry_scr[...] = jnp.zeros_like(carry_scr)

    cum = _dot(incl, logf, precision=lax.Precision.HIGHEST) + carry_scr[...]
    carry_scr[...] = cum[tm - 1:tm, :]
    cum = cum * LOG2E
    cum_t = cum.T

    lane = lax.broadcasted_iota(jnp.int32, (tm, HEAD_PAD), 1)
    sub = lax.broadcasted_iota(jnp.int32, (HEAD_PAD, tm), 0)
    q_raw = _dot(hq, wq_ref[...])
    k_raw_t = _dot_nt(wkt_ref[...], hkv)
    scale = HEAD_DIM ** -0.5 * LOG2E
    for h in range(N_HEADS):
        sl = slice(h * HEAD_PAD, (h + 1) * HEAD_PAD)
        qb = q_raw[:, sl]
        q_ms = jnp.sum(qb * qb, axis=-1, keepdims=True) * (1.0 / HEAD_DIM)
        qn = qb * lax.rsqrt(q_ms + EPS) * gq_ref[...] * scale
        c_hi, c_mid, c_lo = _split3(cum[:, h:h + 1])
        qa = jnp.where(lane < HEAD_DIM, qn,
             jnp.where(lane < AUX_CUM0, 1.0,
             jnp.where(lane == AUX_CUM0, c_hi,
             jnp.where(lane == AUX_CUM0 + 1, c_mid,
             jnp.where(lane == AUX_CUM0 + 2, c_lo, 0.0)))))
        q_ref[:, sl] = qa.astype(bf16)

        kb = k_raw_t[sl, :]
        k_ms = jnp.sum(kb * kb, axis=0, keepdims=True) * (1.0 / HEAD_DIM)
        kn = kb * lax.rsqrt(k_ms + EPS) * gk_ref[...]
        t_hi, t_mid, t_lo = _split3(cum_t[h:h + 1, :])
        ka = jnp.where(sub < HEAD_DIM, kn,
             jnp.where(sub == AUX_ONE0, -t_hi,
             jnp.where(sub == AUX_ONE0 + 1, -t_mid,
             jnp.where(sub == AUX_ONE0 + 2, -t_lo,
             jnp.where(sub < AUX_CUM0 + 3, 1.0, 0.0)))))
        kt_ref[0, sl, :] = ka.astype(bf16)


def _proj(x2d, seq, kv_norm, b_norm, wkt, wv, wf, bfv, wq, wg, gq, gk):
    T = x2d.shape[0]
    n_tiles = T // ROW_TILE
    qw = N_HEADS * HEAD_PAD
    return pl.pallas_call(
        functools.partial(_proj_kernel, seq // ROW_TILE),
        grid=(n_tiles,),
        in_specs=[
            pl.BlockSpec((ROW_TILE, D_MODEL), lambda i: (i, 0)),
            _const_spec((1, D_MODEL)),
            _const_spec((1, D_MODEL)),
            _const_spec((qw, D_MODEL)),
            _const_spec((D_MODEL, D_MODEL)),
            _const_spec((D_MODEL, LANES)),
            _const_spec((1, LANES)),
            _const_spec((D_MODEL, qw)),
            _const_spec((D_MODEL, D_MODEL)),
            _const_spec((1, HEAD_PAD)),
            _const_spec((HEAD_PAD, 1)),
        ],
        out_specs=[
            pl.BlockSpec((ROW_TILE, qw), lambda i: (i, 0)),
            pl.BlockSpec((1, qw, ROW_TILE), lambda i: (i, 0, 0)),
            pl.BlockSpec((ROW_TILE, D_MODEL), lambda i: (i, 0)),
            pl.BlockSpec((ROW_TILE, D_MODEL), lambda i: (i, 0)),
        ],
        out_shape=[
            jax.ShapeDtypeStruct((T, qw), bf16),
            jax.ShapeDtypeStruct((n_tiles, qw, ROW_TILE), bf16),
            jax.ShapeDtypeStruct((T, D_MODEL), bf16),
            jax.ShapeDtypeStruct((T, D_MODEL), bf16),
        ],
        scratch_shapes=[pltpu.VMEM((1, LANES), f32)],
        compiler_params=_params(("arbitrary",)),
        name="proj",
    )(x2d, kv_norm, b_norm, wkt, wv, wf, bfv, wq, wg, gq, gk)


def _attn_kernel(q_ref, kt_ref, v_ref, o_ref, s_scr, m_scr, l_scr, acc_scr):
    qi = pl.program_id(2)
    tiles = ATT_Q // ATT_K
    groups = ATT_K // LANES
    heads = [slice(hh * HEAD_PAD, (hh + 1) * HEAD_PAD) for hh in range(2)]
    m_scr[...] = jnp.full(m_scr.shape, -jnp.inf, f32)

    def lane_max(mx, s):
        for g in range(groups):
            mx = jnp.maximum(mx, s[:, g * LANES:(g + 1) * LANES])
        return mx

    def probs(s, mb):
        ps = [jnp.exp2(s[:, g * LANES:(g + 1) * LANES] - mb) for g in range(groups)]
        return ps, functools.reduce(lambda a, b: a + b, ps)

    def body_a(trip, c):
        for t in range(tiles):
            kt_idx = trip * tiles + t
            for hh in range(2):
                s = _dot(q_ref[:, heads[hh]], kt_ref[kt_idx, heads[hh], :])
                s_scr[hh, kt_idx] = s
                m_scr[hh] = lane_max(m_scr[hh], s)
        return c

    lax.fori_loop(0, qi, body_a, 0)
    for t in range(tiles):
        kt_idx = qi * tiles + t
        rows = ATT_Q - t * ATT_K
        visible = (lax.broadcasted_iota(jnp.int32, (rows, ATT_K), 1)
                   <= lax.broadcasted_iota(jnp.int32, (rows, ATT_K), 0))
        for hh in range(2):
            s = _dot(q_ref[t * ATT_K:, heads[hh]], kt_ref[kt_idx, heads[hh], :])
            s = jnp.where(visible, s, -jnp.inf)
            s_scr[hh, kt_idx, t * ATT_K:, :] = s
            m_scr[hh, t * ATT_K:, :] = lane_max(m_scr[hh, t * ATT_K:, :], s)

    for hh in range(2):
        row_max = jnp.max(m_scr[hh], axis=-1, keepdims=True)
        m_scr[hh] = jnp.broadcast_to(row_max, (ATT_Q, LANES))
    l_scr[...] = jnp.zeros(l_scr.shape, f32)
    acc_scr[...] = jnp.zeros(acc_scr.shape, f32)

    def body_b(trip, c):
        for hh in range(2):
            mb = m_scr[hh]
            lsum, acc = l_scr[hh], acc_scr[hh]
            for t0 in range(0, tiles, PV_TILES):
                kt_idx = trip * tiles + t0
                row0 = pl.multiple_of(kt_idx * ATT_K, PV_TILES * ATT_K)
                vb = v_ref[pl.ds(row0, PV_TILES * ATT_K), :]
                parts = []
                for t in range(PV_TILES):
                    ps, psum = probs(s_scr[hh, kt_idx + t], mb)
                    parts += ps
                    lsum = lsum + psum
                acc = acc + _dot(jnp.concatenate(parts, axis=1).astype(bf16), vb)
            l_scr[hh] = lsum
            acc_scr[hh] = acc
        return c

    lax.fori_loop(0, qi, body_b, 0)
    for t in range(tiles):
        kt_idx = qi * tiles + t
        row0 = pl.multiple_of(kt_idx * ATT_K, ATT_K)
        vb = v_ref[pl.ds(row0, ATT_K), :]
        for hh in range(2):
            ps, psum = probs(s_scr[hh, kt_idx, t * ATT_K:, :], m_scr[hh, t * ATT_K:, :])
            l_scr[hh, t * ATT_K:, :] += psum
            acc_scr[hh, t * ATT_K:, :] += _dot(jnp.concatenate(ps, axis=1).astype(bf16), vb)

    lane = lax.broadcasted_iota(jnp.int32, (ATT_Q, 2 * HEAD_DIM), 1)
    o0 = acc_scr[0] / jnp.sum(l_scr[0], axis=-1, keepdims=True)
    o1 = acc_scr[1] / jnp.sum(l_scr[1], axis=-1, keepdims=True)
    o_ref[...] = jnp.where(lane < HEAD_DIM, o0, o1).astype(bf16)


def _attention(q, kt, v, batch, seq):
    T = q.shape[0]
    nq = seq // ATT_Q
    nk = seq // ATT_K
    pairs = N_HEADS // 2
    return pl.pallas_call(
        _attn_kernel,
        grid=(batch, pairs, nq),
        in_specs=[
            pl.BlockSpec((ATT_Q, 2 * HEAD_PAD), lambda b, p, i: (b * nq + i, p)),
            pl.BlockSpec((nk, 2 * HEAD_PAD, ATT_K), lambda b, p, i: (b, p, 0)),
            pl.BlockSpec((seq, 2 * HEAD_DIM), lambda b, p, i: (b, p)),
        ],
        out_specs=pl.BlockSpec((ATT_Q, 2 * HEAD_DIM), lambda b, p, i: (b * nq + i, p)),
        out_shape=jax.ShapeDtypeStruct((T, D_MODEL), bf16),
        scratch_shapes=[
            pltpu.VMEM((2, nk, ATT_Q, ATT_K), f32),
            pltpu.VMEM((2, ATT_Q, LANES), f32),
            pltpu.VMEM((2, ATT_Q, LANES), f32),
            pltpu.VMEM((2, ATT_Q, 2 * HEAD_DIM), f32),
        ],
        compiler_params=_params(("parallel", "parallel", "parallel")),
        name="attn",
    )(q, kt, v)


def _attn_out_kernel(x_ref, o_ref, sg_ref, wo_ref, mn_ref, wr_ref, x3_ref, lg_ref):
    gated = (o_ref[...].astype(f32) * sg_ref[...].astype(f32)).astype(bf16)
    x3 = x_ref[...] + _dot(gated, wo_ref[...])
    x3_ref[...] = x3
    lg_ref[...] = _router_logits_t(x3, mn_ref[...], wr_ref[...])


def _attn_out(x2d, o, sg, wo, m_norm, w_router):
    T = x2d.shape[0]
    return pl.pallas_call(
        _attn_out_kernel,
        grid=(T // ROW_TILE,),
        in_specs=[
            pl.BlockSpec((ROW_TILE, D_MODEL), lambda i: (i, 0)),
            pl.BlockSpec((ROW_TILE, D_MODEL), lambda i: (i, 0)),
            pl.BlockSpec((ROW_TILE, D_MODEL), lambda i: (i, 0)),
            _const_spec((D_MODEL, D_MODEL)),
            _const_spec((1, D_MODEL)),
            _const_spec((3 * D_MODEL, LANES)),
        ],
        out_specs=[
            pl.BlockSpec((ROW_TILE, D_MODEL), lambda i: (i, 0)),
            pl.BlockSpec((ROUTER_ROWS, ROW_TILE), lambda i: (0, i)),
        ],
        out_shape=[
            jax.ShapeDtypeStruct((T, D_MODEL), f32),
            jax.ShapeDtypeStruct((ROUTER_ROWS, T), f32),
        ],
        compiler_params=_params(("parallel",)),
        name="attn_out",
    )(x2d, o, sg, wo, m_norm, w_router)


def _router_weight(w_group, w_expert):
    pad = jnp.zeros((D_MODEL, LANES - N_EXPERTS - N_GROUPS), f32)
    w = jnp.concatenate([w_expert, w_group, pad], axis=1)
    w_hi = w.astype(bf16)
    w_lo = (w - w_hi.astype(f32)).astype(bf16)
    return jnp.concatenate([w_hi, w_lo, w_hi], axis=0)


def _pad_heads(w):
    w = w.reshape(D_MODEL, N_HEADS, HEAD_DIM)
    w = jnp.pad(w, ((0, 0), (0, 0), (0, HEAD_PAD - HEAD_DIM)))
    return w.reshape(D_MODEL, N_HEADS * HEAD_PAD)


def kernel(x, a_norm, a_w_in, a_b_in, a_v_norm, a_w_s, a_b_s, a_w_out, kv_norm, kv_w, kv_b_f,
           k_norm, b_norm, b_w_qg, q_norm, b_w_out, m_norm, m_w_group, m_b_group, m_w_expert,
           m_b_expert, m_w1, m_w3, m_w2):
    batch, seq, _ = x.shape
    T = batch * seq
    x2d = x.reshape(T, D_MODEL)

    x1, lg0 = _sgu(
        x2d, a_norm[0].reshape(1, -1), a_w_in[0].astype(bf16), a_b_in[0].reshape(1, -1),
        a_v_norm[0].reshape(1, -1), a_w_s[0], a_b_s[0].T, a_w_out[0].astype(bf16),
        m_norm[0].reshape(1, -1), _router_weight(m_w_group[0], m_w_expert[0]))
    x2 = _moe(0, x1, lg0, m_norm[0].reshape(1, -1), m_b_group[0], m_b_expert[0],
              m_w1, m_w3, m_w2)

    wk = kv_w[:, :D_MODEL]
    wv = kv_w[:, D_MODEL:2 * D_MODEL]
    wf = jnp.pad(kv_w[:, 2 * D_MODEL:], ((0, 0), (0, LANES - N_HEADS)))
    bfv = jnp.pad(kv_b_f, (0, LANES - N_HEADS)).reshape(1, LANES)
    wq = b_w_qg[0][:, :D_MODEL]
    wg = b_w_qg[0][:, D_MODEL:]
    gq = jnp.pad(q_norm[0], (0, HEAD_PAD - HEAD_DIM)).reshape(1, HEAD_PAD)
    gk = jnp.pad(k_norm, (0, HEAD_PAD - HEAD_DIM)).reshape(HEAD_PAD, 1)
    q, kt, v, sg = _proj(
        x2, seq, kv_norm.reshape(1, -1), b_norm[0].reshape(1, -1),
        _pad_heads(wk).T.astype(bf16), wv.astype(bf16), wf.astype(bf16), bfv,
        _pad_heads(wq).astype(bf16), wg.astype(bf16), gq, gk)
    o = _attention(q, kt, v, batch, seq)
    x3, lg1 = _attn_out(x2, o, sg, b_w_out[0].astype(bf16), m_norm[1].reshape(1, -1),
                        _router_weight(m_w_group[1], m_w_expert[1]))
    x4 = _moe(1, x3, lg1, m_norm[1].reshape(1, -1), m_b_group[1], m_b_expert[1],
              m_w1, m_w3, m_w2)
    return x4.reshape(batch, seq, D_MODEL)
```

```python
import functools
import math

import jax
import jax.numpy as jnp
import numpy as np
from jax import lax
from jax.experimental import pallas as pl
from jax.experimental.pallas import tpu as pltpu

D_MODEL = 1024
EPS = 1e-6
SGU_BLOCK = 128
SGU_CHUNK = 64
CHUNK_SHIFT = SGU_CHUNK.bit_length() - 1
SGU_WIDTH = 2 * D_MODEL
SGU_GROUPS = 8
SGU_GROUP_DIM = SGU_WIDTH // SGU_GROUPS
N_HEADS = 16
HEAD_DIM = D_MODEL // N_HEADS
N_GROUPS = 4
EXPERTS_PER_GROUP = 8
N_EXPERTS = N_GROUPS * EXPERTS_PER_GROUP
GROUP_SHIFT = EXPERTS_PER_GROUP.bit_length() - 1
TOP_K = 2
D_EXPERT = D_MODEL // 2

LANES = 128
ROW_SUBLANES = D_MODEL // 2 // LANES
ISSUE_UNROLL = 8
VMEM_LIMIT_BYTES = 56 * 1024 * 1024

ROW_TILE = 256
ROUTE_TILE = 512
EXPERT_ROWS = 256
MOVE_TILE = 512
ATT_Q = 1024
ATT_K = ROW_TILE
PV_TILES = 2
HEAD_PAD = 2 * HEAD_DIM
ROUTER_ROWS = 40
AUX_ONE0 = HEAD_DIM
AUX_CUM0 = HEAD_DIM + 3

LOG2E = math.log2(math.e)

bf16 = jnp.bfloat16
f32 = jnp.float32
u32 = jnp.uint32
HI_HALF = np.uint32(0xFFFF0000)


def _dot(a, b, precision=None):
    return jnp.dot(a, b, preferred_element_type=f32, precision=precision)


def _dot_nt(a, b, precision=None):
    return lax.dot_general(a, b, (((1,), (1,)), ((), ())), preferred_element_type=f32,
                           precision=precision)


def _rms(x, g):
    ms = jnp.mean(x * x, axis=-1, keepdims=True)
    return x * lax.rsqrt(ms + EPS) * g


def _gelu_tanh(z):
    c = math.sqrt(2.0 / math.pi)
    return z * (0.5 * (1.0 + jnp.tanh(c * (z + 0.044715 * (z * z * z)))))


def _split3(c):
    hi = c.astype(bf16).astype(f32)
    r = c - hi
    mid = r.astype(bf16).astype(f32)
    lo = r - mid
    return hi, mid, lo


def _pack_rows(v):
    half = D_MODEL // 2
    lo = lax.bitcast_convert_type(v[:, :half].astype(bf16).astype(f32), u32) >> 16
    hi = lax.bitcast_convert_type(v[:, half:].astype(bf16).astype(f32), u32) & HI_HALF
    return (hi | lo).reshape(v.shape[0], ROW_SUBLANES, LANES)


def _unpack_rows(w):
    w = w.reshape(w.shape[0], D_MODEL // 2)
    lo = lax.bitcast_convert_type(w << 16, f32)
    hi = lax.bitcast_convert_type(w & HI_HALF, f32)
    return jnp.concatenate([lo, hi], axis=1)


def _const_spec(shape):
    nd = len(shape)
    return pl.BlockSpec(shape, lambda *_: (0,) * nd)


def _params(sem):
    return pltpu.CompilerParams(dimension_semantics=sem, vmem_limit_bytes=VMEM_LIMIT_BYTES)


def _router_logits_t(x_new, mn, wr3):
    hm = _rms(x_new, mn)
    h_hi = hm.astype(bf16)
    h_lo = (hm - h_hi.astype(f32)).astype(bf16)
    lg = _dot(jnp.concatenate([h_hi, h_hi, h_lo], axis=1), wr3)
    return lg.T[:ROUTER_ROWS, :]


def _sgu_kernel(x_ref, an_ref, win_ref, bin_ref, vn_ref, ws_ref, bst_ref, wout_ref, mn_ref,
                wr_ref, x1_ref, lg_ref, u_scr, v_scr, gated_scr):
    x = x_ref[...]
    h = _rms(x, an_ref[...]).astype(bf16)
    u_scr[...] = _gelu_tanh(_dot(h, win_ref[:, :SGU_WIDTH]) + bin_ref[:, :SGU_WIDTH])
    v = _gelu_tanh(_dot(h, win_ref[:, SGU_WIDTH:]) + bin_ref[:, SGU_WIDTH:])
    v_scr[...] = _rms(v, vn_ref[...]).astype(bf16)

    t_chunk = lax.broadcasted_iota(jnp.int32, (SGU_BLOCK, SGU_BLOCK), 0) >> CHUNK_SHIFT
    s_chunk = lax.broadcasted_iota(jnp.int32, (SGU_BLOCK, SGU_BLOCK), 1) >> CHUNK_SHIFT
    causal = t_chunk >= s_chunk
    for g in range(SGU_GROUPS):
        wsm = jnp.where(causal, ws_ref[g], 0.0).astype(bf16)
        cols = slice(g * SGU_GROUP_DIM, (g + 1) * SGU_GROUP_DIM)
        for sb in range(ROW_TILE // SGU_BLOCK):
            rows = slice(sb * SGU_BLOCK, (sb + 1) * SGU_BLOCK)
            mixed = _dot(wsm, v_scr[rows, cols]) + bst_ref[:, g:g + 1]
            gated_scr[rows, cols] = (u_scr[rows, cols] * mixed).astype(bf16)

    x1 = x + _dot(gated_scr[...], wout_ref[...])
    x1_ref[...] = x1
    lg_ref[...] = _router_logits_t(x1, mn_ref[...], wr_ref[...])


def _sgu(x2d, a_norm, w_in, b_in, v_norm, w_s, b_s_t, w_out, m_norm, w_router):
    T = x2d.shape[0]
    return pl.pallas_call(
        _sgu_kernel,
        grid=(T // ROW_TILE,),
        in_specs=[
            pl.BlockSpec((ROW_TILE, D_MODEL), lambda i: (i, 0)),
            _const_spec((1, D_MODEL)),
            _const_spec((D_MODEL, 2 * SGU_WIDTH)),
            _const_spec((1, 2 * SGU_WIDTH)),
            _const_spec((1, SGU_WIDTH)),
            _const_spec((SGU_GROUPS, SGU_BLOCK, SGU_BLOCK)),
            _const_spec((SGU_BLOCK, SGU_GROUPS)),
            _const_spec((SGU_WIDTH, D_MODEL)),
            _const_spec((1, D_MODEL)),
            _const_spec((3 * D_MODEL, LANES)),
        ],
        out_specs=[
            pl.BlockSpec((ROW_TILE, D_MODEL), lambda i: (i, 0)),
            pl.BlockSpec((ROUTER_ROWS, ROW_TILE), lambda i: (0, i)),
        ],
        out_shape=[
            jax.ShapeDtypeStruct((T, D_MODEL), f32),
            jax.ShapeDtypeStruct((ROUTER_ROWS, T), f32),
        ],
        scratch_shapes=[
            pltpu.VMEM((ROW_TILE, SGU_WIDTH), f32),
            pltpu.VMEM((ROW_TILE, SGU_WIDTH), bf16),
            pltpu.VMEM((ROW_TILE, SGU_WIDTH), bf16),
        ],
        compiler_params=_params(("parallel",)),
        name="sgu",
    )(x2d, a_norm, w_in, b_in, v_norm, w_s, b_s_t, w_out, m_norm, w_router)


def _route_kernel(lg_ref, be_ref, bg_ref, oi_ref, ow_ref, cnt_ref, carry_scr):
    step = pl.program_id(0)

    @pl.when(step == 0)
    def _():
        carry_scr[...] = jnp.zeros_like(carry_scr)

    lg = lg_ref[...]
    e_l = lg[0:N_EXPERTS, :] + be_ref[...]
    g_l = lg[N_EXPERTS:N_EXPERTS + 8, :] + bg_ref[...]
    g_row = lax.broadcasted_iota(jnp.int32, g_l.shape, 0).astype(f32)
    g_l = jnp.where(g_row < N_GROUPS, g_l, -jnp.inf)
    g_max = jnp.max(g_l, axis=0, keepdims=True)
    g_sel = jnp.min(jnp.where(g_l == g_max, g_row, 8.0), axis=0, keepdims=True)
    g_den = jnp.sum(jnp.exp(g_l - g_max), axis=0, keepdims=True)
    g_w = 1.0 / g_den

    e_row_i = lax.broadcasted_iota(jnp.int32, e_l.shape, 0)
    e_row = e_row_i.astype(f32)
    e_grp = (e_row_i >> GROUP_SHIFT).astype(f32)
    e_in = jnp.where(e_grp == g_sel, e_l, -jnp.inf)
    m1 = jnp.max(e_in, axis=0, keepdims=True)
    i1 = jnp.min(jnp.where(e_in == m1, e_row, float(N_EXPERTS)), axis=0, keepdims=True)
    e_in2 = jnp.where(e_row == i1, -jnp.inf, e_in)
    m2 = jnp.max(e_in2, axis=0, keepdims=True)
    i2 = jnp.min(jnp.where(e_in2 == m2, e_row, float(N_EXPERTS)), axis=0, keepdims=True)
    t = jnp.exp(m2 - m1)
    p1 = 1.0 / (1.0 + t)
    p2 = t / (1.0 + t)

    hit1 = e_row == i1
    hit2 = e_row == i2
    member = jnp.logical_or(hit1, hit2)
    tt = lg.shape[1]
    before = (lax.broadcasted_iota(jnp.int32, (tt, tt), 0)
              < lax.broadcasted_iota(jnp.int32, (tt, tt), 1))
    prefix = _dot(member.astype(bf16), before.astype(bf16))
    rank_full = prefix + carry_scr[...]
    rank1 = jnp.sum(jnp.where(hit1, rank_full, 0.0), axis=0, keepdims=True)
    rank2 = jnp.sum(jnp.where(hit2, rank_full, 0.0), axis=0, keepdims=True)
    carry_new = carry_scr[...] + jnp.sum(member.astype(f32), axis=1, keepdims=True)
    carry_scr[...] = carry_new

    oi_ref[...] = jnp.zeros(oi_ref.shape, jnp.int32)
    oi_ref[0:1, :] = i1.astype(jnp.int32)
    oi_ref[1:2, :] = i2.astype(jnp.int32)
    oi_ref[2:3, :] = rank1.astype(jnp.int32)
    oi_ref[3:4, :] = rank2.astype(jnp.int32)
    ow_ref[...] = jnp.zeros(ow_ref.shape, f32)
    ow_ref[0:1, :] = g_w * p1
    ow_ref[1:2, :] = g_w * p2
    cnt_ref[...] = jnp.broadcast_to(carry_new, cnt_ref.shape)


def _route(logits_t, b_expert, b_group):
    T = logits_t.shape[1]
    return pl.pallas_call(
        _route_kernel,
        grid=(T // ROUTE_TILE,),
        in_specs=[
            pl.BlockSpec((ROUTER_ROWS, ROUTE_TILE), lambda i: (0, i)),
            _const_spec((N_EXPERTS, 1)),
            _const_spec((8, 1)),
        ],
        out_specs=[
            pl.BlockSpec((8, ROUTE_TILE), lambda i: (0, i)),
            pl.BlockSpec((8, ROUTE_TILE), lambda i: (0, i)),
            _const_spec((N_EXPERTS, LANES)),
        ],
        out_shape=[
            jax.ShapeDtypeStruct((8, T), jnp.int32),
            jax.ShapeDtypeStruct((8, T), f32),
            jax.ShapeDtypeStruct((N_EXPERTS, LANES), f32),
        ],
        scratch_shapes=[pltpu.VMEM((N_EXPERTS, 1), f32)],
        compiler_params=_params(("arbitrary",)),
        name="route",
    )(logits_t, b_expert, b_group)


def _row_copy(src, s, dst, d, sem):
    return pltpu.make_async_copy(src.at[s], dst.at[d], sem)


def _wait_rows(buf, sem):
    pltpu.make_async_copy(buf, buf, sem).wait()


def _dispatch_kernel(n_blocks, pend_ref, padded_ref, nu_ref, dest_ref, x_ref, mn_ref, xs_hbm,
                     zero_scr, x3_scr, zsem, sem):
    @pl.when(pl.program_id(0) == 0)
    def _():
        zero_scr[...] = jnp.zeros_like(zero_scr)

        def fill(row0):
            return pltpu.make_async_copy(zero_scr, xs_hbm.at[pl.ds(row0, EXPERT_ROWS)], zsem)

        for e in range(N_EXPERTS):
            @pl.when(padded_ref[e] > 0)
            def _():
                fill(pend_ref[e] - EXPERT_ROWS).start()

            @pl.when(nu_ref[0] + e < n_blocks)
            def _():
                fill((nu_ref[0] + e) * EXPERT_ROWS).start()

        for e in range(N_EXPERTS):
            @pl.when(padded_ref[e] > 0)
            def _():
                fill(0).wait()

            @pl.when(nu_ref[0] + e < n_blocks)
            def _():
                fill(0).wait()

    i = pl.program_id(0)
    slot = i % 2
    stage = x3_scr.at[slot]
    stage[...] = _pack_rows(_rms(x_ref[...], mn_ref[...]))

    def start(j, c):
        for k in range(TOP_K):
            _row_copy(stage, j, xs_hbm, dest_ref[k, j], sem.at[slot]).start(priority=k)
        return c

    lax.fori_loop(0, MOVE_TILE, start, 0, unroll=ISSUE_UNROLL)

    @pl.when(i > 0)
    def _():
        for k in range(TOP_K):
            _wait_rows(x3_scr.at[1 - slot], sem.at[1 - slot])

    @pl.when(i == pl.num_programs(0) - 1)
    def _():
        for k in range(TOP_K):
            _wait_rows(stage, sem.at[slot])


def _dispatch(pend, padded, n_used, dest_tiles, x2d, m_norm, n_rows):
    T = x2d.shape[0]
    n_blocks = n_rows // EXPERT_ROWS
    grid_spec = pltpu.PrefetchScalarGridSpec(
        num_scalar_prefetch=3,
        grid=(T // MOVE_TILE,),
        in_specs=[
            pl.BlockSpec((None, TOP_K, MOVE_TILE), lambda i, *_: (i, 0, 0),
                         memory_space=pltpu.SMEM),
            pl.BlockSpec((MOVE_TILE, D_MODEL), lambda i, *_: (i, 0)),
            pl.BlockSpec((1, D_MODEL), lambda i, *_: (0, 0)),
        ],
        out_specs=pl.BlockSpec(memory_space=pl.ANY),
        scratch_shapes=[
            pltpu.VMEM((EXPERT_ROWS, ROW_SUBLANES, LANES), u32),
            pltpu.VMEM((2, MOVE_TILE, ROW_SUBLANES, LANES), u32),
            pltpu.SemaphoreType.DMA(()),
            pltpu.SemaphoreType.DMA((2,)),
        ],
    )
    return pl.pallas_call(
        functools.partial(_dispatch_kernel, n_blocks),
        grid_spec=grid_spec,
        out_shape=jax.ShapeDtypeStruct((n_rows, ROW_SUBLANES, LANES), u32),
        compiler_params=_params(("arbitrary",)),
        name="dispatch",
    )(pend, padded, n_used, dest_tiles, x2d, m_norm)


def _expert_kernel(n_blocks, ps_ref, nb_ref, nu_ref, xs_hbm, w1_ref, w3_ref, w2_ref,
                   ys_hbm, xbuf, ybuf, h_scr, w1_scr, w3_scr, w2_scr, xsem, ysem):
    e = pl.program_id(0)
    nblk = nb_ref[e]
    g0 = ps_ref[e]
    n_used = nu_ref[0]

    def rows(g):
        return pl.ds(pl.multiple_of(g * EXPERT_ROWS, EXPERT_ROWS), EXPERT_ROWS)

    def x_copy(g):
        return pltpu.make_async_copy(xs_hbm.at[rows(g)], xbuf.at[g % 2], xsem.at[g % 2])

    def y_copy(g):
        return pltpu.make_async_copy(ybuf.at[g % 2], ys_hbm.at[rows(g)], ysem.at[g % 2])

    def normed(slot):
        return _unpack_rows(xbuf[slot]).astype(bf16)

    @pl.when(e == 0)
    def _():
        x_copy(0).start(priority=1)
        x_copy(1).start(priority=1)
        x_copy(0).wait()
        h_scr[...] = normed(0)

    @pl.when(nblk > 0)
    def _():
        w1_scr[...] = w1_ref[...].astype(bf16)
        w3_scr[...] = w3_ref[...].astype(bf16)
        w2_scr[...] = w2_ref[...].astype(bf16)

        def body(g, h):
            @pl.when(g + 2 < n_used)
            def _():
                x_copy(g + 2).start(priority=1)

            @pl.when(g + 1 < n_used)
            def _():
                x_copy(g + 1).wait()

            @pl.when(g >= 2)
            def _():
                y_copy(g - 2).wait()

            h_next = normed((g + 1) % 2)
            a = _dot(h, w1_scr[...])
            c = _dot(h, w3_scr[...])
            hid = (a * jax.nn.sigmoid(a)) * c
            ybuf[g % 2] = _pack_rows(_dot(hid.astype(bf16), w2_scr[...]))
            y_copy(g).start(priority=1)
            return h_next

        h_scr[...] = lax.fori_loop(g0, g0 + nblk, body, h_scr[...])

    @pl.when(e == pl.num_programs(0) - 1)
    def _():
        y_copy(n_used - 2).wait()
        y_copy(n_used - 1).wait()
        ybuf[0] = jnp.zeros(ybuf.shape[1:], u32)

        def fill(b, c):
            r = pl.ds(pl.multiple_of(b * EXPERT_ROWS, EXPERT_ROWS), EXPERT_ROWS)
            cp = pltpu.make_async_copy(ybuf.at[0], ys_hbm.at[r], ysem.at[0])
            cp.start()
            cp.wait()
            return c

        lax.fori_loop(nu_ref[0], n_blocks, fill, 0)


def _experts(layer, pstart, seg_blocks, n_used, xs, w1, w3, w2):
    n_rows = xs.shape[0]
    n_blocks = n_rows // EXPERT_ROWS

    def w_map(e, *_):
        return (layer, e, 0, 0)

    block_buf = pltpu.VMEM((2, EXPERT_ROWS, ROW_SUBLANES, LANES), u32)
    grid_spec = pltpu.PrefetchScalarGridSpec(
        num_scalar_prefetch=3,
        grid=(N_EXPERTS,),
        in_specs=[
            pl.BlockSpec(memory_space=pl.ANY),
            pl.BlockSpec((None, None, D_MODEL, D_EXPERT), w_map),
            pl.BlockSpec((None, None, D_MODEL, D_EXPERT), w_map),
            pl.BlockSpec((None, None, D_EXPERT, D_MODEL), w_map),
        ],
        out_specs=pl.BlockSpec(memory_space=pl.ANY),
        scratch_shapes=[
            block_buf,
            block_buf,
            pltpu.VMEM((EXPERT_ROWS, D_MODEL), bf16),
            pltpu.VMEM((D_MODEL, D_EXPERT), bf16),
            pltpu.VMEM((D_MODEL, D_EXPERT), bf16),
            pltpu.VMEM((D_EXPERT, D_MODEL), bf16),
            pltpu.SemaphoreType.DMA((2,)),
            pltpu.SemaphoreType.DMA((2,)),
        ],
    )
    return pl.pallas_call(
        functools.partial(_expert_kernel, n_blocks),
        grid_spec=grid_spec,
        out_shape=jax.ShapeDtypeStruct((n_rows, ROW_SUBLANES, LANES), u32),
        compiler_params=_params(("arbitrary",)),
        name="experts",
    )(pstart, seg_blocks, n_used, xs, w1, w3, w2)


def _combine_kernel(dest_ref, next_ref, x_ref, w_ref, ys_hbm, out_ref, buf, sem):
    i = pl.program_id(0)
    slot = i % 2

    def gather(idx_ref, s):
        def start(j, c):
            for k in range(TOP_K):
                _row_copy(ys_hbm, idx_ref[k, j], buf.at[s, k], j, sem.at[s]).start(priority=k)
            return c

        lax.fori_loop(0, MOVE_TILE, start, 0, unroll=ISSUE_UNROLL)

    @pl.when(i == 0)
    def _():
        gather(dest_ref, slot)

    @pl.when(i + 1 < pl.num_programs(0))
    def _():
        gather(next_ref, 1 - slot)

    for k in range(TOP_K):
        _wait_rows(buf.at[slot, k], sem.at[slot])
    w = w_ref[...]
    y0 = _unpack_rows(buf[slot, 0])
    y1 = _unpack_rows(buf[slot, 1])
    out_ref[...] = x_ref[...] + (w[:, 0:1] * y0 + w[:, 1:2] * y1)


def _combine(dest_tiles, x2d, w_cols, ys):
    T = x2d.shape[0]
    n_tiles = T // MOVE_TILE
    return pl.pallas_call(
        _combine_kernel,
        grid=(n_tiles,),
        in_specs=[
            pl.BlockSpec((None, TOP_K, MOVE_TILE), lambda i: (i, 0, 0),
                         memory_space=pltpu.SMEM),
            pl.BlockSpec((None, TOP_K, MOVE_TILE),
                         lambda i: (jnp.minimum(i + 1, n_tiles - 1), 0, 0),
                         memory_space=pltpu.SMEM),
            pl.BlockSpec((MOVE_TILE, D_MODEL), lambda i: (i, 0)),
            pl.BlockSpec((MOVE_TILE, 8), lambda i: (i, 0)),
            pl.BlockSpec(memory_space=pl.ANY),
        ],
        out_specs=pl.BlockSpec((MOVE_TILE, D_MODEL), lambda i: (i, 0)),
        out_shape=jax.ShapeDtypeStruct((T, D_MODEL), f32),
        scratch_shapes=[
            pltpu.VMEM((2, TOP_K, MOVE_TILE, ROW_SUBLANES, LANES), u32),
            pltpu.SemaphoreType.DMA((2,)),
        ],
        compiler_params=_params(("arbitrary",)),
        name="combine",
    )(dest_tiles, dest_tiles, x2d, w_cols, ys)


def _moe(layer, x2d, logits_t, m_norm, b_group, b_expert, w1, w3, w2):
    T = x2d.shape[0]
    n_blocks = (T * TOP_K) // EXPERT_ROWS + N_EXPERTS
    n_rows = n_blocks * EXPERT_ROWS

    be = b_expert.reshape(N_EXPERTS, 1)
    bg = jnp.concatenate([b_group, jnp.zeros((8 - N_GROUPS,), f32)]).reshape(8, 1)
    oi, ow, cnt = _route(logits_t, be, bg)

    counts = cnt[:, 0].astype(jnp.int32)
    padded = (counts + EXPERT_ROWS - 1) // EXPERT_ROWS * EXPERT_ROWS
    pend = jnp.cumsum(padded)
    pstart = pend - padded
    n_used = (pend[-1] // EXPERT_ROWS).astype(jnp.int32).reshape(1)
    e_ids = jnp.arange(N_EXPERTS, dtype=jnp.int32)[:, None, None]
    seg_start = jnp.sum(jnp.where(oi[None, 0:2] == e_ids, pstart[:, None, None], 0), axis=0)
    dest = seg_start + oi[2:4]
    dest_tiles = dest.reshape(TOP_K, T // MOVE_TILE, MOVE_TILE).transpose(1, 0, 2)
    w_cols = ow.T

    xs = _dispatch(pend.astype(jnp.int32), padded, n_used, dest_tiles, x2d, m_norm, n_rows)
    ys = _experts(layer, (pstart // EXPERT_ROWS).astype(jnp.int32), padded // EXPERT_ROWS,
                  n_used, xs, w1, w3, w2)
    return _combine(dest_tiles, x2d, w_cols, ys)


def _log_sigmoid(x):
    return jnp.minimum(x, 0.0) - jnp.log1p(jnp.exp(-jnp.abs(x)))


def _proj_kernel(tiles_per_seq, x_ref, kvn_ref, bn_ref, wkt_ref, wv_ref, wf_ref, bf_ref,
                 wq_ref, wg_ref, gq_ref, gk_ref, q_ref, kt_ref, v_ref, sg_ref, carry_scr):
    i = pl.program_id(0)
    x = x_ref[...]
    hkv = _rms(x, kvn_ref[...]).astype(bf16)
    hq = _rms(x, bn_ref[...]).astype(bf16)

    v_ref[...] = _dot(hkv, wv_ref[...]).astype(bf16)
    sg_ref[...] = jax.nn.sigmoid(_dot(hq, wg_ref[...])).astype(bf16)

    logf = _log_sigmoid(_dot(hkv, wf_ref[...]) + bf_ref[...])
    tm = x.shape[0]
    incl = (lax.broadcasted_iota(jnp.int32, (tm, tm), 0)
            >= lax.broadcasted_iota(jnp.int32, (tm, tm), 1)).astype(f32)
    @pl.when(i % tiles_per_seq == 0)
    def _():
        carry_scr[...] = jnp.zeros_like(carry_scr)

    cum = _dot(incl, logf, precision=lax.Precision.HIGHEST) + carry_scr[...]
    carry_scr[...] = cum[tm - 1:tm, :]
    cum = cum * LOG2E
    cum_t = cum.T

    lane = lax.broadcasted_iota(jnp.int32, (tm, HEAD_PAD), 1)
    sub = lax.broadcasted_iota(jnp.int32, (HEAD_PAD, tm), 0)
    q_raw = _dot(hq, wq_ref[...])
    k_raw_t = _dot_nt(wkt_ref[...], hkv)
    scale = HEAD_DIM ** -0.5 * LOG2E
    for h in range(N_HEADS):
        sl = slice(h * HEAD_PAD, (h + 1) * HEAD_PAD)
        qb = q_raw[:, sl]
        q_ms = jnp.sum(qb * qb, axis=-1, keepdims=True) * (1.0 / HEAD_DIM)
        qn = qb * lax.rsqrt(q_ms + EPS) * gq_ref[...] * scale
        c_hi, c_mid, c_lo = _split3(cum[:, h:h + 1])
        qa = jnp.where(lane < HEAD_DIM, qn,
             jnp.where(lane < AUX_CUM0, 1.0,
             jnp.where(lane == AUX_CUM0, c_hi,
             jnp.where(lane == AUX_CUM0 + 1, c_mid,
             jnp.where(lane == AUX_CUM0 + 2, c_lo, 0.0)))))
        q_ref[:, sl] = qa.astype(bf16)

        kb = k_raw_t[sl, :]
        k_ms = jnp.sum(kb * kb, axis=0, keepdims=True) * (1.0 / HEAD_DIM)
        kn = kb * lax.rsqrt(k_ms + EPS) * gk_ref[...]
        t_hi, t_mid, t_lo = _split3(cum_t[h:h + 1, :])
        ka = jnp.where(sub < HEAD_DIM, kn,
             jnp.where(sub == AUX_ONE0, -t_hi,
             jnp.where(sub == AUX_ONE0 + 1, -t_mid,
             jnp.where(sub == AUX_ONE0 + 2, -t_lo,
             jnp.where(sub < AUX_CUM0 + 3, 1.0, 0.0)))))
        kt_ref[0, sl, :] = ka.astype(bf16)


def _proj(x2d, seq, kv_norm, b_norm, wkt, wv, wf, bfv, wq, wg, gq, gk):
    T = x2d.shape[0]
    n_tiles = T // ROW_TILE
    qw = N_HEADS * HEAD_PAD
    return pl.pallas_call(
        functools.partial(_proj_kernel, seq // ROW_TILE),
        grid=(n_tiles,),
        in_specs=[
            pl.BlockSpec((ROW_TILE, D_MODEL), lambda i: (i, 0)),
            _const_spec((1, D_MODEL)),
            _const_spec((1, D_MODEL)),
            _const_spec((qw, D_MODEL)),
            _const_spec((D_MODEL, D_MODEL)),
            _const_spec((D_MODEL, LANES)),
            _const_spec((1, LANES)),
            _const_spec((D_MODEL, qw)),
            _const_spec((D_MODEL, D_MODEL)),
            _const_spec((1, HEAD_PAD)),
            _const_spec((HEAD_PAD, 1)),
        ],
        out_specs=[
            pl.BlockSpec((ROW_TILE, qw), lambda i: (i, 0)),
            pl.BlockSpec((1, qw, ROW_TILE), lambda i: (i, 0, 0)),
            pl.BlockSpec((ROW_TILE, D_MODEL), lambda i: (i, 0)),
            pl.BlockSpec((ROW_TILE, D_MODEL), lambda i: (i, 0)),
        ],
        out_shape=[
            jax.ShapeDtypeStruct((T, qw), bf16),
            jax.ShapeDtypeStruct((n_tiles, qw, ROW_TILE), bf16),
            jax.ShapeDtypeStruct((T, D_MODEL), bf16),
            jax.ShapeDtypeStruct((T, D_MODEL), bf16),
        ],
        scratch_shapes=[pltpu.VMEM((1, LANES), f32)],
        compiler_params=_params(("arbitrary",)),
        name="proj",
    )(x2d, kv_norm, b_norm, wkt, wv, wf, bfv, wq, wg, gq, gk)


def _attn_kernel(q_ref, kt_ref, v_ref, o_ref, s_scr, m_scr, l_scr, acc_scr):
    qi = pl.program_id(2)
    tiles = ATT_Q // ATT_K
    groups = ATT_K // LANES
    heads = [slice(hh * HEAD_PAD, (hh + 1) * HEAD_PAD) for hh in range(2)]
    m_scr[...] = jnp.full(m_scr.shape, -jnp.inf, f32)

    def lane_max(mx, s):
        for g in range(groups):
            mx = jnp.maximum(mx, s[:, g * LANES:(g + 1) * LANES])
        return mx

    def probs(s, mb):
        ps = [jnp.exp2(s[:, g * LANES:(g + 1) * LANES] - mb) for g in range(groups)]
        return ps, functools.reduce(lambda a, b: a + b, ps)

    def body_a(trip, c):
        for t in range(tiles):
            kt_idx = trip * tiles + t
            for hh in range(2):
                s = _dot(q_ref[:, heads[hh]], kt_ref[kt_idx, heads[hh], :])
                s_scr[hh, kt_idx] = s
                m_scr[hh] = lane_max(m_scr[hh], s)
        return c

    lax.fori_loop(0, qi, body_a, 0)
    for t in range(tiles):
        kt_idx = qi * tiles + t
        rows = ATT_Q - t * ATT_K
        visible = (lax.broadcasted_iota(jnp.int32, (rows, ATT_K), 1)
                   <= lax.broadcasted_iota(jnp.int32, (rows, ATT_K), 0))
        for hh in range(2):
            s = _dot(q_ref[t * ATT_K:, heads[hh]], kt_ref[kt_idx, heads[hh], :])
            s = jnp.where(visible, s, -jnp.inf)
            s_scr[hh, kt_idx, t * ATT_K:, :] = s
            m_scr[hh, t * ATT_K:, :] = lane_max(m_scr[hh, t * ATT_K:, :], s)

    for hh in range(2):
        row_max = jnp.max(m_scr[hh], axis=-1, keepdims=True)
        m_scr[hh] = jnp.broadcast_to(row_max, (ATT_Q, LANES))
    l_scr[...] = jnp.zeros(l_scr.shape, f32)
    acc_scr[...] = jnp.zeros(acc_scr.shape, f32)

    def body_b(trip, c):
        for hh in range(2):
            mb = m_scr[hh]
            lsum, acc = l_scr[hh], acc_scr[hh]
            for t0 in range(0, tiles, PV_TILES):
                kt_idx = trip * tiles + t0
                row0 = pl.multiple_of(kt_idx * ATT_K, PV_TILES * ATT_K)
                vb = v_ref[pl.ds(row0, PV_TILES * ATT_K), :]
                parts = []
                for t in range(PV_TILES):
                    ps, psum = probs(s_scr[hh, kt_idx + t], mb)
                    parts += ps
                    lsum = lsum + psum
                acc = acc + _dot(jnp.concatenate(parts, axis=1).astype(bf16), vb)
            l_scr[hh] = lsum
            acc_scr[hh] = acc
        return c

    lax.fori_loop(0, qi, body_b, 0)
    for t in range(tiles):
        kt_idx = qi * tiles + t
        row0 = pl.multiple_of(kt_idx * ATT_K, ATT_K)
        vb = v_ref[pl.ds(row0, ATT_K), :]
        for hh in range(2):
            ps, psum = probs(s_scr[hh, kt_idx, t * ATT_K:, :], m_scr[hh, t * ATT_K:, :])
            l_scr[hh, t * ATT_K:, :] += psum
            acc_scr[hh, t * ATT_K:, :] += _dot(jnp.concatenate(ps, axis=1).astype(bf16), vb)

    lane = lax.broadcasted_iota(jnp.int32, (ATT_Q, 2 * HEAD_DIM), 1)
    o0 = acc_scr[0] / jnp.sum(l_scr[0], axis=-1, keepdims=True)
    o1 = acc_scr[1] / jnp.sum(l_scr[1], axis=-1, keepdims=True)
    o_ref[...] = jnp.where(lane < HEAD_DIM, o0, o1).astype(bf16)


def _attention(q, kt, v, batch, seq):
    T = q.shape[0]
    nq = seq // ATT_Q
    nk = seq // ATT_K
    pairs = N_HEADS // 2
    return pl.pallas_call(
        _attn_kernel,
        grid=(batch, pairs, nq),
        in_specs=[
            pl.BlockSpec((ATT_Q, 2 * HEAD_PAD), lambda b, p, i: (b * nq + i, p)),
            pl.BlockSpec((nk, 2 * HEAD_PAD, ATT_K), lambda b, p, i: (b, p, 0)),
            pl.BlockSpec((seq, 2 * HEAD_DIM), lambda b, p, i: (b, p)),
        ],
        out_specs=pl.BlockSpec((ATT_Q, 2 * HEAD_DIM), lambda b, p, i: (b * nq + i, p)),
        out_shape=jax.ShapeDtypeStruct((T, D_MODEL), bf16),
        scratch_shapes=[
            pltpu.VMEM((2, nk, ATT_Q, ATT_K), f32),
            pltpu.VMEM((2, ATT_Q, LANES), f32),
            pltpu.VMEM((2, ATT_Q, LANES), f32),
            pltpu.VMEM((2, ATT_Q, 2 * HEAD_DIM), f32),
        ],
        compiler_params=_params(("parallel", "parallel", "parallel")),
        name="attn",
    )(q, kt, v)


def _attn_out_kernel(x_ref, o_ref, sg_ref, wo_ref, mn_ref, wr_ref, x3_ref, lg_ref):
    gated = (o_ref[...].astype(f32) * sg_ref[...].astype(f32)).astype(bf16)
    x3 = x_ref[...] + _dot(gated, wo_ref[...])
    x3_ref[...] = x3
    lg_ref[...] = _router_logits_t(x3, mn_ref[...], wr_ref[...])


def _attn_out(x2d, o, sg, wo, m_norm, w_router):
    T = x2d.shape[0]
    return pl.pallas_call(
        _attn_out_kernel,
        grid=(T // ROW_TILE,),
        in_specs=[
            pl.BlockSpec((ROW_TILE, D_MODEL), lambda i: (i, 0)),
            pl.BlockSpec((ROW_TILE, D_MODEL), lambda i: (i, 0)),
            pl.BlockSpec((ROW_TILE, D_MODEL), lambda i: (i, 0)),
            _const_spec((D_MODEL, D_MODEL)),
            _const_spec((1, D_MODEL)),
            _const_spec((3 * D_MODEL, LANES)),
        ],
        out_specs=[
            pl.BlockSpec((ROW_TILE, D_MODEL), lambda i: (i, 0)),
            pl.BlockSpec((ROUTER_ROWS, ROW_TILE), lambda i: (0, i)),
        ],
        out_shape=[
            jax.ShapeDtypeStruct((T, D_MODEL), f32),
            jax.ShapeDtypeStruct((ROUTER_ROWS, T), f32),
        ],
        compiler_params=_params(("parallel",)),
        name="attn_out",
    )(x2d, o, sg, wo, m_norm, w_router)


def _router_weight(w_group, w_expert):
    pad = jnp.zeros((D_MODEL, LANES - N_EXPERTS - N_GROUPS), f32)
    w = jnp.concatenate([w_expert, w_group, pad], axis=1)
    w_hi = w.astype(bf16)
    w_lo = (w - w_hi.astype(f32)).astype(bf16)
    return jnp.concatenate([w_hi, w_lo, w_hi], axis=0)


def _pad_heads(w):
    w = w.reshape(D_MODEL, N_HEADS, HEAD_DIM)
    w = jnp.pad(w, ((0, 0), (0, 0), (0, HEAD_PAD - HEAD_DIM)))
    return w.reshape(D_MODEL, N_HEADS * HEAD_PAD)


def kernel(x, a_norm, a_w_in, a_b_in, a_v_norm, a_w_s, a_b_s, a_w_out, kv_norm, kv_w, kv_b_f,
           k_norm, b_norm, b_w_qg, q_norm, b_w_out, m_norm, m_w_group, m_b_group, m_w_expert,
           m_b_expert, m_w1, m_w3, m_w2):
    batch, seq, _ = x.shape
    T = batch * seq
    x2d = x.reshape(T, D_MODEL)

    x1, lg0 = _sgu(
        x2d, a_norm[0].reshape(1, -1), a_w_in[0].astype(bf16), a_b_in[0].reshape(1, -1),
        a_v_norm[0].reshape(1, -1), a_w_s[0], a_b_s[0].T, a_w_out[0].astype(bf16),
        m_norm[0].reshape(1, -1), _router_weight(m_w_group[0], m_w_expert[0]))
    x2 = _moe(0, x1, lg0, m_norm[0].reshape(1, -1), m_b_group[0], m_b_expert[0],
              m_w1, m_w3, m_w2)

    wk = kv_w[:, :D_MODEL]
    wv = kv_w[:, D_MODEL:2 * D_MODEL]
    wf = jnp.pad(kv_w[:, 2 * D_MODEL:], ((0, 0), (0, LANES - N_HEADS)))
    bfv = jnp.pad(kv_b_f, (0, LANES - N_HEADS)).reshape(1, LANES)
    wq = b_w_qg[0][:, :D_MODEL]
    wg = b_w_qg[0][:, D_MODEL:]
    gq = jnp.pad(q_norm[0], (0, HEAD_PAD - HEAD_DIM)).reshape(1, HEAD_PAD)
    gk = jnp.pad(k_norm, (0, HEAD_PAD - HEAD_DIM)).reshape(HEAD_PAD, 1)
    q, kt, v, sg = _proj(
        x2, seq, kv_norm.reshape(1, -1), b_norm[0].reshape(1, -1),
        _pad_heads(wk).T.astype(bf16), wv.astype(bf16), wf.astype(bf16), bfv,
        _pad_heads(wq).astype(bf16), wg.astype(bf16), gq, gk)
    o = _attention(q, kt, v, batch, seq)
    x3, lg1 = _attn_out(x2, o, sg, b_w_out[0].astype(bf16), m_norm[1].reshape(1, -1),
                        _router_weight(m_w_group[1], m_w_expert[1]))
    x4 = _moe(1, x3, lg1, m_norm[1].reshape(1, -1), m_b_group[1], m_b_expert[1],
              m_w1, m_w3, m_w2)
    return x4.reshape(batch, seq, D_MODEL)
```

```python
import functools
import math

import jax
import jax.numpy as jnp
import numpy as np
from jax import lax
from jax.experimental import pallas as pl
from jax.experimental.pallas import tpu as pltpu

D_MODEL = 1024
EPS = 1e-6
SGU_BLOCK = 128
SGU_CHUNK = 64
CHUNK_SHIFT = SGU_CHUNK.bit_length() - 1
SGU_WIDTH = 2 * D_MODEL
SGU_GROUPS = 8
SGU_GROUP_DIM = SGU_WIDTH // SGU_GROUPS
N_HEADS = 16
HEAD_DIM = D_MODEL // N_HEADS
N_GROUPS = 4
EXPERTS_PER_GROUP = 8
N_EXPERTS = N_GROUPS * EXPERTS_PER_GROUP
GROUP_SHIFT = EXPERTS_PER_GROUP.bit_length() - 1
TOP_K = 2
D_EXPERT = D_MODEL // 2

LANES = 128
ROW_SUBLANES = D_MODEL // 2 // LANES
ISSUE_UNROLL = 8
VMEM_LIMIT_BYTES = 56 * 1024 * 1024

ROW_TILE = 256
ROUTE_TILE = 512
EXPERT_ROWS = 512
MOVE_TILE = 512
ATT_Q = 1024
ATT_K = ROW_TILE
PV_TILES = 2
HEAD_PAD = 2 * HEAD_DIM
ROUTER_ROWS = 40
AUX_ONE0 = HEAD_DIM
AUX_CUM0 = HEAD_DIM + 3

LOG2E = math.log2(math.e)

bf16 = jnp.bfloat16
f32 = jnp.float32
u32 = jnp.uint32
HI_HALF = np.uint32(0xFFFF0000)


def _dot(a, b, precision=None):
    return jnp.dot(a, b, preferred_element_type=f32, precision=precision)


def _dot_nt(a, b, precision=None):
    return lax.dot_general(a, b, (((1,), (1,)), ((), ())), preferred_element_type=f32,
                           precision=precision)


def _rms(x, g):
    ms = jnp.mean(x * x, axis=-1, keepdims=True)
    return x * lax.rsqrt(ms + EPS) * g


def _gelu_tanh(z):
    c = math.sqrt(2.0 / math.pi)
    return z * (0.5 * (1.0 + jnp.tanh(c * (z + 0.044715 * (z * z * z)))))


def _split3(c):
    hi = c.astype(bf16).astype(f32)
    r = c - hi
    mid = r.astype(bf16).astype(f32)
    lo = r - mid
    return hi, mid, lo


def _pack_rows(v):
    half = D_MODEL // 2
    lo = lax.bitcast_convert_type(v[:, :half].astype(bf16).astype(f32), u32) >> 16
    hi = lax.bitcast_convert_type(v[:, half:].astype(bf16).astype(f32), u32) & HI_HALF
    return (hi | lo).reshape(v.shape[0], ROW_SUBLANES, LANES)


def _unpack_rows(w):
    w = w.reshape(w.shape[0], D_MODEL // 2)
    lo = lax.bitcast_convert_type(w << 16, f32)
    hi = lax.bitcast_convert_type(w & HI_HALF, f32)
    return jnp.concatenate([lo, hi], axis=1)


def _const_spec(shape):
    nd = len(shape)
    return pl.BlockSpec(shape, lambda *_: (0,) * nd)


def _params(sem):
    return pltpu.CompilerParams(dimension_semantics=sem, vmem_limit_bytes=VMEM_LIMIT_BYTES)


def _router_logits_t(x_new, mn, wr3):
    hm = _rms(x_new, mn)
    h_hi = hm.astype(bf16)
    h_lo = (hm - h_hi.astype(f32)).astype(bf16)
    lg = _dot(jnp.concatenate([h_hi, h_hi, h_lo], axis=1), wr3)
    return lg.T[:ROUTER_ROWS, :]


def _sgu_kernel(x_ref, an_ref, win_ref, bin_ref, vn_ref, ws_ref, bst_ref, wout_ref, mn_ref,
                wr_ref, x1_ref, lg_ref, u_scr, v_scr, gated_scr):
    x = x_ref[...]
    h = _rms(x, an_ref[...]).astype(bf16)
    u_scr[...] = _gelu_tanh(_dot(h, win_ref[:, :SGU_WIDTH]) + bin_ref[:, :SGU_WIDTH])
    v = _gelu_tanh(_dot(h, win_ref[:, SGU_WIDTH:]) + bin_ref[:, SGU_WIDTH:])
    v_scr[...] = _rms(v, vn_ref[...]).astype(bf16)

    t_chunk = lax.broadcasted_iota(jnp.int32, (SGU_BLOCK, SGU_BLOCK), 0) >> CHUNK_SHIFT
    s_chunk = lax.broadcasted_iota(jnp.int32, (SGU_BLOCK, SGU_BLOCK), 1) >> CHUNK_SHIFT
    causal = t_chunk >= s_chunk
    for g in range(SGU_GROUPS):
        wsm = jnp.where(causal, ws_ref[g], 0.0).astype(bf16)
        cols = slice(g * SGU_GROUP_DIM, (g + 1) * SGU_GROUP_DIM)
        for sb in range(ROW_TILE // SGU_BLOCK):
            rows = slice(sb * SGU_BLOCK, (sb + 1) * SGU_BLOCK)
            mixed = _dot(wsm, v_scr[rows, cols]) + bst_ref[:, g:g + 1]
            gated_scr[rows, cols] = (u_scr[rows, cols] * mixed).astype(bf16)

    x1 = x + _dot(gated_scr[...], wout_ref[...])
    x1_ref[...] = x1
    lg_ref[...] = _router_logits_t(x1, mn_ref[...], wr_ref[...])


def _sgu(x2d, a_norm, w_in, b_in, v_norm, w_s, b_s_t, w_out, m_norm, w_router):
    T = x2d.shape[0]
    return pl.pallas_call(
        _sgu_kernel,
        grid=(T // ROW_TILE,),
        in_specs=[
            pl.BlockSpec((ROW_TILE, D_MODEL), lambda i: (i, 0)),
            _const_spec((1, D_MODEL)),
            _const_spec((D_MODEL, 2 * SGU_WIDTH)),
            _const_spec((1, 2 * SGU_WIDTH)),
            _const_spec((1, SGU_WIDTH)),
            _const_spec((SGU_GROUPS, SGU_BLOCK, SGU_BLOCK)),
            _const_spec((SGU_BLOCK, SGU_GROUPS)),
            _const_spec((SGU_WIDTH, D_MODEL)),
            _const_spec((1, D_MODEL)),
            _const_spec((3 * D_MODEL, LANES)),
        ],
        out_specs=[
            pl.BlockSpec((ROW_TILE, D_MODEL), lambda i: (i, 0)),
            pl.BlockSpec((ROUTER_ROWS, ROW_TILE), lambda i: (0, i)),
        ],
        out_shape=[
            jax.ShapeDtypeStruct((T, D_MODEL), f32),
            jax.ShapeDtypeStruct((ROUTER_ROWS, T), f32),
        ],
        scratch_shapes=[
            pltpu.VMEM((ROW_TILE, SGU_WIDTH), f32),
            pltpu.VMEM((ROW_TILE, SGU_WIDTH), bf16),
            pltpu.VMEM((ROW_TILE, SGU_WIDTH), bf16),
        ],
        compiler_params=_params(("parallel",)),
        name="sgu",
    )(x2d, a_norm, w_in, b_in, v_norm, w_s, b_s_t, w_out, m_norm, w_router)


def _route_kernel(lg_ref, be_ref, bg_ref, oi_ref, ow_ref, cnt_ref, carry_scr):
    step = pl.program_id(0)

    @pl.when(step == 0)
    def _():
        carry_scr[...] = jnp.zeros_like(carry_scr)

    lg = lg_ref[...]
    e_l = lg[0:N_EXPERTS, :] + be_ref[...]
    g_l = lg[N_EXPERTS:N_EXPERTS + 8, :] + bg_ref[...]
    g_row = lax.broadcasted_iota(jnp.int32, g_l.shape, 0).astype(f32)
    g_l = jnp.where(g_row < N_GROUPS, g_l, -jnp.inf)
    g_max = jnp.max(g_l, axis=0, keepdims=True)
    g_sel = jnp.min(jnp.where(g_l == g_max, g_row, 8.0), axis=0, keepdims=True)
    g_den = jnp.sum(jnp.exp(g_l - g_max), axis=0, keepdims=True)
    g_w = 1.0 / g_den

    e_row_i = lax.broadcasted_iota(jnp.int32, e_l.shape, 0)
    e_row = e_row_i.astype(f32)
    e_grp = (e_row_i >> GROUP_SHIFT).astype(f32)
    e_in = jnp.where(e_grp == g_sel, e_l, -jnp.inf)
    m1 = jnp.max(e_in, axis=0, keepdims=True)
    i1 = jnp.min(jnp.where(e_in == m1, e_row, float(N_EXPERTS)), axis=0, keepdims=True)
    e_in2 = jnp.where(e_row == i1, -jnp.inf, e_in)
    m2 = jnp.max(e_in2, axis=0, keepdims=True)
    i2 = jnp.min(jnp.where(e_in2 == m2, e_row, float(N_EXPERTS)), axis=0, keepdims=True)
    t = jnp.exp(m2 - m1)
    p1 = 1.0 / (1.0 + t)
    p2 = t / (1.0 + t)

    hit1 = e_row == i1
    hit2 = e_row == i2
    member = jnp.logical_or(hit1, hit2)
    tt = lg.shape[1]
    before = (lax.broadcasted_iota(jnp.int32, (tt, tt), 0)
              < lax.broadcasted_iota(jnp.int32, (tt, tt), 1))
    prefix = _dot(member.astype(bf16), before.astype(bf16))
    rank_full = prefix + carry_scr[...]
    rank1 = jnp.sum(jnp.where(hit1, rank_full, 0.0), axis=0, keepdims=True)
    rank2 = jnp.sum(jnp.where(hit2, rank_full, 0.0), axis=0, keepdims=True)
    carry_new = carry_scr[...] + jnp.sum(member.astype(f32), axis=1, keepdims=True)
    carry_scr[...] = carry_new

    oi_ref[...] = jnp.zeros(oi_ref.shape, jnp.int32)
    oi_ref[0:1, :] = i1.astype(jnp.int32)
    oi_ref[1:2, :] = i2.astype(jnp.int32)
    oi_ref[2:3, :] = rank1.astype(jnp.int32)
    oi_ref[3:4, :] = rank2.astype(jnp.int32)
    ow_ref[...] = jnp.zeros(ow_ref.shape, f32)
    ow_ref[0:1, :] = g_w * p1
    ow_ref[1:2, :] = g_w * p2
    cnt_ref[...] = jnp.broadcast_to(carry_new, cnt_ref.shape)


def _route(logits_t, b_expert, b_group):
    T = logits_t.shape[1]
    return pl.pallas_call(
        _route_kernel,
        grid=(T // ROUTE_TILE,),
        in_specs=[
            pl.BlockSpec((ROUTER_ROWS, ROUTE_TILE), lambda i: (0, i)),
            _const_spec((N_EXPERTS, 1)),
            _const_spec((8, 1)),
        ],
        out_specs=[
            pl.BlockSpec((8, ROUTE_TILE), lambda i: (0, i)),
            pl.BlockSpec((8, ROUTE_TILE), lambda i: (0, i)),
            _const_spec((N_EXPERTS, LANES)),
        ],
        out_shape=[
            jax.ShapeDtypeStruct((8, T), jnp.int32),
            jax.ShapeDtypeStruct((8, T), f32),
            jax.ShapeDtypeStruct((N_EXPERTS, LANES), f32),
        ],
        scratch_shapes=[pltpu.VMEM((N_EXPERTS, 1), f32)],
        compiler_params=_params(("arbitrary",)),
        name="route",
    )(logits_t, b_expert, b_group)


def _row_copy(src, s, dst, d, sem):
    return pltpu.make_async_copy(src.at[s], dst.at[d], sem)


def _wait_rows(buf, sem):
    pltpu.make_async_copy(buf, buf, sem).wait()


def _dispatch_kernel(n_blocks, pend_ref, padded_ref, nu_ref, dest_ref, x_ref, mn_ref, xs_hbm,
                     zero_scr, x3_scr, zsem, sem):
    @pl.when(pl.program_id(0) == 0)
    def _():
        zero_scr[...] = jnp.zeros_like(zero_scr)

        def fill(row0):
            return pltpu.make_async_copy(zero_scr, xs_hbm.at[pl.ds(row0, EXPERT_ROWS)], zsem)

        for e in range(N_EXPERTS):
            @pl.when(padded_ref[e] > 0)
            def _():
                fill(pend_ref[e] - EXPERT_ROWS).start()

            @pl.when(nu_ref[0] + e < n_blocks)
            def _():
                fill((nu_ref[0] + e) * EXPERT_ROWS).start()

        for e in range(N_EXPERTS):
            @pl.when(padded_ref[e] > 0)
            def _():
                fill(0).wait()

            @pl.when(nu_ref[0] + e < n_blocks)
            def _():
                fill(0).wait()

    i = pl.program_id(0)
    slot = i % 2
    stage = x3_scr.at[slot]
    stage[...] = _pack_rows(_rms(x_ref[...], mn_ref[...]))

    def start(j, c):
        for k in range(TOP_K):
            _row_copy(stage, j, xs_hbm, dest_ref[k, j], sem.at[slot]).start(priority=k)
        return c

    lax.fori_loop(0, MOVE_TILE, start, 0, unroll=ISSUE_UNROLL)

    @pl.when(i > 0)
    def _():
        for k in range(TOP_K):
            _wait_rows(x3_scr.at[1 - slot], sem.at[1 - slot])

    @pl.when(i == pl.num_programs(0) - 1)
    def _():
        for k in range(TOP_K):
            _wait_rows(stage, sem.at[slot])


def _dispatch(pend, padded, n_used, dest_tiles, x2d, m_norm, n_rows):
    T = x2d.shape[0]
    n_blocks = n_rows // EXPERT_ROWS
    grid_spec = pltpu.PrefetchScalarGridSpec(
        num_scalar_prefetch=3,
        grid=(T // MOVE_TILE,),
        in_specs=[
            pl.BlockSpec((None, TOP_K, MOVE_TILE), lambda i, *_: (i, 0, 0),
                         memory_space=pltpu.SMEM),
            pl.BlockSpec((MOVE_TILE, D_MODEL), lambda i, *_: (i, 0)),
            pl.BlockSpec((1, D_MODEL), lambda i, *_: (0, 0)),
        ],
        out_specs=pl.BlockSpec(memory_space=pl.ANY),
        scratch_shapes=[
            pltpu.VMEM((EXPERT_ROWS, ROW_SUBLANES, LANES), u32),
            pltpu.VMEM((2, MOVE_TILE, ROW_SUBLANES, LANES), u32),
            pltpu.SemaphoreType.DMA(()),
            pltpu.SemaphoreType.DMA((2,)),
        ],
    )
    return pl.pallas_call(
        functools.partial(_dispatch_kernel, n_blocks),
        grid_spec=grid_spec,
        out_shape=jax.ShapeDtypeStruct((n_rows, ROW_SUBLANES, LANES), u32),
        compiler_params=_params(("arbitrary",)),
        name="dispatch",
    )(pend, padded, n_used, dest_tiles, x2d, m_norm)


def _expert_kernel(n_blocks, ps_ref, nb_ref, nu_ref, xs_hbm, w1_ref, w3_ref, w2_ref,
                   ys_hbm, xbuf, ybuf, h_scr, w1_scr, w3_scr, w2_scr, xsem, ysem):
    e = pl.program_id(0)
    nblk = nb_ref[e]
    g0 = ps_ref[e]
    n_used = nu_ref[0]

    def rows(g):
        return pl.ds(pl.multiple_of(g * EXPERT_ROWS, EXPERT_ROWS), EXPERT_ROWS)

    def x_copy(g):
        return pltpu.make_async_copy(xs_hbm.at[rows(g)], xbuf.at[g % 2], xsem.at[g % 2])

    def y_copy(g):
        return pltpu.make_async_copy(ybuf.at[g % 2], ys_hbm.at[rows(g)], ysem.at[g % 2])

    def normed(slot):
        return _unpack_rows(xbuf[slot]).astype(bf16)

    @pl.when(e == 0)
    def _():
        x_copy(0).start(priority=1)
        x_copy(1).start(priority=1)
        x_copy(0).wait()
        h_scr[...] = normed(0)

    @pl.when(nblk > 0)
    def _():
        w1_scr[...] = w1_ref[...].astype(bf16)
        w3_scr[...] = w3_ref[...].astype(bf16)
        w2_scr[...] = w2_ref[...].astype(bf16)

        def body(g, h):
            @pl.when(g + 2 < n_used)
            def _():
                x_copy(g + 2).start(priority=1)

            @pl.when(g + 1 < n_used)
            def _():
                x_copy(g + 1).wait()

            @pl.when(g >= 2)
            def _():
                y_copy(g - 2).wait()

            h_next = normed((g + 1) % 2)
            a = _dot(h, w1_scr[...])
            c = _dot(h, w3_scr[...])
            hid = (a * jax.nn.sigmoid(a)) * c
            ybuf[g % 2] = _pack_rows(_dot(hid.astype(bf16), w2_scr[...]))
            y_copy(g).start(priority=1)
            return h_next

        h_scr[...] = lax.fori_loop(g0, g0 + nblk, body, h_scr[...])

    @pl.when(e == pl.num_programs(0) - 1)
    def _():
        y_copy(n_used - 2).wait()
        y_copy(n_used - 1).wait()
        ybuf[0] = jnp.zeros(ybuf.shape[1:], u32)

        def fill(b, c):
            r = pl.ds(pl.multiple_of(b * EXPERT_ROWS, EXPERT_ROWS), EXPERT_ROWS)
            cp = pltpu.make_async_copy(ybuf.at[0], ys_hbm.at[r], ysem.at[0])
            cp.start()
            cp.wait()
            return c

        lax.fori_loop(nu_ref[0], n_blocks, fill, 0)


def _experts(layer, pstart, seg_blocks, n_used, xs, w1, w3, w2):
    n_rows = xs.shape[0]
    n_blocks = n_rows // EXPERT_ROWS

    def w_map(e, *_):
        return (layer, e, 0, 0)

    block_buf = pltpu.VMEM((2, EXPERT_ROWS, ROW_SUBLANES, LANES), u32)
    grid_spec = pltpu.PrefetchScalarGridSpec(
        num_scalar_prefetch=3,
        grid=(N_EXPERTS,),
        in_specs=[
            pl.BlockSpec(memory_space=pl.ANY),
            pl.BlockSpec((None, None, D_MODEL, D_EXPERT), w_map),
            pl.BlockSpec((None, None, D_MODEL, D_EXPERT), w_map),
            pl.BlockSpec((None, None, D_EXPERT, D_MODEL), w_map),
        ],
        out_specs=pl.BlockSpec(memory_space=pl.ANY),
        scratch_shapes=[
            block_buf,
            block_buf,
            pltpu.VMEM((EXPERT_ROWS, D_MODEL), bf16),
            pltpu.VMEM((D_MODEL, D_EXPERT), bf16),
            pltpu.VMEM((D_MODEL, D_EXPERT), bf16),
            pltpu.VMEM((D_EXPERT, D_MODEL), bf16),
            pltpu.SemaphoreType.DMA((2,)),
            pltpu.SemaphoreType.DMA((2,)),
        ],
    )
    return pl.pallas_call(
        functools.partial(_expert_kernel, n_blocks),
        grid_spec=grid_spec,
        out_shape=jax.ShapeDtypeStruct((n_rows, ROW_SUBLANES, LANES), u32),
        compiler_params=_params(("arbitrary",)),
        name="experts",
    )(pstart, seg_blocks, n_used, xs, w1, w3, w2)


def _combine_kernel(dest_ref, next_ref, x_ref, w_ref, ys_hbm, out_ref, buf, sem):
    i = pl.program_id(0)
    slot = i % 2

    def gather(idx_ref, s):
        def start(j, c):
            for k in range(TOP_K):
                _row_copy(ys_hbm, idx_ref[k, j], buf.at[s, k], j, sem.at[s]).start(priority=k)
            return c

        lax.fori_loop(0, MOVE_TILE, start, 0, unroll=ISSUE_UNROLL)

    @pl.when(i == 0)
    def _():
        gather(dest_ref, slot)

    @pl.when(i + 1 < pl.num_programs(0))
    def _():
        gather(next_ref, 1 - slot)

    for k in range(TOP_K):
        _wait_rows(buf.at[slot, k], sem.at[slot])
    w = w_ref[...]
    y0 = _unpack_rows(buf[slot, 0])
    y1 = _unpack_rows(buf[slot, 1])
    out_ref[...] = x_ref[...] + (w[:, 0:1] * y0 + w[:, 1:2] * y1)


def _combine(dest_tiles, x2d, w_cols, ys):
    T = x2d.shape[0]
    n_tiles = T // MOVE_TILE
    return pl.pallas_call(
        _combine_kernel,
        grid=(n_tiles,),
        in_specs=[
            pl.BlockSpec((None, TOP_K, MOVE_TILE), lambda i: (i, 0, 0),
                         memory_space=pltpu.SMEM),
            pl.BlockSpec((None, TOP_K, MOVE_TILE),
                         lambda i: (jnp.minimum(i + 1, n_tiles - 1), 0, 0),
                         memory_space=pltpu.SMEM),
            pl.BlockSpec((MOVE_TILE, D_MODEL), lambda i: (i, 0)),
            pl.BlockSpec((MOVE_TILE, 8), lambda i: (i, 0)),
            pl.BlockSpec(memory_space=pl.ANY),
        ],
        out_specs=pl.BlockSpec((MOVE_TILE, D_MODEL), lambda i: (i, 0)),
        out_shape=jax.ShapeDtypeStruct((T, D_MODEL), f32),
        scratch_shapes=[
            pltpu.VMEM((2, TOP_K, MOVE_TILE, ROW_SUBLANES, LANES), u32),
            pltpu.SemaphoreType.DMA((2,)),
        ],
        compiler_params=_params(("arbitrary",)),
        name="combine",
    )(dest_tiles, dest_tiles, x2d, w_cols, ys)


def _moe(layer, x2d, logits_t, m_norm, b_group, b_expert, w1, w3, w2):
    T = x2d.shape[0]
    n_blocks = (T * TOP_K) // EXPERT_ROWS + N_EXPERTS
    n_rows = n_blocks * EXPERT_ROWS

    be = b_expert.reshape(N_EXPERTS, 1)
    bg = jnp.concatenate([b_group, jnp.zeros((8 - N_GROUPS,), f32)]).reshape(8, 1)
    oi, ow, cnt = _route(logits_t, be, bg)

    counts = cnt[:, 0].astype(jnp.int32)
    padded = (counts + EXPERT_ROWS - 1) // EXPERT_ROWS * EXPERT_ROWS
    pend = jnp.cumsum(padded)
    pstart = pend - padded
    n_used = (pend[-1] // EXPERT_ROWS).astype(jnp.int32).reshape(1)
    e_ids = jnp.arange(N_EXPERTS, dtype=jnp.int32)[:, None, None]
    seg_start = jnp.sum(jnp.where(oi[None, 0:2] == e_ids, pstart[:, None, None], 0), axis=0)
    dest = seg_start + oi[2:4]
    dest_tiles = dest.reshape(TOP_K, T // MOVE_TILE, MOVE_TILE).transpose(1, 0, 2)
    w_cols = ow.T

    xs = _dispatch(pend.astype(jnp.int32), padded, n_used, dest_tiles, x2d, m_norm, n_rows)
    ys = _experts(layer, (pstart // EXPERT_ROWS).astype(jnp.int32), padded // EXPERT_ROWS,
                  n_used, xs, w1, w3, w2)
    return _combine(dest_tiles, x2d, w_cols, ys)


def _log_sigmoid(x):
    return jnp.minimum(x, 0.0) - jnp.log1p(jnp.exp(-jnp.abs(x)))


def _proj_kernel(tiles_per_seq, x_ref, kvn_ref, bn_ref, wkt_ref, wv_ref, wf_ref, bf_ref,
                 wq_ref, wg_ref, gq_ref, gk_ref, q_ref, kt_ref, v_ref, sg_ref, carry_scr):
    i = pl.program_id(0)
    x = x_ref[...]
    hkv = _rms(x, kvn_ref[...]).astype(bf16)
    hq = _rms(x, bn_ref[...]).astype(bf16)

    v_ref[...] = _dot(hkv, wv_ref[...]).astype(bf16)
    sg_ref[...] = jax.nn.sigmoid(_dot(hq, wg_ref[...])).astype(bf16)

    logf = _log_sigmoid(_dot(hkv, wf_ref[...]) + bf_ref[...])
    tm = x.shape[0]
    incl = (lax.broadcasted_iota(jnp.int32, (tm, tm), 0)
            >= lax.broadcasted_iota(jnp.int32, (tm, tm), 1)).astype(f32)
    @pl.when(i % tiles_per_seq == 0)
    def _():
        carry_scr[...] = jnp.zeros_like(carry_scr)

    cum = _dot(incl, logf, precision=lax.Precision.HIGHEST) + carry_scr[...]
    carry_scr[...] = cum[tm - 1:tm, :]
    cum = cum * LOG2E
    cum_t = cum.T

    lane = lax.broadcasted_iota(jnp.int32, (tm, HEAD_PAD), 1)
    low = lane < HEAD_DIM
    sub = lax.broadcasted_iota(jnp.int32, (HEAD_DIM, tm), 0)
    q_raw = _dot(hq, wq_ref[...])
    k_raw_t = _dot_nt(wkt_ref[...], hkv)
    scale = HEAD_DIM ** -0.5 * LOG2E

    def q_block(qn, c):
        c_hi, c_mid, c_lo = _split3(c)
        return jnp.where(low, qn,
               jnp.where(lane < AUX_CUM0, 1.0,
               jnp.where(lane == AUX_CUM0, c_hi,
               jnp.where(lane == AUX_CUM0 + 1, c_mid,
               jnp.where(lane == AUX_CUM0 + 2, c_lo, 0.0))))).astype(bf16)

    for p in range(N_HEADS // 2):
        qb = q_raw[:, p * HEAD_PAD:(p + 1) * HEAD_PAD]
        sq = qb * qb
        ms0 = jnp.sum(jnp.where(low, sq, 0.0), axis=-1, keepdims=True) * (1.0 / HEAD_DIM)
        ms1 = jnp.sum(jnp.where(low, 0.0, sq), axis=-1, keepdims=True) * (1.0 / HEAD_DIM)
        inv = jnp.where(low, lax.rsqrt(ms0 + EPS), lax.rsqrt(ms1 + EPS))
        qn = qb * inv * gq_ref[...] * scale
        h0, h1 = 2 * p, 2 * p + 1
        q_ref[:, h0 * HEAD_PAD:(h0 + 1) * HEAD_PAD] = q_block(qn, cum[:, h0:h0 + 1])
        q_ref[:, h1 * HEAD_PAD:(h1 + 1) * HEAD_PAD] = q_block(
            pltpu.roll(qn, HEAD_DIM, 1), cum[:, h1:h1 + 1])

    for h in range(N_HEADS):
        kb = k_raw_t[h * HEAD_DIM:(h + 1) * HEAD_DIM, :]
        k_ms = jnp.sum(kb * kb, axis=0, keepdims=True) * (1.0 / HEAD_DIM)
        kn = kb * lax.rsqrt(k_ms + EPS) * gk_ref[...]
        t_hi, t_mid, t_lo = _split3(cum_t[h:h + 1, :])
        aux = jnp.where(sub == 0, -t_hi,
              jnp.where(sub == 1, -t_mid,
              jnp.where(sub == 2, -t_lo,
              jnp.where(sub < 6, 1.0, 0.0))))
        sl = slice(h * HEAD_PAD, (h + 1) * HEAD_PAD)
        kt_ref[0, sl, :] = jnp.concatenate([kn, aux], axis=0).astype(bf16)


def _proj(x2d, seq, kv_norm, b_norm, wkt, wv, wf, bfv, wq, wg, gq, gk):
    T = x2d.shape[0]
    n_tiles = T // ROW_TILE
    qw = N_HEADS * HEAD_PAD
    return pl.pallas_call(
        functools.partial(_proj_kernel, seq // ROW_TILE),
        grid=(n_tiles,),
        in_specs=[
            pl.BlockSpec((ROW_TILE, D_MODEL), lambda i: (i, 0)),
            _const_spec((1, D_MODEL)),
            _const_spec((1, D_MODEL)),
            _const_spec((D_MODEL, D_MODEL)),
            _const_spec((D_MODEL, D_MODEL)),
            _const_spec((D_MODEL, LANES)),
            _const_spec((1, LANES)),
            _const_spec((D_MODEL, D_MODEL)),
            _const_spec((D_MODEL, D_MODEL)),
            _const_spec((1, HEAD_PAD)),
            _const_spec((HEAD_DIM, 1)),
        ],
        out_specs=[
            pl.BlockSpec((ROW_TILE, qw), lambda i: (i, 0)),
            pl.BlockSpec((1, qw, ROW_TILE), lambda i: (i, 0, 0)),
            pl.BlockSpec((ROW_TILE, D_MODEL), lambda i: (i, 0)),
            pl.BlockSpec((ROW_TILE, D_MODEL), lambda i: (i, 0)),
        ],
        out_shape=[
            jax.ShapeDtypeStruct((T, qw), bf16),
            jax.ShapeDtypeStruct((n_tiles, qw, ROW_TILE), bf16),
            jax.ShapeDtypeStruct((T, D_MODEL), bf16),
            jax.ShapeDtypeStruct((T, D_MODEL), bf16),
        ],
        scratch_shapes=[pltpu.VMEM((1, LANES), f32)],
        compiler_params=_params(("arbitrary",)),
        name="proj",
    )(x2d, kv_norm, b_norm, wkt, wv, wf, bfv, wq, wg, gq, gk)


def _attn_kernel(q_ref, kt_ref, v_ref, o_ref, s_scr, m_scr, l_scr, acc_scr):
    qi = pl.program_id(2)
    tiles = ATT_Q // ATT_K
    groups = ATT_K // LANES
    heads = [slice(hh * HEAD_PAD, (hh + 1) * HEAD_PAD) for hh in range(2)]
    m_scr[...] = jnp.full(m_scr.shape, -jnp.inf, f32)

    def lane_max(mx, s):
        for g in range(groups):
            mx = jnp.maximum(mx, s[:, g * LANES:(g + 1) * LANES])
        return mx

    def probs(s, mb):
        ps = [jnp.exp2(s[:, g * LANES:(g + 1) * LANES] - mb) for g in range(groups)]
        return ps, functools.reduce(lambda a, b: a + b, ps)

    def body_a(trip, c):
        for t in range(tiles):
            kt_idx = trip * tiles + t
            for hh in range(2):
                s = _dot(q_ref[:, heads[hh]], kt_ref[kt_idx, heads[hh], :])
                s_scr[hh, kt_idx] = s
                m_scr[hh] = lane_max(m_scr[hh], s)
        return c

    lax.fori_loop(0, qi, body_a, 0)
    for t in range(tiles):
        kt_idx = qi * tiles + t
        rows = ATT_Q - t * ATT_K
        visible = (lax.broadcasted_iota(jnp.int32, (rows, ATT_K), 1)
                   <= lax.broadcasted_iota(jnp.int32, (rows, ATT_K), 0))
        for hh in range(2):
            s = _dot(q_ref[t * ATT_K:, heads[hh]], kt_ref[kt_idx, heads[hh], :])
            s = jnp.where(visible, s, -jnp.inf)
            s_scr[hh, kt_idx, t * ATT_K:, :] = s
            m_scr[hh, t * ATT_K:, :] = lane_max(m_scr[hh, t * ATT_K:, :], s)

    for hh in range(2):
        row_max = jnp.max(m_scr[hh], axis=-1, keepdims=True)
        m_scr[hh] = jnp.broadcast_to(row_max, (ATT_Q, LANES))
    l_scr[...] = jnp.zeros(l_scr.shape, f32)
    acc_scr[...] = jnp.zeros(acc_scr.shape, f32)

    def body_b(trip, c):
        for hh in range(2):
            mb = m_scr[hh]
            lsum, acc = l_scr[hh], acc_scr[hh]
            for t0 in range(0, tiles, PV_TILES):
                kt_idx = trip * tiles + t0
                row0 = pl.multiple_of(kt_idx * ATT_K, PV_TILES * ATT_K)
                vb = v_ref[pl.ds(row0, PV_TILES * ATT_K), :]
                parts = []
                for t in range(PV_TILES):
                    ps, psum = probs(s_scr[hh, kt_idx + t], mb)
                    parts += ps
                    lsum = lsum + psum
                acc = acc + _dot(jnp.concatenate(parts, axis=1).astype(bf16), vb)
            l_scr[hh] = lsum
            acc_scr[hh] = acc
        return c

    lax.fori_loop(0, qi, body_b, 0)
    for t in range(tiles):
        kt_idx = qi * tiles + t
        row0 = pl.multiple_of(kt_idx * ATT_K, ATT_K)
        vb = v_ref[pl.ds(row0, ATT_K), :]
        for hh in range(2):
            ps, psum = probs(s_scr[hh, kt_idx, t * ATT_K:, :], m_scr[hh, t * ATT_K:, :])
            l_scr[hh, t * ATT_K:, :] += psum
            acc_scr[hh, t * ATT_K:, :] += _dot(jnp.concatenate(ps, axis=1).astype(bf16), vb)

    lane = lax.broadcasted_iota(jnp.int32, (ATT_Q, 2 * HEAD_DIM), 1)
    o0 = acc_scr[0] / jnp.sum(l_scr[0], axis=-1, keepdims=True)
    o1 = acc_scr[1] / jnp.sum(l_scr[1], axis=-1, keepdims=True)
    o_ref[...] = jnp.where(lane < HEAD_DIM, o0, o1).astype(bf16)


def _attention(q, kt, v, batch, seq):
    T = q.shape[0]
    nq = seq // ATT_Q
    nk = seq // ATT_K
    pairs = N_HEADS // 2
    return pl.pallas_call(
        _attn_kernel,
        grid=(batch, pairs, nq),
        in_specs=[
            pl.BlockSpec((ATT_Q, 2 * HEAD_PAD), lambda b, p, i: (b * nq + i, p)),
            pl.BlockSpec((nk, 2 * HEAD_PAD, ATT_K), lambda b, p, i: (b, p, 0)),
            pl.BlockSpec((seq, 2 * HEAD_DIM), lambda b, p, i: (b, p)),
        ],
        out_specs=pl.BlockSpec((ATT_Q, 2 * HEAD_DIM), lambda b, p, i: (b * nq + i, p)),
        out_shape=jax.ShapeDtypeStruct((T, D_MODEL), bf16),
        scratch_shapes=[
            pltpu.VMEM((2, nk, ATT_Q, ATT_K), f32),
            pltpu.VMEM((2, ATT_Q, LANES), f32),
            pltpu.VMEM((2, ATT_Q, LANES), f32),
            pltpu.VMEM((2, ATT_Q, 2 * HEAD_DIM), f32),
        ],
        compiler_params=_params(("parallel", "parallel", "parallel")),
        name="attn",
    )(q, kt, v)


def _attn_out_kernel(x_ref, o_ref, sg_ref, wo_ref, mn_ref, wr_ref, x3_ref, lg_ref):
    gated = (o_ref[...].astype(f32) * sg_ref[...].astype(f32)).astype(bf16)
    x3 = x_ref[...] + _dot(gated, wo_ref[...])
    x3_ref[...] = x3
    lg_ref[...] = _router_logits_t(x3, mn_ref[...], wr_ref[...])


def _attn_out(x2d, o, sg, wo, m_norm, w_router):
    T = x2d.shape[0]
    return pl.pallas_call(
        _attn_out_kernel,
        grid=(T // ROW_TILE,),
        in_specs=[
            pl.BlockSpec((ROW_TILE, D_MODEL), lambda i: (i, 0)),
            pl.BlockSpec((ROW_TILE, D_MODEL), lambda i: (i, 0)),
            pl.BlockSpec((ROW_TILE, D_MODEL), lambda i: (i, 0)),
            _const_spec((D_MODEL, D_MODEL)),
            _const_spec((1, D_MODEL)),
            _const_spec((3 * D_MODEL, LANES)),
        ],
        out_specs=[
            pl.BlockSpec((ROW_TILE, D_MODEL), lambda i: (i, 0)),
            pl.BlockSpec((ROUTER_ROWS, ROW_TILE), lambda i: (0, i)),
        ],
        out_shape=[
            jax.ShapeDtypeStruct((T, D_MODEL), f32),
            jax.ShapeDtypeStruct((ROUTER_ROWS, T), f32),
        ],
        compiler_params=_params(("parallel",)),
        name="attn_out",
    )(x2d, o, sg, wo, m_norm, w_router)


def _router_weight(w_group, w_expert):
    pad = jnp.zeros((D_MODEL, LANES - N_EXPERTS - N_GROUPS), f32)
    w = jnp.concatenate([w_expert, w_group, pad], axis=1)
    w_hi = w.astype(bf16)
    w_lo = (w - w_hi.astype(f32)).astype(bf16)
    return jnp.concatenate([w_hi, w_lo, w_hi], axis=0)


def kernel(x, a_norm, a_w_in, a_b_in, a_v_norm, a_w_s, a_b_s, a_w_out, kv_norm, kv_w, kv_b_f,
           k_norm, b_norm, b_w_qg, q_norm, b_w_out, m_norm, m_w_group, m_b_group, m_w_expert,
           m_b_expert, m_w1, m_w3, m_w2):
    batch, seq, _ = x.shape
    T = batch * seq
    x2d = x.reshape(T, D_MODEL)

    x1, lg0 = _sgu(
        x2d, a_norm[0].reshape(1, -1), a_w_in[0].astype(bf16), a_b_in[0].reshape(1, -1),
        a_v_norm[0].reshape(1, -1), a_w_s[0], a_b_s[0].T, a_w_out[0].astype(bf16),
        m_norm[0].reshape(1, -1), _router_weight(m_w_group[0], m_w_expert[0]))
    x2 = _moe(0, x1, lg0, m_norm[0].reshape(1, -1), m_b_group[0], m_b_expert[0],
              m_w1, m_w3, m_w2)

    wk = kv_w[:, :D_MODEL]
    wv = kv_w[:, D_MODEL:2 * D_MODEL]
    wf = jnp.pad(kv_w[:, 2 * D_MODEL:], ((0, 0), (0, LANES - N_HEADS)))
    bfv = jnp.pad(kv_b_f, (0, LANES - N_HEADS)).reshape(1, LANES)
    wq = b_w_qg[0][:, :D_MODEL]
    wg = b_w_qg[0][:, D_MODEL:]
    gq = jnp.tile(q_norm[0], 2).reshape(1, HEAD_PAD)
    gk = k_norm.reshape(HEAD_DIM, 1)
    q, kt, v, sg = _proj(
        x2, seq, kv_norm.reshape(1, -1), b_norm[0].reshape(1, -1),
        wk.T.astype(bf16), wv.astype(bf16), wf.astype(bf16), bfv,
        wq.astype(bf16), wg.astype(bf16), gq, gk)
    o = _attention(q, kt, v, batch, seq)
    x3, lg1 = _attn_out(x2, o, sg, b_w_out[0].astype(bf16), m_norm[1].reshape(1, -1),
                        _router_weight(m_w_group[1], m_w_expert[1]))
    x4 = _moe(1, x3, lg1, m_norm[1].reshape(1, -1), m_b_group[1], m_b_expert[1],
              m_w1, m_w3, m_w2)
    return x4.reshape(batch, seq, D_MODEL)
```

```python
import functools
import math

import jax
import jax.numpy as jnp
import numpy as np
from jax import lax
from jax.experimental import pallas as pl
from jax.experimental.pallas import tpu as pltpu

D_MODEL = 1024
EPS = 1e-6
SGU_BLOCK = 128
SGU_CHUNK = 64
CHUNK_SHIFT = SGU_CHUNK.bit_length() - 1
SGU_WIDTH = 2 * D_MODEL
SGU_GROUPS = 8
SGU_GROUP_DIM = SGU_WIDTH // SGU_GROUPS
N_HEADS = 16
HEAD_DIM = D_MODEL // N_HEADS
N_GROUPS = 4
EXPERTS_PER_GROUP = 8
N_EXPERTS = N_GROUPS * EXPERTS_PER_GROUP
GROUP_SHIFT = EXPERTS_PER_GROUP.bit_length() - 1
TOP_K = 2
D_EXPERT = D_MODEL // 2

LANES = 128
ROW_SUBLANES = D_MODEL // 2 // LANES
ISSUE_UNROLL = 8
VMEM_LIMIT_BYTES = 56 * 1024 * 1024

ROW_TILE = 256
SGU_TILE = 512
ROUTE_TILE = 512
EXPERT_ROWS = 512
MOVE_TILE = 512
MOVE_CHUNK = 64
ATT_Q = 1024
ATT_K = ROW_TILE
PV_TILES = 2
HEAD_PAD = 2 * HEAD_DIM
ROUTER_ROWS = 40
AUX_ONE0 = HEAD_DIM
AUX_CUM0 = HEAD_DIM + 3

LOG2E = math.log2(math.e)

bf16 = jnp.bfloat16
f32 = jnp.float32
u32 = jnp.uint32
HI_HALF = np.uint32(0xFFFF0000)


def _dot(a, b, precision=None):
    return jnp.dot(a, b, preferred_element_type=f32, precision=precision)


def _dot_nt(a, b, precision=None):
    return lax.dot_general(a, b, (((1,), (1,)), ((), ())), preferred_element_type=f32,
                           precision=precision)


def _rms(x, g):
    ms = jnp.mean(x * x, axis=-1, keepdims=True)
    return x * lax.rsqrt(ms + EPS) * g


def _gelu_tanh(z):
    c = math.sqrt(2.0 / math.pi)
    return z * (0.5 * (1.0 + jnp.tanh(c * (z + 0.044715 * (z * z * z)))))


def _split3(c):
    hi = c.astype(bf16).astype(f32)
    r = c - hi
    mid = r.astype(bf16).astype(f32)
    lo = r - mid
    return hi, mid, lo


def _pack_rows(v):
    half = D_MODEL // 2
    lo = lax.bitcast_convert_type(v[:, :half].astype(bf16).astype(f32), u32) >> 16
    hi = lax.bitcast_convert_type(v[:, half:].astype(bf16).astype(f32), u32) & HI_HALF
    return (hi | lo).reshape(v.shape[0], ROW_SUBLANES, LANES)


def _unpack_rows(w):
    w = w.reshape(w.shape[0], D_MODEL // 2)
    lo = lax.bitcast_convert_type(w << 16, f32)
    hi = lax.bitcast_convert_type(w & HI_HALF, f32)
    return jnp.concatenate([lo, hi], axis=1)


def _const_spec(shape, single=False):
    nd = len(shape)
    mode = pl.Buffered(1) if single else None
    return pl.BlockSpec(shape, lambda *_: (0,) * nd, pipeline_mode=mode)


def _params(sem):
    return pltpu.CompilerParams(dimension_semantics=sem, vmem_limit_bytes=VMEM_LIMIT_BYTES)


def _router_logits_t(x_new, mn, wr3):
    hm = _rms(x_new, mn)
    h_hi = hm.astype(bf16)
    h_lo = (hm - h_hi.astype(f32)).astype(bf16)
    lg = _dot(jnp.concatenate([h_hi, h_hi, h_lo], axis=1), wr3)
    return lg.T[:ROUTER_ROWS, :]


def _sgu_kernel(x_ref, an_ref, win_ref, bin_ref, vn_ref, ws_ref, bst_ref, wout_ref, mn_ref,
                wr_ref, x1_ref, lg_ref, u_scr, v_scr, gated_scr):
    x = x_ref[...]
    h = _rms(x, an_ref[...]).astype(bf16)
    u_scr[...] = _gelu_tanh(_dot(h, win_ref[:, :SGU_WIDTH]) + bin_ref[:, :SGU_WIDTH])
    v = _gelu_tanh(_dot(h, win_ref[:, SGU_WIDTH:]) + bin_ref[:, SGU_WIDTH:])
    v_scr[...] = _rms(v, vn_ref[...]).astype(bf16)

    t_chunk = lax.broadcasted_iota(jnp.int32, (SGU_BLOCK, SGU_BLOCK), 0) >> CHUNK_SHIFT
    s_chunk = lax.broadcasted_iota(jnp.int32, (SGU_BLOCK, SGU_BLOCK), 1) >> CHUNK_SHIFT
    causal = t_chunk >= s_chunk
    for g in range(SGU_GROUPS):
        wsm = jnp.where(causal, ws_ref[g], 0.0).astype(bf16)
        cols = slice(g * SGU_GROUP_DIM, (g + 1) * SGU_GROUP_DIM)
        for sb in range(SGU_TILE // SGU_BLOCK):
            rows = slice(sb * SGU_BLOCK, (sb + 1) * SGU_BLOCK)
            mixed = _dot(wsm, v_scr[rows, cols]) + bst_ref[:, g:g + 1]
            gated_scr[rows, cols] = (u_scr[rows, cols] * mixed).astype(bf16)

    x1 = x + _dot(gated_scr[...], wout_ref[...])
    x1_ref[...] = x1
    lg_ref[...] = _router_logits_t(x1, mn_ref[...], wr_ref[...])


def _sgu(x2d, a_norm, w_in, b_in, v_norm, w_s, b_s_t, w_out, m_norm, w_router):
    T = x2d.shape[0]
    return pl.pallas_call(
        _sgu_kernel,
        grid=(T // SGU_TILE,),
        in_specs=[
            pl.BlockSpec((SGU_TILE, D_MODEL), lambda i: (i, 0)),
            _const_spec((1, D_MODEL)),
            _const_spec((D_MODEL, 2 * SGU_WIDTH), single=True),
            _const_spec((1, 2 * SGU_WIDTH)),
            _const_spec((1, SGU_WIDTH)),
            _const_spec((SGU_GROUPS, SGU_BLOCK, SGU_BLOCK)),
            _const_spec((SGU_BLOCK, SGU_GROUPS)),
            _const_spec((SGU_WIDTH, D_MODEL), single=True),
            _const_spec((1, D_MODEL)),
            _const_spec((3 * D_MODEL, LANES)),
        ],
        out_specs=[
            pl.BlockSpec((SGU_TILE, D_MODEL), lambda i: (i, 0)),
            pl.BlockSpec((ROUTER_ROWS, SGU_TILE), lambda i: (0, i)),
        ],
        out_shape=[
            jax.ShapeDtypeStruct((T, D_MODEL), f32),
            jax.ShapeDtypeStruct((ROUTER_ROWS, T), f32),
        ],
        scratch_shapes=[
            pltpu.VMEM((SGU_TILE, SGU_WIDTH), f32),
            pltpu.VMEM((SGU_TILE, SGU_WIDTH), bf16),
            pltpu.VMEM((SGU_TILE, SGU_WIDTH), bf16),
        ],
        compiler_params=_params(("parallel",)),
        name="sgu",
    )(x2d, a_norm, w_in, b_in, v_norm, w_s, b_s_t, w_out, m_norm, w_router)


def _route_kernel(lg_ref, be_ref, bg_ref, oi_ref, ow_ref, cnt_ref, carry_scr):
    step = pl.program_id(0)

    @pl.when(step == 0)
    def _():
        carry_scr[...] = jnp.zeros_like(carry_scr)

    lg = lg_ref[...]
    e_l = lg[0:N_EXPERTS, :] + be_ref[...]
    g_l = lg[N_EXPERTS:N_EXPERTS + 8, :] + bg_ref[...]
    g_row = lax.broadcasted_iota(jnp.int32, g_l.shape, 0).astype(f32)
    g_l = jnp.where(g_row < N_GROUPS, g_l, -jnp.inf)
    g_max = jnp.max(g_l, axis=0, keepdims=True)
    g_sel = jnp.min(jnp.where(g_l == g_max, g_row, 8.0), axis=0, keepdims=True)
    g_den = jnp.sum(jnp.exp(g_l - g_max), axis=0, keepdims=True)
    g_w = 1.0 / g_den

    e_row_i = lax.broadcasted_iota(jnp.int32, e_l.shape, 0)
    e_row = e_row_i.astype(f32)
    e_grp = (e_row_i >> GROUP_SHIFT).astype(f32)
    e_in = jnp.where(e_grp == g_sel, e_l, -jnp.inf)
    m1 = jnp.max(e_in, axis=0, keepdims=True)
    i1 = jnp.min(jnp.where(e_in == m1, e_row, float(N_EXPERTS)), axis=0, keepdims=True)
    e_in2 = jnp.where(e_row == i1, -jnp.inf, e_in)
    m2 = jnp.max(e_in2, axis=0, keepdims=True)
    i2 = jnp.min(jnp.where(e_in2 == m2, e_row, float(N_EXPERTS)), axis=0, keepdims=True)
    t = jnp.exp(m2 - m1)
    p1 = 1.0 / (1.0 + t)
    p2 = t / (1.0 + t)

    hit1 = e_row == i1
    hit2 = e_row == i2
    member = jnp.logical_or(hit1, hit2)
    tt = lg.shape[1]
    before = (lax.broadcasted_iota(jnp.int32, (tt, tt), 0)
              < lax.broadcasted_iota(jnp.int32, (tt, tt), 1))
    prefix = _dot(member.astype(bf16), before.astype(bf16))
    rank_full = prefix + carry_scr[...]
    rank1 = jnp.sum(jnp.where(hit1, rank_full, 0.0), axis=0, keepdims=True)
    rank2 = jnp.sum(jnp.where(hit2, rank_full, 0.0), axis=0, keepdims=True)
    carry_new = carry_scr[...] + jnp.sum(member.astype(f32), axis=1, keepdims=True)
    carry_scr[...] = carry_new

    oi_ref[...] = jnp.zeros(oi_ref.shape, jnp.int32)
    oi_ref[0:1, :] = i1.astype(jnp.int32)
    oi_ref[1:2, :] = i2.astype(jnp.int32)
    oi_ref[2:3, :] = rank1.astype(jnp.int32)
    oi_ref[3:4, :] = rank2.astype(jnp.int32)
    ow_ref[...] = jnp.zeros(ow_ref.shape, f32)
    ow_ref[0:1, :] = g_w * p1
    ow_ref[1:2, :] = g_w * p2
    cnt_ref[...] = jnp.broadcast_to(carry_new, cnt_ref.shape)


def _route(logits_t, b_expert, b_group):
    T = logits_t.shape[1]
    return pl.pallas_call(
        _route_kernel,
        grid=(T // ROUTE_TILE,),
        in_specs=[
            pl.BlockSpec((ROUTER_ROWS, ROUTE_TILE), lambda i: (0, i)),
            _const_spec((N_EXPERTS, 1)),
            _const_spec((8, 1)),
        ],
        out_specs=[
            pl.BlockSpec((8, ROUTE_TILE), lambda i: (0, i)),
            pl.BlockSpec((8, ROUTE_TILE), lambda i: (0, i)),
            _const_spec((N_EXPERTS, LANES)),
        ],
        out_shape=[
            jax.ShapeDtypeStruct((8, T), jnp.int32),
            jax.ShapeDtypeStruct((8, T), f32),
            jax.ShapeDtypeStruct((N_EXPERTS, LANES), f32),
        ],
        scratch_shapes=[pltpu.VMEM((N_EXPERTS, 1), f32)],
        compiler_params=_params(("arbitrary",)),
        name="route",
    )(logits_t, b_expert, b_group)


def _row_copy(src, s, dst, d, sem):
    return pltpu.make_async_copy(src.at[s], dst.at[d], sem)


def _wait_rows(buf, sem):
    pltpu.make_async_copy(buf, buf, sem).wait()


def _dispatch_kernel(n_blocks, pend_ref, padded_ref, nu_ref, dest_ref, x_ref, mn_ref, xs_hbm,
                     zero_scr, x3_scr, zsem, sem):
    @pl.when(pl.program_id(0) == 0)
    def _():
        zero_scr[...] = jnp.zeros_like(zero_scr)

        def fill(row0):
            return pltpu.make_async_copy(zero_scr, xs_hbm.at[pl.ds(row0, EXPERT_ROWS)], zsem)

        for e in range(N_EXPERTS):
            @pl.when(padded_ref[e] > 0)
            def _():
                fill(pend_ref[e] - EXPERT_ROWS).start()

            @pl.when(nu_ref[0] + e < n_blocks)
            def _():
                fill((nu_ref[0] + e) * EXPERT_ROWS).start()

        for e in range(N_EXPERTS):
            @pl.when(padded_ref[e] > 0)
            def _():
                fill(0).wait()

            @pl.when(nu_ref[0] + e < n_blocks)
            def _():
                fill(0).wait()

    i = pl.program_id(0)
    n_tiles = pl.num_programs(0) - 1

    def staged(rows):
        return _pack_rows(_rms(x_ref[rows, :], mn_ref[...]))

    @pl.when(i == 0)
    def _():
        x3_scr[0] = staged(slice(None))

    @pl.when(i > 0)
    def _():
        cur = (i - 1) % 3
        nxt = i % 3

        def chunk(c, carry):
            r0 = pl.multiple_of(c * MOVE_CHUNK, MOVE_CHUNK)
            for j in range(MOVE_CHUNK):
                for k in range(TOP_K):
                    _row_copy(x3_scr.at[cur], r0 + j, xs_hbm, dest_ref[k, r0 + j],
                              sem.at[cur]).start(priority=k)
            x3_scr[nxt, pl.ds(r0, MOVE_CHUNK)] = staged(pl.ds(r0, MOVE_CHUNK))
            return carry

        lax.fori_loop(0, MOVE_TILE // MOVE_CHUNK, chunk, 0)

        @pl.when(i > 1)
        def _():
            for k in range(TOP_K):
                _wait_rows(x3_scr.at[(i - 2) % 3], sem.at[(i - 2) % 3])

        @pl.when(i == n_tiles)
        def _():
            for k in range(TOP_K):
                _wait_rows(x3_scr.at[cur], sem.at[cur])


def _dispatch(pend, padded, n_used, dest_tiles, x2d, m_norm, n_rows):
    T = x2d.shape[0]
    n_blocks = n_rows // EXPERT_ROWS
    n_tiles = T // MOVE_TILE
    grid_spec = pltpu.PrefetchScalarGridSpec(
        num_scalar_prefetch=3,
        grid=(n_tiles + 1,),
        in_specs=[
            pl.BlockSpec((None, TOP_K, MOVE_TILE), lambda i, *_: (jnp.maximum(i - 1, 0), 0, 0),
                         memory_space=pltpu.SMEM),
            pl.BlockSpec((MOVE_TILE, D_MODEL), lambda i, *_: (jnp.minimum(i, n_tiles - 1), 0)),
            pl.BlockSpec((1, D_MODEL), lambda i, *_: (0, 0)),
        ],
        out_specs=pl.BlockSpec(memory_space=pl.ANY),
        scratch_shapes=[
            pltpu.VMEM((EXPERT_ROWS, ROW_SUBLANES, LANES), u32),
            pltpu.VMEM((3, MOVE_TILE, ROW_SUBLANES, LANES), u32),
            pltpu.SemaphoreType.DMA(()),
            pltpu.SemaphoreType.DMA((3,)),
        ],
    )
    return pl.pallas_call(
        functools.partial(_dispatch_kernel, n_blocks),
        grid_spec=grid_spec,
        out_shape=jax.ShapeDtypeStruct((n_rows, ROW_SUBLANES, LANES), u32),
        compiler_params=_params(("arbitrary",)),
        name="dispatch",
    )(pend, padded, n_used, dest_tiles, x2d, m_norm)


def _expert_kernel(n_blocks, ps_ref, nb_ref, nu_ref, xs_hbm, w1_ref, w3_ref, w2_ref,
                   ys_hbm, xbuf, ybuf, h_scr, w1_scr, w3_scr, w2_scr, xsem, ysem):
    e = pl.program_id(0)
    nblk = nb_ref[e]
    g0 = ps_ref[e]
    n_used = nu_ref[0]

    def rows(g):
        return pl.ds(pl.multiple_of(g * EXPERT_ROWS, EXPERT_ROWS), EXPERT_ROWS)

    def x_copy(g):
        return pltpu.make_async_copy(xs_hbm.at[rows(g)], xbuf.at[g % 2], xsem.at[g % 2])

    def y_copy(g):
        return pltpu.make_async_copy(ybuf.at[g % 2], ys_hbm.at[rows(g)], ysem.at[g % 2])

    def normed(slot):
        return _unpack_rows(xbuf[slot]).astype(bf16)

    @pl.when(e == 0)
    def _():
        x_copy(0).start(priority=1)
        x_copy(1).start(priority=1)
        x_copy(0).wait()
        h_scr[...] = normed(0)

    @pl.when(nblk > 0)
    def _():
        w1_scr[...] = w1_ref[...].astype(bf16)
        w3_scr[...] = w3_ref[...].astype(bf16)
        w2_scr[...] = w2_ref[...].astype(bf16)

        def body(g, h):
            @pl.when(g + 2 < n_used)
            def _():
                x_copy(g + 2).start(priority=1)

            @pl.when(g + 1 < n_used)
            def _():
                x_copy(g + 1).wait()

            @pl.when(g >= 2)
            def _():
                y_copy(g - 2).wait()

            h_next = normed((g + 1) % 2)
            a = _dot(h, w1_scr[...])
            c = _dot(h, w3_scr[...])
            hid = (a * jax.nn.sigmoid(a)) * c
            ybuf[g % 2] = _pack_rows(_dot(hid.astype(bf16), w2_scr[...]))
            y_copy(g).start(priority=1)
            return h_next

        h_scr[...] = lax.fori_loop(g0, g0 + nblk, body, h_scr[...])

    @pl.when(e == pl.num_programs(0) - 1)
    def _():
        y_copy(n_used - 2).wait()
        y_copy(n_used - 1).wait()
        ybuf[0] = jnp.zeros(ybuf.shape[1:], u32)

        def fill(b, c):
            r = pl.ds(pl.multiple_of(b * EXPERT_ROWS, EXPERT_ROWS), EXPERT_ROWS)
            cp = pltpu.make_async_copy(ybuf.at[0], ys_hbm.at[r], ysem.at[0])
            cp.start()
            cp.wait()
            return c

        lax.fori_loop(nu_ref[0], n_blocks, fill, 0)


def _experts(layer, pstart, seg_blocks, n_used, xs, w1, w3, w2):
    n_rows = xs.shape[0]
    n_blocks = n_rows // EXPERT_ROWS

    def w_map(e, *_):
        return (layer, e, 0, 0)

    block_buf = pltpu.VMEM((2, EXPERT_ROWS, ROW_SUBLANES, LANES), u32)
    grid_spec = pltpu.PrefetchScalarGridSpec(
        num_scalar_prefetch=3,
        grid=(N_EXPERTS,),
        in_specs=[
            pl.BlockSpec(memory_space=pl.ANY),
            pl.BlockSpec((None, None, D_MODEL, D_EXPERT), w_map),
            pl.BlockSpec((None, None, D_MODEL, D_EXPERT), w_map),
            pl.BlockSpec((None, None, D_EXPERT, D_MODEL), w_map),
        ],
        out_specs=pl.BlockSpec(memory_space=pl.ANY),
        scratch_shapes=[
            block_buf,
            block_buf,
            pltpu.VMEM((EXPERT_ROWS, D_MODEL), bf16),
            pltpu.VMEM((D_MODEL, D_EXPERT), bf16),
            pltpu.VMEM((D_MODEL, D_EXPERT), bf16),
            pltpu.VMEM((D_EXPERT, D_MODEL), bf16),
            pltpu.SemaphoreType.DMA((2,)),
            pltpu.SemaphoreType.DMA((2,)),
        ],
    )
    return pl.pallas_call(
        functools.partial(_expert_kernel, n_blocks),
        grid_spec=grid_spec,
        out_shape=jax.ShapeDtypeStruct((n_rows, ROW_SUBLANES, LANES), u32),
        compiler_params=_params(("arbitrary",)),
        name="experts",
    )(pstart, seg_blocks, n_used, xs, w1, w3, w2)


def _combine_kernel(dest_ref, next_ref, x_ref, w_ref, ys_hbm, out_ref, buf, sem):
    i = pl.program_id(0)
    slot = i % 2
    other = 1 - slot

    @pl.when(i == 0)
    def _():
        def start(j, c):
            for k in range(TOP_K):
                _row_copy(ys_hbm, dest_ref[k, j], buf.at[slot, k], j,
                          sem.at[slot]).start(priority=k)
            return c

        lax.fori_loop(0, MOVE_TILE, start, 0, unroll=ISSUE_UNROLL)

    for k in range(TOP_K):
        _wait_rows(buf.at[slot, k], sem.at[slot])

    def chunk(c, carry):
        r0 = pl.multiple_of(c * MOVE_CHUNK, MOVE_CHUNK)
        rows = pl.ds(r0, MOVE_CHUNK)
        for j in range(MOVE_CHUNK):
            for k in range(TOP_K):
                _row_copy(ys_hbm, next_ref[k, r0 + j], buf.at[other, k], r0 + j,
                          sem.at[other]).start(priority=k)
        w = w_ref[rows, :]
        y0 = _unpack_rows(buf[slot, 0, rows])
        y1 = _unpack_rows(buf[slot, 1, rows])
        out_ref[rows, :] = x_ref[rows, :] + (w[:, 0:1] * y0 + w[:, 1:2] * y1)
        return carry

    lax.fori_loop(0, MOVE_TILE // MOVE_CHUNK, chunk, 0)

    @pl.when(i == pl.num_programs(0) - 1)
    def _():
        for k in range(TOP_K):
            _wait_rows(buf.at[other, k], sem.at[other])


def _combine(dest_tiles, x2d, w_cols, ys):
    T = x2d.shape[0]
    n_tiles = T // MOVE_TILE
    return pl.pallas_call(
        _combine_kernel,
        grid=(n_tiles,),
        in_specs=[
            pl.BlockSpec((None, TOP_K, MOVE_TILE), lambda i: (i, 0, 0),
                         memory_space=pltpu.SMEM),
            pl.BlockSpec((None, TOP_K, MOVE_TILE),
                         lambda i: (jnp.minimum(i + 1, n_tiles - 1), 0, 0),
                         memory_space=pltpu.SMEM),
            pl.BlockSpec((MOVE_TILE, D_MODEL), lambda i: (i, 0)),
            pl.BlockSpec((MOVE_TILE, 8), lambda i: (i, 0)),
            pl.BlockSpec(memory_space=pl.ANY),
        ],
        out_specs=pl.BlockSpec((MOVE_TILE, D_MODEL), lambda i: (i, 0)),
        out_shape=jax.ShapeDtypeStruct((T, D_MODEL), f32),
        scratch_shapes=[
            pltpu.VMEM((2, TOP_K, MOVE_TILE, ROW_SUBLANES, LANES), u32),
            pltpu.SemaphoreType.DMA((2,)),
        ],
        compiler_params=_params(("arbitrary",)),
        name="combine",
    )(dest_tiles, dest_tiles, x2d, w_cols, ys)


def _moe(layer, x2d, logits_t, m_norm, b_group, b_expert, w1, w3, w2):
    T = x2d.shape[0]
    n_blocks = (T * TOP_K) // EXPERT_ROWS + N_EXPERTS
    n_rows = n_blocks * EXPERT_ROWS

    be = b_expert.reshape(N_EXPERTS, 1)
    bg = jnp.concatenate([b_group, jnp.zeros((8 - N_GROUPS,), f32)]).reshape(8, 1)
    oi, ow, cnt = _route(logits_t, be, bg)

    counts = cnt[:, 0].astype(jnp.int32)
    padded = (counts + EXPERT_ROWS - 1) // EXPERT_ROWS * EXPERT_ROWS
    pend = jnp.cumsum(padded)
    pstart = pend - padded
    n_used = (pend[-1] // EXPERT_ROWS).astype(jnp.int32).reshape(1)
    e_ids = jnp.arange(N_EXPERTS, dtype=jnp.int32)[:, None, None]
    seg_start = jnp.sum(jnp.where(oi[None, 0:2] == e_ids, pstart[:, None, None], 0), axis=0)
    dest = seg_start + oi[2:4]
    dest_tiles = dest.reshape(TOP_K, T // MOVE_TILE, MOVE_TILE).transpose(1, 0, 2)
    w_cols = ow.T

    xs = _dispatch(pend.astype(jnp.int32), padded, n_used, dest_tiles, x2d, m_norm, n_rows)
    ys = _experts(layer, (pstart // EXPERT_ROWS).astype(jnp.int32), padded // EXPERT_ROWS,
                  n_used, xs, w1, w3, w2)
    return _combine(dest_tiles, x2d, w_cols, ys)


def _log_sigmoid(x):
    return jnp.minimum(x, 0.0) - jnp.log1p(jnp.exp(-jnp.abs(x)))


def _proj_kernel(tiles_per_seq, x_ref, kvn_ref, bn_ref, wkt_ref, wv_ref, wf_ref, bf_ref,
                 wq_ref, wg_ref, gq_ref, gk_ref, q_ref, kt_ref, v_ref, sg_ref, carry_scr):
    i = pl.program_id(0)
    x = x_ref[...]
    hkv = _rms(x, kvn_ref[...]).astype(bf16)
    hq = _rms(x, bn_ref[...]).astype(bf16)

    v_ref[...] = _dot(hkv, wv_ref[...]).astype(bf16)
    sg_ref[...] = jax.nn.sigmoid(_dot(hq, wg_ref[...])).astype(bf16)

    logf = _log_sigmoid(_dot(hkv, wf_ref[...]) + bf_ref[...])
    tm = x.shape[0]
    incl = (lax.broadcasted_iota(jnp.int32, (tm, tm), 0)
            >= lax.broadcasted_iota(jnp.int32, (tm, tm), 1)).astype(f32)
    @pl.when(i % tiles_per_seq == 0)
    def _():
        carry_scr[...] = jnp.zeros_like(carry_scr)

    cum = _dot(incl, logf, precision=lax.Precision.HIGHEST) + carry_scr[...]
    carry_scr[...] = cum[tm - 1:tm, :]
    cum = cum * LOG2E
    cum_t = cum.T

    lane = lax.broadcasted_iota(jnp.int32, (tm, HEAD_PAD), 1)
    low = lane < HEAD_DIM
    sub = lax.broadcasted_iota(jnp.int32, (HEAD_DIM, tm), 0)
    q_raw = _dot(hq, wq_ref[...])
    k_raw_t = _dot_nt(wkt_ref[...], hkv)
    scale = HEAD_DIM ** -0.5 * LOG2E

    def q_block(qn, c):
        c_hi, c_mid, c_lo = _split3(c)
        return jnp.where(low, qn,
               jnp.where(lane < AUX_CUM0, 1.0,
               jnp.where(lane == AUX_CUM0, c_hi,
               jnp.where(lane == AUX_CUM0 + 1, c_mid,
               jnp.where(lane == AUX_CUM0 + 2, c_lo, 0.0))))).astype(bf16)

    for p in range(N_HEADS // 2):
        qb = q_raw[:, p * HEAD_PAD:(p + 1) * HEAD_PAD]
        sq = qb * qb
        ms0 = jnp.sum(jnp.where(low, sq, 0.0), axis=-1, keepdims=True) * (1.0 / HEAD_DIM)
        ms1 = jnp.sum(jnp.where(low, 0.0, sq), axis=-1, keepdims=True) * (1.0 / HEAD_DIM)
        inv = jnp.where(low, lax.rsqrt(ms0 + EPS), lax.rsqrt(ms1 + EPS))
        qn = qb * inv * gq_ref[...] * scale
        h0, h1 = 2 * p, 2 * p + 1
        q_ref[:, h0 * HEAD_PAD:(h0 + 1) * HEAD_PAD] = q_block(qn, cum[:, h0:h0 + 1])
        q_ref[:, h1 * HEAD_PAD:(h1 + 1) * HEAD_PAD] = q_block(
            pltpu.roll(qn, HEAD_DIM, 1), cum[:, h1:h1 + 1])

    for h in range(N_HEADS):
        kb = k_raw_t[h * HEAD_DIM:(h + 1) * HEAD_DIM, :]
        k_ms = jnp.sum(kb * kb, axis=0, keepdims=True) * (1.0 / HEAD_DIM)
        kn = kb * lax.rsqrt(k_ms + EPS) * gk_ref[...]
        t_hi, t_mid, t_lo = _split3(cum_t[h:h + 1, :])
        aux = jnp.where(sub == 0, -t_hi,
              jnp.where(sub == 1, -t_mid,
              jnp.where(sub == 2, -t_lo,
              jnp.where(sub < 6, 1.0, 0.0))))
        sl = slice(h * HEAD_PAD, (h + 1) * HEAD_PAD)
        kt_ref[0, sl, :] = jnp.concatenate([kn, aux], axis=0).astype(bf16)


def _proj(x2d, seq, kv_norm, b_norm, wkt, wv, wf, bfv, wq, wg, gq, gk):
    T = x2d.shape[0]
    n_tiles = T // ROW_TILE
    qw = N_HEADS * HEAD_PAD
    return pl.pallas_call(
        functools.partial(_proj_kernel, seq // ROW_TILE),
        grid=(n_tiles,),
        in_specs=[
            pl.BlockSpec((ROW_TILE, D_MODEL), lambda i: (i, 0)),
            _const_spec((1, D_MODEL)),
            _const_spec((1, D_MODEL)),
            _const_spec((D_MODEL, D_MODEL)),
            _const_spec((D_MODEL, D_MODEL)),
            _const_spec((D_MODEL, LANES)),
            _const_spec((1, LANES)),
            _const_spec((D_MODEL, D_MODEL)),
            _const_spec((D_MODEL, D_MODEL)),
            _const_spec((1, HEAD_PAD)),
            _const_spec((HEAD_DIM, 1)),
        ],
        out_specs=[
            pl.BlockSpec((ROW_TILE, qw), lambda i: (i, 0)),
            pl.BlockSpec((1, qw, ROW_TILE), lambda i: (i, 0, 0)),
            pl.BlockSpec((ROW_TILE, D_MODEL), lambda i: (i, 0)),
            pl.BlockSpec((ROW_TILE, D_MODEL), lambda i: (i, 0)),
        ],
        out_shape=[
            jax.ShapeDtypeStruct((T, qw), bf16),
            jax.ShapeDtypeStruct((n_tiles, qw, ROW_TILE), bf16),
            jax.ShapeDtypeStruct((T, D_MODEL), bf16),
            jax.ShapeDtypeStruct((T, D_MODEL), bf16),
        ],
        scratch_shapes=[pltpu.VMEM((1, LANES), f32)],
        compiler_params=_params(("arbitrary",)),
        name="proj",
    )(x2d, kv_norm, b_norm, wkt, wv, wf, bfv, wq, wg, gq, gk)


def _attn_kernel(q_ref, kt_ref, v_ref, o_ref, s_scr, m_scr, l_scr, acc_scr):
    qi = pl.program_id(2)
    tiles = ATT_Q // ATT_K
    groups = ATT_K // LANES
    heads = [slice(hh * HEAD_PAD, (hh + 1) * HEAD_PAD) for hh in range(2)]
    m_scr[...] = jnp.full(m_scr.shape, -jnp.inf, f32)

    def lane_max(mx, s):
        for g in range(groups):
            mx = jnp.maximum(mx, s[:, g * LANES:(g + 1) * LANES])
        return mx

    def probs(s, mb):
        ps = [jnp.exp2(s[:, g * LANES:(g + 1) * LANES] - mb) for g in range(groups)]
        return ps, functools.reduce(lambda a, b: a + b, ps)

    def body_a(trip, c):
        for t in range(tiles):
            kt_idx = trip * tiles + t
            for hh in range(2):
                s = _dot(q_ref[:, heads[hh]], kt_ref[kt_idx, heads[hh], :])
                s_scr[hh, kt_idx] = s
                m_scr[hh] = lane_max(m_scr[hh], s)
        return c

    lax.fori_loop(0, qi, body_a, 0)
    for t in range(tiles):
        kt_idx = qi * tiles + t
        rows = ATT_Q - t * ATT_K
        visible = (lax.broadcasted_iota(jnp.int32, (rows, ATT_K), 1)
                   <= lax.broadcasted_iota(jnp.int32, (rows, ATT_K), 0))
        for hh in range(2):
            s = _dot(q_ref[t * ATT_K:, heads[hh]], kt_ref[kt_idx, heads[hh], :])
            s = jnp.where(visible, s, -jnp.inf)
            s_scr[hh, kt_idx, t * ATT_K:, :] = s
            m_scr[hh, t * ATT_K:, :] = lane_max(m_scr[hh, t * ATT_K:, :], s)

    for hh in range(2):
        row_max = jnp.max(m_scr[hh], axis=-1, keepdims=True)
        m_scr[hh] = jnp.broadcast_to(row_max, (ATT_Q, LANES))
    l_scr[...] = jnp.zeros(l_scr.shape, f32)
    acc_scr[...] = jnp.zeros(acc_scr.shape, f32)

    def body_b(trip, c):
        for hh in range(2):
            mb = m_scr[hh]
            lsum, acc = l_scr[hh], acc_scr[hh]
            for t0 in range(0, tiles, PV_TILES):
                kt_idx = trip * tiles + t0
                row0 = pl.multiple_of(kt_idx * ATT_K, PV_TILES * ATT_K)
                vb = v_ref[pl.ds(row0, PV_TILES * ATT_K), :]
                parts = []
                for t in range(PV_TILES):
                    ps, psum = probs(s_scr[hh, kt_idx + t], mb)
                    parts += ps
                    lsum = lsum + psum
                acc = acc + _dot(jnp.concatenate(parts, axis=1).astype(bf16), vb)
            l_scr[hh] = lsum
            acc_scr[hh] = acc
        return c

    lax.fori_loop(0, qi, body_b, 0)
    for t in range(tiles):
        kt_idx = qi * tiles + t
        row0 = pl.multiple_of(kt_idx * ATT_K, ATT_K)
        vb = v_ref[pl.ds(row0, ATT_K), :]
        for hh in range(2):
            ps, psum = probs(s_scr[hh, kt_idx, t * ATT_K:, :], m_scr[hh, t * ATT_K:, :])
            l_scr[hh, t * ATT_K:, :] += psum
            acc_scr[hh, t * ATT_K:, :] += _dot(jnp.concatenate(ps, axis=1).astype(bf16), vb)

    lane = lax.broadcasted_iota(jnp.int32, (ATT_Q, 2 * HEAD_DIM), 1)
    o0 = acc_scr[0] / jnp.sum(l_scr[0], axis=-1, keepdims=True)
    o1 = acc_scr[1] / jnp.sum(l_scr[1], axis=-1, keepdims=True)
    o_ref[...] = jnp.where(lane < HEAD_DIM, o0, o1).astype(bf16)


def _attention(q, kt, v, batch, seq):
    T = q.shape[0]
    nq = seq // ATT_Q
    nk = seq // ATT_K
    pairs = N_HEADS // 2
    return pl.pallas_call(
        _attn_kernel,
        grid=(batch, pairs, nq),
        in_specs=[
            pl.BlockSpec((ATT_Q, 2 * HEAD_PAD), lambda b, p, i: (b * nq + i, p)),
            pl.BlockSpec((nk, 2 * HEAD_PAD, ATT_K), lambda b, p, i: (b, p, 0)),
            pl.BlockSpec((seq, 2 * HEAD_DIM), lambda b, p, i: (b, p)),
        ],
        out_specs=pl.BlockSpec((ATT_Q, 2 * HEAD_DIM), lambda b, p, i: (b * nq + i, p)),
        out_shape=jax.ShapeDtypeStruct((T, D_MODEL), bf16),
        scratch_shapes=[
            pltpu.VMEM((2, nk, ATT_Q, ATT_K), f32),
            pltpu.VMEM((2, ATT_Q, LANES), f32),
            pltpu.VMEM((2, ATT_Q, LANES), f32),
            pltpu.VMEM((2, ATT_Q, 2 * HEAD_DIM), f32),
        ],
        compiler_params=_params(("parallel", "parallel", "parallel")),
        name="attn",
    )(q, kt, v)


def _attn_out_kernel(x_ref, o_ref, sg_ref, wo_ref, mn_ref, wr_ref, x3_ref, lg_ref):
    gated = (o_ref[...].astype(f32) * sg_ref[...].astype(f32)).astype(bf16)
    x3 = x_ref[...] + _dot(gated, wo_ref[...])
    x3_ref[...] = x3
    lg_ref[...] = _router_logits_t(x3, mn_ref[...], wr_ref[...])


def _attn_out(x2d, o, sg, wo, m_norm, w_router):
    T = x2d.shape[0]
    return pl.pallas_call(
        _attn_out_kernel,
        grid=(T // ROW_TILE,),
        in_specs=[
            pl.BlockSpec((ROW_TILE, D_MODEL), lambda i: (i, 0)),
            pl.BlockSpec((ROW_TILE, D_MODEL), lambda i: (i, 0)),
            pl.BlockSpec((ROW_TILE, D_MODEL), lambda i: (i, 0)),
            _const_spec((D_MODEL, D_MODEL)),
            _const_spec((1, D_MODEL)),
            _const_spec((3 * D_MODEL, LANES)),
        ],
        out_specs=[
            pl.BlockSpec((ROW_TILE, D_MODEL), lambda i: (i, 0)),
            pl.BlockSpec((ROUTER_ROWS, ROW_TILE), lambda i: (0, i)),
        ],
        out_shape=[
            jax.ShapeDtypeStruct((T, D_MODEL), f32),
            jax.ShapeDtypeStruct((ROUTER_ROWS, T), f32),
        ],
        compiler_params=_params(("parallel",)),
        name="attn_out",
    )(x2d, o, sg, wo, m_norm, w_router)


def _router_weight(w_group, w_expert):
    pad = jnp.zeros((D_MODEL, LANES - N_EXPERTS - N_GROUPS), f32)
    w = jnp.concatenate([w_expert, w_group, pad], axis=1)
    w_hi = w.astype(bf16)
    w_lo = (w - w_hi.astype(f32)).astype(bf16)
    return jnp.concatenate([w_hi, w_lo, w_hi], axis=0)


def kernel(x, a_norm, a_w_in, a_b_in, a_v_norm, a_w_s, a_b_s, a_w_out, kv_norm, kv_w, kv_b_f,
           k_norm, b_norm, b_w_qg, q_norm, b_w_out, m_norm, m_w_group, m_b_group, m_w_expert,
           m_b_expert, m_w1, m_w3, m_w2):
    batch, seq, _ = x.shape
    T = batch * seq
    x2d = x.reshape(T, D_MODEL)

    x1, lg0 = _sgu(
        x2d, a_norm[0].reshape(1, -1), a_w_in[0].astype(bf16), a_b_in[0].reshape(1, -1),
        a_v_norm[0].reshape(1, -1), a_w_s[0], a_b_s[0].T, a_w_out[0].astype(bf16),
        m_norm[0].reshape(1, -1), _router_weight(m_w_group[0], m_w_expert[0]))
    x2 = _moe(0, x1, lg0, m_norm[0].reshape(1, -1), m_b_group[0], m_b_expert[0],
              m_w1, m_w3, m_w2)

    wk = kv_w[:, :D_MODEL]
    wv = kv_w[:, D_MODEL:2 * D_MODEL]
    wf = jnp.pad(kv_w[:, 2 * D_MODEL:], ((0, 0), (0, LANES - N_HEADS)))
    bfv = jnp.pad(kv_b_f, (0, LANES - N_HEADS)).reshape(1, LANES)
    wq = b_w_qg[0][:, :D_MODEL]
    wg = b_w_qg[0][:, D_MODEL:]
    gq = jnp.tile(q_norm[0], 2).reshape(1, HEAD_PAD)
    gk = k_norm.reshape(HEAD_DIM, 1)
    q, kt, v, sg = _proj(
        x2, seq, kv_norm.reshape(1, -1), b_norm[0].reshape(1, -1),
        wk.T.astype(bf16), wv.astype(bf16), wf.astype(bf16), bfv,
        wq.astype(bf16), wg.astype(bf16), gq, gk)
    o = _attention(q, kt, v, batch, seq)
    x3, lg1 = _attn_out(x2, o, sg, b_w_out[0].astype(bf16), m_norm[1].reshape(1, -1),
                        _router_weight(m_w_group[1], m_w_expert[1]))
    x4 = _moe(1, x3, lg1, m_norm[1].reshape(1, -1), m_b_group[1], m_b_expert[1],
              m_w1, m_w3, m_w2)
    return x4.reshape(batch, seq, D_MODEL)
```

```python
import functools
import math

import jax
import jax.numpy as jnp
import numpy as np
from jax import lax
from jax.experimental import pallas as pl
from jax.experimental.pallas import tpu as pltpu

D_MODEL = 1024
EPS = 1e-6
SGU_BLOCK = 128
SGU_CHUNK = 64
CHUNK_SHIFT = SGU_CHUNK.bit_length() - 1
SGU_WIDTH = 2 * D_MODEL
SGU_GROUPS = 8
SGU_GROUP_DIM = SGU_WIDTH // SGU_GROUPS
N_HEADS = 16
HEAD_DIM = D_MODEL // N_HEADS
N_GROUPS = 4
EXPERTS_PER_GROUP = 8
N_EXPERTS = N_GROUPS * EXPERTS_PER_GROUP
GROUP_SHIFT = EXPERTS_PER_GROUP.bit_length() - 1
TOP_K = 2
D_EXPERT = D_MODEL // 2

LANES = 128
ROW_SUBLANES = D_MODEL // 2 // LANES
ISSUE_UNROLL = 8
VMEM_LIMIT_BYTES = 56 * 1024 * 1024

ROW_TILE = 256
SGU_TILE = 512
ROUTE_TILE = 512
EXPERT_ROWS = 512
MOVE_TILE = 512
MOVE_CHUNK = 64
ATT_Q = 1024
ATT_K = ROW_TILE
PV_TILES = 2
HEAD_PAD = 2 * HEAD_DIM
ROUTER_ROWS = 40
AUX_ONE0 = HEAD_DIM
AUX_CUM0 = HEAD_DIM + 3

LOG2E = math.log2(math.e)

bf16 = jnp.bfloat16
f32 = jnp.float32
u32 = jnp.uint32
HI_HALF = np.uint32(0xFFFF0000)


def _dot(a, b, precision=None):
    return jnp.dot(a, b, preferred_element_type=f32, precision=precision)


def _dot_nt(a, b, precision=None):
    return lax.dot_general(a, b, (((1,), (1,)), ((), ())), preferred_element_type=f32,
                           precision=precision)


def _rms(x, g):
    ms = jnp.mean(x * x, axis=-1, keepdims=True)
    return x * lax.rsqrt(ms + EPS) * g


def _gelu_tanh(z):
    c = math.sqrt(2.0 / math.pi)
    return z * (0.5 * (1.0 + jnp.tanh(c * (z + 0.044715 * (z * z * z)))))


def _split3(c):
    hi = c.astype(bf16).astype(f32)
    r = c - hi
    mid = r.astype(bf16).astype(f32)
    lo = r - mid
    return hi, mid, lo


def _pack_rows(v):
    half = D_MODEL // 2
    lo = lax.bitcast_convert_type(v[:, :half].astype(bf16).astype(f32), u32) >> 16
    hi = lax.bitcast_convert_type(v[:, half:].astype(bf16).astype(f32), u32) & HI_HALF
    return (hi | lo).reshape(v.shape[0], ROW_SUBLANES, LANES)


def _unpack_rows(w):
    w = w.reshape(w.shape[0], D_MODEL // 2)
    lo = lax.bitcast_convert_type(w << 16, f32)
    hi = lax.bitcast_convert_type(w & HI_HALF, f32)
    return jnp.concatenate([lo, hi], axis=1)


def _const_spec(shape, single=False):
    nd = len(shape)
    mode = pl.Buffered(1) if single else None
    return pl.BlockSpec(shape, lambda *_: (0,) * nd, pipeline_mode=mode)


def _params(sem):
    return pltpu.CompilerParams(dimension_semantics=sem, vmem_limit_bytes=VMEM_LIMIT_BYTES)


def _router_logits_t(x_new, mn, wr3):
    hm = _rms(x_new, mn)
    h_hi = hm.astype(bf16)
    h_lo = (hm - h_hi.astype(f32)).astype(bf16)
    both = _dot(jnp.concatenate([h_hi, h_lo], axis=1), wr3)
    lg = both[:, :LANES] + both[:, LANES:]
    return lg.T[:ROUTER_ROWS, :]


def _sgu_kernel(x_ref, an_ref, win_ref, bin_ref, vn_ref, ws_ref, bst_ref, wout_ref, mn_ref,
                wr_ref, x1_ref, lg_ref, u_scr, v_scr, gated_scr):
    x = x_ref[...]
    h = _rms(x, an_ref[...]).astype(bf16)
    u_scr[...] = _gelu_tanh(_dot(h, win_ref[:, :SGU_WIDTH]) + bin_ref[:, :SGU_WIDTH])
    v = _gelu_tanh(_dot(h, win_ref[:, SGU_WIDTH:]) + bin_ref[:, SGU_WIDTH:])
    v_scr[...] = _rms(v, vn_ref[...]).astype(bf16)

    t_chunk = lax.broadcasted_iota(jnp.int32, (SGU_BLOCK, SGU_BLOCK), 0) >> CHUNK_SHIFT
    s_chunk = lax.broadcasted_iota(jnp.int32, (SGU_BLOCK, SGU_BLOCK), 1) >> CHUNK_SHIFT
    causal = t_chunk >= s_chunk
    for g in range(SGU_GROUPS):
        wsm = jnp.where(causal, ws_ref[g], 0.0).astype(bf16)
        cols = slice(g * SGU_GROUP_DIM, (g + 1) * SGU_GROUP_DIM)
        for sb in range(SGU_TILE // SGU_BLOCK):
            rows = slice(sb * SGU_BLOCK, (sb + 1) * SGU_BLOCK)
            mixed = _dot(wsm, v_scr[rows, cols]) + bst_ref[:, g:g + 1]
            gated_scr[rows, cols] = (u_scr[rows, cols] * mixed).astype(bf16)

    x1 = x + _dot(gated_scr[...], wout_ref[...])
    x1_ref[...] = x1
    lg_ref[...] = _router_logits_t(x1, mn_ref[...], wr_ref[...])


def _sgu(x2d, a_norm, w_in, b_in, v_norm, w_s, b_s_t, w_out, m_norm, w_router):
    T = x2d.shape[0]
    return pl.pallas_call(
        _sgu_kernel,
        grid=(T // SGU_TILE,),
        in_specs=[
            pl.BlockSpec((SGU_TILE, D_MODEL), lambda i: (i, 0)),
            _const_spec((1, D_MODEL)),
            _const_spec((D_MODEL, 2 * SGU_WIDTH), single=True),
            _const_spec((1, 2 * SGU_WIDTH)),
            _const_spec((1, SGU_WIDTH)),
            _const_spec((SGU_GROUPS, SGU_BLOCK, SGU_BLOCK)),
            _const_spec((SGU_BLOCK, SGU_GROUPS)),
            _const_spec((SGU_WIDTH, D_MODEL), single=True),
            _const_spec((1, D_MODEL)),
            _const_spec((2 * D_MODEL, 2 * LANES)),
        ],
        out_specs=[
            pl.BlockSpec((SGU_TILE, D_MODEL), lambda i: (i, 0)),
            pl.BlockSpec((ROUTER_ROWS, SGU_TILE), lambda i: (0, i)),
        ],
        out_shape=[
            jax.ShapeDtypeStruct((T, D_MODEL), f32),
            jax.ShapeDtypeStruct((ROUTER_ROWS, T), f32),
        ],
        scratch_shapes=[
            pltpu.VMEM((SGU_TILE, SGU_WIDTH), f32),
            pltpu.VMEM((SGU_TILE, SGU_WIDTH), bf16),
            pltpu.VMEM((SGU_TILE, SGU_WIDTH), bf16),
        ],
        compiler_params=_params(("parallel",)),
        name="sgu",
    )(x2d, a_norm, w_in, b_in, v_norm, w_s, b_s_t, w_out, m_norm, w_router)


def _route_kernel(lg_ref, be_ref, bg_ref, oi_ref, ow_ref, cnt_ref, carry_scr):
    step = pl.program_id(0)

    @pl.when(step == 0)
    def _():
        carry_scr[...] = jnp.zeros_like(carry_scr)

    lg = lg_ref[...]
    e_l = lg[0:N_EXPERTS, :] + be_ref[...]
    g_l = lg[N_EXPERTS:N_EXPERTS + 8, :] + bg_ref[...]
    g_row = lax.broadcasted_iota(jnp.int32, g_l.shape, 0).astype(f32)
    g_l = jnp.where(g_row < N_GROUPS, g_l, -jnp.inf)
    g_max = jnp.max(g_l, axis=0, keepdims=True)
    g_sel = jnp.min(jnp.where(g_l == g_max, g_row, 8.0), axis=0, keepdims=True)
    g_den = jnp.sum(jnp.exp(g_l - g_max), axis=0, keepdims=True)
    g_w = 1.0 / g_den

    e_row_i = lax.broadcasted_iota(jnp.int32, e_l.shape, 0)
    e_row = e_row_i.astype(f32)
    e_grp = (e_row_i >> GROUP_SHIFT).astype(f32)
    e_in = jnp.where(e_grp == g_sel, e_l, -jnp.inf)
    m1 = jnp.max(e_in, axis=0, keepdims=True)
    i1 = jnp.min(jnp.where(e_in == m1, e_row, float(N_EXPERTS)), axis=0, keepdims=True)
    e_in2 = jnp.where(e_row == i1, -jnp.inf, e_in)
    m2 = jnp.max(e_in2, axis=0, keepdims=True)
    i2 = jnp.min(jnp.where(e_in2 == m2, e_row, float(N_EXPERTS)), axis=0, keepdims=True)
    t = jnp.exp(m2 - m1)
    p1 = 1.0 / (1.0 + t)
    p2 = t / (1.0 + t)

    hit1 = e_row == i1
    hit2 = e_row == i2
    member = jnp.logical_or(hit1, hit2)
    tt = lg.shape[1]
    before = (lax.broadcasted_iota(jnp.int32, (tt, tt), 0)
              < lax.broadcasted_iota(jnp.int32, (tt, tt), 1))
    prefix = _dot(member.astype(bf16), before.astype(bf16))
    rank_full = prefix + carry_scr[...]
    rank1 = jnp.sum(jnp.where(hit1, rank_full, 0.0), axis=0, keepdims=True)
    rank2 = jnp.sum(jnp.where(hit2, rank_full, 0.0), axis=0, keepdims=True)
    carry_new = carry_scr[...] + jnp.sum(member.astype(f32), axis=1, keepdims=True)
    carry_scr[...] = carry_new

    oi_ref[...] = jnp.zeros(oi_ref.shape, jnp.int32)
    oi_ref[0:1, :] = i1.astype(jnp.int32)
    oi_ref[1:2, :] = i2.astype(jnp.int32)
    oi_ref[2:3, :] = rank1.astype(jnp.int32)
    oi_ref[3:4, :] = rank2.astype(jnp.int32)
    ow_ref[...] = jnp.zeros(ow_ref.shape, f32)
    ow_ref[0:1, :] = g_w * p1
    ow_ref[1:2, :] = g_w * p2
    cnt_ref[...] = jnp.broadcast_to(carry_new, cnt_ref.shape)


def _route(logits_t, b_expert, b_group):
    T = logits_t.shape[1]
    return pl.pallas_call(
        _route_kernel,
        grid=(T // ROUTE_TILE,),
        in_specs=[
            pl.BlockSpec((ROUTER_ROWS, ROUTE_TILE), lambda i: (0, i)),
            _const_spec((N_EXPERTS, 1)),
            _const_spec((8, 1)),
        ],
        out_specs=[
            pl.BlockSpec((8, ROUTE_TILE), lambda i: (0, i)),
            pl.BlockSpec((8, ROUTE_TILE), lambda i: (0, i)),
            _const_spec((N_EXPERTS, LANES)),
        ],
        out_shape=[
            jax.ShapeDtypeStruct((8, T), jnp.int32),
            jax.ShapeDtypeStruct((8, T), f32),
            jax.ShapeDtypeStruct((N_EXPERTS, LANES), f32),
        ],
        scratch_shapes=[pltpu.VMEM((N_EXPERTS, 1), f32)],
        compiler_params=_params(("arbitrary",)),
        name="route",
    )(logits_t, b_expert, b_group)


def _row_copy(src, s, dst, d, sem):
    return pltpu.make_async_copy(src.at[s], dst.at[d], sem)


def _wait_rows(buf, sem):
    pltpu.make_async_copy(buf, buf, sem).wait()


def _dispatch_kernel(n_blocks, pend_ref, padded_ref, nu_ref, dest_ref, x_ref, mn_ref, xs_hbm,
                     zero_scr, x3_scr, zsem, sem):
    @pl.when(pl.program_id(0) == 0)
    def _():
        zero_scr[...] = jnp.zeros_like(zero_scr)

        def fill(row0):
            return pltpu.make_async_copy(zero_scr, xs_hbm.at[pl.ds(row0, EXPERT_ROWS)], zsem)

        for e in range(N_EXPERTS):
            @pl.when(padded_ref[e] > 0)
            def _():
                fill(pend_ref[e] - EXPERT_ROWS).start()

            @pl.when(nu_ref[0] + e < n_blocks)
            def _():
                fill((nu_ref[0] + e) * EXPERT_ROWS).start()

        for e in range(N_EXPERTS):
            @pl.when(padded_ref[e] > 0)
            def _():
                fill(0).wait()

            @pl.when(nu_ref[0] + e < n_blocks)
            def _():
                fill(0).wait()

    i = pl.program_id(0)
    n_tiles = pl.num_programs(0) - 1

    def staged(rows):
        return _pack_rows(_rms(x_ref[rows, :], mn_ref[...]))

    @pl.when(i == 0)
    def _():
        x3_scr[0] = staged(slice(None))

    @pl.when(i > 0)
    def _():
        cur = (i - 1) % 3
        nxt = i % 3

        def chunk(c, carry):
            r0 = pl.multiple_of(c * MOVE_CHUNK, MOVE_CHUNK)
            for j in range(MOVE_CHUNK):
                for k in range(TOP_K):
                    _row_copy(x3_scr.at[cur], r0 + j, xs_hbm, dest_ref[k, r0 + j],
                              sem.at[cur]).start(priority=k)
            x3_scr[nxt, pl.ds(r0, MOVE_CHUNK)] = staged(pl.ds(r0, MOVE_CHUNK))
            return carry

        lax.fori_loop(0, MOVE_TILE // MOVE_CHUNK, chunk, 0)

        @pl.when(i > 1)
        def _():
            for k in range(TOP_K):
                _wait_rows(x3_scr.at[(i - 2) % 3], sem.at[(i - 2) % 3])

        @pl.when(i == n_tiles)
        def _():
            for k in range(TOP_K):
                _wait_rows(x3_scr.at[cur], sem.at[cur])


def _dispatch(pend, padded, n_used, dest_tiles, x2d, m_norm, n_rows):
    T = x2d.shape[0]
    n_blocks = n_rows // EXPERT_ROWS
    n_tiles = T // MOVE_TILE
    grid_spec = pltpu.PrefetchScalarGridSpec(
        num_scalar_prefetch=3,
        grid=(n_tiles + 1,),
        in_specs=[
            pl.BlockSpec((None, TOP_K, MOVE_TILE), lambda i, *_: (jnp.maximum(i - 1, 0), 0, 0),
                         memory_space=pltpu.SMEM),
            pl.BlockSpec((MOVE_TILE, D_MODEL), lambda i, *_: (jnp.minimum(i, n_tiles - 1), 0)),
            pl.BlockSpec((1, D_MODEL), lambda i, *_: (0, 0)),
        ],
        out_specs=pl.BlockSpec(memory_space=pl.ANY),
        scratch_shapes=[
            pltpu.VMEM((EXPERT_ROWS, ROW_SUBLANES, LANES), u32),
            pltpu.VMEM((3, MOVE_TILE, ROW_SUBLANES, LANES), u32),
            pltpu.SemaphoreType.DMA(()),
            pltpu.SemaphoreType.DMA((3,)),
        ],
    )
    return pl.pallas_call(
        functools.partial(_dispatch_kernel, n_blocks),
        grid_spec=grid_spec,
        out_shape=jax.ShapeDtypeStruct((n_rows, ROW_SUBLANES, LANES), u32),
        compiler_params=_params(("arbitrary",)),
        name="dispatch",
    )(pend, padded, n_used, dest_tiles, x2d, m_norm)


def _expert_kernel(n_blocks, ps_ref, nb_ref, nu_ref, xs_hbm, w1_ref, w3_ref, w2_ref,
                   ys_hbm, xbuf, ybuf, h_scr, w1_scr, w3_scr, w2_scr, xsem, ysem):
    e = pl.program_id(0)
    nblk = nb_ref[e]
    g0 = ps_ref[e]
    n_used = nu_ref[0]

    def rows(g):
        return pl.ds(pl.multiple_of(g * EXPERT_ROWS, EXPERT_ROWS), EXPERT_ROWS)

    def x_copy(g):
        return pltpu.make_async_copy(xs_hbm.at[rows(g)], xbuf.at[g % 2], xsem.at[g % 2])

    def y_copy(g):
        return pltpu.make_async_copy(ybuf.at[g % 2], ys_hbm.at[rows(g)], ysem.at[g % 2])

    def normed(slot):
        return _unpack_rows(xbuf[slot]).astype(bf16)

    @pl.when(e == 0)
    def _():
        x_copy(0).start(priority=1)
        x_copy(1).start(priority=1)
        x_copy(0).wait()
        h_scr[...] = normed(0)

    @pl.when(nblk > 0)
    def _():
        w1_scr[...] = w1_ref[...].astype(bf16)
        w3_scr[...] = w3_ref[...].astype(bf16)
        w2_scr[...] = w2_ref[...].astype(bf16)

        def body(g, h):
            @pl.when(g + 2 < n_used)
            def _():
                x_copy(g + 2).start(priority=1)

            @pl.when(g + 1 < n_used)
            def _():
                x_copy(g + 1).wait()

            @pl.when(g >= 2)
            def _():
                y_copy(g - 2).wait()

            h_next = normed((g + 1) % 2)
            a = _dot(h, w1_scr[...])
            c = _dot(h, w3_scr[...])
            hid = (a * jax.nn.sigmoid(a)) * c
            ybuf[g % 2] = _pack_rows(_dot(hid.astype(bf16), w2_scr[...]))
            y_copy(g).start(priority=1)
            return h_next

        h_scr[...] = lax.fori_loop(g0, g0 + nblk, body, h_scr[...])

    @pl.when(e == pl.num_programs(0) - 1)
    def _():
        y_copy(n_used - 2).wait()
        y_copy(n_used - 1).wait()
        ybuf[0] = jnp.zeros(ybuf.shape[1:], u32)

        def fill(b, c):
            r = pl.ds(pl.multiple_of(b * EXPERT_ROWS, EXPERT_ROWS), EXPERT_ROWS)
            cp = pltpu.make_async_copy(ybuf.at[0], ys_hbm.at[r], ysem.at[0])
            cp.start()
            cp.wait()
            return c

        lax.fori_loop(nu_ref[0], n_blocks, fill, 0)


def _experts(layer, pstart, seg_blocks, n_used, xs, w1, w3, w2):
    n_rows = xs.shape[0]
    n_blocks = n_rows // EXPERT_ROWS

    def w_map(e, *_):
        return (layer, e, 0, 0)

    block_buf = pltpu.VMEM((2, EXPERT_ROWS, ROW_SUBLANES, LANES), u32)
    grid_spec = pltpu.PrefetchScalarGridSpec(
        num_scalar_prefetch=3,
        grid=(N_EXPERTS,),
        in_specs=[
            pl.BlockSpec(memory_space=pl.ANY),
            pl.BlockSpec((None, None, D_MODEL, D_EXPERT), w_map),
            pl.BlockSpec((None, None, D_MODEL, D_EXPERT), w_map),
            pl.BlockSpec((None, None, D_EXPERT, D_MODEL), w_map),
        ],
        out_specs=pl.BlockSpec(memory_space=pl.ANY),
        scratch_shapes=[
            block_buf,
            block_buf,
            pltpu.VMEM((EXPERT_ROWS, D_MODEL), bf16),
            pltpu.VMEM((D_MODEL, D_EXPERT), bf16),
            pltpu.VMEM((D_MODEL, D_EXPERT), bf16),
            pltpu.VMEM((D_EXPERT, D_MODEL), bf16),
            pltpu.SemaphoreType.DMA((2,)),
            pltpu.SemaphoreType.DMA((2,)),
        ],
    )
    return pl.pallas_call(
        functools.partial(_expert_kernel, n_blocks),
        grid_spec=grid_spec,
        out_shape=jax.ShapeDtypeStruct((n_rows, ROW_SUBLANES, LANES), u32),
        compiler_params=_params(("arbitrary",)),
        name="experts",
    )(pstart, seg_blocks, n_used, xs, w1, w3, w2)


def _combine_kernel(dest_ref, next_ref, x_ref, w_ref, ys_hbm, out_ref, buf, sem):
    i = pl.program_id(0)
    slot = i % 2

    def gather(idx_ref, s):
        def start(j, c):
            for k in range(TOP_K):
                _row_copy(ys_hbm, idx_ref[k, j], buf.at[s, k], j, sem.at[s]).start(priority=k)
            return c

        lax.fori_loop(0, MOVE_TILE, start, 0, unroll=ISSUE_UNROLL)

    @pl.when(i == 0)
    def _():
        gather(dest_ref, slot)

    @pl.when(i + 1 < pl.num_programs(0))
    def _():
        gather(next_ref, 1 - slot)

    for k in range(TOP_K):
        _wait_rows(buf.at[slot, k], sem.at[slot])
    w = w_ref[...]
    y0 = _unpack_rows(buf[slot, 0])
    y1 = _unpack_rows(buf[slot, 1])
    out_ref[...] = x_ref[...] + (w[:, 0:1] * y0 + w[:, 1:2] * y1)


def _combine(dest_tiles, x2d, w_cols, ys):
    T = x2d.shape[0]
    n_tiles = T // MOVE_TILE
    return pl.pallas_call(
        _combine_kernel,
        grid=(n_tiles,),
        in_specs=[
            pl.BlockSpec((None, TOP_K, MOVE_TILE), lambda i: (i, 0, 0),
                         memory_space=pltpu.SMEM),
            pl.BlockSpec((None, TOP_K, MOVE_TILE),
                         lambda i: (jnp.minimum(i + 1, n_tiles - 1), 0, 0),
                         memory_space=pltpu.SMEM),
            pl.BlockSpec((MOVE_TILE, D_MODEL), lambda i: (i, 0)),
            pl.BlockSpec((MOVE_TILE, 8), lambda i: (i, 0)),
            pl.BlockSpec(memory_space=pl.ANY),
        ],
        out_specs=pl.BlockSpec((MOVE_TILE, D_MODEL), lambda i: (i, 0)),
        out_shape=jax.ShapeDtypeStruct((T, D_MODEL), f32),
        scratch_shapes=[
            pltpu.VMEM((2, TOP_K, MOVE_TILE, ROW_SUBLANES, LANES), u32),
            pltpu.SemaphoreType.DMA((2,)),
        ],
        compiler_params=_params(("arbitrary",)),
        name="combine",
    )(dest_tiles, dest_tiles, x2d, w_cols, ys)


def _moe(layer, x2d, logits_t, m_norm, b_group, b_expert, w1, w3, w2):
    T = x2d.shape[0]
    n_blocks = (T * TOP_K) // EXPERT_ROWS + N_EXPERTS
    n_rows = n_blocks * EXPERT_ROWS

    be = b_expert.reshape(N_EXPERTS, 1)
    bg = jnp.concatenate([b_group, jnp.zeros((8 - N_GROUPS,), f32)]).reshape(8, 1)
    oi, ow, cnt = _route(logits_t, be, bg)

    counts = cnt[:, 0].astype(jnp.int32)
    padded = (counts + EXPERT_ROWS - 1) // EXPERT_ROWS * EXPERT_ROWS
    pend = jnp.cumsum(padded)
    pstart = pend - padded
    n_used = (pend[-1] // EXPERT_ROWS).astype(jnp.int32).reshape(1)
    e_ids = jnp.arange(N_EXPERTS, dtype=jnp.int32)[:, None, None]
    seg_start = jnp.sum(jnp.where(oi[None, 0:2] == e_ids, pstart[:, None, None], 0), axis=0)
    dest = seg_start + oi[2:4]
    dest_tiles = dest.reshape(TOP_K, T // MOVE_TILE, MOVE_TILE).transpose(1, 0, 2)
    w_cols = ow.T

    xs = _dispatch(pend.astype(jnp.int32), padded, n_used, dest_tiles, x2d, m_norm, n_rows)
    ys = _experts(layer, (pstart // EXPERT_ROWS).astype(jnp.int32), padded // EXPERT_ROWS,
                  n_used, xs, w1, w3, w2)
    return _combine(dest_tiles, x2d, w_cols, ys)


def _log_sigmoid(x):
    return jnp.minimum(x, 0.0) - jnp.log1p(jnp.exp(-jnp.abs(x)))


def _proj_kernel(tiles_per_seq, x_ref, kvn_ref, bn_ref, wkt_ref, wv_ref, wf_ref, bf_ref,
                 wq_ref, wg_ref, gq_ref, gk_ref, q_ref, kt_ref, v_ref, sg_ref, carry_scr):
    i = pl.program_id(0)
    x = x_ref[...]
    hkv = _rms(x, kvn_ref[...]).astype(bf16)
    hq = _rms(x, bn_ref[...]).astype(bf16)

    v_ref[...] = _dot(hkv, wv_ref[...]).astype(bf16)
    sg_ref[...] = jax.nn.sigmoid(_dot(hq, wg_ref[...])).astype(bf16)

    logf = _log_sigmoid(_dot(hkv, wf_ref[...]) + bf_ref[...])
    tm = x.shape[0]
    incl = (lax.broadcasted_iota(jnp.int32, (tm, tm), 0)
            >= lax.broadcasted_iota(jnp.int32, (tm, tm), 1)).astype(f32)
    @pl.when(i % tiles_per_seq == 0)
    def _():
        carry_scr[...] = jnp.zeros_like(carry_scr)

    cum = _dot(incl, logf, precision=lax.Precision.HIGHEST) + carry_scr[...]
    carry_scr[...] = cum[tm - 1:tm, :]
    cum = cum * LOG2E
    cum_t = cum.T

    lane = lax.broadcasted_iota(jnp.int32, (tm, HEAD_PAD), 1)
    sub = lax.broadcasted_iota(jnp.int32, (HEAD_DIM, tm), 0)
    q_raw = _dot(hq, wq_ref[...])
    k_raw_t = _dot_nt(wkt_ref[...], hkv)
    scale = HEAD_DIM ** -0.5 * LOG2E

    for h in range(N_HEADS):
        sl = slice(h * HEAD_PAD, (h + 1) * HEAD_PAD)
        qb = q_raw[:, sl]
        q_ms = jnp.sum(qb * qb, axis=-1, keepdims=True) * (1.0 / HEAD_DIM)
        qn = qb * lax.rsqrt(q_ms + EPS) * gq_ref[...] * scale
        c_hi, c_mid, c_lo = _split3(cum[:, h:h + 1])
        qa = jnp.where(lane < HEAD_DIM, qn,
             jnp.where(lane < AUX_CUM0, 1.0,
             jnp.where(lane == AUX_CUM0, c_hi,
             jnp.where(lane == AUX_CUM0 + 1, c_mid,
             jnp.where(lane == AUX_CUM0 + 2, c_lo, 0.0)))))
        q_ref[:, sl] = qa.astype(bf16)

        kb = k_raw_t[h * HEAD_DIM:(h + 1) * HEAD_DIM, :]
        k_ms = jnp.sum(kb * kb, axis=0, keepdims=True) * (1.0 / HEAD_DIM)
        kn = kb * lax.rsqrt(k_ms + EPS) * gk_ref[...]
        t_hi, t_mid, t_lo = _split3(cum_t[h:h + 1, :])
        aux = jnp.where(sub == 0, -t_hi,
              jnp.where(sub == 1, -t_mid,
              jnp.where(sub == 2, -t_lo,
              jnp.where(sub < 6, 1.0, 0.0))))
        sl = slice(h * HEAD_PAD, (h + 1) * HEAD_PAD)
        kt_ref[0, sl, :] = jnp.concatenate([kn, aux], axis=0).astype(bf16)


def _proj(x2d, seq, kv_norm, b_norm, wkt, wv, wf, bfv, wq, wg, gq, gk):
    T = x2d.shape[0]
    n_tiles = T // ROW_TILE
    qw = N_HEADS * HEAD_PAD
    return pl.pallas_call(
        functools.partial(_proj_kernel, seq // ROW_TILE),
        grid=(n_tiles,),
        in_specs=[
            pl.BlockSpec((ROW_TILE, D_MODEL), lambda i: (i, 0)),
            _const_spec((1, D_MODEL)),
            _const_spec((1, D_MODEL)),
            _const_spec((D_MODEL, D_MODEL)),
            _const_spec((D_MODEL, D_MODEL)),
            _const_spec((D_MODEL, LANES)),
            _const_spec((1, LANES)),
            _const_spec((D_MODEL, qw)),
            _const_spec((D_MODEL, D_MODEL)),
            _const_spec((1, HEAD_PAD)),
            _const_spec((HEAD_DIM, 1)),
        ],
        out_specs=[
            pl.BlockSpec((ROW_TILE, qw), lambda i: (i, 0)),
            pl.BlockSpec((1, qw, ROW_TILE), lambda i: (i, 0, 0)),
            pl.BlockSpec((ROW_TILE, D_MODEL), lambda i: (i, 0)),
            pl.BlockSpec((ROW_TILE, D_MODEL), lambda i: (i, 0)),
        ],
        out_shape=[
            jax.ShapeDtypeStruct((T, qw), bf16),
            jax.ShapeDtypeStruct((n_tiles, qw, ROW_TILE), bf16),
            jax.ShapeDtypeStruct((T, D_MODEL), bf16),
            jax.ShapeDtypeStruct((T, D_MODEL), bf16),
        ],
        scratch_shapes=[pltpu.VMEM((1, LANES), f32)],
        compiler_params=_params(("arbitrary",)),
        name="proj",
    )(x2d, kv_norm, b_norm, wkt, wv, wf, bfv, wq, wg, gq, gk)


def _attn_kernel(q_ref, kt_ref, v_ref, o_ref, s_scr, m_scr, l_scr, acc_scr):
    qi = pl.program_id(2)
    tiles = ATT_Q // ATT_K
    groups = ATT_K // LANES
    heads = [slice(hh * HEAD_PAD, (hh + 1) * HEAD_PAD) for hh in range(2)]
    m_scr[...] = jnp.full(m_scr.shape, -jnp.inf, f32)

    def lane_max(mx, s):
        for g in range(groups):
            mx = jnp.maximum(mx, s[:, g * LANES:(g + 1) * LANES])
        return mx

    def probs(s, mb):
        ps = [jnp.exp2(s[:, g * LANES:(g + 1) * LANES] - mb) for g in range(groups)]
        return ps, functools.reduce(lambda a, b: a + b, ps)

    def body_a(trip, c):
        for t in range(tiles):
            kt_idx = trip * tiles + t
            for hh in range(2):
                s = _dot(q_ref[:, heads[hh]], kt_ref[kt_idx, heads[hh], :])
                s_scr[hh, kt_idx] = s
                m_scr[hh] = lane_max(m_scr[hh], s)
        return c

    lax.fori_loop(0, qi, body_a, 0)
    for t in range(tiles):
        kt_idx = qi * tiles + t
        rows = ATT_Q - t * ATT_K
        visible = (lax.broadcasted_iota(jnp.int32, (rows, ATT_K), 1)
                   <= lax.broadcasted_iota(jnp.int32, (rows, ATT_K), 0))
        for hh in range(2):
            s = _dot(q_ref[t * ATT_K:, heads[hh]], kt_ref[kt_idx, heads[hh], :])
            s = jnp.where(visible, s, -jnp.inf)
            s_scr[hh, kt_idx, t * ATT_K:, :] = s
            m_scr[hh, t * ATT_K:, :] = lane_max(m_scr[hh, t * ATT_K:, :], s)

    for hh in range(2):
        row_max = jnp.max(m_scr[hh], axis=-1, keepdims=True)
        m_scr[hh] = jnp.broadcast_to(row_max, (ATT_Q, LANES))
    l_scr[...] = jnp.zeros(l_scr.shape, f32)
    acc_scr[...] = jnp.zeros(acc_scr.shape, f32)

    def body_b(trip, c):
        for hh in range(2):
            mb = m_scr[hh]
            lsum, acc = l_scr[hh], acc_scr[hh]
            for t0 in range(0, tiles, PV_TILES):
                kt_idx = trip * tiles + t0
                row0 = pl.multiple_of(kt_idx * ATT_K, PV_TILES * ATT_K)
                vb = v_ref[pl.ds(row0, PV_TILES * ATT_K), :]
                parts = []
                for t in range(PV_TILES):
                    ps, psum = probs(s_scr[hh, kt_idx + t], mb)
                    parts += ps
                    lsum = lsum + psum
                acc = acc + _dot(jnp.concatenate(parts, axis=1).astype(bf16), vb)
            l_scr[hh] = lsum
            acc_scr[hh] = acc
        return c

    lax.fori_loop(0, qi, body_b, 0)
    for t in range(tiles):
        kt_idx = qi * tiles + t
        row0 = pl.multiple_of(kt_idx * ATT_K, ATT_K)
        vb = v_ref[pl.ds(row0, ATT_K), :]
        for hh in range(2):
            ps, psum = probs(s_scr[hh, kt_idx, t * ATT_K:, :], m_scr[hh, t * ATT_K:, :])
            l_scr[hh, t * ATT_K:, :] += psum
            acc_scr[hh, t * ATT_K:, :] += _dot(jnp.concatenate(ps, axis=1).astype(bf16), vb)

    lane = lax.broadcasted_iota(jnp.int32, (ATT_Q, 2 * HEAD_DIM), 1)
    o0 = acc_scr[0] / jnp.sum(l_scr[0], axis=-1, keepdims=True)
    o1 = acc_scr[1] / jnp.sum(l_scr[1], axis=-1, keepdims=True)
    o_ref[...] = jnp.where(lane < HEAD_DIM, o0, o1).astype(bf16)


def _attention(q, kt, v, batch, seq):
    T = q.shape[0]
    nq = seq // ATT_Q
    nk = seq // ATT_K
    pairs = N_HEADS // 2
    return pl.pallas_call(
        _attn_kernel,
        grid=(batch, pairs, nq),
        in_specs=[
            pl.BlockSpec((ATT_Q, 2 * HEAD_PAD), lambda b, p, i: (b * nq + i, p)),
            pl.BlockSpec((nk, 2 * HEAD_PAD, ATT_K), lambda b, p, i: (b, p, 0)),
            pl.BlockSpec((seq, 2 * HEAD_DIM), lambda b, p, i: (b, p)),
        ],
        out_specs=pl.BlockSpec((ATT_Q, 2 * HEAD_DIM), lambda b, p, i: (b * nq + i, p)),
        out_shape=jax.ShapeDtypeStruct((T, D_MODEL), bf16),
        scratch_shapes=[
            pltpu.VMEM((2, nk, ATT_Q, ATT_K), f32),
            pltpu.VMEM((2, ATT_Q, LANES), f32),
            pltpu.VMEM((2, ATT_Q, LANES), f32),
            pltpu.VMEM((2, ATT_Q, 2 * HEAD_DIM), f32),
        ],
        compiler_params=_params(("parallel", "parallel", "parallel")),
        name="attn",
    )(q, kt, v)


def _attn_out_kernel(x_ref, o_ref, sg_ref, wo_ref, mn_ref, wr_ref, x3_ref, lg_ref):
    gated = (o_ref[...].astype(f32) * sg_ref[...].astype(f32)).astype(bf16)
    x3 = x_ref[...] + _dot(gated, wo_ref[...])
    x3_ref[...] = x3
    lg_ref[...] = _router_logits_t(x3, mn_ref[...], wr_ref[...])


def _attn_out(x2d, o, sg, wo, m_norm, w_router):
    T = x2d.shape[0]
    return pl.pallas_call(
        _attn_out_kernel,
        grid=(T // ROW_TILE,),
        in_specs=[
            pl.BlockSpec((ROW_TILE, D_MODEL), lambda i: (i, 0)),
            pl.BlockSpec((ROW_TILE, D_MODEL), lambda i: (i, 0)),
            pl.BlockSpec((ROW_TILE, D_MODEL), lambda i: (i, 0)),
            _const_spec((D_MODEL, D_MODEL)),
            _const_spec((1, D_MODEL)),
            _const_spec((2 * D_MODEL, 2 * LANES)),
        ],
        out_specs=[
            pl.BlockSpec((ROW_TILE, D_MODEL), lambda i: (i, 0)),
            pl.BlockSpec((ROUTER_ROWS, ROW_TILE), lambda i: (0, i)),
        ],
        out_shape=[
            jax.ShapeDtypeStruct((T, D_MODEL), f32),
            jax.ShapeDtypeStruct((ROUTER_ROWS, T), f32),
        ],
        compiler_params=_params(("parallel",)),
        name="attn_out",
    )(x2d, o, sg, wo, m_norm, w_router)


def _router_weight(w_group, w_expert):
    pad = jnp.zeros((D_MODEL, LANES - N_EXPERTS - N_GROUPS), f32)
    w = jnp.concatenate([w_expert, w_group, pad], axis=1)
    w_hi = w.astype(bf16)
    w_lo = (w - w_hi.astype(f32)).astype(bf16)
    top = jnp.concatenate([w_hi, w_lo], axis=1)
    bottom = jnp.concatenate([w_hi, jnp.zeros_like(w_hi)], axis=1)
    return jnp.concatenate([top, bottom], axis=0)


def kernel(x, a_norm, a_w_in, a_b_in, a_v_norm, a_w_s, a_b_s, a_w_out, kv_norm, kv_w, kv_b_f,
           k_norm, b_norm, b_w_qg, q_norm, b_w_out, m_norm, m_w_group, m_b_group, m_w_expert,
           m_b_expert, m_w1, m_w3, m_w2):
    batch, seq, _ = x.shape
    T = batch * seq
    x2d = x.reshape(T, D_MODEL)

    x1, lg0 = _sgu(
        x2d, a_norm[0].reshape(1, -1), a_w_in[0].astype(bf16), a_b_in[0].reshape(1, -1),
        a_v_norm[0].reshape(1, -1), a_w_s[0], a_b_s[0].T, a_w_out[0].astype(bf16),
        m_norm[0].reshape(1, -1), _router_weight(m_w_group[0], m_w_expert[0]))
    x2 = _moe(0, x1, lg0, m_norm[0].reshape(1, -1), m_b_group[0], m_b_expert[0],
              m_w1, m_w3, m_w2)

    wk = kv_w[:, :D_MODEL]
    wv = kv_w[:, D_MODEL:2 * D_MODEL]
    wf = jnp.pad(kv_w[:, 2 * D_MODEL:], ((0, 0), (0, LANES - N_HEADS)))
    bfv = jnp.pad(kv_b_f, (0, LANES - N_HEADS)).reshape(1, LANES)
    wq = b_w_qg[0][:, :D_MODEL]
    wg = b_w_qg[0][:, D_MODEL:]
    gq = jnp.pad(q_norm[0], (0, HEAD_PAD - HEAD_DIM)).reshape(1, HEAD_PAD)
    gk = k_norm.reshape(HEAD_DIM, 1)
    wq_pad = jnp.pad(wq.reshape(D_MODEL, N_HEADS, HEAD_DIM),
                     ((0, 0), (0, 0), (0, HEAD_PAD - HEAD_DIM))).reshape(D_MODEL, -1)
    q, kt, v, sg = _proj(
        x2, seq, kv_norm.reshape(1, -1), b_norm[0].reshape(1, -1),
        wk.T.astype(bf16), wv.astype(bf16), wf.astype(bf16), bfv,
        wq_pad.astype(bf16), wg.astype(bf16), gq, gk)
    o = _attention(q, kt, v, batch, seq)
    x3, lg1 = _attn_out(x2, o, sg, b_w_out[0].astype(bf16), m_norm[1].reshape(1, -1),
                        _router_weight(m_w_group[1], m_w_expert[1]))
    x4 = _moe(1, x3, lg1, m_norm[1].reshape(1, -1), m_b_group[1], m_b_expert[1],
              m_w1, m_w3, m_w2)
    return x4.reshape(batch, seq, D_MODEL)
```

```python
import functools
import math

import jax
import jax.numpy as jnp
import numpy as np
from jax import lax
from jax.experimental import pallas as pl
from jax.experimental.pallas import tpu as pltpu

D_MODEL = 1024
EPS = 1e-6
SGU_BLOCK = 128
SGU_CHUNK = 64
CHUNK_SHIFT = SGU_CHUNK.bit_length() - 1
SGU_WIDTH = 2 * D_MODEL
SGU_GROUPS = 8
SGU_GROUP_DIM = SGU_WIDTH // SGU_GROUPS
N_HEADS = 16
HEAD_DIM = D_MODEL // N_HEADS
N_GROUPS = 4
EXPERTS_PER_GROUP = 8
N_EXPERTS = N_GROUPS * EXPERTS_PER_GROUP
GROUP_SHIFT = EXPERTS_PER_GROUP.bit_length() - 1
TOP_K = 2
D_EXPERT = D_MODEL // 2

LANES = 128
ROW_SUBLANES = D_MODEL // 2 // LANES
ISSUE_UNROLL = 8
VMEM_LIMIT_BYTES = 56 * 1024 * 1024

ROW_TILE = 512
SGU_TILE = 512
ROUTE_TILE = 512
EXPERT_ROWS = 512
MOVE_TILE = 512
MOVE_CHUNK = 64
ATT_Q = 1024
ATT_K = 256
PV_TILES = 2
HEAD_PAD = 2 * HEAD_DIM
ROUTER_ROWS = 40
AUX_ONE0 = HEAD_DIM
AUX_CUM0 = HEAD_DIM + 3

LOG2E = math.log2(math.e)

bf16 = jnp.bfloat16
f32 = jnp.float32
u32 = jnp.uint32
HI_HALF = np.uint32(0xFFFF0000)


def _dot(a, b, precision=None):
    return jnp.dot(a, b, preferred_element_type=f32, precision=precision)


def _dot_nt(a, b, precision=None):
    return lax.dot_general(a, b, (((1,), (1,)), ((), ())), preferred_element_type=f32,
                           precision=precision)


def _rms(x, g):
    ms = jnp.mean(x * x, axis=-1, keepdims=True)
    return x * lax.rsqrt(ms + EPS) * g


def _gelu_tanh(z):
    c = math.sqrt(2.0 / math.pi)
    return z * (0.5 * (1.0 + jnp.tanh(c * (z + 0.044715 * (z * z * z)))))


def _split3(c):
    hi = c.astype(bf16).astype(f32)
    r = c - hi
    mid = r.astype(bf16).astype(f32)
    lo = r - mid
    return hi, mid, lo


def _pack_rows(v):
    half = D_MODEL // 2
    lo = lax.bitcast_convert_type(v[:, :half].astype(bf16).astype(f32), u32) >> 16
    hi = lax.bitcast_convert_type(v[:, half:].astype(bf16).astype(f32), u32) & HI_HALF
    return (hi | lo).reshape(v.shape[0], ROW_SUBLANES, LANES)


def _unpack_rows(w):
    w = w.reshape(w.shape[0], D_MODEL // 2)
    lo = lax.bitcast_convert_type(w << 16, f32)
    hi = lax.bitcast_convert_type(w & HI_HALF, f32)
    return jnp.concatenate([lo, hi], axis=1)


def _const_spec(shape, single=False):
    nd = len(shape)
    mode = pl.Buffered(1) if single else None
    return pl.BlockSpec(shape, lambda *_: (0,) * nd, pipeline_mode=mode)


def _params(sem):
    return pltpu.CompilerParams(dimension_semantics=sem, vmem_limit_bytes=VMEM_LIMIT_BYTES)


def _router_logits_t(x_new, mn, wr3):
    hm = _rms(x_new, mn)
    h_hi = hm.astype(bf16)
    h_lo = (hm - h_hi.astype(f32)).astype(bf16)
    both = _dot(jnp.concatenate([h_hi, h_lo], axis=1), wr3)
    lg = both[:, :LANES] + both[:, LANES:]
    return lg.T[:ROUTER_ROWS, :]


def _sgu_kernel(x_ref, an_ref, win_ref, bin_ref, vn_ref, ws_ref, bst_ref, wout_ref, mn_ref,
                wr_ref, x1_ref, lg_ref, u_scr, v_scr, gated_scr):
    x = x_ref[...]
    h = _rms(x, an_ref[...]).astype(bf16)
    u_scr[...] = _gelu_tanh(_dot(h, win_ref[:, :SGU_WIDTH]) + bin_ref[:, :SGU_WIDTH])
    v = _gelu_tanh(_dot(h, win_ref[:, SGU_WIDTH:]) + bin_ref[:, SGU_WIDTH:])
    v_scr[...] = _rms(v, vn_ref[...]).astype(bf16)

    t_chunk = lax.broadcasted_iota(jnp.int32, (SGU_BLOCK, SGU_BLOCK), 0) >> CHUNK_SHIFT
    s_chunk = lax.broadcasted_iota(jnp.int32, (SGU_BLOCK, SGU_BLOCK), 1) >> CHUNK_SHIFT
    causal = t_chunk >= s_chunk
    for g in range(SGU_GROUPS):
        wsm = jnp.where(causal, ws_ref[g], 0.0).astype(bf16)
        cols = slice(g * SGU_GROUP_DIM, (g + 1) * SGU_GROUP_DIM)
        for sb in range(SGU_TILE // SGU_BLOCK):
            rows = slice(sb * SGU_BLOCK, (sb + 1) * SGU_BLOCK)
            mixed = _dot(wsm, v_scr[rows, cols]) + bst_ref[:, g:g + 1]
            gated_scr[rows, cols] = (u_scr[rows, cols] * mixed).astype(bf16)

    x1 = x + _dot(gated_scr[...], wout_ref[...])
    x1_ref[...] = x1
    lg_ref[...] = _router_logits_t(x1, mn_ref[...], wr_ref[...])


def _sgu(x2d, a_norm, w_in, b_in, v_norm, w_s, b_s_t, w_out, m_norm, w_router):
    T = x2d.shape[0]
    return pl.pallas_call(
        _sgu_kernel,
        grid=(T // SGU_TILE,),
        in_specs=[
            pl.BlockSpec((SGU_TILE, D_MODEL), lambda i: (i, 0)),
            _const_spec((1, D_MODEL)),
            _const_spec((D_MODEL, 2 * SGU_WIDTH), single=True),
            _const_spec((1, 2 * SGU_WIDTH)),
            _const_spec((1, SGU_WIDTH)),
            _const_spec((SGU_GROUPS, SGU_BLOCK, SGU_BLOCK)),
            _const_spec((SGU_BLOCK, SGU_GROUPS)),
            _const_spec((SGU_WIDTH, D_MODEL), single=True),
            _const_spec((1, D_MODEL)),
            _const_spec((2 * D_MODEL, 2 * LANES)),
        ],
        out_specs=[
            pl.BlockSpec((SGU_TILE, D_MODEL), lambda i: (i, 0)),
            pl.BlockSpec((ROUTER_ROWS, SGU_TILE), lambda i: (0, i)),
        ],
        out_shape=[
            jax.ShapeDtypeStruct((T, D_MODEL), f32),
            jax.ShapeDtypeStruct((ROUTER_ROWS, T), f32),
        ],
        scratch_shapes=[
            pltpu.VMEM((SGU_TILE, SGU_WIDTH), f32),
            pltpu.VMEM((SGU_TILE, SGU_WIDTH), bf16),
            pltpu.VMEM((SGU_TILE, SGU_WIDTH), bf16),
        ],
        compiler_params=_params(("parallel",)),
        name="sgu",
    )(x2d, a_norm, w_in, b_in, v_norm, w_s, b_s_t, w_out, m_norm, w_router)


def _route_kernel(lg_ref, be_ref, bg_ref, oi_ref, ow_ref, cnt_ref, carry_scr):
    step = pl.program_id(0)

    @pl.when(step == 0)
    def _():
        carry_scr[...] = jnp.zeros_like(carry_scr)

    lg = lg_ref[...]
    e_l = lg[0:N_EXPERTS, :] + be_ref[...]
    g_l = lg[N_EXPERTS:N_EXPERTS + 8, :] + bg_ref[...]
    g_row = lax.broadcasted_iota(jnp.int32, g_l.shape, 0).astype(f32)
    g_l = jnp.where(g_row < N_GROUPS, g_l, -jnp.inf)
    g_max = jnp.max(g_l, axis=0, keepdims=True)
    g_sel = jnp.min(jnp.where(g_l == g_max, g_row, 8.0), axis=0, keepdims=True)
    g_den = jnp.sum(jnp.exp(g_l - g_max), axis=0, keepdims=True)
    g_w = 1.0 / g_den

    e_row_i = lax.broadcasted_iota(jnp.int32, e_l.shape, 0)
    e_row = e_row_i.astype(f32)
    e_grp = (e_row_i >> GROUP_SHIFT).astype(f32)
    e_in = jnp.where(e_grp == g_sel, e_l, -jnp.inf)
    m1 = jnp.max(e_in, axis=0, keepdims=True)
    i1 = jnp.min(jnp.where(e_in == m1, e_row, float(N_EXPERTS)), axis=0, keepdims=True)
    e_in2 = jnp.where(e_row == i1, -jnp.inf, e_in)
    m2 = jnp.max(e_in2, axis=0, keepdims=True)
    i2 = jnp.min(jnp.where(e_in2 == m2, e_row, float(N_EXPERTS)), axis=0, keepdims=True)
    t = jnp.exp(m2 - m1)
    p1 = 1.0 / (1.0 + t)
    p2 = t / (1.0 + t)

    hit1 = e_row == i1
    hit2 = e_row == i2
    member = jnp.logical_or(hit1, hit2)
    tt = lg.shape[1]
    before = (lax.broadcasted_iota(jnp.int32, (tt, tt), 0)
              < lax.broadcasted_iota(jnp.int32, (tt, tt), 1))
    prefix = _dot(member.astype(bf16), before.astype(bf16))
    rank_full = prefix + carry_scr[...]
    rank1 = jnp.sum(jnp.where(hit1, rank_full, 0.0), axis=0, keepdims=True)
    rank2 = jnp.sum(jnp.where(hit2, rank_full, 0.0), axis=0, keepdims=True)
    carry_new = carry_scr[...] + jnp.sum(member.astype(f32), axis=1, keepdims=True)
    carry_scr[...] = carry_new

    oi_ref[...] = jnp.zeros(oi_ref.shape, jnp.int32)
    oi_ref[0:1, :] = i1.astype(jnp.int32)
    oi_ref[1:2, :] = i2.astype(jnp.int32)
    oi_ref[2:3, :] = rank1.astype(jnp.int32)
    oi_ref[3:4, :] = rank2.astype(jnp.int32)
    ow_ref[...] = jnp.zeros(ow_ref.shape, f32)
    ow_ref[0:1, :] = g_w * p1
    ow_ref[1:2, :] = g_w * p2
    cnt_ref[...] = jnp.broadcast_to(carry_new, cnt_ref.shape)


def _route(logits_t, b_expert, b_group):
    T = logits_t.shape[1]
    return pl.pallas_call(
        _route_kernel,
        grid=(T // ROUTE_TILE,),
        in_specs=[
            pl.BlockSpec((ROUTER_ROWS, ROUTE_TILE), lambda i: (0, i)),
            _const_spec((N_EXPERTS, 1)),
            _const_spec((8, 1)),
        ],
        out_specs=[
            pl.BlockSpec((8, ROUTE_TILE), lambda i: (0, i)),
            pl.BlockSpec((8, ROUTE_TILE), lambda i: (0, i)),
            _const_spec((N_EXPERTS, LANES)),
        ],
        out_shape=[
            jax.ShapeDtypeStruct((8, T), jnp.int32),
            jax.ShapeDtypeStruct((8, T), f32),
            jax.ShapeDtypeStruct((N_EXPERTS, LANES), f32),
        ],
        scratch_shapes=[pltpu.VMEM((N_EXPERTS, 1), f32)],
        compiler_params=_params(("arbitrary",)),
        name="route",
    )(logits_t, b_expert, b_group)


def _row_copy(src, s, dst, d, sem):
    return pltpu.make_async_copy(src.at[s], dst.at[d], sem)


def _wait_rows(buf, sem):
    pltpu.make_async_copy(buf, buf, sem).wait()


def _dispatch_kernel(n_blocks, pend_ref, padded_ref, nu_ref, dest_ref, x_ref, mn_ref, xs_hbm,
                     zero_scr, x3_scr, zsem, sem):
    @pl.when(pl.program_id(0) == 0)
    def _():
        zero_scr[...] = jnp.zeros_like(zero_scr)

        def fill(row0):
            return pltpu.make_async_copy(zero_scr, xs_hbm.at[pl.ds(row0, EXPERT_ROWS)], zsem)

        for e in range(N_EXPERTS):
            @pl.when(padded_ref[e] > 0)
            def _():
                fill(pend_ref[e] - EXPERT_ROWS).start()

            @pl.when(nu_ref[0] + e < n_blocks)
            def _():
                fill((nu_ref[0] + e) * EXPERT_ROWS).start()

        for e in range(N_EXPERTS):
            @pl.when(padded_ref[e] > 0)
            def _():
                fill(0).wait()

            @pl.when(nu_ref[0] + e < n_blocks)
            def _():
                fill(0).wait()

    i = pl.program_id(0)
    n_tiles = pl.num_programs(0) - 1

    def staged(rows):
        return _pack_rows(_rms(x_ref[rows, :], mn_ref[...]))

    @pl.when(i == 0)
    def _():
        x3_scr[0] = staged(slice(None))

    @pl.when(i > 0)
    def _():
        cur = (i - 1) % 3
        nxt = i % 3

        def chunk(c, carry):
            r0 = pl.multiple_of(c * MOVE_CHUNK, MOVE_CHUNK)
            for j in range(MOVE_CHUNK):
                for k in range(TOP_K):
                    _row_copy(x3_scr.at[cur], r0 + j, xs_hbm, dest_ref[k, r0 + j],
                              sem.at[cur]).start(priority=k)
            x3_scr[nxt, pl.ds(r0, MOVE_CHUNK)] = staged(pl.ds(r0, MOVE_CHUNK))
            return carry

        lax.fori_loop(0, MOVE_TILE // MOVE_CHUNK, chunk, 0)

        @pl.when(i > 1)
        def _():
            for k in range(TOP_K):
                _wait_rows(x3_scr.at[(i - 2) % 3], sem.at[(i - 2) % 3])

        @pl.when(i == n_tiles)
        def _():
            for k in range(TOP_K):
                _wait_rows(x3_scr.at[cur], sem.at[cur])


def _dispatch(pend, padded, n_used, dest_tiles, x2d, m_norm, n_rows):
    T = x2d.shape[0]
    n_blocks = n_rows // EXPERT_ROWS
    n_tiles = T // MOVE_TILE
    grid_spec = pltpu.PrefetchScalarGridSpec(
        num_scalar_prefetch=3,
        grid=(n_tiles + 1,),
        in_specs=[
            pl.BlockSpec((None, TOP_K, MOVE_TILE), lambda i, *_: (jnp.maximum(i - 1, 0), 0, 0),
                         memory_space=pltpu.SMEM),
            pl.BlockSpec((MOVE_TILE, D_MODEL), lambda i, *_: (jnp.minimum(i, n_tiles - 1), 0)),
            pl.BlockSpec((1, D_MODEL), lambda i, *_: (0, 0)),
        ],
        out_specs=pl.BlockSpec(memory_space=pl.ANY),
        scratch_shapes=[
            pltpu.VMEM((EXPERT_ROWS, ROW_SUBLANES, LANES), u32),
            pltpu.VMEM((3, MOVE_TILE, ROW_SUBLANES, LANES), u32),
            pltpu.SemaphoreType.DMA(()),
            pltpu.SemaphoreType.DMA((3,)),
        ],
    )
    return pl.pallas_call(
        functools.partial(_dispatch_kernel, n_blocks),
        grid_spec=grid_spec,
        out_shape=jax.ShapeDtypeStruct((n_rows, ROW_SUBLANES, LANES), u32),
        compiler_params=_params(("arbitrary",)),
        name="dispatch",
    )(pend, padded, n_used, dest_tiles, x2d, m_norm)


def _expert_kernel(n_blocks, ps_ref, nb_ref, nu_ref, xs_hbm, w1_ref, w3_ref, w2_ref,
                   ys_hbm, xbuf, ybuf, h_scr, w1_scr, w3_scr, w2_scr, xsem, ysem):
    e = pl.program_id(0)
    nblk = nb_ref[e]
    g0 = ps_ref[e]
    n_used = nu_ref[0]

    def rows(g):
        return pl.ds(pl.multiple_of(g * EXPERT_ROWS, EXPERT_ROWS), EXPERT_ROWS)

    def x_copy(g):
        return pltpu.make_async_copy(xs_hbm.at[rows(g)], xbuf.at[g % 2], xsem.at[g % 2])

    def y_copy(g):
        return pltpu.make_async_copy(ybuf.at[g % 2], ys_hbm.at[rows(g)], ysem.at[g % 2])

    def normed(slot):
        return _unpack_rows(xbuf[slot]).astype(bf16)

    @pl.when(e == 0)
    def _():
        x_copy(0).start(priority=1)
        x_copy(1).start(priority=1)
        x_copy(0).wait()
        h_scr[...] = normed(0)

    @pl.when(nblk > 0)
    def _():
        w1_scr[...] = w1_ref[...].astype(bf16)
        w3_scr[...] = w3_ref[...].astype(bf16)
        w2_scr[...] = w2_ref[...].astype(bf16)

        def body(g, h):
            @pl.when(g + 2 < n_used)
            def _():
                x_copy(g + 2).start(priority=1)

            @pl.when(g + 1 < n_used)
            def _():
                x_copy(g + 1).wait()

            @pl.when(g >= 2)
            def _():
                y_copy(g - 2).wait()

            h_next = normed((g + 1) % 2)
            a = _dot(h, w1_scr[...])
            c = _dot(h, w3_scr[...])
            hid = (a * jax.nn.sigmoid(a)) * c
            ybuf[g % 2] = _pack_rows(_dot(hid.astype(bf16), w2_scr[...]))
            y_copy(g).start(priority=1)
            return h_next

        h_scr[...] = lax.fori_loop(g0, g0 + nblk, body, h_scr[...])

    @pl.when(e == pl.num_programs(0) - 1)
    def _():
        y_copy(n_used - 2).wait()
        y_copy(n_used - 1).wait()
        ybuf[0] = jnp.zeros(ybuf.shape[1:], u32)

        def fill(b, c):
            r = pl.ds(pl.multiple_of(b * EXPERT_ROWS, EXPERT_ROWS), EXPERT_ROWS)
            cp = pltpu.make_async_copy(ybuf.at[0], ys_hbm.at[r], ysem.at[0])
            cp.start()
            cp.wait()
            return c

        lax.fori_loop(nu_ref[0], n_blocks, fill, 0)


def _experts(layer, pstart, seg_blocks, n_used, xs, w1, w3, w2):
    n_rows = xs.shape[0]
    n_blocks = n_rows // EXPERT_ROWS

    def w_map(e, *_):
        return (layer, e, 0, 0)

    block_buf = pltpu.VMEM((2, EXPERT_ROWS, ROW_SUBLANES, LANES), u32)
    grid_spec = pltpu.PrefetchScalarGridSpec(
        num_scalar_prefetch=3,
        grid=(N_EXPERTS,),
        in_specs=[
            pl.BlockSpec(memory_space=pl.ANY),
            pl.BlockSpec((None, None, D_MODEL, D_EXPERT), w_map),
            pl.BlockSpec((None, None, D_MODEL, D_EXPERT), w_map),
            pl.BlockSpec((None, None, D_EXPERT, D_MODEL), w_map),
        ],
        out_specs=pl.BlockSpec(memory_space=pl.ANY),
        scratch_shapes=[
            block_buf,
            block_buf,
            pltpu.VMEM((EXPERT_ROWS, D_MODEL), bf16),
            pltpu.VMEM((D_MODEL, D_EXPERT), bf16),
            pltpu.VMEM((D_MODEL, D_EXPERT), bf16),
            pltpu.VMEM((D_EXPERT, D_MODEL), bf16),
            pltpu.SemaphoreType.DMA((2,)),
            pltpu.SemaphoreType.DMA((2,)),
        ],
    )
    return pl.pallas_call(
        functools.partial(_expert_kernel, n_blocks),
        grid_spec=grid_spec,
        out_shape=jax.ShapeDtypeStruct((n_rows, ROW_SUBLANES, LANES), u32),
        compiler_params=_params(("arbitrary",)),
        name="experts",
    )(pstart, seg_blocks, n_used, xs, w1, w3, w2)


def _combine_kernel(dest_ref, next_ref, x_ref, w_ref, ys_hbm, out_ref, buf, sem):
    i = pl.program_id(0)
    slot = i % 2

    def gather(idx_ref, s):
        def start(j, c):
            for k in range(TOP_K):
                _row_copy(ys_hbm, idx_ref[k, j], buf.at[s, k], j, sem.at[s]).start(priority=k)
            return c

        lax.fori_loop(0, MOVE_TILE, start, 0, unroll=ISSUE_UNROLL)

    @pl.when(i == 0)
    def _():
        gather(dest_ref, slot)

    @pl.when(i + 1 < pl.num_programs(0))
    def _():
        gather(next_ref, 1 - slot)

    for k in range(TOP_K):
        _wait_rows(buf.at[slot, k], sem.at[slot])
    w = w_ref[...]
    y0 = _unpack_rows(buf[slot, 0])
    y1 = _unpack_rows(buf[slot, 1])
    out_ref[...] = x_ref[...] + (w[:, 0:1] * y0 + w[:, 1:2] * y1)


def _combine(dest_tiles, x2d, w_cols, ys):
    T = x2d.shape[0]
    n_tiles = T // MOVE_TILE
    return pl.pallas_call(
        _combine_kernel,
        grid=(n_tiles,),
        in_specs=[
            pl.BlockSpec((None, TOP_K, MOVE_TILE), lambda i: (i, 0, 0),
                         memory_space=pltpu.SMEM),
            pl.BlockSpec((None, TOP_K, MOVE_TILE),
                         lambda i: (jnp.minimum(i + 1, n_tiles - 1), 0, 0),
                         memory_space=pltpu.SMEM),
            pl.BlockSpec((MOVE_TILE, D_MODEL), lambda i: (i, 0)),
            pl.BlockSpec((MOVE_TILE, 8), lambda i: (i, 0)),
            pl.BlockSpec(memory_space=pl.ANY),
        ],
        out_specs=pl.BlockSpec((MOVE_TILE, D_MODEL), lambda i: (i, 0)),
        out_shape=jax.ShapeDtypeStruct((T, D_MODEL), f32),
        scratch_shapes=[
            pltpu.VMEM((2, TOP_K, MOVE_TILE, ROW_SUBLANES, LANES), u32),
            pltpu.SemaphoreType.DMA((2,)),
        ],
        compiler_params=_params(("arbitrary",)),
        name="combine",
    )(dest_tiles, dest_tiles, x2d, w_cols, ys)


def _moe(layer, x2d, logits_t, m_norm, b_group, b_expert, w1, w3, w2):
    T = x2d.shape[0]
    n_blocks = (T * TOP_K) // EXPERT_ROWS + N_EXPERTS
    n_rows = n_blocks * EXPERT_ROWS

    be = b_expert.reshape(N_EXPERTS, 1)
    bg = jnp.concatenate([b_group, jnp.zeros((8 - N_GROUPS,), f32)]).reshape(8, 1)
    oi, ow, cnt = _route(logits_t, be, bg)

    counts = cnt[:, 0].astype(jnp.int32)
    padded = (counts + EXPERT_ROWS - 1) // EXPERT_ROWS * EXPERT_ROWS
    pend = jnp.cumsum(padded)
    pstart = pend - padded
    n_used = (pend[-1] // EXPERT_ROWS).astype(jnp.int32).reshape(1)
    e_ids = jnp.arange(N_EXPERTS, dtype=jnp.int32)[:, None, None]
    seg_start = jnp.sum(jnp.where(oi[None, 0:2] == e_ids, pstart[:, None, None], 0), axis=0)
    dest = seg_start + oi[2:4]
    dest_tiles = dest.reshape(TOP_K, T // MOVE_TILE, MOVE_TILE).transpose(1, 0, 2)
    w_cols = ow.T

    xs = _dispatch(pend.astype(jnp.int32), padded, n_used, dest_tiles, x2d, m_norm, n_rows)
    ys = _experts(layer, (pstart // EXPERT_ROWS).astype(jnp.int32), padded // EXPERT_ROWS,
                  n_used, xs, w1, w3, w2)
    return _combine(dest_tiles, x2d, w_cols, ys)


def _log_sigmoid(x):
    return jnp.minimum(x, 0.0) - jnp.log1p(jnp.exp(-jnp.abs(x)))


def _proj_kernel(tiles_per_seq, x_ref, kvn_ref, bn_ref, wkt_ref, wv_ref, wf_ref, bf_ref,
                 wq_ref, wg_ref, gq_ref, gk_ref, q_ref, kt_ref, v_ref, sg_ref, carry_scr):
    i = pl.program_id(0)
    x = x_ref[...]
    hkv = _rms(x, kvn_ref[...]).astype(bf16)
    hq = _rms(x, bn_ref[...]).astype(bf16)

    v_ref[...] = _dot(hkv, wv_ref[...]).astype(bf16)
    sg_ref[...] = jax.nn.sigmoid(_dot(hq, wg_ref[...])).astype(bf16)

    logf = _log_sigmoid(_dot(hkv, wf_ref[...]) + bf_ref[...])
    tm = x.shape[0]
    incl = (lax.broadcasted_iota(jnp.int32, (tm, tm), 0)
            >= lax.broadcasted_iota(jnp.int32, (tm, tm), 1)).astype(f32)
    @pl.when(i % tiles_per_seq == 0)
    def _():
        carry_scr[...] = jnp.zeros_like(carry_scr)

    cum = _dot(incl, logf, precision=lax.Precision.HIGHEST) + carry_scr[...]
    carry_scr[...] = cum[tm - 1:tm, :]
    cum = cum * LOG2E
    cum_t = cum.T

    lane = lax.broadcasted_iota(jnp.int32, (tm, HEAD_PAD), 1)
    sub = lax.broadcasted_iota(jnp.int32, (HEAD_DIM, tm), 0)
    q_raw = _dot(hq, wq_ref[...])
    k_raw_t = _dot_nt(wkt_ref[...], hkv)
    scale = HEAD_DIM ** -0.5 * LOG2E

    for h in range(N_HEADS):
        sl = slice(h * HEAD_PAD, (h + 1) * HEAD_PAD)
        qb = q_raw[:, sl]
        q_ms = jnp.sum(qb * qb, axis=-1, keepdims=True) * (1.0 / HEAD_DIM)
        qn = qb * lax.rsqrt(q_ms + EPS) * gq_ref[...] * scale
        c_hi, c_mid, c_lo = _split3(cum[:, h:h + 1])
        qa = jnp.where(lane < HEAD_DIM, qn,
             jnp.where(lane < AUX_CUM0, 1.0,
             jnp.where(lane == AUX_CUM0, c_hi,
             jnp.where(lane == AUX_CUM0 + 1, c_mid,
             jnp.where(lane == AUX_CUM0 + 2, c_lo, 0.0)))))
        q_ref[:, sl] = qa.astype(bf16)

        kb = k_raw_t[h * HEAD_DIM:(h + 1) * HEAD_DIM, :]
        k_ms = jnp.sum(kb * kb, axis=0, keepdims=True) * (1.0 / HEAD_DIM)
        kn = kb * lax.rsqrt(k_ms + EPS) * gk_ref[...]
        t_hi, t_mid, t_lo = _split3(cum_t[h:h + 1, :])
        aux = jnp.where(sub == 0, -t_hi,
              jnp.where(sub == 1, -t_mid,
              jnp.where(sub == 2, -t_lo,
              jnp.where(sub < 6, 1.0, 0.0))))
        sl = slice(h * HEAD_PAD, (h + 1) * HEAD_PAD)
        ka = jnp.concatenate([kn, aux], axis=0).astype(bf16)
        for t in range(tm // ATT_K):
            kt_ref[t, sl, :] = ka[:, t * ATT_K:(t + 1) * ATT_K]


def _proj(x2d, seq, kv_norm, b_norm, wkt, wv, wf, bfv, wq, wg, gq, gk):
    T = x2d.shape[0]
    n_tiles = T // ROW_TILE
    qw = N_HEADS * HEAD_PAD
    return pl.pallas_call(
        functools.partial(_proj_kernel, seq // ROW_TILE),
        grid=(n_tiles,),
        in_specs=[
            pl.BlockSpec((ROW_TILE, D_MODEL), lambda i: (i, 0)),
            _const_spec((1, D_MODEL)),
            _const_spec((1, D_MODEL)),
            _const_spec((D_MODEL, D_MODEL), single=True),
            _const_spec((D_MODEL, D_MODEL), single=True),
            _const_spec((D_MODEL, LANES)),
            _const_spec((1, LANES)),
            _const_spec((D_MODEL, qw), single=True),
            _const_spec((D_MODEL, D_MODEL), single=True),
            _const_spec((1, HEAD_PAD)),
            _const_spec((HEAD_DIM, 1)),
        ],
        out_specs=[
            pl.BlockSpec((ROW_TILE, qw), lambda i: (i, 0)),
            pl.BlockSpec((ROW_TILE // ATT_K, qw, ATT_K), lambda i: (i, 0, 0)),
            pl.BlockSpec((ROW_TILE, D_MODEL), lambda i: (i, 0)),
            pl.BlockSpec((ROW_TILE, D_MODEL), lambda i: (i, 0)),
        ],
        out_shape=[
            jax.ShapeDtypeStruct((T, qw), bf16),
            jax.ShapeDtypeStruct((T // ATT_K, qw, ATT_K), bf16),
            jax.ShapeDtypeStruct((T, D_MODEL), bf16),
            jax.ShapeDtypeStruct((T, D_MODEL), bf16),
        ],
        scratch_shapes=[pltpu.VMEM((1, LANES), f32)],
        compiler_params=_params(("arbitrary",)),
        name="proj",
    )(x2d, kv_norm, b_norm, wkt, wv, wf, bfv, wq, wg, gq, gk)


def _attn_kernel(q_ref, kt_ref, v_ref, o_ref, s_scr, m_scr, l_scr, acc_scr):
    qi = pl.program_id(2)
    tiles = ATT_Q // ATT_K
    groups = ATT_K // LANES
    heads = [slice(hh * HEAD_PAD, (hh + 1) * HEAD_PAD) for hh in range(2)]
    m_scr[...] = jnp.full(m_scr.shape, -jnp.inf, f32)

    def lane_max(mx, s):
        for g in range(groups):
            mx = jnp.maximum(mx, s[:, g * LANES:(g + 1) * LANES])
        return mx

    def probs(s, mb):
        ps = [jnp.exp2(s[:, g * LANES:(g + 1) * LANES] - mb) for g in range(groups)]
        return ps, functools.reduce(lambda a, b: a + b, ps)

    def body_a(trip, c):
        for t in range(tiles):
            kt_idx = trip * tiles + t
            for hh in range(2):
                s = _dot(q_ref[:, heads[hh]], kt_ref[kt_idx, heads[hh], :])
                s_scr[hh, kt_idx] = s
                m_scr[hh] = lane_max(m_scr[hh], s)
        return c

    lax.fori_loop(0, qi, body_a, 0)
    for t in range(tiles):
        kt_idx = qi * tiles + t
        rows = ATT_Q - t * ATT_K
        visible = (lax.broadcasted_iota(jnp.int32, (rows, ATT_K), 1)
                   <= lax.broadcasted_iota(jnp.int32, (rows, ATT_K), 0))
        for hh in range(2):
            s = _dot(q_ref[t * ATT_K:, heads[hh]], kt_ref[kt_idx, heads[hh], :])
            s = jnp.where(visible, s, -jnp.inf)
            s_scr[hh, kt_idx, t * ATT_K:, :] = s
            m_scr[hh, t * ATT_K:, :] = lane_max(m_scr[hh, t * ATT_K:, :], s)

    for hh in range(2):
        row_max = jnp.max(m_scr[hh], axis=-1, keepdims=True)
        m_scr[hh] = jnp.broadcast_to(row_max, (ATT_Q, LANES))
    l_scr[...] = jnp.zeros(l_scr.shape, f32)
    acc_scr[...] = jnp.zeros(acc_scr.shape, f32)

    def body_b(trip, c):
        for hh in range(2):
            mb = m_scr[hh]
            lsum, acc = l_scr[hh], acc_scr[hh]
            for t0 in range(0, tiles, PV_TILES):
                kt_idx = trip * tiles + t0
                row0 = pl.multiple_of(kt_idx * ATT_K, PV_TILES * ATT_K)
                vb = v_ref[pl.ds(row0, PV_TILES * ATT_K), :]
                parts = []
                for t in range(PV_TILES):
                    ps, psum = probs(s_scr[hh, kt_idx + t], mb)
                    parts += ps
                    lsum = lsum + psum
                acc = acc + _dot(jnp.concatenate(parts, axis=1).astype(bf16), vb)
            l_scr[hh] = lsum
            acc_scr[hh] = acc
        return c

    lax.fori_loop(0, qi, body_b, 0)
    for t in range(tiles):
        kt_idx = qi * tiles + t
        row0 = pl.multiple_of(kt_idx * ATT_K, ATT_K)
        vb = v_ref[pl.ds(row0, ATT_K), :]
        for hh in range(2):
            ps, psum = probs(s_scr[hh, kt_idx, t * ATT_K:, :], m_scr[hh, t * ATT_K:, :])
            l_scr[hh, t * ATT_K:, :] += psum
            acc_scr[hh, t * ATT_K:, :] += _dot(jnp.concatenate(ps, axis=1).astype(bf16), vb)

    lane = lax.broadcasted_iota(jnp.int32, (ATT_Q, 2 * HEAD_DIM), 1)
    o0 = acc_scr[0] / jnp.sum(l_scr[0], axis=-1, keepdims=True)
    o1 = acc_scr[1] / jnp.sum(l_scr[1], axis=-1, keepdims=True)
    o_ref[...] = jnp.where(lane < HEAD_DIM, o0, o1).astype(bf16)


def _attention(q, kt, v, batch, seq):
    T = q.shape[0]
    nq = seq // ATT_Q
    nk = seq // ATT_K
    pairs = N_HEADS // 2
    return pl.pallas_call(
        _attn_kernel,
        grid=(batch, pairs, nq),
        in_specs=[
            pl.BlockSpec((ATT_Q, 2 * HEAD_PAD), lambda b, p, i: (b * nq + i, p)),
            pl.BlockSpec((nk, 2 * HEAD_PAD, ATT_K), lambda b, p, i: (b, p, 0)),
            pl.BlockSpec((seq, 2 * HEAD_DIM), lambda b, p, i: (b, p)),
        ],
        out_specs=pl.BlockSpec((ATT_Q, 2 * HEAD_DIM), lambda b, p, i: (b * nq + i, p)),
        out_shape=jax.ShapeDtypeStruct((T, D_MODEL), bf16),
        scratch_shapes=[
            pltpu.VMEM((2, nk, ATT_Q, ATT_K), f32),
            pltpu.VMEM((2, ATT_Q, LANES), f32),
            pltpu.VMEM((2, ATT_Q, LANES), f32),
            pltpu.VMEM((2, ATT_Q, 2 * HEAD_DIM), f32),
        ],
        compiler_params=_params(("parallel", "parallel", "parallel")),
        name="attn",
    )(q, kt, v)


def _attn_out_kernel(x_ref, o_ref, sg_ref, wo_ref, mn_ref, wr_ref, x3_ref, lg_ref):
    gated = (o_ref[...].astype(f32) * sg_ref[...].astype(f32)).astype(bf16)
    x3 = x_ref[...] + _dot(gated, wo_ref[...])
    x3_ref[...] = x3
    lg_ref[...] = _router_logits_t(x3, mn_ref[...], wr_ref[...])


def _attn_out(x2d, o, sg, wo, m_norm, w_router):
    T = x2d.shape[0]
    return pl.pallas_call(
        _attn_out_kernel,
        grid=(T // SGU_TILE,),
        in_specs=[
            pl.BlockSpec((SGU_TILE, D_MODEL), lambda i: (i, 0)),
            pl.BlockSpec((SGU_TILE, D_MODEL), lambda i: (i, 0)),
            pl.BlockSpec((SGU_TILE, D_MODEL), lambda i: (i, 0)),
            _const_spec((D_MODEL, D_MODEL)),
            _const_spec((1, D_MODEL)),
            _const_spec((2 * D_MODEL, 2 * LANES)),
        ],
        out_specs=[
            pl.BlockSpec((SGU_TILE, D_MODEL), lambda i: (i, 0)),
            pl.BlockSpec((ROUTER_ROWS, SGU_TILE), lambda i: (0, i)),
        ],
        out_shape=[
            jax.ShapeDtypeStruct((T, D_MODEL), f32),
            jax.ShapeDtypeStruct((ROUTER_ROWS, T), f32),
        ],
        compiler_params=_params(("parallel",)),
        name="attn_out",
    )(x2d, o, sg, wo, m_norm, w_router)


def _router_weight(w_group, w_expert):
    pad = jnp.zeros((D_MODEL, LANES - N_EXPERTS - N_GROUPS), f32)
    w = jnp.concatenate([w_expert, w_group, pad], axis=1)
    w_hi = w.astype(bf16)
    w_lo = (w - w_hi.astype(f32)).astype(bf16)
    top = jnp.concatenate([w_hi, w_lo], axis=1)
    bottom = jnp.concatenate([w_hi, jnp.zeros_like(w_hi)], axis=1)
    return jnp.concatenate([top, bottom], axis=0)


def kernel(x, a_norm, a_w_in, a_b_in, a_v_norm, a_w_s, a_b_s, a_w_out, kv_norm, kv_w, kv_b_f,
           k_norm, b_norm, b_w_qg, q_norm, b_w_out, m_norm, m_w_group, m_b_group, m_w_expert,
           m_b_expert, m_w1, m_w3, m_w2):
    batch, seq, _ = x.shape
    T = batch * seq
    x2d = x.reshape(T, D_MODEL)

    x1, lg0 = _sgu(
        x2d, a_norm[0].reshape(1, -1), a_w_in[0].astype(bf16), a_b_in[0].reshape(1, -1),
        a_v_norm[0].reshape(1, -1), a_w_s[0], a_b_s[0].T, a_w_out[0].astype(bf16),
        m_norm[0].reshape(1, -1), _router_weight(m_w_group[0], m_w_expert[0]))
    x2 = _moe(0, x1, lg0, m_norm[0].reshape(1, -1), m_b_group[0], m_b_expert[0],
              m_w1, m_w3, m_w2)

    wk = kv_w[:, :D_MODEL]
    wv = kv_w[:, D_MODEL:2 * D_MODEL]
    wf = jnp.pad(kv_w[:, 2 * D_MODEL:], ((0, 0), (0, LANES - N_HEADS)))
    bfv = jnp.pad(kv_b_f, (0, LANES - N_HEADS)).reshape(1, LANES)
    wq = b_w_qg[0][:, :D_MODEL]
    wg = b_w_qg[0][:, D_MODEL:]
    gq = jnp.pad(q_norm[0], (0, HEAD_PAD - HEAD_DIM)).reshape(1, HEAD_PAD)
    gk = k_norm.reshape(HEAD_DIM, 1)
    wq_pad = jnp.pad(wq.reshape(D_MODEL, N_HEADS, HEAD_DIM),
                     ((0, 0), (0, 0), (0, HEAD_PAD - HEAD_DIM))).reshape(D_MODEL, -1)
    q, kt, v, sg = _proj(
        x2, seq, kv_norm.reshape(1, -1), b_norm[0].reshape(1, -1),
        wk.T.astype(bf16), wv.astype(bf16), wf.astype(bf16), bfv,
        wq_pad.astype(bf16), wg.astype(bf16), gq, gk)
    o = _attention(q, kt, v, batch, seq)
    x3, lg1 = _attn_out(x2, o, sg, b_w_out[0].astype(bf16), m_norm[1].reshape(1, -1),
                        _router_weight(m_w_group[1], m_w_expert[1]))
    x4 = _moe(1, x3, lg1, m_norm[1].reshape(1, -1), m_b_group[1], m_b_expert[1],
              m_w1, m_w3, m_w2)
    return x4.reshape(batch, seq, D_MODEL)
```

```python
import functools
import math

import jax
import jax.numpy as jnp
import numpy as np
from jax import lax
from jax.experimental import pallas as pl
from jax.experimental.pallas import tpu as pltpu

D_MODEL = 1024
EPS = 1e-6
SGU_BLOCK = 128
SGU_CHUNK = 64
CHUNK_SHIFT = SGU_CHUNK.bit_length() - 1
SGU_WIDTH = 2 * D_MODEL
SGU_GROUPS = 8
SGU_GROUP_DIM = SGU_WIDTH // SGU_GROUPS
N_HEADS = 16
HEAD_DIM = D_MODEL // N_HEADS
N_GROUPS = 4
EXPERTS_PER_GROUP = 8
N_EXPERTS = N_GROUPS * EXPERTS_PER_GROUP
GROUP_SHIFT = EXPERTS_PER_GROUP.bit_length() - 1
TOP_K = 2
D_EXPERT = D_MODEL // 2

LANES = 128
ROW_SUBLANES = D_MODEL // 2 // LANES
ISSUE_UNROLL = 8
VMEM_LIMIT_BYTES = 56 * 1024 * 1024

ROW_TILE = 256
SGU_TILE = 512
ROUTE_TILE = 512
EXPERT_ROWS = 512
MOVE_TILE = 512
MOVE_CHUNK = 64
ATT_Q = 1024
ATT_K = 256
PV_TILES = 2
HEAD_PAD = 2 * HEAD_DIM
ROUTER_ROWS = 40
AUX_ONE0 = HEAD_DIM
AUX_CUM0 = HEAD_DIM + 3

LOG2E = math.log2(math.e)

bf16 = jnp.bfloat16
f32 = jnp.float32
u32 = jnp.uint32
HI_HALF = np.uint32(0xFFFF0000)


def _dot(a, b, precision=None):
    return jnp.dot(a, b, preferred_element_type=f32, precision=precision)


def _dot_nt(a, b, precision=None):
    return lax.dot_general(a, b, (((1,), (1,)), ((), ())), preferred_element_type=f32,
                           precision=precision)


def _rms(x, g):
    ms = jnp.mean(x * x, axis=-1, keepdims=True)
    return x * lax.rsqrt(ms + EPS) * g


def _gelu_tanh(z):
    c = math.sqrt(2.0 / math.pi)
    return z * (0.5 * (1.0 + jnp.tanh(c * (z + 0.044715 * (z * z * z)))))


def _split3(c):
    hi = c.astype(bf16).astype(f32)
    r = c - hi
    mid = r.astype(bf16).astype(f32)
    lo = r - mid
    return hi, mid, lo


def _pack_rows(v):
    half = D_MODEL // 2
    lo = lax.bitcast_convert_type(v[:, :half].astype(bf16).astype(f32), u32) >> 16
    hi = lax.bitcast_convert_type(v[:, half:].astype(bf16).astype(f32), u32) & HI_HALF
    return (hi | lo).reshape(v.shape[0], ROW_SUBLANES, LANES)


def _unpack_rows(w):
    w = w.reshape(w.shape[0], D_MODEL // 2)
    lo = lax.bitcast_convert_type(w << 16, f32)
    hi = lax.bitcast_convert_type(w & HI_HALF, f32)
    return jnp.concatenate([lo, hi], axis=1)


def _const_spec(shape, single=False):
    nd = len(shape)
    mode = pl.Buffered(1) if single else None
    return pl.BlockSpec(shape, lambda *_: (0,) * nd, pipeline_mode=mode)


def _params(sem):
    return pltpu.CompilerParams(dimension_semantics=sem, vmem_limit_bytes=VMEM_LIMIT_BYTES)


def _router_logits_t(x_new, mn, wr3):
    hm = _rms(x_new, mn)
    h_hi = hm.astype(bf16)
    h_lo = (hm - h_hi.astype(f32)).astype(bf16)
    both = _dot(jnp.concatenate([h_hi, h_lo], axis=1), wr3)
    lg = both[:, :LANES] + both[:, LANES:]
    return lg.T[:ROUTER_ROWS, :]


def _sgu_kernel(x_ref, an_ref, win_ref, bin_ref, vn_ref, ws_ref, bst_ref, wout_ref, mn_ref,
                wr_ref, x1_ref, lg_ref, u_scr, v_scr, gated_scr):
    x = x_ref[...]
    h = _rms(x, an_ref[...]).astype(bf16)
    u_scr[...] = _gelu_tanh(_dot(h, win_ref[:, :SGU_WIDTH]) + bin_ref[:, :SGU_WIDTH])
    v = _gelu_tanh(_dot(h, win_ref[:, SGU_WIDTH:]) + bin_ref[:, SGU_WIDTH:])
    v_scr[...] = _rms(v, vn_ref[...]).astype(bf16)

    t_chunk = lax.broadcasted_iota(jnp.int32, (SGU_BLOCK, SGU_BLOCK), 0) >> CHUNK_SHIFT
    s_chunk = lax.broadcasted_iota(jnp.int32, (SGU_BLOCK, SGU_BLOCK), 1) >> CHUNK_SHIFT
    causal = t_chunk >= s_chunk
    for g in range(SGU_GROUPS):
        wsm = jnp.where(causal, ws_ref[g], 0.0).astype(bf16)
        cols = slice(g * SGU_GROUP_DIM, (g + 1) * SGU_GROUP_DIM)
        for sb in range(SGU_TILE // SGU_BLOCK):
            rows = slice(sb * SGU_BLOCK, (sb + 1) * SGU_BLOCK)
            mixed = _dot(wsm, v_scr[rows, cols]) + bst_ref[:, g:g + 1]
            gated_scr[rows, cols] = (u_scr[rows, cols] * mixed).astype(bf16)

    x1 = x + _dot(gated_scr[...], wout_ref[...])
    x1_ref[...] = x1
    lg_ref[...] = _router_logits_t(x1, mn_ref[...], wr_ref[...])


def _sgu(x2d, a_norm, w_in, b_in, v_norm, w_s, b_s_t, w_out, m_norm, w_router):
    T = x2d.shape[0]
    return pl.pallas_call(
        _sgu_kernel,
        grid=(T // SGU_TILE,),
        in_specs=[
            pl.BlockSpec((SGU_TILE, D_MODEL), lambda i: (i, 0)),
            _const_spec((1, D_MODEL)),
            _const_spec((D_MODEL, 2 * SGU_WIDTH), single=True),
            _const_spec((1, 2 * SGU_WIDTH)),
            _const_spec((1, SGU_WIDTH)),
            _const_spec((SGU_GROUPS, SGU_BLOCK, SGU_BLOCK)),
            _const_spec((SGU_BLOCK, SGU_GROUPS)),
            _const_spec((SGU_WIDTH, D_MODEL), single=True),
            _const_spec((1, D_MODEL)),
            _const_spec((2 * D_MODEL, 2 * LANES)),
        ],
        out_specs=[
            pl.BlockSpec((SGU_TILE, D_MODEL), lambda i: (i, 0)),
            pl.BlockSpec((ROUTER_ROWS, SGU_TILE), lambda i: (0, i)),
        ],
        out_shape=[
            jax.ShapeDtypeStruct((T, D_MODEL), f32),
            jax.ShapeDtypeStruct((ROUTER_ROWS, T), f32),
        ],
        scratch_shapes=[
            pltpu.VMEM((SGU_TILE, SGU_WIDTH), f32),
            pltpu.VMEM((SGU_TILE, SGU_WIDTH), bf16),
            pltpu.VMEM((SGU_TILE, SGU_WIDTH), bf16),
        ],
        compiler_params=_params(("parallel",)),
        name="sgu",
    )(x2d, a_norm, w_in, b_in, v_norm, w_s, b_s_t, w_out, m_norm, w_router)


def _route_kernel(lg_ref, be_ref, bg_ref, oi_ref, ow_ref, cnt_ref, carry_scr):
    step = pl.program_id(0)

    @pl.when(step == 0)
    def _():
        carry_scr[...] = jnp.zeros_like(carry_scr)

    lg = lg_ref[...]
    e_l = lg[0:N_EXPERTS, :] + be_ref[...]
    g_l = lg[N_EXPERTS:N_EXPERTS + 8, :] + bg_ref[...]
    g_row = lax.broadcasted_iota(jnp.int32, g_l.shape, 0).astype(f32)
    g_l = jnp.where(g_row < N_GROUPS, g_l, -jnp.inf)
    g_max = jnp.max(g_l, axis=0, keepdims=True)
    g_sel = jnp.min(jnp.where(g_l == g_max, g_row, 8.0), axis=0, keepdims=True)
    g_den = jnp.sum(jnp.exp(g_l - g_max), axis=0, keepdims=True)
    g_w = 1.0 / g_den

    e_row_i = lax.broadcasted_iota(jnp.int32, e_l.shape, 0)
    e_row = e_row_i.astype(f32)
    e_grp = (e_row_i >> GROUP_SHIFT).astype(f32)
    e_in = jnp.where(e_grp == g_sel, e_l, -jnp.inf)
    m1 = jnp.max(e_in, axis=0, keepdims=True)
    i1 = jnp.min(jnp.where(e_in == m1, e_row, float(N_EXPERTS)), axis=0, keepdims=True)
    e_in2 = jnp.where(e_row == i1, -jnp.inf, e_in)
    m2 = jnp.max(e_in2, axis=0, keepdims=True)
    i2 = jnp.min(jnp.where(e_in2 == m2, e_row, float(N_EXPERTS)), axis=0, keepdims=True)
    t = jnp.exp(m2 - m1)
    p1 = 1.0 / (1.0 + t)
    p2 = t / (1.0 + t)

    hit1 = e_row == i1
    hit2 = e_row == i2
    member = jnp.logical_or(hit1, hit2)
    tt = lg.shape[1]
    before = (lax.broadcasted_iota(jnp.int32, (tt, tt), 0)
              < lax.broadcasted_iota(jnp.int32, (tt, tt), 1))
    prefix = _dot(member.astype(bf16), before.astype(bf16))
    rank_full = prefix + carry_scr[...]
    rank1 = jnp.sum(jnp.where(hit1, rank_full, 0.0), axis=0, keepdims=True)
    rank2 = jnp.sum(jnp.where(hit2, rank_full, 0.0), axis=0, keepdims=True)
    carry_new = carry_scr[...] + jnp.sum(member.astype(f32), axis=1, keepdims=True)
    carry_scr[...] = carry_new

    oi_ref[...] = jnp.zeros(oi_ref.shape, jnp.int32)
    oi_ref[0:1, :] = i1.astype(jnp.int32)
    oi_ref[1:2, :] = i2.astype(jnp.int32)
    oi_ref[2:3, :] = rank1.astype(jnp.int32)
    oi_ref[3:4, :] = rank2.astype(jnp.int32)
    ow_ref[...] = jnp.zeros(ow_ref.shape, f32)
    ow_ref[0:1, :] = g_w * p1
    ow_ref[1:2, :] = g_w * p2
    cnt_ref[...] = jnp.broadcast_to(carry_new, cnt_ref.shape)


def _route(logits_t, b_expert, b_group):
    T = logits_t.shape[1]
    return pl.pallas_call(
        _route_kernel,
        grid=(T // ROUTE_TILE,),
        in_specs=[
            pl.BlockSpec((ROUTER_ROWS, ROUTE_TILE), lambda i: (0, i)),
            _const_spec((N_EXPERTS, 1)),
            _const_spec((8, 1)),
        ],
        out_specs=[
            pl.BlockSpec((8, ROUTE_TILE), lambda i: (0, i)),
            pl.BlockSpec((8, ROUTE_TILE), lambda i: (0, i)),
            _const_spec((N_EXPERTS, LANES)),
        ],
        out_shape=[
            jax.ShapeDtypeStruct((8, T), jnp.int32),
            jax.ShapeDtypeStruct((8, T), f32),
            jax.ShapeDtypeStruct((N_EXPERTS, LANES), f32),
        ],
        scratch_shapes=[pltpu.VMEM((N_EXPERTS, 1), f32)],
        compiler_params=_params(("arbitrary",)),
        name="route",
    )(logits_t, b_expert, b_group)


def _row_copy(src, s, dst, d, sem):
    return pltpu.make_async_copy(src.at[s], dst.at[d], sem)


def _wait_rows(buf, sem):
    pltpu.make_async_copy(buf, buf, sem).wait()


def _dispatch_kernel(n_blocks, pend_ref, padded_ref, nu_ref, dest_ref, x_ref, mn_ref, xs_hbm,
                     zero_scr, x3_scr, zsem, sem):
    @pl.when(pl.program_id(0) == 0)
    def _():
        zero_scr[...] = jnp.zeros_like(zero_scr)

        def fill(row0):
            return pltpu.make_async_copy(zero_scr, xs_hbm.at[pl.ds(row0, EXPERT_ROWS)], zsem)

        for e in range(N_EXPERTS):
            @pl.when(padded_ref[e] > 0)
            def _():
                fill(pend_ref[e] - EXPERT_ROWS).start()

            @pl.when(nu_ref[0] + e < n_blocks)
            def _():
                fill((nu_ref[0] + e) * EXPERT_ROWS).start()

        for e in range(N_EXPERTS):
            @pl.when(padded_ref[e] > 0)
            def _():
                fill(0).wait()

            @pl.when(nu_ref[0] + e < n_blocks)
            def _():
                fill(0).wait()

    i = pl.program_id(0)
    n_tiles = pl.num_programs(0) - 1

    def staged(rows):
        return _pack_rows(_rms(x_ref[rows, :], mn_ref[...]))

    @pl.when(i == 0)
    def _():
        x3_scr[0] = staged(slice(None))

    @pl.when(i > 0)
    def _():
        cur = (i - 1) % 3
        nxt = i % 3

        def chunk(c, carry):
            r0 = pl.multiple_of(c * MOVE_CHUNK, MOVE_CHUNK)
            for j in range(MOVE_CHUNK):
                for k in range(TOP_K):
                    _row_copy(x3_scr.at[cur], r0 + j, xs_hbm, dest_ref[k, r0 + j],
                              sem.at[cur]).start(priority=k)
            x3_scr[nxt, pl.ds(r0, MOVE_CHUNK)] = staged(pl.ds(r0, MOVE_CHUNK))
            return carry

        lax.fori_loop(0, MOVE_TILE // MOVE_CHUNK, chunk, 0)

        @pl.when(i > 1)
        def _():
            for k in range(TOP_K):
                _wait_rows(x3_scr.at[(i - 2) % 3], sem.at[(i - 2) % 3])

        @pl.when(i == n_tiles)
        def _():
            for k in range(TOP_K):
                _wait_rows(x3_scr.at[cur], sem.at[cur])


def _dispatch(pend, padded, n_used, dest_tiles, x2d, m_norm, n_rows):
    T = x2d.shape[0]
    n_blocks = n_rows // EXPERT_ROWS
    n_tiles = T // MOVE_TILE
    grid_spec = pltpu.PrefetchScalarGridSpec(
        num_scalar_prefetch=3,
        grid=(n_tiles + 1,),
        in_specs=[
            pl.BlockSpec((None, TOP_K, MOVE_TILE), lambda i, *_: (jnp.maximum(i - 1, 0), 0, 0),
                         memory_space=pltpu.SMEM),
            pl.BlockSpec((MOVE_TILE, D_MODEL), lambda i, *_: (jnp.minimum(i, n_tiles - 1), 0)),
            pl.BlockSpec((1, D_MODEL), lambda i, *_: (0, 0)),
        ],
        out_specs=pl.BlockSpec(memory_space=pl.ANY),
        scratch_shapes=[
            pltpu.VMEM((EXPERT_ROWS, ROW_SUBLANES, LANES), u32),
            pltpu.VMEM((3, MOVE_TILE, ROW_SUBLANES, LANES), u32),
            pltpu.SemaphoreType.DMA(()),
            pltpu.SemaphoreType.DMA((3,)),
        ],
    )
    return pl.pallas_call(
        functools.partial(_dispatch_kernel, n_blocks),
        grid_spec=grid_spec,
        out_shape=jax.ShapeDtypeStruct((n_rows, ROW_SUBLANES, LANES), u32),
        compiler_params=_params(("arbitrary",)),
        name="dispatch",
    )(pend, padded, n_used, dest_tiles, x2d, m_norm)


def _expert_kernel(n_blocks, ps_ref, nb_ref, nu_ref, xs_hbm, w1_ref, w3_ref, w2_ref,
                   ys_hbm, xbuf, ybuf, h_scr, w1_scr, w3_scr, w2_scr, xsem, ysem):
    e = pl.program_id(0)
    nblk = nb_ref[e]
    g0 = ps_ref[e]
    n_used = nu_ref[0]

    def rows(g):
        return pl.ds(pl.multiple_of(g * EXPERT_ROWS, EXPERT_ROWS), EXPERT_ROWS)

    def x_copy(g):
        return pltpu.make_async_copy(xs_hbm.at[rows(g)], xbuf.at[g % 2], xsem.at[g % 2])

    def y_copy(g):
        return pltpu.make_async_copy(ybuf.at[g % 2], ys_hbm.at[rows(g)], ysem.at[g % 2])

    def normed(slot):
        return _unpack_rows(xbuf[slot]).astype(bf16)

    @pl.when(e == 0)
    def _():
        x_copy(0).start(priority=1)
        x_copy(1).start(priority=1)
        x_copy(0).wait()
        h_scr[...] = normed(0)

    @pl.when(nblk > 0)
    def _():
        w1_scr[...] = w1_ref[...].astype(bf16)
        w3_scr[...] = w3_ref[...].astype(bf16)
        w2_scr[...] = w2_ref[...].astype(bf16)

        def body(g, h):
            @pl.when(g + 2 < n_used)
            def _():
                x_copy(g + 2).start(priority=1)

            @pl.when(g + 1 < n_used)
            def _():
                x_copy(g + 1).wait()

            @pl.when(g >= 2)
            def _():
                y_copy(g - 2).wait()

            h_next = normed((g + 1) % 2)
            a = _dot(h, w1_scr[...])
            c = _dot(h, w3_scr[...])
            hid = (a * jax.nn.sigmoid(a)) * c
            ybuf[g % 2] = _pack_rows(_dot(hid.astype(bf16), w2_scr[...]))
            y_copy(g).start(priority=1)
            return h_next

        h_scr[...] = lax.fori_loop(g0, g0 + nblk, body, h_scr[...])

    @pl.when(e == pl.num_programs(0) - 1)
    def _():
        y_copy(n_used - 2).wait()
        y_copy(n_used - 1).wait()
        ybuf[0] = jnp.zeros(ybuf.shape[1:], u32)

        def fill(b):
            r = pl.ds(pl.multiple_of(b * EXPERT_ROWS, EXPERT_ROWS), EXPERT_ROWS)
            return pltpu.make_async_copy(ybuf.at[0], ys_hbm.at[r], ysem.at[0])

        def start(b, c):
            fill(b).start()
            return c

        def wait(b, c):
            fill(b).wait()
            return c

        lax.fori_loop(nu_ref[0], n_blocks, start, 0)
        lax.fori_loop(nu_ref[0], n_blocks, wait, 0)


def _experts(layer, pstart, seg_blocks, n_used, xs, w1, w3, w2):
    n_rows = xs.shape[0]
    n_blocks = n_rows // EXPERT_ROWS

    def w_map(e, *_):
        return (layer, e, 0, 0)

    block_buf = pltpu.VMEM((2, EXPERT_ROWS, ROW_SUBLANES, LANES), u32)
    grid_spec = pltpu.PrefetchScalarGridSpec(
        num_scalar_prefetch=3,
        grid=(N_EXPERTS,),
        in_specs=[
            pl.BlockSpec(memory_space=pl.ANY),
            pl.BlockSpec((None, None, D_MODEL, D_EXPERT), w_map),
            pl.BlockSpec((None, None, D_MODEL, D_EXPERT), w_map),
            pl.BlockSpec((None, None, D_EXPERT, D_MODEL), w_map),
        ],
        out_specs=pl.BlockSpec(memory_space=pl.ANY),
        scratch_shapes=[
            block_buf,
            block_buf,
            pltpu.VMEM((EXPERT_ROWS, D_MODEL), bf16),
            pltpu.VMEM((D_MODEL, D_EXPERT), bf16),
            pltpu.VMEM((D_MODEL, D_EXPERT), bf16),
            pltpu.VMEM((D_EXPERT, D_MODEL), bf16),
            pltpu.SemaphoreType.DMA((2,)),
            pltpu.SemaphoreType.DMA((2,)),
        ],
    )
    return pl.pallas_call(
        functools.partial(_expert_kernel, n_blocks),
        grid_spec=grid_spec,
        out_shape=jax.ShapeDtypeStruct((n_rows, ROW_SUBLANES, LANES), u32),
        compiler_params=_params(("arbitrary",)),
        name="experts",
    )(pstart, seg_blocks, n_used, xs, w1, w3, w2)


def _combine_kernel(dest_ref, next_ref, x_ref, w_ref, ys_hbm, out_ref, buf, sem):
    i = pl.program_id(0)
    slot = i % 2

    def gather(idx_ref, s):
        def start(j, c):
            for k in range(TOP_K):
                _row_copy(ys_hbm, idx_ref[k, j], buf.at[s, k], j, sem.at[s]).start(priority=k)
            return c

        lax.fori_loop(0, MOVE_TILE, start, 0, unroll=ISSUE_UNROLL)

    @pl.when(i == 0)
    def _():
        gather(dest_ref, slot)

    @pl.when(i + 1 < pl.num_programs(0))
    def _():
        gather(next_ref, 1 - slot)

    for k in range(TOP_K):
        _wait_rows(buf.at[slot, k], sem.at[slot])
    w = w_ref[...]
    y0 = _unpack_rows(buf[slot, 0])
    y1 = _unpack_rows(buf[slot, 1])
    out_ref[...] = x_ref[...] + (w[:, 0:1] * y0 + w[:, 1:2] * y1)


def _combine(dest_tiles, x2d, w_cols, ys):
    T = x2d.shape[0]
    n_tiles = T // MOVE_TILE
    return pl.pallas_call(
        _combine_kernel,
        grid=(n_tiles,),
        in_specs=[
            pl.BlockSpec((None, TOP_K, MOVE_TILE), lambda i: (i, 0, 0),
                         memory_space=pltpu.SMEM),
            pl.BlockSpec((None, TOP_K, MOVE_TILE),
                         lambda i: (jnp.minimum(i + 1, n_tiles - 1), 0, 0),
                         memory_space=pltpu.SMEM),
            pl.BlockSpec((MOVE_TILE, D_MODEL), lambda i: (i, 0)),
            pl.BlockSpec((MOVE_TILE, 8), lambda i: (i, 0)),
            pl.BlockSpec(memory_space=pl.ANY),
        ],
        out_specs=pl.BlockSpec((MOVE_TILE, D_MODEL), lambda i: (i, 0)),
        out_shape=jax.ShapeDtypeStruct((T, D_MODEL), f32),
        scratch_shapes=[
            pltpu.VMEM((2, TOP_K, MOVE_TILE, ROW_SUBLANES, LANES), u32),
            pltpu.SemaphoreType.DMA((2,)),
        ],
        compiler_params=_params(("arbitrary",)),
        name="combine",
    )(dest_tiles, dest_tiles, x2d, w_cols, ys)


def _moe(layer, x2d, logits_t, m_norm, b_group, b_expert, w1, w3, w2):
    T = x2d.shape[0]
    n_blocks = (T * TOP_K) // EXPERT_ROWS + N_EXPERTS
    n_rows = n_blocks * EXPERT_ROWS

    be = b_expert.reshape(N_EXPERTS, 1)
    bg = jnp.concatenate([b_group, jnp.zeros((8 - N_GROUPS,), f32)]).reshape(8, 1)
    oi, ow, cnt = _route(logits_t, be, bg)

    counts = cnt[:, 0].astype(jnp.int32)
    padded = (counts + EXPERT_ROWS - 1) // EXPERT_ROWS * EXPERT_ROWS
    pend = jnp.cumsum(padded)
    pstart = pend - padded
    n_used = (pend[-1] // EXPERT_ROWS).astype(jnp.int32).reshape(1)
    e_ids = jnp.arange(N_EXPERTS, dtype=jnp.int32)[:, None, None]
    seg_start = jnp.sum(jnp.where(oi[None, 0:2] == e_ids, pstart[:, None, None], 0), axis=0)
    dest = seg_start + oi[2:4]
    dest_tiles = dest.reshape(TOP_K, T // MOVE_TILE, MOVE_TILE).transpose(1, 0, 2)
    w_cols = ow.T

    xs = _dispatch(pend.astype(jnp.int32), padded, n_used, dest_tiles, x2d, m_norm, n_rows)
    ys = _experts(layer, (pstart // EXPERT_ROWS).astype(jnp.int32), padded // EXPERT_ROWS,
                  n_used, xs, w1, w3, w2)
    return _combine(dest_tiles, x2d, w_cols, ys)


def _log_sigmoid(x):
    return jnp.minimum(x, 0.0) - jnp.log1p(jnp.exp(-jnp.abs(x)))


def _proj_kernel(tiles_per_seq, x_ref, kvn_ref, bn_ref, wkt_ref, wv_ref, wf_ref, bf_ref,
                 wq_ref, wg_ref, gq_ref, gk_ref, q_ref, kt_ref, v_ref, sg_ref, carry_scr):
    i = pl.program_id(0)
    x = x_ref[...]
    hkv = _rms(x, kvn_ref[...]).astype(bf16)
    hq = _rms(x, bn_ref[...]).astype(bf16)

    v_ref[...] = _dot(hkv, wv_ref[...]).astype(bf16)
    sg_ref[...] = jax.nn.sigmoid(_dot(hq, wg_ref[...])).astype(bf16)

    logf = _log_sigmoid(_dot(hkv, wf_ref[...]) + bf_ref[...])
    tm = x.shape[0]
    incl = (lax.broadcasted_iota(jnp.int32, (tm, tm), 0)
            >= lax.broadcasted_iota(jnp.int32, (tm, tm), 1)).astype(f32)
    @pl.when(i % tiles_per_seq == 0)
    def _():
        carry_scr[...] = jnp.zeros_like(carry_scr)

    cum = _dot(incl, logf, precision=lax.Precision.HIGHEST) + carry_scr[...]
    carry_scr[...] = cum[tm - 1:tm, :]
    cum = cum * LOG2E
    cum_t = cum.T

    lane = lax.broadcasted_iota(jnp.int32, (tm, HEAD_PAD), 1)
    sub = lax.broadcasted_iota(jnp.int32, (HEAD_DIM, tm), 0)
    q_raw = _dot(hq, wq_ref[...])
    k_raw_t = _dot_nt(wkt_ref[...], hkv)
    scale = HEAD_DIM ** -0.5 * LOG2E

    for h in range(N_HEADS):
        sl = slice(h * HEAD_PAD, (h + 1) * HEAD_PAD)
        qb = q_raw[:, sl]
        q_ms = jnp.sum(qb * qb, axis=-1, keepdims=True) * (1.0 / HEAD_DIM)
        qn = qb * lax.rsqrt(q_ms + EPS) * gq_ref[...] * scale
        c_hi, c_mid, c_lo = _split3(cum[:, h:h + 1])
        qa = jnp.where(lane < HEAD_DIM, qn,
             jnp.where(lane < AUX_CUM0, 1.0,
             jnp.where(lane == AUX_CUM0, c_hi,
             jnp.where(lane == AUX_CUM0 + 1, c_mid,
             jnp.where(lane == AUX_CUM0 + 2, c_lo, 0.0)))))
        q_ref[:, sl] = qa.astype(bf16)

        kb = k_raw_t[h * HEAD_DIM:(h + 1) * HEAD_DIM, :]
        k_ms = jnp.sum(kb * kb, axis=0, keepdims=True) * (1.0 / HEAD_DIM)
        kn = kb * lax.rsqrt(k_ms + EPS) * gk_ref[...]
        t_hi, t_mid, t_lo = _split3(cum_t[h:h + 1, :])
        aux = jnp.where(sub == 0, -t_hi,
              jnp.where(sub == 1, -t_mid,
              jnp.where(sub == 2, -t_lo,
              jnp.where(sub < 6, 1.0, 0.0))))
        sl = slice(h * HEAD_PAD, (h + 1) * HEAD_PAD)
        ka = jnp.concatenate([kn, aux], axis=0).astype(bf16)
        for t in range(tm // ATT_K):
            kt_ref[t, sl, :] = ka[:, t * ATT_K:(t + 1) * ATT_K]


def _proj(x2d, seq, kv_norm, b_norm, wkt, wv, wf, bfv, wq, wg, gq, gk):
    T = x2d.shape[0]
    n_tiles = T // ROW_TILE
    qw = N_HEADS * HEAD_PAD
    return pl.pallas_call(
        functools.partial(_proj_kernel, seq // ROW_TILE),
        grid=(n_tiles,),
        in_specs=[
            pl.BlockSpec((ROW_TILE, D_MODEL), lambda i: (i, 0)),
            _const_spec((1, D_MODEL)),
            _const_spec((1, D_MODEL)),
            _const_spec((D_MODEL, D_MODEL), single=True),
            _const_spec((D_MODEL, D_MODEL), single=True),
            _const_spec((D_MODEL, LANES)),
            _const_spec((1, LANES)),
            _const_spec((D_MODEL, qw), single=True),
            _const_spec((D_MODEL, D_MODEL), single=True),
            _const_spec((1, HEAD_PAD)),
            _const_spec((HEAD_DIM, 1)),
        ],
        out_specs=[
            pl.BlockSpec((ROW_TILE, qw), lambda i: (i, 0)),
            pl.BlockSpec((ROW_TILE // ATT_K, qw, ATT_K), lambda i: (i, 0, 0)),
            pl.BlockSpec((ROW_TILE, D_MODEL), lambda i: (i, 0)),
            pl.BlockSpec((ROW_TILE, D_MODEL), lambda i: (i, 0)),
        ],
        out_shape=[
            jax.ShapeDtypeStruct((T, qw), bf16),
            jax.ShapeDtypeStruct((T // ATT_K, qw, ATT_K), bf16),
            jax.ShapeDtypeStruct((T, D_MODEL), bf16),
            jax.ShapeDtypeStruct((T, D_MODEL), bf16),
        ],
        scratch_shapes=[pltpu.VMEM((1, LANES), f32)],
        compiler_params=_params(("arbitrary",)),
        name="proj",
    )(x2d, kv_norm, b_norm, wkt, wv, wf, bfv, wq, wg, gq, gk)


def _attn_kernel(q_ref, kt_ref, v_ref, o_ref, s_scr, m_scr, l_scr, acc_scr):
    qi = pl.program_id(2)
    tiles = ATT_Q // ATT_K
    groups = ATT_K // LANES
    heads = [slice(hh * HEAD_PAD, (hh + 1) * HEAD_PAD) for hh in range(2)]
    m_scr[...] = jnp.full(m_scr.shape, -jnp.inf, f32)

    def lane_max(mx, s):
        for g in range(groups):
            mx = jnp.maximum(mx, s[:, g * LANES:(g + 1) * LANES])
        return mx

    def probs(s, mb):
        ps = [jnp.exp2(s[:, g * LANES:(g + 1) * LANES] - mb) for g in range(groups)]
        return ps, functools.reduce(lambda a, b: a + b, ps)

    def body_a(trip, c):
        for t in range(tiles):
            kt_idx = trip * tiles + t
            for hh in range(2):
                s = _dot(q_ref[:, heads[hh]], kt_ref[kt_idx, heads[hh], :])
                s_scr[hh, kt_idx] = s
                m_scr[hh] = lane_max(m_scr[hh], s)
        return c

    lax.fori_loop(0, qi, body_a, 0)
    for t in range(tiles):
        kt_idx = qi * tiles + t
        rows = ATT_Q - t * ATT_K
        visible = (lax.broadcasted_iota(jnp.int32, (rows, ATT_K), 1)
                   <= lax.broadcasted_iota(jnp.int32, (rows, ATT_K), 0))
        for hh in range(2):
            s = _dot(q_ref[t * ATT_K:, heads[hh]], kt_ref[kt_idx, heads[hh], :])
            s = jnp.where(visible, s, -jnp.inf)
            s_scr[hh, kt_idx, t * ATT_K:, :] = s
            m_scr[hh, t * ATT_K:, :] = lane_max(m_scr[hh, t * ATT_K:, :], s)

    for hh in range(2):
        row_max = jnp.max(m_scr[hh], axis=-1, keepdims=True)
        m_scr[hh] = jnp.broadcast_to(row_max, (ATT_Q, LANES))
    l_scr[...] = jnp.zeros(l_scr.shape, f32)
    acc_scr[...] = jnp.zeros(acc_scr.shape, f32)

    def body_b(trip, c):
        for hh in range(2):
            mb = m_scr[hh]
            lsum, acc = l_scr[hh], acc_scr[hh]
            for t0 in range(0, tiles, PV_TILES):
                kt_idx = trip * tiles + t0
                row0 = pl.multiple_of(kt_idx * ATT_K, PV_TILES * ATT_K)
                vb = v_ref[pl.ds(row0, PV_TILES * ATT_K), :]
                parts = []
                for t in range(PV_TILES):
                    ps, psum = probs(s_scr[hh, kt_idx + t], mb)
                    parts += ps
                    lsum = lsum + psum
                acc = acc + _dot(jnp.concatenate(parts, axis=1).astype(bf16), vb)
            l_scr[hh] = lsum
            acc_scr[hh] = acc
        return c

    lax.fori_loop(0, qi, body_b, 0)
    for t in range(tiles):
        kt_idx = qi * tiles + t
        row0 = pl.multiple_of(kt_idx * ATT_K, ATT_K)
        vb = v_ref[pl.ds(row0, ATT_K), :]
        for hh in range(2):
            ps, psum = probs(s_scr[hh, kt_idx, t * ATT_K:, :], m_scr[hh, t * ATT_K:, :])
            l_scr[hh, t * ATT_K:, :] += psum
            acc_scr[hh, t * ATT_K:, :] += _dot(jnp.concatenate(ps, axis=1).astype(bf16), vb)

    lane = lax.broadcasted_iota(jnp.int32, (ATT_Q, 2 * HEAD_DIM), 1)
    o0 = acc_scr[0] / jnp.sum(l_scr[0], axis=-1, keepdims=True)
    o1 = acc_scr[1] / jnp.sum(l_scr[1], axis=-1, keepdims=True)
    o_ref[...] = jnp.where(lane < HEAD_DIM, o0, o1).astype(bf16)


def _attention(q, kt, v, batch, seq):
    T = q.shape[0]
    nq = seq // ATT_Q
    nk = seq // ATT_K
    pairs = N_HEADS // 2
    return pl.pallas_call(
        _attn_kernel,
        grid=(batch, pairs, nq),
        in_specs=[
            pl.BlockSpec((ATT_Q, 2 * HEAD_PAD), lambda b, p, i: (b * nq + i, p)),
            pl.BlockSpec((nk, 2 * HEAD_PAD, ATT_K), lambda b, p, i: (b, p, 0)),
            pl.BlockSpec((seq, 2 * HEAD_DIM), lambda b, p, i: (b, p)),
        ],
        out_specs=pl.BlockSpec((ATT_Q, 2 * HEAD_DIM), lambda b, p, i: (b * nq + i, p)),
        out_shape=jax.ShapeDtypeStruct((T, D_MODEL), bf16),
        scratch_shapes=[
            pltpu.VMEM((2, nk, ATT_Q, ATT_K), f32),
            pltpu.VMEM((2, ATT_Q, LANES), f32),
            pltpu.VMEM((2, ATT_Q, LANES), f32),
            pltpu.VMEM((2, ATT_Q, 2 * HEAD_DIM), f32),
        ],
        compiler_params=_params(("parallel", "parallel", "parallel")),
        name="attn",
    )(q, kt, v)


def _attn_out_kernel(x_ref, o_ref, sg_ref, wo_ref, mn_ref, wr_ref, x3_ref, lg_ref):
    gated = (o_ref[...].astype(f32) * sg_ref[...].astype(f32)).astype(bf16)
    x3 = x_ref[...] + _dot(gated, wo_ref[...])
    x3_ref[...] = x3
    lg_ref[...] = _router_logits_t(x3, mn_ref[...], wr_ref[...])


def _attn_out(x2d, o, sg, wo, m_norm, w_router):
    T = x2d.shape[0]
    return pl.pallas_call(
        _attn_out_kernel,
        grid=(T // SGU_TILE,),
        in_specs=[
            pl.BlockSpec((SGU_TILE, D_MODEL), lambda i: (i, 0)),
            pl.BlockSpec((SGU_TILE, D_MODEL), lambda i: (i, 0)),
            pl.BlockSpec((SGU_TILE, D_MODEL), lambda i: (i, 0)),
            _const_spec((D_MODEL, D_MODEL)),
            _const_spec((1, D_MODEL)),
            _const_spec((2 * D_MODEL, 2 * LANES)),
        ],
        out_specs=[
            pl.BlockSpec((SGU_TILE, D_MODEL), lambda i: (i, 0)),
            pl.BlockSpec((ROUTER_ROWS, SGU_TILE), lambda i: (0, i)),
        ],
        out_shape=[
            jax.ShapeDtypeStruct((T, D_MODEL), f32),
            jax.ShapeDtypeStruct((ROUTER_ROWS, T), f32),
        ],
        compiler_params=_params(("parallel",)),
        name="attn_out",
    )(x2d, o, sg, wo, m_norm, w_router)


def _router_weight(w_group, w_expert):
    pad = jnp.zeros((D_MODEL, LANES - N_EXPERTS - N_GROUPS), f32)
    w = jnp.concatenate([w_expert, w_group, pad], axis=1)
    w_hi = w.astype(bf16)
    w_lo = (w - w_hi.astype(f32)).astype(bf16)
    top = jnp.concatenate([w_hi, w_lo], axis=1)
    bottom = jnp.concatenate([w_hi, jnp.zeros_like(w_hi)], axis=1)
    return jnp.concatenate([top, bottom], axis=0)


def kernel(x, a_norm, a_w_in, a_b_in, a_v_norm, a_w_s, a_b_s, a_w_out, kv_norm, kv_w, kv_b_f,
           k_norm, b_norm, b_w_qg, q_norm, b_w_out, m_norm, m_w_group, m_b_group, m_w_expert,
           m_b_expert, m_w1, m_w3, m_w2):
    batch, seq, _ = x.shape
    T = batch * seq
    x2d = x.reshape(T, D_MODEL)

    x1, lg0 = _sgu(
        x2d, a_norm[0].reshape(1, -1), a_w_in[0].astype(bf16), a_b_in[0].reshape(1, -1),
        a_v_norm[0].reshape(1, -1), a_w_s[0], a_b_s[0].T, a_w_out[0].astype(bf16),
        m_norm[0].reshape(1, -1), _router_weight(m_w_group[0], m_w_expert[0]))
    x2 = _moe(0, x1, lg0, m_norm[0].reshape(1, -1), m_b_group[0], m_b_expert[0],
              m_w1, m_w3, m_w2)

    wk = kv_w[:, :D_MODEL]
    wv = kv_w[:, D_MODEL:2 * D_MODEL]
    wf = jnp.pad(kv_w[:, 2 * D_MODEL:], ((0, 0), (0, LANES - N_HEADS)))
    bfv = jnp.pad(kv_b_f, (0, LANES - N_HEADS)).reshape(1, LANES)
    wq = b_w_qg[0][:, :D_MODEL]
    wg = b_w_qg[0][:, D_MODEL:]
    gq = jnp.pad(q_norm[0], (0, HEAD_PAD - HEAD_DIM)).reshape(1, HEAD_PAD)
    gk = k_norm.reshape(HEAD_DIM, 1)
    wq_pad = jnp.pad(wq.reshape(D_MODEL, N_HEADS, HEAD_DIM),
                     ((0, 0), (0, 0), (0, HEAD_PAD - HEAD_DIM))).reshape(D_MODEL, -1)
    q, kt, v, sg = _proj(
        x2, seq, kv_norm.reshape(1, -1), b_norm[0].reshape(1, -1),
        wk.T.astype(bf16), wv.astype(bf16), wf.astype(bf16), bfv,
        wq_pad.astype(bf16), wg.astype(bf16), gq, gk)
    o = _attention(q, kt, v, batch, seq)
    x3, lg1 = _attn_out(x2, o, sg, b_w_out[0].astype(bf16), m_norm[1].reshape(1, -1),
                        _router_weight(m_w_group[1], m_w_expert[1]))
    x4 = _moe(1, x3, lg1, m_norm[1].reshape(1, -1), m_b_group[1], m_b_expert[1],
              m_w1, m_w3, m_w2)
    return x4.reshape(batch, seq, D_MODEL)
```

```python
import functools
import math

import jax
import jax.numpy as jnp
import numpy as np
from jax import lax
from jax.experimental import pallas as pl
from jax.experimental.pallas import tpu as pltpu

D_MODEL = 1024
EPS = 1e-6
SGU_BLOCK = 128
SGU_CHUNK = 64
CHUNK_SHIFT = SGU_CHUNK.bit_length() - 1
SGU_WIDTH = 2 * D_MODEL
SGU_GROUPS = 8
SGU_GROUP_DIM = SGU_WIDTH // SGU_GROUPS
N_HEADS = 16
HEAD_DIM = D_MODEL // N_HEADS
N_GROUPS = 4
EXPERTS_PER_GROUP = 8
N_EXPERTS = N_GROUPS * EXPERTS_PER_GROUP
GROUP_SHIFT = EXPERTS_PER_GROUP.bit_length() - 1
TOP_K = 2
D_EXPERT = D_MODEL // 2

LANES = 128
ROW_SUBLANES = D_MODEL // 2 // LANES
ISSUE_UNROLL = 8
VMEM_LIMIT_BYTES = 56 * 1024 * 1024

ROW_TILE = 256
SGU_TILE = 512
ROUTE_TILE = 512
EXPERT_ROWS = 512
MOVE_TILE = 1024
MOVE_CHUNK = 64
ATT_Q = 1024
ATT_K = 256
PV_TILES = 2
HEAD_PAD = 2 * HEAD_DIM
ROUTER_ROWS = 40
AUX_ONE0 = HEAD_DIM
AUX_CUM0 = HEAD_DIM + 3

LOG2E = math.log2(math.e)

bf16 = jnp.bfloat16
f32 = jnp.float32
u32 = jnp.uint32
HI_HALF = np.uint32(0xFFFF0000)


def _dot(a, b, precision=None):
    return jnp.dot(a, b, preferred_element_type=f32, precision=precision)


def _dot_nt(a, b, precision=None):
    return lax.dot_general(a, b, (((1,), (1,)), ((), ())), preferred_element_type=f32,
                           precision=precision)


def _rms(x, g):
    ms = jnp.mean(x * x, axis=-1, keepdims=True)
    return x * lax.rsqrt(ms + EPS) * g


def _gelu_tanh(z):
    c = math.sqrt(2.0 / math.pi)
    return z * (0.5 * (1.0 + jnp.tanh(c * (z + 0.044715 * (z * z * z)))))


def _split3(c):
    hi = c.astype(bf16).astype(f32)
    r = c - hi
    mid = r.astype(bf16).astype(f32)
    lo = r - mid
    return hi, mid, lo


def _pack_rows(v):
    half = D_MODEL // 2
    lo = lax.bitcast_convert_type(v[:, :half].astype(bf16).astype(f32), u32) >> 16
    hi = lax.bitcast_convert_type(v[:, half:].astype(bf16).astype(f32), u32) & HI_HALF
    return (hi | lo).reshape(v.shape[0], ROW_SUBLANES, LANES)


def _unpack_rows(w):
    w = w.reshape(w.shape[0], D_MODEL // 2)
    lo = lax.bitcast_convert_type(w << 16, f32)
    hi = lax.bitcast_convert_type(w & HI_HALF, f32)
    return jnp.concatenate([lo, hi], axis=1)


def _const_spec(shape, single=False):
    nd = len(shape)
    mode = pl.Buffered(1) if single else None
    return pl.BlockSpec(shape, lambda *_: (0,) * nd, pipeline_mode=mode)


def _params(sem):
    return pltpu.CompilerParams(dimension_semantics=sem, vmem_limit_bytes=VMEM_LIMIT_BYTES)


def _router_logits_t(x_new, mn, wr3):
    hm = _rms(x_new, mn)
    h_hi = hm.astype(bf16)
    h_lo = (hm - h_hi.astype(f32)).astype(bf16)
    both = _dot(jnp.concatenate([h_hi, h_lo], axis=1), wr3)
    lg = both[:, :LANES] + both[:, LANES:]
    return lg.T[:ROUTER_ROWS, :]


def _sgu_kernel(x_ref, an_ref, win_ref, bin_ref, vn_ref, ws_ref, bst_ref, wout_ref, mn_ref,
                wr_ref, x1_ref, lg_ref, u_scr, v_scr, gated_scr):
    x = x_ref[...]
    h = _rms(x, an_ref[...]).astype(bf16)
    u_scr[...] = _gelu_tanh(_dot(h, win_ref[:, :SGU_WIDTH]) + bin_ref[:, :SGU_WIDTH])
    v = _gelu_tanh(_dot(h, win_ref[:, SGU_WIDTH:]) + bin_ref[:, SGU_WIDTH:])
    v_scr[...] = _rms(v, vn_ref[...]).astype(bf16)

    t_chunk = lax.broadcasted_iota(jnp.int32, (SGU_BLOCK, SGU_BLOCK), 0) >> CHUNK_SHIFT
    s_chunk = lax.broadcasted_iota(jnp.int32, (SGU_BLOCK, SGU_BLOCK), 1) >> CHUNK_SHIFT
    causal = t_chunk >= s_chunk
    for g in range(SGU_GROUPS):
        wsm = jnp.where(causal, ws_ref[g], 0.0).astype(bf16)
        cols = slice(g * SGU_GROUP_DIM, (g + 1) * SGU_GROUP_DIM)
        for sb in range(SGU_TILE // SGU_BLOCK):
            rows = slice(sb * SGU_BLOCK, (sb + 1) * SGU_BLOCK)
            mixed = _dot(wsm, v_scr[rows, cols]) + bst_ref[:, g:g + 1]
            gated_scr[rows, cols] = (u_scr[rows, cols] * mixed).astype(bf16)

    x1 = x + _dot(gated_scr[...], wout_ref[...])
    x1_ref[...] = x1
    lg_ref[...] = _router_logits_t(x1, mn_ref[...], wr_ref[...])


def _sgu(x2d, a_norm, w_in, b_in, v_norm, w_s, b_s_t, w_out, m_norm, w_router):
    T = x2d.shape[0]
    return pl.pallas_call(
        _sgu_kernel,
        grid=(T // SGU_TILE,),
        in_specs=[
            pl.BlockSpec((SGU_TILE, D_MODEL), lambda i: (i, 0)),
            _const_spec((1, D_MODEL)),
            _const_spec((D_MODEL, 2 * SGU_WIDTH), single=True),
            _const_spec((1, 2 * SGU_WIDTH)),
            _const_spec((1, SGU_WIDTH)),
            _const_spec((SGU_GROUPS, SGU_BLOCK, SGU_BLOCK)),
            _const_spec((SGU_BLOCK, SGU_GROUPS)),
            _const_spec((SGU_WIDTH, D_MODEL), single=True),
            _const_spec((1, D_MODEL)),
            _const_spec((2 * D_MODEL, 2 * LANES)),
        ],
        out_specs=[
            pl.BlockSpec((SGU_TILE, D_MODEL), lambda i: (i, 0)),
            pl.BlockSpec((ROUTER_ROWS, SGU_TILE), lambda i: (0, i)),
        ],
        out_shape=[
            jax.ShapeDtypeStruct((T, D_MODEL), f32),
            jax.ShapeDtypeStruct((ROUTER_ROWS, T), f32),
        ],
        scratch_shapes=[
            pltpu.VMEM((SGU_TILE, SGU_WIDTH), f32),
            pltpu.VMEM((SGU_TILE, SGU_WIDTH), bf16),
            pltpu.VMEM((SGU_TILE, SGU_WIDTH), bf16),
        ],
        compiler_params=_params(("parallel",)),
        name="sgu",
    )(x2d, a_norm, w_in, b_in, v_norm, w_s, b_s_t, w_out, m_norm, w_router)


def _route_kernel(lg_ref, be_ref, bg_ref, oi_ref, ow_ref, cnt_ref, carry_scr):
    step = pl.program_id(0)

    @pl.when(step == 0)
    def _():
        carry_scr[...] = jnp.zeros_like(carry_scr)

    lg = lg_ref[...]
    e_l = lg[0:N_EXPERTS, :] + be_ref[...]
    g_l = lg[N_EXPERTS:N_EXPERTS + 8, :] + bg_ref[...]
    g_row = lax.broadcasted_iota(jnp.int32, g_l.shape, 0).astype(f32)
    g_l = jnp.where(g_row < N_GROUPS, g_l, -jnp.inf)
    g_max = jnp.max(g_l, axis=0, keepdims=True)
    g_sel = jnp.min(jnp.where(g_l == g_max, g_row, 8.0), axis=0, keepdims=True)
    g_den = jnp.sum(jnp.exp(g_l - g_max), axis=0, keepdims=True)
    g_w = 1.0 / g_den

    e_row_i = lax.broadcasted_iota(jnp.int32, e_l.shape, 0)
    e_row = e_row_i.astype(f32)
    e_grp = (e_row_i >> GROUP_SHIFT).astype(f32)
    e_in = jnp.where(e_grp == g_sel, e_l, -jnp.inf)
    m1 = jnp.max(e_in, axis=0, keepdims=True)
    i1 = jnp.min(jnp.where(e_in == m1, e_row, float(N_EXPERTS)), axis=0, keepdims=True)
    e_in2 = jnp.where(e_row == i1, -jnp.inf, e_in)
    m2 = jnp.max(e_in2, axis=0, keepdims=True)
    i2 = jnp.min(jnp.where(e_in2 == m2, e_row, float(N_EXPERTS)), axis=0, keepdims=True)
    t = jnp.exp(m2 - m1)
    p1 = 1.0 / (1.0 + t)
    p2 = t / (1.0 + t)

    hit1 = e_row == i1
    hit2 = e_row == i2
    member = jnp.logical_or(hit1, hit2)
    tt = lg.shape[1]
    before = (lax.broadcasted_iota(jnp.int32, (tt, tt), 0)
              < lax.broadcasted_iota(jnp.int32, (tt, tt), 1))
    prefix = _dot(member.astype(bf16), before.astype(bf16))
    rank_full = prefix + carry_scr[...]
    rank1 = jnp.sum(jnp.where(hit1, rank_full, 0.0), axis=0, keepdims=True)
    rank2 = jnp.sum(jnp.where(hit2, rank_full, 0.0), axis=0, keepdims=True)
    carry_new = carry_scr[...] + jnp.sum(member.astype(f32), axis=1, keepdims=True)
    carry_scr[...] = carry_new

    oi_ref[...] = jnp.zeros(oi_ref.shape, jnp.int32)
    oi_ref[0:1, :] = i1.astype(jnp.int32)
    oi_ref[1:2, :] = i2.astype(jnp.int32)
    oi_ref[2:3, :] = rank1.astype(jnp.int32)
    oi_ref[3:4, :] = rank2.astype(jnp.int32)
    ow_ref[...] = jnp.zeros(ow_ref.shape, f32)
    ow_ref[0:1, :] = g_w * p1
    ow_ref[1:2, :] = g_w * p2
    cnt_ref[...] = jnp.broadcast_to(carry_new, cnt_ref.shape)


def _route(logits_t, b_expert, b_group):
    T = logits_t.shape[1]
    return pl.pallas_call(
        _route_kernel,
        grid=(T // ROUTE_TILE,),
        in_specs=[
            pl.BlockSpec((ROUTER_ROWS, ROUTE_TILE), lambda i: (0, i)),
            _const_spec((N_EXPERTS, 1)),
            _const_spec((8, 1)),
        ],
        out_specs=[
            pl.BlockSpec((8, ROUTE_TILE), lambda i: (0, i)),
            pl.BlockSpec((8, ROUTE_TILE), lambda i: (0, i)),
            _const_spec((N_EXPERTS, LANES)),
        ],
        out_shape=[
            jax.ShapeDtypeStruct((8, T), jnp.int32),
            jax.ShapeDtypeStruct((8, T), f32),
            jax.ShapeDtypeStruct((N_EXPERTS, LANES), f32),
        ],
        scratch_shapes=[pltpu.VMEM((N_EXPERTS, 1), f32)],
        compiler_params=_params(("arbitrary",)),
        name="route",
    )(logits_t, b_expert, b_group)


def _row_copy(src, s, dst, d, sem):
    return pltpu.make_async_copy(src.at[s], dst.at[d], sem)


def _wait_rows(buf, sem):
    pltpu.make_async_copy(buf, buf, sem).wait()


def _dispatch_kernel(n_blocks, pend_ref, padded_ref, nu_ref, dest_ref, x_ref, mn_ref, xs_hbm,
                     zero_scr, x3_scr, zsem, sem):
    @pl.when(pl.program_id(0) == 0)
    def _():
        zero_scr[...] = jnp.zeros_like(zero_scr)

        def fill(row0):
            return pltpu.make_async_copy(zero_scr, xs_hbm.at[pl.ds(row0, EXPERT_ROWS)], zsem)

        for e in range(N_EXPERTS):
            @pl.when(padded_ref[e] > 0)
            def _():
                fill(pend_ref[e] - EXPERT_ROWS).start()

            @pl.when(nu_ref[0] + e < n_blocks)
            def _():
                fill((nu_ref[0] + e) * EXPERT_ROWS).start()

        for e in range(N_EXPERTS):
            @pl.when(padded_ref[e] > 0)
            def _():
                fill(0).wait()

            @pl.when(nu_ref[0] + e < n_blocks)
            def _():
                fill(0).wait()

    i = pl.program_id(0)
    n_tiles = pl.num_programs(0) - 1

    def staged(rows):
        return _pack_rows(_rms(x_ref[rows, :], mn_ref[...]))

    @pl.when(i == 0)
    def _():
        x3_scr[0] = staged(slice(None))

    @pl.when(i > 0)
    def _():
        cur = (i - 1) % 3
        nxt = i % 3

        def chunk(c, carry):
            r0 = pl.multiple_of(c * MOVE_CHUNK, MOVE_CHUNK)
            for j in range(MOVE_CHUNK):
                for k in range(TOP_K):
                    _row_copy(x3_scr.at[cur], r0 + j, xs_hbm, dest_ref[k, r0 + j],
                              sem.at[cur]).start(priority=k)
            x3_scr[nxt, pl.ds(r0, MOVE_CHUNK)] = staged(pl.ds(r0, MOVE_CHUNK))
            return carry

        lax.fori_loop(0, MOVE_TILE // MOVE_CHUNK, chunk, 0)

        @pl.when(i > 1)
        def _():
            for k in range(TOP_K):
                _wait_rows(x3_scr.at[(i - 2) % 3], sem.at[(i - 2) % 3])

        @pl.when(i == n_tiles)
        def _():
            for k in range(TOP_K):
                _wait_rows(x3_scr.at[cur], sem.at[cur])


def _dispatch(pend, padded, n_used, dest_tiles, x2d, m_norm, n_rows):
    T = x2d.shape[0]
    n_blocks = n_rows // EXPERT_ROWS
    n_tiles = T // MOVE_TILE
    grid_spec = pltpu.PrefetchScalarGridSpec(
        num_scalar_prefetch=3,
        grid=(n_tiles + 1,),
        in_specs=[
            pl.BlockSpec((None, TOP_K, MOVE_TILE), lambda i, *_: (jnp.maximum(i - 1, 0), 0, 0),
                         memory_space=pltpu.SMEM),
            pl.BlockSpec((MOVE_TILE, D_MODEL), lambda i, *_: (jnp.minimum(i, n_tiles - 1), 0)),
            pl.BlockSpec((1, D_MODEL), lambda i, *_: (0, 0)),
        ],
        out_specs=pl.BlockSpec(memory_space=pl.ANY),
        scratch_shapes=[
            pltpu.VMEM((EXPERT_ROWS, ROW_SUBLANES, LANES), u32),
            pltpu.VMEM((3, MOVE_TILE, ROW_SUBLANES, LANES), u32),
            pltpu.SemaphoreType.DMA(()),
            pltpu.SemaphoreType.DMA((3,)),
        ],
    )
    return pl.pallas_call(
        functools.partial(_dispatch_kernel, n_blocks),
        grid_spec=grid_spec,
        out_shape=jax.ShapeDtypeStruct((n_rows, ROW_SUBLANES, LANES), u32),
        compiler_params=_params(("arbitrary",)),
        name="dispatch",
    )(pend, padded, n_used, dest_tiles, x2d, m_norm)


def _expert_kernel(n_blocks, ps_ref, nb_ref, nu_ref, xs_hbm, w1_ref, w3_ref, w2_ref,
                   ys_hbm, xbuf, ybuf, h_scr, w1_scr, w3_scr, w2_scr, xsem, ysem):
    e = pl.program_id(0)
    nblk = nb_ref[e]
    g0 = ps_ref[e]
    n_used = nu_ref[0]

    def rows(g):
        return pl.ds(pl.multiple_of(g * EXPERT_ROWS, EXPERT_ROWS), EXPERT_ROWS)

    def x_copy(g):
        return pltpu.make_async_copy(xs_hbm.at[rows(g)], xbuf.at[g % 2], xsem.at[g % 2])

    def y_copy(g):
        return pltpu.make_async_copy(ybuf.at[g % 2], ys_hbm.at[rows(g)], ysem.at[g % 2])

    def normed(slot):
        return _unpack_rows(xbuf[slot]).astype(bf16)

    @pl.when(e == 0)
    def _():
        x_copy(0).start(priority=1)
        x_copy(1).start(priority=1)
        x_copy(0).wait()
        h_scr[...] = normed(0)

    @pl.when(nblk > 0)
    def _():
        w1_scr[...] = w1_ref[...].astype(bf16)
        w3_scr[...] = w3_ref[...].astype(bf16)
        w2_scr[...] = w2_ref[...].astype(bf16)

        def body(g, h):
            @pl.when(g + 2 < n_used)
            def _():
                x_copy(g + 2).start(priority=1)

            @pl.when(g + 1 < n_used)
            def _():
                x_copy(g + 1).wait()

            @pl.when(g >= 2)
            def _():
                y_copy(g - 2).wait()

            h_next = normed((g + 1) % 2)
            a = _dot(h, w1_scr[...])
            c = _dot(h, w3_scr[...])
            hid = (a * jax.nn.sigmoid(a)) * c
            ybuf[g % 2] = _pack_rows(_dot(hid.astype(bf16), w2_scr[...]))
            y_copy(g).start(priority=1)
            return h_next

        h_scr[...] = lax.fori_loop(g0, g0 + nblk, body, h_scr[...])

    @pl.when(e == pl.num_programs(0) - 1)
    def _():
        y_copy(n_used - 2).wait()
        y_copy(n_used - 1).wait()
        ybuf[0] = jnp.zeros(ybuf.shape[1:], u32)

        def fill(b):
            r = pl.ds(pl.multiple_of(b * EXPERT_ROWS, EXPERT_ROWS), EXPERT_ROWS)
            return pltpu.make_async_copy(ybuf.at[0], ys_hbm.at[r], ysem.at[0])

        def start(b, c):
            fill(b).start()
            return c

        def wait(b, c):
            fill(b).wait()
            return c

        lax.fori_loop(nu_ref[0], n_blocks, start, 0)
        lax.fori_loop(nu_ref[0], n_blocks, wait, 0)


def _experts(layer, pstart, seg_blocks, n_used, xs, w1, w3, w2):
    n_rows = xs.shape[0]
    n_blocks = n_rows // EXPERT_ROWS

    def w_map(e, *_):
        return (layer, e, 0, 0)

    block_buf = pltpu.VMEM((2, EXPERT_ROWS, ROW_SUBLANES, LANES), u32)
    grid_spec = pltpu.PrefetchScalarGridSpec(
        num_scalar_prefetch=3,
        grid=(N_EXPERTS,),
        in_specs=[
            pl.BlockSpec(memory_space=pl.ANY),
            pl.BlockSpec((None, None, D_MODEL, D_EXPERT), w_map),
            pl.BlockSpec((None, None, D_MODEL, D_EXPERT), w_map),
            pl.BlockSpec((None, None, D_EXPERT, D_MODEL), w_map),
        ],
        out_specs=pl.BlockSpec(memory_space=pl.ANY),
        scratch_shapes=[
            block_buf,
            block_buf,
            pltpu.VMEM((EXPERT_ROWS, D_MODEL), bf16),
            pltpu.VMEM((D_MODEL, D_EXPERT), bf16),
            pltpu.VMEM((D_MODEL, D_EXPERT), bf16),
            pltpu.VMEM((D_EXPERT, D_MODEL), bf16),
            pltpu.SemaphoreType.DMA((2,)),
            pltpu.SemaphoreType.DMA((2,)),
        ],
    )
    return pl.pallas_call(
        functools.partial(_expert_kernel, n_blocks),
        grid_spec=grid_spec,
        out_shape=jax.ShapeDtypeStruct((n_rows, ROW_SUBLANES, LANES), u32),
        compiler_params=_params(("arbitrary",)),
        name="experts",
    )(pstart, seg_blocks, n_used, xs, w1, w3, w2)


def _combine_kernel(dest_ref, next_ref, x_ref, w_ref, ys_hbm, out_ref, buf, sem):
    i = pl.program_id(0)
    slot = i % 2

    def gather(idx_ref, s):
        def start(j, c):
            for k in range(TOP_K):
                _row_copy(ys_hbm, idx_ref[k, j], buf.at[s, k], j, sem.at[s]).start(priority=k)
            return c

        lax.fori_loop(0, MOVE_TILE, start, 0, unroll=ISSUE_UNROLL)

    @pl.when(i == 0)
    def _():
        gather(dest_ref, slot)

    @pl.when(i + 1 < pl.num_programs(0))
    def _():
        gather(next_ref, 1 - slot)

    for k in range(TOP_K):
        _wait_rows(buf.at[slot, k], sem.at[slot])
    w = w_ref[...]
    y0 = _unpack_rows(buf[slot, 0])
    y1 = _unpack_rows(buf[slot, 1])
    out_ref[...] = x_ref[...] + (w[:, 0:1] * y0 + w[:, 1:2] * y1)


def _combine(dest_tiles, x2d, w_cols, ys):
    T = x2d.shape[0]
    n_tiles = T // MOVE_TILE
    return pl.pallas_call(
        _combine_kernel,
        grid=(n_tiles,),
        in_specs=[
            pl.BlockSpec((None, TOP_K, MOVE_TILE), lambda i: (i, 0, 0),
                         memory_space=pltpu.SMEM),
            pl.BlockSpec((None, TOP_K, MOVE_TILE),
                         lambda i: (jnp.minimum(i + 1, n_tiles - 1), 0, 0),
                         memory_space=pltpu.SMEM),
            pl.BlockSpec((MOVE_TILE, D_MODEL), lambda i: (i, 0)),
            pl.BlockSpec((MOVE_TILE, 8), lambda i: (i, 0)),
            pl.BlockSpec(memory_space=pl.ANY),
        ],
        out_specs=pl.BlockSpec((MOVE_TILE, D_MODEL), lambda i: (i, 0)),
        out_shape=jax.ShapeDtypeStruct((T, D_MODEL), f32),
        scratch_shapes=[
            pltpu.VMEM((2, TOP_K, MOVE_TILE, ROW_SUBLANES, LANES), u32),
            pltpu.SemaphoreType.DMA((2,)),
        ],
        compiler_params=_params(("arbitrary",)),
        name="combine",
    )(dest_tiles, dest_tiles, x2d, w_cols, ys)


def _moe(layer, x2d, logits_t, m_norm, b_group, b_expert, w1, w3, w2):
    T = x2d.shape[0]
    n_blocks = (T * TOP_K) // EXPERT_ROWS + N_EXPERTS
    n_rows = n_blocks * EXPERT_ROWS

    be = b_expert.reshape(N_EXPERTS, 1)
    bg = jnp.concatenate([b_group, jnp.zeros((8 - N_GROUPS,), f32)]).reshape(8, 1)
    oi, ow, cnt = _route(logits_t, be, bg)

    counts = cnt[:, 0].astype(jnp.int32)
    padded = (counts + EXPERT_ROWS - 1) // EXPERT_ROWS * EXPERT_ROWS
    pend = jnp.cumsum(padded)
    pstart = pend - padded
    n_used = (pend[-1] // EXPERT_ROWS).astype(jnp.int32).reshape(1)
    e_ids = jnp.arange(N_EXPERTS, dtype=jnp.int32)[:, None, None]
    seg_start = jnp.sum(jnp.where(oi[None, 0:2] == e_ids, pstart[:, None, None], 0), axis=0)
    dest = seg_start + oi[2:4]
    dest_tiles = dest.reshape(TOP_K, T // MOVE_TILE, MOVE_TILE).transpose(1, 0, 2)
    w_cols = ow.T

    xs = _dispatch(pend.astype(jnp.int32), padded, n_used, dest_tiles, x2d, m_norm, n_rows)
    ys = _experts(layer, (pstart // EXPERT_ROWS).astype(jnp.int32), padded // EXPERT_ROWS,
                  n_used, xs, w1, w3, w2)
    return _combine(dest_tiles, x2d, w_cols, ys)


def _log_sigmoid(x):
    return jnp.minimum(x, 0.0) - jnp.log1p(jnp.exp(-jnp.abs(x)))


def _proj_kernel(tiles_per_seq, x_ref, kvn_ref, bn_ref, wkt_ref, wv_ref, wf_ref, bf_ref,
                 wq_ref, wg_ref, gq_ref, gk_ref, q_ref, kt_ref, v_ref, sg_ref, carry_scr):
    i = pl.program_id(0)
    x = x_ref[...]
    hkv = _rms(x, kvn_ref[...]).astype(bf16)
    hq = _rms(x, bn_ref[...]).astype(bf16)

    v_ref[...] = _dot(hkv, wv_ref[...]).astype(bf16)
    sg_ref[...] = jax.nn.sigmoid(_dot(hq, wg_ref[...])).astype(bf16)

    logf = _log_sigmoid(_dot(hkv, wf_ref[...]) + bf_ref[...])
    tm = x.shape[0]
    incl = (lax.broadcasted_iota(jnp.int32, (tm, tm), 0)
            >= lax.broadcasted_iota(jnp.int32, (tm, tm), 1)).astype(f32)
    @pl.when(i % tiles_per_seq == 0)
    def _():
        carry_scr[...] = jnp.zeros_like(carry_scr)

    cum = _dot(incl, logf, precision=lax.Precision.HIGHEST) + carry_scr[...]
    carry_scr[...] = cum[tm - 1:tm, :]
    cum = cum * LOG2E
    cum_t = cum.T

    lane = lax.broadcasted_iota(jnp.int32, (tm, HEAD_PAD), 1)
    sub = lax.broadcasted_iota(jnp.int32, (HEAD_DIM, tm), 0)
    q_raw = _dot(hq, wq_ref[...])
    k_raw_t = _dot_nt(wkt_ref[...], hkv)
    scale = HEAD_DIM ** -0.5 * LOG2E

    for h in range(N_HEADS):
        sl = slice(h * HEAD_PAD, (h + 1) * HEAD_PAD)
        qb = q_raw[:, sl]
        q_ms = jnp.sum(qb * qb, axis=-1, keepdims=True) * (1.0 / HEAD_DIM)
        qn = qb * lax.rsqrt(q_ms + EPS) * gq_ref[...] * scale
        c_hi, c_mid, c_lo = _split3(cum[:, h:h + 1])
        qa = jnp.where(lane < HEAD_DIM, qn,
             jnp.where(lane < AUX_CUM0, 1.0,
             jnp.where(lane == AUX_CUM0, c_hi,
             jnp.where(lane == AUX_CUM0 + 1, c_mid,
             jnp.where(lane == AUX_CUM0 + 2, c_lo, 0.0)))))
        q_ref[:, sl] = qa.astype(bf16)

        kb = k_raw_t[h * HEAD_DIM:(h + 1) * HEAD_DIM, :]
        k_ms = jnp.sum(kb * kb, axis=0, keepdims=True) * (1.0 / HEAD_DIM)
        kn = kb * lax.rsqrt(k_ms + EPS) * gk_ref[...]
        t_hi, t_mid, t_lo = _split3(cum_t[h:h + 1, :])
        aux = jnp.where(sub == 0, -t_hi,
              jnp.where(sub == 1, -t_mid,
              jnp.where(sub == 2, -t_lo,
              jnp.where(sub < 6, 1.0, 0.0))))
        sl = slice(h * HEAD_PAD, (h + 1) * HEAD_PAD)
        ka = jnp.concatenate([kn, aux], axis=0).astype(bf16)
        for t in range(tm // ATT_K):
            kt_ref[t, sl, :] = ka[:, t * ATT_K:(t + 1) * ATT_K]


def _proj(x2d, seq, kv_norm, b_norm, wkt, wv, wf, bfv, wq, wg, gq, gk):
    T = x2d.shape[0]
    n_tiles = T // ROW_TILE
    qw = N_HEADS * HEAD_PAD
    return pl.pallas_call(
        functools.partial(_proj_kernel, seq // ROW_TILE),
        grid=(n_tiles,),
        in_specs=[
            pl.BlockSpec((ROW_TILE, D_MODEL), lambda i: (i, 0)),
            _const_spec((1, D_MODEL)),
            _const_spec((1, D_MODEL)),
            _const_spec((D_MODEL, D_MODEL), single=True),
            _const_spec((D_MODEL, D_MODEL), single=True),
            _const_spec((D_MODEL, LANES)),
            _const_spec((1, LANES)),
            _const_spec((D_MODEL, qw), single=True),
            _const_spec((D_MODEL, D_MODEL), single=True),
            _const_spec((1, HEAD_PAD)),
            _const_spec((HEAD_DIM, 1)),
        ],
        out_specs=[
            pl.BlockSpec((ROW_TILE, qw), lambda i: (i, 0)),
            pl.BlockSpec((ROW_TILE // ATT_K, qw, ATT_K), lambda i: (i, 0, 0)),
            pl.BlockSpec((ROW_TILE, D_MODEL), lambda i: (i, 0)),
            pl.BlockSpec((ROW_TILE, D_MODEL), lambda i: (i, 0)),
        ],
        out_shape=[
            jax.ShapeDtypeStruct((T, qw), bf16),
            jax.ShapeDtypeStruct((T // ATT_K, qw, ATT_K), bf16),
            jax.ShapeDtypeStruct((T, D_MODEL), bf16),
            jax.ShapeDtypeStruct((T, D_MODEL), bf16),
        ],
        scratch_shapes=[pltpu.VMEM((1, LANES), f32)],
        compiler_params=_params(("arbitrary",)),
        name="proj",
    )(x2d, kv_norm, b_norm, wkt, wv, wf, bfv, wq, wg, gq, gk)


def _attn_kernel(q_ref, kt_ref, v_ref, o_ref, s_scr, m_scr, l_scr, acc_scr):
    qi = pl.program_id(2)
    tiles = ATT_Q // ATT_K
    groups = ATT_K // LANES
    heads = [slice(hh * HEAD_PAD, (hh + 1) * HEAD_PAD) for hh in range(2)]
    m_scr[...] = jnp.full(m_scr.shape, -jnp.inf, f32)

    def lane_max(mx, s):
        for g in range(groups):
            mx = jnp.maximum(mx, s[:, g * LANES:(g + 1) * LANES])
        return mx

    def probs(s, mb):
        ps = [jnp.exp2(s[:, g * LANES:(g + 1) * LANES] - mb) for g in range(groups)]
        return ps, functools.reduce(lambda a, b: a + b, ps)

    def body_a(trip, c):
        for t in range(tiles):
            kt_idx = trip * tiles + t
            for hh in range(2):
                s = _dot(q_ref[:, heads[hh]], kt_ref[kt_idx, heads[hh], :])
                s_scr[hh, kt_idx] = s
                m_scr[hh] = lane_max(m_scr[hh], s)
        return c

    lax.fori_loop(0, qi, body_a, 0)
    for t in range(tiles):
        kt_idx = qi * tiles + t
        rows = ATT_Q - t * ATT_K
        visible = (lax.broadcasted_iota(jnp.int32, (rows, ATT_K), 1)
                   <= lax.broadcasted_iota(jnp.int32, (rows, ATT_K), 0))
        for hh in range(2):
            s = _dot(q_ref[t * ATT_K:, heads[hh]], kt_ref[kt_idx, heads[hh], :])
            s = jnp.where(visible, s, -jnp.inf)
            s_scr[hh, kt_idx, t * ATT_K:, :] = s
            m_scr[hh, t * ATT_K:, :] = lane_max(m_scr[hh, t * ATT_K:, :], s)

    for hh in range(2):
        row_max = jnp.max(m_scr[hh], axis=-1, keepdims=True)
        m_scr[hh] = jnp.broadcast_to(row_max, (ATT_Q, LANES))
    l_scr[...] = jnp.zeros(l_scr.shape, f32)
    acc_scr[...] = jnp.zeros(acc_scr.shape, f32)

    def body_b(trip, c):
        for hh in range(2):
            mb = m_scr[hh]
            lsum, acc = l_scr[hh], acc_scr[hh]
            for t0 in range(0, tiles, PV_TILES):
                kt_idx = trip * tiles + t0
                row0 = pl.multiple_of(kt_idx * ATT_K, PV_TILES * ATT_K)
                vb = v_ref[pl.ds(row0, PV_TILES * ATT_K), :]
                parts = []
                for t in range(PV_TILES):
                    ps, psum = probs(s_scr[hh, kt_idx + t], mb)
                    parts += ps
                    lsum = lsum + psum
                acc = acc + _dot(jnp.concatenate(parts, axis=1).astype(bf16), vb)
            l_scr[hh] = lsum
            acc_scr[hh] = acc
        return c

    lax.fori_loop(0, qi, body_b, 0)
    for t in range(tiles):
        kt_idx = qi * tiles + t
        row0 = pl.multiple_of(kt_idx * ATT_K, ATT_K)
        vb = v_ref[pl.ds(row0, ATT_K), :]
        for hh in range(2):
            ps, psum = probs(s_scr[hh, kt_idx, t * ATT_K:, :], m_scr[hh, t * ATT_K:, :])
            l_scr[hh, t * ATT_K:, :] += psum
            acc_scr[hh, t * ATT_K:, :] += _dot(jnp.concatenate(ps, axis=1).astype(bf16), vb)

    lane = lax.broadcasted_iota(jnp.int32, (ATT_Q, 2 * HEAD_DIM), 1)
    o0 = acc_scr[0] / jnp.sum(l_scr[0], axis=-1, keepdims=True)
    o1 = acc_scr[1] / jnp.sum(l_scr[1], axis=-1, keepdims=True)
    o_ref[...] = jnp.where(lane < HEAD_DIM, o0, o1).astype(bf16)


def _attention(q, kt, v, batch, seq):
    T = q.shape[0]
    nq = seq // ATT_Q
    nk = seq // ATT_K
    pairs = N_HEADS // 2
    return pl.pallas_call(
        _attn_kernel,
        grid=(batch, pairs, nq),
        in_specs=[
            pl.BlockSpec((ATT_Q, 2 * HEAD_PAD), lambda b, p, i: (b * nq + i, p)),
            pl.BlockSpec((nk, 2 * HEAD_PAD, ATT_K), lambda b, p, i: (b, p, 0)),
            pl.BlockSpec((seq, 2 * HEAD_DIM), lambda b, p, i: (b, p)),
        ],
        out_specs=pl.BlockSpec((ATT_Q, 2 * HEAD_DIM), lambda b, p, i: (b * nq + i, p)),
        out_shape=jax.ShapeDtypeStruct((T, D_MODEL), bf16),
        scratch_shapes=[
            pltpu.VMEM((2, nk, ATT_Q, ATT_K), f32),
            pltpu.VMEM((2, ATT_Q, LANES), f32),
            pltpu.VMEM((2, ATT_Q, LANES), f32),
            pltpu.VMEM((2, ATT_Q, 2 * HEAD_DIM), f32),
        ],
        compiler_params=_params(("parallel", "parallel", "parallel")),
        name="attn",
    )(q, kt, v)


def _attn_out_kernel(x_ref, o_ref, sg_ref, wo_ref, mn_ref, wr_ref, x3_ref, lg_ref):
    gated = (o_ref[...].astype(f32) * sg_ref[...].astype(f32)).astype(bf16)
    x3 = x_ref[...] + _dot(gated, wo_ref[...])
    x3_ref[...] = x3
    lg_ref[...] = _router_logits_t(x3, mn_ref[...], wr_ref[...])


def _attn_out(x2d, o, sg, wo, m_norm, w_router):
    T = x2d.shape[0]
    return pl.pallas_call(
        _attn_out_kernel,
        grid=(T // SGU_TILE,),
        in_specs=[
            pl.BlockSpec((SGU_TILE, D_MODEL), lambda i: (i, 0)),
            pl.BlockSpec((SGU_TILE, D_MODEL), lambda i: (i, 0)),
            pl.BlockSpec((SGU_TILE, D_MODEL), lambda i: (i, 0)),
            _const_spec((D_MODEL, D_MODEL)),
            _const_spec((1, D_MODEL)),
            _const_spec((2 * D_MODEL, 2 * LANES)),
        ],
        out_specs=[
            pl.BlockSpec((SGU_TILE, D_MODEL), lambda i: (i, 0)),
            pl.BlockSpec((ROUTER_ROWS, SGU_TILE), lambda i: (0, i)),
        ],
        out_shape=[
            jax.ShapeDtypeStruct((T, D_MODEL), f32),
            jax.ShapeDtypeStruct((ROUTER_ROWS, T), f32),
        ],
        compiler_params=_params(("parallel",)),
        name="attn_out",
    )(x2d, o, sg, wo, m_norm, w_router)


def _router_weight(w_group, w_expert):
    pad = jnp.zeros((D_MODEL, LANES - N_EXPERTS - N_GROUPS), f32)
    w = jnp.concatenate([w_expert, w_group, pad], axis=1)
    w_hi = w.astype(bf16)
    w_lo = (w - w_hi.astype(f32)).astype(bf16)
    top = jnp.concatenate([w_hi, w_lo], axis=1)
    bottom = jnp.concatenate([w_hi, jnp.zeros_like(w_hi)], axis=1)
    return jnp.concatenate([top, bottom], axis=0)


def kernel(x, a_norm, a_w_in, a_b_in, a_v_norm, a_w_s, a_b_s, a_w_out, kv_norm, kv_w, kv_b_f,
           k_norm, b_norm, b_w_qg, q_norm, b_w_out, m_norm, m_w_group, m_b_group, m_w_expert,
           m_b_expert, m_w1, m_w3, m_w2):
    batch, seq, _ = x.shape
    T = batch * seq
    x2d = x.reshape(T, D_MODEL)

    x1, lg0 = _sgu(
        x2d, a_norm[0].reshape(1, -1), a_w_in[0].astype(bf16), a_b_in[0].reshape(1, -1),
        a_v_norm[0].reshape(1, -1), a_w_s[0], a_b_s[0].T, a_w_out[0].astype(bf16),
        m_norm[0].reshape(1, -1), _router_weight(m_w_group[0], m_w_expert[0]))
    x2 = _moe(0, x1, lg0, m_norm[0].reshape(1, -1), m_b_group[0], m_b_expert[0],
              m_w1, m_w3, m_w2)

    wk = kv_w[:, :D_MODEL]
    wv = kv_w[:, D_MODEL:2 * D_MODEL]
    wf = jnp.pad(kv_w[:, 2 * D_MODEL:], ((0, 0), (0, LANES - N_HEADS)))
    bfv = jnp.pad(kv_b_f, (0, LANES - N_HEADS)).reshape(1, LANES)
    wq = b_w_qg[0][:, :D_MODEL]
    wg = b_w_qg[0][:, D_MODEL:]
    gq = jnp.pad(q_norm[0], (0, HEAD_PAD - HEAD_DIM)).reshape(1, HEAD_PAD)
    gk = k_norm.reshape(HEAD_DIM, 1)
    wq_pad = jnp.pad(wq.reshape(D_MODEL, N_HEADS, HEAD_DIM),
                     ((0, 0), (0, 0), (0, HEAD_PAD - HEAD_DIM))).reshape(D_MODEL, -1)
    q, kt, v, sg = _proj(
        x2, seq, kv_norm.reshape(1, -1), b_norm[0].reshape(1, -1),
        wk.T.astype(bf16), wv.astype(bf16), wf.astype(bf16), bfv,
        wq_pad.astype(bf16), wg.astype(bf16), gq, gk)
    o = _attention(q, kt, v, batch, seq)
    x3, lg1 = _attn_out(x2, o, sg, b_w_out[0].astype(bf16), m_norm[1].reshape(1, -1),
                        _router_weight(m_w_group[1], m_w_expert[1]))
    x4 = _moe(1, x3, lg1, m_norm[1].reshape(1, -1), m_b_group[1], m_b_expert[1],
              m_w1, m_w3, m_w2)
    return x4.reshape(batch, seq, D_MODEL)
```

```python
import functools
import math

import jax
import jax.numpy as jnp
import numpy as np
from jax import lax
from jax.experimental import pallas as pl
from jax.experimental.pallas import tpu as pltpu

D_MODEL = 1024
EPS = 1e-6
SGU_BLOCK = 128
SGU_CHUNK = 64
CHUNK_SHIFT = SGU_CHUNK.bit_length() - 1
SGU_WIDTH = 2 * D_MODEL
SGU_GROUPS = 8
SGU_GROUP_DIM = SGU_WIDTH // SGU_GROUPS
N_HEADS = 16
HEAD_DIM = D_MODEL // N_HEADS
N_GROUPS = 4
EXPERTS_PER_GROUP = 8
N_EXPERTS = N_GROUPS * EXPERTS_PER_GROUP
GROUP_SHIFT = EXPERTS_PER_GROUP.bit_length() - 1
TOP_K = 2
D_EXPERT = D_MODEL // 2

LANES = 128
ROW_SUBLANES = D_MODEL // 2 // LANES
ISSUE_UNROLL = 8
VMEM_LIMIT_BYTES = 56 * 1024 * 1024

ROW_TILE = 256
SGU_TILE = 512
ROUTE_TILE = 512
EXPERT_ROWS = 512
MOVE_TILE = 512
MOVE_CHUNK = 64
ATT_Q = 1024
ATT_K = 256
PV_TILES = 2
HEAD_PAD = 2 * HEAD_DIM
ROUTER_ROWS = 40
AUX_ONE0 = HEAD_DIM
AUX_CUM0 = HEAD_DIM + 3

LOG2E = math.log2(math.e)

bf16 = jnp.bfloat16
f32 = jnp.float32
u32 = jnp.uint32
HI_HALF = np.uint32(0xFFFF0000)


def _dot(a, b, precision=None):
    return jnp.dot(a, b, preferred_element_type=f32, precision=precision)


def _dot_nt(a, b, precision=None):
    return lax.dot_general(a, b, (((1,), (1,)), ((), ())), preferred_element_type=f32,
                           precision=precision)


def _rms(x, g):
    ms = jnp.mean(x * x, axis=-1, keepdims=True)
    return x * lax.rsqrt(ms + EPS) * g


def _gelu_tanh(z):
    c = math.sqrt(2.0 / math.pi)
    return z * (0.5 * (1.0 + jnp.tanh(c * (z + 0.044715 * (z * z * z)))))


def _split3(c):
    hi = c.astype(bf16).astype(f32)
    r = c - hi
    mid = r.astype(bf16).astype(f32)
    lo = r - mid
    return hi, mid, lo


def _pack_rows(v):
    half = D_MODEL // 2
    lo = lax.bitcast_convert_type(v[:, :half].astype(bf16).astype(f32), u32) >> 16
    hi = lax.bitcast_convert_type(v[:, half:].astype(bf16).astype(f32), u32) & HI_HALF
    return (hi | lo).reshape(v.shape[0], ROW_SUBLANES, LANES)


def _unpack_rows(w):
    w = w.reshape(w.shape[0], D_MODEL // 2)
    lo = lax.bitcast_convert_type(w << 16, f32)
    hi = lax.bitcast_convert_type(w & HI_HALF, f32)
    return jnp.concatenate([lo, hi], axis=1)


def _const_spec(shape, single=False):
    nd = len(shape)
    mode = pl.Buffered(1) if single else None
    return pl.BlockSpec(shape, lambda *_: (0,) * nd, pipeline_mode=mode)


def _params(sem):
    return pltpu.CompilerParams(dimension_semantics=sem, vmem_limit_bytes=VMEM_LIMIT_BYTES)


def _router_logits_t(x_new, mn, wr3):
    hm = _rms(x_new, mn)
    h_hi = hm.astype(bf16)
    h_lo = (hm - h_hi.astype(f32)).astype(bf16)
    both = _dot(jnp.concatenate([h_hi, h_lo], axis=1), wr3)
    lg = both[:, :LANES] + both[:, LANES:]
    return lg.T[:ROUTER_ROWS, :]


def _sgu_kernel(x_ref, an_ref, win_ref, bin_ref, vn_ref, ws_ref, bst_ref, wout_ref, mn_ref,
                wr_ref, x1_ref, lg_ref, u_scr, v_scr, gated_scr):
    x = x_ref[...]
    h = _rms(x, an_ref[...]).astype(bf16)
    u_scr[...] = _gelu_tanh(_dot(h, win_ref[:, :SGU_WIDTH]) + bin_ref[:, :SGU_WIDTH])
    v = _gelu_tanh(_dot(h, win_ref[:, SGU_WIDTH:]) + bin_ref[:, SGU_WIDTH:])
    v_scr[...] = _rms(v, vn_ref[...]).astype(bf16)

    t_chunk = lax.broadcasted_iota(jnp.int32, (SGU_BLOCK, SGU_BLOCK), 0) >> CHUNK_SHIFT
    s_chunk = lax.broadcasted_iota(jnp.int32, (SGU_BLOCK, SGU_BLOCK), 1) >> CHUNK_SHIFT
    causal = t_chunk >= s_chunk
    for g in range(SGU_GROUPS):
        wsm = jnp.where(causal, ws_ref[g], 0.0).astype(bf16)
        cols = slice(g * SGU_GROUP_DIM, (g + 1) * SGU_GROUP_DIM)
        for sb in range(SGU_TILE // SGU_BLOCK):
            rows = slice(sb * SGU_BLOCK, (sb + 1) * SGU_BLOCK)
            mixed = _dot(wsm, v_scr[rows, cols]) + bst_ref[:, g:g + 1]
            gated_scr[rows, cols] = (u_scr[rows, cols] * mixed).astype(bf16)

    x1 = x + _dot(gated_scr[...], wout_ref[...])
    x1_ref[...] = x1
    lg_ref[...] = _router_logits_t(x1, mn_ref[...], wr_ref[...])


def _sgu(x2d, a_norm, w_in, b_in, v_norm, w_s, b_s_t, w_out, m_norm, w_router):
    T = x2d.shape[0]
    return pl.pallas_call(
        _sgu_kernel,
        grid=(T // SGU_TILE,),
        in_specs=[
            pl.BlockSpec((SGU_TILE, D_MODEL), lambda i: (i, 0)),
            _const_spec((1, D_MODEL)),
            _const_spec((D_MODEL, 2 * SGU_WIDTH), single=True),
            _const_spec((1, 2 * SGU_WIDTH)),
            _const_spec((1, SGU_WIDTH)),
            _const_spec((SGU_GROUPS, SGU_BLOCK, SGU_BLOCK)),
            _const_spec((SGU_BLOCK, SGU_GROUPS)),
            _const_spec((SGU_WIDTH, D_MODEL), single=True),
            _const_spec((1, D_MODEL)),
            _const_spec((2 * D_MODEL, 2 * LANES)),
        ],
        out_specs=[
            pl.BlockSpec((SGU_TILE, D_MODEL), lambda i: (i, 0)),
            pl.BlockSpec((ROUTER_ROWS, SGU_TILE), lambda i: (0, i)),
        ],
        out_shape=[
            jax.ShapeDtypeStruct((T, D_MODEL), f32),
            jax.ShapeDtypeStruct((ROUTER_ROWS, T), f32),
        ],
        scratch_shapes=[
            pltpu.VMEM((SGU_TILE, SGU_WIDTH), f32),
            pltpu.VMEM((SGU_TILE, SGU_WIDTH), bf16),
            pltpu.VMEM((SGU_TILE, SGU_WIDTH), bf16),
        ],
        compiler_params=_params(("parallel",)),
        name="sgu",
    )(x2d, a_norm, w_in, b_in, v_norm, w_s, b_s_t, w_out, m_norm, w_router)


def _route_kernel(lg_ref, be_ref, bg_ref, oi_ref, ow_ref, cnt_ref, carry_scr):
    step = pl.program_id(0)

    @pl.when(step == 0)
    def _():
        carry_scr[...] = jnp.zeros_like(carry_scr)

    lg = lg_ref[...]
    e_l = lg[0:N_EXPERTS, :] + be_ref[...]
    g_l = lg[N_EXPERTS:N_EXPERTS + 8, :] + bg_ref[...]
    g_row = lax.broadcasted_iota(jnp.int32, g_l.shape, 0).astype(f32)
    g_l = jnp.where(g_row < N_GROUPS, g_l, -jnp.inf)
    g_max = jnp.max(g_l, axis=0, keepdims=True)
    g_sel = jnp.min(jnp.where(g_l == g_max, g_row, 8.0), axis=0, keepdims=True)
    g_den = jnp.sum(jnp.exp(g_l - g_max), axis=0, keepdims=True)
    g_w = 1.0 / g_den

    e_row_i = lax.broadcasted_iota(jnp.int32, e_l.shape, 0)
    e_row = e_row_i.astype(f32)
    e_grp = (e_row_i >> GROUP_SHIFT).astype(f32)
    e_in = jnp.where(e_grp == g_sel, e_l, -jnp.inf)
    m1 = jnp.max(e_in, axis=0, keepdims=True)
    i1 = jnp.min(jnp.where(e_in == m1, e_row, float(N_EXPERTS)), axis=0, keepdims=True)
    e_in2 = jnp.where(e_row == i1, -jnp.inf, e_in)
    m2 = jnp.max(e_in2, axis=0, keepdims=True)
    i2 = jnp.min(jnp.where(e_in2 == m2, e_row, float(N_EXPERTS)), axis=0, keepdims=True)
    t = jnp.exp(m2 - m1)
    p1 = 1.0 / (1.0 + t)
    p2 = t / (1.0 + t)

    hit1 = e_row == i1
    hit2 = e_row == i2
    member = jnp.logical_or(hit1, hit2)
    tt = lg.shape[1]
    before = (lax.broadcasted_iota(jnp.int32, (tt, tt), 0)
              < lax.broadcasted_iota(jnp.int32, (tt, tt), 1))
    prefix = _dot(member.astype(bf16), before.astype(bf16))
    rank_full = prefix + carry_scr[...]
    rank1 = jnp.sum(jnp.where(hit1, rank_full, 0.0), axis=0, keepdims=True)
    rank2 = jnp.sum(jnp.where(hit2, rank_full, 0.0), axis=0, keepdims=True)
    carry_new = carry_scr[...] + jnp.sum(member.astype(f32), axis=1, keepdims=True)
    carry_scr[...] = carry_new

    oi_ref[...] = jnp.zeros(oi_ref.shape, jnp.int32)
    oi_ref[0:1, :] = i1.astype(jnp.int32)
    oi_ref[1:2, :] = i2.astype(jnp.int32)
    oi_ref[2:3, :] = rank1.astype(jnp.int32)
    oi_ref[3:4, :] = rank2.astype(jnp.int32)
    ow_ref[...] = jnp.zeros(ow_ref.shape, f32)
    ow_ref[0:1, :] = g_w * p1
    ow_ref[1:2, :] = g_w * p2
    cnt_ref[...] = jnp.broadcast_to(carry_new, cnt_ref.shape)


def _route(logits_t, b_expert, b_group):
    T = logits_t.shape[1]
    return pl.pallas_call(
        _route_kernel,
        grid=(T // ROUTE_TILE,),
        in_specs=[
            pl.BlockSpec((ROUTER_ROWS, ROUTE_TILE), lambda i: (0, i)),
            _const_spec((N_EXPERTS, 1)),
            _const_spec((8, 1)),
        ],
        out_specs=[
            pl.BlockSpec((8, ROUTE_TILE), lambda i: (0, i)),
            pl.BlockSpec((8, ROUTE_TILE), lambda i: (0, i)),
            _const_spec((N_EXPERTS, LANES)),
        ],
        out_shape=[
            jax.ShapeDtypeStruct((8, T), jnp.int32),
            jax.ShapeDtypeStruct((8, T), f32),
            jax.ShapeDtypeStruct((N_EXPERTS, LANES), f32),
        ],
        scratch_shapes=[pltpu.VMEM((N_EXPERTS, 1), f32)],
        compiler_params=_params(("arbitrary",)),
        name="route",
    )(logits_t, b_expert, b_group)


def _row_copy(src, s, dst, d, sem):
    return pltpu.make_async_copy(src.at[s], dst.at[d], sem)


def _wait_rows(buf, sem):
    pltpu.make_async_copy(buf, buf, sem).wait()


def _dispatch_kernel(n_blocks, pend_ref, padded_ref, nu_ref, dest_ref, x_ref, mn_ref, xs_hbm,
                     zero_scr, x3_scr, zsem, sem):
    @pl.when(pl.program_id(0) == 0)
    def _():
        zero_scr[...] = jnp.zeros_like(zero_scr)

        def fill(row0):
            return pltpu.make_async_copy(zero_scr, xs_hbm.at[pl.ds(row0, EXPERT_ROWS)], zsem)

        for e in range(N_EXPERTS):
            @pl.when(padded_ref[e] > 0)
            def _():
                fill(pend_ref[e] - EXPERT_ROWS).start()

            @pl.when(nu_ref[0] + e < n_blocks)
            def _():
                fill((nu_ref[0] + e) * EXPERT_ROWS).start()

        for e in range(N_EXPERTS):
            @pl.when(padded_ref[e] > 0)
            def _():
                fill(0).wait()

            @pl.when(nu_ref[0] + e < n_blocks)
            def _():
                fill(0).wait()

    i = pl.program_id(0)
    n_tiles = pl.num_programs(0) - 1

    def staged(rows):
        return _pack_rows(_rms(x_ref[rows, :], mn_ref[...]))

    @pl.when(i == 0)
    def _():
        x3_scr[0] = staged(slice(None))

    @pl.when(i > 0)
    def _():
        cur = (i - 1) % 3
        nxt = i % 3

        def chunk(c, carry):
            r0 = pl.multiple_of(c * MOVE_CHUNK, MOVE_CHUNK)
            for j in range(MOVE_CHUNK):
                for k in range(TOP_K):
                    _row_copy(x3_scr.at[cur], r0 + j, xs_hbm, dest_ref[k, r0 + j],
                              sem.at[cur]).start(priority=k)
            x3_scr[nxt, pl.ds(r0, MOVE_CHUNK)] = staged(pl.ds(r0, MOVE_CHUNK))
            return carry

        lax.fori_loop(0, MOVE_TILE // MOVE_CHUNK, chunk, 0)

        @pl.when(i > 1)
        def _():
            for k in range(TOP_K):
                _wait_rows(x3_scr.at[(i - 2) % 3], sem.at[(i - 2) % 3])

        @pl.when(i == n_tiles)
        def _():
            for k in range(TOP_K):
                _wait_rows(x3_scr.at[cur], sem.at[cur])


def _dispatch(pend, padded, n_used, dest_tiles, x2d, m_norm, n_rows):
    T = x2d.shape[0]
    n_blocks = n_rows // EXPERT_ROWS
    n_tiles = T // MOVE_TILE
    grid_spec = pltpu.PrefetchScalarGridSpec(
        num_scalar_prefetch=3,
        grid=(n_tiles + 1,),
        in_specs=[
            pl.BlockSpec((None, TOP_K, MOVE_TILE), lambda i, *_: (jnp.maximum(i - 1, 0), 0, 0),
                         memory_space=pltpu.SMEM),
            pl.BlockSpec((MOVE_TILE, D_MODEL), lambda i, *_: (jnp.minimum(i, n_tiles - 1), 0)),
            pl.BlockSpec((1, D_MODEL), lambda i, *_: (0, 0)),
        ],
        out_specs=pl.BlockSpec(memory_space=pl.ANY),
        scratch_shapes=[
            pltpu.VMEM((EXPERT_ROWS, ROW_SUBLANES, LANES), u32),
            pltpu.VMEM((3, MOVE_TILE, ROW_SUBLANES, LANES), u32),
            pltpu.SemaphoreType.DMA(()),
            pltpu.SemaphoreType.DMA((3,)),
        ],
    )
    return pl.pallas_call(
        functools.partial(_dispatch_kernel, n_blocks),
        grid_spec=grid_spec,
        out_shape=jax.ShapeDtypeStruct((n_rows, ROW_SUBLANES, LANES), u32),
        compiler_params=_params(("arbitrary",)),
        name="dispatch",
    )(pend, padded, n_used, dest_tiles, x2d, m_norm)


def _expert_kernel(n_blocks, ps_ref, nb_ref, nu_ref, xs_hbm, w1_ref, w3_ref, w2_ref,
                   ys_hbm, xbuf, ybuf, h_scr, w1_scr, w3_scr, w2_scr, xsem, ysem):
    e = pl.program_id(0)
    nblk = nb_ref[e]
    g0 = ps_ref[e]
    n_used = nu_ref[0]

    def rows(g):
        return pl.ds(pl.multiple_of(g * EXPERT_ROWS, EXPERT_ROWS), EXPERT_ROWS)

    def x_copy(g):
        return pltpu.make_async_copy(xs_hbm.at[rows(g)], xbuf.at[g % 2], xsem.at[g % 2])

    def y_copy(g):
        return pltpu.make_async_copy(ybuf.at[g % 2], ys_hbm.at[rows(g)], ysem.at[g % 2])

    def normed(slot):
        return _unpack_rows(xbuf[slot]).astype(bf16)

    @pl.when(e == 0)
    def _():
        x_copy(0).start(priority=1)
        x_copy(1).start(priority=1)
        x_copy(0).wait()
        h_scr[...] = normed(0)

    @pl.when(nblk > 0)
    def _():
        w1_scr[...] = w1_ref[...].astype(bf16)
        w3_scr[...] = w3_ref[...].astype(bf16)
        w2_scr[...] = w2_ref[...].astype(bf16)

        def body(g, h):
            @pl.when(g + 2 < n_used)
            def _():
                x_copy(g + 2).start(priority=1)

            @pl.when(g + 1 < n_used)
            def _():
                x_copy(g + 1).wait()

            @pl.when(g >= 2)
            def _():
                y_copy(g - 2).wait()

            h_next = normed((g + 1) % 2)
            a = _dot(h, w1_scr[...])
            c = _dot(h, w3_scr[...])
            hid = (a * jax.nn.sigmoid(a)) * c
            ybuf[g % 2] = _pack_rows(_dot(hid.astype(bf16), w2_scr[...]))
            y_copy(g).start(priority=1)
            return h_next

        h_scr[...] = lax.fori_loop(g0, g0 + nblk, body, h_scr[...])

    @pl.when(e == pl.num_programs(0) - 1)
    def _():
        y_copy(n_used - 2).wait()
        y_copy(n_used - 1).wait()
        ybuf[0] = jnp.zeros(ybuf.shape[1:], u32)

        def fill(b):
            r = pl.ds(pl.multiple_of(b * EXPERT_ROWS, EXPERT_ROWS), EXPERT_ROWS)
            return pltpu.make_async_copy(ybuf.at[0], ys_hbm.at[r], ysem.at[0])

        def start(b, c):
            fill(b).start()
            return c

        def wait(b, c):
            fill(b).wait()
            return c

        lax.fori_loop(nu_ref[0], n_blocks, start, 0)
        lax.fori_loop(nu_ref[0], n_blocks, wait, 0)


def _experts(layer, pstart, seg_blocks, n_used, xs, w1, w3, w2):
    n_rows = xs.shape[0]
    n_blocks = n_rows // EXPERT_ROWS

    def w_map(e, *_):
        return (layer, e, 0, 0)

    block_buf = pltpu.VMEM((2, EXPERT_ROWS, ROW_SUBLANES, LANES), u32)
    grid_spec = pltpu.PrefetchScalarGridSpec(
        num_scalar_prefetch=3,
        grid=(N_EXPERTS,),
        in_specs=[
            pl.BlockSpec(memory_space=pl.ANY),
            pl.BlockSpec((None, None, D_MODEL, D_EXPERT), w_map),
            pl.BlockSpec((None, None, D_MODEL, D_EXPERT), w_map),
            pl.BlockSpec((None, None, D_EXPERT, D_MODEL), w_map),
        ],
        out_specs=pl.BlockSpec(memory_space=pl.ANY),
        scratch_shapes=[
            block_buf,
            block_buf,
            pltpu.VMEM((EXPERT_ROWS, D_MODEL), bf16),
            pltpu.VMEM((D_MODEL, D_EXPERT), bf16),
            pltpu.VMEM((D_MODEL, D_EXPERT), bf16),
            pltpu.VMEM((D_EXPERT, D_MODEL), bf16),
            pltpu.SemaphoreType.DMA((2,)),
            pltpu.SemaphoreType.DMA((2,)),
        ],
    )
    return pl.pallas_call(
        functools.partial(_expert_kernel, n_blocks),
        grid_spec=grid_spec,
        out_shape=jax.ShapeDtypeStruct((n_rows, ROW_SUBLANES, LANES), u32),
        compiler_params=_params(("arbitrary",)),
        name="experts",
    )(pstart, seg_blocks, n_used, xs, w1, w3, w2)


def _combined_tile(dest_ref, next_ref, x_ref, w_ref, ys_hbm, buf, sem):
    i = pl.program_id(0)
    slot = i % 2
    n_tok = x_ref.shape[0]

    def gather(idx_ref, s):
        def start(j, c):
            for k in range(TOP_K):
                _row_copy(ys_hbm, idx_ref[k, j], buf.at[s, k], j, sem.at[s]).start(priority=k)
            return c

        lax.fori_loop(0, n_tok, start, 0, unroll=ISSUE_UNROLL)

    @pl.when(i == 0)
    def _():
        gather(dest_ref, slot)

    @pl.when(i + 1 < pl.num_programs(0))
    def _():
        gather(next_ref, 1 - slot)

    for k in range(TOP_K):
        _wait_rows(buf.at[slot, k], sem.at[slot])
    w = w_ref[...]
    y0 = _unpack_rows(buf[slot, 0])
    y1 = _unpack_rows(buf[slot, 1])
    return x_ref[...] + (w[:, 0:1] * y0 + w[:, 1:2] * y1)


def _combine_kernel(dest_ref, next_ref, x_ref, w_ref, ys_hbm, out_ref, buf, sem):
    out_ref[...] = _combined_tile(dest_ref, next_ref, x_ref, w_ref, ys_hbm, buf, sem)


def _combine_specs(n_tok, n_tiles):
    in_specs = [
        pl.BlockSpec((None, TOP_K, n_tok), lambda i: (i, 0, 0), memory_space=pltpu.SMEM),
        pl.BlockSpec((None, TOP_K, n_tok), lambda i: (jnp.minimum(i + 1, n_tiles - 1), 0, 0),
                     memory_space=pltpu.SMEM),
        pl.BlockSpec((n_tok, D_MODEL), lambda i: (i, 0)),
        pl.BlockSpec((n_tok, 8), lambda i: (i, 0)),
        pl.BlockSpec(memory_space=pl.ANY),
    ]
    scratch = [
        pltpu.VMEM((2, TOP_K, n_tok, ROW_SUBLANES, LANES), u32),
        pltpu.SemaphoreType.DMA((2,)),
    ]
    return in_specs, scratch


def _dest_tiles(dest, n_tok):
    return dest.reshape(TOP_K, -1, n_tok).transpose(1, 0, 2)


def _combine(dest, x2d, w_cols, ys):
    T = x2d.shape[0]
    n_tiles = T // MOVE_TILE
    in_specs, scratch = _combine_specs(MOVE_TILE, n_tiles)
    dest_tiles = _dest_tiles(dest, MOVE_TILE)
    return pl.pallas_call(
        _combine_kernel,
        grid=(n_tiles,),
        in_specs=in_specs,
        out_specs=pl.BlockSpec((MOVE_TILE, D_MODEL), lambda i: (i, 0)),
        out_shape=jax.ShapeDtypeStruct((T, D_MODEL), f32),
        scratch_shapes=scratch,
        compiler_params=_params(("arbitrary",)),
        name="combine",
    )(dest_tiles, dest_tiles, x2d, w_cols, ys)


def _moe_experts(layer, x2d, logits_t, m_norm, b_group, b_expert, w1, w3, w2):
    T = x2d.shape[0]
    n_blocks = (T * TOP_K) // EXPERT_ROWS + N_EXPERTS
    n_rows = n_blocks * EXPERT_ROWS

    be = b_expert.reshape(N_EXPERTS, 1)
    bg = jnp.concatenate([b_group, jnp.zeros((8 - N_GROUPS,), f32)]).reshape(8, 1)
    oi, ow, cnt = _route(logits_t, be, bg)

    counts = cnt[:, 0].astype(jnp.int32)
    padded = (counts + EXPERT_ROWS - 1) // EXPERT_ROWS * EXPERT_ROWS
    pend = jnp.cumsum(padded)
    pstart = pend - padded
    n_used = (pend[-1] // EXPERT_ROWS).astype(jnp.int32).reshape(1)
    e_ids = jnp.arange(N_EXPERTS, dtype=jnp.int32)[:, None, None]
    seg_start = jnp.sum(jnp.where(oi[None, 0:2] == e_ids, pstart[:, None, None], 0), axis=0)
    dest = seg_start + oi[2:4]
    w_cols = ow.T

    xs = _dispatch(pend.astype(jnp.int32), padded, n_used, _dest_tiles(dest, MOVE_TILE), x2d,
                   m_norm, n_rows)
    ys = _experts(layer, (pstart // EXPERT_ROWS).astype(jnp.int32), padded // EXPERT_ROWS,
                  n_used, xs, w1, w3, w2)
    return dest, w_cols, ys


def _log_sigmoid(x):
    return jnp.minimum(x, 0.0) - jnp.log1p(jnp.exp(-jnp.abs(x)))


def _proj_kernel(tiles_per_seq, dest_ref, next_ref, x1_ref, w_ref, ys_hbm, kvn_ref, bn_ref,
                 wkt_ref, wv_ref, wf_ref, bf_ref, wq_ref, wg_ref, gq_ref, gk_ref,
                 x_ref, q_ref, kt_ref, v_ref, sg_ref, carry_scr, gbuf, gsem):
    i = pl.program_id(0)
    x = _combined_tile(dest_ref, next_ref, x1_ref, w_ref, ys_hbm, gbuf, gsem)
    x_ref[...] = x
    hkv = _rms(x, kvn_ref[...]).astype(bf16)
    hq = _rms(x, bn_ref[...]).astype(bf16)

    v_ref[...] = _dot(hkv, wv_ref[...]).astype(bf16)
    sg_ref[...] = jax.nn.sigmoid(_dot(hq, wg_ref[...])).astype(bf16)

    logf = _log_sigmoid(_dot(hkv, wf_ref[...]) + bf_ref[...])
    tm = x.shape[0]
    incl = (lax.broadcasted_iota(jnp.int32, (tm, tm), 0)
            >= lax.broadcasted_iota(jnp.int32, (tm, tm), 1)).astype(f32)
    @pl.when(i % tiles_per_seq == 0)
    def _():
        carry_scr[...] = jnp.zeros_like(carry_scr)

    cum = _dot(incl, logf, precision=lax.Precision.HIGHEST) + carry_scr[...]
    carry_scr[...] = cum[tm - 1:tm, :]
    cum = cum * LOG2E
    cum_t = cum.T

    lane = lax.broadcasted_iota(jnp.int32, (tm, HEAD_PAD), 1)
    sub = lax.broadcasted_iota(jnp.int32, (HEAD_DIM, tm), 0)
    q_raw = _dot(hq, wq_ref[...])
    k_raw_t = _dot_nt(wkt_ref[...], hkv)
    scale = HEAD_DIM ** -0.5 * LOG2E

    for h in range(N_HEADS):
        sl = slice(h * HEAD_PAD, (h + 1) * HEAD_PAD)
        qb = q_raw[:, sl]
        q_ms = jnp.sum(qb * qb, axis=-1, keepdims=True) * (1.0 / HEAD_DIM)
        qn = qb * lax.rsqrt(q_ms + EPS) * gq_ref[...] * scale
        c_hi, c_mid, c_lo = _split3(cum[:, h:h + 1])
        qa = jnp.where(lane < HEAD_DIM, qn,
             jnp.where(lane < AUX_CUM0, 1.0,
             jnp.where(lane == AUX_CUM0, c_hi,
             jnp.where(lane == AUX_CUM0 + 1, c_mid,
             jnp.where(lane == AUX_CUM0 + 2, c_lo, 0.0)))))
        q_ref[:, sl] = qa.astype(bf16)

        kb = k_raw_t[h * HEAD_DIM:(h + 1) * HEAD_DIM, :]
        k_ms = jnp.sum(kb * kb, axis=0, keepdims=True) * (1.0 / HEAD_DIM)
        kn = kb * lax.rsqrt(k_ms + EPS) * gk_ref[...]
        t_hi, t_mid, t_lo = _split3(cum_t[h:h + 1, :])
        aux = jnp.where(sub == 0, -t_hi,
              jnp.where(sub == 1, -t_mid,
              jnp.where(sub == 2, -t_lo,
              jnp.where(sub < 6, 1.0, 0.0))))
        sl = slice(h * HEAD_PAD, (h + 1) * HEAD_PAD)
        ka = jnp.concatenate([kn, aux], axis=0).astype(bf16)
        for t in range(tm // ATT_K):
            kt_ref[t, sl, :] = ka[:, t * ATT_K:(t + 1) * ATT_K]


def _proj(x1, dest, w_cols, ys, seq, kv_norm, b_norm, wkt, wv, wf, bfv, wq, wg, gq, gk):
    T = x1.shape[0]
    n_tiles = T // ROW_TILE
    qw = N_HEADS * HEAD_PAD
    combine_specs, combine_scratch = _combine_specs(ROW_TILE, n_tiles)
    dest_tiles = _dest_tiles(dest, ROW_TILE)
    return pl.pallas_call(
        functools.partial(_proj_kernel, seq // ROW_TILE),
        grid=(n_tiles,),
        in_specs=combine_specs + [
            _const_spec((1, D_MODEL)),
            _const_spec((1, D_MODEL)),
            _const_spec((D_MODEL, D_MODEL), single=True),
            _const_spec((D_MODEL, D_MODEL), single=True),
            _const_spec((D_MODEL, LANES)),
            _const_spec((1, LANES)),
            _const_spec((D_MODEL, qw), single=True),
            _const_spec((D_MODEL, D_MODEL), single=True),
            _const_spec((1, HEAD_PAD)),
            _const_spec((HEAD_DIM, 1)),
        ],
        out_specs=[
            pl.BlockSpec((ROW_TILE, D_MODEL), lambda i: (i, 0)),
            pl.BlockSpec((ROW_TILE, qw), lambda i: (i, 0)),
            pl.BlockSpec((ROW_TILE // ATT_K, qw, ATT_K), lambda i: (i, 0, 0)),
            pl.BlockSpec((ROW_TILE, D_MODEL), lambda i: (i, 0)),
            pl.BlockSpec((ROW_TILE, D_MODEL), lambda i: (i, 0)),
        ],
        out_shape=[
            jax.ShapeDtypeStruct((T, D_MODEL), f32),
            jax.ShapeDtypeStruct((T, qw), bf16),
            jax.ShapeDtypeStruct((T // ATT_K, qw, ATT_K), bf16),
            jax.ShapeDtypeStruct((T, D_MODEL), bf16),
            jax.ShapeDtypeStruct((T, D_MODEL), bf16),
        ],
        scratch_shapes=[pltpu.VMEM((1, LANES), f32)] + combine_scratch,
        compiler_params=_params(("arbitrary",)),
        name="proj",
    )(dest_tiles, dest_tiles, x1, w_cols, ys, kv_norm, b_norm, wkt, wv, wf, bfv, wq, wg, gq, gk)


def _attn_kernel(q_ref, kt_ref, v_ref, o_ref, s_scr, m_scr, l_scr, acc_scr):
    qi = pl.program_id(2)
    tiles = ATT_Q // ATT_K
    groups = ATT_K // LANES
    heads = [slice(hh * HEAD_PAD, (hh + 1) * HEAD_PAD) for hh in range(2)]
    m_scr[...] = jnp.full(m_scr.shape, -jnp.inf, f32)

    def lane_max(mx, s):
        for g in range(groups):
            mx = jnp.maximum(mx, s[:, g * LANES:(g + 1) * LANES])
        return mx

    def probs(s, mb):
        ps = [jnp.exp2(s[:, g * LANES:(g + 1) * LANES] - mb) for g in range(groups)]
        return ps, functools.reduce(lambda a, b: a + b, ps)

    def body_a(trip, c):
        for t in range(tiles):
            kt_idx = trip * tiles + t
            for hh in range(2):
                s = _dot(q_ref[:, heads[hh]], kt_ref[kt_idx, heads[hh], :])
                s_scr[hh, kt_idx] = s
                m_scr[hh] = lane_max(m_scr[hh], s)
        return c

    lax.fori_loop(0, qi, body_a, 0)
    for t in range(tiles):
        kt_idx = qi * tiles + t
        rows = ATT_Q - t * ATT_K
        visible = (lax.broadcasted_iota(jnp.int32, (rows, ATT_K), 1)
                   <= lax.broadcasted_iota(jnp.int32, (rows, ATT_K), 0))
        for hh in range(2):
            s = _dot(q_ref[t * ATT_K:, heads[hh]], kt_ref[kt_idx, heads[hh], :])
            s = jnp.where(visible, s, -jnp.inf)
            s_scr[hh, kt_idx, t * ATT_K:, :] = s
            m_scr[hh, t * ATT_K:, :] = lane_max(m_scr[hh, t * ATT_K:, :], s)

    for hh in range(2):
        row_max = jnp.max(m_scr[hh], axis=-1, keepdims=True)
        m_scr[hh] = jnp.broadcast_to(row_max, (ATT_Q, LANES))
    l_scr[...] = jnp.zeros(l_scr.shape, f32)
    acc_scr[...] = jnp.zeros(acc_scr.shape, f32)

    def body_b(trip, c):
        for hh in range(2):
            mb = m_scr[hh]
            lsum, acc = l_scr[hh], acc_scr[hh]
            for t0 in range(0, tiles, PV_TILES):
                kt_idx = trip * tiles + t0
                row0 = pl.multiple_of(kt_idx * ATT_K, PV_TILES * ATT_K)
                vb = v_ref[pl.ds(row0, PV_TILES * ATT_K), :]
                parts = []
                for t in range(PV_TILES):
                    ps, psum = probs(s_scr[hh, kt_idx + t], mb)
                    parts += ps
                    lsum = lsum + psum
                acc = acc + _dot(jnp.concatenate(parts, axis=1).astype(bf16), vb)
            l_scr[hh] = lsum
            acc_scr[hh] = acc
        return c

    lax.fori_loop(0, qi, body_b, 0)
    for t in range(tiles):
        kt_idx = qi * tiles + t
        row0 = pl.multiple_of(kt_idx * ATT_K, ATT_K)
        vb = v_ref[pl.ds(row0, ATT_K), :]
        for hh in range(2):
            ps, psum = probs(s_scr[hh, kt_idx, t * ATT_K:, :], m_scr[hh, t * ATT_K:, :])
            l_scr[hh, t * ATT_K:, :] += psum
            acc_scr[hh, t * ATT_K:, :] += _dot(jnp.concatenate(ps, axis=1).astype(bf16), vb)

    lane = lax.broadcasted_iota(jnp.int32, (ATT_Q, 2 * HEAD_DIM), 1)
    o0 = acc_scr[0] / jnp.sum(l_scr[0], axis=-1, keepdims=True)
    o1 = acc_scr[1] / jnp.sum(l_scr[1], axis=-1, keepdims=True)
    o_ref[...] = jnp.where(lane < HEAD_DIM, o0, o1).astype(bf16)


def _attention(q, kt, v, batch, seq):
    T = q.shape[0]
    nq = seq // ATT_Q
    nk = seq // ATT_K
    pairs = N_HEADS // 2
    return pl.pallas_call(
        _attn_kernel,
        grid=(batch, pairs, nq),
        in_specs=[
            pl.BlockSpec((ATT_Q, 2 * HEAD_PAD), lambda b, p, i: (b * nq + i, p)),
            pl.BlockSpec((nk, 2 * HEAD_PAD, ATT_K), lambda b, p, i: (b, p, 0)),
            pl.BlockSpec((seq, 2 * HEAD_DIM), lambda b, p, i: (b, p)),
        ],
        out_specs=pl.BlockSpec((ATT_Q, 2 * HEAD_DIM), lambda b, p, i: (b * nq + i, p)),
        out_shape=jax.ShapeDtypeStruct((T, D_MODEL), bf16),
        scratch_shapes=[
            pltpu.VMEM((2, nk, ATT_Q, ATT_K), f32),
            pltpu.VMEM((2, ATT_Q, LANES), f32),
            pltpu.VMEM((2, ATT_Q, LANES), f32),
            pltpu.VMEM((2, ATT_Q, 2 * HEAD_DIM), f32),
        ],
        compiler_params=_params(("parallel", "parallel", "parallel")),
        name="attn",
    )(q, kt, v)


def _attn_out_kernel(x_ref, o_ref, sg_ref, wo_ref, mn_ref, wr_ref, x3_ref, lg_ref):
    gated = (o_ref[...].astype(f32) * sg_ref[...].astype(f32)).astype(bf16)
    x3 = x_ref[...] + _dot(gated, wo_ref[...])
    x3_ref[...] = x3
    lg_ref[...] = _router_logits_t(x3, mn_ref[...], wr_ref[...])


def _attn_out(x2d, o, sg, wo, m_norm, w_router):
    T = x2d.shape[0]
    return pl.pallas_call(
        _attn_out_kernel,
        grid=(T // SGU_TILE,),
        in_specs=[
            pl.BlockSpec((SGU_TILE, D_MODEL), lambda i: (i, 0)),
            pl.BlockSpec((SGU_TILE, D_MODEL), lambda i: (i, 0)),
            pl.BlockSpec((SGU_TILE, D_MODEL), lambda i: (i, 0)),
            _const_spec((D_MODEL, D_MODEL)),
            _const_spec((1, D_MODEL)),
            _const_spec((2 * D_MODEL, 2 * LANES)),
        ],
        out_specs=[
            pl.BlockSpec((SGU_TILE, D_MODEL), lambda i: (i, 0)),
            pl.BlockSpec((ROUTER_ROWS, SGU_TILE), lambda i: (0, i)),
        ],
        out_shape=[
            jax.ShapeDtypeStruct((T, D_MODEL), f32),
            jax.ShapeDtypeStruct((ROUTER_ROWS, T), f32),
        ],
        compiler_params=_params(("parallel",)),
        name="attn_out",
    )(x2d, o, sg, wo, m_norm, w_router)


def _router_weight(w_group, w_expert):
    pad = jnp.zeros((D_MODEL, LANES - N_EXPERTS - N_GROUPS), f32)
    w = jnp.concatenate([w_expert, w_group, pad], axis=1)
    w_hi = w.astype(bf16)
    w_lo = (w - w_hi.astype(f32)).astype(bf16)
    top = jnp.concatenate([w_hi, w_lo], axis=1)
    bottom = jnp.concatenate([w_hi, jnp.zeros_like(w_hi)], axis=1)
    return jnp.concatenate([top, bottom], axis=0)


def kernel(x, a_norm, a_w_in, a_b_in, a_v_norm, a_w_s, a_b_s, a_w_out, kv_norm, kv_w, kv_b_f,
           k_norm, b_norm, b_w_qg, q_norm, b_w_out, m_norm, m_w_group, m_b_group, m_w_expert,
           m_b_expert, m_w1, m_w3, m_w2):
    batch, seq, _ = x.shape
    T = batch * seq
    x2d = x.reshape(T, D_MODEL)

    x1, lg0 = _sgu(
        x2d, a_norm[0].reshape(1, -1), a_w_in[0].astype(bf16), a_b_in[0].reshape(1, -1),
        a_v_norm[0].reshape(1, -1), a_w_s[0], a_b_s[0].T, a_w_out[0].astype(bf16),
        m_norm[0].reshape(1, -1), _router_weight(m_w_group[0], m_w_expert[0]))
    dest0, w_cols0, ys0 = _moe_experts(0, x1, lg0, m_norm[0].reshape(1, -1), m_b_group[0],
                                       m_b_expert[0], m_w1, m_w3, m_w2)

    wk = kv_w[:, :D_MODEL]
    wv = kv_w[:, D_MODEL:2 * D_MODEL]
    wf = jnp.pad(kv_w[:, 2 * D_MODEL:], ((0, 0), (0, LANES - N_HEADS)))
    bfv = jnp.pad(kv_b_f, (0, LANES - N_HEADS)).reshape(1, LANES)
    wq = b_w_qg[0][:, :D_MODEL]
    wg = b_w_qg[0][:, D_MODEL:]
    gq = jnp.pad(q_norm[0], (0, HEAD_PAD - HEAD_DIM)).reshape(1, HEAD_PAD)
    gk = k_norm.reshape(HEAD_DIM, 1)
    wq_pad = jnp.pad(wq.reshape(D_MODEL, N_HEADS, HEAD_DIM),
                     ((0, 0), (0, 0), (0, HEAD_PAD - HEAD_DIM))).reshape(D_MODEL, -1)
    x2, q, kt, v, sg = _proj(
        x1, dest0, w_cols0, ys0, seq, kv_norm.reshape(1, -1), b_norm[0].reshape(1, -1),
        wk.T.astype(bf16), wv.astype(bf16), wf.astype(bf16), bfv,
        wq_pad.astype(bf16), wg.astype(bf16), gq, gk)
    o = _attention(q, kt, v, batch, seq)
    x3, lg1 = _attn_out(x2, o, sg, b_w_out[0].astype(bf16), m_norm[1].reshape(1, -1),
                        _router_weight(m_w_group[1], m_w_expert[1]))
    dest1, w_cols1, ys1 = _moe_experts(1, x3, lg1, m_norm[1].reshape(1, -1), m_b_group[1],
                                       m_b_expert[1], m_w1, m_w3, m_w2)
    x4 = _combine(dest1, x3, w_cols1, ys1)
    return x4.reshape(batch, seq, D_MODEL)
```

```python
import functools
import math

import jax
import jax.numpy as jnp
import numpy as np
from jax import lax
from jax.experimental import pallas as pl
from jax.experimental.pallas import tpu as pltpu

D_MODEL = 1024
EPS = 1e-6
SGU_BLOCK = 128
SGU_CHUNK = 64
CHUNK_SHIFT = SGU_CHUNK.bit_length() - 1
SGU_WIDTH = 2 * D_MODEL
SGU_GROUPS = 8
SGU_GROUP_DIM = SGU_WIDTH // SGU_GROUPS
N_HEADS = 16
HEAD_DIM = D_MODEL // N_HEADS
N_GROUPS = 4
EXPERTS_PER_GROUP = 8
N_EXPERTS = N_GROUPS * EXPERTS_PER_GROUP
GROUP_SHIFT = EXPERTS_PER_GROUP.bit_length() - 1
TOP_K = 2
D_EXPERT = D_MODEL // 2

LANES = 128
ROW_SUBLANES = D_MODEL // 2 // LANES
ISSUE_UNROLL = 8
VMEM_LIMIT_BYTES = 56 * 1024 * 1024

ROW_TILE = 256
SGU_TILE = 512
ROUTE_TILE = 1024
EXPERT_ROWS = 512
MOVE_TILE = 512
MOVE_CHUNK = 64
ATT_Q = 1024
ATT_K = 256
PV_TILES = 2
HEAD_PAD = 2 * HEAD_DIM
ROUTER_ROWS = 40
AUX_ONE0 = HEAD_DIM
AUX_CUM0 = HEAD_DIM + 3

LOG2E = math.log2(math.e)

bf16 = jnp.bfloat16
f32 = jnp.float32
u32 = jnp.uint32
HI_HALF = np.uint32(0xFFFF0000)


def _dot(a, b, precision=None):
    return jnp.dot(a, b, preferred_element_type=f32, precision=precision)


def _dot_nt(a, b, precision=None):
    return lax.dot_general(a, b, (((1,), (1,)), ((), ())), preferred_element_type=f32,
                           precision=precision)


def _rms(x, g):
    ms = jnp.mean(x * x, axis=-1, keepdims=True)
    return x * lax.rsqrt(ms + EPS) * g


def _gelu_tanh(z):
    c = math.sqrt(2.0 / math.pi)
    return z * (0.5 * (1.0 + jnp.tanh(c * (z + 0.044715 * (z * z * z)))))


def _split3(c):
    hi = c.astype(bf16).astype(f32)
    r = c - hi
    mid = r.astype(bf16).astype(f32)
    lo = r - mid
    return hi, mid, lo


def _pack_rows(v):
    half = D_MODEL // 2
    lo = lax.bitcast_convert_type(v[:, :half].astype(bf16).astype(f32), u32) >> 16
    hi = lax.bitcast_convert_type(v[:, half:].astype(bf16).astype(f32), u32) & HI_HALF
    return (hi | lo).reshape(v.shape[0], ROW_SUBLANES, LANES)


def _unpack_rows(w):
    w = w.reshape(w.shape[0], D_MODEL // 2)
    lo = lax.bitcast_convert_type(w << 16, f32)
    hi = lax.bitcast_convert_type(w & HI_HALF, f32)
    return jnp.concatenate([lo, hi], axis=1)


def _const_spec(shape, single=False):
    nd = len(shape)
    mode = pl.Buffered(1) if single else None
    return pl.BlockSpec(shape, lambda *_: (0,) * nd, pipeline_mode=mode)


def _params(sem):
    return pltpu.CompilerParams(dimension_semantics=sem, vmem_limit_bytes=VMEM_LIMIT_BYTES)


def _router_logits_t(x_new, mn, wr3):
    hm = _rms(x_new, mn)
    h_hi = hm.astype(bf16)
    h_lo = (hm - h_hi.astype(f32)).astype(bf16)
    both = _dot(jnp.concatenate([h_hi, h_lo], axis=1), wr3)
    lg = both[:, :LANES] + both[:, LANES:]
    return lg.T[:ROUTER_ROWS, :]


def _sgu_kernel(x_ref, an_ref, win_ref, bin_ref, vn_ref, ws_ref, bst_ref, wout_ref, mn_ref,
                wr_ref, x1_ref, lg_ref, u_scr, v_scr, gated_scr):
    x = x_ref[...]
    h = _rms(x, an_ref[...]).astype(bf16)
    u_scr[...] = _gelu_tanh(_dot(h, win_ref[:, :SGU_WIDTH]) + bin_ref[:, :SGU_WIDTH])
    v = _gelu_tanh(_dot(h, win_ref[:, SGU_WIDTH:]) + bin_ref[:, SGU_WIDTH:])
    v_scr[...] = _rms(v, vn_ref[...]).astype(bf16)

    t_chunk = lax.broadcasted_iota(jnp.int32, (SGU_BLOCK, SGU_BLOCK), 0) >> CHUNK_SHIFT
    s_chunk = lax.broadcasted_iota(jnp.int32, (SGU_BLOCK, SGU_BLOCK), 1) >> CHUNK_SHIFT
    causal = t_chunk >= s_chunk
    for g in range(SGU_GROUPS):
        wsm = jnp.where(causal, ws_ref[g], 0.0).astype(bf16)
        cols = slice(g * SGU_GROUP_DIM, (g + 1) * SGU_GROUP_DIM)
        for sb in range(SGU_TILE // SGU_BLOCK):
            rows = slice(sb * SGU_BLOCK, (sb + 1) * SGU_BLOCK)
            mixed = _dot(wsm, v_scr[rows, cols]) + bst_ref[:, g:g + 1]
            gated_scr[rows, cols] = (u_scr[rows, cols] * mixed).astype(bf16)

    x1 = x + _dot(gated_scr[...], wout_ref[...])
    x1_ref[...] = x1
    lg_ref[...] = _router_logits_t(x1, mn_ref[...], wr_ref[...])


def _sgu(x2d, a_norm, w_in, b_in, v_norm, w_s, b_s_t, w_out, m_norm, w_router):
    T = x2d.shape[0]
    return pl.pallas_call(
        _sgu_kernel,
        grid=(T // SGU_TILE,),
        in_specs=[
            pl.BlockSpec((SGU_TILE, D_MODEL), lambda i: (i, 0)),
            _const_spec((1, D_MODEL)),
            _const_spec((D_MODEL, 2 * SGU_WIDTH), single=True),
            _const_spec((1, 2 * SGU_WIDTH)),
            _const_spec((1, SGU_WIDTH)),
            _const_spec((SGU_GROUPS, SGU_BLOCK, SGU_BLOCK)),
            _const_spec((SGU_BLOCK, SGU_GROUPS)),
            _const_spec((SGU_WIDTH, D_MODEL), single=True),
            _const_spec((1, D_MODEL)),
            _const_spec((2 * D_MODEL, 2 * LANES)),
        ],
        out_specs=[
            pl.BlockSpec((SGU_TILE, D_MODEL), lambda i: (i, 0)),
            pl.BlockSpec((ROUTER_ROWS, SGU_TILE), lambda i: (0, i)),
        ],
        out_shape=[
            jax.ShapeDtypeStruct((T, D_MODEL), f32),
            jax.ShapeDtypeStruct((ROUTER_ROWS, T), f32),
        ],
        scratch_shapes=[
            pltpu.VMEM((SGU_TILE, SGU_WIDTH), f32),
            pltpu.VMEM((SGU_TILE, SGU_WIDTH), bf16),
            pltpu.VMEM((SGU_TILE, SGU_WIDTH), bf16),
        ],
        compiler_params=_params(("parallel",)),
        name="sgu",
    )(x2d, a_norm, w_in, b_in, v_norm, w_s, b_s_t, w_out, m_norm, w_router)


def _route_kernel(lg_ref, be_ref, bg_ref, oi_ref, ow_ref, cnt_ref, carry_scr):
    step = pl.program_id(0)

    @pl.when(step == 0)
    def _():
        carry_scr[...] = jnp.zeros_like(carry_scr)

    lg = lg_ref[...]
    e_l = lg[0:N_EXPERTS, :] + be_ref[...]
    g_l = lg[N_EXPERTS:N_EXPERTS + 8, :] + bg_ref[...]
    g_row = lax.broadcasted_iota(jnp.int32, g_l.shape, 0).astype(f32)
    g_l = jnp.where(g_row < N_GROUPS, g_l, -jnp.inf)
    g_max = jnp.max(g_l, axis=0, keepdims=True)
    g_sel = jnp.min(jnp.where(g_l == g_max, g_row, 8.0), axis=0, keepdims=True)
    g_den = jnp.sum(jnp.exp(g_l - g_max), axis=0, keepdims=True)
    g_w = 1.0 / g_den

    e_row_i = lax.broadcasted_iota(jnp.int32, e_l.shape, 0)
    e_row = e_row_i.astype(f32)
    e_grp = (e_row_i >> GROUP_SHIFT).astype(f32)
    e_in = jnp.where(e_grp == g_sel, e_l, -jnp.inf)
    m1 = jnp.max(e_in, axis=0, keepdims=True)
    i1 = jnp.min(jnp.where(e_in == m1, e_row, float(N_EXPERTS)), axis=0, keepdims=True)
    e_in2 = jnp.where(e_row == i1, -jnp.inf, e_in)
    m2 = jnp.max(e_in2, axis=0, keepdims=True)
    i2 = jnp.min(jnp.where(e_in2 == m2, e_row, float(N_EXPERTS)), axis=0, keepdims=True)
    t = jnp.exp(m2 - m1)
    p1 = 1.0 / (1.0 + t)
    p2 = t / (1.0 + t)

    hit1 = e_row == i1
    hit2 = e_row == i2
    member = jnp.logical_or(hit1, hit2)
    tt = lg.shape[1]
    before = (lax.broadcasted_iota(jnp.int32, (tt, tt), 0)
              < lax.broadcasted_iota(jnp.int32, (tt, tt), 1))
    prefix = _dot(member.astype(bf16), before.astype(bf16))
    rank_full = prefix + carry_scr[...]
    rank1 = jnp.sum(jnp.where(hit1, rank_full, 0.0), axis=0, keepdims=True)
    rank2 = jnp.sum(jnp.where(hit2, rank_full, 0.0), axis=0, keepdims=True)
    carry_new = carry_scr[...] + jnp.sum(member.astype(f32), axis=1, keepdims=True)
    carry_scr[...] = carry_new

    oi_ref[...] = jnp.zeros(oi_ref.shape, jnp.int32)
    oi_ref[0:1, :] = i1.astype(jnp.int32)
    oi_ref[1:2, :] = i2.astype(jnp.int32)
    oi_ref[2:3, :] = rank1.astype(jnp.int32)
    oi_ref[3:4, :] = rank2.astype(jnp.int32)
    ow_ref[...] = jnp.zeros(ow_ref.shape, f32)
    ow_ref[0:1, :] = g_w * p1
    ow_ref[1:2, :] = g_w * p2
    cnt_ref[...] = jnp.broadcast_to(carry_new, cnt_ref.shape)


def _route(logits_t, b_expert, b_group):
    T = logits_t.shape[1]
    return pl.pallas_call(
        _route_kernel,
        grid=(T // ROUTE_TILE,),
        in_specs=[
            pl.BlockSpec((ROUTER_ROWS, ROUTE_TILE), lambda i: (0, i)),
            _const_spec((N_EXPERTS, 1)),
            _const_spec((8, 1)),
        ],
        out_specs=[
            pl.BlockSpec((8, ROUTE_TILE), lambda i: (0, i)),
            pl.BlockSpec((8, ROUTE_TILE), lambda i: (0, i)),
            _const_spec((N_EXPERTS, LANES)),
        ],
        out_shape=[
            jax.ShapeDtypeStruct((8, T), jnp.int32),
            jax.ShapeDtypeStruct((8, T), f32),
            jax.ShapeDtypeStruct((N_EXPERTS, LANES), f32),
        ],
        scratch_shapes=[pltpu.VMEM((N_EXPERTS, 1), f32)],
        compiler_params=_params(("arbitrary",)),
        name="route",
    )(logits_t, b_expert, b_group)


def _row_copy(src, s, dst, d, sem):
    return pltpu.make_async_copy(src.at[s], dst.at[d], sem)


def _wait_rows(buf, sem):
    pltpu.make_async_copy(buf, buf, sem).wait()


def _dispatch_kernel(n_blocks, pend_ref, padded_ref, nu_ref, dest_ref, x_ref, mn_ref, xs_hbm,
                     zero_scr, x3_scr, zsem, sem):
    @pl.when(pl.program_id(0) == 0)
    def _():
        zero_scr[...] = jnp.zeros_like(zero_scr)

        def fill(row0):
            return pltpu.make_async_copy(zero_scr, xs_hbm.at[pl.ds(row0, EXPERT_ROWS)], zsem)

        for e in range(N_EXPERTS):
            @pl.when(padded_ref[e] > 0)
            def _():
                fill(pend_ref[e] - EXPERT_ROWS).start()

            @pl.when(nu_ref[0] + e < n_blocks)
            def _():
                fill((nu_ref[0] + e) * EXPERT_ROWS).start()

        for e in range(N_EXPERTS):
            @pl.when(padded_ref[e] > 0)
            def _():
                fill(0).wait()

            @pl.when(nu_ref[0] + e < n_blocks)
            def _():
                fill(0).wait()

    i = pl.program_id(0)
    n_tiles = pl.num_programs(0) - 1

    def staged(rows):
        return _pack_rows(_rms(x_ref[rows, :], mn_ref[...]))

    @pl.when(i == 0)
    def _():
        x3_scr[0] = staged(slice(None))

    @pl.when(i > 0)
    def _():
        cur = (i - 1) % 3
        nxt = i % 3

        def chunk(c, carry):
            r0 = pl.multiple_of(c * MOVE_CHUNK, MOVE_CHUNK)
            for j in range(MOVE_CHUNK):
                for k in range(TOP_K):
                    _row_copy(x3_scr.at[cur], r0 + j, xs_hbm,
                              dest_ref[0, k * MOVE_TILE + r0 + j], sem.at[cur]).start(priority=k)
            x3_scr[nxt, pl.ds(r0, MOVE_CHUNK)] = staged(pl.ds(r0, MOVE_CHUNK))
            return carry

        lax.fori_loop(0, MOVE_TILE // MOVE_CHUNK, chunk, 0)

        @pl.when(i > 1)
        def _():
            for k in range(TOP_K):
                _wait_rows(x3_scr.at[(i - 2) % 3], sem.at[(i - 2) % 3])

        @pl.when(i == n_tiles)
        def _():
            for k in range(TOP_K):
                _wait_rows(x3_scr.at[cur], sem.at[cur])


def _dispatch(pend, padded, n_used, dest_tiles, x2d, m_norm, n_rows):
    T = x2d.shape[0]
    n_blocks = n_rows // EXPERT_ROWS
    n_tiles = T // MOVE_TILE
    grid_spec = pltpu.PrefetchScalarGridSpec(
        num_scalar_prefetch=3,
        grid=(n_tiles + 1,),
        in_specs=[
            pl.BlockSpec((None, 1, TOP_K * MOVE_TILE),
                         lambda i, *_: (jnp.maximum(i - 1, 0), 0, 0), memory_space=pltpu.SMEM),
            pl.BlockSpec((MOVE_TILE, D_MODEL), lambda i, *_: (jnp.minimum(i, n_tiles - 1), 0)),
            pl.BlockSpec((1, D_MODEL), lambda i, *_: (0, 0)),
        ],
        out_specs=pl.BlockSpec(memory_space=pl.ANY),
        scratch_shapes=[
            pltpu.VMEM((EXPERT_ROWS, ROW_SUBLANES, LANES), u32),
            pltpu.VMEM((3, MOVE_TILE, ROW_SUBLANES, LANES), u32),
            pltpu.SemaphoreType.DMA(()),
            pltpu.SemaphoreType.DMA((3,)),
        ],
    )
    return pl.pallas_call(
        functools.partial(_dispatch_kernel, n_blocks),
        grid_spec=grid_spec,
        out_shape=jax.ShapeDtypeStruct((n_rows, ROW_SUBLANES, LANES), u32),
        compiler_params=_params(("arbitrary",)),
        name="dispatch",
    )(pend, padded, n_used, dest_tiles, x2d, m_norm)


def _expert_kernel(n_blocks, ps_ref, nb_ref, nu_ref, xs_hbm, w1_ref, w3_ref, w2_ref,
                   ys_hbm, xbuf, ybuf, h_scr, w1_scr, w3_scr, w2_scr, xsem, ysem):
    e = pl.program_id(0)
    nblk = nb_ref[e]
    g0 = ps_ref[e]
    n_used = nu_ref[0]

    def rows(g):
        return pl.ds(pl.multiple_of(g * EXPERT_ROWS, EXPERT_ROWS), EXPERT_ROWS)

    def x_copy(g):
        return pltpu.make_async_copy(xs_hbm.at[rows(g)], xbuf.at[g % 2], xsem.at[g % 2])

    def y_copy(g):
        return pltpu.make_async_copy(ybuf.at[g % 2], ys_hbm.at[rows(g)], ysem.at[g % 2])

    def normed(slot):
        return _unpack_rows(xbuf[slot]).astype(bf16)

    @pl.when(e == 0)
    def _():
        x_copy(0).start(priority=1)
        x_copy(1).start(priority=1)
        x_copy(0).wait()
        h_scr[...] = normed(0)

    @pl.when(nblk > 0)
    def _():
        w1_scr[...] = w1_ref[...].astype(bf16)
        w3_scr[...] = w3_ref[...].astype(bf16)
        w2_scr[...] = w2_ref[...].astype(bf16)

        def body(g, h):
            @pl.when(g + 2 < n_used)
            def _():
                x_copy(g + 2).start(priority=1)

            @pl.when(g + 1 < n_used)
            def _():
                x_copy(g + 1).wait()

            @pl.when(g >= 2)
            def _():
                y_copy(g - 2).wait()

            h_next = normed((g + 1) % 2)
            a = _dot(h, w1_scr[...])
            c = _dot(h, w3_scr[...])
            hid = (a * jax.nn.sigmoid(a)) * c
            ybuf[g % 2] = _pack_rows(_dot(hid.astype(bf16), w2_scr[...]))
            y_copy(g).start(priority=1)
            return h_next

        h_scr[...] = lax.fori_loop(g0, g0 + nblk, body, h_scr[...])

    @pl.when(e == pl.num_programs(0) - 1)
    def _():
        y_copy(n_used - 2).wait()
        y_copy(n_used - 1).wait()
        ybuf[0] = jnp.zeros(ybuf.shape[1:], u32)

        def fill(b):
            r = pl.ds(pl.multiple_of(b * EXPERT_ROWS, EXPERT_ROWS), EXPERT_ROWS)
            return pltpu.make_async_copy(ybuf.at[0], ys_hbm.at[r], ysem.at[0])

        def start(b, c):
            fill(b).start()
            return c

        def wait(b, c):
            fill(b).wait()
            return c

        lax.fori_loop(nu_ref[0], n_blocks, start, 0)
        lax.fori_loop(nu_ref[0], n_blocks, wait, 0)


def _experts(layer, pstart, seg_blocks, n_used, xs, w1, w3, w2):
    n_rows = xs.shape[0]
    n_blocks = n_rows // EXPERT_ROWS

    def w_map(e, *_):
        return (layer, e, 0, 0)

    block_buf = pltpu.VMEM((2, EXPERT_ROWS, ROW_SUBLANES, LANES), u32)
    grid_spec = pltpu.PrefetchScalarGridSpec(
        num_scalar_prefetch=3,
        grid=(N_EXPERTS,),
        in_specs=[
            pl.BlockSpec(memory_space=pl.ANY),
            pl.BlockSpec((None, None, D_MODEL, D_EXPERT), w_map),
            pl.BlockSpec((None, None, D_MODEL, D_EXPERT), w_map),
            pl.BlockSpec((None, None, D_EXPERT, D_MODEL), w_map),
        ],
        out_specs=pl.BlockSpec(memory_space=pl.ANY),
        scratch_shapes=[
            block_buf,
            block_buf,
            pltpu.VMEM((EXPERT_ROWS, D_MODEL), bf16),
            pltpu.VMEM((D_MODEL, D_EXPERT), bf16),
            pltpu.VMEM((D_MODEL, D_EXPERT), bf16),
            pltpu.VMEM((D_EXPERT, D_MODEL), bf16),
            pltpu.SemaphoreType.DMA((2,)),
            pltpu.SemaphoreType.DMA((2,)),
        ],
    )
    return pl.pallas_call(
        functools.partial(_expert_kernel, n_blocks),
        grid_spec=grid_spec,
        out_shape=jax.ShapeDtypeStruct((n_rows, ROW_SUBLANES, LANES), u32),
        compiler_params=_params(("arbitrary",)),
        name="experts",
    )(pstart, seg_blocks, n_used, xs, w1, w3, w2)


def _combine_kernel(dest_ref, next_ref, x_ref, w_ref, ys_hbm, out_ref, buf, sem):
    i = pl.program_id(0)
    slot = i % 2

    def gather(idx_ref, s):
        def start(j, c):
            for k in range(TOP_K):
                _row_copy(ys_hbm, idx_ref[0, k * MOVE_TILE + j], buf.at[s, k], j,
                          sem.at[s]).start(priority=k)
            return c

        lax.fori_loop(0, MOVE_TILE, start, 0, unroll=ISSUE_UNROLL)

    @pl.when(i == 0)
    def _():
        gather(dest_ref, slot)

    @pl.when(i + 1 < pl.num_programs(0))
    def _():
        gather(next_ref, 1 - slot)

    for k in range(TOP_K):
        _wait_rows(buf.at[slot, k], sem.at[slot])
    w = w_ref[...]
    y0 = _unpack_rows(buf[slot, 0])
    y1 = _unpack_rows(buf[slot, 1])
    out_ref[...] = x_ref[...] + (w[:, 0:1] * y0 + w[:, 1:2] * y1)


def _combine(dest_tiles, x2d, w_cols, ys):
    T = x2d.shape[0]
    n_tiles = T // MOVE_TILE
    return pl.pallas_call(
        _combine_kernel,
        grid=(n_tiles,),
        in_specs=[
            pl.BlockSpec((None, 1, TOP_K * MOVE_TILE), lambda i: (i, 0, 0),
                         memory_space=pltpu.SMEM),
            pl.BlockSpec((None, 1, TOP_K * MOVE_TILE),
                         lambda i: (jnp.minimum(i + 1, n_tiles - 1), 0, 0),
                         memory_space=pltpu.SMEM),
            pl.BlockSpec((MOVE_TILE, D_MODEL), lambda i: (i, 0)),
            pl.BlockSpec((MOVE_TILE, 8), lambda i: (i, 0)),
            pl.BlockSpec(memory_space=pl.ANY),
        ],
        out_specs=pl.BlockSpec((MOVE_TILE, D_MODEL), lambda i: (i, 0)),
        out_shape=jax.ShapeDtypeStruct((T, D_MODEL), f32),
        scratch_shapes=[
            pltpu.VMEM((2, TOP_K, MOVE_TILE, ROW_SUBLANES, LANES), u32),
            pltpu.SemaphoreType.DMA((2,)),
        ],
        compiler_params=_params(("arbitrary",)),
        name="combine",
    )(dest_tiles, dest_tiles, x2d, w_cols, ys)


def _moe(layer, x2d, logits_t, m_norm, b_group, b_expert, w1, w3, w2):
    T = x2d.shape[0]
    n_blocks = (T * TOP_K) // EXPERT_ROWS + N_EXPERTS
    n_rows = n_blocks * EXPERT_ROWS

    be = b_expert.reshape(N_EXPERTS, 1)
    bg = jnp.concatenate([b_group, jnp.zeros((8 - N_GROUPS,), f32)]).reshape(8, 1)
    oi, ow, cnt = _route(logits_t, be, bg)

    counts = cnt[:, 0].astype(jnp.int32)
    padded = (counts + EXPERT_ROWS - 1) // EXPERT_ROWS * EXPERT_ROWS
    pend = jnp.cumsum(padded)
    pstart = pend - padded
    n_used = (pend[-1] // EXPERT_ROWS).astype(jnp.int32).reshape(1)
    e_ids = jnp.arange(N_EXPERTS, dtype=jnp.int32)[:, None, None]
    seg_start = jnp.sum(jnp.where(oi[None, 0:2] == e_ids, pstart[:, None, None], 0), axis=0)
    dest = seg_start + oi[2:4]
    dest_tiles = dest.reshape(TOP_K, T // MOVE_TILE, MOVE_TILE).transpose(1, 0, 2).reshape(
        T // MOVE_TILE, 1, TOP_K * MOVE_TILE)
    w_cols = ow.T

    xs = _dispatch(pend.astype(jnp.int32), padded, n_used, dest_tiles, x2d, m_norm, n_rows)
    ys = _experts(layer, (pstart // EXPERT_ROWS).astype(jnp.int32), padded // EXPERT_ROWS,
                  n_used, xs, w1, w3, w2)
    return _combine(dest_tiles, x2d, w_cols, ys)


def _log_sigmoid(x):
    return jnp.minimum(x, 0.0) - jnp.log1p(jnp.exp(-jnp.abs(x)))


def _proj_kernel(tiles_per_seq, x_ref, kvn_ref, bn_ref, wkt_ref, wv_ref, wf_ref, bf_ref,
                 wq_ref, wg_ref, gq_ref, gk_ref, q_ref, kt_ref, v_ref, sg_ref, carry_scr):
    i = pl.program_id(0)
    x = x_ref[...]
    hkv = _rms(x, kvn_ref[...]).astype(bf16)
    hq = _rms(x, bn_ref[...]).astype(bf16)

    v_ref[...] = _dot(hkv, wv_ref[...]).astype(bf16)
    sg_ref[...] = jax.nn.sigmoid(_dot(hq, wg_ref[...])).astype(bf16)

    logf = _log_sigmoid(_dot(hkv, wf_ref[...]) + bf_ref[...])
    tm = x.shape[0]
    incl = (lax.broadcasted_iota(jnp.int32, (tm, tm), 0)
            >= lax.broadcasted_iota(jnp.int32, (tm, tm), 1)).astype(f32)
    @pl.when(i % tiles_per_seq == 0)
    def _():
        carry_scr[...] = jnp.zeros_like(carry_scr)

    cum = _dot(incl, logf, precision=lax.Precision.HIGHEST) + carry_scr[...]
    carry_scr[...] = cum[tm - 1:tm, :]
    cum = cum * LOG2E
    cum_t = cum.T

    lane = lax.broadcasted_iota(jnp.int32, (tm, HEAD_PAD), 1)
    sub = lax.broadcasted_iota(jnp.int32, (HEAD_DIM, tm), 0)
    q_raw = _dot(hq, wq_ref[...])
    k_raw_t = _dot_nt(wkt_ref[...], hkv)
    scale = HEAD_DIM ** -0.5 * LOG2E

    for h in range(N_HEADS):
        sl = slice(h * HEAD_PAD, (h + 1) * HEAD_PAD)
        qb = q_raw[:, sl]
        q_ms = jnp.sum(qb * qb, axis=-1, keepdims=True) * (1.0 / HEAD_DIM)
        qn = qb * lax.rsqrt(q_ms + EPS) * gq_ref[...] * scale
        c_hi, c_mid, c_lo = _split3(cum[:, h:h + 1])
        qa = jnp.where(lane < HEAD_DIM, qn,
             jnp.where(lane < AUX_CUM0, 1.0,
             jnp.where(lane == AUX_CUM0, c_hi,
             jnp.where(lane == AUX_CUM0 + 1, c_mid,
             jnp.where(lane == AUX_CUM0 + 2, c_lo, 0.0)))))
        q_ref[:, sl] = qa.astype(bf16)

        kb = k_raw_t[h * HEAD_DIM:(h + 1) * HEAD_DIM, :]
        k_ms = jnp.sum(kb * kb, axis=0, keepdims=True) * (1.0 / HEAD_DIM)
        kn = kb * lax.rsqrt(k_ms + EPS) * gk_ref[...]
        t_hi, t_mid, t_lo = _split3(cum_t[h:h + 1, :])
        aux = jnp.where(sub == 0, -t_hi,
              jnp.where(sub == 1, -t_mid,
              jnp.where(sub == 2, -t_lo,
              jnp.where(sub < 6, 1.0, 0.0))))
        sl = slice(h * HEAD_PAD, (h + 1) * HEAD_PAD)
        ka = jnp.concatenate([kn, aux], axis=0).astype(bf16)
        for t in range(tm // ATT_K):
            kt_ref[t, sl, :] = ka[:, t * ATT_K:(t + 1) * ATT_K]


def _proj(x2d, seq, kv_norm, b_norm, wkt, wv, wf, bfv, wq, wg, gq, gk):
    T = x2d.shape[0]
    n_tiles = T // ROW_TILE
    qw = N_HEADS * HEAD_PAD
    return pl.pallas_call(
        functools.partial(_proj_kernel, seq // ROW_TILE),
        grid=(n_tiles,),
        in_specs=[
            pl.BlockSpec((ROW_TILE, D_MODEL), lambda i: (i, 0)),
            _const_spec((1, D_MODEL)),
            _const_spec((1, D_MODEL)),
            _const_spec((D_MODEL, D_MODEL), single=True),
            _const_spec((D_MODEL, D_MODEL), single=True),
            _const_spec((D_MODEL, LANES)),
            _const_spec((1, LANES)),
            _const_spec((D_MODEL, qw), single=True),
            _const_spec((D_MODEL, D_MODEL), single=True),
            _const_spec((1, HEAD_PAD)),
            _const_spec((HEAD_DIM, 1)),
        ],
        out_specs=[
            pl.BlockSpec((ROW_TILE, qw), lambda i: (i, 0)),
            pl.BlockSpec((ROW_TILE // ATT_K, qw, ATT_K), lambda i: (i, 0, 0)),
            pl.BlockSpec((ROW_TILE, D_MODEL), lambda i: (i, 0)),
            pl.BlockSpec((ROW_TILE, D_MODEL), lambda i: (i, 0)),
        ],
        out_shape=[
            jax.ShapeDtypeStruct((T, qw), bf16),
            jax.ShapeDtypeStruct((T // ATT_K, qw, ATT_K), bf16),
            jax.ShapeDtypeStruct((T, D_MODEL), bf16),
            jax.ShapeDtypeStruct((T, D_MODEL), bf16),
        ],
        scratch_shapes=[pltpu.VMEM((1, LANES), f32)],
        compiler_params=_params(("arbitrary",)),
        name="proj",
    )(x2d, kv_norm, b_norm, wkt, wv, wf, bfv, wq, wg, gq, gk)


def _attn_kernel(q_ref, kt_ref, v_ref, o_ref, s_scr, m_scr, l_scr, acc_scr):
    qi = pl.program_id(2)
    tiles = ATT_Q // ATT_K
    groups = ATT_K // LANES
    heads = [slice(hh * HEAD_PAD, (hh + 1) * HEAD_PAD) for hh in range(2)]
    m_scr[...] = jnp.full(m_scr.shape, -jnp.inf, f32)

    def lane_max(mx, s):
        for g in range(groups):
            mx = jnp.maximum(mx, s[:, g * LANES:(g + 1) * LANES])
        return mx

    def probs(s, mb):
        ps = [jnp.exp2(s[:, g * LANES:(g + 1) * LANES] - mb) for g in range(groups)]
        return ps, functools.reduce(lambda a, b: a + b, ps)

    def body_a(trip, c):
        for t in range(tiles):
            kt_idx = trip * tiles + t
            for hh in range(2):
                s = _dot(q_ref[:, heads[hh]], kt_ref[kt_idx, heads[hh], :])
                s_scr[hh, kt_idx] = s
                m_scr[hh] = lane_max(m_scr[hh], s)
        return c

    lax.fori_loop(0, qi, body_a, 0)
    for t in range(tiles):
        kt_idx = qi * tiles + t
        rows = ATT_Q - t * ATT_K
        visible = (lax.broadcasted_iota(jnp.int32, (rows, ATT_K), 1)
                   <= lax.broadcasted_iota(jnp.int32, (rows, ATT_K), 0))
        for hh in range(2):
            s = _dot(q_ref[t * ATT_K:, heads[hh]], kt_ref[kt_idx, heads[hh], :])
            s = jnp.where(visible, s, -jnp.inf)
            s_scr[hh, kt_idx, t * ATT_K:, :] = s
            m_scr[hh, t * ATT_K:, :] = lane_max(m_scr[hh, t * ATT_K:, :], s)

    for hh in range(2):
        row_max = jnp.max(m_scr[hh], axis=-1, keepdims=True)
        m_scr[hh] = jnp.broadcast_to(row_max, (ATT_Q, LANES))
    l_scr[...] = jnp.zeros(l_scr.shape, f32)
    acc_scr[...] = jnp.zeros(acc_scr.shape, f32)

    def body_b(trip, c):
        for hh in range(2):
            mb = m_scr[hh]
            lsum, acc = l_scr[hh], acc_scr[hh]
            for t0 in range(0, tiles, PV_TILES):
                kt_idx = trip * tiles + t0
                row0 = pl.multiple_of(kt_idx * ATT_K, PV_TILES * ATT_K)
                vb = v_ref[pl.ds(row0, PV_TILES * ATT_K), :]
                parts = []
                for t in range(PV_TILES):
                    ps, psum = probs(s_scr[hh, kt_idx + t], mb)
                    parts += ps
                    lsum = lsum + psum
                acc = acc + _dot(jnp.concatenate(parts, axis=1).astype(bf16), vb)
            l_scr[hh] = lsum
            acc_scr[hh] = acc
        return c

    lax.fori_loop(0, qi, body_b, 0)
    for t in range(tiles):
        kt_idx = qi * tiles + t
        row0 = pl.multiple_of(kt_idx * ATT_K, ATT_K)
        vb = v_ref[pl.ds(row0, ATT_K), :]
        for hh in range(2):
            ps, psum = probs(s_scr[hh, kt_idx, t * ATT_K:, :], m_scr[hh, t * ATT_K:, :])
            l_scr[hh, t * ATT_K:, :] += psum
            acc_scr[hh, t * ATT_K:, :] += _dot(jnp.concatenate(ps, axis=1).astype(bf16), vb)

    lane = lax.broadcasted_iota(jnp.int32, (ATT_Q, 2 * HEAD_DIM), 1)
    o0 = acc_scr[0] / jnp.sum(l_scr[0], axis=-1, keepdims=True)
    o1 = acc_scr[1] / jnp.sum(l_scr[1], axis=-1, keepdims=True)
    o_ref[...] = jnp.where(lane < HEAD_DIM, o0, o1).astype(bf16)


def _attention(q, kt, v, batch, seq):
    T = q.shape[0]
    nq = seq // ATT_Q
    nk = seq // ATT_K
    pairs = N_HEADS // 2
    return pl.pallas_call(
        _attn_kernel,
        grid=(batch, pairs, nq),
        in_specs=[
            pl.BlockSpec((ATT_Q, 2 * HEAD_PAD), lambda b, p, i: (b * nq + i, p)),
            pl.BlockSpec((nk, 2 * HEAD_PAD, ATT_K), lambda b, p, i: (b, p, 0)),
            pl.BlockSpec((seq, 2 * HEAD_DIM), lambda b, p, i: (b, p)),
        ],
        out_specs=pl.BlockSpec((ATT_Q, 2 * HEAD_DIM), lambda b, p, i: (b * nq + i, p)),
        out_shape=jax.ShapeDtypeStruct((T, D_MODEL), bf16),
        scratch_shapes=[
            pltpu.VMEM((2, nk, ATT_Q, ATT_K), f32),
            pltpu.VMEM((2, ATT_Q, LANES), f32),
            pltpu.VMEM((2, ATT_Q, LANES), f32),
            pltpu.VMEM((2, ATT_Q, 2 * HEAD_DIM), f32),
        ],
        compiler_params=_params(("parallel", "parallel", "parallel")),
        name="attn",
    )(q, kt, v)


def _attn_out_kernel(x_ref, o_ref, sg_ref, wo_ref, mn_ref, wr_ref, x3_ref, lg_ref):
    gated = (o_ref[...].astype(f32) * sg_ref[...].astype(f32)).astype(bf16)
    x3 = x_ref[...] + _dot(gated, wo_ref[...])
    x3_ref[...] = x3
    lg_ref[...] = _router_logits_t(x3, mn_ref[...], wr_ref[...])


def _attn_out(x2d, o, sg, wo, m_norm, w_router):
    T = x2d.shape[0]
    return pl.pallas_call(
        _attn_out_kernel,
        grid=(T // SGU_TILE,),
        in_specs=[
            pl.BlockSpec((SGU_TILE, D_MODEL), lambda i: (i, 0)),
            pl.BlockSpec((SGU_TILE, D_MODEL), lambda i: (i, 0)),
            pl.BlockSpec((SGU_TILE, D_MODEL), lambda i: (i, 0)),
            _const_spec((D_MODEL, D_MODEL)),
            _const_spec((1, D_MODEL)),
            _const_spec((2 * D_MODEL, 2 * LANES)),
        ],
        out_specs=[
            pl.BlockSpec((SGU_TILE, D_MODEL), lambda i: (i, 0)),
            pl.BlockSpec((ROUTER_ROWS, SGU_TILE), lambda i: (0, i)),
        ],
        out_shape=[
            jax.ShapeDtypeStruct((T, D_MODEL), f32),
            jax.ShapeDtypeStruct((ROUTER_ROWS, T), f32),
        ],
        compiler_params=_params(("parallel",)),
        name="attn_out",
    )(x2d, o, sg, wo, m_norm, w_router)


def _router_weight(w_group, w_expert):
    pad = jnp.zeros((D_MODEL, LANES - N_EXPERTS - N_GROUPS), f32)
    w = jnp.concatenate([w_expert, w_group, pad], axis=1)
    w_hi = w.astype(bf16)
    w_lo = (w - w_hi.astype(f32)).astype(bf16)
    top = jnp.concatenate([w_hi, w_lo], axis=1)
    bottom = jnp.concatenate([w_hi, jnp.zeros_like(w_hi)], axis=1)
    return jnp.concatenate([top, bottom], axis=0)


def kernel(x, a_norm, a_w_in, a_b_in, a_v_norm, a_w_s, a_b_s, a_w_out, kv_norm, kv_w, kv_b_f,
           k_norm, b_norm, b_w_qg, q_norm, b_w_out, m_norm, m_w_group, m_b_group, m_w_expert,
           m_b_expert, m_w1, m_w3, m_w2):
    batch, seq, _ = x.shape
    T = batch * seq
    x2d = x.reshape(T, D_MODEL)

    x1, lg0 = _sgu(
        x2d, a_norm[0].reshape(1, -1), a_w_in[0].astype(bf16), a_b_in[0].reshape(1, -1),
        a_v_norm[0].reshape(1, -1), a_w_s[0], a_b_s[0].T, a_w_out[0].astype(bf16),
        m_norm[0].reshape(1, -1), _router_weight(m_w_group[0], m_w_expert[0]))
    x2 = _moe(0, x1, lg0, m_norm[0].reshape(1, -1), m_b_group[0], m_b_expert[0],
              m_w1, m_w3, m_w2)

    wk = kv_w[:, :D_MODEL]
    wv = kv_w[:, D_MODEL:2 * D_MODEL]
    wf = jnp.pad(kv_w[:, 2 * D_MODEL:], ((0, 0), (0, LANES - N_HEADS)))
    bfv = jnp.pad(kv_b_f, (0, LANES - N_HEADS)).reshape(1, LANES)
    wq = b_w_qg[0][:, :D_MODEL]
    wg = b_w_qg[0][:, D_MODEL:]
    gq = jnp.pad(q_norm[0], (0, HEAD_PAD - HEAD_DIM)).reshape(1, HEAD_PAD)
    gk = k_norm.reshape(HEAD_DIM, 1)
    wq_pad = jnp.pad(wq.reshape(D_MODEL, N_HEADS, HEAD_DIM),
                     ((0, 0), (0, 0), (0, HEAD_PAD - HEAD_DIM))).reshape(D_MODEL, -1)
    q, kt, v, sg = _proj(
        x2, seq, kv_norm.reshape(1, -1), b_norm[0].reshape(1, -1),
        wk.T.astype(bf16), wv.astype(bf16), wf.astype(bf16), bfv,
        wq_pad.astype(bf16), wg.astype(bf16), gq, gk)
    o = _attention(q, kt, v, batch, seq)
    x3, lg1 = _attn_out(x2, o, sg, b_w_out[0].astype(bf16), m_norm[1].reshape(1, -1),
                        _router_weight(m_w_group[1], m_w_expert[1]))
    x4 = _moe(1, x3, lg1, m_norm[1].reshape(1, -1), m_b_group[1], m_b_expert[1],
              m_w1, m_w3, m_w2)
    return x4.reshape(batch, seq, D_MODEL)
```

```python
import functools
import math

import jax
import jax.numpy as jnp
import numpy as np
from jax import lax
from jax.experimental import pallas as pl
from jax.experimental.pallas import tpu as pltpu

D_MODEL = 1024
EPS = 1e-6
SGU_BLOCK = 128
SGU_CHUNK = 64
CHUNK_SHIFT = SGU_CHUNK.bit_length() - 1
SGU_WIDTH = 2 * D_MODEL
SGU_GROUPS = 8
SGU_GROUP_DIM = SGU_WIDTH // SGU_GROUPS
N_HEADS = 16
HEAD_DIM = D_MODEL // N_HEADS
N_GROUPS = 4
EXPERTS_PER_GROUP = 8
N_EXPERTS = N_GROUPS * EXPERTS_PER_GROUP
GROUP_SHIFT = EXPERTS_PER_GROUP.bit_length() - 1
TOP_K = 2
D_EXPERT = D_MODEL // 2

LANES = 128
ROW_SUBLANES = D_MODEL // 2 // LANES
ISSUE_UNROLL = 8
VMEM_LIMIT_BYTES = 56 * 1024 * 1024

ROW_TILE = 256
SGU_TILE = 512
ROUTE_TILE = 1024
EXPERT_ROWS = 512
MOVE_TILE = 512
MOVE_CHUNK = 64
ATT_Q = 1024
ATT_K = 256
PV_TILES = 2
HEAD_PAD = 2 * HEAD_DIM
ROUTER_ROWS = 40
AUX_ONE0 = HEAD_DIM
AUX_CUM0 = HEAD_DIM + 3

LOG2E = math.log2(math.e)

bf16 = jnp.bfloat16
f32 = jnp.float32
u32 = jnp.uint32
HI_HALF = np.uint32(0xFFFF0000)


def _dot(a, b, precision=None):
    return jnp.dot(a, b, preferred_element_type=f32, precision=precision)


def _dot_nt(a, b, precision=None):
    return lax.dot_general(a, b, (((1,), (1,)), ((), ())), preferred_element_type=f32,
                           precision=precision)


def _rms(x, g):
    ms = jnp.mean(x * x, axis=-1, keepdims=True)
    return x * lax.rsqrt(ms + EPS) * g


def _gelu_tanh(z):
    c = math.sqrt(2.0 / math.pi)
    return z * (0.5 * (1.0 + jnp.tanh(c * (z + 0.044715 * (z * z * z)))))


def _split3(c):
    hi = c.astype(bf16).astype(f32)
    r = c - hi
    mid = r.astype(bf16).astype(f32)
    lo = r - mid
    return hi, mid, lo


def _pack_rows(v):
    half = D_MODEL // 2
    lo = lax.bitcast_convert_type(v[:, :half].astype(bf16).astype(f32), u32) >> 16
    hi = lax.bitcast_convert_type(v[:, half:].astype(bf16).astype(f32), u32) & HI_HALF
    return (hi | lo).reshape(v.shape[0], ROW_SUBLANES, LANES)


def _unpack_rows(w):
    w = w.reshape(w.shape[0], D_MODEL // 2)
    lo = lax.bitcast_convert_type(w << 16, f32)
    hi = lax.bitcast_convert_type(w & HI_HALF, f32)
    return jnp.concatenate([lo, hi], axis=1)


def _const_spec(shape, single=False):
    nd = len(shape)
    mode = pl.Buffered(1) if single else None
    return pl.BlockSpec(shape, lambda *_: (0,) * nd, pipeline_mode=mode)


def _params(sem):
    return pltpu.CompilerParams(dimension_semantics=sem, vmem_limit_bytes=VMEM_LIMIT_BYTES)


def _router_logits_t(x_new, mn, wr3):
    hm = _rms(x_new, mn)
    h_hi = hm.astype(bf16)
    h_lo = (hm - h_hi.astype(f32)).astype(bf16)
    both = _dot(jnp.concatenate([h_hi, h_lo], axis=1), wr3)
    lg = both[:, :LANES] + both[:, LANES:]
    return lg.T[:ROUTER_ROWS, :]


def _sgu_kernel(x_ref, an_ref, win_ref, bin_ref, vn_ref, ws_ref, bst_ref, wout_ref, mn_ref,
                wr_ref, x1_ref, lg_ref, u_scr, v_scr, gated_scr):
    x = x_ref[...]
    h = _rms(x, an_ref[...]).astype(bf16)
    u_scr[...] = _gelu_tanh(_dot(h, win_ref[:, :SGU_WIDTH]) + bin_ref[:, :SGU_WIDTH])
    v = _gelu_tanh(_dot(h, win_ref[:, SGU_WIDTH:]) + bin_ref[:, SGU_WIDTH:])
    v_scr[...] = _rms(v, vn_ref[...]).astype(bf16)

    t_chunk = lax.broadcasted_iota(jnp.int32, (SGU_BLOCK, SGU_BLOCK), 0) >> CHUNK_SHIFT
    s_chunk = lax.broadcasted_iota(jnp.int32, (SGU_BLOCK, SGU_BLOCK), 1) >> CHUNK_SHIFT
    causal = t_chunk >= s_chunk
    for g in range(SGU_GROUPS):
        wsm = jnp.where(causal, ws_ref[g], 0.0).astype(bf16)
        cols = slice(g * SGU_GROUP_DIM, (g + 1) * SGU_GROUP_DIM)
        for sb in range(SGU_TILE // SGU_BLOCK):
            rows = slice(sb * SGU_BLOCK, (sb + 1) * SGU_BLOCK)
            mixed = _dot(wsm, v_scr[rows, cols]) + bst_ref[:, g:g + 1]
            gated_scr[rows, cols] = (u_scr[rows, cols] * mixed).astype(bf16)

    x1 = x + _dot(gated_scr[...], wout_ref[...])
    x1_ref[...] = x1
    lg_ref[...] = _router_logits_t(x1, mn_ref[...], wr_ref[...])


def _sgu(x2d, a_norm, w_in, b_in, v_norm, w_s, b_s_t, w_out, m_norm, w_router):
    T = x2d.shape[0]
    return pl.pallas_call(
        _sgu_kernel,
        grid=(T // SGU_TILE,),
        in_specs=[
            pl.BlockSpec((SGU_TILE, D_MODEL), lambda i: (i, 0)),
            _const_spec((1, D_MODEL)),
            _const_spec((D_MODEL, 2 * SGU_WIDTH), single=True),
            _const_spec((1, 2 * SGU_WIDTH)),
            _const_spec((1, SGU_WIDTH)),
            _const_spec((SGU_GROUPS, SGU_BLOCK, SGU_BLOCK)),
            _const_spec((SGU_BLOCK, SGU_GROUPS)),
            _const_spec((SGU_WIDTH, D_MODEL), single=True),
            _const_spec((1, D_MODEL)),
            _const_spec((2 * D_MODEL, 2 * LANES)),
        ],
        out_specs=[
            pl.BlockSpec((SGU_TILE, D_MODEL), lambda i: (i, 0)),
            pl.BlockSpec((ROUTER_ROWS, SGU_TILE), lambda i: (0, i)),
        ],
        out_shape=[
            jax.ShapeDtypeStruct((T, D_MODEL), f32),
            jax.ShapeDtypeStruct((ROUTER_ROWS, T), f32),
        ],
        scratch_shapes=[
            pltpu.VMEM((SGU_TILE, SGU_WIDTH), f32),
            pltpu.VMEM((SGU_TILE, SGU_WIDTH), bf16),
            pltpu.VMEM((SGU_TILE, SGU_WIDTH), bf16),
        ],
        compiler_params=_params(("parallel",)),
        name="sgu",
    )(x2d, a_norm, w_in, b_in, v_norm, w_s, b_s_t, w_out, m_norm, w_router)


def _route_kernel(lg_ref, be_ref, bg_ref, oi_ref, ow_ref, cnt_ref, carry_scr, before_scr):
    step = pl.program_id(0)
    tt = lg_ref.shape[1]

    @pl.when(step == 0)
    def _():
        carry_scr[...] = jnp.zeros_like(carry_scr)
        before_scr[...] = (lax.broadcasted_iota(jnp.int32, (tt, tt), 0)
                           < lax.broadcasted_iota(jnp.int32, (tt, tt), 1)).astype(bf16)

    lg = lg_ref[...]
    e_l = lg[0:N_EXPERTS, :] + be_ref[...]
    g_l = lg[N_EXPERTS:N_EXPERTS + 8, :] + bg_ref[...]
    g_row = lax.broadcasted_iota(jnp.int32, g_l.shape, 0).astype(f32)
    g_l = jnp.where(g_row < N_GROUPS, g_l, -jnp.inf)
    g_max = jnp.max(g_l, axis=0, keepdims=True)
    g_sel = jnp.min(jnp.where(g_l == g_max, g_row, 8.0), axis=0, keepdims=True)
    g_den = jnp.sum(jnp.exp(g_l - g_max), axis=0, keepdims=True)
    g_w = 1.0 / g_den

    e_row_i = lax.broadcasted_iota(jnp.int32, e_l.shape, 0)
    e_row = e_row_i.astype(f32)
    e_grp = (e_row_i >> GROUP_SHIFT).astype(f32)
    e_in = jnp.where(e_grp == g_sel, e_l, -jnp.inf)
    m1 = jnp.max(e_in, axis=0, keepdims=True)
    i1 = jnp.min(jnp.where(e_in == m1, e_row, float(N_EXPERTS)), axis=0, keepdims=True)
    e_in2 = jnp.where(e_row == i1, -jnp.inf, e_in)
    m2 = jnp.max(e_in2, axis=0, keepdims=True)
    i2 = jnp.min(jnp.where(e_in2 == m2, e_row, float(N_EXPERTS)), axis=0, keepdims=True)
    t = jnp.exp(m2 - m1)
    p1 = 1.0 / (1.0 + t)
    p2 = t / (1.0 + t)

    hit1 = e_row == i1
    hit2 = e_row == i2
    member = jnp.logical_or(hit1, hit2)
    prefix = _dot(member.astype(bf16), before_scr[...])
    rank_full = prefix + carry_scr[...]
    rank1 = jnp.sum(jnp.where(hit1, rank_full, 0.0), axis=0, keepdims=True)
    rank2 = jnp.sum(jnp.where(hit2, rank_full, 0.0), axis=0, keepdims=True)
    carry_new = carry_scr[...] + jnp.sum(member.astype(f32), axis=1, keepdims=True)
    carry_scr[...] = carry_new

    oi_ref[...] = jnp.zeros(oi_ref.shape, jnp.int32)
    oi_ref[0:1, :] = i1.astype(jnp.int32)
    oi_ref[1:2, :] = i2.astype(jnp.int32)
    oi_ref[2:3, :] = rank1.astype(jnp.int32)
    oi_ref[3:4, :] = rank2.astype(jnp.int32)
    ow_ref[...] = jnp.zeros(ow_ref.shape, f32)
    ow_ref[0:1, :] = g_w * p1
    ow_ref[1:2, :] = g_w * p2
    cnt_ref[...] = jnp.broadcast_to(carry_new, cnt_ref.shape)


def _route(logits_t, b_expert, b_group):
    T = logits_t.shape[1]
    return pl.pallas_call(
        _route_kernel,
        grid=(T // ROUTE_TILE,),
        in_specs=[
            pl.BlockSpec((ROUTER_ROWS, ROUTE_TILE), lambda i: (0, i)),
            _const_spec((N_EXPERTS, 1)),
            _const_spec((8, 1)),
        ],
        out_specs=[
            pl.BlockSpec((8, ROUTE_TILE), lambda i: (0, i)),
            pl.BlockSpec((8, ROUTE_TILE), lambda i: (0, i)),
            _const_spec((N_EXPERTS, LANES)),
        ],
        out_shape=[
            jax.ShapeDtypeStruct((8, T), jnp.int32),
            jax.ShapeDtypeStruct((8, T), f32),
            jax.ShapeDtypeStruct((N_EXPERTS, LANES), f32),
        ],
        scratch_shapes=[pltpu.VMEM((N_EXPERTS, 1), f32),
                        pltpu.VMEM((ROUTE_TILE, ROUTE_TILE), bf16)],
        compiler_params=_params(("arbitrary",)),
        name="route",
    )(logits_t, b_expert, b_group)


def _row_copy(src, s, dst, d, sem):
    return pltpu.make_async_copy(src.at[s], dst.at[d], sem)


def _wait_rows(buf, sem):
    pltpu.make_async_copy(buf, buf, sem).wait()


def _dispatch_kernel(n_blocks, pend_ref, padded_ref, nu_ref, dest_ref, x_ref, mn_ref, xs_hbm,
                     zero_scr, x3_scr, zsem, sem):
    @pl.when(pl.program_id(0) == 0)
    def _():
        zero_scr[...] = jnp.zeros_like(zero_scr)

        def fill(row0):
            return pltpu.make_async_copy(zero_scr, xs_hbm.at[pl.ds(row0, EXPERT_ROWS)], zsem)

        for e in range(N_EXPERTS):
            @pl.when(padded_ref[e] > 0)
            def _():
                fill(pend_ref[e] - EXPERT_ROWS).start()

            @pl.when(nu_ref[0] + e < n_blocks)
            def _():
                fill((nu_ref[0] + e) * EXPERT_ROWS).start()

        for e in range(N_EXPERTS):
            @pl.when(padded_ref[e] > 0)
            def _():
                fill(0).wait()

            @pl.when(nu_ref[0] + e < n_blocks)
            def _():
                fill(0).wait()

    i = pl.program_id(0)
    n_tiles = pl.num_programs(0) - 1

    def staged(rows):
        return _pack_rows(_rms(x_ref[rows, :], mn_ref[...]))

    @pl.when(i == 0)
    def _():
        x3_scr[0] = staged(slice(None))

    @pl.when(i > 0)
    def _():
        cur = (i - 1) % 3
        nxt = i % 3

        def chunk(c, carry):
            r0 = pl.multiple_of(c * MOVE_CHUNK, MOVE_CHUNK)
            for j in range(MOVE_CHUNK):
                for k in range(TOP_K):
                    _row_copy(x3_scr.at[cur], r0 + j, xs_hbm,
                              dest_ref[0, k * MOVE_TILE + r0 + j], sem.at[cur]).start(priority=k)
            x3_scr[nxt, pl.ds(r0, MOVE_CHUNK)] = staged(pl.ds(r0, MOVE_CHUNK))
            return carry

        lax.fori_loop(0, MOVE_TILE // MOVE_CHUNK, chunk, 0)

        @pl.when(i > 1)
        def _():
            for k in range(TOP_K):
                _wait_rows(x3_scr.at[(i - 2) % 3], sem.at[(i - 2) % 3])

        @pl.when(i == n_tiles)
        def _():
            for k in range(TOP_K):
                _wait_rows(x3_scr.at[cur], sem.at[cur])


def _dispatch(pend, padded, n_used, dest_tiles, x2d, m_norm, n_rows):
    T = x2d.shape[0]
    n_blocks = n_rows // EXPERT_ROWS
    n_tiles = T // MOVE_TILE
    grid_spec = pltpu.PrefetchScalarGridSpec(
        num_scalar_prefetch=3,
        grid=(n_tiles + 1,),
        in_specs=[
            pl.BlockSpec((None, 1, TOP_K * MOVE_TILE),
                         lambda i, *_: (jnp.maximum(i - 1, 0), 0, 0), memory_space=pltpu.SMEM),
            pl.BlockSpec((MOVE_TILE, D_MODEL), lambda i, *_: (jnp.minimum(i, n_tiles - 1), 0)),
            pl.BlockSpec((1, D_MODEL), lambda i, *_: (0, 0)),
        ],
        out_specs=pl.BlockSpec(memory_space=pl.ANY),
        scratch_shapes=[
            pltpu.VMEM((EXPERT_ROWS, ROW_SUBLANES, LANES), u32),
            pltpu.VMEM((3, MOVE_TILE, ROW_SUBLANES, LANES), u32),
            pltpu.SemaphoreType.DMA(()),
            pltpu.SemaphoreType.DMA((3,)),
        ],
    )
    return pl.pallas_call(
        functools.partial(_dispatch_kernel, n_blocks),
        grid_spec=grid_spec,
        out_shape=jax.ShapeDtypeStruct((n_rows, ROW_SUBLANES, LANES), u32),
        compiler_params=_params(("arbitrary",)),
        name="dispatch",
    )(pend, padded, n_used, dest_tiles, x2d, m_norm)


def _expert_kernel(n_blocks, ps_ref, nb_ref, nu_ref, xs_hbm, w1_ref, w3_ref, w2_ref,
                   ys_hbm, xbuf, ybuf, h_scr, w1_scr, w3_scr, w2_scr, xsem, ysem):
    e = pl.program_id(0)
    nblk = nb_ref[e]
    g0 = ps_ref[e]
    n_used = nu_ref[0]

    def rows(g):
        return pl.ds(pl.multiple_of(g * EXPERT_ROWS, EXPERT_ROWS), EXPERT_ROWS)

    def x_copy(g):
        return pltpu.make_async_copy(xs_hbm.at[rows(g)], xbuf.at[g % 2], xsem.at[g % 2])

    def y_copy(g):
        return pltpu.make_async_copy(ybuf.at[g % 2], ys_hbm.at[rows(g)], ysem.at[g % 2])

    def normed(slot):
        return _unpack_rows(xbuf[slot]).astype(bf16)

    @pl.when(e == 0)
    def _():
        x_copy(0).start(priority=1)
        x_copy(1).start(priority=1)
        x_copy(0).wait()
        h_scr[...] = normed(0)

    @pl.when(nblk > 0)
    def _():
        w1_scr[...] = w1_ref[...].astype(bf16)
        w3_scr[...] = w3_ref[...].astype(bf16)
        w2_scr[...] = w2_ref[...].astype(bf16)

        def body(g, h):
            @pl.when(g + 2 < n_used)
            def _():
                x_copy(g + 2).start(priority=1)

            @pl.when(g + 1 < n_used)
            def _():
                x_copy(g + 1).wait()

            @pl.when(g >= 2)
            def _():
                y_copy(g - 2).wait()

            h_next = normed((g + 1) % 2)
            a = _dot(h, w1_scr[...])
            c = _dot(h, w3_scr[...])
            hid = (a * jax.nn.sigmoid(a)) * c
            ybuf[g % 2] = _pack_rows(_dot(hid.astype(bf16), w2_scr[...]))
            y_copy(g).start(priority=1)
            return h_next

        h_scr[...] = lax.fori_loop(g0, g0 + nblk, body, h_scr[...])

    @pl.when(e == pl.num_programs(0) - 1)
    def _():
        y_copy(n_used - 2).wait()
        y_copy(n_used - 1).wait()
        ybuf[0] = jnp.zeros(ybuf.shape[1:], u32)

        def fill(b):
            r = pl.ds(pl.multiple_of(b * EXPERT_ROWS, EXPERT_ROWS), EXPERT_ROWS)
            return pltpu.make_async_copy(ybuf.at[0], ys_hbm.at[r], ysem.at[0])

        def start(b, c):
            fill(b).start()
            return c

        def wait(b, c):
            fill(b).wait()
            return c

        lax.fori_loop(nu_ref[0], n_blocks, start, 0)
        lax.fori_loop(nu_ref[0], n_blocks, wait, 0)


def _experts(layer, pstart, seg_blocks, n_used, xs, w1, w3, w2):
    n_rows = xs.shape[0]
    n_blocks = n_rows // EXPERT_ROWS

    def w_map(e, *_):
        return (layer, e, 0, 0)

    block_buf = pltpu.VMEM((2, EXPERT_ROWS, ROW_SUBLANES, LANES), u32)
    grid_spec = pltpu.PrefetchScalarGridSpec(
        num_scalar_prefetch=3,
        grid=(N_EXPERTS,),
        in_specs=[
            pl.BlockSpec(memory_space=pl.ANY),
            pl.BlockSpec((None, None, D_MODEL, D_EXPERT), w_map),
            pl.BlockSpec((None, None, D_MODEL, D_EXPERT), w_map),
            pl.BlockSpec((None, None, D_EXPERT, D_MODEL), w_map),
        ],
        out_specs=pl.BlockSpec(memory_space=pl.ANY),
        scratch_shapes=[
            block_buf,
            block_buf,
            pltpu.VMEM((EXPERT_ROWS, D_MODEL), bf16),
            pltpu.VMEM((D_MODEL, D_EXPERT), bf16),
            pltpu.VMEM((D_MODEL, D_EXPERT), bf16),
            pltpu.VMEM((D_EXPERT, D_MODEL), bf16),
            pltpu.SemaphoreType.DMA((2,)),
            pltpu.SemaphoreType.DMA((2,)),
        ],
    )
    return pl.pallas_call(
        functools.partial(_expert_kernel, n_blocks),
        grid_spec=grid_spec,
        out_shape=jax.ShapeDtypeStruct((n_rows, ROW_SUBLANES, LANES), u32),
        compiler_params=_params(("arbitrary",)),
        name="experts",
    )(pstart, seg_blocks, n_used, xs, w1, w3, w2)


def _combine_kernel(dest_ref, next_ref, x_ref, w_ref, ys_hbm, out_ref, buf, sem):
    i = pl.program_id(0)
    slot = i % 2

    def gather(idx_ref, s):
        def start(j, c):
            for k in range(TOP_K):
                _row_copy(ys_hbm, idx_ref[0, k * MOVE_TILE + j], buf.at[s, k], j,
                          sem.at[s]).start(priority=k)
            return c

        lax.fori_loop(0, MOVE_TILE, start, 0, unroll=ISSUE_UNROLL)

    @pl.when(i == 0)
    def _():
        gather(dest_ref, slot)

    @pl.when(i + 1 < pl.num_programs(0))
    def _():
        gather(next_ref, 1 - slot)

    for k in range(TOP_K):
        _wait_rows(buf.at[slot, k], sem.at[slot])
    w = w_ref[...]
    y0 = _unpack_rows(buf[slot, 0])
    y1 = _unpack_rows(buf[slot, 1])
    out_ref[...] = x_ref[...] + (w[:, 0:1] * y0 + w[:, 1:2] * y1)


def _combine(dest_tiles, x2d, w_cols, ys):
    T = x2d.shape[0]
    n_tiles = T // MOVE_TILE
    return pl.pallas_call(
        _combine_kernel,
        grid=(n_tiles,),
        in_specs=[
            pl.BlockSpec((None, 1, TOP_K * MOVE_TILE), lambda i: (i, 0, 0),
                         memory_space=pltpu.SMEM),
            pl.BlockSpec((None, 1, TOP_K * MOVE_TILE),
                         lambda i: (jnp.minimum(i + 1, n_tiles - 1), 0, 0),
                         memory_space=pltpu.SMEM),
            pl.BlockSpec((MOVE_TILE, D_MODEL), lambda i: (i, 0)),
            pl.BlockSpec((MOVE_TILE, 8), lambda i: (i, 0)),
            pl.BlockSpec(memory_space=pl.ANY),
        ],
        out_specs=pl.BlockSpec((MOVE_TILE, D_MODEL), lambda i: (i, 0)),
        out_shape=jax.ShapeDtypeStruct((T, D_MODEL), f32),
        scratch_shapes=[
            pltpu.VMEM((2, TOP_K, MOVE_TILE, ROW_SUBLANES, LANES), u32),
            pltpu.SemaphoreType.DMA((2,)),
        ],
        compiler_params=_params(("arbitrary",)),
        name="combine",
    )(dest_tiles, dest_tiles, x2d, w_cols, ys)


def _moe(layer, x2d, logits_t, m_norm, b_group, b_expert, w1, w3, w2):
    T = x2d.shape[0]
    n_blocks = (T * TOP_K) // EXPERT_ROWS + N_EXPERTS
    n_rows = n_blocks * EXPERT_ROWS

    be = b_expert.reshape(N_EXPERTS, 1)
    bg = jnp.concatenate([b_group, jnp.zeros((8 - N_GROUPS,), f32)]).reshape(8, 1)
    oi, ow, cnt = _route(logits_t, be, bg)

    counts = cnt[:, 0].astype(jnp.int32)
    padded = (counts + EXPERT_ROWS - 1) // EXPERT_ROWS * EXPERT_ROWS
    pend = jnp.cumsum(padded)
    pstart = pend - padded
    n_used = (pend[-1] // EXPERT_ROWS).astype(jnp.int32).reshape(1)
    e_ids = jnp.arange(N_EXPERTS, dtype=jnp.int32)[:, None, None]
    seg_start = jnp.sum(jnp.where(oi[None, 0:2] == e_ids, pstart[:, None, None], 0), axis=0)
    dest = seg_start + oi[2:4]
    dest_tiles = dest.reshape(TOP_K, T // MOVE_TILE, MOVE_TILE).transpose(1, 0, 2).reshape(
        T // MOVE_TILE, 1, TOP_K * MOVE_TILE)
    w_cols = ow.T

    xs = _dispatch(pend.astype(jnp.int32), padded, n_used, dest_tiles, x2d, m_norm, n_rows)
    ys = _experts(layer, (pstart // EXPERT_ROWS).astype(jnp.int32), padded // EXPERT_ROWS,
                  n_used, xs, w1, w3, w2)
    return _combine(dest_tiles, x2d, w_cols, ys)


def _log_sigmoid(x):
    return jnp.minimum(x, 0.0) - jnp.log1p(jnp.exp(-jnp.abs(x)))


def _proj_kernel(tiles_per_seq, x_ref, kvn_ref, bn_ref, wkt_ref, wv_ref, wf_ref, bf_ref,
                 wq_ref, wg_ref, gq_ref, gk_ref, q_ref, kt_ref, v_ref, sg_ref, carry_scr):
    i = pl.program_id(0)
    x = x_ref[...]
    hkv = _rms(x, kvn_ref[...]).astype(bf16)
    hq = _rms(x, bn_ref[...]).astype(bf16)

    v_ref[...] = _dot(hkv, wv_ref[...]).astype(bf16)
    sg_ref[...] = jax.nn.sigmoid(_dot(hq, wg_ref[...])).astype(bf16)

    logf = _log_sigmoid(_dot(hkv, wf_ref[...]) + bf_ref[...])
    tm = x.shape[0]
    incl = (lax.broadcasted_iota(jnp.int32, (tm, tm), 0)
            >= lax.broadcasted_iota(jnp.int32, (tm, tm), 1)).astype(bf16)

    @pl.when(i % tiles_per_seq == 0)
    def _():
        carry_scr[...] = jnp.zeros_like(carry_scr)

    parts = _dot(incl, jnp.concatenate(_split3(logf), axis=1).astype(bf16))
    cum = (parts[:, :LANES] + parts[:, LANES:2 * LANES] + parts[:, 2 * LANES:]
           + carry_scr[...])
    carry_scr[...] = cum[tm - 1:tm, :]
    cum = cum * LOG2E
    cum_t = cum.T

    lane = lax.broadcasted_iota(jnp.int32, (tm, HEAD_PAD), 1)
    sub = lax.broadcasted_iota(jnp.int32, (HEAD_DIM, tm), 0)
    q_raw = _dot(hq, wq_ref[...])
    k_raw_t = _dot_nt(wkt_ref[...], hkv)
    scale = HEAD_DIM ** -0.5 * LOG2E

    for h in range(N_HEADS):
        sl = slice(h * HEAD_PAD, (h + 1) * HEAD_PAD)
        qb = q_raw[:, sl]
        q_ms = jnp.sum(qb * qb, axis=-1, keepdims=True) * (1.0 / HEAD_DIM)
        qn = qb * lax.rsqrt(q_ms + EPS) * gq_ref[...] * scale
        c_hi, c_mid, c_lo = _split3(cum[:, h:h + 1])
        qa = jnp.where(lane < HEAD_DIM, qn,
             jnp.where(lane < AUX_CUM0, 1.0,
             jnp.where(lane == AUX_CUM0, c_hi,
             jnp.where(lane == AUX_CUM0 + 1, c_mid,
             jnp.where(lane == AUX_CUM0 + 2, c_lo, 0.0)))))
        q_ref[:, sl] = qa.astype(bf16)

        kb = k_raw_t[h * HEAD_DIM:(h + 1) * HEAD_DIM, :]
        k_ms = jnp.sum(kb * kb, axis=0, keepdims=True) * (1.0 / HEAD_DIM)
        kn = kb * lax.rsqrt(k_ms + EPS) * gk_ref[...]
        t_hi, t_mid, t_lo = _split3(cum_t[h:h + 1, :])
        aux = jnp.where(sub == 0, -t_hi,
              jnp.where(sub == 1, -t_mid,
              jnp.where(sub == 2, -t_lo,
              jnp.where(sub < 6, 1.0, 0.0))))
        sl = slice(h * HEAD_PAD, (h + 1) * HEAD_PAD)
        ka = jnp.concatenate([kn, aux], axis=0).astype(bf16)
        for t in range(tm // ATT_K):
            kt_ref[t, sl, :] = ka[:, t * ATT_K:(t + 1) * ATT_K]


def _proj(x2d, seq, kv_norm, b_norm, wkt, wv, wf, bfv, wq, wg, gq, gk):
    T = x2d.shape[0]
    n_tiles = T // ROW_TILE
    qw = N_HEADS * HEAD_PAD
    return pl.pallas_call(
        functools.partial(_proj_kernel, seq // ROW_TILE),
        grid=(n_tiles,),
        in_specs=[
            pl.BlockSpec((ROW_TILE, D_MODEL), lambda i: (i, 0)),
            _const_spec((1, D_MODEL)),
            _const_spec((1, D_MODEL)),
            _const_spec((D_MODEL, D_MODEL), single=True),
            _const_spec((D_MODEL, D_MODEL), single=True),
            _const_spec((D_MODEL, LANES)),
            _const_spec((1, LANES)),
            _const_spec((D_MODEL, qw), single=True),
            _const_spec((D_MODEL, D_MODEL), single=True),
            _const_spec((1, HEAD_PAD)),
            _const_spec((HEAD_DIM, 1)),
        ],
        out_specs=[
            pl.BlockSpec((ROW_TILE, qw), lambda i: (i, 0)),
            pl.BlockSpec((ROW_TILE // ATT_K, qw, ATT_K), lambda i: (i, 0, 0)),
            pl.BlockSpec((ROW_TILE, D_MODEL), lambda i: (i, 0)),
            pl.BlockSpec((ROW_TILE, D_MODEL), lambda i: (i, 0)),
        ],
        out_shape=[
            jax.ShapeDtypeStruct((T, qw), bf16),
            jax.ShapeDtypeStruct((T // ATT_K, qw, ATT_K), bf16),
            jax.ShapeDtypeStruct((T, D_MODEL), bf16),
            jax.ShapeDtypeStruct((T, D_MODEL), bf16),
        ],
        scratch_shapes=[pltpu.VMEM((1, LANES), f32)],
        compiler_params=_params(("arbitrary",)),
        name="proj",
    )(x2d, kv_norm, b_norm, wkt, wv, wf, bfv, wq, wg, gq, gk)


def _attn_kernel(q_ref, kt_ref, v_ref, o_ref, s_scr, m_scr, l_scr, acc_scr):
    qi = pl.program_id(2)
    tiles = ATT_Q // ATT_K
    groups = ATT_K // LANES
    heads = [slice(hh * HEAD_PAD, (hh + 1) * HEAD_PAD) for hh in range(2)]
    m_scr[...] = jnp.full(m_scr.shape, -jnp.inf, f32)

    def lane_max(mx, s):
        for g in range(groups):
            mx = jnp.maximum(mx, s[:, g * LANES:(g + 1) * LANES])
        return mx

    def probs(s, mb):
        ps = [jnp.exp2(s[:, g * LANES:(g + 1) * LANES] - mb) for g in range(groups)]
        return ps, functools.reduce(lambda a, b: a + b, ps)

    def body_a(trip, c):
        for t in range(tiles):
            kt_idx = trip * tiles + t
            for hh in range(2):
                s = _dot(q_ref[:, heads[hh]], kt_ref[kt_idx, heads[hh], :])
                s_scr[hh, kt_idx] = s
                m_scr[hh] = lane_max(m_scr[hh], s)
        return c

    lax.fori_loop(0, qi, body_a, 0)
    for t in range(tiles):
        kt_idx = qi * tiles + t
        rows = ATT_Q - t * ATT_K
        visible = (lax.broadcasted_iota(jnp.int32, (rows, ATT_K), 1)
                   <= lax.broadcasted_iota(jnp.int32, (rows, ATT_K), 0))
        for hh in range(2):
            s = _dot(q_ref[t * ATT_K:, heads[hh]], kt_ref[kt_idx, heads[hh], :])
            s = jnp.where(visible, s, -jnp.inf)
            s_scr[hh, kt_idx, t * ATT_K:, :] = s
            m_scr[hh, t * ATT_K:, :] = lane_max(m_scr[hh, t * ATT_K:, :], s)

    for hh in range(2):
        row_max = jnp.max(m_scr[hh], axis=-1, keepdims=True)
        m_scr[hh] = jnp.broadcast_to(row_max, (ATT_Q, LANES))
    l_scr[...] = jnp.zeros(l_scr.shape, f32)
    acc_scr[...] = jnp.zeros(acc_scr.shape, f32)

    def body_b(trip, c):
        for hh in range(2):
            mb = m_scr[hh]
            lsum, acc = l_scr[hh], acc_scr[hh]
            for t0 in range(0, tiles, PV_TILES):
                kt_idx = trip * tiles + t0
                row0 = pl.multiple_of(kt_idx * ATT_K, PV_TILES * ATT_K)
                vb = v_ref[pl.ds(row0, PV_TILES * ATT_K), :]
                parts = []
                for t in range(PV_TILES):
                    ps, psum = probs(s_scr[hh, kt_idx + t], mb)
                    parts += ps
                    lsum = lsum + psum
                acc = acc + _dot(jnp.concatenate(parts, axis=1).astype(bf16), vb)
            l_scr[hh] = lsum
            acc_scr[hh] = acc
        return c

    lax.fori_loop(0, qi, body_b, 0)
    for t in range(tiles):
        kt_idx = qi * tiles + t
        row0 = pl.multiple_of(kt_idx * ATT_K, ATT_K)
        vb = v_ref[pl.ds(row0, ATT_K), :]
        for hh in range(2):
            ps, psum = probs(s_scr[hh, kt_idx, t * ATT_K:, :], m_scr[hh, t * ATT_K:, :])
            l_scr[hh, t * ATT_K:, :] += psum
            acc_scr[hh, t * ATT_K:, :] += _dot(jnp.concatenate(ps, axis=1).astype(bf16), vb)

    lane = lax.broadcasted_iota(jnp.int32, (ATT_Q, 2 * HEAD_DIM), 1)
    o0 = acc_scr[0] / jnp.sum(l_scr[0], axis=-1, keepdims=True)
    o1 = acc_scr[1] / jnp.sum(l_scr[1], axis=-1, keepdims=True)
    o_ref[...] = jnp.where(lane < HEAD_DIM, o0, o1).astype(bf16)


def _attention(q, kt, v, batch, seq):
    T = q.shape[0]
    nq = seq // ATT_Q
    nk = seq // ATT_K
    pairs = N_HEADS // 2
    return pl.pallas_call(
        _attn_kernel,
        grid=(batch, pairs, nq),
        in_specs=[
            pl.BlockSpec((ATT_Q, 2 * HEAD_PAD), lambda b, p, i: (b * nq + i, p)),
            pl.BlockSpec((nk, 2 * HEAD_PAD, ATT_K), lambda b, p, i: (b, p, 0)),
            pl.BlockSpec((seq, 2 * HEAD_DIM), lambda b, p, i: (b, p)),
        ],
        out_specs=pl.BlockSpec((ATT_Q, 2 * HEAD_DIM), lambda b, p, i: (b * nq + i, p)),
        out_shape=jax.ShapeDtypeStruct((T, D_MODEL), bf16),
        scratch_shapes=[
            pltpu.VMEM((2, nk, ATT_Q, ATT_K), f32),
            pltpu.VMEM((2, ATT_Q, LANES), f32),
            pltpu.VMEM((2, ATT_Q, LANES), f32),
            pltpu.VMEM((2, ATT_Q, 2 * HEAD_DIM), f32),
        ],
        compiler_params=_params(("parallel", "parallel", "parallel")),
        name="attn",
    )(q, kt, v)


def _attn_out_kernel(x_ref, o_ref, sg_ref, wo_ref, mn_ref, wr_ref, x3_ref, lg_ref):
    gated = (o_ref[...].astype(f32) * sg_ref[...].astype(f32)).astype(bf16)
    x3 = x_ref[...] + _dot(gated, wo_ref[...])
    x3_ref[...] = x3
    lg_ref[...] = _router_logits_t(x3, mn_ref[...], wr_ref[...])


def _attn_out(x2d, o, sg, wo, m_norm, w_router):
    T = x2d.shape[0]
    return pl.pallas_call(
        _attn_out_kernel,
        grid=(T // SGU_TILE,),
        in_specs=[
            pl.BlockSpec((SGU_TILE, D_MODEL), lambda i: (i, 0)),
            pl.BlockSpec((SGU_TILE, D_MODEL), lambda i: (i, 0)),
            pl.BlockSpec((SGU_TILE, D_MODEL), lambda i: (i, 0)),
            _const_spec((D_MODEL, D_MODEL)),
            _const_spec((1, D_MODEL)),
            _const_spec((2 * D_MODEL, 2 * LANES)),
        ],
        out_specs=[
            pl.BlockSpec((SGU_TILE, D_MODEL), lambda i: (i, 0)),
            pl.BlockSpec((ROUTER_ROWS, SGU_TILE), lambda i: (0, i)),
        ],
        out_shape=[
            jax.ShapeDtypeStruct((T, D_MODEL), f32),
            jax.ShapeDtypeStruct((ROUTER_ROWS, T), f32),
        ],
        compiler_params=_params(("parallel",)),
        name="attn_out",
    )(x2d, o, sg, wo, m_norm, w_router)


def _router_weight(w_group, w_expert):
    pad = jnp.zeros((D_MODEL, LANES - N_EXPERTS - N_GROUPS), f32)
    w = jnp.concatenate([w_expert, w_group, pad], axis=1)
    w_hi = w.astype(bf16)
    w_lo = (w - w_hi.astype(f32)).astype(bf16)
    top = jnp.concatenate([w_hi, w_lo], axis=1)
    bottom = jnp.concatenate([w_hi, jnp.zeros_like(w_hi)], axis=1)
    return jnp.concatenate([top, bottom], axis=0)


def kernel(x, a_norm, a_w_in, a_b_in, a_v_norm, a_w_s, a_b_s, a_w_out, kv_norm, kv_w, kv_b_f,
           k_norm, b_norm, b_w_qg, q_norm, b_w_out, m_norm, m_w_group, m_b_group, m_w_expert,
           m_b_expert, m_w1, m_w3, m_w2):
    batch, seq, _ = x.shape
    T = batch * seq
    x2d = x.reshape(T, D_MODEL)

    x1, lg0 = _sgu(
        x2d, a_norm[0].reshape(1, -1), a_w_in[0].astype(bf16), a_b_in[0].reshape(1, -1),
        a_v_norm[0].reshape(1, -1), a_w_s[0], a_b_s[0].T, a_w_out[0].astype(bf16),
        m_norm[0].reshape(1, -1), _router_weight(m_w_group[0], m_w_expert[0]))
    x2 = _moe(0, x1, lg0, m_norm[0].reshape(1, -1), m_b_group[0], m_b_expert[0],
              m_w1, m_w3, m_w2)

    wk = kv_w[:, :D_MODEL]
    wv = kv_w[:, D_MODEL:2 * D_MODEL]
    wf = jnp.pad(kv_w[:, 2 * D_MODEL:], ((0, 0), (0, LANES - N_HEADS)))
    bfv = jnp.pad(kv_b_f, (0, LANES - N_HEADS)).reshape(1, LANES)
    wq = b_w_qg[0][:, :D_MODEL]
    wg = b_w_qg[0][:, D_MODEL:]
    gq = jnp.pad(q_norm[0], (0, HEAD_PAD - HEAD_DIM)).reshape(1, HEAD_PAD)
    gk = k_norm.reshape(HEAD_DIM, 1)
    wq_pad = jnp.pad(wq.reshape(D_MODEL, N_HEADS, HEAD_DIM),
                     ((0, 0), (0, 0), (0, HEAD_PAD - HEAD_DIM))).reshape(D_MODEL, -1)
    q, kt, v, sg = _proj(
        x2, seq, kv_norm.reshape(1, -1), b_norm[0].reshape(1, -1),
        wk.T.astype(bf16), wv.astype(bf16), wf.astype(bf16), bfv,
        wq_pad.astype(bf16), wg.astype(bf16), gq, gk)
    o = _attention(q, kt, v, batch, seq)
    x3, lg1 = _attn_out(x2, o, sg, b_w_out[0].astype(bf16), m_norm[1].reshape(1, -1),
                        _router_weight(m_w_group[1], m_w_expert[1]))
    x4 = _moe(1, x3, lg1, m_norm[1].reshape(1, -1), m_b_group[1], m_b_expert[1],
              m_w1, m_w3, m_w2)
    return x4.reshape(batch, seq, D_MODEL)
```

```python
import functools
import math

import jax
import jax.numpy as jnp
import numpy as np
from jax import lax
from jax.experimental import pallas as pl
from jax.experimental.pallas import tpu as pltpu

D_MODEL = 1024
EPS = 1e-6
SGU_BLOCK = 128
SGU_CHUNK = 64
CHUNK_SHIFT = SGU_CHUNK.bit_length() - 1
SGU_WIDTH = 2 * D_MODEL
SGU_GROUPS = 8
SGU_GROUP_DIM = SGU_WIDTH // SGU_GROUPS
N_HEADS = 16
HEAD_DIM = D_MODEL // N_HEADS
N_GROUPS = 4
EXPERTS_PER_GROUP = 8
N_EXPERTS = N_GROUPS * EXPERTS_PER_GROUP
GROUP_SHIFT = EXPERTS_PER_GROUP.bit_length() - 1
TOP_K = 2
D_EXPERT = D_MODEL // 2

LANES = 128
ROW_SUBLANES = D_MODEL // 2 // LANES
ISSUE_UNROLL = 8
VMEM_LIMIT_BYTES = 56 * 1024 * 1024

ROW_TILE = 256
SGU_TILE = 512
ROUTE_TILE = 1024
EXPERT_ROWS = 512
MOVE_TILE = 512
MOVE_CHUNK = 64
ATT_Q = 1024
ATT_K = 256
PV_TILES = 2
HEAD_PAD = 2 * HEAD_DIM
ROUTER_ROWS = 40
AUX_ONE0 = HEAD_DIM
AUX_CUM0 = HEAD_DIM + 3

LOG2E = math.log2(math.e)

bf16 = jnp.bfloat16
f32 = jnp.float32
u32 = jnp.uint32
HI_HALF = np.uint32(0xFFFF0000)


def _dot(a, b, precision=None):
    return jnp.dot(a, b, preferred_element_type=f32, precision=precision)


def _dot_nt(a, b, precision=None):
    return lax.dot_general(a, b, (((1,), (1,)), ((), ())), preferred_element_type=f32,
                           precision=precision)


def _rms(x, g):
    ms = jnp.mean(x * x, axis=-1, keepdims=True)
    return x * lax.rsqrt(ms + EPS) * g


def _gelu_tanh(z):
    c = math.sqrt(2.0 / math.pi)
    return z * (0.5 * (1.0 + jnp.tanh(c * (z + 0.044715 * (z * z * z)))))


def _split3(c):
    hi = c.astype(bf16).astype(f32)
    r = c - hi
    mid = r.astype(bf16).astype(f32)
    lo = r - mid
    return hi, mid, lo


def _pack_rows(v):
    half = D_MODEL // 2
    lo = lax.bitcast_convert_type(v[:, :half].astype(bf16).astype(f32), u32) >> 16
    hi = lax.bitcast_convert_type(v[:, half:].astype(bf16).astype(f32), u32) & HI_HALF
    return (hi | lo).reshape(v.shape[0], ROW_SUBLANES, LANES)


def _unpack_rows(w):
    w = w.reshape(w.shape[0], D_MODEL // 2)
    lo = lax.bitcast_convert_type(w << 16, f32)
    hi = lax.bitcast_convert_type(w & HI_HALF, f32)
    return jnp.concatenate([lo, hi], axis=1)


def _const_spec(shape, single=False):
    nd = len(shape)
    mode = pl.Buffered(1) if single else None
    return pl.BlockSpec(shape, lambda *_: (0,) * nd, pipeline_mode=mode)


def _params(sem):
    return pltpu.CompilerParams(dimension_semantics=sem, vmem_limit_bytes=VMEM_LIMIT_BYTES)


def _router_logits_t(x_new, mn, wr3):
    hm = _rms(x_new, mn)
    h_hi = hm.astype(bf16)
    h_lo = (hm - h_hi.astype(f32)).astype(bf16)
    both = _dot(jnp.concatenate([h_hi, h_lo], axis=1), wr3)
    lg = both[:, :LANES] + both[:, LANES:]
    return lg.T[:ROUTER_ROWS, :]


def _sgu_kernel(x_ref, an_ref, win_ref, bin_ref, vn_ref, ws_ref, bst_ref, wout_ref, mn_ref,
                wr_ref, x1_ref, lg_ref, u_scr, v_scr, gated_scr):
    x = x_ref[...]
    h = _rms(x, an_ref[...]).astype(bf16)
    v = _gelu_tanh(_dot(h, win_ref[:, SGU_WIDTH:]) + bin_ref[:, SGU_WIDTH:])
    v_scr[...] = _rms(v, vn_ref[...]).astype(bf16)
    u_scr[...] = _gelu_tanh(_dot(h, win_ref[:, :SGU_WIDTH]) + bin_ref[:, :SGU_WIDTH])

    t_chunk = lax.broadcasted_iota(jnp.int32, (SGU_BLOCK, SGU_BLOCK), 0) >> CHUNK_SHIFT
    s_chunk = lax.broadcasted_iota(jnp.int32, (SGU_BLOCK, SGU_BLOCK), 1) >> CHUNK_SHIFT
    causal = t_chunk >= s_chunk
    for g in range(SGU_GROUPS):
        wsm = jnp.where(causal, ws_ref[g], 0.0).astype(bf16)
        cols = slice(g * SGU_GROUP_DIM, (g + 1) * SGU_GROUP_DIM)
        for sb in range(SGU_TILE // SGU_BLOCK):
            rows = slice(sb * SGU_BLOCK, (sb + 1) * SGU_BLOCK)
            mixed = _dot(wsm, v_scr[rows, cols]) + bst_ref[:, g:g + 1]
            gated_scr[rows, cols] = (u_scr[rows, cols] * mixed).astype(bf16)

    x1 = x + _dot(gated_scr[...], wout_ref[...])
    x1_ref[...] = x1
    lg_ref[...] = _router_logits_t(x1, mn_ref[...], wr_ref[...])


def _sgu(x2d, a_norm, w_in, b_in, v_norm, w_s, b_s_t, w_out, m_norm, w_router):
    T = x2d.shape[0]
    return pl.pallas_call(
        _sgu_kernel,
        grid=(T // SGU_TILE,),
        in_specs=[
            pl.BlockSpec((SGU_TILE, D_MODEL), lambda i: (i, 0)),
            _const_spec((1, D_MODEL)),
            _const_spec((D_MODEL, 2 * SGU_WIDTH), single=True),
            _const_spec((1, 2 * SGU_WIDTH)),
            _const_spec((1, SGU_WIDTH)),
            _const_spec((SGU_GROUPS, SGU_BLOCK, SGU_BLOCK)),
            _const_spec((SGU_BLOCK, SGU_GROUPS)),
            _const_spec((SGU_WIDTH, D_MODEL), single=True),
            _const_spec((1, D_MODEL)),
            _const_spec((2 * D_MODEL, 2 * LANES)),
        ],
        out_specs=[
            pl.BlockSpec((SGU_TILE, D_MODEL), lambda i: (i, 0)),
            pl.BlockSpec((ROUTER_ROWS, SGU_TILE), lambda i: (0, i)),
        ],
        out_shape=[
            jax.ShapeDtypeStruct((T, D_MODEL), f32),
            jax.ShapeDtypeStruct((ROUTER_ROWS, T), f32),
        ],
        scratch_shapes=[
            pltpu.VMEM((SGU_TILE, SGU_WIDTH), f32),
            pltpu.VMEM((SGU_TILE, SGU_WIDTH), bf16),
            pltpu.VMEM((SGU_TILE, SGU_WIDTH), bf16),
        ],
        compiler_params=_params(("parallel",)),
        name="sgu",
    )(x2d, a_norm, w_in, b_in, v_norm, w_s, b_s_t, w_out, m_norm, w_router)


def _route_kernel(lg_ref, be_ref, bg_ref, oi_ref, ow_ref, cnt_ref, carry_scr, before_scr):
    step = pl.program_id(0)
    tt = lg_ref.shape[1]

    @pl.when(step == 0)
    def _():
        carry_scr[...] = jnp.zeros_like(carry_scr)
        before_scr[...] = (lax.broadcasted_iota(jnp.int32, (tt, tt), 0)
                           < lax.broadcasted_iota(jnp.int32, (tt, tt), 1)).astype(bf16)

    lg = lg_ref[...]
    e_l = lg[0:N_EXPERTS, :] + be_ref[...]
    g_l = lg[N_EXPERTS:N_EXPERTS + 8, :] + bg_ref[...]
    g_row = lax.broadcasted_iota(jnp.int32, g_l.shape, 0).astype(f32)
    g_l = jnp.where(g_row < N_GROUPS, g_l, -jnp.inf)
    g_max = jnp.max(g_l, axis=0, keepdims=True)
    g_sel = jnp.min(jnp.where(g_l == g_max, g_row, 8.0), axis=0, keepdims=True)
    g_den = jnp.sum(jnp.exp(g_l - g_max), axis=0, keepdims=True)
    g_w = 1.0 / g_den

    e_row_i = lax.broadcasted_iota(jnp.int32, e_l.shape, 0)
    e_row = e_row_i.astype(f32)
    e_grp = (e_row_i >> GROUP_SHIFT).astype(f32)
    e_in = jnp.where(e_grp == g_sel, e_l, -jnp.inf)
    m1 = jnp.max(e_in, axis=0, keepdims=True)
    i1 = jnp.min(jnp.where(e_in == m1, e_row, float(N_EXPERTS)), axis=0, keepdims=True)
    e_in2 = jnp.where(e_row == i1, -jnp.inf, e_in)
    m2 = jnp.max(e_in2, axis=0, keepdims=True)
    i2 = jnp.min(jnp.where(e_in2 == m2, e_row, float(N_EXPERTS)), axis=0, keepdims=True)
    t = jnp.exp(m2 - m1)
    p1 = 1.0 / (1.0 + t)
    p2 = t / (1.0 + t)

    hit1 = e_row == i1
    hit2 = e_row == i2
    member = jnp.logical_or(hit1, hit2)
    prefix = _dot(member.astype(bf16), before_scr[...])
    rank_full = prefix + carry_scr[...]
    rank1 = jnp.sum(jnp.where(hit1, rank_full, 0.0), axis=0, keepdims=True)
    rank2 = jnp.sum(jnp.where(hit2, rank_full, 0.0), axis=0, keepdims=True)
    carry_new = carry_scr[...] + jnp.sum(member.astype(f32), axis=1, keepdims=True)
    carry_scr[...] = carry_new

    oi_ref[...] = jnp.zeros(oi_ref.shape, jnp.int32)
    oi_ref[0:1, :] = i1.astype(jnp.int32)
    oi_ref[1:2, :] = i2.astype(jnp.int32)
    oi_ref[2:3, :] = rank1.astype(jnp.int32)
    oi_ref[3:4, :] = rank2.astype(jnp.int32)
    ow_ref[...] = jnp.zeros(ow_ref.shape, f32)
    ow_ref[0:1, :] = g_w * p1
    ow_ref[1:2, :] = g_w * p2
    cnt_ref[...] = jnp.broadcast_to(carry_new, cnt_ref.shape)


def _route(logits_t, b_expert, b_group):
    T = logits_t.shape[1]
    return pl.pallas_call(
        _route_kernel,
        grid=(T // ROUTE_TILE,),
        in_specs=[
            pl.BlockSpec((ROUTER_ROWS, ROUTE_TILE), lambda i: (0, i)),
            _const_spec((N_EXPERTS, 1)),
            _const_spec((8, 1)),
        ],
        out_specs=[
            pl.BlockSpec((8, ROUTE_TILE), lambda i: (0, i)),
            pl.BlockSpec((8, ROUTE_TILE), lambda i: (0, i)),
            _const_spec((N_EXPERTS, LANES)),
        ],
        out_shape=[
            jax.ShapeDtypeStruct((8, T), jnp.int32),
            jax.ShapeDtypeStruct((8, T), f32),
            jax.ShapeDtypeStruct((N_EXPERTS, LANES), f32),
        ],
        scratch_shapes=[pltpu.VMEM((N_EXPERTS, 1), f32),
                        pltpu.VMEM((ROUTE_TILE, ROUTE_TILE), bf16)],
        compiler_params=_params(("arbitrary",)),
        name="route",
    )(logits_t, b_expert, b_group)


def _row_copy(src, s, dst, d, sem):
    return pltpu.make_async_copy(src.at[s], dst.at[d], sem)


def _wait_rows(buf, sem):
    pltpu.make_async_copy(buf, buf, sem).wait()


def _dispatch_kernel(n_blocks, pend_ref, padded_ref, nu_ref, dest_ref, x_ref, mn_ref, xs_hbm,
                     zero_scr, x3_scr, zsem, sem):
    @pl.when(pl.program_id(0) == 0)
    def _():
        zero_scr[...] = jnp.zeros_like(zero_scr)

        def fill(row0):
            return pltpu.make_async_copy(zero_scr, xs_hbm.at[pl.ds(row0, EXPERT_ROWS)], zsem)

        for e in range(N_EXPERTS):
            @pl.when(padded_ref[e] > 0)
            def _():
                fill(pend_ref[e] - EXPERT_ROWS).start()

            @pl.when(nu_ref[0] + e < n_blocks)
            def _():
                fill((nu_ref[0] + e) * EXPERT_ROWS).start()

        for e in range(N_EXPERTS):
            @pl.when(padded_ref[e] > 0)
            def _():
                fill(0).wait()

            @pl.when(nu_ref[0] + e < n_blocks)
            def _():
                fill(0).wait()

    i = pl.program_id(0)
    n_tiles = pl.num_programs(0) - 1

    def staged(rows):
        return _pack_rows(_rms(x_ref[rows, :], mn_ref[...]))

    @pl.when(i == 0)
    def _():
        x3_scr[0] = staged(slice(None))

    @pl.when(i > 0)
    def _():
        cur = (i - 1) % 3
        nxt = i % 3

        def chunk(c, carry):
            r0 = pl.multiple_of(c * MOVE_CHUNK, MOVE_CHUNK)
            for j in range(MOVE_CHUNK):
                for k in range(TOP_K):
                    _row_copy(x3_scr.at[cur], r0 + j, xs_hbm,
                              dest_ref[0, k * MOVE_TILE + r0 + j], sem.at[cur]).start(priority=k)
            x3_scr[nxt, pl.ds(r0, MOVE_CHUNK)] = staged(pl.ds(r0, MOVE_CHUNK))
            return carry

        lax.fori_loop(0, MOVE_TILE // MOVE_CHUNK, chunk, 0)

        @pl.when(i > 1)
        def _():
            for k in range(TOP_K):
                _wait_rows(x3_scr.at[(i - 2) % 3], sem.at[(i - 2) % 3])

        @pl.when(i == n_tiles)
        def _():
            for k in range(TOP_K):
                _wait_rows(x3_scr.at[cur], sem.at[cur])


def _dispatch(pend, padded, n_used, dest_tiles, x2d, m_norm, n_rows):
    T = x2d.shape[0]
    n_blocks = n_rows // EXPERT_ROWS
    n_tiles = T // MOVE_TILE
    grid_spec = pltpu.PrefetchScalarGridSpec(
        num_scalar_prefetch=3,
        grid=(n_tiles + 1,),
        in_specs=[
            pl.BlockSpec((None, 1, TOP_K * MOVE_TILE),
                         lambda i, *_: (jnp.maximum(i - 1, 0), 0, 0), memory_space=pltpu.SMEM),
            pl.BlockSpec((MOVE_TILE, D_MODEL), lambda i, *_: (jnp.minimum(i, n_tiles - 1), 0)),
            pl.BlockSpec((1, D_MODEL), lambda i, *_: (0, 0)),
        ],
        out_specs=pl.BlockSpec(memory_space=pl.ANY),
        scratch_shapes=[
            pltpu.VMEM((EXPERT_ROWS, ROW_SUBLANES, LANES), u32),
            pltpu.VMEM((3, MOVE_TILE, ROW_SUBLANES, LANES), u32),
            pltpu.SemaphoreType.DMA(()),
            pltpu.SemaphoreType.DMA((3,)),
        ],
    )
    return pl.pallas_call(
        functools.partial(_dispatch_kernel, n_blocks),
        grid_spec=grid_spec,
        out_shape=jax.ShapeDtypeStruct((n_rows, ROW_SUBLANES, LANES), u32),
        compiler_params=_params(("arbitrary",)),
        name="dispatch",
    )(pend, padded, n_used, dest_tiles, x2d, m_norm)


def _expert_kernel(n_blocks, ps_ref, nb_ref, nu_ref, xs_hbm, w1_ref, w3_ref, w2_ref,
                   ys_hbm, xbuf, ybuf, h_scr, w1_scr, w3_scr, w2_scr, xsem, ysem):
    e = pl.program_id(0)
    nblk = nb_ref[e]
    g0 = ps_ref[e]
    n_used = nu_ref[0]

    def rows(g):
        return pl.ds(pl.multiple_of(g * EXPERT_ROWS, EXPERT_ROWS), EXPERT_ROWS)

    def x_copy(g):
        return pltpu.make_async_copy(xs_hbm.at[rows(g)], xbuf.at[g % 2], xsem.at[g % 2])

    def y_copy(g):
        return pltpu.make_async_copy(ybuf.at[g % 2], ys_hbm.at[rows(g)], ysem.at[g % 2])

    def normed(slot):
        return _unpack_rows(xbuf[slot]).astype(bf16)

    @pl.when(e == 0)
    def _():
        x_copy(0).start(priority=1)
        x_copy(1).start(priority=1)
        x_copy(0).wait()
        h_scr[...] = normed(0)

    @pl.when(nblk > 0)
    def _():
        w1_scr[...] = w1_ref[...].astype(bf16)
        w3_scr[...] = w3_ref[...].astype(bf16)
        w2_scr[...] = w2_ref[...].astype(bf16)

        def body(g, h):
            @pl.when(g + 2 < n_used)
            def _():
                x_copy(g + 2).start(priority=1)

            @pl.when(g + 1 < n_used)
            def _():
                x_copy(g + 1).wait()

            @pl.when(g >= 2)
            def _():
                y_copy(g - 2).wait()

            h_next = normed((g + 1) % 2)
            a = _dot(h, w1_scr[...])
            c = _dot(h, w3_scr[...])
            hid = (a * jax.nn.sigmoid(a)) * c
            ybuf[g % 2] = _pack_rows(_dot(hid.astype(bf16), w2_scr[...]))
            y_copy(g).start(priority=1)
            return h_next

        h_scr[...] = lax.fori_loop(g0, g0 + nblk, body, h_scr[...])

    @pl.when(e == pl.num_programs(0) - 1)
    def _():
        y_copy(n_used - 2).wait()
        y_copy(n_used - 1).wait()
        ybuf[0] = jnp.zeros(ybuf.shape[1:], u32)

        def fill(b):
            r = pl.ds(pl.multiple_of(b * EXPERT_ROWS, EXPERT_ROWS), EXPERT_ROWS)
            return pltpu.make_async_copy(ybuf.at[0], ys_hbm.at[r], ysem.at[0])

        def start(b, c):
            fill(b).start()
            return c

        def wait(b, c):
            fill(b).wait()
            return c

        lax.fori_loop(nu_ref[0], n_blocks, start, 0)
        lax.fori_loop(nu_ref[0], n_blocks, wait, 0)


def _experts(layer, pstart, seg_blocks, n_used, xs, w1, w3, w2):
    n_rows = xs.shape[0]
    n_blocks = n_rows // EXPERT_ROWS

    def w_map(e, *_):
        return (layer, e, 0, 0)

    block_buf = pltpu.VMEM((2, EXPERT_ROWS, ROW_SUBLANES, LANES), u32)
    grid_spec = pltpu.PrefetchScalarGridSpec(
        num_scalar_prefetch=3,
        grid=(N_EXPERTS,),
        in_specs=[
            pl.BlockSpec(memory_space=pl.ANY),
            pl.BlockSpec((None, None, D_MODEL, D_EXPERT), w_map),
            pl.BlockSpec((None, None, D_MODEL, D_EXPERT), w_map),
            pl.BlockSpec((None, None, D_EXPERT, D_MODEL), w_map),
        ],
        out_specs=pl.BlockSpec(memory_space=pl.ANY),
        scratch_shapes=[
            block_buf,
            block_buf,
            pltpu.VMEM((EXPERT_ROWS, D_MODEL), bf16),
            pltpu.VMEM((D_MODEL, D_EXPERT), bf16),
            pltpu.VMEM((D_MODEL, D_EXPERT), bf16),
            pltpu.VMEM((D_EXPERT, D_MODEL), bf16),
            pltpu.SemaphoreType.DMA((2,)),
            pltpu.SemaphoreType.DMA((2,)),
        ],
    )
    return pl.pallas_call(
        functools.partial(_expert_kernel, n_blocks),
        grid_spec=grid_spec,
        out_shape=jax.ShapeDtypeStruct((n_rows, ROW_SUBLANES, LANES), u32),
        compiler_params=_params(("arbitrary",)),
        name="experts",
    )(pstart, seg_blocks, n_used, xs, w1, w3, w2)


def _combine_kernel(dest_ref, next_ref, x_ref, w_ref, ys_hbm, out_ref, buf, sem):
    i = pl.program_id(0)
    slot = i % 2

    def gather(idx_ref, s):
        def start(j, c):
            for k in range(TOP_K):
                _row_copy(ys_hbm, idx_ref[0, k * MOVE_TILE + j], buf.at[s, k], j,
                          sem.at[s]).start(priority=k)
            return c

        lax.fori_loop(0, MOVE_TILE, start, 0, unroll=ISSUE_UNROLL)

    @pl.when(i == 0)
    def _():
        gather(dest_ref, slot)

    @pl.when(i + 1 < pl.num_programs(0))
    def _():
        gather(next_ref, 1 - slot)

    for k in range(TOP_K):
        _wait_rows(buf.at[slot, k], sem.at[slot])
    w = w_ref[...]
    y0 = _unpack_rows(buf[slot, 0])
    y1 = _unpack_rows(buf[slot, 1])
    out_ref[...] = x_ref[...] + (w[:, 0:1] * y0 + w[:, 1:2] * y1)


def _combine(dest_tiles, x2d, w_cols, ys):
    T = x2d.shape[0]
    n_tiles = T // MOVE_TILE
    return pl.pallas_call(
        _combine_kernel,
        grid=(n_tiles,),
        in_specs=[
            pl.BlockSpec((None, 1, TOP_K * MOVE_TILE), lambda i: (i, 0, 0),
                         memory_space=pltpu.SMEM),
            pl.BlockSpec((None, 1, TOP_K * MOVE_TILE),
                         lambda i: (jnp.minimum(i + 1, n_tiles - 1), 0, 0),
                         memory_space=pltpu.SMEM),
            pl.BlockSpec((MOVE_TILE, D_MODEL), lambda i: (i, 0)),
            pl.BlockSpec((MOVE_TILE, 8), lambda i: (i, 0)),
            pl.BlockSpec(memory_space=pl.ANY),
        ],
        out_specs=pl.BlockSpec((MOVE_TILE, D_MODEL), lambda i: (i, 0)),
        out_shape=jax.ShapeDtypeStruct((T, D_MODEL), f32),
        scratch_shapes=[
            pltpu.VMEM((2, TOP_K, MOVE_TILE, ROW_SUBLANES, LANES), u32),
            pltpu.SemaphoreType.DMA((2,)),
        ],
        compiler_params=_params(("arbitrary",)),
        name="combine",
    )(dest_tiles, dest_tiles, x2d, w_cols, ys)


def _moe(layer, x2d, logits_t, m_norm, b_group, b_expert, w1, w3, w2):
    T = x2d.shape[0]
    n_blocks = (T * TOP_K) // EXPERT_ROWS + N_EXPERTS
    n_rows = n_blocks * EXPERT_ROWS

    be = b_expert.reshape(N_EXPERTS, 1)
    bg = jnp.concatenate([b_group, jnp.zeros((8 - N_GROUPS,), f32)]).reshape(8, 1)
    oi, ow, cnt = _route(logits_t, be, bg)

    counts = cnt[:, 0].astype(jnp.int32)
    padded = (counts + EXPERT_ROWS - 1) // EXPERT_ROWS * EXPERT_ROWS
    pend = jnp.cumsum(padded)
    pstart = pend - padded
    n_used = (pend[-1] // EXPERT_ROWS).astype(jnp.int32).reshape(1)
    e_ids = jnp.arange(N_EXPERTS, dtype=jnp.int32)[:, None, None]
    seg_start = jnp.sum(jnp.where(oi[None, 0:2] == e_ids, pstart[:, None, None], 0), axis=0)
    dest = seg_start + oi[2:4]
    dest_tiles = dest.reshape(TOP_K, T // MOVE_TILE, MOVE_TILE).transpose(1, 0, 2).reshape(
        T // MOVE_TILE, 1, TOP_K * MOVE_TILE)
    w_cols = ow.T

    xs = _dispatch(pend.astype(jnp.int32), padded, n_used, dest_tiles, x2d, m_norm, n_rows)
    ys = _experts(layer, (pstart // EXPERT_ROWS).astype(jnp.int32), padded // EXPERT_ROWS,
                  n_used, xs, w1, w3, w2)
    return _combine(dest_tiles, x2d, w_cols, ys)


def _log_sigmoid(x):
    return jnp.minimum(x, 0.0) - jnp.log1p(jnp.exp(-jnp.abs(x)))


def _proj_kernel(tiles_per_seq, x_ref, kvn_ref, bn_ref, wkt_ref, wv_ref, wf_ref, bf_ref,
                 wq_ref, wg_ref, gq_ref, gk_ref, q_ref, kt_ref, v_ref, sg_ref, carry_scr):
    i = pl.program_id(0)
    x = x_ref[...]
    hkv = _rms(x, kvn_ref[...]).astype(bf16)
    hq = _rms(x, bn_ref[...]).astype(bf16)

    v_ref[...] = _dot(hkv, wv_ref[...]).astype(bf16)
    sg_ref[...] = jax.nn.sigmoid(_dot(hq, wg_ref[...])).astype(bf16)

    logf = _log_sigmoid(_dot(hkv, wf_ref[...]) + bf_ref[...])
    tm = x.shape[0]
    incl = (lax.broadcasted_iota(jnp.int32, (tm, tm), 0)
            >= lax.broadcasted_iota(jnp.int32, (tm, tm), 1)).astype(bf16)

    @pl.when(i % tiles_per_seq == 0)
    def _():
        carry_scr[...] = jnp.zeros_like(carry_scr)

    parts = _dot(incl, jnp.concatenate(_split3(logf), axis=1).astype(bf16))
    cum = (parts[:, :LANES] + parts[:, LANES:2 * LANES] + parts[:, 2 * LANES:]
           + carry_scr[...])
    carry_scr[...] = cum[tm - 1:tm, :]
    cum = cum * LOG2E
    cum_t = cum.T

    lane = lax.broadcasted_iota(jnp.int32, (tm, HEAD_PAD), 1)
    sub = lax.broadcasted_iota(jnp.int32, (HEAD_DIM, tm), 0)
    q_raw = _dot(hq, wq_ref[...])
    k_raw_t = _dot_nt(wkt_ref[...], hkv)
    scale = HEAD_DIM ** -0.5 * LOG2E

    for h in range(N_HEADS):
        sl = slice(h * HEAD_PAD, (h + 1) * HEAD_PAD)
        qb = q_raw[:, sl]
        q_ms = jnp.sum(qb * qb, axis=-1, keepdims=True) * (1.0 / HEAD_DIM)
        qn = qb * lax.rsqrt(q_ms + EPS) * gq_ref[...] * scale
        c_hi, c_mid, c_lo = _split3(cum[:, h:h + 1])
        qa = jnp.where(lane < HEAD_DIM, qn,
             jnp.where(lane < AUX_CUM0, 1.0,
             jnp.where(lane == AUX_CUM0, c_hi,
             jnp.where(lane == AUX_CUM0 + 1, c_mid,
             jnp.where(lane == AUX_CUM0 + 2, c_lo, 0.0)))))
        q_ref[:, sl] = qa.astype(bf16)

        kb = k_raw_t[h * HEAD_DIM:(h + 1) * HEAD_DIM, :]
        k_ms = jnp.sum(kb * kb, axis=0, keepdims=True) * (1.0 / HEAD_DIM)
        kn = kb * lax.rsqrt(k_ms + EPS) * gk_ref[...]
        t_hi, t_mid, t_lo = _split3(cum_t[h:h + 1, :])
        aux = jnp.where(sub == 0, -t_hi,
              jnp.where(sub == 1, -t_mid,
              jnp.where(sub == 2, -t_lo,
              jnp.where(sub < 6, 1.0, 0.0))))
        sl = slice(h * HEAD_PAD, (h + 1) * HEAD_PAD)
        ka = jnp.concatenate([kn, aux], axis=0).astype(bf16)
        for t in range(tm // ATT_K):
            kt_ref[t, sl, :] = ka[:, t * ATT_K:(t + 1) * ATT_K]


def _proj(x2d, seq, kv_norm, b_norm, wkt, wv, wf, bfv, wq, wg, gq, gk):
    T = x2d.shape[0]
    n_tiles = T // ROW_TILE
    qw = N_HEADS * HEAD_PAD
    return pl.pallas_call(
        functools.partial(_proj_kernel, seq // ROW_TILE),
        grid=(n_tiles,),
        in_specs=[
            pl.BlockSpec((ROW_TILE, D_MODEL), lambda i: (i, 0)),
            _const_spec((1, D_MODEL)),
            _const_spec((1, D_MODEL)),
            _const_spec((D_MODEL, D_MODEL), single=True),
            _const_spec((D_MODEL, D_MODEL), single=True),
            _const_spec((D_MODEL, LANES)),
            _const_spec((1, LANES)),
            _const_spec((D_MODEL, qw), single=True),
            _const_spec((D_MODEL, D_MODEL), single=True),
            _const_spec((1, HEAD_PAD)),
            _const_spec((HEAD_DIM, 1)),
        ],
        out_specs=[
            pl.BlockSpec((ROW_TILE, qw), lambda i: (i, 0)),
            pl.BlockSpec((ROW_TILE // ATT_K, qw, ATT_K), lambda i: (i, 0, 0)),
            pl.BlockSpec((ROW_TILE, D_MODEL), lambda i: (i, 0)),
            pl.BlockSpec((ROW_TILE, D_MODEL), lambda i: (i, 0)),
        ],
        out_shape=[
            jax.ShapeDtypeStruct((T, qw), bf16),
            jax.ShapeDtypeStruct((T // ATT_K, qw, ATT_K), bf16),
            jax.ShapeDtypeStruct((T, D_MODEL), bf16),
            jax.ShapeDtypeStruct((T, D_MODEL), bf16),
        ],
        scratch_shapes=[pltpu.VMEM((1, LANES), f32)],
        compiler_params=_params(("arbitrary",)),
        name="proj",
    )(x2d, kv_norm, b_norm, wkt, wv, wf, bfv, wq, wg, gq, gk)


def _attn_kernel(q_ref, kt_ref, v_ref, o_ref, s_scr, m_scr, l_scr, acc_scr):
    qi = pl.program_id(2)
    tiles = ATT_Q // ATT_K
    groups = ATT_K // LANES
    heads = [slice(hh * HEAD_PAD, (hh + 1) * HEAD_PAD) for hh in range(2)]
    m_scr[...] = jnp.full(m_scr.shape, -jnp.inf, f32)

    def lane_max(mx, s):
        for g in range(groups):
            mx = jnp.maximum(mx, s[:, g * LANES:(g + 1) * LANES])
        return mx

    def probs(s, mb):
        ps = [jnp.exp2(s[:, g * LANES:(g + 1) * LANES] - mb) for g in range(groups)]
        return ps, functools.reduce(lambda a, b: a + b, ps)

    def body_a(trip, c):
        for t in range(tiles):
            kt_idx = trip * tiles + t
            for hh in range(2):
                s = _dot(q_ref[:, heads[hh]], kt_ref[kt_idx, heads[hh], :])
                s_scr[hh, kt_idx] = s
                m_scr[hh] = lane_max(m_scr[hh], s)
        return c

    lax.fori_loop(0, qi, body_a, 0)
    for t in range(tiles):
        kt_idx = qi * tiles + t
        rows = ATT_Q - t * ATT_K
        visible = (lax.broadcasted_iota(jnp.int32, (rows, ATT_K), 1)
                   <= lax.broadcasted_iota(jnp.int32, (rows, ATT_K), 0))
        for hh in range(2):
            s = _dot(q_ref[t * ATT_K:, heads[hh]], kt_ref[kt_idx, heads[hh], :])
            s = jnp.where(visible, s, -jnp.inf)
            s_scr[hh, kt_idx, t * ATT_K:, :] = s
            m_scr[hh, t * ATT_K:, :] = lane_max(m_scr[hh, t * ATT_K:, :], s)

    for hh in range(2):
        row_max = jnp.max(m_scr[hh], axis=-1, keepdims=True)
        m_scr[hh] = jnp.broadcast_to(row_max, (ATT_Q, LANES))
    l_scr[...] = jnp.zeros(l_scr.shape, f32)
    acc_scr[...] = jnp.zeros(acc_scr.shape, f32)

    def body_b(trip, c):
        for hh in range(2):
            mb = m_scr[hh]
            lsum, acc = l_scr[hh], acc_scr[hh]
            for t0 in range(0, tiles, PV_TILES):
                kt_idx = trip * tiles + t0
                row0 = pl.multiple_of(kt_idx * ATT_K, PV_TILES * ATT_K)
                vb = v_ref[pl.ds(row0, PV_TILES * ATT_K), :]
                parts = []
                for t in range(PV_TILES):
                    ps, psum = probs(s_scr[hh, kt_idx + t], mb)
                    parts += ps
                    lsum = lsum + psum
                acc = acc + _dot(jnp.concatenate(parts, axis=1).astype(bf16), vb)
            l_scr[hh] = lsum
            acc_scr[hh] = acc
        return c

    lax.fori_loop(0, qi, body_b, 0)
    for t in range(tiles):
        kt_idx = qi * tiles + t
        row0 = pl.multiple_of(kt_idx * ATT_K, ATT_K)
        vb = v_ref[pl.ds(row0, ATT_K), :]
        for hh in range(2):
            ps, psum = probs(s_scr[hh, kt_idx, t * ATT_K:, :], m_scr[hh, t * ATT_K:, :])
            l_scr[hh, t * ATT_K:, :] += psum
            acc_scr[hh, t * ATT_K:, :] += _dot(jnp.concatenate(ps, axis=1).astype(bf16), vb)

    lane = lax.broadcasted_iota(jnp.int32, (ATT_Q, 2 * HEAD_DIM), 1)
    o0 = acc_scr[0] / jnp.sum(l_scr[0], axis=-1, keepdims=True)
    o1 = acc_scr[1] / jnp.sum(l_scr[1], axis=-1, keepdims=True)
    o_ref[...] = jnp.where(lane < HEAD_DIM, o0, o1).astype(bf16)


def _attention(q, kt, v, batch, seq):
    T = q.shape[0]
    nq = seq // ATT_Q
    nk = seq // ATT_K
    pairs = N_HEADS // 2
    return pl.pallas_call(
        _attn_kernel,
        grid=(batch, pairs, nq),
        in_specs=[
            pl.BlockSpec((ATT_Q, 2 * HEAD_PAD), lambda b, p, i: (b * nq + i, p)),
            pl.BlockSpec((nk, 2 * HEAD_PAD, ATT_K), lambda b, p, i: (b, p, 0)),
            pl.BlockSpec((seq, 2 * HEAD_DIM), lambda b, p, i: (b, p)),
        ],
        out_specs=pl.BlockSpec((ATT_Q, 2 * HEAD_DIM), lambda b, p, i: (b * nq + i, p)),
        out_shape=jax.ShapeDtypeStruct((T, D_MODEL), bf16),
        scratch_shapes=[
            pltpu.VMEM((2, nk, ATT_Q, ATT_K), f32),
            pltpu.VMEM((2, ATT_Q, LANES), f32),
            pltpu.VMEM((2, ATT_Q, LANES), f32),
            pltpu.VMEM((2, ATT_Q, 2 * HEAD_DIM), f32),
        ],
        compiler_params=_params(("parallel", "parallel", "parallel")),
        name="attn",
    )(q, kt, v)


def _attn_out_kernel(x_ref, o_ref, sg_ref, wo_ref, mn_ref, wr_ref, x3_ref, lg_ref):
    gated = (o_ref[...].astype(f32) * sg_ref[...].astype(f32)).astype(bf16)
    x3 = x_ref[...] + _dot(gated, wo_ref[...])
    x3_ref[...] = x3
    lg_ref[...] = _router_logits_t(x3, mn_ref[...], wr_ref[...])


def _attn_out(x2d, o, sg, wo, m_norm, w_router):
    T = x2d.shape[0]
    return pl.pallas_call(
        _attn_out_kernel,
        grid=(T // SGU_TILE,),
        in_specs=[
            pl.BlockSpec((SGU_TILE, D_MODEL), lambda i: (i, 0)),
            pl.BlockSpec((SGU_TILE, D_MODEL), lambda i: (i, 0)),
            pl.BlockSpec((SGU_TILE, D_MODEL), lambda i: (i, 0)),
            _const_spec((D_MODEL, D_MODEL)),
            _const_spec((1, D_MODEL)),
            _const_spec((2 * D_MODEL, 2 * LANES)),
        ],
        out_specs=[
            pl.BlockSpec((SGU_TILE, D_MODEL), lambda i: (i, 0)),
            pl.BlockSpec((ROUTER_ROWS, SGU_TILE), lambda i: (0, i)),
        ],
        out_shape=[
            jax.ShapeDtypeStruct((T, D_MODEL), f32),
            jax.ShapeDtypeStruct((ROUTER_ROWS, T), f32),
        ],
        compiler_params=_params(("parallel",)),
        name="attn_out",
    )(x2d, o, sg, wo, m_norm, w_router)


def _router_weight(w_group, w_expert):
    pad = jnp.zeros((D_MODEL, LANES - N_EXPERTS - N_GROUPS), f32)
    w = jnp.concatenate([w_expert, w_group, pad], axis=1)
    w_hi = w.astype(bf16)
    w_lo = (w - w_hi.astype(f32)).astype(bf16)
    top = jnp.concatenate([w_hi, w_lo], axis=1)
    bottom = jnp.concatenate([w_hi, jnp.zeros_like(w_hi)], axis=1)
    return jnp.concatenate([top, bottom], axis=0)


def kernel(x, a_norm, a_w_in, a_b_in, a_v_norm, a_w_s, a_b_s, a_w_out, kv_norm, kv_w, kv_b_f,
           k_norm, b_norm, b_w_qg, q_norm, b_w_out, m_norm, m_w_group, m_b_group, m_w_expert,
           m_b_expert, m_w1, m_w3, m_w2):
    batch, seq, _ = x.shape
    T = batch * seq
    x2d = x.reshape(T, D_MODEL)

    x1, lg0 = _sgu(
        x2d, a_norm[0].reshape(1, -1), a_w_in[0].astype(bf16), a_b_in[0].reshape(1, -1),
        a_v_norm[0].reshape(1, -1), a_w_s[0], a_b_s[0].T, a_w_out[0].astype(bf16),
        m_norm[0].reshape(1, -1), _router_weight(m_w_group[0], m_w_expert[0]))
    x2 = _moe(0, x1, lg0, m_norm[0].reshape(1, -1), m_b_group[0], m_b_expert[0],
              m_w1, m_w3, m_w2)

    wk = kv_w[:, :D_MODEL]
    wv = kv_w[:, D_MODEL:2 * D_MODEL]
    wf = jnp.pad(kv_w[:, 2 * D_MODEL:], ((0, 0), (0, LANES - N_HEADS)))
    bfv = jnp.pad(kv_b_f, (0, LANES - N_HEADS)).reshape(1, LANES)
    wq = b_w_qg[0][:, :D_MODEL]
    wg = b_w_qg[0][:, D_MODEL:]
    gq = jnp.pad(q_norm[0], (0, HEAD_PAD - HEAD_DIM)).reshape(1, HEAD_PAD)
    gk = k_norm.reshape(HEAD_DIM, 1)
    wq_pad = jnp.pad(wq.reshape(D_MODEL, N_HEADS, HEAD_DIM),
                     ((0, 0), (0, 0), (0, HEAD_PAD - HEAD_DIM))).reshape(D_MODEL, -1)
    q, kt, v, sg = _proj(
        x2, seq, kv_norm.reshape(1, -1), b_norm[0].reshape(1, -1),
        wk.T.astype(bf16), wv.astype(bf16), wf.astype(bf16), bfv,
        wq_pad.astype(bf16), wg.astype(bf16), gq, gk)
    o = _attention(q, kt, v, batch, seq)
    x3, lg1 = _attn_out(x2, o, sg, b_w_out[0].astype(bf16), m_norm[1].reshape(1, -1),
                        _router_weight(m_w_group[1], m_w_expert[1]))
    x4 = _moe(1, x3, lg1, m_norm[1].reshape(1, -1), m_b_group[1], m_b_expert[1],
              m_w1, m_w3, m_w2)
    return x4.reshape(batch, seq, D_MODEL)
```

```python
import functools
import math

import jax
import jax.numpy as jnp
import numpy as np
from jax import lax
from jax.experimental import pallas as pl
from jax.experimental.pallas import tpu as pltpu

D_MODEL = 1024
EPS = 1e-6
SGU_BLOCK = 128
SGU_CHUNK = 64
CHUNK_SHIFT = SGU_CHUNK.bit_length() - 1
SGU_WIDTH = 2 * D_MODEL
SGU_GROUPS = 8
SGU_GROUP_DIM = SGU_WIDTH // SGU_GROUPS
N_HEADS = 16
HEAD_DIM = D_MODEL // N_HEADS
N_GROUPS = 4
EXPERTS_PER_GROUP = 8
N_EXPERTS = N_GROUPS * EXPERTS_PER_GROUP
GROUP_SHIFT = EXPERTS_PER_GROUP.bit_length() - 1
TOP_K = 2
D_EXPERT = D_MODEL // 2

LANES = 128
ROW_SUBLANES = D_MODEL // 2 // LANES
ISSUE_UNROLL = 8
VMEM_LIMIT_BYTES = 56 * 1024 * 1024

ROW_TILE = 256
SGU_TILE = 512
EPILOGUE_ROWS = 256
ROUTE_TILE = 1024
EXPERT_ROWS = 512
MOVE_TILE = 512
MOVE_CHUNK = 64
ATT_Q = 1024
ATT_K = 256
PV_TILES = 2
HEAD_PAD = 2 * HEAD_DIM
ROUTER_ROWS = 40
AUX_ONE0 = HEAD_DIM
AUX_CUM0 = HEAD_DIM + 3

LOG2E = math.log2(math.e)

bf16 = jnp.bfloat16
f32 = jnp.float32
u32 = jnp.uint32
HI_HALF = np.uint32(0xFFFF0000)


def _dot(a, b, precision=None):
    return jnp.dot(a, b, preferred_element_type=f32, precision=precision)


def _dot_nt(a, b, precision=None):
    return lax.dot_general(a, b, (((1,), (1,)), ((), ())), preferred_element_type=f32,
                           precision=precision)


def _rms(x, g):
    ms = jnp.mean(x * x, axis=-1, keepdims=True)
    return x * lax.rsqrt(ms + EPS) * g


def _gelu_tanh(z):
    c = math.sqrt(2.0 / math.pi)
    return z * (0.5 * (1.0 + jnp.tanh(c * (z + 0.044715 * (z * z * z)))))


def _split3(c):
    hi = c.astype(bf16).astype(f32)
    r = c - hi
    mid = r.astype(bf16).astype(f32)
    lo = r - mid
    return hi, mid, lo


def _pack_rows(v):
    half = D_MODEL // 2
    lo = lax.bitcast_convert_type(v[:, :half].astype(bf16).astype(f32), u32) >> 16
    hi = lax.bitcast_convert_type(v[:, half:].astype(bf16).astype(f32), u32) & HI_HALF
    return (hi | lo).reshape(v.shape[0], ROW_SUBLANES, LANES)


def _unpack_rows(w):
    w = w.reshape(w.shape[0], D_MODEL // 2)
    lo = lax.bitcast_convert_type(w << 16, f32)
    hi = lax.bitcast_convert_type(w & HI_HALF, f32)
    return jnp.concatenate([lo, hi], axis=1)


def _const_spec(shape, single=False):
    nd = len(shape)
    mode = pl.Buffered(1) if single else None
    return pl.BlockSpec(shape, lambda *_: (0,) * nd, pipeline_mode=mode)


def _params(sem):
    return pltpu.CompilerParams(dimension_semantics=sem, vmem_limit_bytes=VMEM_LIMIT_BYTES)


def _router_logits_t(x_new, mn, wr3):
    hm = _rms(x_new, mn)
    h_hi = hm.astype(bf16)
    h_lo = (hm - h_hi.astype(f32)).astype(bf16)
    both = _dot(jnp.concatenate([h_hi, h_lo], axis=1), wr3)
    lg = both[:, :LANES] + both[:, LANES:]
    return lg.T[:ROUTER_ROWS, :]


def _sgu_kernel(x_ref, an_ref, win_ref, bin_ref, vn_ref, ws_ref, bst_ref, wout_ref, mn_ref,
                wr_ref, x1_ref, lg_ref, u_scr, v_scr, gated_scr):
    x = x_ref[...]
    h = _rms(x, an_ref[...]).astype(bf16)
    v = _gelu_tanh(_dot(h, win_ref[:, SGU_WIDTH:]) + bin_ref[:, SGU_WIDTH:])
    v_scr[...] = _rms(v, vn_ref[...]).astype(bf16)
    u_scr[...] = _gelu_tanh(_dot(h, win_ref[:, :SGU_WIDTH]) + bin_ref[:, :SGU_WIDTH])

    t_chunk = lax.broadcasted_iota(jnp.int32, (SGU_BLOCK, SGU_BLOCK), 0) >> CHUNK_SHIFT
    s_chunk = lax.broadcasted_iota(jnp.int32, (SGU_BLOCK, SGU_BLOCK), 1) >> CHUNK_SHIFT
    causal = t_chunk >= s_chunk
    for g in range(SGU_GROUPS):
        wsm = jnp.where(causal, ws_ref[g], 0.0).astype(bf16)
        cols = slice(g * SGU_GROUP_DIM, (g + 1) * SGU_GROUP_DIM)
        for sb in range(SGU_TILE // SGU_BLOCK):
            rows = slice(sb * SGU_BLOCK, (sb + 1) * SGU_BLOCK)
            mixed = _dot(wsm, v_scr[rows, cols]) + bst_ref[:, g:g + 1]
            gated_scr[rows, cols] = (u_scr[rows, cols] * mixed).astype(bf16)

    x1 = x + _dot(gated_scr[...], wout_ref[...])
    x1_ref[...] = x1
    lg_ref[...] = _router_logits_t(x1, mn_ref[...], wr_ref[...])


def _sgu(x2d, a_norm, w_in, b_in, v_norm, w_s, b_s_t, w_out, m_norm, w_router):
    T = x2d.shape[0]
    return pl.pallas_call(
        _sgu_kernel,
        grid=(T // SGU_TILE,),
        in_specs=[
            pl.BlockSpec((SGU_TILE, D_MODEL), lambda i: (i, 0)),
            _const_spec((1, D_MODEL)),
            _const_spec((D_MODEL, 2 * SGU_WIDTH), single=True),
            _const_spec((1, 2 * SGU_WIDTH)),
            _const_spec((1, SGU_WIDTH)),
            _const_spec((SGU_GROUPS, SGU_BLOCK, SGU_BLOCK)),
            _const_spec((SGU_BLOCK, SGU_GROUPS)),
            _const_spec((SGU_WIDTH, D_MODEL), single=True),
            _const_spec((1, D_MODEL)),
            _const_spec((2 * D_MODEL, 2 * LANES)),
        ],
        out_specs=[
            pl.BlockSpec((SGU_TILE, D_MODEL), lambda i: (i, 0)),
            pl.BlockSpec((ROUTER_ROWS, SGU_TILE), lambda i: (0, i)),
        ],
        out_shape=[
            jax.ShapeDtypeStruct((T, D_MODEL), f32),
            jax.ShapeDtypeStruct((ROUTER_ROWS, T), f32),
        ],
        scratch_shapes=[
            pltpu.VMEM((SGU_TILE, SGU_WIDTH), f32),
            pltpu.VMEM((SGU_TILE, SGU_WIDTH), bf16),
            pltpu.VMEM((SGU_TILE, SGU_WIDTH), bf16),
        ],
        compiler_params=_params(("parallel",)),
        name="sgu",
    )(x2d, a_norm, w_in, b_in, v_norm, w_s, b_s_t, w_out, m_norm, w_router)


def _route_kernel(lg_ref, be_ref, bg_ref, oi_ref, ow_ref, cnt_ref, carry_scr, before_scr):
    step = pl.program_id(0)
    tt = lg_ref.shape[1]

    @pl.when(step == 0)
    def _():
        carry_scr[...] = jnp.zeros_like(carry_scr)
        before_scr[...] = (lax.broadcasted_iota(jnp.int32, (tt, tt), 0)
                           < lax.broadcasted_iota(jnp.int32, (tt, tt), 1)).astype(bf16)

    lg = lg_ref[...]
    e_l = lg[0:N_EXPERTS, :] + be_ref[...]
    g_l = lg[N_EXPERTS:N_EXPERTS + 8, :] + bg_ref[...]
    g_row = lax.broadcasted_iota(jnp.int32, g_l.shape, 0).astype(f32)
    g_l = jnp.where(g_row < N_GROUPS, g_l, -jnp.inf)
    g_max = jnp.max(g_l, axis=0, keepdims=True)
    g_sel = jnp.min(jnp.where(g_l == g_max, g_row, 8.0), axis=0, keepdims=True)
    g_den = jnp.sum(jnp.exp(g_l - g_max), axis=0, keepdims=True)
    g_w = 1.0 / g_den

    e_row_i = lax.broadcasted_iota(jnp.int32, e_l.shape, 0)
    e_row = e_row_i.astype(f32)
    e_grp = (e_row_i >> GROUP_SHIFT).astype(f32)
    e_in = jnp.where(e_grp == g_sel, e_l, -jnp.inf)
    m1 = jnp.max(e_in, axis=0, keepdims=True)
    i1 = jnp.min(jnp.where(e_in == m1, e_row, float(N_EXPERTS)), axis=0, keepdims=True)
    e_in2 = jnp.where(e_row == i1, -jnp.inf, e_in)
    m2 = jnp.max(e_in2, axis=0, keepdims=True)
    i2 = jnp.min(jnp.where(e_in2 == m2, e_row, float(N_EXPERTS)), axis=0, keepdims=True)
    t = jnp.exp(m2 - m1)
    p1 = 1.0 / (1.0 + t)
    p2 = t / (1.0 + t)

    hit1 = e_row == i1
    hit2 = e_row == i2
    member = jnp.logical_or(hit1, hit2)
    prefix = _dot(member.astype(bf16), before_scr[...])
    rank_full = prefix + carry_scr[...]
    rank1 = jnp.sum(jnp.where(hit1, rank_full, 0.0), axis=0, keepdims=True)
    rank2 = jnp.sum(jnp.where(hit2, rank_full, 0.0), axis=0, keepdims=True)
    carry_new = carry_scr[...] + jnp.sum(member.astype(f32), axis=1, keepdims=True)
    carry_scr[...] = carry_new

    oi_ref[...] = jnp.zeros(oi_ref.shape, jnp.int32)
    oi_ref[0:1, :] = i1.astype(jnp.int32)
    oi_ref[1:2, :] = i2.astype(jnp.int32)
    oi_ref[2:3, :] = rank1.astype(jnp.int32)
    oi_ref[3:4, :] = rank2.astype(jnp.int32)
    ow_ref[...] = jnp.zeros(ow_ref.shape, f32)
    ow_ref[0:1, :] = g_w * p1
    ow_ref[1:2, :] = g_w * p2
    cnt_ref[...] = jnp.broadcast_to(carry_new, cnt_ref.shape)


def _route(logits_t, b_expert, b_group):
    T = logits_t.shape[1]
    return pl.pallas_call(
        _route_kernel,
        grid=(T // ROUTE_TILE,),
        in_specs=[
            pl.BlockSpec((ROUTER_ROWS, ROUTE_TILE), lambda i: (0, i)),
            _const_spec((N_EXPERTS, 1)),
            _const_spec((8, 1)),
        ],
        out_specs=[
            pl.BlockSpec((8, ROUTE_TILE), lambda i: (0, i)),
            pl.BlockSpec((8, ROUTE_TILE), lambda i: (0, i)),
            _const_spec((N_EXPERTS, LANES)),
        ],
        out_shape=[
            jax.ShapeDtypeStruct((8, T), jnp.int32),
            jax.ShapeDtypeStruct((8, T), f32),
            jax.ShapeDtypeStruct((N_EXPERTS, LANES), f32),
        ],
        scratch_shapes=[pltpu.VMEM((N_EXPERTS, 1), f32),
                        pltpu.VMEM((ROUTE_TILE, ROUTE_TILE), bf16)],
        compiler_params=_params(("arbitrary",)),
        name="route",
    )(logits_t, b_expert, b_group)


def _row_copy(src, s, dst, d, sem):
    return pltpu.make_async_copy(src.at[s], dst.at[d], sem)


def _wait_rows(buf, sem):
    pltpu.make_async_copy(buf, buf, sem).wait()


def _dispatch_kernel(n_blocks, pend_ref, padded_ref, nu_ref, dest_ref, x_ref, mn_ref, xs_hbm,
                     zero_scr, x3_scr, zsem, sem):
    @pl.when(pl.program_id(0) == 0)
    def _():
        zero_scr[...] = jnp.zeros_like(zero_scr)

        def fill(row0):
            return pltpu.make_async_copy(zero_scr, xs_hbm.at[pl.ds(row0, EXPERT_ROWS)], zsem)

        for e in range(N_EXPERTS):
            @pl.when(padded_ref[e] > 0)
            def _():
                fill(pend_ref[e] - EXPERT_ROWS).start()

            @pl.when(nu_ref[0] + e < n_blocks)
            def _():
                fill((nu_ref[0] + e) * EXPERT_ROWS).start()

        for e in range(N_EXPERTS):
            @pl.when(padded_ref[e] > 0)
            def _():
                fill(0).wait()

            @pl.when(nu_ref[0] + e < n_blocks)
            def _():
                fill(0).wait()

    i = pl.program_id(0)
    n_tiles = pl.num_programs(0) - 1

    def staged(rows):
        return _pack_rows(_rms(x_ref[rows, :], mn_ref[...]))

    @pl.when(i == 0)
    def _():
        x3_scr[0] = staged(slice(None))

    @pl.when(i > 0)
    def _():
        cur = (i - 1) % 3
        nxt = i % 3

        def chunk(c, carry):
            r0 = pl.multiple_of(c * MOVE_CHUNK, MOVE_CHUNK)
            for j in range(MOVE_CHUNK):
                for k in range(TOP_K):
                    _row_copy(x3_scr.at[cur], r0 + j, xs_hbm,
                              dest_ref[0, k * MOVE_TILE + r0 + j], sem.at[cur]).start(priority=k)
            x3_scr[nxt, pl.ds(r0, MOVE_CHUNK)] = staged(pl.ds(r0, MOVE_CHUNK))
            return carry

        lax.fori_loop(0, MOVE_TILE // MOVE_CHUNK, chunk, 0)

        @pl.when(i > 1)
        def _():
            for k in range(TOP_K):
                _wait_rows(x3_scr.at[(i - 2) % 3], sem.at[(i - 2) % 3])

        @pl.when(i == n_tiles)
        def _():
            for k in range(TOP_K):
                _wait_rows(x3_scr.at[cur], sem.at[cur])


def _dispatch(pend, padded, n_used, dest_tiles, x2d, m_norm, n_rows):
    T = x2d.shape[0]
    n_blocks = n_rows // EXPERT_ROWS
    n_tiles = T // MOVE_TILE
    grid_spec = pltpu.PrefetchScalarGridSpec(
        num_scalar_prefetch=3,
        grid=(n_tiles + 1,),
        in_specs=[
            pl.BlockSpec((None, 1, TOP_K * MOVE_TILE),
                         lambda i, *_: (jnp.maximum(i - 1, 0), 0, 0), memory_space=pltpu.SMEM),
            pl.BlockSpec((MOVE_TILE, D_MODEL), lambda i, *_: (jnp.minimum(i, n_tiles - 1), 0)),
            pl.BlockSpec((1, D_MODEL), lambda i, *_: (0, 0)),
        ],
        out_specs=pl.BlockSpec(memory_space=pl.ANY),
        scratch_shapes=[
            pltpu.VMEM((EXPERT_ROWS, ROW_SUBLANES, LANES), u32),
            pltpu.VMEM((3, MOVE_TILE, ROW_SUBLANES, LANES), u32),
            pltpu.SemaphoreType.DMA(()),
            pltpu.SemaphoreType.DMA((3,)),
        ],
    )
    return pl.pallas_call(
        functools.partial(_dispatch_kernel, n_blocks),
        grid_spec=grid_spec,
        out_shape=jax.ShapeDtypeStruct((n_rows, ROW_SUBLANES, LANES), u32),
        compiler_params=_params(("arbitrary",)),
        name="dispatch",
    )(pend, padded, n_used, dest_tiles, x2d, m_norm)


def _expert_kernel(n_blocks, ps_ref, nb_ref, nu_ref, xs_hbm, w1_ref, w3_ref, w2_ref,
                   ys_hbm, xbuf, ybuf, h_scr, w1_scr, w3_scr, w2_scr, xsem, ysem):
    e = pl.program_id(0)
    nblk = nb_ref[e]
    g0 = ps_ref[e]
    n_used = nu_ref[0]

    def rows(g):
        return pl.ds(pl.multiple_of(g * EXPERT_ROWS, EXPERT_ROWS), EXPERT_ROWS)

    def x_copy(g):
        return pltpu.make_async_copy(xs_hbm.at[rows(g)], xbuf.at[g % 2], xsem.at[g % 2])

    def y_copy(g):
        return pltpu.make_async_copy(ybuf.at[g % 2], ys_hbm.at[rows(g)], ysem.at[g % 2])

    def normed(slot):
        return _unpack_rows(xbuf[slot]).astype(bf16)

    @pl.when(e == 0)
    def _():
        x_copy(0).start(priority=1)
        x_copy(1).start(priority=1)
        x_copy(0).wait()
        h_scr[...] = normed(0)

    @pl.when(nblk > 0)
    def _():
        w1_scr[...] = w1_ref[...].astype(bf16)
        w3_scr[...] = w3_ref[...].astype(bf16)
        w2_scr[...] = w2_ref[...].astype(bf16)

        def body(g, h):
            @pl.when(g + 2 < n_used)
            def _():
                x_copy(g + 2).start(priority=1)

            @pl.when(g + 1 < n_used)
            def _():
                x_copy(g + 1).wait()

            @pl.when(g >= 2)
            def _():
                y_copy(g - 2).wait()

            h_next = normed((g + 1) % 2)
            a = _dot(h, w1_scr[...])
            c = _dot(h, w3_scr[...])
            hid = (a * jax.nn.sigmoid(a)) * c
            ybuf[g % 2] = _pack_rows(_dot(hid.astype(bf16), w2_scr[...]))
            y_copy(g).start(priority=1)
            return h_next

        h_scr[...] = lax.fori_loop(g0, g0 + nblk, body, h_scr[...])

    @pl.when(e == pl.num_programs(0) - 1)
    def _():
        y_copy(n_used - 2).wait()
        y_copy(n_used - 1).wait()
        ybuf[0] = jnp.zeros(ybuf.shape[1:], u32)

        def fill(b):
            r = pl.ds(pl.multiple_of(b * EXPERT_ROWS, EXPERT_ROWS), EXPERT_ROWS)
            return pltpu.make_async_copy(ybuf.at[0], ys_hbm.at[r], ysem.at[0])

        def start(b, c):
            fill(b).start()
            return c

        def wait(b, c):
            fill(b).wait()
            return c

        lax.fori_loop(nu_ref[0], n_blocks, start, 0)
        lax.fori_loop(nu_ref[0], n_blocks, wait, 0)


def _experts(layer, pstart, seg_blocks, n_used, xs, w1, w3, w2):
    n_rows = xs.shape[0]
    n_blocks = n_rows // EXPERT_ROWS

    def w_map(e, *_):
        return (layer, e, 0, 0)

    block_buf = pltpu.VMEM((2, EXPERT_ROWS, ROW_SUBLANES, LANES), u32)
    grid_spec = pltpu.PrefetchScalarGridSpec(
        num_scalar_prefetch=3,
        grid=(N_EXPERTS,),
        in_specs=[
            pl.BlockSpec(memory_space=pl.ANY),
            pl.BlockSpec((None, None, D_MODEL, D_EXPERT), w_map),
            pl.BlockSpec((None, None, D_MODEL, D_EXPERT), w_map),
            pl.BlockSpec((None, None, D_EXPERT, D_MODEL), w_map),
        ],
        out_specs=pl.BlockSpec(memory_space=pl.ANY),
        scratch_shapes=[
            block_buf,
            block_buf,
            pltpu.VMEM((EXPERT_ROWS, D_MODEL), bf16),
            pltpu.VMEM((D_MODEL, D_EXPERT), bf16),
            pltpu.VMEM((D_MODEL, D_EXPERT), bf16),
            pltpu.VMEM((D_EXPERT, D_MODEL), bf16),
            pltpu.SemaphoreType.DMA((2,)),
            pltpu.SemaphoreType.DMA((2,)),
        ],
    )
    return pl.pallas_call(
        functools.partial(_expert_kernel, n_blocks),
        grid_spec=grid_spec,
        out_shape=jax.ShapeDtypeStruct((n_rows, ROW_SUBLANES, LANES), u32),
        compiler_params=_params(("arbitrary",)),
        name="experts",
    )(pstart, seg_blocks, n_used, xs, w1, w3, w2)


def _combine_kernel(dest_ref, next_ref, x_ref, w_ref, ys_hbm, out_ref, buf, sem):
    i = pl.program_id(0)
    slot = i % 2

    def gather(idx_ref, s):
        def start(j, c):
            for k in range(TOP_K):
                _row_copy(ys_hbm, idx_ref[0, k * MOVE_TILE + j], buf.at[s, k], j,
                          sem.at[s]).start(priority=k)
            return c

        lax.fori_loop(0, MOVE_TILE, start, 0, unroll=ISSUE_UNROLL)

    @pl.when(i == 0)
    def _():
        gather(dest_ref, slot)

    @pl.when(i + 1 < pl.num_programs(0))
    def _():
        gather(next_ref, 1 - slot)

    for k in range(TOP_K):
        _wait_rows(buf.at[slot, k], sem.at[slot])
    w = w_ref[...]
    y0 = _unpack_rows(buf[slot, 0])
    y1 = _unpack_rows(buf[slot, 1])
    out_ref[...] = x_ref[...] + (w[:, 0:1] * y0 + w[:, 1:2] * y1)


def _combine(dest_tiles, x2d, w_cols, ys):
    T = x2d.shape[0]
    n_tiles = T // MOVE_TILE
    return pl.pallas_call(
        _combine_kernel,
        grid=(n_tiles,),
        in_specs=[
            pl.BlockSpec((None, 1, TOP_K * MOVE_TILE), lambda i: (i, 0, 0),
                         memory_space=pltpu.SMEM),
            pl.BlockSpec((None, 1, TOP_K * MOVE_TILE),
                         lambda i: (jnp.minimum(i + 1, n_tiles - 1), 0, 0),
                         memory_space=pltpu.SMEM),
            pl.BlockSpec((MOVE_TILE, D_MODEL), lambda i: (i, 0)),
            pl.BlockSpec((MOVE_TILE, 8), lambda i: (i, 0)),
            pl.BlockSpec(memory_space=pl.ANY),
        ],
        out_specs=pl.BlockSpec((MOVE_TILE, D_MODEL), lambda i: (i, 0)),
        out_shape=jax.ShapeDtypeStruct((T, D_MODEL), f32),
        scratch_shapes=[
            pltpu.VMEM((2, TOP_K, MOVE_TILE, ROW_SUBLANES, LANES), u32),
            pltpu.SemaphoreType.DMA((2,)),
        ],
        compiler_params=_params(("arbitrary",)),
        name="combine",
    )(dest_tiles, dest_tiles, x2d, w_cols, ys)


def _moe(layer, x2d, logits_t, m_norm, b_group, b_expert, w1, w3, w2):
    T = x2d.shape[0]
    n_blocks = (T * TOP_K) // EXPERT_ROWS + N_EXPERTS
    n_rows = n_blocks * EXPERT_ROWS

    be = b_expert.reshape(N_EXPERTS, 1)
    bg = jnp.concatenate([b_group, jnp.zeros((8 - N_GROUPS,), f32)]).reshape(8, 1)
    oi, ow, cnt = _route(logits_t, be, bg)

    counts = cnt[:, 0].astype(jnp.int32)
    padded = (counts + EXPERT_ROWS - 1) // EXPERT_ROWS * EXPERT_ROWS
    pend = jnp.cumsum(padded)
    pstart = pend - padded
    n_used = (pend[-1] // EXPERT_ROWS).astype(jnp.int32).reshape(1)
    e_ids = jnp.arange(N_EXPERTS, dtype=jnp.int32)[:, None, None]
    seg_start = jnp.sum(jnp.where(oi[None, 0:2] == e_ids, pstart[:, None, None], 0), axis=0)
    dest = seg_start + oi[2:4]
    dest_tiles = dest.reshape(TOP_K, T // MOVE_TILE, MOVE_TILE).transpose(1, 0, 2).reshape(
        T // MOVE_TILE, 1, TOP_K * MOVE_TILE)
    w_cols = ow.T

    xs = _dispatch(pend.astype(jnp.int32), padded, n_used, dest_tiles, x2d, m_norm, n_rows)
    ys = _experts(layer, (pstart // EXPERT_ROWS).astype(jnp.int32), padded // EXPERT_ROWS,
                  n_used, xs, w1, w3, w2)
    return _combine(dest_tiles, x2d, w_cols, ys)


def _log_sigmoid(x):
    return jnp.minimum(x, 0.0) - jnp.log1p(jnp.exp(-jnp.abs(x)))


def _proj_kernel(tiles_per_seq, x_ref, kvn_ref, bn_ref, wkt_ref, wv_ref, wf_ref, bf_ref,
                 wq_ref, wg_ref, gq_ref, gk_ref, q_ref, kt_ref, v_ref, sg_ref, carry_scr):
    i = pl.program_id(0)
    x = x_ref[...]
    hkv = _rms(x, kvn_ref[...]).astype(bf16)
    hq = _rms(x, bn_ref[...]).astype(bf16)

    v_ref[...] = _dot(hkv, wv_ref[...]).astype(bf16)
    sg_ref[...] = jax.nn.sigmoid(_dot(hq, wg_ref[...])).astype(bf16)

    logf = _log_sigmoid(_dot(hkv, wf_ref[...]) + bf_ref[...])
    tm = x.shape[0]
    incl = (lax.broadcasted_iota(jnp.int32, (tm, tm), 0)
            >= lax.broadcasted_iota(jnp.int32, (tm, tm), 1)).astype(bf16)

    @pl.when(i % tiles_per_seq == 0)
    def _():
        carry_scr[...] = jnp.zeros_like(carry_scr)

    parts = _dot(incl, jnp.concatenate(_split3(logf), axis=1).astype(bf16))
    cum = (parts[:, :LANES] + parts[:, LANES:2 * LANES] + parts[:, 2 * LANES:]
           + carry_scr[...])
    carry_scr[...] = cum[tm - 1:tm, :]
    cum = cum * LOG2E
    cum_t = cum.T

    lane = lax.broadcasted_iota(jnp.int32, (tm, HEAD_PAD), 1)
    sub = lax.broadcasted_iota(jnp.int32, (HEAD_DIM, tm), 0)
    q_raw = _dot(hq, wq_ref[...])
    k_raw_t = _dot_nt(wkt_ref[...], hkv)
    scale = HEAD_DIM ** -0.5 * LOG2E

    for h in range(N_HEADS):
        sl = slice(h * HEAD_PAD, (h + 1) * HEAD_PAD)
        qb = q_raw[:, sl]
        q_ms = jnp.sum(qb * qb, axis=-1, keepdims=True) * (1.0 / HEAD_DIM)
        qn = qb * lax.rsqrt(q_ms + EPS) * gq_ref[...] * scale
        c_hi, c_mid, c_lo = _split3(cum[:, h:h + 1])
        qa = jnp.where(lane < HEAD_DIM, qn,
             jnp.where(lane < AUX_CUM0, 1.0,
             jnp.where(lane == AUX_CUM0, c_hi,
             jnp.where(lane == AUX_CUM0 + 1, c_mid,
             jnp.where(lane == AUX_CUM0 + 2, c_lo, 0.0)))))
        q_ref[:, sl] = qa.astype(bf16)

        kb = k_raw_t[h * HEAD_DIM:(h + 1) * HEAD_DIM, :]
        k_ms = jnp.sum(kb * kb, axis=0, keepdims=True) * (1.0 / HEAD_DIM)
        kn = kb * lax.rsqrt(k_ms + EPS) * gk_ref[...]
        t_hi, t_mid, t_lo = _split3(cum_t[h:h + 1, :])
        aux = jnp.where(sub == 0, -t_hi,
              jnp.where(sub == 1, -t_mid,
              jnp.where(sub == 2, -t_lo,
              jnp.where(sub < 6, 1.0, 0.0))))
        sl = slice(h * HEAD_PAD, (h + 1) * HEAD_PAD)
        ka = jnp.concatenate([kn, aux], axis=0).astype(bf16)
        for t in range(tm // ATT_K):
            kt_ref[t, sl, :] = ka[:, t * ATT_K:(t + 1) * ATT_K]


def _proj(x2d, seq, kv_norm, b_norm, wkt, wv, wf, bfv, wq, wg, gq, gk):
    T = x2d.shape[0]
    n_tiles = T // ROW_TILE
    qw = N_HEADS * HEAD_PAD
    return pl.pallas_call(
        functools.partial(_proj_kernel, seq // ROW_TILE),
        grid=(n_tiles,),
        in_specs=[
            pl.BlockSpec((ROW_TILE, D_MODEL), lambda i: (i, 0)),
            _const_spec((1, D_MODEL)),
            _const_spec((1, D_MODEL)),
            _const_spec((D_MODEL, D_MODEL), single=True),
            _const_spec((D_MODEL, D_MODEL), single=True),
            _const_spec((D_MODEL, LANES)),
            _const_spec((1, LANES)),
            _const_spec((D_MODEL, qw), single=True),
            _const_spec((D_MODEL, D_MODEL), single=True),
            _const_spec((1, HEAD_PAD)),
            _const_spec((HEAD_DIM, 1)),
        ],
        out_specs=[
            pl.BlockSpec((ROW_TILE, qw), lambda i: (i, 0)),
            pl.BlockSpec((ROW_TILE // ATT_K, qw, ATT_K), lambda i: (i, 0, 0)),
            pl.BlockSpec((ROW_TILE, D_MODEL), lambda i: (i, 0)),
            pl.BlockSpec((ROW_TILE, D_MODEL), lambda i: (i, 0)),
        ],
        out_shape=[
            jax.ShapeDtypeStruct((T, qw), bf16),
            jax.ShapeDtypeStruct((T // ATT_K, qw, ATT_K), bf16),
            jax.ShapeDtypeStruct((T, D_MODEL), bf16),
            jax.ShapeDtypeStruct((T, D_MODEL), bf16),
        ],
        scratch_shapes=[pltpu.VMEM((1, LANES), f32)],
        compiler_params=_params(("arbitrary",)),
        name="proj",
    )(x2d, kv_norm, b_norm, wkt, wv, wf, bfv, wq, wg, gq, gk)


def _attn_kernel(q_ref, kt_ref, v_ref, o_ref, s_scr, m_scr, l_scr, acc_scr):
    qi = pl.program_id(2)
    tiles = ATT_Q // ATT_K
    groups = ATT_K // LANES
    heads = [slice(hh * HEAD_PAD, (hh + 1) * HEAD_PAD) for hh in range(2)]
    m_scr[...] = jnp.full(m_scr.shape, -jnp.inf, f32)

    def lane_max(mx, s):
        for g in range(groups):
            mx = jnp.maximum(mx, s[:, g * LANES:(g + 1) * LANES])
        return mx

    def probs(s, mb):
        ps = [jnp.exp2(s[:, g * LANES:(g + 1) * LANES] - mb) for g in range(groups)]
        return ps, functools.reduce(lambda a, b: a + b, ps)

    def body_a(trip, c):
        for t in range(tiles):
            kt_idx = trip * tiles + t
            for hh in range(2):
                s = _dot(q_ref[:, heads[hh]], kt_ref[kt_idx, heads[hh], :])
                s_scr[hh, kt_idx] = s
                m_scr[hh] = lane_max(m_scr[hh], s)
        return c

    lax.fori_loop(0, qi, body_a, 0)
    for t in range(tiles):
        kt_idx = qi * tiles + t
        rows = ATT_Q - t * ATT_K
        visible = (lax.broadcasted_iota(jnp.int32, (rows, ATT_K), 1)
                   <= lax.broadcasted_iota(jnp.int32, (rows, ATT_K), 0))
        for hh in range(2):
            s = _dot(q_ref[t * ATT_K:, heads[hh]], kt_ref[kt_idx, heads[hh], :])
            s = jnp.where(visible, s, -jnp.inf)
            s_scr[hh, kt_idx, t * ATT_K:, :] = s
            m_scr[hh, t * ATT_K:, :] = lane_max(m_scr[hh, t * ATT_K:, :], s)

    for hh in range(2):
        row_max = jnp.max(m_scr[hh], axis=-1, keepdims=True)
        m_scr[hh] = jnp.broadcast_to(row_max, (ATT_Q, LANES))
    l_scr[...] = jnp.zeros(l_scr.shape, f32)
    acc_scr[...] = jnp.zeros(acc_scr.shape, f32)

    def body_b(trip, c):
        for hh in range(2):
            mb = m_scr[hh]
            lsum, acc = l_scr[hh], acc_scr[hh]
            for t0 in range(0, tiles, PV_TILES):
                kt_idx = trip * tiles + t0
                row0 = pl.multiple_of(kt_idx * ATT_K, PV_TILES * ATT_K)
                vb = v_ref[pl.ds(row0, PV_TILES * ATT_K), :]
                parts = []
                for t in range(PV_TILES):
                    ps, psum = probs(s_scr[hh, kt_idx + t], mb)
                    parts += ps
                    lsum = lsum + psum
                acc = acc + _dot(jnp.concatenate(parts, axis=1).astype(bf16), vb)
            l_scr[hh] = lsum
            acc_scr[hh] = acc
        return c

    lax.fori_loop(0, qi, body_b, 0)
    for t in range(tiles):
        kt_idx = qi * tiles + t
        row0 = pl.multiple_of(kt_idx * ATT_K, ATT_K)
        vb = v_ref[pl.ds(row0, ATT_K), :]
        for hh in range(2):
            ps, psum = probs(s_scr[hh, kt_idx, t * ATT_K:, :], m_scr[hh, t * ATT_K:, :])
            l_scr[hh, t * ATT_K:, :] += psum
            acc_scr[hh, t * ATT_K:, :] += _dot(jnp.concatenate(ps, axis=1).astype(bf16), vb)

    lane = lax.broadcasted_iota(jnp.int32, (ATT_Q, 2 * HEAD_DIM), 1)
    o0 = acc_scr[0] / jnp.sum(l_scr[0], axis=-1, keepdims=True)
    o1 = acc_scr[1] / jnp.sum(l_scr[1], axis=-1, keepdims=True)
    o_ref[...] = jnp.where(lane < HEAD_DIM, o0, o1).astype(bf16)


def _attention(q, kt, v, batch, seq):
    T = q.shape[0]
    nq = seq // ATT_Q
    nk = seq // ATT_K
    pairs = N_HEADS // 2
    return pl.pallas_call(
        _attn_kernel,
        grid=(batch, pairs, nq),
        in_specs=[
            pl.BlockSpec((ATT_Q, 2 * HEAD_PAD), lambda b, p, i: (b * nq + i, p)),
            pl.BlockSpec((nk, 2 * HEAD_PAD, ATT_K), lambda b, p, i: (b, p, 0)),
            pl.BlockSpec((seq, 2 * HEAD_DIM), lambda b, p, i: (b, p)),
        ],
        out_specs=pl.BlockSpec((ATT_Q, 2 * HEAD_DIM), lambda b, p, i: (b * nq + i, p)),
        out_shape=jax.ShapeDtypeStruct((T, D_MODEL), bf16),
        scratch_shapes=[
            pltpu.VMEM((2, nk, ATT_Q, ATT_K), f32),
            pltpu.VMEM((2, ATT_Q, LANES), f32),
            pltpu.VMEM((2, ATT_Q, LANES), f32),
            pltpu.VMEM((2, ATT_Q, 2 * HEAD_DIM), f32),
        ],
        compiler_params=_params(("parallel", "parallel", "parallel")),
        name="attn",
    )(q, kt, v)


def _attn_out_kernel(x_ref, o_ref, sg_ref, wo_ref, mn_ref, wr_ref, x3_ref, lg_ref):
    for r0 in range(0, SGU_TILE, EPILOGUE_ROWS):
        rows = slice(r0, r0 + EPILOGUE_ROWS)
        gated = (o_ref[rows, :].astype(f32) * sg_ref[rows, :].astype(f32)).astype(bf16)
        x3 = x_ref[rows, :] + _dot(gated, wo_ref[...])
        x3_ref[rows, :] = x3
        lg_ref[:, rows] = _router_logits_t(x3, mn_ref[...], wr_ref[...])


def _attn_out(x2d, o, sg, wo, m_norm, w_router):
    T = x2d.shape[0]
    return pl.pallas_call(
        _attn_out_kernel,
        grid=(T // SGU_TILE,),
        in_specs=[
            pl.BlockSpec((SGU_TILE, D_MODEL), lambda i: (i, 0)),
            pl.BlockSpec((SGU_TILE, D_MODEL), lambda i: (i, 0)),
            pl.BlockSpec((SGU_TILE, D_MODEL), lambda i: (i, 0)),
            _const_spec((D_MODEL, D_MODEL)),
            _const_spec((1, D_MODEL)),
            _const_spec((2 * D_MODEL, 2 * LANES)),
        ],
        out_specs=[
            pl.BlockSpec((SGU_TILE, D_MODEL), lambda i: (i, 0)),
            pl.BlockSpec((ROUTER_ROWS, SGU_TILE), lambda i: (0, i)),
        ],
        out_shape=[
            jax.ShapeDtypeStruct((T, D_MODEL), f32),
            jax.ShapeDtypeStruct((ROUTER_ROWS, T), f32),
        ],
        compiler_params=_params(("parallel",)),
        name="attn_out",
    )(x2d, o, sg, wo, m_norm, w_router)


def _router_weight(w_group, w_expert):
    pad = jnp.zeros((D_MODEL, LANES - N_EXPERTS - N_GROUPS), f32)
    w = jnp.concatenate([w_expert, w_group, pad], axis=1)
    w_hi = w.astype(bf16)
    w_lo = (w - w_hi.astype(f32)).astype(bf16)
    top = jnp.concatenate([w_hi, w_lo], axis=1)
    bottom = jnp.concatenate([w_hi, jnp.zeros_like(w_hi)], axis=1)
    return jnp.concatenate([top, bottom], axis=0)


def kernel(x, a_norm, a_w_in, a_b_in, a_v_norm, a_w_s, a_b_s, a_w_out, kv_norm, kv_w, kv_b_f,
           k_norm, b_norm, b_w_qg, q_norm, b_w_out, m_norm, m_w_group, m_b_group, m_w_expert,
           m_b_expert, m_w1, m_w3, m_w2):
    batch, seq, _ = x.shape
    T = batch * seq
    x2d = x.reshape(T, D_MODEL)

    x1, lg0 = _sgu(
        x2d, a_norm[0].reshape(1, -1), a_w_in[0].astype(bf16), a_b_in[0].reshape(1, -1),
        a_v_norm[0].reshape(1, -1), a_w_s[0], a_b_s[0].T, a_w_out[0].astype(bf16),
        m_norm[0].reshape(1, -1), _router_weight(m_w_group[0], m_w_expert[0]))
    x2 = _moe(0, x1, lg0, m_norm[0].reshape(1, -1), m_b_group[0], m_b_expert[0],
              m_w1, m_w3, m_w2)

    wk = kv_w[:, :D_MODEL]
    wv = kv_w[:, D_MODEL:2 * D_MODEL]
    wf = jnp.pad(kv_w[:, 2 * D_MODEL:], ((0, 0), (0, LANES - N_HEADS)))
    bfv = jnp.pad(kv_b_f, (0, LANES - N_HEADS)).reshape(1, LANES)
    wq = b_w_qg[0][:, :D_MODEL]
    wg = b_w_qg[0][:, D_MODEL:]
    gq = jnp.pad(q_norm[0], (0, HEAD_PAD - HEAD_DIM)).reshape(1, HEAD_PAD)
    gk = k_norm.reshape(HEAD_DIM, 1)
    wq_pad = jnp.pad(wq.reshape(D_MODEL, N_HEADS, HEAD_DIM),
                     ((0, 0), (0, 0), (0, HEAD_PAD - HEAD_DIM))).reshape(D_MODEL, -1)
    q, kt, v, sg = _proj(
        x2, seq, kv_norm.reshape(1, -1), b_norm[0].reshape(1, -1),
        wk.T.astype(bf16), wv.astype(bf16), wf.astype(bf16), bfv,
        wq_pad.astype(bf16), wg.astype(bf16), gq, gk)
    o = _attention(q, kt, v, batch, seq)
    x3, lg1 = _attn_out(x2, o, sg, b_w_out[0].astype(bf16), m_norm[1].reshape(1, -1),
                        _router_weight(m_w_group[1], m_w_expert[1]))
    x4 = _moe(1, x3, lg1, m_norm[1].reshape(1, -1), m_b_group[1], m_b_expert[1],
              m_w1, m_w3, m_w2)
    return x4.reshape(batch, seq, D_MODEL)
```

```python
import functools
import math

import jax
import jax.numpy as jnp
import numpy as np
from jax import lax
from jax.experimental import pallas as pl
from jax.experimental.pallas import tpu as pltpu

D_MODEL = 1024
EPS = 1e-6
SGU_BLOCK = 128
SGU_CHUNK = 64
CHUNK_SHIFT = SGU_CHUNK.bit_length() - 1
SGU_WIDTH = 2 * D_MODEL
SGU_GROUPS = 8
SGU_GROUP_DIM = SGU_WIDTH // SGU_GROUPS
N_HEADS = 16
HEAD_DIM = D_MODEL // N_HEADS
N_GROUPS = 4
EXPERTS_PER_GROUP = 8
N_EXPERTS = N_GROUPS * EXPERTS_PER_GROUP
GROUP_SHIFT = EXPERTS_PER_GROUP.bit_length() - 1
TOP_K = 2
D_EXPERT = D_MODEL // 2

LANES = 128
ROW_SUBLANES = D_MODEL // 2 // LANES
ISSUE_UNROLL = 8
VMEM_LIMIT_BYTES = 56 * 1024 * 1024

ROW_TILE = 256
SGU_TILE = 512
EPILOGUE_ROWS = 256
ROUTE_TILE = 1024
EXPERT_ROWS = 512
MOVE_TILE = 512
MOVE_CHUNK = 64
ATT_Q = 1024
ATT_K = 256
PV_TILES = 2
HEAD_PAD = 2 * HEAD_DIM
ROUTER_ROWS = 40
AUX_ONE0 = HEAD_DIM
AUX_CUM0 = HEAD_DIM + 3

LOG2E = math.log2(math.e)

bf16 = jnp.bfloat16
f32 = jnp.float32
u32 = jnp.uint32
HI_HALF = np.uint32(0xFFFF0000)


def _dot(a, b, precision=None):
    return jnp.dot(a, b, preferred_element_type=f32, precision=precision)


def _dot_nt(a, b, precision=None):
    return lax.dot_general(a, b, (((1,), (1,)), ((), ())), preferred_element_type=f32,
                           precision=precision)


def _rms(x, g):
    ms = jnp.mean(x * x, axis=-1, keepdims=True)
    return x * lax.rsqrt(ms + EPS) * g


def _gelu_tanh(z):
    c = math.sqrt(2.0 / math.pi)
    return z * (0.5 * (1.0 + jnp.tanh(c * (z + 0.044715 * (z * z * z)))))


def _split3(c):
    hi = c.astype(bf16).astype(f32)
    r = c - hi
    mid = r.astype(bf16).astype(f32)
    lo = r - mid
    return hi, mid, lo


def _pack_rows(v):
    half = D_MODEL // 2
    lo = lax.bitcast_convert_type(v[:, :half].astype(bf16).astype(f32), u32) >> 16
    hi = lax.bitcast_convert_type(v[:, half:].astype(bf16).astype(f32), u32) & HI_HALF
    return (hi | lo).reshape(v.shape[0], ROW_SUBLANES, LANES)


def _unpack_rows(w):
    w = w.reshape(w.shape[0], D_MODEL // 2)
    lo = lax.bitcast_convert_type(w << 16, f32)
    hi = lax.bitcast_convert_type(w & HI_HALF, f32)
    return jnp.concatenate([lo, hi], axis=1)


def _const_spec(shape, single=False):
    nd = len(shape)
    mode = pl.Buffered(1) if single else None
    return pl.BlockSpec(shape, lambda *_: (0,) * nd, pipeline_mode=mode)


def _params(sem):
    return pltpu.CompilerParams(dimension_semantics=sem, vmem_limit_bytes=VMEM_LIMIT_BYTES)


def _router_logits_t(x_new, mn, wr3):
    hm = _rms(x_new, mn)
    h_hi = hm.astype(bf16)
    h_lo = (hm - h_hi.astype(f32)).astype(bf16)
    both = _dot(jnp.concatenate([h_hi, h_lo], axis=1), wr3)
    lg = both[:, :LANES] + both[:, LANES:]
    return lg.T[:ROUTER_ROWS, :]


def _sgu_kernel(x_ref, an_ref, win_ref, bin_ref, vn_ref, ws_ref, bst_ref, wout_ref, mn_ref,
                wr_ref, x1_ref, lg_ref, u_scr, v_scr, gated_scr):
    x = x_ref[...]
    h = _rms(x, an_ref[...]).astype(bf16)
    v = _gelu_tanh(_dot(h, win_ref[:, SGU_WIDTH:]) + bin_ref[:, SGU_WIDTH:])
    v_scr[...] = _rms(v, vn_ref[...]).astype(bf16)
    u_scr[...] = _gelu_tanh(_dot(h, win_ref[:, :SGU_WIDTH]) + bin_ref[:, :SGU_WIDTH])

    t_chunk = lax.broadcasted_iota(jnp.int32, (SGU_BLOCK, SGU_BLOCK), 0) >> CHUNK_SHIFT
    s_chunk = lax.broadcasted_iota(jnp.int32, (SGU_BLOCK, SGU_BLOCK), 1) >> CHUNK_SHIFT
    causal = t_chunk >= s_chunk
    wsm = [jnp.where(causal, ws_ref[g], 0.0).astype(bf16) for g in range(SGU_GROUPS)]
    for r0 in range(0, SGU_TILE, EPILOGUE_ROWS):
        for g in range(SGU_GROUPS):
            cols = slice(g * SGU_GROUP_DIM, (g + 1) * SGU_GROUP_DIM)
            for sb in range(r0 // SGU_BLOCK, (r0 + EPILOGUE_ROWS) // SGU_BLOCK):
                rows = slice(sb * SGU_BLOCK, (sb + 1) * SGU_BLOCK)
                mixed = _dot(wsm[g], v_scr[rows, cols]) + bst_ref[:, g:g + 1]
                gated_scr[rows, cols] = (u_scr[rows, cols] * mixed).astype(bf16)
        half = slice(r0, r0 + EPILOGUE_ROWS)
        x1 = x[half, :] + _dot(gated_scr[half, :], wout_ref[...])
        x1_ref[half, :] = x1
        lg_ref[:, half] = _router_logits_t(x1, mn_ref[...], wr_ref[...])


def _sgu(x2d, a_norm, w_in, b_in, v_norm, w_s, b_s_t, w_out, m_norm, w_router):
    T = x2d.shape[0]
    return pl.pallas_call(
        _sgu_kernel,
        grid=(T // SGU_TILE,),
        in_specs=[
            pl.BlockSpec((SGU_TILE, D_MODEL), lambda i: (i, 0)),
            _const_spec((1, D_MODEL)),
            _const_spec((D_MODEL, 2 * SGU_WIDTH), single=True),
            _const_spec((1, 2 * SGU_WIDTH)),
            _const_spec((1, SGU_WIDTH)),
            _const_spec((SGU_GROUPS, SGU_BLOCK, SGU_BLOCK)),
            _const_spec((SGU_BLOCK, SGU_GROUPS)),
            _const_spec((SGU_WIDTH, D_MODEL), single=True),
            _const_spec((1, D_MODEL)),
            _const_spec((2 * D_MODEL, 2 * LANES)),
        ],
        out_specs=[
            pl.BlockSpec((SGU_TILE, D_MODEL), lambda i: (i, 0)),
            pl.BlockSpec((ROUTER_ROWS, SGU_TILE), lambda i: (0, i)),
        ],
        out_shape=[
            jax.ShapeDtypeStruct((T, D_MODEL), f32),
            jax.ShapeDtypeStruct((ROUTER_ROWS, T), f32),
        ],
        scratch_shapes=[
            pltpu.VMEM((SGU_TILE, SGU_WIDTH), f32),
            pltpu.VMEM((SGU_TILE, SGU_WIDTH), bf16),
            pltpu.VMEM((SGU_TILE, SGU_WIDTH), bf16),
        ],
        compiler_params=_params(("parallel",)),
        name="sgu",
    )(x2d, a_norm, w_in, b_in, v_norm, w_s, b_s_t, w_out, m_norm, w_router)


def _route_kernel(lg_ref, be_ref, bg_ref, oi_ref, ow_ref, cnt_ref, carry_scr, before_scr):
    step = pl.program_id(0)
    tt = lg_ref.shape[1]

    @pl.when(step == 0)
    def _():
        carry_scr[...] = jnp.zeros_like(carry_scr)
        before_scr[...] = (lax.broadcasted_iota(jnp.int32, (tt, tt), 0)
                           < lax.broadcasted_iota(jnp.int32, (tt, tt), 1)).astype(bf16)

    lg = lg_ref[...]
    e_l = lg[0:N_EXPERTS, :] + be_ref[...]
    g_l = lg[N_EXPERTS:N_EXPERTS + 8, :] + bg_ref[...]
    g_row = lax.broadcasted_iota(jnp.int32, g_l.shape, 0).astype(f32)
    g_l = jnp.where(g_row < N_GROUPS, g_l, -jnp.inf)
    g_max = jnp.max(g_l, axis=0, keepdims=True)
    g_sel = jnp.min(jnp.where(g_l == g_max, g_row, 8.0), axis=0, keepdims=True)
    g_den = jnp.sum(jnp.exp(g_l - g_max), axis=0, keepdims=True)
    g_w = 1.0 / g_den

    e_row_i = lax.broadcasted_iota(jnp.int32, e_l.shape, 0)
    e_row = e_row_i.astype(f32)
    e_grp = (e_row_i >> GROUP_SHIFT).astype(f32)
    e_in = jnp.where(e_grp == g_sel, e_l, -jnp.inf)
    m1 = jnp.max(e_in, axis=0, keepdims=True)
    i1 = jnp.min(jnp.where(e_in == m1, e_row, float(N_EXPERTS)), axis=0, keepdims=True)
    e_in2 = jnp.where(e_row == i1, -jnp.inf, e_in)
    m2 = jnp.max(e_in2, axis=0, keepdims=True)
    i2 = jnp.min(jnp.where(e_in2 == m2, e_row, float(N_EXPERTS)), axis=0, keepdims=True)
    t = jnp.exp(m2 - m1)
    p1 = 1.0 / (1.0 + t)
    p2 = t / (1.0 + t)

    hit1 = e_row == i1
    hit2 = e_row == i2
    member = jnp.logical_or(hit1, hit2)
    prefix = _dot(member.astype(bf16), before_scr[...])
    rank_full = prefix + carry_scr[...]
    rank1 = jnp.sum(jnp.where(hit1, rank_full, 0.0), axis=0, keepdims=True)
    rank2 = jnp.sum(jnp.where(hit2, rank_full, 0.0), axis=0, keepdims=True)
    carry_new = carry_scr[...] + jnp.sum(member.astype(f32), axis=1, keepdims=True)
    carry_scr[...] = carry_new

    oi_ref[...] = jnp.zeros(oi_ref.shape, jnp.int32)
    oi_ref[0:1, :] = i1.astype(jnp.int32)
    oi_ref[1:2, :] = i2.astype(jnp.int32)
    oi_ref[2:3, :] = rank1.astype(jnp.int32)
    oi_ref[3:4, :] = rank2.astype(jnp.int32)
    ow_ref[...] = jnp.zeros(ow_ref.shape, f32)
    ow_ref[0:1, :] = g_w * p1
    ow_ref[1:2, :] = g_w * p2
    cnt_ref[...] = jnp.broadcast_to(carry_new, cnt_ref.shape)


def _route(logits_t, b_expert, b_group):
    T = logits_t.shape[1]
    return pl.pallas_call(
        _route_kernel,
        grid=(T // ROUTE_TILE,),
        in_specs=[
            pl.BlockSpec((ROUTER_ROWS, ROUTE_TILE), lambda i: (0, i)),
            _const_spec((N_EXPERTS, 1)),
            _const_spec((8, 1)),
        ],
        out_specs=[
            pl.BlockSpec((8, ROUTE_TILE), lambda i: (0, i)),
            pl.BlockSpec((8, ROUTE_TILE), lambda i: (0, i)),
            _const_spec((N_EXPERTS, LANES)),
        ],
        out_shape=[
            jax.ShapeDtypeStruct((8, T), jnp.int32),
            jax.ShapeDtypeStruct((8, T), f32),
            jax.ShapeDtypeStruct((N_EXPERTS, LANES), f32),
        ],
        scratch_shapes=[pltpu.VMEM((N_EXPERTS, 1), f32),
                        pltpu.VMEM((ROUTE_TILE, ROUTE_TILE), bf16)],
        compiler_params=_params(("arbitrary",)),
        name="route",
    )(logits_t, b_expert, b_group)


def _row_copy(src, s, dst, d, sem):
    return pltpu.make_async_copy(src.at[s], dst.at[d], sem)


def _wait_rows(buf, sem):
    pltpu.make_async_copy(buf, buf, sem).wait()


def _dispatch_kernel(n_blocks, pend_ref, padded_ref, nu_ref, dest_ref, x_ref, mn_ref, xs_hbm,
                     zero_scr, x3_scr, zsem, sem):
    @pl.when(pl.program_id(0) == 0)
    def _():
        zero_scr[...] = jnp.zeros_like(zero_scr)

        def fill(row0):
            return pltpu.make_async_copy(zero_scr, xs_hbm.at[pl.ds(row0, EXPERT_ROWS)], zsem)

        for e in range(N_EXPERTS):
            @pl.when(padded_ref[e] > 0)
            def _():
                fill(pend_ref[e] - EXPERT_ROWS).start()

            @pl.when(nu_ref[0] + e < n_blocks)
            def _():
                fill((nu_ref[0] + e) * EXPERT_ROWS).start()

        for e in range(N_EXPERTS):
            @pl.when(padded_ref[e] > 0)
            def _():
                fill(0).wait()

            @pl.when(nu_ref[0] + e < n_blocks)
            def _():
                fill(0).wait()

    i = pl.program_id(0)
    n_tiles = pl.num_programs(0) - 1

    def staged(rows):
        return _pack_rows(_rms(x_ref[rows, :], mn_ref[...]))

    @pl.when(i == 0)
    def _():
        x3_scr[0] = staged(slice(None))

    @pl.when(i > 0)
    def _():
        cur = (i - 1) % 3
        nxt = i % 3

        def chunk(c, carry):
            r0 = pl.multiple_of(c * MOVE_CHUNK, MOVE_CHUNK)
            for j in range(MOVE_CHUNK):
                for k in range(TOP_K):
                    _row_copy(x3_scr.at[cur], r0 + j, xs_hbm,
                              dest_ref[0, k * MOVE_TILE + r0 + j], sem.at[cur]).start(priority=k)
            x3_scr[nxt, pl.ds(r0, MOVE_CHUNK)] = staged(pl.ds(r0, MOVE_CHUNK))
            return carry

        lax.fori_loop(0, MOVE_TILE // MOVE_CHUNK, chunk, 0)

        @pl.when(i > 1)
        def _():
            for k in range(TOP_K):
                _wait_rows(x3_scr.at[(i - 2) % 3], sem.at[(i - 2) % 3])

        @pl.when(i == n_tiles)
        def _():
            for k in range(TOP_K):
                _wait_rows(x3_scr.at[cur], sem.at[cur])


def _dispatch(pend, padded, n_used, dest_tiles, x2d, m_norm, n_rows):
    T = x2d.shape[0]
    n_blocks = n_rows // EXPERT_ROWS
    n_tiles = T // MOVE_TILE
    grid_spec = pltpu.PrefetchScalarGridSpec(
        num_scalar_prefetch=3,
        grid=(n_tiles + 1,),
        in_specs=[
            pl.BlockSpec((None, 1, TOP_K * MOVE_TILE),
                         lambda i, *_: (jnp.maximum(i - 1, 0), 0, 0), memory_space=pltpu.SMEM),
            pl.BlockSpec((MOVE_TILE, D_MODEL), lambda i, *_: (jnp.minimum(i, n_tiles - 1), 0)),
            pl.BlockSpec((1, D_MODEL), lambda i, *_: (0, 0)),
        ],
        out_specs=pl.BlockSpec(memory_space=pl.ANY),
        scratch_shapes=[
            pltpu.VMEM((EXPERT_ROWS, ROW_SUBLANES, LANES), u32),
            pltpu.VMEM((3, MOVE_TILE, ROW_SUBLANES, LANES), u32),
            pltpu.SemaphoreType.DMA(()),
            pltpu.SemaphoreType.DMA((3,)),
        ],
    )
    return pl.pallas_call(
        functools.partial(_dispatch_kernel, n_blocks),
        grid_spec=grid_spec,
        out_shape=jax.ShapeDtypeStruct((n_rows, ROW_SUBLANES, LANES), u32),
        compiler_params=_params(("arbitrary",)),
        name="dispatch",
    )(pend, padded, n_used, dest_tiles, x2d, m_norm)


def _expert_kernel(n_blocks, ps_ref, nb_ref, nu_ref, xs_hbm, w1_ref, w3_ref, w2_ref,
                   ys_hbm, xbuf, ybuf, h_scr, w1_scr, w3_scr, w2_scr, xsem, ysem):
    e = pl.program_id(0)
    nblk = nb_ref[e]
    g0 = ps_ref[e]
    n_used = nu_ref[0]

    def rows(g):
        return pl.ds(pl.multiple_of(g * EXPERT_ROWS, EXPERT_ROWS), EXPERT_ROWS)

    def x_copy(g):
        return pltpu.make_async_copy(xs_hbm.at[rows(g)], xbuf.at[g % 2], xsem.at[g % 2])

    def y_copy(g):
        return pltpu.make_async_copy(ybuf.at[g % 2], ys_hbm.at[rows(g)], ysem.at[g % 2])

    def normed(slot):
        return _unpack_rows(xbuf[slot]).astype(bf16)

    @pl.when(e == 0)
    def _():
        x_copy(0).start(priority=1)
        x_copy(1).start(priority=1)
        x_copy(0).wait()
        h_scr[...] = normed(0)

    @pl.when(nblk > 0)
    def _():
        w1_scr[...] = w1_ref[...].astype(bf16)
        w3_scr[...] = w3_ref[...].astype(bf16)
        w2_scr[...] = w2_ref[...].astype(bf16)

        def body(g, h):
            @pl.when(g + 2 < n_used)
            def _():
                x_copy(g + 2).start(priority=1)

            @pl.when(g + 1 < n_used)
            def _():
                x_copy(g + 1).wait()

            @pl.when(g >= 2)
            def _():
                y_copy(g - 2).wait()

            h_next = normed((g + 1) % 2)
            a = _dot(h, w1_scr[...])
            c = _dot(h, w3_scr[...])
            hid = (a * jax.nn.sigmoid(a)) * c
            ybuf[g % 2] = _pack_rows(_dot(hid.astype(bf16), w2_scr[...]))
            y_copy(g).start(priority=1)
            return h_next

        h_scr[...] = lax.fori_loop(g0, g0 + nblk, body, h_scr[...])

    @pl.when(e == pl.num_programs(0) - 1)
    def _():
        y_copy(n_used - 2).wait()
        y_copy(n_used - 1).wait()
        ybuf[0] = jnp.zeros(ybuf.shape[1:], u32)

        def fill(b):
            r = pl.ds(pl.multiple_of(b * EXPERT_ROWS, EXPERT_ROWS), EXPERT_ROWS)
            return pltpu.make_async_copy(ybuf.at[0], ys_hbm.at[r], ysem.at[0])

        def start(b, c):
            fill(b).start()
            return c

        def wait(b, c):
            fill(b).wait()
            return c

        lax.fori_loop(nu_ref[0], n_blocks, start, 0)
        lax.fori_loop(nu_ref[0], n_blocks, wait, 0)


def _experts(layer, pstart, seg_blocks, n_used, xs, w1, w3, w2):
    n_rows = xs.shape[0]
    n_blocks = n_rows // EXPERT_ROWS

    def w_map(e, *_):
        return (layer, e, 0, 0)

    block_buf = pltpu.VMEM((2, EXPERT_ROWS, ROW_SUBLANES, LANES), u32)
    grid_spec = pltpu.PrefetchScalarGridSpec(
        num_scalar_prefetch=3,
        grid=(N_EXPERTS,),
        in_specs=[
            pl.BlockSpec(memory_space=pl.ANY),
            pl.BlockSpec((None, None, D_MODEL, D_EXPERT), w_map),
            pl.BlockSpec((None, None, D_MODEL, D_EXPERT), w_map),
            pl.BlockSpec((None, None, D_EXPERT, D_MODEL), w_map),
        ],
        out_specs=pl.BlockSpec(memory_space=pl.ANY),
        scratch_shapes=[
            block_buf,
            block_buf,
            pltpu.VMEM((EXPERT_ROWS, D_MODEL), bf16),
            pltpu.VMEM((D_MODEL, D_EXPERT), bf16),
            pltpu.VMEM((D_MODEL, D_EXPERT), bf16),
            pltpu.VMEM((D_EXPERT, D_MODEL), bf16),
            pltpu.SemaphoreType.DMA((2,)),
            pltpu.SemaphoreType.DMA((2,)),
        ],
    )
    return pl.pallas_call(
        functools.partial(_expert_kernel, n_blocks),
        grid_spec=grid_spec,
        out_shape=jax.ShapeDtypeStruct((n_rows, ROW_SUBLANES, LANES), u32),
        compiler_params=_params(("arbitrary",)),
        name="experts",
    )(pstart, seg_blocks, n_used, xs, w1, w3, w2)


def _combine_kernel(dest_ref, next_ref, x_ref, w_ref, ys_hbm, out_ref, buf, sem):
    i = pl.program_id(0)
    slot = i % 2

    def gather(idx_ref, s):
        def start(j, c):
            for k in range(TOP_K):
                _row_copy(ys_hbm, idx_ref[0, k * MOVE_TILE + j], buf.at[s, k], j,
                          sem.at[s]).start(priority=k)
            return c

        lax.fori_loop(0, MOVE_TILE, start, 0, unroll=ISSUE_UNROLL)

    @pl.when(i == 0)
    def _():
        gather(dest_ref, slot)

    @pl.when(i + 1 < pl.num_programs(0))
    def _():
        gather(next_ref, 1 - slot)

    for k in range(TOP_K):
        _wait_rows(buf.at[slot, k], sem.at[slot])
    w = w_ref[...]
    y0 = _unpack_rows(buf[slot, 0])
    y1 = _unpack_rows(buf[slot, 1])
    out_ref[...] = x_ref[...] + (w[:, 0:1] * y0 + w[:, 1:2] * y1)


def _combine(dest_tiles, x2d, w_cols, ys):
    T = x2d.shape[0]
    n_tiles = T // MOVE_TILE
    return pl.pallas_call(
        _combine_kernel,
        grid=(n_tiles,),
        in_specs=[
            pl.BlockSpec((None, 1, TOP_K * MOVE_TILE), lambda i: (i, 0, 0),
                         memory_space=pltpu.SMEM),
            pl.BlockSpec((None, 1, TOP_K * MOVE_TILE),
                         lambda i: (jnp.minimum(i + 1, n_tiles - 1), 0, 0),
                         memory_space=pltpu.SMEM),
            pl.BlockSpec((MOVE_TILE, D_MODEL), lambda i: (i, 0)),
            pl.BlockSpec((MOVE_TILE, 8), lambda i: (i, 0)),
            pl.BlockSpec(memory_space=pl.ANY),
        ],
        out_specs=pl.BlockSpec((MOVE_TILE, D_MODEL), lambda i: (i, 0)),
        out_shape=jax.ShapeDtypeStruct((T, D_MODEL), f32),
        scratch_shapes=[
            pltpu.VMEM((2, TOP_K, MOVE_TILE, ROW_SUBLANES, LANES), u32),
            pltpu.SemaphoreType.DMA((2,)),
        ],
        compiler_params=_params(("arbitrary",)),
        name="combine",
    )(dest_tiles, dest_tiles, x2d, w_cols, ys)


def _moe(layer, x2d, logits_t, m_norm, b_group, b_expert, w1, w3, w2):
    T = x2d.shape[0]
    n_blocks = (T * TOP_K) // EXPERT_ROWS + N_EXPERTS
    n_rows = n_blocks * EXPERT_ROWS

    be = b_expert.reshape(N_EXPERTS, 1)
    bg = jnp.concatenate([b_group, jnp.zeros((8 - N_GROUPS,), f32)]).reshape(8, 1)
    oi, ow, cnt = _route(logits_t, be, bg)

    counts = cnt[:, 0].astype(jnp.int32)
    padded = (counts + EXPERT_ROWS - 1) // EXPERT_ROWS * EXPERT_ROWS
    pend = jnp.cumsum(padded)
    pstart = pend - padded
    n_used = (pend[-1] // EXPERT_ROWS).astype(jnp.int32).reshape(1)
    e_ids = jnp.arange(N_EXPERTS, dtype=jnp.int32)[:, None, None]
    seg_start = jnp.sum(jnp.where(oi[None, 0:2] == e_ids, pstart[:, None, None], 0), axis=0)
    dest = seg_start + oi[2:4]
    dest_tiles = dest.reshape(TOP_K, T // MOVE_TILE, MOVE_TILE).transpose(1, 0, 2).reshape(
        T // MOVE_TILE, 1, TOP_K * MOVE_TILE)
    w_cols = ow.T

    xs = _dispatch(pend.astype(jnp.int32), padded, n_used, dest_tiles, x2d, m_norm, n_rows)
    ys = _experts(layer, (pstart // EXPERT_ROWS).astype(jnp.int32), padded // EXPERT_ROWS,
                  n_used, xs, w1, w3, w2)
    return _combine(dest_tiles, x2d, w_cols, ys)


def _log_sigmoid(x):
    return jnp.minimum(x, 0.0) - jnp.log1p(jnp.exp(-jnp.abs(x)))


def _proj_kernel(tiles_per_seq, x_ref, kvn_ref, bn_ref, wkt_ref, wv_ref, wf_ref, bf_ref,
                 wq_ref, wg_ref, gq_ref, gk_ref, q_ref, kt_ref, v_ref, sg_ref, carry_scr):
    i = pl.program_id(0)
    x = x_ref[...]
    hkv = _rms(x, kvn_ref[...]).astype(bf16)
    hq = _rms(x, bn_ref[...]).astype(bf16)

    v_ref[...] = _dot(hkv, wv_ref[...]).astype(bf16)
    sg_ref[...] = jax.nn.sigmoid(_dot(hq, wg_ref[...])).astype(bf16)

    logf = _log_sigmoid(_dot(hkv, wf_ref[...]) + bf_ref[...])
    tm = x.shape[0]
    incl = (lax.broadcasted_iota(jnp.int32, (tm, tm), 0)
            >= lax.broadcasted_iota(jnp.int32, (tm, tm), 1)).astype(bf16)

    @pl.when(i % tiles_per_seq == 0)
    def _():
        carry_scr[...] = jnp.zeros_like(carry_scr)

    parts = _dot(incl, jnp.concatenate(_split3(logf), axis=1).astype(bf16))
    cum = (parts[:, :LANES] + parts[:, LANES:2 * LANES] + parts[:, 2 * LANES:]
           + carry_scr[...])
    carry_scr[...] = cum[tm - 1:tm, :]
    cum = cum * LOG2E
    cum_t = cum.T

    lane = lax.broadcasted_iota(jnp.int32, (tm, HEAD_PAD), 1)
    sub = lax.broadcasted_iota(jnp.int32, (HEAD_DIM, tm), 0)
    q_raw = _dot(hq, wq_ref[...])
    k_raw_t = _dot_nt(wkt_ref[...], hkv)
    scale = HEAD_DIM ** -0.5 * LOG2E

    for h in range(N_HEADS):
        sl = slice(h * HEAD_PAD, (h + 1) * HEAD_PAD)
        qb = q_raw[:, sl]
        q_ms = jnp.sum(qb * qb, axis=-1, keepdims=True) * (1.0 / HEAD_DIM)
        qn = qb * lax.rsqrt(q_ms + EPS) * gq_ref[...] * scale
        c_hi, c_mid, c_lo = _split3(cum[:, h:h + 1])
        qa = jnp.where(lane < HEAD_DIM, qn,
             jnp.where(lane < AUX_CUM0, 1.0,
             jnp.where(lane == AUX_CUM0, c_hi,
             jnp.where(lane == AUX_CUM0 + 1, c_mid,
             jnp.where(lane == AUX_CUM0 + 2, c_lo, 0.0)))))
        q_ref[:, sl] = qa.astype(bf16)

        kb = k_raw_t[h * HEAD_DIM:(h + 1) * HEAD_DIM, :]
        k_ms = jnp.sum(kb * kb, axis=0, keepdims=True) * (1.0 / HEAD_DIM)
        kn = kb * lax.rsqrt(k_ms + EPS) * gk_ref[...]
        t_hi, t_mid, t_lo = _split3(cum_t[h:h + 1, :])
        aux = jnp.where(sub == 0, -t_hi,
              jnp.where(sub == 1, -t_mid,
              jnp.where(sub == 2, -t_lo,
              jnp.where(sub < 6, 1.0, 0.0))))
        sl = slice(h * HEAD_PAD, (h + 1) * HEAD_PAD)
        ka = jnp.concatenate([kn, aux], axis=0).astype(bf16)
        for t in range(tm // ATT_K):
            kt_ref[t, sl, :] = ka[:, t * ATT_K:(t + 1) * ATT_K]


def _proj(x2d, seq, kv_norm, b_norm, wkt, wv, wf, bfv, wq, wg, gq, gk):
    T = x2d.shape[0]
    n_tiles = T // ROW_TILE
    qw = N_HEADS * HEAD_PAD
    return pl.pallas_call(
        functools.partial(_proj_kernel, seq // ROW_TILE),
        grid=(n_tiles,),
        in_specs=[
            pl.BlockSpec((ROW_TILE, D_MODEL), lambda i: (i, 0)),
            _const_spec((1, D_MODEL)),
            _const_spec((1, D_MODEL)),
            _const_spec((D_MODEL, D_MODEL), single=True),
            _const_spec((D_MODEL, D_MODEL), single=True),
            _const_spec((D_MODEL, LANES)),
            _const_spec((1, LANES)),
            _const_spec((D_MODEL, qw), single=True),
            _const_spec((D_MODEL, D_MODEL), single=True),
            _const_spec((1, HEAD_PAD)),
            _const_spec((HEAD_DIM, 1)),
        ],
        out_specs=[
            pl.BlockSpec((ROW_TILE, qw), lambda i: (i, 0)),
            pl.BlockSpec((ROW_TILE // ATT_K, qw, ATT_K), lambda i: (i, 0, 0)),
            pl.BlockSpec((ROW_TILE, D_MODEL), lambda i: (i, 0)),
            pl.BlockSpec((ROW_TILE, D_MODEL), lambda i: (i, 0)),
        ],
        out_shape=[
            jax.ShapeDtypeStruct((T, qw), bf16),
            jax.ShapeDtypeStruct((T // ATT_K, qw, ATT_K), bf16),
            jax.ShapeDtypeStruct((T, D_MODEL), bf16),
            jax.ShapeDtypeStruct((T, D_MODEL), bf16),
        ],
        scratch_shapes=[pltpu.VMEM((1, LANES), f32)],
        compiler_params=_params(("arbitrary",)),
        name="proj",
    )(x2d, kv_norm, b_norm, wkt, wv, wf, bfv, wq, wg, gq, gk)


def _attn_kernel(q_ref, kt_ref, v_ref, o_ref, s_scr, m_scr, l_scr, acc_scr):
    qi = pl.program_id(2)
    tiles = ATT_Q // ATT_K
    groups = ATT_K // LANES
    heads = [slice(hh * HEAD_PAD, (hh + 1) * HEAD_PAD) for hh in range(2)]
    m_scr[...] = jnp.full(m_scr.shape, -jnp.inf, f32)

    def lane_max(mx, s):
        for g in range(groups):
            mx = jnp.maximum(mx, s[:, g * LANES:(g + 1) * LANES])
        return mx

    def probs(s, mb):
        ps = [jnp.exp2(s[:, g * LANES:(g + 1) * LANES] - mb) for g in range(groups)]
        return ps, functools.reduce(lambda a, b: a + b, ps)

    def body_a(trip, c):
        for t in range(tiles):
            kt_idx = trip * tiles + t
            for hh in range(2):
                s = _dot(q_ref[:, heads[hh]], kt_ref[kt_idx, heads[hh], :])
                s_scr[hh, kt_idx] = s
                m_scr[hh] = lane_max(m_scr[hh], s)
        return c

    lax.fori_loop(0, qi, body_a, 0)
    for t in range(tiles):
        kt_idx = qi * tiles + t
        rows = ATT_Q - t * ATT_K
        visible = (lax.broadcasted_iota(jnp.int32, (rows, ATT_K), 1)
                   <= lax.broadcasted_iota(jnp.int32, (rows, ATT_K), 0))
        for hh in range(2):
            s = _dot(q_ref[t * ATT_K:, heads[hh]], kt_ref[kt_idx, heads[hh], :])
            s = jnp.where(visible, s, -jnp.inf)
            s_scr[hh, kt_idx, t * ATT_K:, :] = s
            m_scr[hh, t * ATT_K:, :] = lane_max(m_scr[hh, t * ATT_K:, :], s)

    for hh in range(2):
        row_max = jnp.max(m_scr[hh], axis=-1, keepdims=True)
        m_scr[hh] = jnp.broadcast_to(row_max, (ATT_Q, LANES))
    l_scr[...] = jnp.zeros(l_scr.shape, f32)
    acc_scr[...] = jnp.zeros(acc_scr.shape, f32)

    def body_b(trip, c):
        for hh in range(2):
            mb = m_scr[hh]
            lsum, acc = l_scr[hh], acc_scr[hh]
            for t0 in range(0, tiles, PV_TILES):
                kt_idx = trip * tiles + t0
                row0 = pl.multiple_of(kt_idx * ATT_K, PV_TILES * ATT_K)
                vb = v_ref[pl.ds(row0, PV_TILES * ATT_K), :]
                parts = []
                for t in range(PV_TILES):
                    ps, psum = probs(s_scr[hh, kt_idx + t], mb)
                    parts += ps
                    lsum = lsum + psum
                acc = acc + _dot(jnp.concatenate(parts, axis=1).astype(bf16), vb)
            l_scr[hh] = lsum
            acc_scr[hh] = acc
        return c

    lax.fori_loop(0, qi, body_b, 0)
    for t in range(tiles):
        kt_idx = qi * tiles + t
        row0 = pl.multiple_of(kt_idx * ATT_K, ATT_K)
        vb = v_ref[pl.ds(row0, ATT_K), :]
        for hh in range(2):
            ps, psum = probs(s_scr[hh, kt_idx, t * ATT_K:, :], m_scr[hh, t * ATT_K:, :])
            l_scr[hh, t * ATT_K:, :] += psum
            acc_scr[hh, t * ATT_K:, :] += _dot(jnp.concatenate(ps, axis=1).astype(bf16), vb)

    lane = lax.broadcasted_iota(jnp.int32, (ATT_Q, 2 * HEAD_DIM), 1)
    o0 = acc_scr[0] / jnp.sum(l_scr[0], axis=-1, keepdims=True)
    o1 = acc_scr[1] / jnp.sum(l_scr[1], axis=-1, keepdims=True)
    o_ref[...] = jnp.where(lane < HEAD_DIM, o0, o1).astype(bf16)


def _attention(q, kt, v, batch, seq):
    T = q.shape[0]
    nq = seq // ATT_Q
    nk = seq // ATT_K
    pairs = N_HEADS // 2
    return pl.pallas_call(
        _attn_kernel,
        grid=(batch, pairs, nq),
        in_specs=[
            pl.BlockSpec((ATT_Q, 2 * HEAD_PAD), lambda b, p, i: (b * nq + i, p)),
            pl.BlockSpec((nk, 2 * HEAD_PAD, ATT_K), lambda b, p, i: (b, p, 0)),
            pl.BlockSpec((seq, 2 * HEAD_DIM), lambda b, p, i: (b, p)),
        ],
        out_specs=pl.BlockSpec((ATT_Q, 2 * HEAD_DIM), lambda b, p, i: (b * nq + i, p)),
        out_shape=jax.ShapeDtypeStruct((T, D_MODEL), bf16),
        scratch_shapes=[
            pltpu.VMEM((2, nk, ATT_Q, ATT_K), f32),
            pltpu.VMEM((2, ATT_Q, LANES), f32),
            pltpu.VMEM((2, ATT_Q, LANES), f32),
            pltpu.VMEM((2, ATT_Q, 2 * HEAD_DIM), f32),
        ],
        compiler_params=_params(("parallel", "parallel", "parallel")),
        name="attn",
    )(q, kt, v)


def _attn_out_kernel(x_ref, o_ref, sg_ref, wo_ref, mn_ref, wr_ref, x3_ref, lg_ref):
    for r0 in range(0, SGU_TILE, EPILOGUE_ROWS):
        rows = slice(r0, r0 + EPILOGUE_ROWS)
        gated = (o_ref[rows, :].astype(f32) * sg_ref[rows, :].astype(f32)).astype(bf16)
        x3 = x_ref[rows, :] + _dot(gated, wo_ref[...])
        x3_ref[rows, :] = x3
        lg_ref[:, rows] = _router_logits_t(x3, mn_ref[...], wr_ref[...])


def _attn_out(x2d, o, sg, wo, m_norm, w_router):
    T = x2d.shape[0]
    return pl.pallas_call(
        _attn_out_kernel,
        grid=(T // SGU_TILE,),
        in_specs=[
            pl.BlockSpec((SGU_TILE, D_MODEL), lambda i: (i, 0)),
            pl.BlockSpec((SGU_TILE, D_MODEL), lambda i: (i, 0)),
            pl.BlockSpec((SGU_TILE, D_MODEL), lambda i: (i, 0)),
            _const_spec((D_MODEL, D_MODEL)),
            _const_spec((1, D_MODEL)),
            _const_spec((2 * D_MODEL, 2 * LANES)),
        ],
        out_specs=[
            pl.BlockSpec((SGU_TILE, D_MODEL), lambda i: (i, 0)),
            pl.BlockSpec((ROUTER_ROWS, SGU_TILE), lambda i: (0, i)),
        ],
        out_shape=[
            jax.ShapeDtypeStruct((T, D_MODEL), f32),
            jax.ShapeDtypeStruct((ROUTER_ROWS, T), f32),
        ],
        compiler_params=_params(("parallel",)),
        name="attn_out",
    )(x2d, o, sg, wo, m_norm, w_router)


def _router_weight(w_group, w_expert):
    pad = jnp.zeros((D_MODEL, LANES - N_EXPERTS - N_GROUPS), f32)
    w = jnp.concatenate([w_expert, w_group, pad], axis=1)
    w_hi = w.astype(bf16)
    w_lo = (w - w_hi.astype(f32)).astype(bf16)
    top = jnp.concatenate([w_hi, w_lo], axis=1)
    bottom = jnp.concatenate([w_hi, jnp.zeros_like(w_hi)], axis=1)
    return jnp.concatenate([top, bottom], axis=0)


def kernel(x, a_norm, a_w_in, a_b_in, a_v_norm, a_w_s, a_b_s, a_w_out, kv_norm, kv_w, kv_b_f,
           k_norm, b_norm, b_w_qg, q_norm, b_w_out, m_norm, m_w_group, m_b_group, m_w_expert,
           m_b_expert, m_w1, m_w3, m_w2):
    batch, seq, _ = x.shape
    T = batch * seq
    x2d = x.reshape(T, D_MODEL)

    x1, lg0 = _sgu(
        x2d, a_norm[0].reshape(1, -1), a_w_in[0].astype(bf16), a_b_in[0].reshape(1, -1),
        a_v_norm[0].reshape(1, -1), a_w_s[0], a_b_s[0].T, a_w_out[0].astype(bf16),
        m_norm[0].reshape(1, -1), _router_weight(m_w_group[0], m_w_expert[0]))
    x2 = _moe(0, x1, lg0, m_norm[0].reshape(1, -1), m_b_group[0], m_b_expert[0],
              m_w1, m_w3, m_w2)

    wk = kv_w[:, :D_MODEL]
    wv = kv_w[:, D_MODEL:2 * D_MODEL]
    wf = jnp.pad(kv_w[:, 2 * D_MODEL:], ((0, 0), (0, LANES - N_HEADS)))
    bfv = jnp.pad(kv_b_f, (0, LANES - N_HEADS)).reshape(1, LANES)
    wq = b_w_qg[0][:, :D_MODEL]
    wg = b_w_qg[0][:, D_MODEL:]
    gq = jnp.pad(q_norm[0], (0, HEAD_PAD - HEAD_DIM)).reshape(1, HEAD_PAD)
    gk = k_norm.reshape(HEAD_DIM, 1)
    wq_pad = jnp.pad(wq.reshape(D_MODEL, N_HEADS, HEAD_DIM),
                     ((0, 0), (0, 0), (0, HEAD_PAD - HEAD_DIM))).reshape(D_MODEL, -1)
    q, kt, v, sg = _proj(
        x2, seq, kv_norm.reshape(1, -1), b_norm[0].reshape(1, -1),
        wk.T.astype(bf16), wv.astype(bf16), wf.astype(bf16), bfv,
        wq_pad.astype(bf16), wg.astype(bf16), gq, gk)
    o = _attention(q, kt, v, batch, seq)
    x3, lg1 = _attn_out(x2, o, sg, b_w_out[0].astype(bf16), m_norm[1].reshape(1, -1),
                        _router_weight(m_w_group[1], m_w_expert[1]))
    x4 = _moe(1, x3, lg1, m_norm[1].reshape(1, -1), m_b_group[1], m_b_expert[1],
              m_w1, m_w3, m_w2)
    return x4.reshape(batch, seq, D_MODEL)
```

```python
import functools
import math

import jax
import jax.numpy as jnp
import numpy as np
from jax import lax
from jax.experimental import pallas as pl
from jax.experimental.pallas import tpu as pltpu

D_MODEL = 1024
EPS = 1e-6
SGU_BLOCK = 128
SGU_CHUNK = 64
CHUNK_SHIFT = SGU_CHUNK.bit_length() - 1
SGU_WIDTH = 2 * D_MODEL
SGU_GROUPS = 8
SGU_GROUP_DIM = SGU_WIDTH // SGU_GROUPS
N_HEADS = 16
HEAD_DIM = D_MODEL // N_HEADS
N_GROUPS = 4
EXPERTS_PER_GROUP = 8
N_EXPERTS = N_GROUPS * EXPERTS_PER_GROUP
GROUP_SHIFT = EXPERTS_PER_GROUP.bit_length() - 1
TOP_K = 2
D_EXPERT = D_MODEL // 2

LANES = 128
ROW_SUBLANES = D_MODEL // 2 // LANES
ISSUE_UNROLL = 8
VMEM_LIMIT_BYTES = 56 * 1024 * 1024

ROW_TILE = 256
SGU_TILE = 512
EPILOGUE_ROWS = 256
ROUTE_TILE = 1024
EXPERT_ROWS = 512
MOVE_TILE = 512
MOVE_CHUNK = 64
ATT_Q = 1024
ATT_K = 256
PV_TILES = 2
HEAD_PAD = 2 * HEAD_DIM
ROUTER_ROWS = 40
AUX_ONE0 = HEAD_DIM
AUX_CUM0 = HEAD_DIM + 3

LOG2E = math.log2(math.e)

bf16 = jnp.bfloat16
f32 = jnp.float32
u32 = jnp.uint32
HI_HALF = np.uint32(0xFFFF0000)


def _dot(a, b, precision=None):
    return jnp.dot(a, b, preferred_element_type=f32, precision=precision)


def _dot_nt(a, b, precision=None):
    return lax.dot_general(a, b, (((1,), (1,)), ((), ())), preferred_element_type=f32,
                           precision=precision)


def _rms(x, g):
    ms = jnp.mean(x * x, axis=-1, keepdims=True)
    return x * lax.rsqrt(ms + EPS) * g


def _gelu_tanh(z):
    c = math.sqrt(2.0 / math.pi)
    return z * (0.5 * (1.0 + jnp.tanh(c * (z + 0.044715 * (z * z * z)))))


def _split3(c):
    hi = c.astype(bf16).astype(f32)
    r = c - hi
    mid = r.astype(bf16).astype(f32)
    lo = r - mid
    return hi, mid, lo


def _pack_rows(v):
    half = D_MODEL // 2
    lo = lax.bitcast_convert_type(v[:, :half].astype(bf16).astype(f32), u32) >> 16
    hi = lax.bitcast_convert_type(v[:, half:].astype(bf16).astype(f32), u32) & HI_HALF
    return (hi | lo).reshape(v.shape[0], ROW_SUBLANES, LANES)


def _unpack_rows(w):
    w = w.reshape(w.shape[0], D_MODEL // 2)
    lo = lax.bitcast_convert_type(w << 16, f32)
    hi = lax.bitcast_convert_type(w & HI_HALF, f32)
    return jnp.concatenate([lo, hi], axis=1)


def _const_spec(shape, single=False):
    nd = len(shape)
    mode = pl.Buffered(1) if single else None
    return pl.BlockSpec(shape, lambda *_: (0,) * nd, pipeline_mode=mode)


def _params(sem):
    return pltpu.CompilerParams(dimension_semantics=sem, vmem_limit_bytes=VMEM_LIMIT_BYTES)


def _router_logits_t(x_new, mn, wr3):
    hm = _rms(x_new, mn)
    h_hi = hm.astype(bf16)
    h_lo = (hm - h_hi.astype(f32)).astype(bf16)
    both = _dot(jnp.concatenate([h_hi, h_lo], axis=1), wr3)
    lg = both[:, :LANES] + both[:, LANES:]
    return lg.T[:ROUTER_ROWS, :]


def _sgu_kernel(x_ref, an_ref, win_ref, bin_ref, vn_ref, ws_ref, bst_ref, wout_ref, mn_ref,
                wr_ref, x1_ref, lg_ref, u_scr, v_scr, gated_scr):
    x = x_ref[...]
    h = _rms(x, an_ref[...]).astype(bf16)
    v = _gelu_tanh(_dot(h, win_ref[:, SGU_WIDTH:]) + bin_ref[:, SGU_WIDTH:])
    v_scr[...] = _rms(v, vn_ref[...]).astype(bf16)
    u_scr[...] = _gelu_tanh(_dot(h, win_ref[:, :SGU_WIDTH]) + bin_ref[:, :SGU_WIDTH])

    t_chunk = lax.broadcasted_iota(jnp.int32, (SGU_BLOCK, SGU_BLOCK), 0) >> CHUNK_SHIFT
    s_chunk = lax.broadcasted_iota(jnp.int32, (SGU_BLOCK, SGU_BLOCK), 1) >> CHUNK_SHIFT
    causal = t_chunk >= s_chunk
    wsm = [jnp.where(causal, ws_ref[g], 0.0).astype(bf16) for g in range(SGU_GROUPS)]
    for r0 in range(0, SGU_TILE, EPILOGUE_ROWS):
        for g in range(SGU_GROUPS):
            cols = slice(g * SGU_GROUP_DIM, (g + 1) * SGU_GROUP_DIM)
            for sb in range(r0 // SGU_BLOCK, (r0 + EPILOGUE_ROWS) // SGU_BLOCK):
                rows = slice(sb * SGU_BLOCK, (sb + 1) * SGU_BLOCK)
                mixed = _dot(wsm[g], v_scr[rows, cols]) + bst_ref[:, g:g + 1]
                gated_scr[rows, cols] = (u_scr[rows, cols] * mixed).astype(bf16)
        half = slice(r0, r0 + EPILOGUE_ROWS)
        x1 = x[half, :] + _dot(gated_scr[half, :], wout_ref[...])
        x1_ref[half, :] = x1
        lg_ref[:, half] = _router_logits_t(x1, mn_ref[...], wr_ref[...])


def _sgu(x2d, a_norm, w_in, b_in, v_norm, w_s, b_s_t, w_out, m_norm, w_router):
    T = x2d.shape[0]
    return pl.pallas_call(
        _sgu_kernel,
        grid=(T // SGU_TILE,),
        in_specs=[
            pl.BlockSpec((SGU_TILE, D_MODEL), lambda i: (i, 0)),
            _const_spec((1, D_MODEL)),
            _const_spec((D_MODEL, 2 * SGU_WIDTH), single=True),
            _const_spec((1, 2 * SGU_WIDTH)),
            _const_spec((1, SGU_WIDTH)),
            _const_spec((SGU_GROUPS, SGU_BLOCK, SGU_BLOCK)),
            _const_spec((SGU_BLOCK, SGU_GROUPS)),
            _const_spec((SGU_WIDTH, D_MODEL), single=True),
            _const_spec((1, D_MODEL)),
            _const_spec((2 * D_MODEL, 2 * LANES)),
        ],
        out_specs=[
            pl.BlockSpec((SGU_TILE, D_MODEL), lambda i: (i, 0)),
            pl.BlockSpec((ROUTER_ROWS, SGU_TILE), lambda i: (0, i)),
        ],
        out_shape=[
            jax.ShapeDtypeStruct((T, D_MODEL), f32),
            jax.ShapeDtypeStruct((ROUTER_ROWS, T), f32),
        ],
        scratch_shapes=[
            pltpu.VMEM((SGU_TILE, SGU_WIDTH), f32),
            pltpu.VMEM((SGU_TILE, SGU_WIDTH), bf16),
            pltpu.VMEM((SGU_TILE, SGU_WIDTH), bf16),
        ],
        compiler_params=_params(("parallel",)),
        name="sgu",
    )(x2d, a_norm, w_in, b_in, v_norm, w_s, b_s_t, w_out, m_norm, w_router)


def _route_kernel(lg_ref, be_ref, bg_ref, oi_ref, ow_ref, cnt_ref, carry_scr, before_scr):
    step = pl.program_id(0)
    tt = lg_ref.shape[1]

    @pl.when(step == 0)
    def _():
        carry_scr[...] = jnp.zeros_like(carry_scr)
        before_scr[...] = (lax.broadcasted_iota(jnp.int32, (tt, tt), 0)
                           < lax.broadcasted_iota(jnp.int32, (tt, tt), 1)).astype(bf16)

    lg = lg_ref[...]
    e_l = lg[0:N_EXPERTS, :] + be_ref[...]
    g_l = lg[N_EXPERTS:N_EXPERTS + 8, :] + bg_ref[...]
    g_row = lax.broadcasted_iota(jnp.int32, g_l.shape, 0).astype(f32)
    g_l = jnp.where(g_row < N_GROUPS, g_l, -jnp.inf)
    g_max = jnp.max(g_l, axis=0, keepdims=True)
    g_sel = jnp.min(jnp.where(g_l == g_max, g_row, 8.0), axis=0, keepdims=True)
    g_den = jnp.sum(jnp.exp(g_l - g_max), axis=0, keepdims=True)
    g_w = 1.0 / g_den

    e_row_i = lax.broadcasted_iota(jnp.int32, e_l.shape, 0)
    e_row = e_row_i.astype(f32)
    e_grp = (e_row_i >> GROUP_SHIFT).astype(f32)
    e_in = jnp.where(e_grp == g_sel, e_l, -jnp.inf)
    m1 = jnp.max(e_in, axis=0, keepdims=True)
    i1 = jnp.min(jnp.where(e_in == m1, e_row, float(N_EXPERTS)), axis=0, keepdims=True)
    e_in2 = jnp.where(e_row == i1, -jnp.inf, e_in)
    m2 = jnp.max(e_in2, axis=0, keepdims=True)
    i2 = jnp.min(jnp.where(e_in2 == m2, e_row, float(N_EXPERTS)), axis=0, keepdims=True)
    t = jnp.exp(m2 - m1)
    p1 = 1.0 / (1.0 + t)
    p2 = t / (1.0 + t)

    hit1 = e_row == i1
    hit2 = e_row == i2
    member = jnp.logical_or(hit1, hit2)
    prefix = _dot(member.astype(bf16), before_scr[...])
    rank_full = prefix + carry_scr[...]
    rank1 = jnp.sum(jnp.where(hit1, rank_full, 0.0), axis=0, keepdims=True)
    rank2 = jnp.sum(jnp.where(hit2, rank_full, 0.0), axis=0, keepdims=True)
    carry_new = carry_scr[...] + jnp.sum(member.astype(f32), axis=1, keepdims=True)
    carry_scr[...] = carry_new

    oi_ref[...] = jnp.zeros(oi_ref.shape, jnp.int32)
    oi_ref[0:1, :] = i1.astype(jnp.int32)
    oi_ref[1:2, :] = i2.astype(jnp.int32)
    oi_ref[2:3, :] = rank1.astype(jnp.int32)
    oi_ref[3:4, :] = rank2.astype(jnp.int32)
    ow_ref[...] = jnp.zeros(ow_ref.shape, f32)
    ow_ref[0:1, :] = g_w * p1
    ow_ref[1:2, :] = g_w * p2
    cnt_ref[...] = jnp.broadcast_to(carry_new, cnt_ref.shape)


def _route(logits_t, b_expert, b_group):
    T = logits_t.shape[1]
    return pl.pallas_call(
        _route_kernel,
        grid=(T // ROUTE_TILE,),
        in_specs=[
            pl.BlockSpec((ROUTER_ROWS, ROUTE_TILE), lambda i: (0, i)),
            _const_spec((N_EXPERTS, 1)),
            _const_spec((8, 1)),
        ],
        out_specs=[
            pl.BlockSpec((8, ROUTE_TILE), lambda i: (0, i)),
            pl.BlockSpec((8, ROUTE_TILE), lambda i: (0, i)),
            _const_spec((N_EXPERTS, LANES)),
        ],
        out_shape=[
            jax.ShapeDtypeStruct((8, T), jnp.int32),
            jax.ShapeDtypeStruct((8, T), f32),
            jax.ShapeDtypeStruct((N_EXPERTS, LANES), f32),
        ],
        scratch_shapes=[pltpu.VMEM((N_EXPERTS, 1), f32),
                        pltpu.VMEM((ROUTE_TILE, ROUTE_TILE), bf16)],
        compiler_params=_params(("arbitrary",)),
        name="route",
    )(logits_t, b_expert, b_group)


def _row_copy(src, s, dst, d, sem):
    return pltpu.make_async_copy(src.at[s], dst.at[d], sem)


def _wait_rows(buf, sem):
    pltpu.make_async_copy(buf, buf, sem).wait()


def _dispatch_kernel(n_blocks, pend_ref, padded_ref, nu_ref, dest_ref, x_ref, mn_ref, xs_hbm,
                     zero_scr, x3_scr, zsem, sem):
    @pl.when(pl.program_id(0) == 0)
    def _():
        zero_scr[...] = jnp.zeros_like(zero_scr)

        def fill(row0):
            return pltpu.make_async_copy(zero_scr, xs_hbm.at[pl.ds(row0, EXPERT_ROWS)], zsem)

        for e in range(N_EXPERTS):
            @pl.when(padded_ref[e] > 0)
            def _():
                fill(pend_ref[e] - EXPERT_ROWS).start()

            @pl.when(nu_ref[0] + e < n_blocks)
            def _():
                fill((nu_ref[0] + e) * EXPERT_ROWS).start()

        for e in range(N_EXPERTS):
            @pl.when(padded_ref[e] > 0)
            def _():
                fill(0).wait()

            @pl.when(nu_ref[0] + e < n_blocks)
            def _():
                fill(0).wait()

    i = pl.program_id(0)
    n_tiles = pl.num_programs(0) - 1

    def staged(rows):
        return _pack_rows(_rms(x_ref[rows, :], mn_ref[...]))

    @pl.when(i == 0)
    def _():
        x3_scr[0] = staged(slice(None))

    @pl.when(i > 0)
    def _():
        cur = (i - 1) % 3
        nxt = i % 3

        def chunk(c, carry):
            r0 = pl.multiple_of(c * MOVE_CHUNK, MOVE_CHUNK)
            for j in range(MOVE_CHUNK):
                for k in range(TOP_K):
                    _row_copy(x3_scr.at[cur], r0 + j, xs_hbm,
                              dest_ref[0, k * MOVE_TILE + r0 + j], sem.at[cur]).start(priority=k)
            x3_scr[nxt, pl.ds(r0, MOVE_CHUNK)] = staged(pl.ds(r0, MOVE_CHUNK))
            return carry

        lax.fori_loop(0, MOVE_TILE // MOVE_CHUNK, chunk, 0)

        @pl.when(i > 1)
        def _():
            for k in range(TOP_K):
                _wait_rows(x3_scr.at[(i - 2) % 3], sem.at[(i - 2) % 3])

        @pl.when(i == n_tiles)
        def _():
            for k in range(TOP_K):
                _wait_rows(x3_scr.at[cur], sem.at[cur])


def _dispatch(pend, padded, n_used, dest_tiles, x2d, m_norm, n_rows):
    T = x2d.shape[0]
    n_blocks = n_rows // EXPERT_ROWS
    n_tiles = T // MOVE_TILE
    grid_spec = pltpu.PrefetchScalarGridSpec(
        num_scalar_prefetch=3,
        grid=(n_tiles + 1,),
        in_specs=[
            pl.BlockSpec((None, 1, TOP_K * MOVE_TILE),
                         lambda i, *_: (jnp.maximum(i - 1, 0), 0, 0), memory_space=pltpu.SMEM),
            pl.BlockSpec((MOVE_TILE, D_MODEL), lambda i, *_: (jnp.minimum(i, n_tiles - 1), 0)),
            pl.BlockSpec((1, D_MODEL), lambda i, *_: (0, 0)),
        ],
        out_specs=pl.BlockSpec(memory_space=pl.ANY),
        scratch_shapes=[
            pltpu.VMEM((EXPERT_ROWS, ROW_SUBLANES, LANES), u32),
            pltpu.VMEM((3, MOVE_TILE, ROW_SUBLANES, LANES), u32),
            pltpu.SemaphoreType.DMA(()),
            pltpu.SemaphoreType.DMA((3,)),
        ],
    )
    return pl.pallas_call(
        functools.partial(_dispatch_kernel, n_blocks),
        grid_spec=grid_spec,
        out_shape=jax.ShapeDtypeStruct((n_rows, ROW_SUBLANES, LANES), u32),
        compiler_params=_params(("arbitrary",)),
        name="dispatch",
    )(pend, padded, n_used, dest_tiles, x2d, m_norm)


def _expert_kernel(n_blocks, ps_ref, nb_ref, nu_ref, xs_hbm, w1_ref, w3_ref, w2_ref,
                   ys_hbm, xbuf, ybuf, h_scr, w1_scr, w3_scr, w2_scr, xsem, ysem):
    e = pl.program_id(0)
    nblk = nb_ref[e]
    g0 = ps_ref[e]
    n_used = nu_ref[0]

    def rows(g):
        return pl.ds(pl.multiple_of(g * EXPERT_ROWS, EXPERT_ROWS), EXPERT_ROWS)

    def x_copy(g):
        return pltpu.make_async_copy(xs_hbm.at[rows(g)], xbuf.at[g % 2], xsem.at[g % 2])

    def y_copy(g):
        return pltpu.make_async_copy(ybuf.at[g % 2], ys_hbm.at[rows(g)], ysem.at[g % 2])

    def normed(slot):
        return _unpack_rows(xbuf[slot]).astype(bf16)

    @pl.when(e == 0)
    def _():
        x_copy(0).start(priority=1)
        x_copy(1).start(priority=1)
        x_copy(0).wait()
        h_scr[...] = normed(0)

    @pl.when(nblk > 0)
    def _():
        w1_scr[...] = w1_ref[...].astype(bf16)
        w3_scr[...] = w3_ref[...].astype(bf16)
        w2_scr[...] = w2_ref[...].astype(bf16)

        def body(g, h):
            @pl.when(g + 2 < n_used)
            def _():
                x_copy(g + 2).start(priority=1)

            @pl.when(g + 1 < n_used)
            def _():
                x_copy(g + 1).wait()

            @pl.when(g >= 2)
            def _():
                y_copy(g - 2).wait()

            h_next = normed((g + 1) % 2)
            y = None
            for c0 in range(0, D_EXPERT, D_EXPERT // 2):
                cols = slice(c0, c0 + D_EXPERT // 2)
                a = _dot(h, w1_scr[:, cols])
                c = _dot(h, w3_scr[:, cols])
                hid = ((a * jax.nn.sigmoid(a)) * c).astype(bf16)
                part = _dot(hid, w2_scr[cols, :])
                y = part if y is None else y + part
            ybuf[g % 2] = _pack_rows(y)
            y_copy(g).start(priority=1)
            return h_next

        h_scr[...] = lax.fori_loop(g0, g0 + nblk, body, h_scr[...])

    @pl.when(e == pl.num_programs(0) - 1)
    def _():
        y_copy(n_used - 2).wait()
        y_copy(n_used - 1).wait()
        ybuf[0] = jnp.zeros(ybuf.shape[1:], u32)

        def fill(b):
            r = pl.ds(pl.multiple_of(b * EXPERT_ROWS, EXPERT_ROWS), EXPERT_ROWS)
            return pltpu.make_async_copy(ybuf.at[0], ys_hbm.at[r], ysem.at[0])

        def start(b, c):
            fill(b).start()
            return c

        def wait(b, c):
            fill(b).wait()
            return c

        lax.fori_loop(nu_ref[0], n_blocks, start, 0)
        lax.fori_loop(nu_ref[0], n_blocks, wait, 0)


def _experts(layer, pstart, seg_blocks, n_used, xs, w1, w3, w2):
    n_rows = xs.shape[0]
    n_blocks = n_rows // EXPERT_ROWS

    def w_map(e, *_):
        return (layer, e, 0, 0)

    block_buf = pltpu.VMEM((2, EXPERT_ROWS, ROW_SUBLANES, LANES), u32)
    grid_spec = pltpu.PrefetchScalarGridSpec(
        num_scalar_prefetch=3,
        grid=(N_EXPERTS,),
        in_specs=[
            pl.BlockSpec(memory_space=pl.ANY),
            pl.BlockSpec((None, None, D_MODEL, D_EXPERT), w_map),
            pl.BlockSpec((None, None, D_MODEL, D_EXPERT), w_map),
            pl.BlockSpec((None, None, D_EXPERT, D_MODEL), w_map),
        ],
        out_specs=pl.BlockSpec(memory_space=pl.ANY),
        scratch_shapes=[
            block_buf,
            block_buf,
            pltpu.VMEM((EXPERT_ROWS, D_MODEL), bf16),
            pltpu.VMEM((D_MODEL, D_EXPERT), bf16),
            pltpu.VMEM((D_MODEL, D_EXPERT), bf16),
            pltpu.VMEM((D_EXPERT, D_MODEL), bf16),
            pltpu.SemaphoreType.DMA((2,)),
            pltpu.SemaphoreType.DMA((2,)),
        ],
    )
    return pl.pallas_call(
        functools.partial(_expert_kernel, n_blocks),
        grid_spec=grid_spec,
        out_shape=jax.ShapeDtypeStruct((n_rows, ROW_SUBLANES, LANES), u32),
        compiler_params=_params(("arbitrary",)),
        name="experts",
    )(pstart, seg_blocks, n_used, xs, w1, w3, w2)


def _combine_kernel(dest_ref, next_ref, x_ref, w_ref, ys_hbm, out_ref, buf, sem):
    i = pl.program_id(0)
    slot = i % 2

    def gather(idx_ref, s):
        def start(j, c):
            for k in range(TOP_K):
                _row_copy(ys_hbm, idx_ref[0, k * MOVE_TILE + j], buf.at[s, k], j,
                          sem.at[s]).start(priority=k)
            return c

        lax.fori_loop(0, MOVE_TILE, start, 0, unroll=ISSUE_UNROLL)

    @pl.when(i == 0)
    def _():
        gather(dest_ref, slot)

    @pl.when(i + 1 < pl.num_programs(0))
    def _():
        gather(next_ref, 1 - slot)

    for k in range(TOP_K):
        _wait_rows(buf.at[slot, k], sem.at[slot])
    w = w_ref[...]
    y0 = _unpack_rows(buf[slot, 0])
    y1 = _unpack_rows(buf[slot, 1])
    out_ref[...] = x_ref[...] + (w[:, 0:1] * y0 + w[:, 1:2] * y1)


def _combine(dest_tiles, x2d, w_cols, ys):
    T = x2d.shape[0]
    n_tiles = T // MOVE_TILE
    return pl.pallas_call(
        _combine_kernel,
        grid=(n_tiles,),
        in_specs=[
            pl.BlockSpec((None, 1, TOP_K * MOVE_TILE), lambda i: (i, 0, 0),
                         memory_space=pltpu.SMEM),
            pl.BlockSpec((None, 1, TOP_K * MOVE_TILE),
                         lambda i: (jnp.minimum(i + 1, n_tiles - 1), 0, 0),
                         memory_space=pltpu.SMEM),
            pl.BlockSpec((MOVE_TILE, D_MODEL), lambda i: (i, 0)),
            pl.BlockSpec((MOVE_TILE, 8), lambda i: (i, 0)),
            pl.BlockSpec(memory_space=pl.ANY),
        ],
        out_specs=pl.BlockSpec((MOVE_TILE, D_MODEL), lambda i: (i, 0)),
        out_shape=jax.ShapeDtypeStruct((T, D_MODEL), f32),
        scratch_shapes=[
            pltpu.VMEM((2, TOP_K, MOVE_TILE, ROW_SUBLANES, LANES), u32),
            pltpu.SemaphoreType.DMA((2,)),
        ],
        compiler_params=_params(("arbitrary",)),
        name="combine",
    )(dest_tiles, dest_tiles, x2d, w_cols, ys)


def _moe(layer, x2d, logits_t, m_norm, b_group, b_expert, w1, w3, w2):
    T = x2d.shape[0]
    n_blocks = (T * TOP_K) // EXPERT_ROWS + N_EXPERTS
    n_rows = n_blocks * EXPERT_ROWS

    be = b_expert.reshape(N_EXPERTS, 1)
    bg = jnp.concatenate([b_group, jnp.zeros((8 - N_GROUPS,), f32)]).reshape(8, 1)
    oi, ow, cnt = _route(logits_t, be, bg)

    counts = cnt[:, 0].astype(jnp.int32)
    padded = (counts + EXPERT_ROWS - 1) // EXPERT_ROWS * EXPERT_ROWS
    pend = jnp.cumsum(padded)
    pstart = pend - padded
    n_used = (pend[-1] // EXPERT_ROWS).astype(jnp.int32).reshape(1)
    e_ids = jnp.arange(N_EXPERTS, dtype=jnp.int32)[:, None, None]
    seg_start = jnp.sum(jnp.where(oi[None, 0:2] == e_ids, pstart[:, None, None], 0), axis=0)
    dest = seg_start + oi[2:4]
    dest_tiles = dest.reshape(TOP_K, T // MOVE_TILE, MOVE_TILE).transpose(1, 0, 2).reshape(
        T // MOVE_TILE, 1, TOP_K * MOVE_TILE)
    w_cols = ow.T

    xs = _dispatch(pend.astype(jnp.int32), padded, n_used, dest_tiles, x2d, m_norm, n_rows)
    ys = _experts(layer, (pstart // EXPERT_ROWS).astype(jnp.int32), padded // EXPERT_ROWS,
                  n_used, xs, w1, w3, w2)
    return _combine(dest_tiles, x2d, w_cols, ys)


def _log_sigmoid(x):
    return jnp.minimum(x, 0.0) - jnp.log1p(jnp.exp(-jnp.abs(x)))


def _proj_kernel(tiles_per_seq, x_ref, kvn_ref, bn_ref, wkt_ref, wv_ref, wf_ref, bf_ref,
                 wq_ref, wg_ref, gq_ref, gk_ref, q_ref, kt_ref, v_ref, sg_ref, carry_scr):
    i = pl.program_id(0)
    x = x_ref[...]
    hkv = _rms(x, kvn_ref[...]).astype(bf16)
    hq = _rms(x, bn_ref[...]).astype(bf16)

    v_ref[...] = _dot(hkv, wv_ref[...]).astype(bf16)
    sg_ref[...] = jax.nn.sigmoid(_dot(hq, wg_ref[...])).astype(bf16)

    logf = _log_sigmoid(_dot(hkv, wf_ref[...]) + bf_ref[...])
    tm = x.shape[0]
    incl = (lax.broadcasted_iota(jnp.int32, (tm, tm), 0)
            >= lax.broadcasted_iota(jnp.int32, (tm, tm), 1)).astype(bf16)

    @pl.when(i % tiles_per_seq == 0)
    def _():
        carry_scr[...] = jnp.zeros_like(carry_scr)

    parts = _dot(incl, jnp.concatenate(_split3(logf), axis=1).astype(bf16))
    cum = (parts[:, :LANES] + parts[:, LANES:2 * LANES] + parts[:, 2 * LANES:]
           + carry_scr[...])
    carry_scr[...] = cum[tm - 1:tm, :]
    cum = cum * LOG2E
    cum_t = cum.T

    lane = lax.broadcasted_iota(jnp.int32, (tm, HEAD_PAD), 1)
    sub = lax.broadcasted_iota(jnp.int32, (HEAD_DIM, tm), 0)
    q_raw = _dot(hq, wq_ref[...])
    k_raw_t = _dot_nt(wkt_ref[...], hkv)
    scale = HEAD_DIM ** -0.5 * LOG2E

    for h in range(N_HEADS):
        sl = slice(h * HEAD_PAD, (h + 1) * HEAD_PAD)
        qb = q_raw[:, sl]
        q_ms = jnp.sum(qb * qb, axis=-1, keepdims=True) * (1.0 / HEAD_DIM)
        qn = qb * lax.rsqrt(q_ms + EPS) * gq_ref[...] * scale
        c_hi, c_mid, c_lo = _split3(cum[:, h:h + 1])
        qa = jnp.where(lane < HEAD_DIM, qn,
             jnp.where(lane < AUX_CUM0, 1.0,
             jnp.where(lane == AUX_CUM0, c_hi,
             jnp.where(lane == AUX_CUM0 + 1, c_mid,
             jnp.where(lane == AUX_CUM0 + 2, c_lo, 0.0)))))
        q_ref[:, sl] = qa.astype(bf16)

        kb = k_raw_t[h * HEAD_DIM:(h + 1) * HEAD_DIM, :]
        k_ms = jnp.sum(kb * kb, axis=0, keepdims=True) * (1.0 / HEAD_DIM)
        kn = kb * lax.rsqrt(k_ms + EPS) * gk_ref[...]
        t_hi, t_mid, t_lo = _split3(cum_t[h:h + 1, :])
        aux = jnp.where(sub == 0, -t_hi,
              jnp.where(sub == 1, -t_mid,
              jnp.where(sub == 2, -t_lo,
              jnp.where(sub < 6, 1.0, 0.0))))
        sl = slice(h * HEAD_PAD, (h + 1) * HEAD_PAD)
        ka = jnp.concatenate([kn, aux], axis=0).astype(bf16)
        for t in range(tm // ATT_K):
            kt_ref[t, sl, :] = ka[:, t * ATT_K:(t + 1) * ATT_K]


def _proj(x2d, seq, kv_norm, b_norm, wkt, wv, wf, bfv, wq, wg, gq, gk):
    T = x2d.shape[0]
    n_tiles = T // ROW_TILE
    qw = N_HEADS * HEAD_PAD
    return pl.pallas_call(
        functools.partial(_proj_kernel, seq // ROW_TILE),
        grid=(n_tiles,),
        in_specs=[
            pl.BlockSpec((ROW_TILE, D_MODEL), lambda i: (i, 0)),
            _const_spec((1, D_MODEL)),
            _const_spec((1, D_MODEL)),
            _const_spec((D_MODEL, D_MODEL), single=True),
            _const_spec((D_MODEL, D_MODEL), single=True),
            _const_spec((D_MODEL, LANES)),
            _const_spec((1, LANES)),
            _const_spec((D_MODEL, qw), single=True),
            _const_spec((D_MODEL, D_MODEL), single=True),
            _const_spec((1, HEAD_PAD)),
            _const_spec((HEAD_DIM, 1)),
        ],
        out_specs=[
            pl.BlockSpec((ROW_TILE, qw), lambda i: (i, 0)),
            pl.BlockSpec((ROW_TILE // ATT_K, qw, ATT_K), lambda i: (i, 0, 0)),
            pl.BlockSpec((ROW_TILE, D_MODEL), lambda i: (i, 0)),
            pl.BlockSpec((ROW_TILE, D_MODEL), lambda i: (i, 0)),
        ],
        out_shape=[
            jax.ShapeDtypeStruct((T, qw), bf16),
            jax.ShapeDtypeStruct((T // ATT_K, qw, ATT_K), bf16),
            jax.ShapeDtypeStruct((T, D_MODEL), bf16),
            jax.ShapeDtypeStruct((T, D_MODEL), bf16),
        ],
        scratch_shapes=[pltpu.VMEM((1, LANES), f32)],
        compiler_params=_params(("arbitrary",)),
        name="proj",
    )(x2d, kv_norm, b_norm, wkt, wv, wf, bfv, wq, wg, gq, gk)


def _attn_kernel(q_ref, kt_ref, v_ref, o_ref, s_scr, m_scr, l_scr, acc_scr):
    qi = pl.program_id(2)
    tiles = ATT_Q // ATT_K
    groups = ATT_K // LANES
    heads = [slice(hh * HEAD_PAD, (hh + 1) * HEAD_PAD) for hh in range(2)]
    m_scr[...] = jnp.full(m_scr.shape, -jnp.inf, f32)

    def lane_max(mx, s):
        for g in range(groups):
            mx = jnp.maximum(mx, s[:, g * LANES:(g + 1) * LANES])
        return mx

    def probs(s, mb):
        ps = [jnp.exp2(s[:, g * LANES:(g + 1) * LANES] - mb) for g in range(groups)]
        return ps, functools.reduce(lambda a, b: a + b, ps)

    def body_a(trip, c):
        for t in range(tiles):
            kt_idx = trip * tiles + t
            for hh in range(2):
                s = _dot(q_ref[:, heads[hh]], kt_ref[kt_idx, heads[hh], :])
                s_scr[hh, kt_idx] = s
                m_scr[hh] = lane_max(m_scr[hh], s)
        return c

    lax.fori_loop(0, qi, body_a, 0)
    for t in range(tiles):
        kt_idx = qi * tiles + t
        rows = ATT_Q - t * ATT_K
        visible = (lax.broadcasted_iota(jnp.int32, (rows, ATT_K), 1)
                   <= lax.broadcasted_iota(jnp.int32, (rows, ATT_K), 0))
        for hh in range(2):
            s = _dot(q_ref[t * ATT_K:, heads[hh]], kt_ref[kt_idx, heads[hh], :])
            s = jnp.where(visible, s, -jnp.inf)
            s_scr[hh, kt_idx, t * ATT_K:, :] = s
            m_scr[hh, t * ATT_K:, :] = lane_max(m_scr[hh, t * ATT_K:, :], s)

    for hh in range(2):
        row_max = jnp.max(m_scr[hh], axis=-1, keepdims=True)
        m_scr[hh] = jnp.broadcast_to(row_max, (ATT_Q, LANES))
    l_scr[...] = jnp.zeros(l_scr.shape, f32)
    acc_scr[...] = jnp.zeros(acc_scr.shape, f32)

    def body_b(trip, c):
        for hh in range(2):
            mb = m_scr[hh]
            lsum, acc = l_scr[hh], acc_scr[hh]
            for t0 in range(0, tiles, PV_TILES):
                kt_idx = trip * tiles + t0
                row0 = pl.multiple_of(kt_idx * ATT_K, PV_TILES * ATT_K)
                vb = v_ref[pl.ds(row0, PV_TILES * ATT_K), :]
                parts = []
                for t in range(PV_TILES):
                    ps, psum = probs(s_scr[hh, kt_idx + t], mb)
                    parts += ps
                    lsum = lsum + psum
                acc = acc + _dot(jnp.concatenate(parts, axis=1).astype(bf16), vb)
            l_scr[hh] = lsum
            acc_scr[hh] = acc
        return c

    lax.fori_loop(0, qi, body_b, 0)
    for t in range(tiles):
        kt_idx = qi * tiles + t
        row0 = pl.multiple_of(kt_idx * ATT_K, ATT_K)
        vb = v_ref[pl.ds(row0, ATT_K), :]
        for hh in range(2):
            ps, psum = probs(s_scr[hh, kt_idx, t * ATT_K:, :], m_scr[hh, t * ATT_K:, :])
            l_scr[hh, t * ATT_K:, :] += psum
            acc_scr[hh, t * ATT_K:, :] += _dot(jnp.concatenate(ps, axis=1).astype(bf16), vb)

    lane = lax.broadcasted_iota(jnp.int32, (ATT_Q, 2 * HEAD_DIM), 1)
    o0 = acc_scr[0] / jnp.sum(l_scr[0], axis=-1, keepdims=True)
    o1 = acc_scr[1] / jnp.sum(l_scr[1], axis=-1, keepdims=True)
    o_ref[...] = jnp.where(lane < HEAD_DIM, o0, o1).astype(bf16)


def _attention(q, kt, v, batch, seq):
    T = q.shape[0]
    nq = seq // ATT_Q
    nk = seq // ATT_K
    pairs = N_HEADS // 2
    return pl.pallas_call(
        _attn_kernel,
        grid=(batch, pairs, nq),
        in_specs=[
            pl.BlockSpec((ATT_Q, 2 * HEAD_PAD), lambda b, p, i: (b * nq + i, p)),
            pl.BlockSpec((nk, 2 * HEAD_PAD, ATT_K), lambda b, p, i: (b, p, 0)),
            pl.BlockSpec((seq, 2 * HEAD_DIM), lambda b, p, i: (b, p)),
        ],
        out_specs=pl.BlockSpec((ATT_Q, 2 * HEAD_DIM), lambda b, p, i: (b * nq + i, p)),
        out_shape=jax.ShapeDtypeStruct((T, D_MODEL), bf16),
        scratch_shapes=[
            pltpu.VMEM((2, nk, ATT_Q, ATT_K), f32),
            pltpu.VMEM((2, ATT_Q, LANES), f32),
            pltpu.VMEM((2, ATT_Q, LANES), f32),
            pltpu.VMEM((2, ATT_Q, 2 * HEAD_DIM), f32),
        ],
        compiler_params=_params(("parallel", "parallel", "parallel")),
        name="attn",
    )(q, kt, v)


def _attn_out_kernel(x_ref, o_ref, sg_ref, wo_ref, mn_ref, wr_ref, x3_ref, lg_ref):
    for r0 in range(0, SGU_TILE, EPILOGUE_ROWS):
        rows = slice(r0, r0 + EPILOGUE_ROWS)
        gated = (o_ref[rows, :].astype(f32) * sg_ref[rows, :].astype(f32)).astype(bf16)
        x3 = x_ref[rows, :] + _dot(gated, wo_ref[...])
        x3_ref[rows, :] = x3
        lg_ref[:, rows] = _router_logits_t(x3, mn_ref[...], wr_ref[...])


def _attn_out(x2d, o, sg, wo, m_norm, w_router):
    T = x2d.shape[0]
    return pl.pallas_call(
        _attn_out_kernel,
        grid=(T // SGU_TILE,),
        in_specs=[
            pl.BlockSpec((SGU_TILE, D_MODEL), lambda i: (i, 0)),
            pl.BlockSpec((SGU_TILE, D_MODEL), lambda i: (i, 0)),
            pl.BlockSpec((SGU_TILE, D_MODEL), lambda i: (i, 0)),
            _const_spec((D_MODEL, D_MODEL)),
            _const_spec((1, D_MODEL)),
            _const_spec((2 * D_MODEL, 2 * LANES)),
        ],
        out_specs=[
            pl.BlockSpec((SGU_TILE, D_MODEL), lambda i: (i, 0)),
            pl.BlockSpec((ROUTER_ROWS, SGU_TILE), lambda i: (0, i)),
        ],
        out_shape=[
            jax.ShapeDtypeStruct((T, D_MODEL), f32),
            jax.ShapeDtypeStruct((ROUTER_ROWS, T), f32),
        ],
        compiler_params=_params(("parallel",)),
        name="attn_out",
    )(x2d, o, sg, wo, m_norm, w_router)


def _router_weight(w_group, w_expert):
    pad = jnp.zeros((D_MODEL, LANES - N_EXPERTS - N_GROUPS), f32)
    w = jnp.concatenate([w_expert, w_group, pad], axis=1)
    w_hi = w.astype(bf16)
    w_lo = (w - w_hi.astype(f32)).astype(bf16)
    top = jnp.concatenate([w_hi, w_lo], axis=1)
    bottom = jnp.concatenate([w_hi, jnp.zeros_like(w_hi)], axis=1)
    return jnp.concatenate([top, bottom], axis=0)


def kernel(x, a_norm, a_w_in, a_b_in, a_v_norm, a_w_s, a_b_s, a_w_out, kv_norm, kv_w, kv_b_f,
           k_norm, b_norm, b_w_qg, q_norm, b_w_out, m_norm, m_w_group, m_b_group, m_w_expert,
           m_b_expert, m_w1, m_w3, m_w2):
    batch, seq, _ = x.shape
    T = batch * seq
    x2d = x.reshape(T, D_MODEL)

    x1, lg0 = _sgu(
        x2d, a_norm[0].reshape(1, -1), a_w_in[0].astype(bf16), a_b_in[0].reshape(1, -1),
        a_v_norm[0].reshape(1, -1), a_w_s[0], a_b_s[0].T, a_w_out[0].astype(bf16),
        m_norm[0].reshape(1, -1), _router_weight(m_w_group[0], m_w_expert[0]))
    x2 = _moe(0, x1, lg0, m_norm[0].reshape(1, -1), m_b_group[0], m_b_expert[0],
              m_w1, m_w3, m_w2)

    wk = kv_w[:, :D_MODEL]
    wv = kv_w[:, D_MODEL:2 * D_MODEL]
    wf = jnp.pad(kv_w[:, 2 * D_MODEL:], ((0, 0), (0, LANES - N_HEADS)))
    bfv = jnp.pad(kv_b_f, (0, LANES - N_HEADS)).reshape(1, LANES)
    wq = b_w_qg[0][:, :D_MODEL]
    wg = b_w_qg[0][:, D_MODEL:]
    gq = jnp.pad(q_norm[0], (0, HEAD_PAD - HEAD_DIM)).reshape(1, HEAD_PAD)
    gk = k_norm.reshape(HEAD_DIM, 1)
    wq_pad = jnp.pad(wq.reshape(D_MODEL, N_HEADS, HEAD_DIM),
                     ((0, 0), (0, 0), (0, HEAD_PAD - HEAD_DIM))).reshape(D_MODEL, -1)
    q, kt, v, sg = _proj(
        x2, seq, kv_norm.reshape(1, -1), b_norm[0].reshape(1, -1),
        wk.T.astype(bf16), wv.astype(bf16), wf.astype(bf16), bfv,
        wq_pad.astype(bf16), wg.astype(bf16), gq, gk)
    o = _attention(q, kt, v, batch, seq)
    x3, lg1 = _attn_out(x2, o, sg, b_w_out[0].astype(bf16), m_norm[1].reshape(1, -1),
                        _router_weight(m_w_group[1], m_w_expert[1]))
    x4 = _moe(1, x3, lg1, m_norm[1].reshape(1, -1), m_b_group[1], m_b_expert[1],
              m_w1, m_w3, m_w2)
    return x4.reshape(batch, seq, D_MODEL)
```

```python
import functools
import math

import jax
import jax.numpy as jnp
import numpy as np
from jax import lax
from jax.experimental import pallas as pl
from jax.experimental.pallas import tpu as pltpu

D_MODEL = 1024
EPS = 1e-6
SGU_BLOCK = 128
SGU_CHUNK = 64
CHUNK_SHIFT = SGU_CHUNK.bit_length() - 1
SGU_WIDTH = 2 * D_MODEL
SGU_GROUPS = 8
SGU_GROUP_DIM = SGU_WIDTH // SGU_GROUPS
N_HEADS = 16
HEAD_DIM = D_MODEL // N_HEADS
N_GROUPS = 4
EXPERTS_PER_GROUP = 8
N_EXPERTS = N_GROUPS * EXPERTS_PER_GROUP
GROUP_SHIFT = EXPERTS_PER_GROUP.bit_length() - 1
TOP_K = 2
D_EXPERT = D_MODEL // 2

LANES = 128
ROW_SUBLANES = D_MODEL // 2 // LANES
ISSUE_UNROLL = 8
VMEM_LIMIT_BYTES = 56 * 1024 * 1024

ROW_TILE = 256
SGU_TILE = 512
EPILOGUE_ROWS = 256
ROUTE_TILE = 1024
EXPERT_ROWS = 512
MOVE_TILE = 512
MOVE_CHUNK = 64
ATT_Q = 1024
ATT_K = 256
PV_TILES = 2
HEAD_PAD = 2 * HEAD_DIM
ROUTER_ROWS = 40
AUX_ONE0 = HEAD_DIM
AUX_CUM0 = HEAD_DIM + 3

LOG2E = math.log2(math.e)

bf16 = jnp.bfloat16
f32 = jnp.float32
u32 = jnp.uint32
HI_HALF = np.uint32(0xFFFF0000)


def _dot(a, b, precision=None):
    return jnp.dot(a, b, preferred_element_type=f32, precision=precision)


def _dot_nt(a, b, precision=None):
    return lax.dot_general(a, b, (((1,), (1,)), ((), ())), preferred_element_type=f32,
                           precision=precision)


def _rms(x, g):
    ms = jnp.mean(x * x, axis=-1, keepdims=True)
    return x * lax.rsqrt(ms + EPS) * g


def _gelu_tanh(z):
    c = math.sqrt(2.0 / math.pi)
    return z * (0.5 * (1.0 + jnp.tanh(c * (z + 0.044715 * (z * z * z)))))


def _split3(c):
    hi = c.astype(bf16).astype(f32)
    r = c - hi
    mid = r.astype(bf16).astype(f32)
    lo = r - mid
    return hi, mid, lo


def _pack_rows(v):
    half = D_MODEL // 2
    lo = lax.bitcast_convert_type(v[:, :half].astype(bf16).astype(f32), u32) >> 16
    hi = lax.bitcast_convert_type(v[:, half:].astype(bf16).astype(f32), u32) & HI_HALF
    return (hi | lo).reshape(v.shape[0], ROW_SUBLANES, LANES)


def _unpack_rows(w):
    w = w.reshape(w.shape[0], D_MODEL // 2)
    lo = lax.bitcast_convert_type(w << 16, f32)
    hi = lax.bitcast_convert_type(w & HI_HALF, f32)
    return jnp.concatenate([lo, hi], axis=1)


def _const_spec(shape, single=False):
    nd = len(shape)
    mode = pl.Buffered(1) if single else None
    return pl.BlockSpec(shape, lambda *_: (0,) * nd, pipeline_mode=mode)


def _params(sem):
    return pltpu.CompilerParams(dimension_semantics=sem, vmem_limit_bytes=VMEM_LIMIT_BYTES)


def _router_logits_t(x_new, mn, wr3):
    hm = _rms(x_new, mn)
    h_hi = hm.astype(bf16)
    h_lo = (hm - h_hi.astype(f32)).astype(bf16)
    both = _dot(jnp.concatenate([h_hi, h_lo], axis=1), wr3)
    lg = both[:, :LANES] + both[:, LANES:]
    return lg.T[:ROUTER_ROWS, :]


def _sgu_kernel(x_ref, an_ref, win_ref, bin_ref, vn_ref, ws_ref, bst_ref, wout_ref, mn_ref,
                wr_ref, x1_ref, lg_ref, u_scr, v_scr, gated_scr):
    x = x_ref[...]
    h = _rms(x, an_ref[...]).astype(bf16)
    v = _gelu_tanh(_dot(h, win_ref[:, SGU_WIDTH:]) + bin_ref[:, SGU_WIDTH:])
    v_scr[...] = _rms(v, vn_ref[...]).astype(bf16)
    u_scr[...] = _gelu_tanh(_dot(h, win_ref[:, :SGU_WIDTH]) + bin_ref[:, :SGU_WIDTH])

    t_chunk = lax.broadcasted_iota(jnp.int32, (SGU_BLOCK, SGU_BLOCK), 0) >> CHUNK_SHIFT
    s_chunk = lax.broadcasted_iota(jnp.int32, (SGU_BLOCK, SGU_BLOCK), 1) >> CHUNK_SHIFT
    causal = t_chunk >= s_chunk
    wsm = [jnp.where(causal, ws_ref[g], 0.0).astype(bf16) for g in range(SGU_GROUPS)]
    for r0 in range(0, SGU_TILE, EPILOGUE_ROWS):
        for g in range(SGU_GROUPS):
            cols = slice(g * SGU_GROUP_DIM, (g + 1) * SGU_GROUP_DIM)
            for sb in range(r0 // SGU_BLOCK, (r0 + EPILOGUE_ROWS) // SGU_BLOCK):
                rows = slice(sb * SGU_BLOCK, (sb + 1) * SGU_BLOCK)
                mixed = _dot(wsm[g], v_scr[rows, cols]) + bst_ref[:, g:g + 1]
                gated_scr[rows, cols] = (u_scr[rows, cols] * mixed).astype(bf16)
        half = slice(r0, r0 + EPILOGUE_ROWS)
        x1 = x[half, :] + _dot(gated_scr[half, :], wout_ref[...])
        x1_ref[half, :] = x1
        lg_ref[:, half] = _router_logits_t(x1, mn_ref[...], wr_ref[...])


def _sgu(x2d, a_norm, w_in, b_in, v_norm, w_s, b_s_t, w_out, m_norm, w_router):
    T = x2d.shape[0]
    return pl.pallas_call(
        _sgu_kernel,
        grid=(T // SGU_TILE,),
        in_specs=[
            pl.BlockSpec((SGU_TILE, D_MODEL), lambda i: (i, 0)),
            _const_spec((1, D_MODEL)),
            _const_spec((D_MODEL, 2 * SGU_WIDTH), single=True),
            _const_spec((1, 2 * SGU_WIDTH)),
            _const_spec((1, SGU_WIDTH)),
            _const_spec((SGU_GROUPS, SGU_BLOCK, SGU_BLOCK)),
            _const_spec((SGU_BLOCK, SGU_GROUPS)),
            _const_spec((SGU_WIDTH, D_MODEL), single=True),
            _const_spec((1, D_MODEL)),
            _const_spec((2 * D_MODEL, 2 * LANES)),
        ],
        out_specs=[
            pl.BlockSpec((SGU_TILE, D_MODEL), lambda i: (i, 0)),
            pl.BlockSpec((ROUTER_ROWS, SGU_TILE), lambda i: (0, i)),
        ],
        out_shape=[
            jax.ShapeDtypeStruct((T, D_MODEL), f32),
            jax.ShapeDtypeStruct((ROUTER_ROWS, T), f32),
        ],
        scratch_shapes=[
            pltpu.VMEM((SGU_TILE, SGU_WIDTH), f32),
            pltpu.VMEM((SGU_TILE, SGU_WIDTH), bf16),
            pltpu.VMEM((SGU_TILE, SGU_WIDTH), bf16),
        ],
        compiler_params=_params(("parallel",)),
        name="sgu",
    )(x2d, a_norm, w_in, b_in, v_norm, w_s, b_s_t, w_out, m_norm, w_router)


def _route_kernel(lg_ref, be_ref, bg_ref, oi_ref, ow_ref, cnt_ref, carry_scr, before_scr,
                  w_scr):
    step = pl.program_id(0)
    tt = lg_ref.shape[1]

    @pl.when(step == 0)
    def _():
        carry_scr[...] = jnp.zeros_like(carry_scr)
        before_scr[...] = (lax.broadcasted_iota(jnp.int32, (tt, tt), 0)
                           < lax.broadcasted_iota(jnp.int32, (tt, tt), 1)).astype(bf16)

    lg = lg_ref[...]
    e_l = lg[0:N_EXPERTS, :] + be_ref[...]
    g_l = lg[N_EXPERTS:N_EXPERTS + 8, :] + bg_ref[...]
    g_row = lax.broadcasted_iota(jnp.int32, g_l.shape, 0).astype(f32)
    g_l = jnp.where(g_row < N_GROUPS, g_l, -jnp.inf)
    g_max = jnp.max(g_l, axis=0, keepdims=True)
    g_sel = jnp.min(jnp.where(g_l == g_max, g_row, 8.0), axis=0, keepdims=True)
    g_den = jnp.sum(jnp.exp(g_l - g_max), axis=0, keepdims=True)
    g_w = 1.0 / g_den

    e_row_i = lax.broadcasted_iota(jnp.int32, e_l.shape, 0)
    e_row = e_row_i.astype(f32)
    e_grp = (e_row_i >> GROUP_SHIFT).astype(f32)
    e_in = jnp.where(e_grp == g_sel, e_l, -jnp.inf)
    m1 = jnp.max(e_in, axis=0, keepdims=True)
    i1 = jnp.min(jnp.where(e_in == m1, e_row, float(N_EXPERTS)), axis=0, keepdims=True)
    e_in2 = jnp.where(e_row == i1, -jnp.inf, e_in)
    m2 = jnp.max(e_in2, axis=0, keepdims=True)
    i2 = jnp.min(jnp.where(e_in2 == m2, e_row, float(N_EXPERTS)), axis=0, keepdims=True)
    t = jnp.exp(m2 - m1)
    p1 = 1.0 / (1.0 + t)
    p2 = t / (1.0 + t)

    hit1 = e_row == i1
    hit2 = e_row == i2
    member = jnp.logical_or(hit1, hit2)
    prefix = _dot(member.astype(bf16), before_scr[...])
    rank_full = prefix + carry_scr[...]
    rank1 = jnp.sum(jnp.where(hit1, rank_full, 0.0), axis=0, keepdims=True)
    rank2 = jnp.sum(jnp.where(hit2, rank_full, 0.0), axis=0, keepdims=True)
    carry_new = carry_scr[...] + jnp.sum(member.astype(f32), axis=1, keepdims=True)
    carry_scr[...] = carry_new

    oi_ref[...] = jnp.zeros(oi_ref.shape, jnp.int32)
    oi_ref[0:1, :] = i1.astype(jnp.int32)
    oi_ref[1:2, :] = i2.astype(jnp.int32)
    oi_ref[2:3, :] = rank1.astype(jnp.int32)
    oi_ref[3:4, :] = rank2.astype(jnp.int32)
    w_scr[...] = jnp.zeros(w_scr.shape, f32)
    w_scr[0:1, :] = g_w * p1
    w_scr[1:2, :] = g_w * p2
    ow_ref[...] = w_scr[...].T
    cnt_ref[...] = jnp.broadcast_to(carry_new, cnt_ref.shape)


def _route(logits_t, b_expert, b_group):
    T = logits_t.shape[1]
    return pl.pallas_call(
        _route_kernel,
        grid=(T // ROUTE_TILE,),
        in_specs=[
            pl.BlockSpec((ROUTER_ROWS, ROUTE_TILE), lambda i: (0, i)),
            _const_spec((N_EXPERTS, 1)),
            _const_spec((8, 1)),
        ],
        out_specs=[
            pl.BlockSpec((8, ROUTE_TILE), lambda i: (0, i)),
            pl.BlockSpec((ROUTE_TILE, 8), lambda i: (i, 0)),
            _const_spec((N_EXPERTS, LANES)),
        ],
        out_shape=[
            jax.ShapeDtypeStruct((8, T), jnp.int32),
            jax.ShapeDtypeStruct((T, 8), f32),
            jax.ShapeDtypeStruct((N_EXPERTS, LANES), f32),
        ],
        scratch_shapes=[pltpu.VMEM((N_EXPERTS, 1), f32),
                        pltpu.VMEM((ROUTE_TILE, ROUTE_TILE), bf16),
                        pltpu.VMEM((8, ROUTE_TILE), f32)],
        compiler_params=_params(("arbitrary",)),
        name="route",
    )(logits_t, b_expert, b_group)


def _row_copy(src, s, dst, d, sem):
    return pltpu.make_async_copy(src.at[s], dst.at[d], sem)


def _wait_rows(buf, sem):
    pltpu.make_async_copy(buf, buf, sem).wait()


def _dispatch_kernel(n_blocks, pend_ref, padded_ref, nu_ref, dest_ref, x_ref, mn_ref, xs_hbm,
                     zero_scr, x3_scr, zsem, sem):
    @pl.when(pl.program_id(0) == 0)
    def _():
        zero_scr[...] = jnp.zeros_like(zero_scr)

        def fill(row0):
            return pltpu.make_async_copy(zero_scr, xs_hbm.at[pl.ds(row0, EXPERT_ROWS)], zsem)

        for e in range(N_EXPERTS):
            @pl.when(padded_ref[e] > 0)
            def _():
                fill(pend_ref[e] - EXPERT_ROWS).start()

            @pl.when(nu_ref[0] + e < n_blocks)
            def _():
                fill((nu_ref[0] + e) * EXPERT_ROWS).start()

        for e in range(N_EXPERTS):
            @pl.when(padded_ref[e] > 0)
            def _():
                fill(0).wait()

            @pl.when(nu_ref[0] + e < n_blocks)
            def _():
                fill(0).wait()

    i = pl.program_id(0)
    n_tiles = pl.num_programs(0) - 1

    def staged(rows):
        return _pack_rows(_rms(x_ref[rows, :], mn_ref[...]))

    @pl.when(i == 0)
    def _():
        x3_scr[0] = staged(slice(None))

    @pl.when(i > 0)
    def _():
        cur = (i - 1) % 3
        nxt = i % 3

        def chunk(c, carry):
            r0 = pl.multiple_of(c * MOVE_CHUNK, MOVE_CHUNK)
            for j in range(MOVE_CHUNK):
                for k in range(TOP_K):
                    _row_copy(x3_scr.at[cur], r0 + j, xs_hbm,
                              dest_ref[0, k * MOVE_TILE + r0 + j], sem.at[cur]).start(priority=k)
            x3_scr[nxt, pl.ds(r0, MOVE_CHUNK)] = staged(pl.ds(r0, MOVE_CHUNK))
            return carry

        lax.fori_loop(0, MOVE_TILE // MOVE_CHUNK, chunk, 0)

        @pl.when(i > 1)
        def _():
            for k in range(TOP_K):
                _wait_rows(x3_scr.at[(i - 2) % 3], sem.at[(i - 2) % 3])

        @pl.when(i == n_tiles)
        def _():
            for k in range(TOP_K):
                _wait_rows(x3_scr.at[cur], sem.at[cur])


def _dispatch(pend, padded, n_used, dest_tiles, x2d, m_norm, n_rows):
    T = x2d.shape[0]
    n_blocks = n_rows // EXPERT_ROWS
    n_tiles = T // MOVE_TILE
    grid_spec = pltpu.PrefetchScalarGridSpec(
        num_scalar_prefetch=3,
        grid=(n_tiles + 1,),
        in_specs=[
            pl.BlockSpec((None, 1, TOP_K * MOVE_TILE),
                         lambda i, *_: (jnp.maximum(i - 1, 0), 0, 0), memory_space=pltpu.SMEM),
            pl.BlockSpec((MOVE_TILE, D_MODEL), lambda i, *_: (jnp.minimum(i, n_tiles - 1), 0)),
            pl.BlockSpec((1, D_MODEL), lambda i, *_: (0, 0)),
        ],
        out_specs=pl.BlockSpec(memory_space=pl.ANY),
        scratch_shapes=[
            pltpu.VMEM((EXPERT_ROWS, ROW_SUBLANES, LANES), u32),
            pltpu.VMEM((3, MOVE_TILE, ROW_SUBLANES, LANES), u32),
            pltpu.SemaphoreType.DMA(()),
            pltpu.SemaphoreType.DMA((3,)),
        ],
    )
    return pl.pallas_call(
        functools.partial(_dispatch_kernel, n_blocks),
        grid_spec=grid_spec,
        out_shape=jax.ShapeDtypeStruct((n_rows, ROW_SUBLANES, LANES), u32),
        compiler_params=_params(("arbitrary",)),
        name="dispatch",
    )(pend, padded, n_used, dest_tiles, x2d, m_norm)


def _expert_kernel(n_blocks, ps_ref, nb_ref, nu_ref, xs_hbm, w1_ref, w3_ref, w2_ref,
                   ys_hbm, xbuf, ybuf, h_scr, w1_scr, w3_scr, w2_scr, xsem, ysem):
    e = pl.program_id(0)
    nblk = nb_ref[e]
    g0 = ps_ref[e]
    n_used = nu_ref[0]

    def rows(g):
        return pl.ds(pl.multiple_of(g * EXPERT_ROWS, EXPERT_ROWS), EXPERT_ROWS)

    def x_copy(g):
        return pltpu.make_async_copy(xs_hbm.at[rows(g)], xbuf.at[g % 2], xsem.at[g % 2])

    def y_copy(g):
        return pltpu.make_async_copy(ybuf.at[g % 2], ys_hbm.at[rows(g)], ysem.at[g % 2])

    def normed(slot):
        return _unpack_rows(xbuf[slot]).astype(bf16)

    @pl.when(e == 0)
    def _():
        x_copy(0).start(priority=1)
        x_copy(1).start(priority=1)
        x_copy(0).wait()
        h_scr[...] = normed(0)

    @pl.when(nblk > 0)
    def _():
        w1_scr[...] = w1_ref[...].astype(bf16)
        w3_scr[...] = w3_ref[...].astype(bf16)
        w2_scr[...] = w2_ref[...].astype(bf16)

        def body(g, h):
            @pl.when(g + 2 < n_used)
            def _():
                x_copy(g + 2).start(priority=1)

            @pl.when(g + 1 < n_used)
            def _():
                x_copy(g + 1).wait()

            @pl.when(g >= 2)
            def _():
                y_copy(g - 2).wait()

            h_next = normed((g + 1) % 2)
            y = None
            for c0 in range(0, D_EXPERT, D_EXPERT // 2):
                cols = slice(c0, c0 + D_EXPERT // 2)
                a = _dot(h, w1_scr[:, cols])
                c = _dot(h, w3_scr[:, cols])
                hid = ((a * jax.nn.sigmoid(a)) * c).astype(bf16)
                part = _dot(hid, w2_scr[cols, :])
                y = part if y is None else y + part
            ybuf[g % 2] = _pack_rows(y)
            y_copy(g).start(priority=1)
            return h_next

        h_scr[...] = lax.fori_loop(g0, g0 + nblk, body, h_scr[...])

    @pl.when(e == pl.num_programs(0) - 1)
    def _():
        y_copy(n_used - 2).wait()
        y_copy(n_used - 1).wait()
        ybuf[0] = jnp.zeros(ybuf.shape[1:], u32)

        def fill(b):
            r = pl.ds(pl.multiple_of(b * EXPERT_ROWS, EXPERT_ROWS), EXPERT_ROWS)
            return pltpu.make_async_copy(ybuf.at[0], ys_hbm.at[r], ysem.at[0])

        def start(b, c):
            fill(b).start()
            return c

        def wait(b, c):
            fill(b).wait()
            return c

        lax.fori_loop(nu_ref[0], n_blocks, start, 0)
        lax.fori_loop(nu_ref[0], n_blocks, wait, 0)


def _experts(layer, pstart, seg_blocks, n_used, xs, w1, w3, w2):
    n_rows = xs.shape[0]
    n_blocks = n_rows // EXPERT_ROWS

    def w_map(e, *_):
        return (layer, e, 0, 0)

    block_buf = pltpu.VMEM((2, EXPERT_ROWS, ROW_SUBLANES, LANES), u32)
    grid_spec = pltpu.PrefetchScalarGridSpec(
        num_scalar_prefetch=3,
        grid=(N_EXPERTS,),
        in_specs=[
            pl.BlockSpec(memory_space=pl.ANY),
            pl.BlockSpec((None, None, D_MODEL, D_EXPERT), w_map),
            pl.BlockSpec((None, None, D_MODEL, D_EXPERT), w_map),
            pl.BlockSpec((None, None, D_EXPERT, D_MODEL), w_map),
        ],
        out_specs=pl.BlockSpec(memory_space=pl.ANY),
        scratch_shapes=[
            block_buf,
            block_buf,
            pltpu.VMEM((EXPERT_ROWS, D_MODEL), bf16),
            pltpu.VMEM((D_MODEL, D_EXPERT), bf16),
            pltpu.VMEM((D_MODEL, D_EXPERT), bf16),
            pltpu.VMEM((D_EXPERT, D_MODEL), bf16),
            pltpu.SemaphoreType.DMA((2,)),
            pltpu.SemaphoreType.DMA((2,)),
        ],
    )
    return pl.pallas_call(
        functools.partial(_expert_kernel, n_blocks),
        grid_spec=grid_spec,
        out_shape=jax.ShapeDtypeStruct((n_rows, ROW_SUBLANES, LANES), u32),
        compiler_params=_params(("arbitrary",)),
        name="experts",
    )(pstart, seg_blocks, n_used, xs, w1, w3, w2)


def _combine_kernel(dest_ref, next_ref, x_ref, w_ref, ys_hbm, out_ref, buf, sem):
    i = pl.program_id(0)
    slot = i % 2

    def gather(idx_ref, s):
        def start(j, c):
            for k in range(TOP_K):
                _row_copy(ys_hbm, idx_ref[0, k * MOVE_TILE + j], buf.at[s, k], j,
                          sem.at[s]).start(priority=k)
            return c

        lax.fori_loop(0, MOVE_TILE, start, 0, unroll=ISSUE_UNROLL)

    @pl.when(i == 0)
    def _():
        gather(dest_ref, slot)

    @pl.when(i + 1 < pl.num_programs(0))
    def _():
        gather(next_ref, 1 - slot)

    for k in range(TOP_K):
        _wait_rows(buf.at[slot, k], sem.at[slot])
    w = w_ref[...]
    y0 = _unpack_rows(buf[slot, 0])
    y1 = _unpack_rows(buf[slot, 1])
    out_ref[...] = x_ref[...] + (w[:, 0:1] * y0 + w[:, 1:2] * y1)


def _combine(dest_tiles, x2d, w_cols, ys):
    T = x2d.shape[0]
    n_tiles = T // MOVE_TILE
    return pl.pallas_call(
        _combine_kernel,
        grid=(n_tiles,),
        in_specs=[
            pl.BlockSpec((None, 1, TOP_K * MOVE_TILE), lambda i: (i, 0, 0),
                         memory_space=pltpu.SMEM),
            pl.BlockSpec((None, 1, TOP_K * MOVE_TILE),
                         lambda i: (jnp.minimum(i + 1, n_tiles - 1), 0, 0),
                         memory_space=pltpu.SMEM),
            pl.BlockSpec((MOVE_TILE, D_MODEL), lambda i: (i, 0)),
            pl.BlockSpec((MOVE_TILE, 8), lambda i: (i, 0)),
            pl.BlockSpec(memory_space=pl.ANY),
        ],
        out_specs=pl.BlockSpec((MOVE_TILE, D_MODEL), lambda i: (i, 0)),
        out_shape=jax.ShapeDtypeStruct((T, D_MODEL), f32),
        scratch_shapes=[
            pltpu.VMEM((2, TOP_K, MOVE_TILE, ROW_SUBLANES, LANES), u32),
            pltpu.SemaphoreType.DMA((2,)),
        ],
        compiler_params=_params(("arbitrary",)),
        name="combine",
    )(dest_tiles, dest_tiles, x2d, w_cols, ys)


def _moe(layer, x2d, logits_t, m_norm, b_group, b_expert, w1, w3, w2):
    T = x2d.shape[0]
    n_blocks = (T * TOP_K) // EXPERT_ROWS + N_EXPERTS
    n_rows = n_blocks * EXPERT_ROWS

    be = b_expert.reshape(N_EXPERTS, 1)
    bg = jnp.concatenate([b_group, jnp.zeros((8 - N_GROUPS,), f32)]).reshape(8, 1)
    oi, w_cols, cnt = _route(logits_t, be, bg)

    counts = cnt[:, 0].astype(jnp.int32)
    padded = (counts + EXPERT_ROWS - 1) // EXPERT_ROWS * EXPERT_ROWS
    pend = jnp.cumsum(padded)
    pstart = pend - padded
    n_used = (pend[-1] // EXPERT_ROWS).astype(jnp.int32).reshape(1)
    e_ids = jnp.arange(N_EXPERTS, dtype=jnp.int32)[:, None, None]
    seg_start = jnp.sum(jnp.where(oi[None, 0:2] == e_ids, pstart[:, None, None], 0), axis=0)
    dest = seg_start + oi[2:4]
    dest_tiles = dest.reshape(TOP_K, T // MOVE_TILE, MOVE_TILE).transpose(1, 0, 2).reshape(
        T // MOVE_TILE, 1, TOP_K * MOVE_TILE)

    xs = _dispatch(pend.astype(jnp.int32), padded, n_used, dest_tiles, x2d, m_norm, n_rows)
    ys = _experts(layer, (pstart // EXPERT_ROWS).astype(jnp.int32), padded // EXPERT_ROWS,
                  n_used, xs, w1, w3, w2)
    return _combine(dest_tiles, x2d, w_cols, ys)


def _log_sigmoid(x):
    return jnp.minimum(x, 0.0) - jnp.log1p(jnp.exp(-jnp.abs(x)))


def _proj_kernel(tiles_per_seq, x_ref, kvn_ref, bn_ref, wkt_ref, wv_ref, wf_ref, bf_ref,
                 wq_ref, wg_ref, gq_ref, gk_ref, q_ref, kt_ref, v_ref, sg_ref, carry_scr):
    i = pl.program_id(0)
    x = x_ref[...]
    hkv = _rms(x, kvn_ref[...]).astype(bf16)
    hq = _rms(x, bn_ref[...]).astype(bf16)

    v_ref[...] = _dot(hkv, wv_ref[...]).astype(bf16)
    sg_ref[...] = jax.nn.sigmoid(_dot(hq, wg_ref[...])).astype(bf16)

    logf = _log_sigmoid(_dot(hkv, wf_ref[...]) + bf_ref[...])
    tm = x.shape[0]
    incl = (lax.broadcasted_iota(jnp.int32, (tm, tm), 0)
            >= lax.broadcasted_iota(jnp.int32, (tm, tm), 1)).astype(bf16)

    @pl.when(i % tiles_per_seq == 0)
    def _():
        carry_scr[...] = jnp.zeros_like(carry_scr)

    parts = _dot(incl, jnp.concatenate(_split3(logf), axis=1).astype(bf16))
    cum = (parts[:, :LANES] + parts[:, LANES:2 * LANES] + parts[:, 2 * LANES:]
           + carry_scr[...])
    carry_scr[...] = cum[tm - 1:tm, :]
    cum = cum * LOG2E
    cum_t = cum.T

    lane = lax.broadcasted_iota(jnp.int32, (tm, HEAD_PAD), 1)
    sub = lax.broadcasted_iota(jnp.int32, (HEAD_DIM, tm), 0)
    q_raw = _dot(hq, wq_ref[...])
    k_raw_t = _dot_nt(wkt_ref[...], hkv)
    scale = HEAD_DIM ** -0.5 * LOG2E

    for h in range(N_HEADS):
        sl = slice(h * HEAD_PAD, (h + 1) * HEAD_PAD)
        qb = q_raw[:, sl]
        q_ms = jnp.sum(qb * qb, axis=-1, keepdims=True) * (1.0 / HEAD_DIM)
        qn = qb * lax.rsqrt(q_ms + EPS) * gq_ref[...] * scale
        c_hi, c_mid, c_lo = _split3(cum[:, h:h + 1])
        qa = jnp.where(lane < HEAD_DIM, qn,
             jnp.where(lane < AUX_CUM0, 1.0,
             jnp.where(lane == AUX_CUM0, c_hi,
             jnp.where(lane == AUX_CUM0 + 1, c_mid,
             jnp.where(lane == AUX_CUM0 + 2, c_lo, 0.0)))))
        q_ref[:, sl] = qa.astype(bf16)

        kb = k_raw_t[h * HEAD_DIM:(h + 1) * HEAD_DIM, :]
        k_ms = jnp.sum(kb * kb, axis=0, keepdims=True) * (1.0 / HEAD_DIM)
        kn = kb * lax.rsqrt(k_ms + EPS) * gk_ref[...]
        t_hi, t_mid, t_lo = _split3(cum_t[h:h + 1, :])
        aux = jnp.where(sub == 0, -t_hi,
              jnp.where(sub == 1, -t_mid,
              jnp.where(sub == 2, -t_lo,
              jnp.where(sub < 6, 1.0, 0.0))))
        sl = slice(h * HEAD_PAD, (h + 1) * HEAD_PAD)
        ka = jnp.concatenate([kn, aux], axis=0).astype(bf16)
        for t in range(tm // ATT_K):
            kt_ref[t, sl, :] = ka[:, t * ATT_K:(t + 1) * ATT_K]


def _proj(x2d, seq, kv_norm, b_norm, wkt, wv, wf, bfv, wq, wg, gq, gk):
    T = x2d.shape[0]
    n_tiles = T // ROW_TILE
    qw = N_HEADS * HEAD_PAD
    return pl.pallas_call(
        functools.partial(_proj_kernel, seq // ROW_TILE),
        grid=(n_tiles,),
        in_specs=[
            pl.BlockSpec((ROW_TILE, D_MODEL), lambda i: (i, 0)),
            _const_spec((1, D_MODEL)),
            _const_spec((1, D_MODEL)),
            _const_spec((D_MODEL, D_MODEL), single=True),
            _const_spec((D_MODEL, D_MODEL), single=True),
            _const_spec((D_MODEL, LANES)),
            _const_spec((1, LANES)),
            _const_spec((D_MODEL, qw), single=True),
            _const_spec((D_MODEL, D_MODEL), single=True),
            _const_spec((1, HEAD_PAD)),
            _const_spec((HEAD_DIM, 1)),
        ],
        out_specs=[
            pl.BlockSpec((ROW_TILE, qw), lambda i: (i, 0)),
            pl.BlockSpec((ROW_TILE // ATT_K, qw, ATT_K), lambda i: (i, 0, 0)),
            pl.BlockSpec((ROW_TILE, D_MODEL), lambda i: (i, 0)),
            pl.BlockSpec((ROW_TILE, D_MODEL), lambda i: (i, 0)),
        ],
        out_shape=[
            jax.ShapeDtypeStruct((T, qw), bf16),
            jax.ShapeDtypeStruct((T // ATT_K, qw, ATT_K), bf16),
            jax.ShapeDtypeStruct((T, D_MODEL), bf16),
            jax.ShapeDtypeStruct((T, D_MODEL), bf16),
        ],
        scratch_shapes=[pltpu.VMEM((1, LANES), f32)],
        compiler_params=_params(("arbitrary",)),
        name="proj",
    )(x2d, kv_norm, b_norm, wkt, wv, wf, bfv, wq, wg, gq, gk)


def _attn_kernel(q_ref, kt_ref, v_ref, o_ref, s_scr, m_scr, l_scr, acc_scr):
    qi = pl.program_id(2)
    tiles = ATT_Q // ATT_K
    groups = ATT_K // LANES
    heads = [slice(hh * HEAD_PAD, (hh + 1) * HEAD_PAD) for hh in range(2)]
    m_scr[...] = jnp.full(m_scr.shape, -jnp.inf, f32)

    def lane_max(mx, s):
        for g in range(groups):
            mx = jnp.maximum(mx, s[:, g * LANES:(g + 1) * LANES])
        return mx

    def probs(s, mb):
        ps = [jnp.exp2(s[:, g * LANES:(g + 1) * LANES] - mb) for g in range(groups)]
        return ps, functools.reduce(lambda a, b: a + b, ps)

    def body_a(trip, c):
        for t in range(tiles):
            kt_idx = trip * tiles + t
            for hh in range(2):
                s = _dot(q_ref[:, heads[hh]], kt_ref[kt_idx, heads[hh], :])
                s_scr[hh, kt_idx] = s
                m_scr[hh] = lane_max(m_scr[hh], s)
        return c

    lax.fori_loop(0, qi, body_a, 0)
    for t in range(tiles):
        kt_idx = qi * tiles + t
        rows = ATT_Q - t * ATT_K
        visible = (lax.broadcasted_iota(jnp.int32, (rows, ATT_K), 1)
                   <= lax.broadcasted_iota(jnp.int32, (rows, ATT_K), 0))
        for hh in range(2):
            s = _dot(q_ref[t * ATT_K:, heads[hh]], kt_ref[kt_idx, heads[hh], :])
            s = jnp.where(visible, s, -jnp.inf)
            s_scr[hh, kt_idx, t * ATT_K:, :] = s
            m_scr[hh, t * ATT_K:, :] = lane_max(m_scr[hh, t * ATT_K:, :], s)

    for hh in range(2):
        row_max = jnp.max(m_scr[hh], axis=-1, keepdims=True)
        m_scr[hh] = jnp.broadcast_to(row_max, (ATT_Q, LANES))
    l_scr[...] = jnp.zeros(l_scr.shape, f32)
    acc_scr[...] = jnp.zeros(acc_scr.shape, f32)

    def body_b(trip, c):
        for hh in range(2):
            mb = m_scr[hh]
            lsum, acc = l_scr[hh], acc_scr[hh]
            for t0 in range(0, tiles, PV_TILES):
                kt_idx = trip * tiles + t0
                row0 = pl.multiple_of(kt_idx * ATT_K, PV_TILES * ATT_K)
                vb = v_ref[pl.ds(row0, PV_TILES * ATT_K), :]
                parts = []
                for t in range(PV_TILES):
                    ps, psum = probs(s_scr[hh, kt_idx + t], mb)
                    parts += ps
                    lsum = lsum + psum
                acc = acc + _dot(jnp.concatenate(parts, axis=1).astype(bf16), vb)
            l_scr[hh] = lsum
            acc_scr[hh] = acc
        return c

    lax.fori_loop(0, qi, body_b, 0)
    for t in range(tiles):
        kt_idx = qi * tiles + t
        row0 = pl.multiple_of(kt_idx * ATT_K, ATT_K)
        vb = v_ref[pl.ds(row0, ATT_K), :]
        for hh in range(2):
            ps, psum = probs(s_scr[hh, kt_idx, t * ATT_K:, :], m_scr[hh, t * ATT_K:, :])
            l_scr[hh, t * ATT_K:, :] += psum
            acc_scr[hh, t * ATT_K:, :] += _dot(jnp.concatenate(ps, axis=1).astype(bf16), vb)

    lane = lax.broadcasted_iota(jnp.int32, (ATT_Q, 2 * HEAD_DIM), 1)
    o0 = acc_scr[0] / jnp.sum(l_scr[0], axis=-1, keepdims=True)
    o1 = acc_scr[1] / jnp.sum(l_scr[1], axis=-1, keepdims=True)
    o_ref[...] = jnp.where(lane < HEAD_DIM, o0, o1).astype(bf16)


def _attention(q, kt, v, batch, seq):
    T = q.shape[0]
    nq = seq // ATT_Q
    nk = seq // ATT_K
    pairs = N_HEADS // 2
    return pl.pallas_call(
        _attn_kernel,
        grid=(batch, pairs, nq),
        in_specs=[
            pl.BlockSpec((ATT_Q, 2 * HEAD_PAD), lambda b, p, i: (b * nq + i, p)),
            pl.BlockSpec((nk, 2 * HEAD_PAD, ATT_K), lambda b, p, i: (b, p, 0)),
            pl.BlockSpec((seq, 2 * HEAD_DIM), lambda b, p, i: (b, p)),
        ],
        out_specs=pl.BlockSpec((ATT_Q, 2 * HEAD_DIM), lambda b, p, i: (b * nq + i, p)),
        out_shape=jax.ShapeDtypeStruct((T, D_MODEL), bf16),
        scratch_shapes=[
            pltpu.VMEM((2, nk, ATT_Q, ATT_K), f32),
            pltpu.VMEM((2, ATT_Q, LANES), f32),
            pltpu.VMEM((2, ATT_Q, LANES), f32),
            pltpu.VMEM((2, ATT_Q, 2 * HEAD_DIM), f32),
        ],
        compiler_params=_params(("parallel", "parallel", "parallel")),
        name="attn",
    )(q, kt, v)


def _attn_out_kernel(x_ref, o_ref, sg_ref, wo_ref, mn_ref, wr_ref, x3_ref, lg_ref):
    for r0 in range(0, SGU_TILE, EPILOGUE_ROWS):
        rows = slice(r0, r0 + EPILOGUE_ROWS)
        gated = (o_ref[rows, :].astype(f32) * sg_ref[rows, :].astype(f32)).astype(bf16)
        x3 = x_ref[rows, :] + _dot(gated, wo_ref[...])
        x3_ref[rows, :] = x3
        lg_ref[:, rows] = _router_logits_t(x3, mn_ref[...], wr_ref[...])


def _attn_out(x2d, o, sg, wo, m_norm, w_router):
    T = x2d.shape[0]
    return pl.pallas_call(
        _attn_out_kernel,
        grid=(T // SGU_TILE,),
        in_specs=[
            pl.BlockSpec((SGU_TILE, D_MODEL), lambda i: (i, 0)),
            pl.BlockSpec((SGU_TILE, D_MODEL), lambda i: (i, 0)),
            pl.BlockSpec((SGU_TILE, D_MODEL), lambda i: (i, 0)),
            _const_spec((D_MODEL, D_MODEL)),
            _const_spec((1, D_MODEL)),
            _const_spec((2 * D_MODEL, 2 * LANES)),
        ],
        out_specs=[
            pl.BlockSpec((SGU_TILE, D_MODEL), lambda i: (i, 0)),
            pl.BlockSpec((ROUTER_ROWS, SGU_TILE), lambda i: (0, i)),
        ],
        out_shape=[
            jax.ShapeDtypeStruct((T, D_MODEL), f32),
            jax.ShapeDtypeStruct((ROUTER_ROWS, T), f32),
        ],
        compiler_params=_params(("parallel",)),
        name="attn_out",
    )(x2d, o, sg, wo, m_norm, w_router)


def _router_weight(w_group, w_expert):
    pad = jnp.zeros((D_MODEL, LANES - N_EXPERTS - N_GROUPS), f32)
    w = jnp.concatenate([w_expert, w_group, pad], axis=1)
    w_hi = w.astype(bf16)
    w_lo = (w - w_hi.astype(f32)).astype(bf16)
    top = jnp.concatenate([w_hi, w_lo], axis=1)
    bottom = jnp.concatenate([w_hi, jnp.zeros_like(w_hi)], axis=1)
    return jnp.concatenate([top, bottom], axis=0)


def kernel(x, a_norm, a_w_in, a_b_in, a_v_norm, a_w_s, a_b_s, a_w_out, kv_norm, kv_w, kv_b_f,
           k_norm, b_norm, b_w_qg, q_norm, b_w_out, m_norm, m_w_group, m_b_group, m_w_expert,
           m_b_expert, m_w1, m_w3, m_w2):
    batch, seq, _ = x.shape
    T = batch * seq
    x2d = x.reshape(T, D_MODEL)

    x1, lg0 = _sgu(
        x2d, a_norm[0].reshape(1, -1), a_w_in[0].astype(bf16), a_b_in[0].reshape(1, -1),
        a_v_norm[0].reshape(1, -1), a_w_s[0], a_b_s[0].T, a_w_out[0].astype(bf16),
        m_norm[0].reshape(1, -1), _router_weight(m_w_group[0], m_w_expert[0]))
    x2 = _moe(0, x1, lg0, m_norm[0].reshape(1, -1), m_b_group[0], m_b_expert[0],
              m_w1, m_w3, m_w2)

    wk = kv_w[:, :D_MODEL]
    wv = kv_w[:, D_MODEL:2 * D_MODEL]
    wf = jnp.pad(kv_w[:, 2 * D_MODEL:], ((0, 0), (0, LANES - N_HEADS)))
    bfv = jnp.pad(kv_b_f, (0, LANES - N_HEADS)).reshape(1, LANES)
    wq = b_w_qg[0][:, :D_MODEL]
    wg = b_w_qg[0][:, D_MODEL:]
    gq = jnp.pad(q_norm[0], (0, HEAD_PAD - HEAD_DIM)).reshape(1, HEAD_PAD)
    gk = k_norm.reshape(HEAD_DIM, 1)
    wq_pad = jnp.pad(wq.reshape(D_MODEL, N_HEADS, HEAD_DIM),
                     ((0, 0), (0, 0), (0, HEAD_PAD - HEAD_DIM))).reshape(D_MODEL, -1)
    q, kt, v, sg = _proj(
        x2, seq, kv_norm.reshape(1, -1), b_norm[0].reshape(1, -1),
        wk.T.astype(bf16), wv.astype(bf16), wf.astype(bf16), bfv,
        wq_pad.astype(bf16), wg.astype(bf16), gq, gk)
    o = _attention(q, kt, v, batch, seq)
    x3, lg1 = _attn_out(x2, o, sg, b_w_out[0].astype(bf16), m_norm[1].reshape(1, -1),
                        _router_weight(m_w_group[1], m_w_expert[1]))
    x4 = _moe(1, x3, lg1, m_norm[1].reshape(1, -1), m_b_group[1], m_b_expert[1],
              m_w1, m_w3, m_w2)
    return x4.reshape(batch, seq, D_MODEL)
```

```python
import functools
import math

import jax
import jax.numpy as jnp
import numpy as np
from jax import lax
from jax.experimental import pallas as pl
from jax.experimental.pallas import tpu as pltpu

D_MODEL = 1024
EPS = 1e-6
SGU_BLOCK = 128
SGU_CHUNK = 64
CHUNK_SHIFT = SGU_CHUNK.bit_length() - 1
SGU_WIDTH = 2 * D_MODEL
SGU_GROUPS = 8
SGU_GROUP_DIM = SGU_WIDTH // SGU_GROUPS
N_HEADS = 16
HEAD_DIM = D_MODEL // N_HEADS
N_GROUPS = 4
EXPERTS_PER_GROUP = 8
N_EXPERTS = N_GROUPS * EXPERTS_PER_GROUP
GROUP_SHIFT = EXPERTS_PER_GROUP.bit_length() - 1
TOP_K = 2
D_EXPERT = D_MODEL // 2

LANES = 128
ROW_SUBLANES = D_MODEL // 2 // LANES
ISSUE_UNROLL = 8
VMEM_LIMIT_BYTES = 56 * 1024 * 1024

ROW_TILE = 256
SGU_TILE = 512
EPILOGUE_ROWS = 256
ROUTE_TILE = 1024
EXPERT_ROWS = 512
MOVE_TILE = 512
MOVE_CHUNK = 64
ATT_Q = 1024
ATT_K = 256
PV_TILES = 2
HEAD_PAD = 2 * HEAD_DIM
ROUTER_ROWS = 40
AUX_ONE0 = HEAD_DIM
AUX_CUM0 = HEAD_DIM + 3

LOG2E = math.log2(math.e)

bf16 = jnp.bfloat16
f32 = jnp.float32
u32 = jnp.uint32
HI_HALF = np.uint32(0xFFFF0000)


def _dot(a, b, precision=None):
    return jnp.dot(a, b, preferred_element_type=f32, precision=precision)


def _dot_nt(a, b, precision=None):
    return lax.dot_general(a, b, (((1,), (1,)), ((), ())), preferred_element_type=f32,
                           precision=precision)


def _rms(x, g):
    ms = jnp.mean(x * x, axis=-1, keepdims=True)
    return x * lax.rsqrt(ms + EPS) * g


def _gelu_tanh(z):
    c = math.sqrt(2.0 / math.pi)
    return z * (0.5 * (1.0 + jnp.tanh(c * (z + 0.044715 * (z * z * z)))))


def _split3(c):
    hi = c.astype(bf16).astype(f32)
    r = c - hi
    mid = r.astype(bf16).astype(f32)
    lo = r - mid
    return hi, mid, lo


def _pack_rows(v):
    half = D_MODEL // 2
    lo = lax.bitcast_convert_type(v[:, :half].astype(bf16).astype(f32), u32) >> 16
    hi = lax.bitcast_convert_type(v[:, half:].astype(bf16).astype(f32), u32) & HI_HALF
    return (hi | lo).reshape(v.shape[0], ROW_SUBLANES, LANES)


def _unpack_rows(w):
    w = w.reshape(w.shape[0], D_MODEL // 2)
    lo = lax.bitcast_convert_type(w << 16, f32)
    hi = lax.bitcast_convert_type(w & HI_HALF, f32)
    return jnp.concatenate([lo, hi], axis=1)


def _const_spec(shape, single=False):
    nd = len(shape)
    mode = pl.Buffered(1) if single else None
    return pl.BlockSpec(shape, lambda *_: (0,) * nd, pipeline_mode=mode)


def _params(sem):
    return pltpu.CompilerParams(dimension_semantics=sem, vmem_limit_bytes=VMEM_LIMIT_BYTES)


def _router_logits_t(x_new, mn, wr3):
    hm = _rms(x_new, mn)
    h_hi = hm.astype(bf16)
    h_lo = (hm - h_hi.astype(f32)).astype(bf16)
    both = _dot(jnp.concatenate([h_hi, h_lo], axis=1), wr3)
    lg = both[:, :LANES] + both[:, LANES:]
    return lg.T[:ROUTER_ROWS, :]


def _sgu_kernel(x_ref, an_ref, win_ref, bin_ref, vn_ref, ws_ref, bst_ref, wout_ref, mn_ref,
                wr_ref, x1_ref, lg_ref, u_scr, v_scr, gated_scr):
    x = x_ref[...]
    h = _rms(x, an_ref[...]).astype(bf16)
    v = _gelu_tanh(_dot(h, win_ref[:, SGU_WIDTH:]) + bin_ref[:, SGU_WIDTH:])
    v_scr[...] = _rms(v, vn_ref[...]).astype(bf16)
    u_scr[...] = _gelu_tanh(_dot(h, win_ref[:, :SGU_WIDTH]) + bin_ref[:, :SGU_WIDTH])

    t_chunk = lax.broadcasted_iota(jnp.int32, (SGU_BLOCK, SGU_BLOCK), 0) >> CHUNK_SHIFT
    s_chunk = lax.broadcasted_iota(jnp.int32, (SGU_BLOCK, SGU_BLOCK), 1) >> CHUNK_SHIFT
    causal = t_chunk >= s_chunk
    wsm = [jnp.where(causal, ws_ref[g], 0.0).astype(bf16) for g in range(SGU_GROUPS)]
    for r0 in range(0, SGU_TILE, EPILOGUE_ROWS):
        for g in range(SGU_GROUPS):
            cols = slice(g * SGU_GROUP_DIM, (g + 1) * SGU_GROUP_DIM)
            for sb in range(r0 // SGU_BLOCK, (r0 + EPILOGUE_ROWS) // SGU_BLOCK):
                rows = slice(sb * SGU_BLOCK, (sb + 1) * SGU_BLOCK)
                mixed = _dot(wsm[g], v_scr[rows, cols]) + bst_ref[:, g:g + 1]
                gated_scr[rows, cols] = (u_scr[rows, cols] * mixed).astype(bf16)
        half = slice(r0, r0 + EPILOGUE_ROWS)
        x1 = x[half, :] + _dot(gated_scr[half, :], wout_ref[...])
        x1_ref[half, :] = x1
        lg_ref[:, half] = _router_logits_t(x1, mn_ref[...], wr_ref[...])


def _sgu(x2d, a_norm, w_in, b_in, v_norm, w_s, b_s_t, w_out, m_norm, w_router):
    T = x2d.shape[0]
    return pl.pallas_call(
        _sgu_kernel,
        grid=(T // SGU_TILE,),
        in_specs=[
            pl.BlockSpec((SGU_TILE, D_MODEL), lambda i: (i, 0)),
            _const_spec((1, D_MODEL)),
            _const_spec((D_MODEL, 2 * SGU_WIDTH), single=True),
            _const_spec((1, 2 * SGU_WIDTH)),
            _const_spec((1, SGU_WIDTH)),
            _const_spec((SGU_GROUPS, SGU_BLOCK, SGU_BLOCK)),
            _const_spec((SGU_BLOCK, SGU_GROUPS)),
            _const_spec((SGU_WIDTH, D_MODEL), single=True),
            _const_spec((1, D_MODEL)),
            _const_spec((2 * D_MODEL, 2 * LANES)),
        ],
        out_specs=[
            pl.BlockSpec((SGU_TILE, D_MODEL), lambda i: (i, 0)),
            pl.BlockSpec((ROUTER_ROWS, SGU_TILE), lambda i: (0, i)),
        ],
        out_shape=[
            jax.ShapeDtypeStruct((T, D_MODEL), f32),
            jax.ShapeDtypeStruct((ROUTER_ROWS, T), f32),
        ],
        scratch_shapes=[
            pltpu.VMEM((SGU_TILE, SGU_WIDTH), f32),
            pltpu.VMEM((SGU_TILE, SGU_WIDTH), bf16),
            pltpu.VMEM((SGU_TILE, SGU_WIDTH), bf16),
        ],
        compiler_params=_params(("parallel",)),
        name="sgu",
    )(x2d, a_norm, w_in, b_in, v_norm, w_s, b_s_t, w_out, m_norm, w_router)


def _route_kernel(lg_ref, be_ref, bg_ref, oi_ref, ow_ref, cnt_ref, carry_scr, before_scr,
                  w_scr):
    step = pl.program_id(0)
    tt = lg_ref.shape[1]

    @pl.when(step == 0)
    def _():
        carry_scr[...] = jnp.zeros_like(carry_scr)
        before_scr[...] = (lax.broadcasted_iota(jnp.int32, (tt, tt), 0)
                           < lax.broadcasted_iota(jnp.int32, (tt, tt), 1)).astype(bf16)

    lg = lg_ref[...]
    e_l = lg[0:N_EXPERTS, :] + be_ref[...]
    g_l = lg[N_EXPERTS:N_EXPERTS + 8, :] + bg_ref[...]
    g_row = lax.broadcasted_iota(jnp.int32, g_l.shape, 0).astype(f32)
    g_l = jnp.where(g_row < N_GROUPS, g_l, -jnp.inf)
    g_max = jnp.max(g_l, axis=0, keepdims=True)
    g_sel = jnp.min(jnp.where(g_l == g_max, g_row, 8.0), axis=0, keepdims=True)
    g_den = jnp.sum(jnp.exp(g_l - g_max), axis=0, keepdims=True)
    g_w = 1.0 / g_den

    e_row_i = lax.broadcasted_iota(jnp.int32, e_l.shape, 0)
    e_row = e_row_i.astype(f32)
    e_grp = (e_row_i >> GROUP_SHIFT).astype(f32)
    e_in = jnp.where(e_grp == g_sel, e_l, -jnp.inf)
    m1 = jnp.max(e_in, axis=0, keepdims=True)
    i1 = jnp.min(jnp.where(e_in == m1, e_row, float(N_EXPERTS)), axis=0, keepdims=True)
    e_in2 = jnp.where(e_row == i1, -jnp.inf, e_in)
    m2 = jnp.max(e_in2, axis=0, keepdims=True)
    i2 = jnp.min(jnp.where(e_in2 == m2, e_row, float(N_EXPERTS)), axis=0, keepdims=True)
    t = jnp.exp(m2 - m1)
    p1 = 1.0 / (1.0 + t)
    p2 = t / (1.0 + t)

    hit1 = e_row == i1
    hit2 = e_row == i2
    member = jnp.logical_or(hit1, hit2)
    prefix = _dot(member.astype(bf16), before_scr[...])
    rank_full = prefix + carry_scr[...]
    rank1 = jnp.sum(jnp.where(hit1, rank_full, 0.0), axis=0, keepdims=True)
    rank2 = jnp.sum(jnp.where(hit2, rank_full, 0.0), axis=0, keepdims=True)
    carry_new = carry_scr[...] + jnp.sum(member.astype(f32), axis=1, keepdims=True)
    carry_scr[...] = carry_new

    oi_ref[...] = jnp.zeros(oi_ref.shape, jnp.int32)
    oi_ref[0:1, :] = i1.astype(jnp.int32)
    oi_ref[1:2, :] = i2.astype(jnp.int32)
    oi_ref[2:3, :] = rank1.astype(jnp.int32)
    oi_ref[3:4, :] = rank2.astype(jnp.int32)
    w_scr[...] = jnp.zeros(w_scr.shape, f32)
    w_scr[0:1, :] = g_w * p1
    w_scr[1:2, :] = g_w * p2
    ow_ref[...] = w_scr[...].T
    cnt_ref[...] = jnp.broadcast_to(carry_new, cnt_ref.shape)


def _route(logits_t, b_expert, b_group):
    T = logits_t.shape[1]
    return pl.pallas_call(
        _route_kernel,
        grid=(T // ROUTE_TILE,),
        in_specs=[
            pl.BlockSpec((ROUTER_ROWS, ROUTE_TILE), lambda i: (0, i)),
            _const_spec((N_EXPERTS, 1)),
            _const_spec((8, 1)),
        ],
        out_specs=[
            pl.BlockSpec((8, ROUTE_TILE), lambda i: (0, i)),
            pl.BlockSpec((ROUTE_TILE, 8), lambda i: (i, 0)),
            _const_spec((N_EXPERTS, LANES)),
        ],
        out_shape=[
            jax.ShapeDtypeStruct((8, T), jnp.int32),
            jax.ShapeDtypeStruct((T, 8), f32),
            jax.ShapeDtypeStruct((N_EXPERTS, LANES), f32),
        ],
        scratch_shapes=[pltpu.VMEM((N_EXPERTS, 1), f32),
                        pltpu.VMEM((ROUTE_TILE, ROUTE_TILE), bf16),
                        pltpu.VMEM((8, ROUTE_TILE), f32)],
        compiler_params=_params(("arbitrary",)),
        name="route",
    )(logits_t, b_expert, b_group)


def _row_copy(src, s, dst, d, sem):
    return pltpu.make_async_copy(src.at[s], dst.at[d], sem)


def _wait_rows(buf, sem):
    pltpu.make_async_copy(buf, buf, sem).wait()


def _dispatch_kernel(n_blocks, pend_ref, padded_ref, nu_ref, dest_ref, x_ref, mn_ref, xs_hbm,
                     zero_scr, x3_scr, zsem, sem):
    @pl.when(pl.program_id(0) == 0)
    def _():
        zero_scr[...] = jnp.zeros_like(zero_scr)

        def fill(row0):
            return pltpu.make_async_copy(zero_scr, xs_hbm.at[pl.ds(row0, EXPERT_ROWS)], zsem)

        for e in range(N_EXPERTS):
            @pl.when(padded_ref[e] > 0)
            def _():
                fill(pend_ref[e] - EXPERT_ROWS).start()

            @pl.when(nu_ref[0] + e < n_blocks)
            def _():
                fill((nu_ref[0] + e) * EXPERT_ROWS).start()

        for e in range(N_EXPERTS):
            @pl.when(padded_ref[e] > 0)
            def _():
                fill(0).wait()

            @pl.when(nu_ref[0] + e < n_blocks)
            def _():
                fill(0).wait()

    i = pl.program_id(0)
    n_tiles = pl.num_programs(0) - 1

    def staged(rows):
        return _pack_rows(_rms(x_ref[rows, :], mn_ref[...]))

    @pl.when(i == 0)
    def _():
        x3_scr[0] = staged(slice(None))

    @pl.when(i > 0)
    def _():
        cur = (i - 1) % 3
        nxt = i % 3

        def chunk(c, carry):
            r0 = pl.multiple_of(c * MOVE_CHUNK, MOVE_CHUNK)
            for j in range(MOVE_CHUNK):
                for k in range(TOP_K):
                    _row_copy(x3_scr.at[cur], r0 + j, xs_hbm,
                              dest_ref[0, k * MOVE_TILE + r0 + j], sem.at[cur]).start(priority=k)
            x3_scr[nxt, pl.ds(r0, MOVE_CHUNK)] = staged(pl.ds(r0, MOVE_CHUNK))
            return carry

        lax.fori_loop(0, MOVE_TILE // MOVE_CHUNK, chunk, 0)

        @pl.when(i > 1)
        def _():
            for k in range(TOP_K):
                _wait_rows(x3_scr.at[(i - 2) % 3], sem.at[(i - 2) % 3])

        @pl.when(i == n_tiles)
        def _():
            for k in range(TOP_K):
                _wait_rows(x3_scr.at[cur], sem.at[cur])


def _dispatch(pend, padded, n_used, dest_tiles, x2d, m_norm, n_rows):
    T = x2d.shape[0]
    n_blocks = n_rows // EXPERT_ROWS
    n_tiles = T // MOVE_TILE
    grid_spec = pltpu.PrefetchScalarGridSpec(
        num_scalar_prefetch=3,
        grid=(n_tiles + 1,),
        in_specs=[
            pl.BlockSpec((None, 1, TOP_K * MOVE_TILE),
                         lambda i, *_: (jnp.maximum(i - 1, 0), 0, 0), memory_space=pltpu.SMEM),
            pl.BlockSpec((MOVE_TILE, D_MODEL), lambda i, *_: (jnp.minimum(i, n_tiles - 1), 0)),
            pl.BlockSpec((1, D_MODEL), lambda i, *_: (0, 0)),
        ],
        out_specs=pl.BlockSpec(memory_space=pl.ANY),
        scratch_shapes=[
            pltpu.VMEM((EXPERT_ROWS, ROW_SUBLANES, LANES), u32),
            pltpu.VMEM((3, MOVE_TILE, ROW_SUBLANES, LANES), u32),
            pltpu.SemaphoreType.DMA(()),
            pltpu.SemaphoreType.DMA((3,)),
        ],
    )
    return pl.pallas_call(
        functools.partial(_dispatch_kernel, n_blocks),
        grid_spec=grid_spec,
        out_shape=jax.ShapeDtypeStruct((n_rows, ROW_SUBLANES, LANES), u32),
        compiler_params=_params(("arbitrary",)),
        name="dispatch",
    )(pend, padded, n_used, dest_tiles, x2d, m_norm)


def _expert_kernel(n_blocks, ps_ref, nb_ref, nu_ref, xs_hbm, w1_ref, w3_ref, w2_ref,
                   ys_hbm, xbuf, ybuf, h_scr, w1_scr, w3_scr, w2_scr, xsem, ysem):
    e = pl.program_id(0)
    nblk = nb_ref[e]
    g0 = ps_ref[e]
    n_used = nu_ref[0]

    def rows(g):
        return pl.ds(pl.multiple_of(g * EXPERT_ROWS, EXPERT_ROWS), EXPERT_ROWS)

    def x_copy(g):
        return pltpu.make_async_copy(xs_hbm.at[rows(g)], xbuf.at[g % 2], xsem.at[g % 2])

    def y_copy(g):
        return pltpu.make_async_copy(ybuf.at[g % 2], ys_hbm.at[rows(g)], ysem.at[g % 2])

    def normed(slot):
        return _unpack_rows(xbuf[slot]).astype(bf16)

    @pl.when(e == 0)
    def _():
        x_copy(0).start(priority=1)
        x_copy(1).start(priority=1)
        x_copy(0).wait()
        h_scr[...] = normed(0)

    @pl.when(nblk > 0)
    def _():
        w1_scr[...] = w1_ref[...].astype(bf16)
        w3_scr[...] = w3_ref[...].astype(bf16)
        w2_scr[...] = w2_ref[...].astype(bf16)

        def body(g, h):
            @pl.when(g + 2 < n_used)
            def _():
                x_copy(g + 2).start(priority=1)

            @pl.when(g + 1 < n_used)
            def _():
                x_copy(g + 1).wait()

            @pl.when(g >= 2)
            def _():
                y_copy(g - 2).wait()

            h_next = normed((g + 1) % 2)
            y = None
            for c0 in range(0, D_EXPERT, D_EXPERT // 2):
                cols = slice(c0, c0 + D_EXPERT // 2)
                a = _dot(h, w1_scr[:, cols])
                c = _dot(h, w3_scr[:, cols])
                hid = ((a * jax.nn.sigmoid(a)) * c).astype(bf16)
                part = _dot(hid, w2_scr[cols, :])
                y = part if y is None else y + part
            ybuf[g % 2] = _pack_rows(y)
            y_copy(g).start(priority=1)
            return h_next

        h_scr[...] = lax.fori_loop(g0, g0 + nblk, body, h_scr[...])

    @pl.when(e == pl.num_programs(0) - 1)
    def _():
        y_copy(n_used - 2).wait()
        y_copy(n_used - 1).wait()
        ybuf[0] = jnp.zeros(ybuf.shape[1:], u32)

        def fill(b):
            r = pl.ds(pl.multiple_of(b * EXPERT_ROWS, EXPERT_ROWS), EXPERT_ROWS)
            return pltpu.make_async_copy(ybuf.at[0], ys_hbm.at[r], ysem.at[0])

        def start(b, c):
            fill(b).start()
            return c

        def wait(b, c):
            fill(b).wait()
            return c

        lax.fori_loop(nu_ref[0], n_blocks, start, 0)
        lax.fori_loop(nu_ref[0], n_blocks, wait, 0)


def _experts(layer, pstart, seg_blocks, n_used, xs, w1, w3, w2):
    n_rows = xs.shape[0]
    n_blocks = n_rows // EXPERT_ROWS

    def w_map(e, *_):
        return (layer, e, 0, 0)

    block_buf = pltpu.VMEM((2, EXPERT_ROWS, ROW_SUBLANES, LANES), u32)
    grid_spec = pltpu.PrefetchScalarGridSpec(
        num_scalar_prefetch=3,
        grid=(N_EXPERTS,),
        in_specs=[
            pl.BlockSpec(memory_space=pl.ANY),
            pl.BlockSpec((None, None, D_MODEL, D_EXPERT), w_map),
            pl.BlockSpec((None, None, D_MODEL, D_EXPERT), w_map),
            pl.BlockSpec((None, None, D_EXPERT, D_MODEL), w_map),
        ],
        out_specs=pl.BlockSpec(memory_space=pl.ANY),
        scratch_shapes=[
            block_buf,
            block_buf,
            pltpu.VMEM((EXPERT_ROWS, D_MODEL), bf16),
            pltpu.VMEM((D_MODEL, D_EXPERT), bf16),
            pltpu.VMEM((D_MODEL, D_EXPERT), bf16),
            pltpu.VMEM((D_EXPERT, D_MODEL), bf16),
            pltpu.SemaphoreType.DMA((2,)),
            pltpu.SemaphoreType.DMA((2,)),
        ],
    )
    return pl.pallas_call(
        functools.partial(_expert_kernel, n_blocks),
        grid_spec=grid_spec,
        out_shape=jax.ShapeDtypeStruct((n_rows, ROW_SUBLANES, LANES), u32),
        compiler_params=_params(("arbitrary",)),
        name="experts",
    )(pstart, seg_blocks, n_used, xs, w1, w3, w2)


def _combine_kernel(dest_ref, next_ref, x_ref, w_ref, ys_hbm, out_ref, buf, sem):
    i = pl.program_id(0)
    slot = i % 2

    def gather(idx_ref, s):
        def start(j, c):
            for k in range(TOP_K):
                _row_copy(ys_hbm, idx_ref[0, k * MOVE_TILE + j], buf.at[s, k], j,
                          sem.at[s]).start(priority=k)
            return c

        lax.fori_loop(0, MOVE_TILE, start, 0, unroll=ISSUE_UNROLL)

    @pl.when(i == 0)
    def _():
        gather(dest_ref, slot)

    @pl.when(i + 1 < pl.num_programs(0))
    def _():
        gather(next_ref, 1 - slot)

    for k in range(TOP_K):
        _wait_rows(buf.at[slot, k], sem.at[slot])
    w = w_ref[...]
    y0 = _unpack_rows(buf[slot, 0])
    y1 = _unpack_rows(buf[slot, 1])
    out_ref[...] = x_ref[...] + (w[:, 0:1] * y0 + w[:, 1:2] * y1)


def _combine(dest_tiles, x2d, w_cols, ys):
    T = x2d.shape[0]
    n_tiles = T // MOVE_TILE
    return pl.pallas_call(
        _combine_kernel,
        grid=(n_tiles,),
        in_specs=[
            pl.BlockSpec((None, 1, TOP_K * MOVE_TILE), lambda i: (i, 0, 0),
                         memory_space=pltpu.SMEM),
            pl.BlockSpec((None, 1, TOP_K * MOVE_TILE),
                         lambda i: (jnp.minimum(i + 1, n_tiles - 1), 0, 0),
                         memory_space=pltpu.SMEM),
            pl.BlockSpec((MOVE_TILE, D_MODEL), lambda i: (i, 0)),
            pl.BlockSpec((MOVE_TILE, 8), lambda i: (i, 0)),
            pl.BlockSpec(memory_space=pl.ANY),
        ],
        out_specs=pl.BlockSpec((MOVE_TILE, D_MODEL), lambda i: (i, 0)),
        out_shape=jax.ShapeDtypeStruct((T, D_MODEL), f32),
        scratch_shapes=[
            pltpu.VMEM((2, TOP_K, MOVE_TILE, ROW_SUBLANES, LANES), u32),
            pltpu.SemaphoreType.DMA((2,)),
        ],
        compiler_params=_params(("arbitrary",)),
        name="combine",
    )(dest_tiles, dest_tiles, x2d, w_cols, ys)


def _moe(layer, x2d, logits_t, m_norm, b_group, b_expert, w1, w3, w2):
    T = x2d.shape[0]
    n_blocks = (T * TOP_K) // EXPERT_ROWS + N_EXPERTS
    n_rows = n_blocks * EXPERT_ROWS

    be = b_expert.reshape(N_EXPERTS, 1)
    bg = jnp.concatenate([b_group, jnp.zeros((8 - N_GROUPS,), f32)]).reshape(8, 1)
    oi, w_cols, cnt = _route(logits_t, be, bg)

    counts = cnt[:, 0].astype(jnp.int32)
    padded = (counts + EXPERT_ROWS - 1) // EXPERT_ROWS * EXPERT_ROWS
    pend = jnp.cumsum(padded)
    pstart = pend - padded
    n_used = (pend[-1] // EXPERT_ROWS).astype(jnp.int32).reshape(1)
    e_ids = jnp.arange(N_EXPERTS, dtype=jnp.int32)[:, None, None]
    seg_start = jnp.sum(jnp.where(oi[None, 0:2] == e_ids, pstart[:, None, None], 0), axis=0)
    dest = seg_start + oi[2:4]
    dest_tiles = dest.reshape(TOP_K, T // MOVE_TILE, MOVE_TILE).transpose(1, 0, 2).reshape(
        T // MOVE_TILE, 1, TOP_K * MOVE_TILE)

    xs = _dispatch(pend.astype(jnp.int32), padded, n_used, dest_tiles, x2d, m_norm, n_rows)
    ys = _experts(layer, (pstart // EXPERT_ROWS).astype(jnp.int32), padded // EXPERT_ROWS,
                  n_used, xs, w1, w3, w2)
    return _combine(dest_tiles, x2d, w_cols, ys)


def _log_sigmoid(x):
    return jnp.minimum(x, 0.0) - jnp.log1p(jnp.exp(-jnp.abs(x)))


def _proj_kernel(tiles_per_seq, x_ref, kvn_ref, bn_ref, kvw_ref, bf_ref, qgw_ref, gq_ref, gk_ref,
                 q_ref, kt_ref, v_ref, sg_ref, carry_scr, wkt_ref, wv_ref, wf_ref, wq_ref, wg_ref):
    i = pl.program_id(0)

    @pl.when(i == 0)
    def _():
        wkt_ref[...] = kvw_ref[:, :D_MODEL].T.astype(bf16)
        wv_ref[...] = kvw_ref[:, D_MODEL:2 * D_MODEL].astype(bf16)
        wf_ref[...] = jnp.concatenate(
            [kvw_ref[:, 2 * D_MODEL:], jnp.zeros((D_MODEL, LANES - N_HEADS), f32)],
            axis=1).astype(bf16)
        wg_ref[...] = qgw_ref[:, D_MODEL:].astype(bf16)
        zeros = jnp.zeros((D_MODEL, HEAD_PAD - HEAD_DIM), f32)
        for h in range(N_HEADS):
            wq_ref[:, h * HEAD_PAD:(h + 1) * HEAD_PAD] = jnp.concatenate(
                [qgw_ref[:, h * HEAD_DIM:(h + 1) * HEAD_DIM], zeros], axis=1).astype(bf16)

    x = x_ref[...]
    hkv = _rms(x, kvn_ref[...]).astype(bf16)
    hq = _rms(x, bn_ref[...]).astype(bf16)

    v_ref[...] = _dot(hkv, wv_ref[...]).astype(bf16)
    sg_ref[...] = jax.nn.sigmoid(_dot(hq, wg_ref[...])).astype(bf16)

    logf = _log_sigmoid(_dot(hkv, wf_ref[...]) + bf_ref[...])
    tm = x.shape[0]
    incl = (lax.broadcasted_iota(jnp.int32, (tm, tm), 0)
            >= lax.broadcasted_iota(jnp.int32, (tm, tm), 1)).astype(bf16)

    @pl.when(i % tiles_per_seq == 0)
    def _():
        carry_scr[...] = jnp.zeros_like(carry_scr)

    parts = _dot(incl, jnp.concatenate(_split3(logf), axis=1).astype(bf16))
    cum = (parts[:, :LANES] + parts[:, LANES:2 * LANES] + parts[:, 2 * LANES:]
           + carry_scr[...])
    carry_scr[...] = cum[tm - 1:tm, :]
    cum = cum * LOG2E
    cum_t = cum.T

    lane = lax.broadcasted_iota(jnp.int32, (tm, HEAD_PAD), 1)
    sub = lax.broadcasted_iota(jnp.int32, (HEAD_DIM, tm), 0)
    q_raw = _dot(hq, wq_ref[...])
    k_raw_t = _dot_nt(wkt_ref[...], hkv)
    scale = HEAD_DIM ** -0.5 * LOG2E

    for h in range(N_HEADS):
        sl = slice(h * HEAD_PAD, (h + 1) * HEAD_PAD)
        qb = q_raw[:, sl]
        q_ms = jnp.sum(qb * qb, axis=-1, keepdims=True) * (1.0 / HEAD_DIM)
        qn = qb * lax.rsqrt(q_ms + EPS) * gq_ref[...] * scale
        c_hi, c_mid, c_lo = _split3(cum[:, h:h + 1])
        qa = jnp.where(lane < HEAD_DIM, qn,
             jnp.where(lane < AUX_CUM0, 1.0,
             jnp.where(lane == AUX_CUM0, c_hi,
             jnp.where(lane == AUX_CUM0 + 1, c_mid,
             jnp.where(lane == AUX_CUM0 + 2, c_lo, 0.0)))))
        q_ref[:, sl] = qa.astype(bf16)

        kb = k_raw_t[h * HEAD_DIM:(h + 1) * HEAD_DIM, :]
        k_ms = jnp.sum(kb * kb, axis=0, keepdims=True) * (1.0 / HEAD_DIM)
        kn = kb * lax.rsqrt(k_ms + EPS) * gk_ref[...]
        t_hi, t_mid, t_lo = _split3(cum_t[h:h + 1, :])
        aux = jnp.where(sub == 0, -t_hi,
              jnp.where(sub == 1, -t_mid,
              jnp.where(sub == 2, -t_lo,
              jnp.where(sub < 6, 1.0, 0.0))))
        sl = slice(h * HEAD_PAD, (h + 1) * HEAD_PAD)
        ka = jnp.concatenate([kn, aux], axis=0).astype(bf16)
        for t in range(tm // ATT_K):
            kt_ref[t, sl, :] = ka[:, t * ATT_K:(t + 1) * ATT_K]


def _proj(x2d, seq, kv_norm, b_norm, kv_w, bfv, w_qg, gq, gk):
    T = x2d.shape[0]
    n_tiles = T // ROW_TILE
    qw = N_HEADS * HEAD_PAD
    return pl.pallas_call(
        functools.partial(_proj_kernel, seq // ROW_TILE),
        grid=(n_tiles,),
        in_specs=[
            pl.BlockSpec((ROW_TILE, D_MODEL), lambda i: (i, 0)),
            _const_spec((1, D_MODEL)),
            _const_spec((1, D_MODEL)),
            _const_spec(kv_w.shape, single=True),
            _const_spec((1, LANES)),
            _const_spec(w_qg.shape, single=True),
            _const_spec((1, HEAD_PAD)),
            _const_spec((HEAD_DIM, 1)),
        ],
        out_specs=[
            pl.BlockSpec((ROW_TILE, qw), lambda i: (i, 0)),
            pl.BlockSpec((ROW_TILE // ATT_K, qw, ATT_K), lambda i: (i, 0, 0)),
            pl.BlockSpec((ROW_TILE, D_MODEL), lambda i: (i, 0)),
            pl.BlockSpec((ROW_TILE, D_MODEL), lambda i: (i, 0)),
        ],
        out_shape=[
            jax.ShapeDtypeStruct((T, qw), bf16),
            jax.ShapeDtypeStruct((T // ATT_K, qw, ATT_K), bf16),
            jax.ShapeDtypeStruct((T, D_MODEL), bf16),
            jax.ShapeDtypeStruct((T, D_MODEL), bf16),
        ],
        scratch_shapes=[
            pltpu.VMEM((1, LANES), f32),
            pltpu.VMEM((D_MODEL, D_MODEL), bf16),
            pltpu.VMEM((D_MODEL, D_MODEL), bf16),
            pltpu.VMEM((D_MODEL, LANES), bf16),
            pltpu.VMEM((D_MODEL, qw), bf16),
            pltpu.VMEM((D_MODEL, D_MODEL), bf16),
        ],
        compiler_params=_params(("arbitrary",)),
        name="proj",
    )(x2d, kv_norm, b_norm, kv_w, bfv, w_qg, gq, gk)


def _attn_kernel(q_ref, kt_ref, v_ref, o_ref, s_scr, m_scr, l_scr, acc_scr):
    qi = pl.program_id(2)
    tiles = ATT_Q // ATT_K
    groups = ATT_K // LANES
    heads = [slice(hh * HEAD_PAD, (hh + 1) * HEAD_PAD) for hh in range(2)]
    m_scr[...] = jnp.full(m_scr.shape, -jnp.inf, f32)

    def lane_max(mx, s):
        for g in range(groups):
            mx = jnp.maximum(mx, s[:, g * LANES:(g + 1) * LANES])
        return mx

    def probs(s, mb):
        ps = [jnp.exp2(s[:, g * LANES:(g + 1) * LANES] - mb) for g in range(groups)]
        return ps, functools.reduce(lambda a, b: a + b, ps)

    def body_a(trip, c):
        for t in range(tiles):
            kt_idx = trip * tiles + t
            for hh in range(2):
                s = _dot(q_ref[:, heads[hh]], kt_ref[kt_idx, heads[hh], :])
                s_scr[hh, kt_idx] = s
                m_scr[hh] = lane_max(m_scr[hh], s)
        return c

    lax.fori_loop(0, qi, body_a, 0)
    for t in range(tiles):
        kt_idx = qi * tiles + t
        rows = ATT_Q - t * ATT_K
        visible = (lax.broadcasted_iota(jnp.int32, (rows, ATT_K), 1)
                   <= lax.broadcasted_iota(jnp.int32, (rows, ATT_K), 0))
        for hh in range(2):
            s = _dot(q_ref[t * ATT_K:, heads[hh]], kt_ref[kt_idx, heads[hh], :])
            s = jnp.where(visible, s, -jnp.inf)
            s_scr[hh, kt_idx, t * ATT_K:, :] = s
            m_scr[hh, t * ATT_K:, :] = lane_max(m_scr[hh, t * ATT_K:, :], s)

    for hh in range(2):
        row_max = jnp.max(m_scr[hh], axis=-1, keepdims=True)
        m_scr[hh] = jnp.broadcast_to(row_max, (ATT_Q, LANES))
    l_scr[...] = jnp.zeros(l_scr.shape, f32)
    acc_scr[...] = jnp.zeros(acc_scr.shape, f32)

    def body_b(trip, c):
        for hh in range(2):
            mb = m_scr[hh]
            lsum, acc = l_scr[hh], acc_scr[hh]
            for t0 in range(0, tiles, PV_TILES):
                kt_idx = trip * tiles + t0
                row0 = pl.multiple_of(kt_idx * ATT_K, PV_TILES * ATT_K)
                vb = v_ref[pl.ds(row0, PV_TILES * ATT_K), :]
                parts = []
                for t in range(PV_TILES):
                    ps, psum = probs(s_scr[hh, kt_idx + t], mb)
                    parts += ps
                    lsum = lsum + psum
                acc = acc + _dot(jnp.concatenate(parts, axis=1).astype(bf16), vb)
            l_scr[hh] = lsum
            acc_scr[hh] = acc
        return c

    lax.fori_loop(0, qi, body_b, 0)
    for t in range(tiles):
        kt_idx = qi * tiles + t
        row0 = pl.multiple_of(kt_idx * ATT_K, ATT_K)
        vb = v_ref[pl.ds(row0, ATT_K), :]
        for hh in range(2):
            ps, psum = probs(s_scr[hh, kt_idx, t * ATT_K:, :], m_scr[hh, t * ATT_K:, :])
            l_scr[hh, t * ATT_K:, :] += psum
            acc_scr[hh, t * ATT_K:, :] += _dot(jnp.concatenate(ps, axis=1).astype(bf16), vb)

    lane = lax.broadcasted_iota(jnp.int32, (ATT_Q, 2 * HEAD_DIM), 1)
    o0 = acc_scr[0] / jnp.sum(l_scr[0], axis=-1, keepdims=True)
    o1 = acc_scr[1] / jnp.sum(l_scr[1], axis=-1, keepdims=True)
    o_ref[...] = jnp.where(lane < HEAD_DIM, o0, o1).astype(bf16)


def _attention(q, kt, v, batch, seq):
    T = q.shape[0]
    nq = seq // ATT_Q
    nk = seq // ATT_K
    pairs = N_HEADS // 2
    return pl.pallas_call(
        _attn_kernel,
        grid=(batch, pairs, nq),
        in_specs=[
            pl.BlockSpec((ATT_Q, 2 * HEAD_PAD), lambda b, p, i: (b * nq + i, p)),
            pl.BlockSpec((nk, 2 * HEAD_PAD, ATT_K), lambda b, p, i: (b, p, 0)),
            pl.BlockSpec((seq, 2 * HEAD_DIM), lambda b, p, i: (b, p)),
        ],
        out_specs=pl.BlockSpec((ATT_Q, 2 * HEAD_DIM), lambda b, p, i: (b * nq + i, p)),
        out_shape=jax.ShapeDtypeStruct((T, D_MODEL), bf16),
        scratch_shapes=[
            pltpu.VMEM((2, nk, ATT_Q, ATT_K), f32),
            pltpu.VMEM((2, ATT_Q, LANES), f32),
            pltpu.VMEM((2, ATT_Q, LANES), f32),
            pltpu.VMEM((2, ATT_Q, 2 * HEAD_DIM), f32),
        ],
        compiler_params=_params(("parallel", "parallel", "parallel")),
        name="attn",
    )(q, kt, v)


def _attn_out_kernel(x_ref, o_ref, sg_ref, wo_ref, mn_ref, wr_ref, x3_ref, lg_ref):
    for r0 in range(0, SGU_TILE, EPILOGUE_ROWS):
        rows = slice(r0, r0 + EPILOGUE_ROWS)
        gated = (o_ref[rows, :].astype(f32) * sg_ref[rows, :].astype(f32)).astype(bf16)
        x3 = x_ref[rows, :] + _dot(gated, wo_ref[...])
        x3_ref[rows, :] = x3
        lg_ref[:, rows] = _router_logits_t(x3, mn_ref[...], wr_ref[...])


def _attn_out(x2d, o, sg, wo, m_norm, w_router):
    T = x2d.shape[0]
    return pl.pallas_call(
        _attn_out_kernel,
        grid=(T // SGU_TILE,),
        in_specs=[
            pl.BlockSpec((SGU_TILE, D_MODEL), lambda i: (i, 0)),
            pl.BlockSpec((SGU_TILE, D_MODEL), lambda i: (i, 0)),
            pl.BlockSpec((SGU_TILE, D_MODEL), lambda i: (i, 0)),
            _const_spec((D_MODEL, D_MODEL)),
            _const_spec((1, D_MODEL)),
            _const_spec((2 * D_MODEL, 2 * LANES)),
        ],
        out_specs=[
            pl.BlockSpec((SGU_TILE, D_MODEL), lambda i: (i, 0)),
            pl.BlockSpec((ROUTER_ROWS, SGU_TILE), lambda i: (0, i)),
        ],
        out_shape=[
            jax.ShapeDtypeStruct((T, D_MODEL), f32),
            jax.ShapeDtypeStruct((ROUTER_ROWS, T), f32),
        ],
        compiler_params=_params(("parallel",)),
        name="attn_out",
    )(x2d, o, sg, wo, m_norm, w_router)


def _router_weight(w_group, w_expert):
    pad = jnp.zeros((D_MODEL, LANES - N_EXPERTS - N_GROUPS), f32)
    w = jnp.concatenate([w_expert, w_group, pad], axis=1)
    w_hi = w.astype(bf16)
    w_lo = (w - w_hi.astype(f32)).astype(bf16)
    top = jnp.concatenate([w_hi, w_lo], axis=1)
    bottom = jnp.concatenate([w_hi, jnp.zeros_like(w_hi)], axis=1)
    return jnp.concatenate([top, bottom], axis=0)


def kernel(x, a_norm, a_w_in, a_b_in, a_v_norm, a_w_s, a_b_s, a_w_out, kv_norm, kv_w, kv_b_f,
           k_norm, b_norm, b_w_qg, q_norm, b_w_out, m_norm, m_w_group, m_b_group, m_w_expert,
           m_b_expert, m_w1, m_w3, m_w2):
    batch, seq, _ = x.shape
    T = batch * seq
    x2d = x.reshape(T, D_MODEL)

    x1, lg0 = _sgu(
        x2d, a_norm[0].reshape(1, -1), a_w_in[0].astype(bf16), a_b_in[0].reshape(1, -1),
        a_v_norm[0].reshape(1, -1), a_w_s[0], a_b_s[0].T, a_w_out[0].astype(bf16),
        m_norm[0].reshape(1, -1), _router_weight(m_w_group[0], m_w_expert[0]))
    x2 = _moe(0, x1, lg0, m_norm[0].reshape(1, -1), m_b_group[0], m_b_expert[0],
              m_w1, m_w3, m_w2)

    bfv = jnp.pad(kv_b_f, (0, LANES - N_HEADS)).reshape(1, LANES)
    gq = jnp.pad(q_norm[0], (0, HEAD_PAD - HEAD_DIM)).reshape(1, HEAD_PAD)
    gk = k_norm.reshape(HEAD_DIM, 1)
    q, kt, v, sg = _proj(x2, seq, kv_norm.reshape(1, -1), b_norm[0].reshape(1, -1),
                         kv_w, bfv, b_w_qg[0], gq, gk)
    o = _attention(q, kt, v, batch, seq)
    x3, lg1 = _attn_out(x2, o, sg, b_w_out[0].astype(bf16), m_norm[1].reshape(1, -1),
                        _router_weight(m_w_group[1], m_w_expert[1]))
    x4 = _moe(1, x3, lg1, m_norm[1].reshape(1, -1), m_b_group[1], m_b_expert[1],
              m_w1, m_w3, m_w2)
    return x4.reshape(batch, seq, D_MODEL)
```

```python
import functools
import math

import jax
import jax.numpy as jnp
import numpy as np
from jax import lax
from jax.experimental import pallas as pl
from jax.experimental.pallas import tpu as pltpu

D_MODEL = 1024
EPS = 1e-6
SGU_BLOCK = 128
SGU_CHUNK = 64
CHUNK_SHIFT = SGU_CHUNK.bit_length() - 1
SGU_WIDTH = 2 * D_MODEL
SGU_GROUPS = 8
SGU_GROUP_DIM = SGU_WIDTH // SGU_GROUPS
N_HEADS = 16
HEAD_DIM = D_MODEL // N_HEADS
N_GROUPS = 4
EXPERTS_PER_GROUP = 8
N_EXPERTS = N_GROUPS * EXPERTS_PER_GROUP
GROUP_SHIFT = EXPERTS_PER_GROUP.bit_length() - 1
TOP_K = 2
D_EXPERT = D_MODEL // 2

LANES = 128
ROW_SUBLANES = D_MODEL // 2 // LANES
ISSUE_UNROLL = 8
VMEM_LIMIT_BYTES = 56 * 1024 * 1024

ROW_TILE = 256
SGU_TILE = 512
EPILOGUE_ROWS = 256
ROUTE_TILE = 1024
EXPERT_ROWS = 512
MOVE_TILE = 512
MOVE_CHUNK = 64
ATT_Q = 1024
ATT_K = 256
PV_TILES = 2
HEAD_PAD = 2 * HEAD_DIM
ROUTER_ROWS = 40
AUX_ONE0 = HEAD_DIM
AUX_CUM0 = HEAD_DIM + 3

LOG2E = math.log2(math.e)

bf16 = jnp.bfloat16
f32 = jnp.float32
u32 = jnp.uint32
HI_HALF = np.uint32(0xFFFF0000)


def _dot(a, b, precision=None):
    return jnp.dot(a, b, preferred_element_type=f32, precision=precision)


def _dot_nt(a, b, precision=None):
    return lax.dot_general(a, b, (((1,), (1,)), ((), ())), preferred_element_type=f32,
                           precision=precision)


def _rms(x, g):
    ms = jnp.mean(x * x, axis=-1, keepdims=True)
    return x * lax.rsqrt(ms + EPS) * g


def _gelu_tanh(z):
    c = math.sqrt(2.0 / math.pi)
    return z * (0.5 * (1.0 + jnp.tanh(c * (z + 0.044715 * (z * z * z)))))


def _split3(c):
    hi = c.astype(bf16).astype(f32)
    r = c - hi
    mid = r.astype(bf16).astype(f32)
    lo = r - mid
    return hi, mid, lo


def _pack_rows(v):
    half = D_MODEL // 2
    lo = lax.bitcast_convert_type(v[:, :half].astype(bf16).astype(f32), u32) >> 16
    hi = lax.bitcast_convert_type(v[:, half:].astype(bf16).astype(f32), u32) & HI_HALF
    return (hi | lo).reshape(v.shape[0], ROW_SUBLANES, LANES)


def _unpack_rows(w):
    w = w.reshape(w.shape[0], D_MODEL // 2)
    lo = lax.bitcast_convert_type(w << 16, f32)
    hi = lax.bitcast_convert_type(w & HI_HALF, f32)
    return jnp.concatenate([lo, hi], axis=1)


def _const_spec(shape, single=False):
    nd = len(shape)
    mode = pl.Buffered(1) if single else None
    return pl.BlockSpec(shape, lambda *_: (0,) * nd, pipeline_mode=mode)


def _params(sem):
    return pltpu.CompilerParams(dimension_semantics=sem, vmem_limit_bytes=VMEM_LIMIT_BYTES)


def _router_logits_t(x_new, mn, wr3):
    hm = _rms(x_new, mn)
    h_hi = hm.astype(bf16)
    h_lo = (hm - h_hi.astype(f32)).astype(bf16)
    both = _dot(jnp.concatenate([h_hi, h_lo], axis=1), wr3)
    lg = both[:, :LANES] + both[:, LANES:]
    return lg.T[:ROUTER_ROWS, :]


def _sgu_kernel(x_ref, an_ref, win_ref, bin_ref, vn_ref, ws_ref, bst_ref, wout_ref, mn_ref,
                wr_ref, x1_ref, lg_ref, u_scr, v_scr, gated_scr):
    x = x_ref[...]
    h = _rms(x, an_ref[...]).astype(bf16)
    v = _gelu_tanh(_dot(h, win_ref[:, SGU_WIDTH:]) + bin_ref[:, SGU_WIDTH:])
    v_scr[...] = _rms(v, vn_ref[...]).astype(bf16)
    u_scr[...] = _gelu_tanh(_dot(h, win_ref[:, :SGU_WIDTH]) + bin_ref[:, :SGU_WIDTH])

    t_chunk = lax.broadcasted_iota(jnp.int32, (SGU_BLOCK, SGU_BLOCK), 0) >> CHUNK_SHIFT
    s_chunk = lax.broadcasted_iota(jnp.int32, (SGU_BLOCK, SGU_BLOCK), 1) >> CHUNK_SHIFT
    causal = t_chunk >= s_chunk
    wsm = [jnp.where(causal, ws_ref[g], 0.0).astype(bf16) for g in range(SGU_GROUPS)]
    for r0 in range(0, SGU_TILE, EPILOGUE_ROWS):
        for g in range(SGU_GROUPS):
            cols = slice(g * SGU_GROUP_DIM, (g + 1) * SGU_GROUP_DIM)
            for sb in range(r0 // SGU_BLOCK, (r0 + EPILOGUE_ROWS) // SGU_BLOCK):
                rows = slice(sb * SGU_BLOCK, (sb + 1) * SGU_BLOCK)
                mixed = _dot(wsm[g], v_scr[rows, cols]) + bst_ref[:, g:g + 1]
                gated_scr[rows, cols] = (u_scr[rows, cols] * mixed).astype(bf16)
        half = slice(r0, r0 + EPILOGUE_ROWS)
        x1 = x[half, :] + _dot(gated_scr[half, :], wout_ref[...])
        x1_ref[half, :] = x1
        lg_ref[:, half] = _router_logits_t(x1, mn_ref[...], wr_ref[...])


def _sgu(x2d, a_norm, w_in, b_in, v_norm, w_s, b_s_t, w_out, m_norm, w_router):
    T = x2d.shape[0]
    return pl.pallas_call(
        _sgu_kernel,
        grid=(T // SGU_TILE,),
        in_specs=[
            pl.BlockSpec((SGU_TILE, D_MODEL), lambda i: (i, 0)),
            _const_spec((1, D_MODEL)),
            _const_spec((D_MODEL, 2 * SGU_WIDTH), single=True),
            _const_spec((1, 2 * SGU_WIDTH)),
            _const_spec((1, SGU_WIDTH)),
            _const_spec((SGU_GROUPS, SGU_BLOCK, SGU_BLOCK)),
            _const_spec((SGU_BLOCK, SGU_GROUPS)),
            _const_spec((SGU_WIDTH, D_MODEL), single=True),
            _const_spec((1, D_MODEL)),
            _const_spec((2 * D_MODEL, 2 * LANES)),
        ],
        out_specs=[
            pl.BlockSpec((SGU_TILE, D_MODEL), lambda i: (i, 0)),
            pl.BlockSpec((ROUTER_ROWS, SGU_TILE), lambda i: (0, i)),
        ],
        out_shape=[
            jax.ShapeDtypeStruct((T, D_MODEL), f32),
            jax.ShapeDtypeStruct((ROUTER_ROWS, T), f32),
        ],
        scratch_shapes=[
            pltpu.VMEM((SGU_TILE, SGU_WIDTH), f32),
            pltpu.VMEM((SGU_TILE, SGU_WIDTH), bf16),
            pltpu.VMEM((SGU_TILE, SGU_WIDTH), bf16),
        ],
        compiler_params=_params(("parallel",)),
        name="sgu",
    )(x2d, a_norm, w_in, b_in, v_norm, w_s, b_s_t, w_out, m_norm, w_router)


def _route_kernel(lg_ref, be_ref, bg_ref, oi_ref, ow_ref, cnt_ref, carry_scr, before_scr,
                  w_scr):
    step = pl.program_id(0)
    tt = lg_ref.shape[1]

    @pl.when(step == 0)
    def _():
        carry_scr[...] = jnp.zeros_like(carry_scr)
        before_scr[...] = (lax.broadcasted_iota(jnp.int32, (tt, tt), 0)
                           < lax.broadcasted_iota(jnp.int32, (tt, tt), 1)).astype(bf16)

    lg = lg_ref[...]
    e_l = lg[0:N_EXPERTS, :] + be_ref[...]
    g_l = lg[N_EXPERTS:N_EXPERTS + 8, :] + bg_ref[...]
    g_row = lax.broadcasted_iota(jnp.int32, g_l.shape, 0).astype(f32)
    g_l = jnp.where(g_row < N_GROUPS, g_l, -jnp.inf)
    g_max = jnp.max(g_l, axis=0, keepdims=True)
    g_sel = jnp.min(jnp.where(g_l == g_max, g_row, 8.0), axis=0, keepdims=True)
    g_den = jnp.sum(jnp.exp(g_l - g_max), axis=0, keepdims=True)
    g_w = 1.0 / g_den

    e_row_i = lax.broadcasted_iota(jnp.int32, e_l.shape, 0)
    e_row = e_row_i.astype(f32)
    e_grp = (e_row_i >> GROUP_SHIFT).astype(f32)
    e_in = jnp.where(e_grp == g_sel, e_l, -jnp.inf)
    m1 = jnp.max(e_in, axis=0, keepdims=True)
    i1 = jnp.min(jnp.where(e_in == m1, e_row, float(N_EXPERTS)), axis=0, keepdims=True)
    e_in2 = jnp.where(e_row == i1, -jnp.inf, e_in)
    m2 = jnp.max(e_in2, axis=0, keepdims=True)
    i2 = jnp.min(jnp.where(e_in2 == m2, e_row, float(N_EXPERTS)), axis=0, keepdims=True)
    t = jnp.exp(m2 - m1)
    p1 = 1.0 / (1.0 + t)
    p2 = t / (1.0 + t)

    hit1 = e_row == i1
    hit2 = e_row == i2
    member = jnp.logical_or(hit1, hit2)
    prefix = _dot(member.astype(bf16), before_scr[...])
    rank_full = prefix + carry_scr[...]
    rank1 = jnp.sum(jnp.where(hit1, rank_full, 0.0), axis=0, keepdims=True)
    rank2 = jnp.sum(jnp.where(hit2, rank_full, 0.0), axis=0, keepdims=True)
    carry_new = carry_scr[...] + jnp.sum(member.astype(f32), axis=1, keepdims=True)
    carry_scr[...] = carry_new

    oi_ref[...] = jnp.zeros(oi_ref.shape, jnp.int32)
    oi_ref[0:1, :] = i1.astype(jnp.int32)
    oi_ref[1:2, :] = i2.astype(jnp.int32)
    oi_ref[2:3, :] = rank1.astype(jnp.int32)
    oi_ref[3:4, :] = rank2.astype(jnp.int32)
    w_scr[...] = jnp.zeros(w_scr.shape, f32)
    w_scr[0:1, :] = g_w * p1
    w_scr[1:2, :] = g_w * p2
    ow_ref[...] = w_scr[...].T
    cnt_ref[...] = jnp.broadcast_to(carry_new, cnt_ref.shape)


def _route(logits_t, b_expert, b_group):
    T = logits_t.shape[1]
    return pl.pallas_call(
        _route_kernel,
        grid=(T // ROUTE_TILE,),
        in_specs=[
            pl.BlockSpec((ROUTER_ROWS, ROUTE_TILE), lambda i: (0, i)),
            _const_spec((N_EXPERTS, 1)),
            _const_spec((8, 1)),
        ],
        out_specs=[
            pl.BlockSpec((8, ROUTE_TILE), lambda i: (0, i)),
            pl.BlockSpec((ROUTE_TILE, 8), lambda i: (i, 0)),
            _const_spec((N_EXPERTS, LANES)),
        ],
        out_shape=[
            jax.ShapeDtypeStruct((8, T), jnp.int32),
            jax.ShapeDtypeStruct((T, 8), f32),
            jax.ShapeDtypeStruct((N_EXPERTS, LANES), f32),
        ],
        scratch_shapes=[pltpu.VMEM((N_EXPERTS, 1), f32),
                        pltpu.VMEM((ROUTE_TILE, ROUTE_TILE), bf16),
                        pltpu.VMEM((8, ROUTE_TILE), f32)],
        compiler_params=_params(("arbitrary",)),
        name="route",
    )(logits_t, b_expert, b_group)


def _row_copy(src, s, dst, d, sem):
    return pltpu.make_async_copy(src.at[s], dst.at[d], sem)


def _wait_rows(buf, sem):
    pltpu.make_async_copy(buf, buf, sem).wait()


def _dispatch_kernel(n_blocks, pend_ref, padded_ref, nu_ref, dest_ref, x_ref, mn_ref, xs_hbm,
                     zero_scr, x3_scr, zsem, sem):
    @pl.when(pl.program_id(0) == 0)
    def _():
        zero_scr[...] = jnp.zeros_like(zero_scr)

        def fill(row0):
            return pltpu.make_async_copy(zero_scr, xs_hbm.at[pl.ds(row0, EXPERT_ROWS)], zsem)

        for e in range(N_EXPERTS):
            @pl.when(padded_ref[e] > 0)
            def _():
                fill(pend_ref[e] - EXPERT_ROWS).start()

            @pl.when(nu_ref[0] + e < n_blocks)
            def _():
                fill((nu_ref[0] + e) * EXPERT_ROWS).start()

        for e in range(N_EXPERTS):
            @pl.when(padded_ref[e] > 0)
            def _():
                fill(0).wait()

            @pl.when(nu_ref[0] + e < n_blocks)
            def _():
                fill(0).wait()

    i = pl.program_id(0)
    n_tiles = pl.num_programs(0) - 1

    def staged(rows):
        return _pack_rows(_rms(x_ref[rows, :], mn_ref[...]))

    @pl.when(i == 0)
    def _():
        x3_scr[0] = staged(slice(None))

    @pl.when(i > 0)
    def _():
        cur = (i - 1) % 3
        nxt = i % 3

        def chunk(c, carry):
            r0 = pl.multiple_of(c * MOVE_CHUNK, MOVE_CHUNK)
            for j in range(MOVE_CHUNK):
                for k in range(TOP_K):
                    _row_copy(x3_scr.at[cur], r0 + j, xs_hbm,
                              dest_ref[0, k * MOVE_TILE + r0 + j], sem.at[cur]).start(priority=k)
            x3_scr[nxt, pl.ds(r0, MOVE_CHUNK)] = staged(pl.ds(r0, MOVE_CHUNK))
            return carry

        lax.fori_loop(0, MOVE_TILE // MOVE_CHUNK, chunk, 0)

        @pl.when(i > 1)
        def _():
            for k in range(TOP_K):
                _wait_rows(x3_scr.at[(i - 2) % 3], sem.at[(i - 2) % 3])

        @pl.when(i == n_tiles)
        def _():
            for k in range(TOP_K):
                _wait_rows(x3_scr.at[cur], sem.at[cur])


def _dispatch(pend, padded, n_used, dest_tiles, x2d, m_norm, n_rows):
    T = x2d.shape[0]
    n_blocks = n_rows // EXPERT_ROWS
    n_tiles = T // MOVE_TILE
    grid_spec = pltpu.PrefetchScalarGridSpec(
        num_scalar_prefetch=3,
        grid=(n_tiles + 1,),
        in_specs=[
            pl.BlockSpec((None, 1, TOP_K * MOVE_TILE),
                         lambda i, *_: (jnp.maximum(i - 1, 0), 0, 0), memory_space=pltpu.SMEM),
            pl.BlockSpec((MOVE_TILE, D_MODEL), lambda i, *_: (jnp.minimum(i, n_tiles - 1), 0)),
            pl.BlockSpec((1, D_MODEL), lambda i, *_: (0, 0)),
        ],
        out_specs=pl.BlockSpec(memory_space=pl.ANY),
        scratch_shapes=[
            pltpu.VMEM((EXPERT_ROWS, ROW_SUBLANES, LANES), u32),
            pltpu.VMEM((3, MOVE_TILE, ROW_SUBLANES, LANES), u32),
            pltpu.SemaphoreType.DMA(()),
            pltpu.SemaphoreType.DMA((3,)),
        ],
    )
    return pl.pallas_call(
        functools.partial(_dispatch_kernel, n_blocks),
        grid_spec=grid_spec,
        out_shape=jax.ShapeDtypeStruct((n_rows, ROW_SUBLANES, LANES), u32),
        compiler_params=_params(("arbitrary",)),
        name="dispatch",
    )(pend, padded, n_used, dest_tiles, x2d, m_norm)


def _expert_kernel(n_blocks, ps_ref, nb_ref, nu_ref, xs_hbm, w1_ref, w3_ref, w2_ref,
                   ys_hbm, xbuf, ybuf, h_scr, w1_scr, w3_scr, w2_scr, xsem, ysem):
    e = pl.program_id(0)
    nblk = nb_ref[e]
    g0 = ps_ref[e]
    n_used = nu_ref[0]

    def rows(g):
        return pl.ds(pl.multiple_of(g * EXPERT_ROWS, EXPERT_ROWS), EXPERT_ROWS)

    def x_copy(g):
        return pltpu.make_async_copy(xs_hbm.at[rows(g)], xbuf.at[g % 2], xsem.at[g % 2])

    def y_copy(g):
        return pltpu.make_async_copy(ybuf.at[g % 2], ys_hbm.at[rows(g)], ysem.at[g % 2])

    def normed(slot):
        return _unpack_rows(xbuf[slot]).astype(bf16)

    @pl.when(e == 0)
    def _():
        x_copy(0).start(priority=1)
        x_copy(1).start(priority=1)
        x_copy(0).wait()
        h_scr[...] = normed(0)

    @pl.when(nblk > 0)
    def _():
        w1_scr[...] = w1_ref[...].astype(bf16)
        w3_scr[...] = w3_ref[...].astype(bf16)
        w2_scr[...] = w2_ref[...].astype(bf16)

        def body(g, h):
            @pl.when(g + 2 < n_used)
            def _():
                x_copy(g + 2).start(priority=1)

            @pl.when(g + 1 < n_used)
            def _():
                x_copy(g + 1).wait()

            @pl.when(g >= 2)
            def _():
                y_copy(g - 2).wait()

            h_next = normed((g + 1) % 2)
            y = None
            for c0 in range(0, D_EXPERT, D_EXPERT // 2):
                cols = slice(c0, c0 + D_EXPERT // 2)
                a = _dot(h, w1_scr[:, cols])
                c = _dot(h, w3_scr[:, cols])
                hid = ((a * jax.nn.sigmoid(a)) * c).astype(bf16)
                part = _dot(hid, w2_scr[cols, :])
                y = part if y is None else y + part
            ybuf[g % 2] = _pack_rows(y)
            y_copy(g).start(priority=1)
            return h_next

        h_scr[...] = lax.fori_loop(g0, g0 + nblk, body, h_scr[...])

    @pl.when(e == pl.num_programs(0) - 1)
    def _():
        y_copy(n_used - 2).wait()
        y_copy(n_used - 1).wait()
        ybuf[0] = jnp.zeros(ybuf.shape[1:], u32)

        def fill(b):
            r = pl.ds(pl.multiple_of(b * EXPERT_ROWS, EXPERT_ROWS), EXPERT_ROWS)
            return pltpu.make_async_copy(ybuf.at[0], ys_hbm.at[r], ysem.at[0])

        def start(b, c):
            fill(b).start()
            return c

        def wait(b, c):
            fill(b).wait()
            return c

        lax.fori_loop(nu_ref[0], n_blocks, start, 0)
        lax.fori_loop(nu_ref[0], n_blocks, wait, 0)


def _experts(layer, pstart, seg_blocks, n_used, xs, w1, w3, w2):
    n_rows = xs.shape[0]
    n_blocks = n_rows // EXPERT_ROWS

    def w_map(e, *_):
        return (layer, e, 0, 0)

    block_buf = pltpu.VMEM((2, EXPERT_ROWS, ROW_SUBLANES, LANES), u32)
    grid_spec = pltpu.PrefetchScalarGridSpec(
        num_scalar_prefetch=3,
        grid=(N_EXPERTS,),
        in_specs=[
            pl.BlockSpec(memory_space=pl.ANY),
            pl.BlockSpec((None, None, D_MODEL, D_EXPERT), w_map),
            pl.BlockSpec((None, None, D_MODEL, D_EXPERT), w_map),
            pl.BlockSpec((None, None, D_EXPERT, D_MODEL), w_map),
        ],
        out_specs=pl.BlockSpec(memory_space=pl.ANY),
        scratch_shapes=[
            block_buf,
            block_buf,
            pltpu.VMEM((EXPERT_ROWS, D_MODEL), bf16),
            pltpu.VMEM((D_MODEL, D_EXPERT), bf16),
            pltpu.VMEM((D_MODEL, D_EXPERT), bf16),
            pltpu.VMEM((D_EXPERT, D_MODEL), bf16),
            pltpu.SemaphoreType.DMA((2,)),
            pltpu.SemaphoreType.DMA((2,)),
        ],
    )
    return pl.pallas_call(
        functools.partial(_expert_kernel, n_blocks),
        grid_spec=grid_spec,
        out_shape=jax.ShapeDtypeStruct((n_rows, ROW_SUBLANES, LANES), u32),
        compiler_params=_params(("arbitrary",)),
        name="experts",
    )(pstart, seg_blocks, n_used, xs, w1, w3, w2)


def _combine_kernel(dest_ref, next_ref, x_ref, w_ref, ys_hbm, out_ref, buf, sem):
    i = pl.program_id(0)
    slot = i % 2

    def gather(idx_ref, s):
        def start(j, c):
            for k in range(TOP_K):
                _row_copy(ys_hbm, idx_ref[0, k * MOVE_TILE + j], buf.at[s, k], j,
                          sem.at[s]).start(priority=k)
            return c

        lax.fori_loop(0, MOVE_TILE, start, 0, unroll=ISSUE_UNROLL)

    @pl.when(i == 0)
    def _():
        gather(dest_ref, slot)

    @pl.when(i + 1 < pl.num_programs(0))
    def _():
        gather(next_ref, 1 - slot)

    for k in range(TOP_K):
        _wait_rows(buf.at[slot, k], sem.at[slot])
    w = w_ref[...]
    y0 = _unpack_rows(buf[slot, 0])
    y1 = _unpack_rows(buf[slot, 1])
    out_ref[...] = x_ref[...] + (w[:, 0:1] * y0 + w[:, 1:2] * y1)


def _combine(dest_tiles, x2d, w_cols, ys):
    T = x2d.shape[0]
    n_tiles = T // MOVE_TILE
    return pl.pallas_call(
        _combine_kernel,
        grid=(n_tiles,),
        in_specs=[
            pl.BlockSpec((None, 1, TOP_K * MOVE_TILE), lambda i: (i, 0, 0),
                         memory_space=pltpu.SMEM),
            pl.BlockSpec((None, 1, TOP_K * MOVE_TILE),
                         lambda i: (jnp.minimum(i + 1, n_tiles - 1), 0, 0),
                         memory_space=pltpu.SMEM),
            pl.BlockSpec((MOVE_TILE, D_MODEL), lambda i: (i, 0)),
            pl.BlockSpec((MOVE_TILE, 8), lambda i: (i, 0)),
            pl.BlockSpec(memory_space=pl.ANY),
        ],
        out_specs=pl.BlockSpec((MOVE_TILE, D_MODEL), lambda i: (i, 0)),
        out_shape=jax.ShapeDtypeStruct((T, D_MODEL), f32),
        scratch_shapes=[
            pltpu.VMEM((2, TOP_K, MOVE_TILE, ROW_SUBLANES, LANES), u32),
            pltpu.SemaphoreType.DMA((2,)),
        ],
        compiler_params=_params(("arbitrary",)),
        name="combine",
    )(dest_tiles, dest_tiles, x2d, w_cols, ys)


def _moe(layer, x2d, logits_t, m_norm, b_group, b_expert, w1, w3, w2):
    T = x2d.shape[0]
    n_blocks = (T * TOP_K) // EXPERT_ROWS + N_EXPERTS
    n_rows = n_blocks * EXPERT_ROWS

    be = b_expert.reshape(N_EXPERTS, 1)
    bg = jnp.concatenate([b_group, jnp.zeros((8 - N_GROUPS,), f32)]).reshape(8, 1)
    oi, w_cols, cnt = _route(logits_t, be, bg)

    counts = cnt[:, 0].astype(jnp.int32)
    padded = (counts + EXPERT_ROWS - 1) // EXPERT_ROWS * EXPERT_ROWS
    pend = jnp.cumsum(padded)
    pstart = pend - padded
    n_used = (pend[-1] // EXPERT_ROWS).astype(jnp.int32).reshape(1)
    e_ids = jnp.arange(N_EXPERTS, dtype=jnp.int32)[:, None, None]
    seg_start = jnp.sum(jnp.where(oi[None, 0:2] == e_ids, pstart[:, None, None], 0), axis=0)
    dest = seg_start + oi[2:4]
    dest_tiles = dest.reshape(TOP_K, T // MOVE_TILE, MOVE_TILE).transpose(1, 0, 2).reshape(
        T // MOVE_TILE, 1, TOP_K * MOVE_TILE)

    xs = _dispatch(pend.astype(jnp.int32), padded, n_used, dest_tiles, x2d, m_norm, n_rows)
    ys = _experts(layer, (pstart // EXPERT_ROWS).astype(jnp.int32), padded // EXPERT_ROWS,
                  n_used, xs, w1, w3, w2)
    return _combine(dest_tiles, x2d, w_cols, ys)


def _log_sigmoid(x):
    return jnp.minimum(x, 0.0) - jnp.log1p(jnp.exp(-jnp.abs(x)))


def _proj_kernel(tiles_per_seq, x_ref, kvn_ref, bn_ref, kvw_ref, bf_ref, qgw_ref, gq_ref, gk_ref,
                 q_ref, kt_ref, v_ref, sg_ref, carry_scr, wkt_ref, wv_ref, wf_ref, wq_ref, wg_ref):
    i = pl.program_id(0)

    @pl.when(i == 0)
    def _():
        wkt_ref[...] = kvw_ref[:D_MODEL, :].astype(bf16)
        wv_ref[...] = kvw_ref[D_MODEL:2 * D_MODEL, :].T.astype(bf16)
        wf_ref[...] = jnp.concatenate(
            [kvw_ref[2 * D_MODEL:, :].T, jnp.zeros((D_MODEL, LANES - N_HEADS), f32)],
            axis=1).astype(bf16)
        wg_ref[...] = qgw_ref[:, D_MODEL:].astype(bf16)
        zeros = jnp.zeros((D_MODEL, HEAD_PAD - HEAD_DIM), f32)
        for h in range(N_HEADS):
            wq_ref[:, h * HEAD_PAD:(h + 1) * HEAD_PAD] = jnp.concatenate(
                [qgw_ref[:, h * HEAD_DIM:(h + 1) * HEAD_DIM], zeros], axis=1).astype(bf16)

    x = x_ref[...]
    hkv = _rms(x, kvn_ref[...]).astype(bf16)
    hq = _rms(x, bn_ref[...]).astype(bf16)

    v_ref[...] = _dot(hkv, wv_ref[...]).astype(bf16)
    sg_ref[...] = jax.nn.sigmoid(_dot(hq, wg_ref[...])).astype(bf16)

    logf = _log_sigmoid(_dot(hkv, wf_ref[...]) + bf_ref[...])
    tm = x.shape[0]
    incl = (lax.broadcasted_iota(jnp.int32, (tm, tm), 0)
            >= lax.broadcasted_iota(jnp.int32, (tm, tm), 1)).astype(bf16)

    @pl.when(i % tiles_per_seq == 0)
    def _():
        carry_scr[...] = jnp.zeros_like(carry_scr)

    parts = _dot(incl, jnp.concatenate(_split3(logf), axis=1).astype(bf16))
    cum = (parts[:, :LANES] + parts[:, LANES:2 * LANES] + parts[:, 2 * LANES:]
           + carry_scr[...])
    carry_scr[...] = cum[tm - 1:tm, :]
    cum = cum * LOG2E
    cum_t = cum.T

    lane = lax.broadcasted_iota(jnp.int32, (tm, HEAD_PAD), 1)
    sub = lax.broadcasted_iota(jnp.int32, (HEAD_DIM, tm), 0)
    q_raw = _dot(hq, wq_ref[...])
    k_raw_t = _dot_nt(wkt_ref[...], hkv)
    scale = HEAD_DIM ** -0.5 * LOG2E

    for h in range(N_HEADS):
        sl = slice(h * HEAD_PAD, (h + 1) * HEAD_PAD)
        qb = q_raw[:, sl]
        q_ms = jnp.sum(qb * qb, axis=-1, keepdims=True) * (1.0 / HEAD_DIM)
        qn = qb * lax.rsqrt(q_ms + EPS) * gq_ref[...] * scale
        c_hi, c_mid, c_lo = _split3(cum[:, h:h + 1])
        qa = jnp.where(lane < HEAD_DIM, qn,
             jnp.where(lane < AUX_CUM0, 1.0,
             jnp.where(lane == AUX_CUM0, c_hi,
             jnp.where(lane == AUX_CUM0 + 1, c_mid,
             jnp.where(lane == AUX_CUM0 + 2, c_lo, 0.0)))))
        q_ref[:, sl] = qa.astype(bf16)

        kb = k_raw_t[h * HEAD_DIM:(h + 1) * HEAD_DIM, :]
        k_ms = jnp.sum(kb * kb, axis=0, keepdims=True) * (1.0 / HEAD_DIM)
        kn = kb * lax.rsqrt(k_ms + EPS) * gk_ref[...]
        t_hi, t_mid, t_lo = _split3(cum_t[h:h + 1, :])
        aux = jnp.where(sub == 0, -t_hi,
              jnp.where(sub == 1, -t_mid,
              jnp.where(sub == 2, -t_lo,
              jnp.where(sub < 6, 1.0, 0.0))))
        sl = slice(h * HEAD_PAD, (h + 1) * HEAD_PAD)
        ka = jnp.concatenate([kn, aux], axis=0).astype(bf16)
        for t in range(tm // ATT_K):
            kt_ref[t, sl, :] = ka[:, t * ATT_K:(t + 1) * ATT_K]


def _proj(x2d, seq, kv_norm, b_norm, kv_w, bfv, w_qg, gq, gk):
    T = x2d.shape[0]
    n_tiles = T // ROW_TILE
    qw = N_HEADS * HEAD_PAD
    return pl.pallas_call(
        functools.partial(_proj_kernel, seq // ROW_TILE),
        grid=(n_tiles,),
        in_specs=[
            pl.BlockSpec((ROW_TILE, D_MODEL), lambda i: (i, 0)),
            _const_spec((1, D_MODEL)),
            _const_spec((1, D_MODEL)),
            _const_spec(kv_w.shape, single=True),
            _const_spec((1, LANES)),
            _const_spec(w_qg.shape, single=True),
            _const_spec((1, HEAD_PAD)),
            _const_spec((HEAD_DIM, 1)),
        ],
        out_specs=[
            pl.BlockSpec((ROW_TILE, qw), lambda i: (i, 0)),
            pl.BlockSpec((ROW_TILE // ATT_K, qw, ATT_K), lambda i: (i, 0, 0)),
            pl.BlockSpec((ROW_TILE, D_MODEL), lambda i: (i, 0)),
            pl.BlockSpec((ROW_TILE, D_MODEL), lambda i: (i, 0)),
        ],
        out_shape=[
            jax.ShapeDtypeStruct((T, qw), bf16),
            jax.ShapeDtypeStruct((T // ATT_K, qw, ATT_K), bf16),
            jax.ShapeDtypeStruct((T, D_MODEL), bf16),
            jax.ShapeDtypeStruct((T, D_MODEL), bf16),
        ],
        scratch_shapes=[
            pltpu.VMEM((1, LANES), f32),
            pltpu.VMEM((D_MODEL, D_MODEL), bf16),
            pltpu.VMEM((D_MODEL, D_MODEL), bf16),
            pltpu.VMEM((D_MODEL, LANES), bf16),
            pltpu.VMEM((D_MODEL, qw), bf16),
            pltpu.VMEM((D_MODEL, D_MODEL), bf16),
        ],
        compiler_params=_params(("arbitrary",)),
        name="proj",
    )(x2d, kv_norm, b_norm, kv_w, bfv, w_qg, gq, gk)


def _attn_kernel(q_ref, kt_ref, v_ref, o_ref, s_scr, m_scr, l_scr, acc_scr):
    qi = pl.program_id(2)
    tiles = ATT_Q // ATT_K
    groups = ATT_K // LANES
    heads = [slice(hh * HEAD_PAD, (hh + 1) * HEAD_PAD) for hh in range(2)]
    m_scr[...] = jnp.full(m_scr.shape, -jnp.inf, f32)

    def lane_max(mx, s):
        for g in range(groups):
            mx = jnp.maximum(mx, s[:, g * LANES:(g + 1) * LANES])
        return mx

    def probs(s, mb):
        ps = [jnp.exp2(s[:, g * LANES:(g + 1) * LANES] - mb) for g in range(groups)]
        return ps, functools.reduce(lambda a, b: a + b, ps)

    def body_a(trip, c):
        for t in range(tiles):
            kt_idx = trip * tiles + t
            for hh in range(2):
                s = _dot(q_ref[:, heads[hh]], kt_ref[kt_idx, heads[hh], :])
                s_scr[hh, kt_idx] = s
                m_scr[hh] = lane_max(m_scr[hh], s)
        return c

    lax.fori_loop(0, qi, body_a, 0)
    for t in range(tiles):
        kt_idx = qi * tiles + t
        rows = ATT_Q - t * ATT_K
        visible = (lax.broadcasted_iota(jnp.int32, (rows, ATT_K), 1)
                   <= lax.broadcasted_iota(jnp.int32, (rows, ATT_K), 0))
        for hh in range(2):
            s = _dot(q_ref[t * ATT_K:, heads[hh]], kt_ref[kt_idx, heads[hh], :])
            s = jnp.where(visible, s, -jnp.inf)
            s_scr[hh, kt_idx, t * ATT_K:, :] = s
            m_scr[hh, t * ATT_K:, :] = lane_max(m_scr[hh, t * ATT_K:, :], s)

    for hh in range(2):
        row_max = jnp.max(m_scr[hh], axis=-1, keepdims=True)
        m_scr[hh] = jnp.broadcast_to(row_max, (ATT_Q, LANES))
    l_scr[...] = jnp.zeros(l_scr.shape, f32)
    acc_scr[...] = jnp.zeros(acc_scr.shape, f32)

    def body_b(trip, c):
        for hh in range(2):
            mb = m_scr[hh]
            lsum, acc = l_scr[hh], acc_scr[hh]
            for t0 in range(0, tiles, PV_TILES):
                kt_idx = trip * tiles + t0
                row0 = pl.multiple_of(kt_idx * ATT_K, PV_TILES * ATT_K)
                vb = v_ref[pl.ds(row0, PV_TILES * ATT_K), :]
                parts = []
                for t in range(PV_TILES):
                    ps, psum = probs(s_scr[hh, kt_idx + t], mb)
                    parts += ps
                    lsum = lsum + psum
                acc = acc + _dot(jnp.concatenate(parts, axis=1).astype(bf16), vb)
            l_scr[hh] = lsum
            acc_scr[hh] = acc
        return c

    lax.fori_loop(0, qi, body_b, 0)
    for t in range(tiles):
        kt_idx = qi * tiles + t
        row0 = pl.multiple_of(kt_idx * ATT_K, ATT_K)
        vb = v_ref[pl.ds(row0, ATT_K), :]
        for hh in range(2):
            ps, psum = probs(s_scr[hh, kt_idx, t * ATT_K:, :], m_scr[hh, t * ATT_K:, :])
            l_scr[hh, t * ATT_K:, :] += psum
            acc_scr[hh, t * ATT_K:, :] += _dot(jnp.concatenate(ps, axis=1).astype(bf16), vb)

    lane = lax.broadcasted_iota(jnp.int32, (ATT_Q, 2 * HEAD_DIM), 1)
    o0 = acc_scr[0] / jnp.sum(l_scr[0], axis=-1, keepdims=True)
    o1 = acc_scr[1] / jnp.sum(l_scr[1], axis=-1, keepdims=True)
    o_ref[...] = jnp.where(lane < HEAD_DIM, o0, o1).astype(bf16)


def _attention(q, kt, v, batch, seq):
    T = q.shape[0]
    nq = seq // ATT_Q
    nk = seq // ATT_K
    pairs = N_HEADS // 2
    return pl.pallas_call(
        _attn_kernel,
        grid=(batch, pairs, nq),
        in_specs=[
            pl.BlockSpec((ATT_Q, 2 * HEAD_PAD), lambda b, p, i: (b * nq + i, p)),
            pl.BlockSpec((nk, 2 * HEAD_PAD, ATT_K), lambda b, p, i: (b, p, 0)),
            pl.BlockSpec((seq, 2 * HEAD_DIM), lambda b, p, i: (b, p)),
        ],
        out_specs=pl.BlockSpec((ATT_Q, 2 * HEAD_DIM), lambda b, p, i: (b * nq + i, p)),
        out_shape=jax.ShapeDtypeStruct((T, D_MODEL), bf16),
        scratch_shapes=[
            pltpu.VMEM((2, nk, ATT_Q, ATT_K), f32),
            pltpu.VMEM((2, ATT_Q, LANES), f32),
            pltpu.VMEM((2, ATT_Q, LANES), f32),
            pltpu.VMEM((2, ATT_Q, 2 * HEAD_DIM), f32),
        ],
        compiler_params=_params(("parallel", "parallel", "parallel")),
        name="attn",
    )(q, kt, v)


def _attn_out_kernel(x_ref, o_ref, sg_ref, wo_ref, mn_ref, wr_ref, x3_ref, lg_ref):
    for r0 in range(0, SGU_TILE, EPILOGUE_ROWS):
        rows = slice(r0, r0 + EPILOGUE_ROWS)
        gated = (o_ref[rows, :].astype(f32) * sg_ref[rows, :].astype(f32)).astype(bf16)
        x3 = x_ref[rows, :] + _dot(gated, wo_ref[...])
        x3_ref[rows, :] = x3
        lg_ref[:, rows] = _router_logits_t(x3, mn_ref[...], wr_ref[...])


def _attn_out(x2d, o, sg, wo, m_norm, w_router):
    T = x2d.shape[0]
    return pl.pallas_call(
        _attn_out_kernel,
        grid=(T // SGU_TILE,),
        in_specs=[
            pl.BlockSpec((SGU_TILE, D_MODEL), lambda i: (i, 0)),
            pl.BlockSpec((SGU_TILE, D_MODEL), lambda i: (i, 0)),
            pl.BlockSpec((SGU_TILE, D_MODEL), lambda i: (i, 0)),
            _const_spec((D_MODEL, D_MODEL)),
            _const_spec((1, D_MODEL)),
            _const_spec((2 * D_MODEL, 2 * LANES)),
        ],
        out_specs=[
            pl.BlockSpec((SGU_TILE, D_MODEL), lambda i: (i, 0)),
            pl.BlockSpec((ROUTER_ROWS, SGU_TILE), lambda i: (0, i)),
        ],
        out_shape=[
            jax.ShapeDtypeStruct((T, D_MODEL), f32),
            jax.ShapeDtypeStruct((ROUTER_ROWS, T), f32),
        ],
        compiler_params=_params(("parallel",)),
        name="attn_out",
    )(x2d, o, sg, wo, m_norm, w_router)


def _router_weight(w_group, w_expert):
    pad = jnp.zeros((D_MODEL, LANES - N_EXPERTS - N_GROUPS), f32)
    w = jnp.concatenate([w_expert, w_group, pad], axis=1)
    w_hi = w.astype(bf16)
    w_lo = (w - w_hi.astype(f32)).astype(bf16)
    top = jnp.concatenate([w_hi, w_lo], axis=1)
    bottom = jnp.concatenate([w_hi, jnp.zeros_like(w_hi)], axis=1)
    return jnp.concatenate([top, bottom], axis=0)


def kernel(x, a_norm, a_w_in, a_b_in, a_v_norm, a_w_s, a_b_s, a_w_out, kv_norm, kv_w, kv_b_f,
           k_norm, b_norm, b_w_qg, q_norm, b_w_out, m_norm, m_w_group, m_b_group, m_w_expert,
           m_b_expert, m_w1, m_w3, m_w2):
    batch, seq, _ = x.shape
    T = batch * seq
    x2d = x.reshape(T, D_MODEL)

    x1, lg0 = _sgu(
        x2d, a_norm[0].reshape(1, -1), a_w_in[0].astype(bf16), a_b_in[0].reshape(1, -1),
        a_v_norm[0].reshape(1, -1), a_w_s[0], a_b_s[0].T, a_w_out[0].astype(bf16),
        m_norm[0].reshape(1, -1), _router_weight(m_w_group[0], m_w_expert[0]))
    x2 = _moe(0, x1, lg0, m_norm[0].reshape(1, -1), m_b_group[0], m_b_expert[0],
              m_w1, m_w3, m_w2)

    bfv = jnp.pad(kv_b_f, (0, LANES - N_HEADS)).reshape(1, LANES)
    gq = jnp.pad(q_norm[0], (0, HEAD_PAD - HEAD_DIM)).reshape(1, HEAD_PAD)
    gk = k_norm.reshape(HEAD_DIM, 1)
    q, kt, v, sg = _proj(x2, seq, kv_norm.reshape(1, -1), b_norm[0].reshape(1, -1),
                         kv_w.T, bfv, b_w_qg[0], gq, gk)
    o = _attention(q, kt, v, batch, seq)
    x3, lg1 = _attn_out(x2, o, sg, b_w_out[0].astype(bf16), m_norm[1].reshape(1, -1),
                        _router_weight(m_w_group[1], m_w_expert[1]))
    x4 = _moe(1, x3, lg1, m_norm[1].reshape(1, -1), m_b_group[1], m_b_expert[1],
              m_w1, m_w3, m_w2)
    return x4.reshape(batch, seq, D_MODEL)
```

```python
import functools
import math

import jax
import jax.numpy as jnp
import numpy as np
from jax import lax
from jax.experimental import pallas as pl
from jax.experimental.pallas import tpu as pltpu

D_MODEL = 1024
EPS = 1e-6
SGU_BLOCK = 128
SGU_CHUNK = 64
CHUNK_SHIFT = SGU_CHUNK.bit_length() - 1
SGU_WIDTH = 2 * D_MODEL
SGU_GROUPS = 8
SGU_GROUP_DIM = SGU_WIDTH // SGU_GROUPS
N_HEADS = 16
HEAD_DIM = D_MODEL // N_HEADS
N_GROUPS = 4
EXPERTS_PER_GROUP = 8
N_EXPERTS = N_GROUPS * EXPERTS_PER_GROUP
GROUP_SHIFT = EXPERTS_PER_GROUP.bit_length() - 1
TOP_K = 2
D_EXPERT = D_MODEL // 2

LANES = 128
ROW_SUBLANES = D_MODEL // 2 // LANES
ISSUE_UNROLL = 8
VMEM_LIMIT_BYTES = 56 * 1024 * 1024

ROW_TILE = 256
SGU_TILE = 512
EPILOGUE_ROWS = 256
ROUTE_TILE = 1024
EXPERT_ROWS = 512
MOVE_TILE = 512
MOVE_CHUNK = 64
ATT_Q = 1024
ATT_K = 256
PV_TILES = 2
HEAD_PAD = 2 * HEAD_DIM
ROUTER_ROWS = 40
AUX_ONE0 = HEAD_DIM
AUX_CUM0 = HEAD_DIM + 3

LOG2E = math.log2(math.e)

bf16 = jnp.bfloat16
f32 = jnp.float32
u32 = jnp.uint32
HI_HALF = np.uint32(0xFFFF0000)


def _dot(a, b, precision=None):
    return jnp.dot(a, b, preferred_element_type=f32, precision=precision)


def _dot_nt(a, b, precision=None):
    return lax.dot_general(a, b, (((1,), (1,)), ((), ())), preferred_element_type=f32,
                           precision=precision)


def _rms(x, g):
    ms = jnp.mean(x * x, axis=-1, keepdims=True)
    return x * lax.rsqrt(ms + EPS) * g


def _gelu_tanh(z):
    c = math.sqrt(2.0 / math.pi)
    return z * (0.5 * (1.0 + jnp.tanh(c * (z + 0.044715 * (z * z * z)))))


def _split3(c):
    hi = c.astype(bf16).astype(f32)
    r = c - hi
    mid = r.astype(bf16).astype(f32)
    lo = r - mid
    return hi, mid, lo


def _pack_rows(v):
    half = D_MODEL // 2
    lo = lax.bitcast_convert_type(v[:, :half].astype(bf16).astype(f32), u32) >> 16
    hi = lax.bitcast_convert_type(v[:, half:].astype(bf16).astype(f32), u32) & HI_HALF
    return (hi | lo).reshape(v.shape[0], ROW_SUBLANES, LANES)


def _unpack_rows(w):
    w = w.reshape(w.shape[0], D_MODEL // 2)
    lo = lax.bitcast_convert_type(w << 16, f32)
    hi = lax.bitcast_convert_type(w & HI_HALF, f32)
    return jnp.concatenate([lo, hi], axis=1)


def _const_spec(shape, single=False):
    nd = len(shape)
    mode = pl.Buffered(1) if single else None
    return pl.BlockSpec(shape, lambda *_: (0,) * nd, pipeline_mode=mode)


def _params(sem):
    return pltpu.CompilerParams(dimension_semantics=sem, vmem_limit_bytes=VMEM_LIMIT_BYTES)


def _router_logits_t(x_new, mn, wr3):
    hm = _rms(x_new, mn)
    h_hi = hm.astype(bf16)
    h_lo = (hm - h_hi.astype(f32)).astype(bf16)
    both = _dot(jnp.concatenate([h_hi, h_lo], axis=1), wr3)
    lg = both[:, :LANES] + both[:, LANES:]
    return lg.T[:ROUTER_ROWS, :]


def _sgu_kernel(x_ref, an_ref, win_f32_ref, bin_ref, vn_ref, ws_ref, bst_ref, wout_ref, mn_ref,
                wr_ref, x1_ref, lg_ref, u_scr, v_scr, gated_scr, win_ref):
    @pl.when(pl.program_id(0) == 0)
    def _():
        win_ref[...] = win_f32_ref[...].astype(bf16)

    x = x_ref[...]
    h = _rms(x, an_ref[...]).astype(bf16)
    v = _gelu_tanh(_dot(h, win_ref[:, SGU_WIDTH:]) + bin_ref[:, SGU_WIDTH:])
    v_scr[...] = _rms(v, vn_ref[...]).astype(bf16)
    u_scr[...] = _gelu_tanh(_dot(h, win_ref[:, :SGU_WIDTH]) + bin_ref[:, :SGU_WIDTH])

    t_chunk = lax.broadcasted_iota(jnp.int32, (SGU_BLOCK, SGU_BLOCK), 0) >> CHUNK_SHIFT
    s_chunk = lax.broadcasted_iota(jnp.int32, (SGU_BLOCK, SGU_BLOCK), 1) >> CHUNK_SHIFT
    causal = t_chunk >= s_chunk
    wsm = [jnp.where(causal, ws_ref[g], 0.0).astype(bf16) for g in range(SGU_GROUPS)]
    for r0 in range(0, SGU_TILE, EPILOGUE_ROWS):
        for g in range(SGU_GROUPS):
            cols = slice(g * SGU_GROUP_DIM, (g + 1) * SGU_GROUP_DIM)
            for sb in range(r0 // SGU_BLOCK, (r0 + EPILOGUE_ROWS) // SGU_BLOCK):
                rows = slice(sb * SGU_BLOCK, (sb + 1) * SGU_BLOCK)
                mixed = _dot(wsm[g], v_scr[rows, cols]) + bst_ref[:, g:g + 1]
                gated_scr[rows, cols] = (u_scr[rows, cols] * mixed).astype(bf16)
        half = slice(r0, r0 + EPILOGUE_ROWS)
        x1 = x[half, :] + _dot(gated_scr[half, :], wout_ref[...])
        x1_ref[half, :] = x1
        lg_ref[:, half] = _router_logits_t(x1, mn_ref[...], wr_ref[...])


def _sgu(x2d, a_norm, w_in, b_in, v_norm, w_s, b_s_t, w_out, m_norm, w_router):
    T = x2d.shape[0]
    return pl.pallas_call(
        _sgu_kernel,
        grid=(T // SGU_TILE,),
        in_specs=[
            pl.BlockSpec((SGU_TILE, D_MODEL), lambda i: (i, 0)),
            _const_spec((1, D_MODEL)),
            _const_spec((D_MODEL, 2 * SGU_WIDTH), single=True),
            _const_spec((1, 2 * SGU_WIDTH)),
            _const_spec((1, SGU_WIDTH)),
            _const_spec((SGU_GROUPS, SGU_BLOCK, SGU_BLOCK)),
            _const_spec((SGU_BLOCK, SGU_GROUPS)),
            _const_spec((SGU_WIDTH, D_MODEL), single=True),
            _const_spec((1, D_MODEL)),
            _const_spec((2 * D_MODEL, 2 * LANES)),
        ],
        out_specs=[
            pl.BlockSpec((SGU_TILE, D_MODEL), lambda i: (i, 0)),
            pl.BlockSpec((ROUTER_ROWS, SGU_TILE), lambda i: (0, i)),
        ],
        out_shape=[
            jax.ShapeDtypeStruct((T, D_MODEL), f32),
            jax.ShapeDtypeStruct((ROUTER_ROWS, T), f32),
        ],
        scratch_shapes=[
            pltpu.VMEM((SGU_TILE, SGU_WIDTH), f32),
            pltpu.VMEM((SGU_TILE, SGU_WIDTH), bf16),
            pltpu.VMEM((SGU_TILE, SGU_WIDTH), bf16),
            pltpu.VMEM((D_MODEL, 2 * SGU_WIDTH), bf16),
        ],
        compiler_params=_params(("arbitrary",)),
        name="sgu",
    )(x2d, a_norm, w_in, b_in, v_norm, w_s, b_s_t, w_out, m_norm, w_router)


def _route_kernel(lg_ref, be_ref, bg_ref, oi_ref, ow_ref, cnt_ref, carry_scr, before_scr,
                  w_scr):
    step = pl.program_id(0)
    tt = lg_ref.shape[1]

    @pl.when(step == 0)
    def _():
        carry_scr[...] = jnp.zeros_like(carry_scr)
        before_scr[...] = (lax.broadcasted_iota(jnp.int32, (tt, tt), 0)
                           < lax.broadcasted_iota(jnp.int32, (tt, tt), 1)).astype(bf16)

    lg = lg_ref[...]
    e_l = lg[0:N_EXPERTS, :] + be_ref[...]
    g_l = lg[N_EXPERTS:N_EXPERTS + 8, :] + bg_ref[...]
    g_row = lax.broadcasted_iota(jnp.int32, g_l.shape, 0).astype(f32)
    g_l = jnp.where(g_row < N_GROUPS, g_l, -jnp.inf)
    g_max = jnp.max(g_l, axis=0, keepdims=True)
    g_sel = jnp.min(jnp.where(g_l == g_max, g_row, 8.0), axis=0, keepdims=True)
    g_den = jnp.sum(jnp.exp(g_l - g_max), axis=0, keepdims=True)
    g_w = 1.0 / g_den

    e_row_i = lax.broadcasted_iota(jnp.int32, e_l.shape, 0)
    e_row = e_row_i.astype(f32)
    e_grp = (e_row_i >> GROUP_SHIFT).astype(f32)
    e_in = jnp.where(e_grp == g_sel, e_l, -jnp.inf)
    m1 = jnp.max(e_in, axis=0, keepdims=True)
    i1 = jnp.min(jnp.where(e_in == m1, e_row, float(N_EXPERTS)), axis=0, keepdims=True)
    e_in2 = jnp.where(e_row == i1, -jnp.inf, e_in)
    m2 = jnp.max(e_in2, axis=0, keepdims=True)
    i2 = jnp.min(jnp.where(e_in2 == m2, e_row, float(N_EXPERTS)), axis=0, keepdims=True)
    t = jnp.exp(m2 - m1)
    p1 = 1.0 / (1.0 + t)
    p2 = t / (1.0 + t)

    hit1 = e_row == i1
    hit2 = e_row == i2
    member = jnp.logical_or(hit1, hit2)
    prefix = _dot(member.astype(bf16), before_scr[...])
    rank_full = prefix + carry_scr[...]
    rank1 = jnp.sum(jnp.where(hit1, rank_full, 0.0), axis=0, keepdims=True)
    rank2 = jnp.sum(jnp.where(hit2, rank_full, 0.0), axis=0, keepdims=True)
    carry_new = carry_scr[...] + jnp.sum(member.astype(f32), axis=1, keepdims=True)
    carry_scr[...] = carry_new

    oi_ref[...] = jnp.zeros(oi_ref.shape, jnp.int32)
    oi_ref[0:1, :] = i1.astype(jnp.int32)
    oi_ref[1:2, :] = i2.astype(jnp.int32)
    oi_ref[2:3, :] = rank1.astype(jnp.int32)
    oi_ref[3:4, :] = rank2.astype(jnp.int32)
    w_scr[...] = jnp.zeros(w_scr.shape, f32)
    w_scr[0:1, :] = g_w * p1
    w_scr[1:2, :] = g_w * p2
    ow_ref[...] = w_scr[...].T
    cnt_ref[...] = jnp.broadcast_to(carry_new, cnt_ref.shape)


def _route(logits_t, b_expert, b_group):
    T = logits_t.shape[1]
    return pl.pallas_call(
        _route_kernel,
        grid=(T // ROUTE_TILE,),
        in_specs=[
            pl.BlockSpec((ROUTER_ROWS, ROUTE_TILE), lambda i: (0, i)),
            _const_spec((N_EXPERTS, 1)),
            _const_spec((8, 1)),
        ],
        out_specs=[
            pl.BlockSpec((8, ROUTE_TILE), lambda i: (0, i)),
            pl.BlockSpec((ROUTE_TILE, 8), lambda i: (i, 0)),
            _const_spec((N_EXPERTS, LANES)),
        ],
        out_shape=[
            jax.ShapeDtypeStruct((8, T), jnp.int32),
            jax.ShapeDtypeStruct((T, 8), f32),
            jax.ShapeDtypeStruct((N_EXPERTS, LANES), f32),
        ],
        scratch_shapes=[pltpu.VMEM((N_EXPERTS, 1), f32),
                        pltpu.VMEM((ROUTE_TILE, ROUTE_TILE), bf16),
                        pltpu.VMEM((8, ROUTE_TILE), f32)],
        compiler_params=_params(("arbitrary",)),
        name="route",
    )(logits_t, b_expert, b_group)


def _row_copy(src, s, dst, d, sem):
    return pltpu.make_async_copy(src.at[s], dst.at[d], sem)


def _wait_rows(buf, sem):
    pltpu.make_async_copy(buf, buf, sem).wait()


def _dispatch_kernel(n_blocks, pend_ref, padded_ref, nu_ref, dest_ref, x_ref, mn_ref, xs_hbm,
                     zero_scr, x3_scr, zsem, sem):
    @pl.when(pl.program_id(0) == 0)
    def _():
        zero_scr[...] = jnp.zeros_like(zero_scr)

        def fill(row0):
            return pltpu.make_async_copy(zero_scr, xs_hbm.at[pl.ds(row0, EXPERT_ROWS)], zsem)

        for e in range(N_EXPERTS):
            @pl.when(padded_ref[e] > 0)
            def _():
                fill(pend_ref[e] - EXPERT_ROWS).start()

            @pl.when(nu_ref[0] + e < n_blocks)
            def _():
                fill((nu_ref[0] + e) * EXPERT_ROWS).start()

        for e in range(N_EXPERTS):
            @pl.when(padded_ref[e] > 0)
            def _():
                fill(0).wait()

            @pl.when(nu_ref[0] + e < n_blocks)
            def _():
                fill(0).wait()

    i = pl.program_id(0)
    n_tiles = pl.num_programs(0) - 1

    def staged(rows):
        return _pack_rows(_rms(x_ref[rows, :], mn_ref[...]))

    @pl.when(i == 0)
    def _():
        x3_scr[0] = staged(slice(None))

    @pl.when(i > 0)
    def _():
        cur = (i - 1) % 3
        nxt = i % 3

        def chunk(c, carry):
            r0 = pl.multiple_of(c * MOVE_CHUNK, MOVE_CHUNK)
            for j in range(MOVE_CHUNK):
                for k in range(TOP_K):
                    _row_copy(x3_scr.at[cur], r0 + j, xs_hbm,
                              dest_ref[0, k * MOVE_TILE + r0 + j], sem.at[cur]).start(priority=k)
            x3_scr[nxt, pl.ds(r0, MOVE_CHUNK)] = staged(pl.ds(r0, MOVE_CHUNK))
            return carry

        lax.fori_loop(0, MOVE_TILE // MOVE_CHUNK, chunk, 0)

        @pl.when(i > 1)
        def _():
            for k in range(TOP_K):
                _wait_rows(x3_scr.at[(i - 2) % 3], sem.at[(i - 2) % 3])

        @pl.when(i == n_tiles)
        def _():
            for k in range(TOP_K):
                _wait_rows(x3_scr.at[cur], sem.at[cur])


def _dispatch(pend, padded, n_used, dest_tiles, x2d, m_norm, n_rows):
    T = x2d.shape[0]
    n_blocks = n_rows // EXPERT_ROWS
    n_tiles = T // MOVE_TILE
    grid_spec = pltpu.PrefetchScalarGridSpec(
        num_scalar_prefetch=3,
        grid=(n_tiles + 1,),
        in_specs=[
            pl.BlockSpec((None, 1, TOP_K * MOVE_TILE),
                         lambda i, *_: (jnp.maximum(i - 1, 0), 0, 0), memory_space=pltpu.SMEM),
            pl.BlockSpec((MOVE_TILE, D_MODEL), lambda i, *_: (jnp.minimum(i, n_tiles - 1), 0)),
            pl.BlockSpec((1, D_MODEL), lambda i, *_: (0, 0)),
        ],
        out_specs=pl.BlockSpec(memory_space=pl.ANY),
        scratch_shapes=[
            pltpu.VMEM((EXPERT_ROWS, ROW_SUBLANES, LANES), u32),
            pltpu.VMEM((3, MOVE_TILE, ROW_SUBLANES, LANES), u32),
            pltpu.SemaphoreType.DMA(()),
            pltpu.SemaphoreType.DMA((3,)),
        ],
    )
    return pl.pallas_call(
        functools.partial(_dispatch_kernel, n_blocks),
        grid_spec=grid_spec,
        out_shape=jax.ShapeDtypeStruct((n_rows, ROW_SUBLANES, LANES), u32),
        compiler_params=_params(("arbitrary",)),
        name="dispatch",
    )(pend, padded, n_used, dest_tiles, x2d, m_norm)


def _expert_kernel(n_blocks, ps_ref, nb_ref, nu_ref, xs_hbm, w1_ref, w3_ref, w2_ref,
                   ys_hbm, xbuf, ybuf, h_scr, w1_scr, w3_scr, w2_scr, xsem, ysem):
    e = pl.program_id(0)
    nblk = nb_ref[e]
    g0 = ps_ref[e]
    n_used = nu_ref[0]

    def rows(g):
        return pl.ds(pl.multiple_of(g * EXPERT_ROWS, EXPERT_ROWS), EXPERT_ROWS)

    def x_copy(g):
        return pltpu.make_async_copy(xs_hbm.at[rows(g)], xbuf.at[g % 2], xsem.at[g % 2])

    def y_copy(g):
        return pltpu.make_async_copy(ybuf.at[g % 2], ys_hbm.at[rows(g)], ysem.at[g % 2])

    def normed(slot):
        return _unpack_rows(xbuf[slot]).astype(bf16)

    @pl.when(e == 0)
    def _():
        x_copy(0).start(priority=1)
        x_copy(1).start(priority=1)
        x_copy(0).wait()
        h_scr[...] = normed(0)

    @pl.when(nblk > 0)
    def _():
        w1_scr[...] = w1_ref[...].astype(bf16)
        w3_scr[...] = w3_ref[...].astype(bf16)
        w2_scr[...] = w2_ref[...].astype(bf16)

        def body(g, h):
            @pl.when(g + 2 < n_used)
            def _():
                x_copy(g + 2).start(priority=1)

            @pl.when(g + 1 < n_used)
            def _():
                x_copy(g + 1).wait()

            @pl.when(g >= 2)
            def _():
                y_copy(g - 2).wait()

            h_next = normed((g + 1) % 2)
            y = None
            for c0 in range(0, D_EXPERT, D_EXPERT // 2):
                cols = slice(c0, c0 + D_EXPERT // 2)
                a = _dot(h, w1_scr[:, cols])
                c = _dot(h, w3_scr[:, cols])
                hid = ((a * jax.nn.sigmoid(a)) * c).astype(bf16)
                part = _dot(hid, w2_scr[cols, :])
                y = part if y is None else y + part
            ybuf[g % 2] = _pack_rows(y)
            y_copy(g).start(priority=1)
            return h_next

        h_scr[...] = lax.fori_loop(g0, g0 + nblk, body, h_scr[...])

    @pl.when(e == pl.num_programs(0) - 1)
    def _():
        y_copy(n_used - 2).wait()
        y_copy(n_used - 1).wait()
        ybuf[0] = jnp.zeros(ybuf.shape[1:], u32)

        def fill(b):
            r = pl.ds(pl.multiple_of(b * EXPERT_ROWS, EXPERT_ROWS), EXPERT_ROWS)
            return pltpu.make_async_copy(ybuf.at[0], ys_hbm.at[r], ysem.at[0])

        def start(b, c):
            fill(b).start()
            return c

        def wait(b, c):
            fill(b).wait()
            return c

        lax.fori_loop(nu_ref[0], n_blocks, start, 0)
        lax.fori_loop(nu_ref[0], n_blocks, wait, 0)


def _experts(layer, pstart, seg_blocks, n_used, xs, w1, w3, w2):
    n_rows = xs.shape[0]
    n_blocks = n_rows // EXPERT_ROWS

    def w_map(e, *_):
        return (layer, e, 0, 0)

    block_buf = pltpu.VMEM((2, EXPERT_ROWS, ROW_SUBLANES, LANES), u32)
    grid_spec = pltpu.PrefetchScalarGridSpec(
        num_scalar_prefetch=3,
        grid=(N_EXPERTS,),
        in_specs=[
            pl.BlockSpec(memory_space=pl.ANY),
            pl.BlockSpec((None, None, D_MODEL, D_EXPERT), w_map),
            pl.BlockSpec((None, None, D_MODEL, D_EXPERT), w_map),
            pl.BlockSpec((None, None, D_EXPERT, D_MODEL), w_map),
        ],
        out_specs=pl.BlockSpec(memory_space=pl.ANY),
        scratch_shapes=[
            block_buf,
            block_buf,
            pltpu.VMEM((EXPERT_ROWS, D_MODEL), bf16),
            pltpu.VMEM((D_MODEL, D_EXPERT), bf16),
            pltpu.VMEM((D_MODEL, D_EXPERT), bf16),
            pltpu.VMEM((D_EXPERT, D_MODEL), bf16),
            pltpu.SemaphoreType.DMA((2,)),
            pltpu.SemaphoreType.DMA((2,)),
        ],
    )
    return pl.pallas_call(
        functools.partial(_expert_kernel, n_blocks),
        grid_spec=grid_spec,
        out_shape=jax.ShapeDtypeStruct((n_rows, ROW_SUBLANES, LANES), u32),
        compiler_params=_params(("arbitrary",)),
        name="experts",
    )(pstart, seg_blocks, n_used, xs, w1, w3, w2)


def _combine_kernel(dest_ref, next_ref, x_ref, w_ref, ys_hbm, out_ref, buf, sem):
    i = pl.program_id(0)
    slot = i % 2

    def gather(idx_ref, s):
        def start(j, c):
            for k in range(TOP_K):
                _row_copy(ys_hbm, idx_ref[0, k * MOVE_TILE + j], buf.at[s, k], j,
                          sem.at[s]).start(priority=k)
            return c

        lax.fori_loop(0, MOVE_TILE, start, 0, unroll=ISSUE_UNROLL)

    @pl.when(i == 0)
    def _():
        gather(dest_ref, slot)

    @pl.when(i + 1 < pl.num_programs(0))
    def _():
        gather(next_ref, 1 - slot)

    for k in range(TOP_K):
        _wait_rows(buf.at[slot, k], sem.at[slot])
    w = w_ref[...]
    y0 = _unpack_rows(buf[slot, 0])
    y1 = _unpack_rows(buf[slot, 1])
    out_ref[...] = x_ref[...] + (w[:, 0:1] * y0 + w[:, 1:2] * y1)


def _combine(dest_tiles, x2d, w_cols, ys):
    T = x2d.shape[0]
    n_tiles = T // MOVE_TILE
    return pl.pallas_call(
        _combine_kernel,
        grid=(n_tiles,),
        in_specs=[
            pl.BlockSpec((None, 1, TOP_K * MOVE_TILE), lambda i: (i, 0, 0),
                         memory_space=pltpu.SMEM),
            pl.BlockSpec((None, 1, TOP_K * MOVE_TILE),
                         lambda i: (jnp.minimum(i + 1, n_tiles - 1), 0, 0),
                         memory_space=pltpu.SMEM),
            pl.BlockSpec((MOVE_TILE, D_MODEL), lambda i: (i, 0)),
            pl.BlockSpec((MOVE_TILE, 8), lambda i: (i, 0)),
            pl.BlockSpec(memory_space=pl.ANY),
        ],
        out_specs=pl.BlockSpec((MOVE_TILE, D_MODEL), lambda i: (i, 0)),
        out_shape=jax.ShapeDtypeStruct((T, D_MODEL), f32),
        scratch_shapes=[
            pltpu.VMEM((2, TOP_K, MOVE_TILE, ROW_SUBLANES, LANES), u32),
            pltpu.SemaphoreType.DMA((2,)),
        ],
        compiler_params=_params(("arbitrary",)),
        name="combine",
    )(dest_tiles, dest_tiles, x2d, w_cols, ys)


def _moe(layer, x2d, logits_t, m_norm, b_group, b_expert, w1, w3, w2):
    T = x2d.shape[0]
    n_blocks = (T * TOP_K) // EXPERT_ROWS + N_EXPERTS
    n_rows = n_blocks * EXPERT_ROWS

    be = b_expert.reshape(N_EXPERTS, 1)
    bg = jnp.concatenate([b_group, jnp.zeros((8 - N_GROUPS,), f32)]).reshape(8, 1)
    oi, w_cols, cnt = _route(logits_t, be, bg)

    counts = cnt[:, 0].astype(jnp.int32)
    padded = (counts + EXPERT_ROWS - 1) // EXPERT_ROWS * EXPERT_ROWS
    pend = jnp.cumsum(padded)
    pstart = pend - padded
    n_used = (pend[-1] // EXPERT_ROWS).astype(jnp.int32).reshape(1)
    e_ids = jnp.arange(N_EXPERTS, dtype=jnp.int32)[:, None, None]
    seg_start = jnp.sum(jnp.where(oi[None, 0:2] == e_ids, pstart[:, None, None], 0), axis=0)
    dest = seg_start + oi[2:4]
    dest_tiles = dest.reshape(TOP_K, T // MOVE_TILE, MOVE_TILE).transpose(1, 0, 2).reshape(
        T // MOVE_TILE, 1, TOP_K * MOVE_TILE)

    xs = _dispatch(pend.astype(jnp.int32), padded, n_used, dest_tiles, x2d, m_norm, n_rows)
    ys = _experts(layer, (pstart // EXPERT_ROWS).astype(jnp.int32), padded // EXPERT_ROWS,
                  n_used, xs, w1, w3, w2)
    return _combine(dest_tiles, x2d, w_cols, ys)


def _log_sigmoid(x):
    return jnp.minimum(x, 0.0) - jnp.log1p(jnp.exp(-jnp.abs(x)))


def _proj_kernel(tiles_per_seq, x_ref, kvn_ref, bn_ref, kvw_ref, bf_ref, qgw_ref, gq_ref, gk_ref,
                 q_ref, kt_ref, v_ref, sg_ref, carry_scr, wkt_ref, wv_ref, wf_ref, wq_ref, wg_ref):
    i = pl.program_id(0)

    @pl.when(i == 0)
    def _():
        wkt_ref[...] = kvw_ref[:D_MODEL, :].astype(bf16)
        wv_ref[...] = kvw_ref[D_MODEL:2 * D_MODEL, :].T.astype(bf16)
        wf_ref[...] = jnp.concatenate(
            [kvw_ref[2 * D_MODEL:, :].T, jnp.zeros((D_MODEL, LANES - N_HEADS), f32)],
            axis=1).astype(bf16)
        wg_ref[...] = qgw_ref[:, D_MODEL:].astype(bf16)
        zeros = jnp.zeros((D_MODEL, HEAD_PAD - HEAD_DIM), f32)
        for h in range(N_HEADS):
            wq_ref[:, h * HEAD_PAD:(h + 1) * HEAD_PAD] = jnp.concatenate(
                [qgw_ref[:, h * HEAD_DIM:(h + 1) * HEAD_DIM], zeros], axis=1).astype(bf16)

    x = x_ref[...]
    hkv = _rms(x, kvn_ref[...]).astype(bf16)
    hq = _rms(x, bn_ref[...]).astype(bf16)

    v_ref[...] = _dot(hkv, wv_ref[...]).astype(bf16)
    sg_ref[...] = jax.nn.sigmoid(_dot(hq, wg_ref[...])).astype(bf16)

    logf = _log_sigmoid(_dot(hkv, wf_ref[...]) + bf_ref[...])
    tm = x.shape[0]
    incl = (lax.broadcasted_iota(jnp.int32, (tm, tm), 0)
            >= lax.broadcasted_iota(jnp.int32, (tm, tm), 1)).astype(bf16)

    @pl.when(i % tiles_per_seq == 0)
    def _():
        carry_scr[...] = jnp.zeros_like(carry_scr)

    parts = _dot(incl, jnp.concatenate(_split3(logf), axis=1).astype(bf16))
    cum = (parts[:, :LANES] + parts[:, LANES:2 * LANES] + parts[:, 2 * LANES:]
           + carry_scr[...])
    carry_scr[...] = cum[tm - 1:tm, :]
    cum = cum * LOG2E
    cum_t = cum.T

    lane = lax.broadcasted_iota(jnp.int32, (tm, HEAD_PAD), 1)
    sub = lax.broadcasted_iota(jnp.int32, (HEAD_DIM, tm), 0)
    q_raw = _dot(hq, wq_ref[...])
    k_raw_t = _dot_nt(wkt_ref[...], hkv)
    scale = HEAD_DIM ** -0.5 * LOG2E

    for h in range(N_HEADS):
        sl = slice(h * HEAD_PAD, (h + 1) * HEAD_PAD)
        qb = q_raw[:, sl]
        q_ms = jnp.sum(qb * qb, axis=-1, keepdims=True) * (1.0 / HEAD_DIM)
        qn = qb * lax.rsqrt(q_ms + EPS) * gq_ref[...] * scale
        c_hi, c_mid, c_lo = _split3(cum[:, h:h + 1])
        qa = jnp.where(lane < HEAD_DIM, qn,
             jnp.where(lane < AUX_CUM0, 1.0,
             jnp.where(lane == AUX_CUM0, c_hi,
             jnp.where(lane == AUX_CUM0 + 1, c_mid,
             jnp.where(lane == AUX_CUM0 + 2, c_lo, 0.0)))))
        q_ref[:, sl] = qa.astype(bf16)

        kb = k_raw_t[h * HEAD_DIM:(h + 1) * HEAD_DIM, :]
        k_ms = jnp.sum(kb * kb, axis=0, keepdims=True) * (1.0 / HEAD_DIM)
        kn = kb * lax.rsqrt(k_ms + EPS) * gk_ref[...]
        t_hi, t_mid, t_lo = _split3(cum_t[h:h + 1, :])
        aux = jnp.where(sub == 0, -t_hi,
              jnp.where(sub == 1, -t_mid,
              jnp.where(sub == 2, -t_lo,
              jnp.where(sub < 6, 1.0, 0.0))))
        sl = slice(h * HEAD_PAD, (h + 1) * HEAD_PAD)
        ka = jnp.concatenate([kn, aux], axis=0).astype(bf16)
        for t in range(tm // ATT_K):
            kt_ref[t, sl, :] = ka[:, t * ATT_K:(t + 1) * ATT_K]


def _proj(x2d, seq, kv_norm, b_norm, kv_w, bfv, w_qg, gq, gk):
    T = x2d.shape[0]
    n_tiles = T // ROW_TILE
    qw = N_HEADS * HEAD_PAD
    return pl.pallas_call(
        functools.partial(_proj_kernel, seq // ROW_TILE),
        grid=(n_tiles,),
        in_specs=[
            pl.BlockSpec((ROW_TILE, D_MODEL), lambda i: (i, 0)),
            _const_spec((1, D_MODEL)),
            _const_spec((1, D_MODEL)),
            _const_spec(kv_w.shape, single=True),
            _const_spec((1, LANES)),
            _const_spec(w_qg.shape, single=True),
            _const_spec((1, HEAD_PAD)),
            _const_spec((HEAD_DIM, 1)),
        ],
        out_specs=[
            pl.BlockSpec((ROW_TILE, qw), lambda i: (i, 0)),
            pl.BlockSpec((ROW_TILE // ATT_K, qw, ATT_K), lambda i: (i, 0, 0)),
            pl.BlockSpec((ROW_TILE, D_MODEL), lambda i: (i, 0)),
            pl.BlockSpec((ROW_TILE, D_MODEL), lambda i: (i, 0)),
        ],
        out_shape=[
            jax.ShapeDtypeStruct((T, qw), bf16),
            jax.ShapeDtypeStruct((T // ATT_K, qw, ATT_K), bf16),
            jax.ShapeDtypeStruct((T, D_MODEL), bf16),
            jax.ShapeDtypeStruct((T, D_MODEL), bf16),
        ],
        scratch_shapes=[
            pltpu.VMEM((1, LANES), f32),
            pltpu.VMEM((D_MODEL, D_MODEL), bf16),
            pltpu.VMEM((D_MODEL, D_MODEL), bf16),
            pltpu.VMEM((D_MODEL, LANES), bf16),
            pltpu.VMEM((D_MODEL, qw), bf16),
            pltpu.VMEM((D_MODEL, D_MODEL), bf16),
        ],
        compiler_params=_params(("arbitrary",)),
        name="proj",
    )(x2d, kv_norm, b_norm, kv_w, bfv, w_qg, gq, gk)


def _attn_kernel(q_ref, kt_ref, v_ref, o_ref, s_scr, m_scr, l_scr, acc_scr):
    qi = pl.program_id(2)
    tiles = ATT_Q // ATT_K
    groups = ATT_K // LANES
    heads = [slice(hh * HEAD_PAD, (hh + 1) * HEAD_PAD) for hh in range(2)]
    m_scr[...] = jnp.full(m_scr.shape, -jnp.inf, f32)

    def lane_max(mx, s):
        for g in range(groups):
            mx = jnp.maximum(mx, s[:, g * LANES:(g + 1) * LANES])
        return mx

    def probs(s, mb):
        ps = [jnp.exp2(s[:, g * LANES:(g + 1) * LANES] - mb) for g in range(groups)]
        return ps, functools.reduce(lambda a, b: a + b, ps)

    def body_a(trip, c):
        for t in range(tiles):
            kt_idx = trip * tiles + t
            for hh in range(2):
                s = _dot(q_ref[:, heads[hh]], kt_ref[kt_idx, heads[hh], :])
                s_scr[hh, kt_idx] = s
                m_scr[hh] = lane_max(m_scr[hh], s)
        return c

    lax.fori_loop(0, qi, body_a, 0)
    for t in range(tiles):
        kt_idx = qi * tiles + t
        rows = ATT_Q - t * ATT_K
        visible = (lax.broadcasted_iota(jnp.int32, (rows, ATT_K), 1)
                   <= lax.broadcasted_iota(jnp.int32, (rows, ATT_K), 0))
        for hh in range(2):
            s = _dot(q_ref[t * ATT_K:, heads[hh]], kt_ref[kt_idx, heads[hh], :])
            s = jnp.where(visible, s, -jnp.inf)
            s_scr[hh, kt_idx, t * ATT_K:, :] = s
            m_scr[hh, t * ATT_K:, :] = lane_max(m_scr[hh, t * ATT_K:, :], s)

    for hh in range(2):
        row_max = jnp.max(m_scr[hh], axis=-1, keepdims=True)
        m_scr[hh] = jnp.broadcast_to(row_max, (ATT_Q, LANES))
    l_scr[...] = jnp.zeros(l_scr.shape, f32)
    acc_scr[...] = jnp.zeros(acc_scr.shape, f32)

    def body_b(trip, c):
        for hh in range(2):
            mb = m_scr[hh]
            lsum, acc = l_scr[hh], acc_scr[hh]
            for t0 in range(0, tiles, PV_TILES):
                kt_idx = trip * tiles + t0
                row0 = pl.multiple_of(kt_idx * ATT_K, PV_TILES * ATT_K)
                vb = v_ref[pl.ds(row0, PV_TILES * ATT_K), :]
                parts = []
                for t in range(PV_TILES):
                    ps, psum = probs(s_scr[hh, kt_idx + t], mb)
                    parts += ps
                    lsum = lsum + psum
                acc = acc + _dot(jnp.concatenate(parts, axis=1).astype(bf16), vb)
            l_scr[hh] = lsum
            acc_scr[hh] = acc
        return c

    lax.fori_loop(0, qi, body_b, 0)
    for t in range(tiles):
        kt_idx = qi * tiles + t
        row0 = pl.multiple_of(kt_idx * ATT_K, ATT_K)
        vb = v_ref[pl.ds(row0, ATT_K), :]
        for hh in range(2):
            ps, psum = probs(s_scr[hh, kt_idx, t * ATT_K:, :], m_scr[hh, t * ATT_K:, :])
            l_scr[hh, t * ATT_K:, :] += psum
            acc_scr[hh, t * ATT_K:, :] += _dot(jnp.concatenate(ps, axis=1).astype(bf16), vb)

    lane = lax.broadcasted_iota(jnp.int32, (ATT_Q, 2 * HEAD_DIM), 1)
    o0 = acc_scr[0] / jnp.sum(l_scr[0], axis=-1, keepdims=True)
    o1 = acc_scr[1] / jnp.sum(l_scr[1], axis=-1, keepdims=True)
    o_ref[...] = jnp.where(lane < HEAD_DIM, o0, o1).astype(bf16)


def _attention(q, kt, v, batch, seq):
    T = q.shape[0]
    nq = seq // ATT_Q
    nk = seq // ATT_K
    pairs = N_HEADS // 2
    return pl.pallas_call(
        _attn_kernel,
        grid=(batch, pairs, nq),
        in_specs=[
            pl.BlockSpec((ATT_Q, 2 * HEAD_PAD), lambda b, p, i: (b * nq + i, p)),
            pl.BlockSpec((nk, 2 * HEAD_PAD, ATT_K), lambda b, p, i: (b, p, 0)),
            pl.BlockSpec((seq, 2 * HEAD_DIM), lambda b, p, i: (b, p)),
        ],
        out_specs=pl.BlockSpec((ATT_Q, 2 * HEAD_DIM), lambda b, p, i: (b * nq + i, p)),
        out_shape=jax.ShapeDtypeStruct((T, D_MODEL), bf16),
        scratch_shapes=[
            pltpu.VMEM((2, nk, ATT_Q, ATT_K), f32),
            pltpu.VMEM((2, ATT_Q, LANES), f32),
            pltpu.VMEM((2, ATT_Q, LANES), f32),
            pltpu.VMEM((2, ATT_Q, 2 * HEAD_DIM), f32),
        ],
        compiler_params=_params(("parallel", "parallel", "parallel")),
        name="attn",
    )(q, kt, v)


def _attn_out_kernel(x_ref, o_ref, sg_ref, wo_ref, mn_ref, wr_ref, x3_ref, lg_ref):
    for r0 in range(0, SGU_TILE, EPILOGUE_ROWS):
        rows = slice(r0, r0 + EPILOGUE_ROWS)
        gated = (o_ref[rows, :].astype(f32) * sg_ref[rows, :].astype(f32)).astype(bf16)
        x3 = x_ref[rows, :] + _dot(gated, wo_ref[...])
        x3_ref[rows, :] = x3
        lg_ref[:, rows] = _router_logits_t(x3, mn_ref[...], wr_ref[...])


def _attn_out(x2d, o, sg, wo, m_norm, w_router):
    T = x2d.shape[0]
    return pl.pallas_call(
        _attn_out_kernel,
        grid=(T // SGU_TILE,),
        in_specs=[
            pl.BlockSpec((SGU_TILE, D_MODEL), lambda i: (i, 0)),
            pl.BlockSpec((SGU_TILE, D_MODEL), lambda i: (i, 0)),
            pl.BlockSpec((SGU_TILE, D_MODEL), lambda i: (i, 0)),
            _const_spec((D_MODEL, D_MODEL)),
            _const_spec((1, D_MODEL)),
            _const_spec((2 * D_MODEL, 2 * LANES)),
        ],
        out_specs=[
            pl.BlockSpec((SGU_TILE, D_MODEL), lambda i: (i, 0)),
            pl.BlockSpec((ROUTER_ROWS, SGU_TILE), lambda i: (0, i)),
        ],
        out_shape=[
            jax.ShapeDtypeStruct((T, D_MODEL), f32),
            jax.ShapeDtypeStruct((ROUTER_ROWS, T), f32),
        ],
        compiler_params=_params(("parallel",)),
        name="attn_out",
    )(x2d, o, sg, wo, m_norm, w_router)


def _router_weight(w_group, w_expert):
    pad = jnp.zeros((D_MODEL, LANES - N_EXPERTS - N_GROUPS), f32)
    w = jnp.concatenate([w_expert, w_group, pad], axis=1)
    w_hi = w.astype(bf16)
    w_lo = (w - w_hi.astype(f32)).astype(bf16)
    top = jnp.concatenate([w_hi, w_lo], axis=1)
    bottom = jnp.concatenate([w_hi, jnp.zeros_like(w_hi)], axis=1)
    return jnp.concatenate([top, bottom], axis=0)


def kernel(x, a_norm, a_w_in, a_b_in, a_v_norm, a_w_s, a_b_s, a_w_out, kv_norm, kv_w, kv_b_f,
           k_norm, b_norm, b_w_qg, q_norm, b_w_out, m_norm, m_w_group, m_b_group, m_w_expert,
           m_b_expert, m_w1, m_w3, m_w2):
    batch, seq, _ = x.shape
    T = batch * seq
    x2d = x.reshape(T, D_MODEL)

    x1, lg0 = _sgu(
        x2d, a_norm[0].reshape(1, -1), a_w_in[0], a_b_in[0].reshape(1, -1),
        a_v_norm[0].reshape(1, -1), a_w_s[0], a_b_s[0].T, a_w_out[0].astype(bf16),
        m_norm[0].reshape(1, -1), _router_weight(m_w_group[0], m_w_expert[0]))
    x2 = _moe(0, x1, lg0, m_norm[0].reshape(1, -1), m_b_group[0], m_b_expert[0],
              m_w1, m_w3, m_w2)

    bfv = jnp.pad(kv_b_f, (0, LANES - N_HEADS)).reshape(1, LANES)
    gq = jnp.pad(q_norm[0], (0, HEAD_PAD - HEAD_DIM)).reshape(1, HEAD_PAD)
    gk = k_norm.reshape(HEAD_DIM, 1)
    q, kt, v, sg = _proj(x2, seq, kv_norm.reshape(1, -1), b_norm[0].reshape(1, -1),
                         kv_w.T, bfv, b_w_qg[0], gq, gk)
    o = _attention(q, kt, v, batch, seq)
    x3, lg1 = _attn_out(x2, o, sg, b_w_out[0].astype(bf16), m_norm[1].reshape(1, -1),
                        _router_weight(m_w_group[1], m_w_expert[1]))
    x4 = _moe(1, x3, lg1, m_norm[1].reshape(1, -1), m_b_group[1], m_b_expert[1],
              m_w1, m_w3, m_w2)
    return x4.reshape(batch, seq, D_MODEL)
```

```python
import functools
import math

import jax
import jax.numpy as jnp
import numpy as np
from jax import lax
from jax.experimental import pallas as pl
from jax.experimental.pallas import tpu as pltpu
from jax.experimental.pallas import tpu_sc as plsc

D_MODEL = 1024
EPS = 1e-6
SGU_BLOCK = 128
SGU_CHUNK = 64
CHUNK_SHIFT = SGU_CHUNK.bit_length() - 1
SGU_WIDTH = 2 * D_MODEL
SGU_GROUPS = 8
SGU_GROUP_DIM = SGU_WIDTH // SGU_GROUPS
N_HEADS = 16
HEAD_DIM = D_MODEL // N_HEADS
N_GROUPS = 4
EXPERTS_PER_GROUP = 8
N_EXPERTS = N_GROUPS * EXPERTS_PER_GROUP
GROUP_SHIFT = EXPERTS_PER_GROUP.bit_length() - 1
TOP_K = 2
D_EXPERT = D_MODEL // 2

LANES = 128
ROW_SUBLANES = D_MODEL // 2 // LANES
ISSUE_UNROLL = 8
VMEM_LIMIT_BYTES = 56 * 1024 * 1024

ROW_TILE = 256
SGU_TILE = 512
EPILOGUE_ROWS = 256
ROUTE_TILE = 1024
EXPERT_ROWS = 512
MOVE_TILE = 512
MOVE_CHUNK = 64
GATHER_WINDOW = 128
GATHER_WIDTH = 256
ATT_Q = 1024
ATT_K = 256
PV_TILES = 2
HEAD_PAD = 2 * HEAD_DIM
ROUTER_ROWS = 40
AUX_ONE0 = HEAD_DIM
AUX_CUM0 = HEAD_DIM + 3

LOG2E = math.log2(math.e)

bf16 = jnp.bfloat16
f32 = jnp.float32
u32 = jnp.uint32
HI_HALF = np.uint32(0xFFFF0000)


def _dot(a, b, precision=None):
    return jnp.dot(a, b, preferred_element_type=f32, precision=precision)


def _dot_nt(a, b, precision=None):
    return lax.dot_general(a, b, (((1,), (1,)), ((), ())), preferred_element_type=f32,
                           precision=precision)


def _rms(x, g):
    ms = jnp.mean(x * x, axis=-1, keepdims=True)
    return x * lax.rsqrt(ms + EPS) * g


def _gelu_tanh(z):
    c = math.sqrt(2.0 / math.pi)
    return z * (0.5 * (1.0 + jnp.tanh(c * (z + 0.044715 * (z * z * z)))))


def _split3(c):
    hi = c.astype(bf16).astype(f32)
    r = c - hi
    mid = r.astype(bf16).astype(f32)
    lo = r - mid
    return hi, mid, lo


def _pack_rows(v):
    half = D_MODEL // 2
    lo = lax.bitcast_convert_type(v[:, :half].astype(bf16).astype(f32), u32) >> 16
    hi = lax.bitcast_convert_type(v[:, half:].astype(bf16).astype(f32), u32) & HI_HALF
    return (hi | lo).reshape(v.shape[0], ROW_SUBLANES, LANES)


def _unpack_rows(w):
    w = w.reshape(w.shape[0], D_MODEL // 2)
    lo = lax.bitcast_convert_type(w << 16, f32)
    hi = lax.bitcast_convert_type(w & HI_HALF, f32)
    return jnp.concatenate([lo, hi], axis=1)


def _const_spec(shape, single=False):
    nd = len(shape)
    mode = pl.Buffered(1) if single else None
    return pl.BlockSpec(shape, lambda *_: (0,) * nd, pipeline_mode=mode)


def _params(sem):
    return pltpu.CompilerParams(dimension_semantics=sem, vmem_limit_bytes=VMEM_LIMIT_BYTES)


def _router_logits_t(x_new, mn, wr3):
    hm = _rms(x_new, mn)
    h_hi = hm.astype(bf16)
    h_lo = (hm - h_hi.astype(f32)).astype(bf16)
    both = _dot(jnp.concatenate([h_hi, h_lo], axis=1), wr3)
    lg = both[:, :LANES] + both[:, LANES:]
    return lg.T[:ROUTER_ROWS, :]


def _sgu_kernel(x_ref, an_ref, win_f32_ref, bin_ref, vn_ref, ws_ref, bst_ref, wout_ref, mn_ref,
                wr_ref, x1_ref, lg_ref, u_scr, v_scr, gated_scr, win_ref):
    @pl.when(pl.program_id(0) == 0)
    def _():
        win_ref[...] = win_f32_ref[...].astype(bf16)

    x = x_ref[...]
    h = _rms(x, an_ref[...]).astype(bf16)
    v = _gelu_tanh(_dot(h, win_ref[:, SGU_WIDTH:]) + bin_ref[:, SGU_WIDTH:])
    v_scr[...] = _rms(v, vn_ref[...]).astype(bf16)
    u_scr[...] = _gelu_tanh(_dot(h, win_ref[:, :SGU_WIDTH]) + bin_ref[:, :SGU_WIDTH])

    t_chunk = lax.broadcasted_iota(jnp.int32, (SGU_BLOCK, SGU_BLOCK), 0) >> CHUNK_SHIFT
    s_chunk = lax.broadcasted_iota(jnp.int32, (SGU_BLOCK, SGU_BLOCK), 1) >> CHUNK_SHIFT
    causal = t_chunk >= s_chunk
    wsm = [jnp.where(causal, ws_ref[g], 0.0).astype(bf16) for g in range(SGU_GROUPS)]
    for r0 in range(0, SGU_TILE, EPILOGUE_ROWS):
        for g in range(SGU_GROUPS):
            cols = slice(g * SGU_GROUP_DIM, (g + 1) * SGU_GROUP_DIM)
            for sb in range(r0 // SGU_BLOCK, (r0 + EPILOGUE_ROWS) // SGU_BLOCK):
                rows = slice(sb * SGU_BLOCK, (sb + 1) * SGU_BLOCK)
                mixed = _dot(wsm[g], v_scr[rows, cols]) + bst_ref[:, g:g + 1]
                gated_scr[rows, cols] = (u_scr[rows, cols] * mixed).astype(bf16)
        half = slice(r0, r0 + EPILOGUE_ROWS)
        x1 = x[half, :] + _dot(gated_scr[half, :], wout_ref[...])
        x1_ref[half, :] = x1
        lg_ref[:, half] = _router_logits_t(x1, mn_ref[...], wr_ref[...])


def _sgu(x2d, a_norm, w_in, b_in, v_norm, w_s, b_s_t, w_out, m_norm, w_router):
    T = x2d.shape[0]
    return pl.pallas_call(
        _sgu_kernel,
        grid=(T // SGU_TILE,),
        in_specs=[
            pl.BlockSpec((SGU_TILE, D_MODEL), lambda i: (i, 0)),
            _const_spec((1, D_MODEL)),
            _const_spec((D_MODEL, 2 * SGU_WIDTH), single=True),
            _const_spec((1, 2 * SGU_WIDTH)),
            _const_spec((1, SGU_WIDTH)),
            _const_spec((SGU_GROUPS, SGU_BLOCK, SGU_BLOCK)),
            _const_spec((SGU_BLOCK, SGU_GROUPS)),
            _const_spec((SGU_WIDTH, D_MODEL), single=True),
            _const_spec((1, D_MODEL)),
            _const_spec((2 * D_MODEL, 2 * LANES)),
        ],
        out_specs=[
            pl.BlockSpec((SGU_TILE, D_MODEL), lambda i: (i, 0)),
            pl.BlockSpec((ROUTER_ROWS, SGU_TILE), lambda i: (0, i)),
        ],
        out_shape=[
            jax.ShapeDtypeStruct((T, D_MODEL), f32),
            jax.ShapeDtypeStruct((ROUTER_ROWS, T), f32),
        ],
        scratch_shapes=[
            pltpu.VMEM((SGU_TILE, SGU_WIDTH), f32),
            pltpu.VMEM((SGU_TILE, SGU_WIDTH), bf16),
            pltpu.VMEM((SGU_TILE, SGU_WIDTH), bf16),
            pltpu.VMEM((D_MODEL, 2 * SGU_WIDTH), bf16),
        ],
        compiler_params=_params(("arbitrary",)),
        name="sgu",
    )(x2d, a_norm, w_in, b_in, v_norm, w_s, b_s_t, w_out, m_norm, w_router)


def _route_kernel(lg_ref, be_ref, bg_ref, oi_ref, ow_ref, cnt_ref, carry_scr, before_scr,
                  w_scr):
    step = pl.program_id(0)
    tt = lg_ref.shape[1]

    @pl.when(step == 0)
    def _():
        carry_scr[...] = jnp.zeros_like(carry_scr)
        before_scr[...] = (lax.broadcasted_iota(jnp.int32, (tt, tt), 0)
                           < lax.broadcasted_iota(jnp.int32, (tt, tt), 1)).astype(bf16)

    lg = lg_ref[...]
    e_l = lg[0:N_EXPERTS, :] + be_ref[...]
    g_l = lg[N_EXPERTS:N_EXPERTS + 8, :] + bg_ref[...]
    g_row = lax.broadcasted_iota(jnp.int32, g_l.shape, 0).astype(f32)
    g_l = jnp.where(g_row < N_GROUPS, g_l, -jnp.inf)
    g_max = jnp.max(g_l, axis=0, keepdims=True)
    g_sel = jnp.min(jnp.where(g_l == g_max, g_row, 8.0), axis=0, keepdims=True)
    g_den = jnp.sum(jnp.exp(g_l - g_max), axis=0, keepdims=True)
    g_w = 1.0 / g_den

    e_row_i = lax.broadcasted_iota(jnp.int32, e_l.shape, 0)
    e_row = e_row_i.astype(f32)
    e_grp = (e_row_i >> GROUP_SHIFT).astype(f32)
    e_in = jnp.where(e_grp == g_sel, e_l, -jnp.inf)
    m1 = jnp.max(e_in, axis=0, keepdims=True)
    i1 = jnp.min(jnp.where(e_in == m1, e_row, float(N_EXPERTS)), axis=0, keepdims=True)
    e_in2 = jnp.where(e_row == i1, -jnp.inf, e_in)
    m2 = jnp.max(e_in2, axis=0, keepdims=True)
    i2 = jnp.min(jnp.where(e_in2 == m2, e_row, float(N_EXPERTS)), axis=0, keepdims=True)
    t = jnp.exp(m2 - m1)
    p1 = 1.0 / (1.0 + t)
    p2 = t / (1.0 + t)

    hit1 = e_row == i1
    hit2 = e_row == i2
    member = jnp.logical_or(hit1, hit2)
    prefix = _dot(member.astype(bf16), before_scr[...])
    rank_full = prefix + carry_scr[...]
    rank1 = jnp.sum(jnp.where(hit1, rank_full, 0.0), axis=0, keepdims=True)
    rank2 = jnp.sum(jnp.where(hit2, rank_full, 0.0), axis=0, keepdims=True)
    carry_new = carry_scr[...] + jnp.sum(member.astype(f32), axis=1, keepdims=True)
    carry_scr[...] = carry_new

    oi_ref[...] = jnp.zeros(oi_ref.shape, jnp.int32)
    oi_ref[0:1, :] = i1.astype(jnp.int32)
    oi_ref[1:2, :] = i2.astype(jnp.int32)
    oi_ref[2:3, :] = rank1.astype(jnp.int32)
    oi_ref[3:4, :] = rank2.astype(jnp.int32)
    w_scr[...] = jnp.zeros(w_scr.shape, f32)
    w_scr[0:1, :] = g_w * p1
    w_scr[1:2, :] = g_w * p2
    ow_ref[...] = w_scr[...].T
    cnt_ref[...] = jnp.broadcast_to(carry_new, cnt_ref.shape)


def _route(logits_t, b_expert, b_group):
    T = logits_t.shape[1]
    return pl.pallas_call(
        _route_kernel,
        grid=(T // ROUTE_TILE,),
        in_specs=[
            pl.BlockSpec((ROUTER_ROWS, ROUTE_TILE), lambda i: (0, i)),
            _const_spec((N_EXPERTS, 1)),
            _const_spec((8, 1)),
        ],
        out_specs=[
            pl.BlockSpec((8, ROUTE_TILE), lambda i: (0, i)),
            pl.BlockSpec((ROUTE_TILE, 8), lambda i: (i, 0)),
            _const_spec((N_EXPERTS, LANES)),
        ],
        out_shape=[
            jax.ShapeDtypeStruct((8, T), jnp.int32),
            jax.ShapeDtypeStruct((T, 8), f32),
            jax.ShapeDtypeStruct((N_EXPERTS, LANES), f32),
        ],
        scratch_shapes=[pltpu.VMEM((N_EXPERTS, 1), f32),
                        pltpu.VMEM((ROUTE_TILE, ROUTE_TILE), bf16),
                        pltpu.VMEM((8, ROUTE_TILE), f32)],
        compiler_params=_params(("arbitrary",)),
        name="route",
    )(logits_t, b_expert, b_group)


def _row_copy(src, s, dst, d, sem):
    return pltpu.make_async_copy(src.at[s], dst.at[d], sem)


def _wait_rows(buf, sem):
    pltpu.make_async_copy(buf, buf, sem).wait()


def _dispatch_kernel(n_blocks, pend_ref, padded_ref, nu_ref, dest_ref, x_ref, mn_ref, xs_hbm,
                     zero_scr, x3_scr, zsem, sem):
    @pl.when(pl.program_id(0) == 0)
    def _():
        zero_scr[...] = jnp.zeros_like(zero_scr)

        def fill(row0):
            return pltpu.make_async_copy(zero_scr, xs_hbm.at[pl.ds(row0, EXPERT_ROWS)], zsem)

        for e in range(N_EXPERTS):
            @pl.when(padded_ref[e] > 0)
            def _():
                fill(pend_ref[e] - EXPERT_ROWS).start()

            @pl.when(nu_ref[0] + e < n_blocks)
            def _():
                fill((nu_ref[0] + e) * EXPERT_ROWS).start()

        for e in range(N_EXPERTS):
            @pl.when(padded_ref[e] > 0)
            def _():
                fill(0).wait()

            @pl.when(nu_ref[0] + e < n_blocks)
            def _():
                fill(0).wait()

    i = pl.program_id(0)
    n_tiles = pl.num_programs(0) - 1

    def staged(rows):
        return _pack_rows(_rms(x_ref[rows, :], mn_ref[...]))

    @pl.when(i == 0)
    def _():
        x3_scr[0] = staged(slice(None))

    @pl.when(i > 0)
    def _():
        cur = (i - 1) % 3
        nxt = i % 3

        def chunk(c, carry):
            r0 = pl.multiple_of(c * MOVE_CHUNK, MOVE_CHUNK)
            for j in range(MOVE_CHUNK):
                for k in range(TOP_K):
                    _row_copy(x3_scr.at[cur], r0 + j, xs_hbm,
                              dest_ref[0, k * MOVE_TILE + r0 + j], sem.at[cur]).start(priority=k)
            x3_scr[nxt, pl.ds(r0, MOVE_CHUNK)] = staged(pl.ds(r0, MOVE_CHUNK))
            return carry

        lax.fori_loop(0, MOVE_TILE // MOVE_CHUNK, chunk, 0)

        @pl.when(i > 1)
        def _():
            for k in range(TOP_K):
                _wait_rows(x3_scr.at[(i - 2) % 3], sem.at[(i - 2) % 3])

        @pl.when(i == n_tiles)
        def _():
            for k in range(TOP_K):
                _wait_rows(x3_scr.at[cur], sem.at[cur])


def _dispatch(pend, padded, n_used, dest_tiles, x2d, m_norm, n_rows):
    T = x2d.shape[0]
    n_blocks = n_rows // EXPERT_ROWS
    n_tiles = T // MOVE_TILE
    grid_spec = pltpu.PrefetchScalarGridSpec(
        num_scalar_prefetch=3,
        grid=(n_tiles + 1,),
        in_specs=[
            pl.BlockSpec((None, 1, TOP_K * MOVE_TILE),
                         lambda i, *_: (jnp.maximum(i - 1, 0), 0, 0), memory_space=pltpu.SMEM),
            pl.BlockSpec((MOVE_TILE, D_MODEL), lambda i, *_: (jnp.minimum(i, n_tiles - 1), 0)),
            pl.BlockSpec((1, D_MODEL), lambda i, *_: (0, 0)),
        ],
        out_specs=pl.BlockSpec(memory_space=pl.ANY),
        scratch_shapes=[
            pltpu.VMEM((EXPERT_ROWS, ROW_SUBLANES, LANES), u32),
            pltpu.VMEM((3, MOVE_TILE, ROW_SUBLANES, LANES), u32),
            pltpu.SemaphoreType.DMA(()),
            pltpu.SemaphoreType.DMA((3,)),
        ],
    )
    return pl.pallas_call(
        functools.partial(_dispatch_kernel, n_blocks),
        grid_spec=grid_spec,
        out_shape=jax.ShapeDtypeStruct((n_rows, ROW_SUBLANES, LANES), u32),
        compiler_params=_params(("arbitrary",)),
        name="dispatch",
    )(pend, padded, n_used, dest_tiles, x2d, m_norm)


def _expert_kernel(n_blocks, ps_ref, nb_ref, nu_ref, xs_hbm, w1_ref, w3_ref, w2_ref,
                   ys_hbm, xbuf, ybuf, h_scr, w1_scr, w3_scr, w2_scr, xsem, ysem):
    e = pl.program_id(0)
    nblk = nb_ref[e]
    g0 = ps_ref[e]
    n_used = nu_ref[0]

    def rows(g):
        return pl.ds(pl.multiple_of(g * EXPERT_ROWS, EXPERT_ROWS), EXPERT_ROWS)

    def x_copy(g):
        return pltpu.make_async_copy(xs_hbm.at[rows(g)], xbuf.at[g % 2], xsem.at[g % 2])

    def y_copy(g):
        return pltpu.make_async_copy(ybuf.at[g % 2], ys_hbm.at[rows(g)], ysem.at[g % 2])

    def normed(slot):
        return _unpack_rows(xbuf[slot]).astype(bf16)

    @pl.when(e == 0)
    def _():
        x_copy(0).start(priority=1)
        x_copy(1).start(priority=1)
        x_copy(0).wait()
        h_scr[...] = normed(0)

    @pl.when(nblk > 0)
    def _():
        w1_scr[...] = w1_ref[...].astype(bf16)
        w3_scr[...] = w3_ref[...].astype(bf16)
        w2_scr[...] = w2_ref[...].astype(bf16)

        def body(g, h):
            @pl.when(g + 2 < n_used)
            def _():
                x_copy(g + 2).start(priority=1)

            @pl.when(g + 1 < n_used)
            def _():
                x_copy(g + 1).wait()

            @pl.when(g >= 2)
            def _():
                y_copy(g - 2).wait()

            h_next = normed((g + 1) % 2)
            y = None
            for c0 in range(0, D_EXPERT, D_EXPERT // 2):
                cols = slice(c0, c0 + D_EXPERT // 2)
                a = _dot(h, w1_scr[:, cols])
                c = _dot(h, w3_scr[:, cols])
                hid = ((a * jax.nn.sigmoid(a)) * c).astype(bf16)
                part = _dot(hid, w2_scr[cols, :])
                y = part if y is None else y + part
            ybuf[g % 2] = _pack_rows(y)
            y_copy(g).start(priority=1)
            return h_next

        h_scr[...] = lax.fori_loop(g0, g0 + nblk, body, h_scr[...])

    @pl.when(e == pl.num_programs(0) - 1)
    def _():
        y_copy(n_used - 2).wait()
        y_copy(n_used - 1).wait()
        ybuf[0] = jnp.zeros(ybuf.shape[1:], u32)

        def fill(b):
            r = pl.ds(pl.multiple_of(b * EXPERT_ROWS, EXPERT_ROWS), EXPERT_ROWS)
            return pltpu.make_async_copy(ybuf.at[0], ys_hbm.at[r], ysem.at[0])

        def start(b, c):
            fill(b).start()
            return c

        def wait(b, c):
            fill(b).wait()
            return c

        lax.fori_loop(nu_ref[0], n_blocks, start, 0)
        lax.fori_loop(nu_ref[0], n_blocks, wait, 0)


def _experts(layer, pstart, seg_blocks, n_used, xs, w1, w3, w2):
    n_rows = xs.shape[0]
    n_blocks = n_rows // EXPERT_ROWS

    def w_map(e, *_):
        return (layer, e, 0, 0)

    block_buf = pltpu.VMEM((2, EXPERT_ROWS, ROW_SUBLANES, LANES), u32)
    grid_spec = pltpu.PrefetchScalarGridSpec(
        num_scalar_prefetch=3,
        grid=(N_EXPERTS,),
        in_specs=[
            pl.BlockSpec(memory_space=pl.ANY),
            pl.BlockSpec((None, None, D_MODEL, D_EXPERT), w_map),
            pl.BlockSpec((None, None, D_MODEL, D_EXPERT), w_map),
            pl.BlockSpec((None, None, D_EXPERT, D_MODEL), w_map),
        ],
        out_specs=pl.BlockSpec(memory_space=pl.ANY),
        scratch_shapes=[
            block_buf,
            block_buf,
            pltpu.VMEM((EXPERT_ROWS, D_MODEL), bf16),
            pltpu.VMEM((D_MODEL, D_EXPERT), bf16),
            pltpu.VMEM((D_MODEL, D_EXPERT), bf16),
            pltpu.VMEM((D_EXPERT, D_MODEL), bf16),
            pltpu.SemaphoreType.DMA((2,)),
            pltpu.SemaphoreType.DMA((2,)),
        ],
    )
    return pl.pallas_call(
        functools.partial(_expert_kernel, n_blocks),
        grid_spec=grid_spec,
        out_shape=jax.ShapeDtypeStruct((n_rows, ROW_SUBLANES, LANES), u32),
        compiler_params=_params(("arbitrary",)),
        name="experts",
    )(pstart, seg_blocks, n_used, xs, w1, w3, w2)


def _gather_rows(ys, dest):
    halves = ROW_SUBLANES * LANES // GATHER_WIDTH
    n_idx = dest.size * halves
    src = ys.reshape(ys.shape[0] * halves, GATHER_WIDTH)
    idx = (dest.reshape(-1, 1) * halves + jnp.arange(halves, dtype=jnp.int32)).reshape(1, n_idx)
    mesh = plsc.VectorSubcoreMesh(core_axis_name="core", subcore_axis_name="subcore")

    @pl.kernel(out_type=jax.ShapeDtypeStruct((n_idx, GATHER_WIDTH), u32), mesh=mesh)
    def gather(src_hbm, idx_hbm, out_hbm):
        def body(idx_vmem, out_vmem):
            pltpu.sync_copy(src_hbm.at[idx_vmem.at[0]], out_vmem)

        pltpu.emit_pipeline(
            body,
            grid=(n_idx // GATHER_WINDOW,),
            in_specs=[pl.BlockSpec((1, GATHER_WINDOW), index_map=lambda i: (0, i))],
            out_specs=[pl.BlockSpec((GATHER_WINDOW, GATHER_WIDTH), index_map=lambda i: (i, 0))],
            core_axis_name=("core", "subcore"),
            dimension_semantics=(pltpu.PARALLEL,),
        )(idx_hbm, out_hbm)

    return gather(src, idx).reshape(dest.shape + (ROW_SUBLANES, LANES))


def _combine_kernel(x_ref, w_ref, y0_ref, y1_ref, out_ref):
    w = w_ref[...]
    y0 = _unpack_rows(y0_ref[...])
    y1 = _unpack_rows(y1_ref[...])
    out_ref[...] = x_ref[...] + (w[:, 0:1] * y0 + w[:, 1:2] * y1)


def _combine(dest, x2d, w_cols, ys):
    T = x2d.shape[0]
    n_tiles = T // MOVE_TILE
    yg = _gather_rows(ys, dest)
    return pl.pallas_call(
        _combine_kernel,
        grid=(n_tiles,),
        in_specs=[
            pl.BlockSpec((MOVE_TILE, D_MODEL), lambda i: (i, 0)),
            pl.BlockSpec((MOVE_TILE, 8), lambda i: (i, 0)),
            pl.BlockSpec((None, MOVE_TILE, ROW_SUBLANES, LANES), lambda i: (0, i, 0, 0)),
            pl.BlockSpec((None, MOVE_TILE, ROW_SUBLANES, LANES), lambda i: (1, i, 0, 0)),
        ],
        out_specs=pl.BlockSpec((MOVE_TILE, D_MODEL), lambda i: (i, 0)),
        out_shape=jax.ShapeDtypeStruct((T, D_MODEL), f32),
        compiler_params=_params(("parallel",)),
        name="combine",
    )(x2d, w_cols, yg, yg)


def _moe(layer, x2d, logits_t, m_norm, b_group, b_expert, w1, w3, w2):
    T = x2d.shape[0]
    n_blocks = (T * TOP_K) // EXPERT_ROWS + N_EXPERTS
    n_rows = n_blocks * EXPERT_ROWS

    be = b_expert.reshape(N_EXPERTS, 1)
    bg = jnp.concatenate([b_group, jnp.zeros((8 - N_GROUPS,), f32)]).reshape(8, 1)
    oi, w_cols, cnt = _route(logits_t, be, bg)

    counts = cnt[:, 0].astype(jnp.int32)
    padded = (counts + EXPERT_ROWS - 1) // EXPERT_ROWS * EXPERT_ROWS
    pend = jnp.cumsum(padded)
    pstart = pend - padded
    n_used = (pend[-1] // EXPERT_ROWS).astype(jnp.int32).reshape(1)
    e_ids = jnp.arange(N_EXPERTS, dtype=jnp.int32)[:, None, None]
    seg_start = jnp.sum(jnp.where(oi[None, 0:2] == e_ids, pstart[:, None, None], 0), axis=0)
    dest = seg_start + oi[2:4]
    dest_tiles = dest.reshape(TOP_K, T // MOVE_TILE, MOVE_TILE).transpose(1, 0, 2).reshape(
        T // MOVE_TILE, 1, TOP_K * MOVE_TILE)

    xs = _dispatch(pend.astype(jnp.int32), padded, n_used, dest_tiles, x2d, m_norm, n_rows)
    ys = _experts(layer, (pstart // EXPERT_ROWS).astype(jnp.int32), padded // EXPERT_ROWS,
                  n_used, xs, w1, w3, w2)
    return _combine(dest, x2d, w_cols, ys)


def _log_sigmoid(x):
    return jnp.minimum(x, 0.0) - jnp.log1p(jnp.exp(-jnp.abs(x)))


def _proj_kernel(tiles_per_seq, x_ref, kvn_ref, bn_ref, kvw_ref, bf_ref, qgw_ref, gq_ref, gk_ref,
                 q_ref, kt_ref, v_ref, sg_ref, carry_scr, wkt_ref, wv_ref, wf_ref, wq_ref, wg_ref):
    i = pl.program_id(0)

    @pl.when(i == 0)
    def _():
        wkt_ref[...] = kvw_ref[:D_MODEL, :].astype(bf16)
        wv_ref[...] = kvw_ref[D_MODEL:2 * D_MODEL, :].T.astype(bf16)
        wf_ref[...] = jnp.concatenate(
            [kvw_ref[2 * D_MODEL:, :].T, jnp.zeros((D_MODEL, LANES - N_HEADS), f32)],
            axis=1).astype(bf16)
        wg_ref[...] = qgw_ref[:, D_MODEL:].astype(bf16)
        zeros = jnp.zeros((D_MODEL, HEAD_PAD - HEAD_DIM), f32)
        for h in range(N_HEADS):
            wq_ref[:, h * HEAD_PAD:(h + 1) * HEAD_PAD] = jnp.concatenate(
                [qgw_ref[:, h * HEAD_DIM:(h + 1) * HEAD_DIM], zeros], axis=1).astype(bf16)

    x = x_ref[...]
    hkv = _rms(x, kvn_ref[...]).astype(bf16)
    hq = _rms(x, bn_ref[...]).astype(bf16)

    v_ref[...] = _dot(hkv, wv_ref[...]).astype(bf16)
    sg_ref[...] = jax.nn.sigmoid(_dot(hq, wg_ref[...])).astype(bf16)

    logf = _log_sigmoid(_dot(hkv, wf_ref[...]) + bf_ref[...])
    tm = x.shape[0]
    incl = (lax.broadcasted_iota(jnp.int32, (tm, tm), 0)
            >= lax.broadcasted_iota(jnp.int32, (tm, tm), 1)).astype(bf16)

    @pl.when(i % tiles_per_seq == 0)
    def _():
        carry_scr[...] = jnp.zeros_like(carry_scr)

    parts = _dot(incl, jnp.concatenate(_split3(logf), axis=1).astype(bf16))
    cum = (parts[:, :LANES] + parts[:, LANES:2 * LANES] + parts[:, 2 * LANES:]
           + carry_scr[...])
    carry_scr[...] = cum[tm - 1:tm, :]
    cum = cum * LOG2E
    cum_t = cum.T

    lane = lax.broadcasted_iota(jnp.int32, (tm, HEAD_PAD), 1)
    sub = lax.broadcasted_iota(jnp.int32, (HEAD_DIM, tm), 0)
    q_raw = _dot(hq, wq_ref[...])
    k_raw_t = _dot_nt(wkt_ref[...], hkv)
    scale = HEAD_DIM ** -0.5 * LOG2E

    for h in range(N_HEADS):
        sl = slice(h * HEAD_PAD, (h + 1) * HEAD_PAD)
        qb = q_raw[:, sl]
        q_ms = jnp.sum(qb * qb, axis=-1, keepdims=True) * (1.0 / HEAD_DIM)
        qn = qb * lax.rsqrt(q_ms + EPS) * gq_ref[...] * scale
        c_hi, c_mid, c_lo = _split3(cum[:, h:h + 1])
        qa = jnp.where(lane < HEAD_DIM, qn,
             jnp.where(lane < AUX_CUM0, 1.0,
             jnp.where(lane == AUX_CUM0, c_hi,
             jnp.where(lane == AUX_CUM0 + 1, c_mid,
             jnp.where(lane == AUX_CUM0 + 2, c_lo, 0.0)))))
        q_ref[:, sl] = qa.astype(bf16)

        kb = k_raw_t[h * HEAD_DIM:(h + 1) * HEAD_DIM, :]
        k_ms = jnp.sum(kb * kb, axis=0, keepdims=True) * (1.0 / HEAD_DIM)
        kn = kb * lax.rsqrt(k_ms + EPS) * gk_ref[...]
        t_hi, t_mid, t_lo = _split3(cum_t[h:h + 1, :])
        aux = jnp.where(sub == 0, -t_hi,
              jnp.where(sub == 1, -t_mid,
              jnp.where(sub == 2, -t_lo,
              jnp.where(sub < 6, 1.0, 0.0))))
        sl = slice(h * HEAD_PAD, (h + 1) * HEAD_PAD)
        ka = jnp.concatenate([kn, aux], axis=0).astype(bf16)
        for t in range(tm // ATT_K):
            kt_ref[t, sl, :] = ka[:, t * ATT_K:(t + 1) * ATT_K]


def _proj(x2d, seq, kv_norm, b_norm, kv_w, bfv, w_qg, gq, gk):
    T = x2d.shape[0]
    n_tiles = T // ROW_TILE
    qw = N_HEADS * HEAD_PAD
    return pl.pallas_call(
        functools.partial(_proj_kernel, seq // ROW_TILE),
        grid=(n_tiles,),
        in_specs=[
            pl.BlockSpec((ROW_TILE, D_MODEL), lambda i: (i, 0)),
            _const_spec((1, D_MODEL)),
            _const_spec((1, D_MODEL)),
            _const_spec(kv_w.shape, single=True),
            _const_spec((1, LANES)),
            _const_spec(w_qg.shape, single=True),
            _const_spec((1, HEAD_PAD)),
            _const_spec((HEAD_DIM, 1)),
        ],
        out_specs=[
            pl.BlockSpec((ROW_TILE, qw), lambda i: (i, 0)),
            pl.BlockSpec((ROW_TILE // ATT_K, qw, ATT_K), lambda i: (i, 0, 0)),
            pl.BlockSpec((ROW_TILE, D_MODEL), lambda i: (i, 0)),
            pl.BlockSpec((ROW_TILE, D_MODEL), lambda i: (i, 0)),
        ],
        out_shape=[
            jax.ShapeDtypeStruct((T, qw), bf16),
            jax.ShapeDtypeStruct((T // ATT_K, qw, ATT_K), bf16),
            jax.ShapeDtypeStruct((T, D_MODEL), bf16),
            jax.ShapeDtypeStruct((T, D_MODEL), bf16),
        ],
        scratch_shapes=[
            pltpu.VMEM((1, LANES), f32),
            pltpu.VMEM((D_MODEL, D_MODEL), bf16),
            pltpu.VMEM((D_MODEL, D_MODEL), bf16),
            pltpu.VMEM((D_MODEL, LANES), bf16),
            pltpu.VMEM((D_MODEL, qw), bf16),
            pltpu.VMEM((D_MODEL, D_MODEL), bf16),
        ],
        compiler_params=_params(("arbitrary",)),
        name="proj",
    )(x2d, kv_norm, b_norm, kv_w, bfv, w_qg, gq, gk)


def _attn_kernel(q_ref, kt_ref, v_ref, o_ref, s_scr, m_scr, l_scr, acc_scr):
    qi = pl.program_id(2)
    tiles = ATT_Q // ATT_K
    groups = ATT_K // LANES
    heads = [slice(hh * HEAD_PAD, (hh + 1) * HEAD_PAD) for hh in range(2)]
    m_scr[...] = jnp.full(m_scr.shape, -jnp.inf, f32)

    def lane_max(mx, s):
        for g in range(groups):
            mx = jnp.maximum(mx, s[:, g * LANES:(g + 1) * LANES])
        return mx

    def probs(s, mb):
        ps = [jnp.exp2(s[:, g * LANES:(g + 1) * LANES] - mb) for g in range(groups)]
        return ps, functools.reduce(lambda a, b: a + b, ps)

    def body_a(trip, c):
        for t in range(tiles):
            kt_idx = trip * tiles + t
            for hh in range(2):
                s = _dot(q_ref[:, heads[hh]], kt_ref[kt_idx, heads[hh], :])
                s_scr[hh, kt_idx] = s
                m_scr[hh] = lane_max(m_scr[hh], s)
        return c

    lax.fori_loop(0, qi, body_a, 0)
    for t in range(tiles):
        kt_idx = qi * tiles + t
        rows = ATT_Q - t * ATT_K
        visible = (lax.broadcasted_iota(jnp.int32, (rows, ATT_K), 1)
                   <= lax.broadcasted_iota(jnp.int32, (rows, ATT_K), 0))
        for hh in range(2):
            s = _dot(q_ref[t * ATT_K:, heads[hh]], kt_ref[kt_idx, heads[hh], :])
            s = jnp.where(visible, s, -jnp.inf)
            s_scr[hh, kt_idx, t * ATT_K:, :] = s
            m_scr[hh, t * ATT_K:, :] = lane_max(m_scr[hh, t * ATT_K:, :], s)

    for hh in range(2):
        row_max = jnp.max(m_scr[hh], axis=-1, keepdims=True)
        m_scr[hh] = jnp.broadcast_to(row_max, (ATT_Q, LANES))
    l_scr[...] = jnp.zeros(l_scr.shape, f32)
    acc_scr[...] = jnp.zeros(acc_scr.shape, f32)

    def body_b(trip, c):
        for hh in range(2):
            mb = m_scr[hh]
            lsum, acc = l_scr[hh], acc_scr[hh]
            for t0 in range(0, tiles, PV_TILES):
                kt_idx = trip * tiles + t0
                row0 = pl.multiple_of(kt_idx * ATT_K, PV_TILES * ATT_K)
                vb = v_ref[pl.ds(row0, PV_TILES * ATT_K), :]
                parts = []
                for t in range(PV_TILES):
                    ps, psum = probs(s_scr[hh, kt_idx + t], mb)
                    parts += ps
                    lsum = lsum + psum
                acc = acc + _dot(jnp.concatenate(parts, axis=1).astype(bf16), vb)
            l_scr[hh] = lsum
            acc_scr[hh] = acc
        return c

    lax.fori_loop(0, qi, body_b, 0)
    for t in range(tiles):
        kt_idx = qi * tiles + t
        row0 = pl.multiple_of(kt_idx * ATT_K, ATT_K)
        vb = v_ref[pl.ds(row0, ATT_K), :]
        for hh in range(2):
            ps, psum = probs(s_scr[hh, kt_idx, t * ATT_K:, :], m_scr[hh, t * ATT_K:, :])
            l_scr[hh, t * ATT_K:, :] += psum
            acc_scr[hh, t * ATT_K:, :] += _dot(jnp.concatenate(ps, axis=1).astype(bf16), vb)

    lane = lax.broadcasted_iota(jnp.int32, (ATT_Q, 2 * HEAD_DIM), 1)
    o0 = acc_scr[0] / jnp.sum(l_scr[0], axis=-1, keepdims=True)
    o1 = acc_scr[1] / jnp.sum(l_scr[1], axis=-1, keepdims=True)
    o_ref[...] = jnp.where(lane < HEAD_DIM, o0, o1).astype(bf16)


def _attention(q, kt, v, batch, seq):
    T = q.shape[0]
    nq = seq // ATT_Q
    nk = seq // ATT_K
    pairs = N_HEADS // 2
    return pl.pallas_call(
        _attn_kernel,
        grid=(batch, pairs, nq),
        in_specs=[
            pl.BlockSpec((ATT_Q, 2 * HEAD_PAD), lambda b, p, i: (b * nq + i, p)),
            pl.BlockSpec((nk, 2 * HEAD_PAD, ATT_K), lambda b, p, i: (b, p, 0)),
            pl.BlockSpec((seq, 2 * HEAD_DIM), lambda b, p, i: (b, p)),
        ],
        out_specs=pl.BlockSpec((ATT_Q, 2 * HEAD_DIM), lambda b, p, i: (b * nq + i, p)),
        out_shape=jax.ShapeDtypeStruct((T, D_MODEL), bf16),
        scratch_shapes=[
            pltpu.VMEM((2, nk, ATT_Q, ATT_K), f32),
            pltpu.VMEM((2, ATT_Q, LANES), f32),
            pltpu.VMEM((2, ATT_Q, LANES), f32),
            pltpu.VMEM((2, ATT_Q, 2 * HEAD_DIM), f32),
        ],
        compiler_params=_params(("parallel", "parallel", "parallel")),
        name="attn",
    )(q, kt, v)


def _attn_out_kernel(x_ref, o_ref, sg_ref, wo_ref, mn_ref, wr_ref, x3_ref, lg_ref):
    for r0 in range(0, SGU_TILE, EPILOGUE_ROWS):
        rows = slice(r0, r0 + EPILOGUE_ROWS)
        gated = (o_ref[rows, :].astype(f32) * sg_ref[rows, :].astype(f32)).astype(bf16)
        x3 = x_ref[rows, :] + _dot(gated, wo_ref[...])
        x3_ref[rows, :] = x3
        lg_ref[:, rows] = _router_logits_t(x3, mn_ref[...], wr_ref[...])


def _attn_out(x2d, o, sg, wo, m_norm, w_router):
    T = x2d.shape[0]
    return pl.pallas_call(
        _attn_out_kernel,
        grid=(T // SGU_TILE,),
        in_specs=[
            pl.BlockSpec((SGU_TILE, D_MODEL), lambda i: (i, 0)),
            pl.BlockSpec((SGU_TILE, D_MODEL), lambda i: (i, 0)),
            pl.BlockSpec((SGU_TILE, D_MODEL), lambda i: (i, 0)),
            _const_spec((D_MODEL, D_MODEL)),
            _const_spec((1, D_MODEL)),
            _const_spec((2 * D_MODEL, 2 * LANES)),
        ],
        out_specs=[
            pl.BlockSpec((SGU_TILE, D_MODEL), lambda i: (i, 0)),
            pl.BlockSpec((ROUTER_ROWS, SGU_TILE), lambda i: (0, i)),
        ],
        out_shape=[
            jax.ShapeDtypeStruct((T, D_MODEL), f32),
            jax.ShapeDtypeStruct((ROUTER_ROWS, T), f32),
        ],
        compiler_params=_params(("parallel",)),
        name="attn_out",
    )(x2d, o, sg, wo, m_norm, w_router)


def _router_weight(w_group, w_expert):
    pad = jnp.zeros((D_MODEL, LANES - N_EXPERTS - N_GROUPS), f32)
    w = jnp.concatenate([w_expert, w_group, pad], axis=1)
    w_hi = w.astype(bf16)
    w_lo = (w - w_hi.astype(f32)).astype(bf16)
    top = jnp.concatenate([w_hi, w_lo], axis=1)
    bottom = jnp.concatenate([w_hi, jnp.zeros_like(w_hi)], axis=1)
    return jnp.concatenate([top, bottom], axis=0)


def kernel(x, a_norm, a_w_in, a_b_in, a_v_norm, a_w_s, a_b_s, a_w_out, kv_norm, kv_w, kv_b_f,
           k_norm, b_norm, b_w_qg, q_norm, b_w_out, m_norm, m_w_group, m_b_group, m_w_expert,
           m_b_expert, m_w1, m_w3, m_w2):
    batch, seq, _ = x.shape
    T = batch * seq
    x2d = x.reshape(T, D_MODEL)

    x1, lg0 = _sgu(
        x2d, a_norm[0].reshape(1, -1), a_w_in[0], a_b_in[0].reshape(1, -1),
        a_v_norm[0].reshape(1, -1), a_w_s[0], a_b_s[0].T, a_w_out[0].astype(bf16),
        m_norm[0].reshape(1, -1), _router_weight(m_w_group[0], m_w_expert[0]))
    x2 = _moe(0, x1, lg0, m_norm[0].reshape(1, -1), m_b_group[0], m_b_expert[0],
              m_w1, m_w3, m_w2)

    bfv = jnp.pad(kv_b_f, (0, LANES - N_HEADS)).reshape(1, LANES)
    gq = jnp.pad(q_norm[0], (0, HEAD_PAD - HEAD_DIM)).reshape(1, HEAD_PAD)
    gk = k_norm.reshape(HEAD_DIM, 1)
    q, kt, v, sg = _proj(x2, seq, kv_norm.reshape(1, -1), b_norm[0].reshape(1, -1),
                         kv_w.T, bfv, b_w_qg[0], gq, gk)
    o = _attention(q, kt, v, batch, seq)
    x3, lg1 = _attn_out(x2, o, sg, b_w_out[0].astype(bf16), m_norm[1].reshape(1, -1),
                        _router_weight(m_w_group[1], m_w_expert[1]))
    x4 = _moe(1, x3, lg1, m_norm[1].reshape(1, -1), m_b_group[1], m_b_expert[1],
              m_w1, m_w3, m_w2)
    return x4.reshape(batch, seq, D_MODEL)
```

```python
import functools
import math

import jax
import jax.numpy as jnp
import numpy as np
from jax import lax
from jax.experimental import pallas as pl
from jax.experimental.pallas import tpu as pltpu
from jax.experimental.pallas import tpu_sc as plsc

D_MODEL = 1024
EPS = 1e-6
SGU_BLOCK = 128
SGU_CHUNK = 64
CHUNK_SHIFT = SGU_CHUNK.bit_length() - 1
SGU_WIDTH = 2 * D_MODEL
SGU_GROUPS = 8
SGU_GROUP_DIM = SGU_WIDTH // SGU_GROUPS
N_HEADS = 16
HEAD_DIM = D_MODEL // N_HEADS
N_GROUPS = 4
EXPERTS_PER_GROUP = 8
N_EXPERTS = N_GROUPS * EXPERTS_PER_GROUP
GROUP_SHIFT = EXPERTS_PER_GROUP.bit_length() - 1
TOP_K = 2
D_EXPERT = D_MODEL // 2

LANES = 128
ROW_SUBLANES = D_MODEL // 2 // LANES
ISSUE_UNROLL = 8
VMEM_LIMIT_BYTES = 56 * 1024 * 1024

ROW_TILE = 256
SGU_TILE = 512
EPILOGUE_ROWS = 256
ROUTE_TILE = 1024
EXPERT_ROWS = 512
MOVE_TILE = 512
MOVE_CHUNK = 64
GATHER_WINDOW = 128
GATHER_WIDTH = 256
ATT_Q = 1024
ATT_K = 256
PV_TILES = 2
HEAD_PAD = 2 * HEAD_DIM
ROUTER_ROWS = 40
AUX_ONE0 = HEAD_DIM
AUX_CUM0 = HEAD_DIM + 3

LOG2E = math.log2(math.e)

bf16 = jnp.bfloat16
f32 = jnp.float32
u32 = jnp.uint32
HI_HALF = np.uint32(0xFFFF0000)


def _dot(a, b, precision=None):
    return jnp.dot(a, b, preferred_element_type=f32, precision=precision)


def _dot_nt(a, b, precision=None):
    return lax.dot_general(a, b, (((1,), (1,)), ((), ())), preferred_element_type=f32,
                           precision=precision)


def _rms(x, g):
    ms = jnp.mean(x * x, axis=-1, keepdims=True)
    return x * lax.rsqrt(ms + EPS) * g


def _gelu_tanh(z):
    c = math.sqrt(2.0 / math.pi)
    return z * (0.5 * (1.0 + jnp.tanh(c * (z + 0.044715 * (z * z * z)))))


def _split3(c):
    hi = c.astype(bf16).astype(f32)
    r = c - hi
    mid = r.astype(bf16).astype(f32)
    lo = r - mid
    return hi, mid, lo


def _pack_words(v):
    half = D_MODEL // 2
    lo = lax.bitcast_convert_type(v[:, :half].astype(bf16).astype(f32), u32) >> 16
    hi = lax.bitcast_convert_type(v[:, half:].astype(bf16).astype(f32), u32) & HI_HALF
    return hi | lo


def _unpack_words(w):
    lo = lax.bitcast_convert_type(w << 16, f32)
    hi = lax.bitcast_convert_type(w & HI_HALF, f32)
    return jnp.concatenate([lo, hi], axis=1)


def _pack_rows(v):
    return _pack_words(v).reshape(v.shape[0], ROW_SUBLANES, LANES)


def _unpack_rows(w):
    return _unpack_words(w.reshape(w.shape[0], D_MODEL // 2))


def _const_spec(shape, single=False):
    nd = len(shape)
    mode = pl.Buffered(1) if single else None
    return pl.BlockSpec(shape, lambda *_: (0,) * nd, pipeline_mode=mode)


def _params(sem):
    return pltpu.CompilerParams(dimension_semantics=sem, vmem_limit_bytes=VMEM_LIMIT_BYTES)


def _router_logits_t(x_new, mn, wr3):
    hm = _rms(x_new, mn)
    h_hi = hm.astype(bf16)
    h_lo = (hm - h_hi.astype(f32)).astype(bf16)
    both = _dot(jnp.concatenate([h_hi, h_lo], axis=1), wr3)
    lg = both[:, :LANES] + both[:, LANES:]
    return lg.T[:ROUTER_ROWS, :]


def _sgu_kernel(x_ref, an_ref, win_f32_ref, bin_ref, vn_ref, ws_ref, bst_ref, wout_ref, mn_ref,
                wr_ref, x1_ref, lg_ref, u_scr, v_scr, gated_scr, win_ref):
    @pl.when(pl.program_id(0) == 0)
    def _():
        win_ref[...] = win_f32_ref[...].astype(bf16)

    x = x_ref[...]
    h = _rms(x, an_ref[...]).astype(bf16)
    v = _gelu_tanh(_dot(h, win_ref[:, SGU_WIDTH:]) + bin_ref[:, SGU_WIDTH:])
    v_scr[...] = _rms(v, vn_ref[...]).astype(bf16)
    u_scr[...] = _gelu_tanh(_dot(h, win_ref[:, :SGU_WIDTH]) + bin_ref[:, :SGU_WIDTH])

    t_chunk = lax.broadcasted_iota(jnp.int32, (SGU_BLOCK, SGU_BLOCK), 0) >> CHUNK_SHIFT
    s_chunk = lax.broadcasted_iota(jnp.int32, (SGU_BLOCK, SGU_BLOCK), 1) >> CHUNK_SHIFT
    causal = t_chunk >= s_chunk
    wsm = [jnp.where(causal, ws_ref[g], 0.0).astype(bf16) for g in range(SGU_GROUPS)]
    for r0 in range(0, SGU_TILE, EPILOGUE_ROWS):
        for g in range(SGU_GROUPS):
            cols = slice(g * SGU_GROUP_DIM, (g + 1) * SGU_GROUP_DIM)
            for sb in range(r0 // SGU_BLOCK, (r0 + EPILOGUE_ROWS) // SGU_BLOCK):
                rows = slice(sb * SGU_BLOCK, (sb + 1) * SGU_BLOCK)
                mixed = _dot(wsm[g], v_scr[rows, cols]) + bst_ref[:, g:g + 1]
                gated_scr[rows, cols] = (u_scr[rows, cols] * mixed).astype(bf16)
        half = slice(r0, r0 + EPILOGUE_ROWS)
        x1 = x[half, :] + _dot(gated_scr[half, :], wout_ref[...])
        x1_ref[half, :] = x1
        lg_ref[:, half] = _router_logits_t(x1, mn_ref[...], wr_ref[...])


def _sgu(x2d, a_norm, w_in, b_in, v_norm, w_s, b_s_t, w_out, m_norm, w_router):
    T = x2d.shape[0]
    return pl.pallas_call(
        _sgu_kernel,
        grid=(T // SGU_TILE,),
        in_specs=[
            pl.BlockSpec((SGU_TILE, D_MODEL), lambda i: (i, 0)),
            _const_spec((1, D_MODEL)),
            _const_spec((D_MODEL, 2 * SGU_WIDTH), single=True),
            _const_spec((1, 2 * SGU_WIDTH)),
            _const_spec((1, SGU_WIDTH)),
            _const_spec((SGU_GROUPS, SGU_BLOCK, SGU_BLOCK)),
            _const_spec((SGU_BLOCK, SGU_GROUPS)),
            _const_spec((SGU_WIDTH, D_MODEL), single=True),
            _const_spec((1, D_MODEL)),
            _const_spec((2 * D_MODEL, 2 * LANES)),
        ],
        out_specs=[
            pl.BlockSpec((SGU_TILE, D_MODEL), lambda i: (i, 0)),
            pl.BlockSpec((ROUTER_ROWS, SGU_TILE), lambda i: (0, i)),
        ],
        out_shape=[
            jax.ShapeDtypeStruct((T, D_MODEL), f32),
            jax.ShapeDtypeStruct((ROUTER_ROWS, T), f32),
        ],
        scratch_shapes=[
            pltpu.VMEM((SGU_TILE, SGU_WIDTH), f32),
            pltpu.VMEM((SGU_TILE, SGU_WIDTH), bf16),
            pltpu.VMEM((SGU_TILE, SGU_WIDTH), bf16),
            pltpu.VMEM((D_MODEL, 2 * SGU_WIDTH), bf16),
        ],
        compiler_params=_params(("arbitrary",)),
        name="sgu",
    )(x2d, a_norm, w_in, b_in, v_norm, w_s, b_s_t, w_out, m_norm, w_router)


def _route_kernel(lg_ref, be_ref, bg_ref, oi_ref, ow_ref, cnt_ref, carry_scr, before_scr,
                  w_scr):
    step = pl.program_id(0)
    tt = lg_ref.shape[1]

    @pl.when(step == 0)
    def _():
        carry_scr[...] = jnp.zeros_like(carry_scr)
        before_scr[...] = (lax.broadcasted_iota(jnp.int32, (tt, tt), 0)
                           < lax.broadcasted_iota(jnp.int32, (tt, tt), 1)).astype(bf16)

    lg = lg_ref[...]
    e_l = lg[0:N_EXPERTS, :] + be_ref[...]
    g_l = lg[N_EXPERTS:N_EXPERTS + 8, :] + bg_ref[...]
    g_row = lax.broadcasted_iota(jnp.int32, g_l.shape, 0).astype(f32)
    g_l = jnp.where(g_row < N_GROUPS, g_l, -jnp.inf)
    g_max = jnp.max(g_l, axis=0, keepdims=True)
    g_sel = jnp.min(jnp.where(g_l == g_max, g_row, 8.0), axis=0, keepdims=True)
    g_den = jnp.sum(jnp.exp(g_l - g_max), axis=0, keepdims=True)
    g_w = 1.0 / g_den

    e_row_i = lax.broadcasted_iota(jnp.int32, e_l.shape, 0)
    e_row = e_row_i.astype(f32)
    e_grp = (e_row_i >> GROUP_SHIFT).astype(f32)
    e_in = jnp.where(e_grp == g_sel, e_l, -jnp.inf)
    m1 = jnp.max(e_in, axis=0, keepdims=True)
    i1 = jnp.min(jnp.where(e_in == m1, e_row, float(N_EXPERTS)), axis=0, keepdims=True)
    e_in2 = jnp.where(e_row == i1, -jnp.inf, e_in)
    m2 = jnp.max(e_in2, axis=0, keepdims=True)
    i2 = jnp.min(jnp.where(e_in2 == m2, e_row, float(N_EXPERTS)), axis=0, keepdims=True)
    t = jnp.exp(m2 - m1)
    p1 = 1.0 / (1.0 + t)
    p2 = t / (1.0 + t)

    hit1 = e_row == i1
    hit2 = e_row == i2
    member = jnp.logical_or(hit1, hit2)
    prefix = _dot(member.astype(bf16), before_scr[...])
    rank_full = prefix + carry_scr[...]
    rank1 = jnp.sum(jnp.where(hit1, rank_full, 0.0), axis=0, keepdims=True)
    rank2 = jnp.sum(jnp.where(hit2, rank_full, 0.0), axis=0, keepdims=True)
    carry_new = carry_scr[...] + jnp.sum(member.astype(f32), axis=1, keepdims=True)
    carry_scr[...] = carry_new

    oi_ref[...] = jnp.zeros(oi_ref.shape, jnp.int32)
    oi_ref[0:1, :] = i1.astype(jnp.int32)
    oi_ref[1:2, :] = i2.astype(jnp.int32)
    oi_ref[2:3, :] = rank1.astype(jnp.int32)
    oi_ref[3:4, :] = rank2.astype(jnp.int32)
    w_scr[...] = jnp.zeros(w_scr.shape, f32)
    w_scr[0:1, :] = g_w * p1
    w_scr[1:2, :] = g_w * p2
    ow_ref[...] = w_scr[...].T
    cnt_ref[...] = jnp.broadcast_to(carry_new, cnt_ref.shape)


def _route(logits_t, b_expert, b_group):
    T = logits_t.shape[1]
    return pl.pallas_call(
        _route_kernel,
        grid=(T // ROUTE_TILE,),
        in_specs=[
            pl.BlockSpec((ROUTER_ROWS, ROUTE_TILE), lambda i: (0, i)),
            _const_spec((N_EXPERTS, 1)),
            _const_spec((8, 1)),
        ],
        out_specs=[
            pl.BlockSpec((8, ROUTE_TILE), lambda i: (0, i)),
            pl.BlockSpec((ROUTE_TILE, 8), lambda i: (i, 0)),
            _const_spec((N_EXPERTS, LANES)),
        ],
        out_shape=[
            jax.ShapeDtypeStruct((8, T), jnp.int32),
            jax.ShapeDtypeStruct((T, 8), f32),
            jax.ShapeDtypeStruct((N_EXPERTS, LANES), f32),
        ],
        scratch_shapes=[pltpu.VMEM((N_EXPERTS, 1), f32),
                        pltpu.VMEM((ROUTE_TILE, ROUTE_TILE), bf16),
                        pltpu.VMEM((8, ROUTE_TILE), f32)],
        compiler_params=_params(("arbitrary",)),
        name="route",
    )(logits_t, b_expert, b_group)


def _row_copy(src, s, dst, d, sem):
    return pltpu.make_async_copy(src.at[s], dst.at[d], sem)


def _wait_rows(buf, sem):
    pltpu.make_async_copy(buf, buf, sem).wait()


def _dispatch_kernel(n_blocks, pend_ref, padded_ref, nu_ref, dest_ref, x_ref, mn_ref, xs_hbm,
                     zero_scr, x3_scr, zsem, sem):
    @pl.when(pl.program_id(0) == 0)
    def _():
        zero_scr[...] = jnp.zeros_like(zero_scr)

        def fill(row0):
            return pltpu.make_async_copy(zero_scr, xs_hbm.at[pl.ds(row0, EXPERT_ROWS)], zsem)

        for e in range(N_EXPERTS):
            @pl.when(padded_ref[e] > 0)
            def _():
                fill(pend_ref[e] - EXPERT_ROWS).start()

            @pl.when(nu_ref[0] + e < n_blocks)
            def _():
                fill((nu_ref[0] + e) * EXPERT_ROWS).start()

        for e in range(N_EXPERTS):
            @pl.when(padded_ref[e] > 0)
            def _():
                fill(0).wait()

            @pl.when(nu_ref[0] + e < n_blocks)
            def _():
                fill(0).wait()

    i = pl.program_id(0)
    n_tiles = pl.num_programs(0) - 1

    def staged(rows):
        return _pack_rows(_rms(x_ref[rows, :], mn_ref[...]))

    @pl.when(i == 0)
    def _():
        x3_scr[0] = staged(slice(None))

    @pl.when(i > 0)
    def _():
        cur = (i - 1) % 3
        nxt = i % 3

        def chunk(c, carry):
            r0 = pl.multiple_of(c * MOVE_CHUNK, MOVE_CHUNK)
            for j in range(MOVE_CHUNK):
                for k in range(TOP_K):
                    _row_copy(x3_scr.at[cur], r0 + j, xs_hbm,
                              dest_ref[0, k * MOVE_TILE + r0 + j], sem.at[cur]).start(priority=k)
            x3_scr[nxt, pl.ds(r0, MOVE_CHUNK)] = staged(pl.ds(r0, MOVE_CHUNK))
            return carry

        lax.fori_loop(0, MOVE_TILE // MOVE_CHUNK, chunk, 0)

        @pl.when(i > 1)
        def _():
            for k in range(TOP_K):
                _wait_rows(x3_scr.at[(i - 2) % 3], sem.at[(i - 2) % 3])

        @pl.when(i == n_tiles)
        def _():
            for k in range(TOP_K):
                _wait_rows(x3_scr.at[cur], sem.at[cur])


def _dispatch(pend, padded, n_used, dest_tiles, x2d, m_norm, n_rows):
    T = x2d.shape[0]
    n_blocks = n_rows // EXPERT_ROWS
    n_tiles = T // MOVE_TILE
    grid_spec = pltpu.PrefetchScalarGridSpec(
        num_scalar_prefetch=3,
        grid=(n_tiles + 1,),
        in_specs=[
            pl.BlockSpec((None, 1, TOP_K * MOVE_TILE),
                         lambda i, *_: (jnp.maximum(i - 1, 0), 0, 0), memory_space=pltpu.SMEM),
            pl.BlockSpec((MOVE_TILE, D_MODEL), lambda i, *_: (jnp.minimum(i, n_tiles - 1), 0)),
            pl.BlockSpec((1, D_MODEL), lambda i, *_: (0, 0)),
        ],
        out_specs=pl.BlockSpec(memory_space=pl.ANY),
        scratch_shapes=[
            pltpu.VMEM((EXPERT_ROWS, ROW_SUBLANES, LANES), u32),
            pltpu.VMEM((3, MOVE_TILE, ROW_SUBLANES, LANES), u32),
            pltpu.SemaphoreType.DMA(()),
            pltpu.SemaphoreType.DMA((3,)),
        ],
    )
    return pl.pallas_call(
        functools.partial(_dispatch_kernel, n_blocks),
        grid_spec=grid_spec,
        out_shape=jax.ShapeDtypeStruct((n_rows, ROW_SUBLANES, LANES), u32),
        compiler_params=_params(("arbitrary",)),
        name="dispatch",
    )(pend, padded, n_used, dest_tiles, x2d, m_norm)


def _expert_kernel(n_blocks, ps_ref, nb_ref, nu_ref, xs_hbm, w1_ref, w3_ref, w2_ref,
                   ys_hbm, xbuf, ybuf, h_scr, w1_scr, w3_scr, w2_scr, xsem, ysem):
    e = pl.program_id(0)
    nblk = nb_ref[e]
    g0 = ps_ref[e]
    n_used = nu_ref[0]

    def rows(g):
        return pl.ds(pl.multiple_of(g * EXPERT_ROWS, EXPERT_ROWS), EXPERT_ROWS)

    def x_copy(g):
        return pltpu.make_async_copy(xs_hbm.at[rows(g)], xbuf.at[g % 2], xsem.at[g % 2])

    def y_copy(g):
        return pltpu.make_async_copy(ybuf.at[g % 2], ys_hbm.at[rows(g)], ysem.at[g % 2])

    def normed(slot):
        return _unpack_rows(xbuf[slot]).astype(bf16)

    @pl.when(e == 0)
    def _():
        x_copy(0).start(priority=1)
        x_copy(1).start(priority=1)
        x_copy(0).wait()
        h_scr[...] = normed(0)

    @pl.when(nblk > 0)
    def _():
        w1_scr[...] = w1_ref[...].astype(bf16)
        w3_scr[...] = w3_ref[...].astype(bf16)
        w2_scr[...] = w2_ref[...].astype(bf16)

        def body(g, h):
            @pl.when(g + 2 < n_used)
            def _():
                x_copy(g + 2).start(priority=1)

            @pl.when(g + 1 < n_used)
            def _():
                x_copy(g + 1).wait()

            @pl.when(g >= 2)
            def _():
                y_copy(g - 2).wait()

            h_next = normed((g + 1) % 2)
            y = None
            for c0 in range(0, D_EXPERT, D_EXPERT // 2):
                cols = slice(c0, c0 + D_EXPERT // 2)
                a = _dot(h, w1_scr[:, cols])
                c = _dot(h, w3_scr[:, cols])
                hid = ((a * jax.nn.sigmoid(a)) * c).astype(bf16)
                part = _dot(hid, w2_scr[cols, :])
                y = part if y is None else y + part
            ybuf[g % 2] = _pack_words(y)
            y_copy(g).start(priority=1)
            return h_next

        h_scr[...] = lax.fori_loop(g0, g0 + nblk, body, h_scr[...])

    @pl.when(e == pl.num_programs(0) - 1)
    def _():
        y_copy(n_used - 2).wait()
        y_copy(n_used - 1).wait()
        ybuf[0] = jnp.zeros(ybuf.shape[1:], u32)

        def fill(b):
            r = pl.ds(pl.multiple_of(b * EXPERT_ROWS, EXPERT_ROWS), EXPERT_ROWS)
            return pltpu.make_async_copy(ybuf.at[0], ys_hbm.at[r], ysem.at[0])

        def start(b, c):
            fill(b).start()
            return c

        def wait(b, c):
            fill(b).wait()
            return c

        lax.fori_loop(nu_ref[0], n_blocks, start, 0)
        lax.fori_loop(nu_ref[0], n_blocks, wait, 0)


def _experts(layer, pstart, seg_blocks, n_used, xs, w1, w3, w2):
    n_rows = xs.shape[0]
    n_blocks = n_rows // EXPERT_ROWS

    def w_map(e, *_):
        return (layer, e, 0, 0)

    block_buf = pltpu.VMEM((2, EXPERT_ROWS, ROW_SUBLANES, LANES), u32)
    out_buf = pltpu.VMEM((2, EXPERT_ROWS, D_MODEL // 2), u32)
    grid_spec = pltpu.PrefetchScalarGridSpec(
        num_scalar_prefetch=3,
        grid=(N_EXPERTS,),
        in_specs=[
            pl.BlockSpec(memory_space=pl.ANY),
            pl.BlockSpec((None, None, D_MODEL, D_EXPERT), w_map),
            pl.BlockSpec((None, None, D_MODEL, D_EXPERT), w_map),
            pl.BlockSpec((None, None, D_EXPERT, D_MODEL), w_map),
        ],
        out_specs=pl.BlockSpec(memory_space=pl.ANY),
        scratch_shapes=[
            block_buf,
            out_buf,
            pltpu.VMEM((EXPERT_ROWS, D_MODEL), bf16),
            pltpu.VMEM((D_MODEL, D_EXPERT), bf16),
            pltpu.VMEM((D_MODEL, D_EXPERT), bf16),
            pltpu.VMEM((D_EXPERT, D_MODEL), bf16),
            pltpu.SemaphoreType.DMA((2,)),
            pltpu.SemaphoreType.DMA((2,)),
        ],
    )
    return pl.pallas_call(
        functools.partial(_expert_kernel, n_blocks),
        grid_spec=grid_spec,
        out_shape=jax.ShapeDtypeStruct((n_rows, D_MODEL // 2), u32),
        compiler_params=_params(("arbitrary",)),
        name="experts",
    )(pstart, seg_blocks, n_used, xs, w1, w3, w2)


def _gather_rows(ys, dest):
    n_idx = dest.size
    words = ys.shape[1]
    mesh = plsc.VectorSubcoreMesh(core_axis_name="core", subcore_axis_name="subcore")

    @pl.kernel(out_type=jax.ShapeDtypeStruct((n_idx, words), u32), mesh=mesh)
    def gather(src_hbm, idx_hbm, out_hbm):
        for slab in range(words // GATHER_WIDTH):
            def body(idx_vmem, out_vmem, slab=slab):
                cols = pl.ds(slab * GATHER_WIDTH, GATHER_WIDTH)
                pltpu.sync_copy(src_hbm.at[:, cols].at[idx_vmem.at[0]], out_vmem)

            pltpu.emit_pipeline(
                body,
                grid=(n_idx // GATHER_WINDOW,),
                in_specs=[pl.BlockSpec((1, GATHER_WINDOW), index_map=lambda i: (0, i))],
                out_specs=[pl.BlockSpec((GATHER_WINDOW, GATHER_WIDTH),
                                        index_map=lambda i, slab=slab: (i, slab))],
                core_axis_name=("core", "subcore"),
                dimension_semantics=(pltpu.PARALLEL,),
            )(idx_hbm, out_hbm)

    return gather(ys, dest.reshape(1, n_idx)).reshape(dest.shape + (words,))


def _combine_kernel(x_ref, w_ref, y0_ref, y1_ref, out_ref):
    w = w_ref[...]
    y0 = _unpack_words(y0_ref[...])
    y1 = _unpack_words(y1_ref[...])
    out_ref[...] = x_ref[...] + (w[:, 0:1] * y0 + w[:, 1:2] * y1)


def _combine(dest, x2d, w_cols, ys):
    T = x2d.shape[0]
    n_tiles = T // MOVE_TILE
    yg = _gather_rows(ys, dest)
    return pl.pallas_call(
        _combine_kernel,
        grid=(n_tiles,),
        in_specs=[
            pl.BlockSpec((MOVE_TILE, D_MODEL), lambda i: (i, 0)),
            pl.BlockSpec((MOVE_TILE, 8), lambda i: (i, 0)),
            pl.BlockSpec((None, MOVE_TILE, D_MODEL // 2), lambda i: (0, i, 0)),
            pl.BlockSpec((None, MOVE_TILE, D_MODEL // 2), lambda i: (1, i, 0)),
        ],
        out_specs=pl.BlockSpec((MOVE_TILE, D_MODEL), lambda i: (i, 0)),
        out_shape=jax.ShapeDtypeStruct((T, D_MODEL), f32),
        compiler_params=_params(("parallel",)),
        name="combine",
    )(x2d, w_cols, yg, yg)


def _moe(layer, x2d, logits_t, m_norm, b_group, b_expert, w1, w3, w2):
    T = x2d.shape[0]
    n_blocks = (T * TOP_K) // EXPERT_ROWS + N_EXPERTS
    n_rows = n_blocks * EXPERT_ROWS

    be = b_expert.reshape(N_EXPERTS, 1)
    bg = jnp.concatenate([b_group, jnp.zeros((8 - N_GROUPS,), f32)]).reshape(8, 1)
    oi, w_cols, cnt = _route(logits_t, be, bg)

    counts = cnt[:, 0].astype(jnp.int32)
    padded = (counts + EXPERT_ROWS - 1) // EXPERT_ROWS * EXPERT_ROWS
    pend = jnp.cumsum(padded)
    pstart = pend - padded
    n_used = (pend[-1] // EXPERT_ROWS).astype(jnp.int32).reshape(1)
    e_ids = jnp.arange(N_EXPERTS, dtype=jnp.int32)[:, None, None]
    seg_start = jnp.sum(jnp.where(oi[None, 0:2] == e_ids, pstart[:, None, None], 0), axis=0)
    dest = seg_start + oi[2:4]
    dest_tiles = dest.reshape(TOP_K, T // MOVE_TILE, MOVE_TILE).transpose(1, 0, 2).reshape(
        T // MOVE_TILE, 1, TOP_K * MOVE_TILE)

    xs = _dispatch(pend.astype(jnp.int32), padded, n_used, dest_tiles, x2d, m_norm, n_rows)
    ys = _experts(layer, (pstart // EXPERT_ROWS).astype(jnp.int32), padded // EXPERT_ROWS,
                  n_used, xs, w1, w3, w2)
    return _combine(dest, x2d, w_cols, ys)


def _log_sigmoid(x):
    return jnp.minimum(x, 0.0) - jnp.log1p(jnp.exp(-jnp.abs(x)))


def _proj_kernel(tiles_per_seq, x_ref, kvn_ref, bn_ref, kvw_ref, bf_ref, qgw_ref, gq_ref, gk_ref,
                 q_ref, kt_ref, v_ref, sg_ref, carry_scr, wkt_ref, wv_ref, wf_ref, wq_ref, wg_ref):
    i = pl.program_id(0)

    @pl.when(i == 0)
    def _():
        wkt_ref[...] = kvw_ref[:D_MODEL, :].astype(bf16)
        wv_ref[...] = kvw_ref[D_MODEL:2 * D_MODEL, :].T.astype(bf16)
        wf_ref[...] = jnp.concatenate(
            [kvw_ref[2 * D_MODEL:, :].T, jnp.zeros((D_MODEL, LANES - N_HEADS), f32)],
            axis=1).astype(bf16)
        wg_ref[...] = qgw_ref[:, D_MODEL:].astype(bf16)
        zeros = jnp.zeros((D_MODEL, HEAD_PAD - HEAD_DIM), f32)
        for h in range(N_HEADS):
            wq_ref[:, h * HEAD_PAD:(h + 1) * HEAD_PAD] = jnp.concatenate(
                [qgw_ref[:, h * HEAD_DIM:(h + 1) * HEAD_DIM], zeros], axis=1).astype(bf16)

    x = x_ref[...]
    hkv = _rms(x, kvn_ref[...]).astype(bf16)
    hq = _rms(x, bn_ref[...]).astype(bf16)

    v_ref[...] = _dot(hkv, wv_ref[...]).astype(bf16)
    sg_ref[...] = jax.nn.sigmoid(_dot(hq, wg_ref[...])).astype(bf16)

    logf = _log_sigmoid(_dot(hkv, wf_ref[...]) + bf_ref[...])
    tm = x.shape[0]
    incl = (lax.broadcasted_iota(jnp.int32, (tm, tm), 0)
            >= lax.broadcasted_iota(jnp.int32, (tm, tm), 1)).astype(bf16)

    @pl.when(i % tiles_per_seq == 0)
    def _():
        carry_scr[...] = jnp.zeros_like(carry_scr)

    parts = _dot(incl, jnp.concatenate(_split3(logf), axis=1).astype(bf16))
    cum = (parts[:, :LANES] + parts[:, LANES:2 * LANES] + parts[:, 2 * LANES:]
           + carry_scr[...])
    carry_scr[...] = cum[tm - 1:tm, :]
    cum = cum * LOG2E
    cum_t = cum.T

    lane = lax.broadcasted_iota(jnp.int32, (tm, HEAD_PAD), 1)
    sub = lax.broadcasted_iota(jnp.int32, (HEAD_DIM, tm), 0)
    q_raw = _dot(hq, wq_ref[...])
    k_raw_t = _dot_nt(wkt_ref[...], hkv)
    scale = HEAD_DIM ** -0.5 * LOG2E

    for h in range(N_HEADS):
        sl = slice(h * HEAD_PAD, (h + 1) * HEAD_PAD)
        qb = q_raw[:, sl]
        q_ms = jnp.sum(qb * qb, axis=-1, keepdims=True) * (1.0 / HEAD_DIM)
        qn = qb * lax.rsqrt(q_ms + EPS) * gq_ref[...] * scale
        c_hi, c_mid, c_lo = _split3(cum[:, h:h + 1])
        qa = jnp.where(lane < HEAD_DIM, qn,
             jnp.where(lane < AUX_CUM0, 1.0,
             jnp.where(lane == AUX_CUM0, c_hi,
             jnp.where(lane == AUX_CUM0 + 1, c_mid,
             jnp.where(lane == AUX_CUM0 + 2, c_lo, 0.0)))))
        q_ref[:, sl] = qa.astype(bf16)

        kb = k_raw_t[h * HEAD_DIM:(h + 1) * HEAD_DIM, :]
        k_ms = jnp.sum(kb * kb, axis=0, keepdims=True) * (1.0 / HEAD_DIM)
        kn = kb * lax.rsqrt(k_ms + EPS) * gk_ref[...]
        t_hi, t_mid, t_lo = _split3(cum_t[h:h + 1, :])
        aux = jnp.where(sub == 0, -t_hi,
              jnp.where(sub == 1, -t_mid,
              jnp.where(sub == 2, -t_lo,
              jnp.where(sub < 6, 1.0, 0.0))))
        sl = slice(h * HEAD_PAD, (h + 1) * HEAD_PAD)
        ka = jnp.concatenate([kn, aux], axis=0).astype(bf16)
        for t in range(tm // ATT_K):
            kt_ref[t, sl, :] = ka[:, t * ATT_K:(t + 1) * ATT_K]


def _proj(x2d, seq, kv_norm, b_norm, kv_w, bfv, w_qg, gq, gk):
    T = x2d.shape[0]
    n_tiles = T // ROW_TILE
    qw = N_HEADS * HEAD_PAD
    return pl.pallas_call(
        functools.partial(_proj_kernel, seq // ROW_TILE),
        grid=(n_tiles,),
        in_specs=[
            pl.BlockSpec((ROW_TILE, D_MODEL), lambda i: (i, 0)),
            _const_spec((1, D_MODEL)),
            _const_spec((1, D_MODEL)),
            _const_spec(kv_w.shape, single=True),
            _const_spec((1, LANES)),
            _const_spec(w_qg.shape, single=True),
            _const_spec((1, HEAD_PAD)),
            _const_spec((HEAD_DIM, 1)),
        ],
        out_specs=[
            pl.BlockSpec((ROW_TILE, qw), lambda i: (i, 0)),
            pl.BlockSpec((ROW_TILE // ATT_K, qw, ATT_K), lambda i: (i, 0, 0)),
            pl.BlockSpec((ROW_TILE, D_MODEL), lambda i: (i, 0)),
            pl.BlockSpec((ROW_TILE, D_MODEL), lambda i: (i, 0)),
        ],
        out_shape=[
            jax.ShapeDtypeStruct((T, qw), bf16),
            jax.ShapeDtypeStruct((T // ATT_K, qw, ATT_K), bf16),
            jax.ShapeDtypeStruct((T, D_MODEL), bf16),
            jax.ShapeDtypeStruct((T, D_MODEL), bf16),
        ],
        scratch_shapes=[
            pltpu.VMEM((1, LANES), f32),
            pltpu.VMEM((D_MODEL, D_MODEL), bf16),
            pltpu.VMEM((D_MODEL, D_MODEL), bf16),
            pltpu.VMEM((D_MODEL, LANES), bf16),
            pltpu.VMEM((D_MODEL, qw), bf16),
            pltpu.VMEM((D_MODEL, D_MODEL), bf16),
        ],
        compiler_params=_params(("arbitrary",)),
        name="proj",
    )(x2d, kv_norm, b_norm, kv_w, bfv, w_qg, gq, gk)


def _attn_kernel(q_ref, kt_ref, v_ref, o_ref, s_scr, m_scr, l_scr, acc_scr):
    qi = pl.program_id(2)
    tiles = ATT_Q // ATT_K
    groups = ATT_K // LANES
    heads = [slice(hh * HEAD_PAD, (hh + 1) * HEAD_PAD) for hh in range(2)]
    m_scr[...] = jnp.full(m_scr.shape, -jnp.inf, f32)

    def lane_max(mx, s):
        for g in range(groups):
            mx = jnp.maximum(mx, s[:, g * LANES:(g + 1) * LANES])
        return mx

    def probs(s, mb):
        ps = [jnp.exp2(s[:, g * LANES:(g + 1) * LANES] - mb) for g in range(groups)]
        return ps, functools.reduce(lambda a, b: a + b, ps)

    def body_a(trip, c):
        for t in range(tiles):
            kt_idx = trip * tiles + t
            for hh in range(2):
                s = _dot(q_ref[:, heads[hh]], kt_ref[kt_idx, heads[hh], :])
                s_scr[hh, kt_idx] = s
                m_scr[hh] = lane_max(m_scr[hh], s)
        return c

    lax.fori_loop(0, qi, body_a, 0)
    for t in range(tiles):
        kt_idx = qi * tiles + t
        rows = ATT_Q - t * ATT_K
        visible = (lax.broadcasted_iota(jnp.int32, (rows, ATT_K), 1)
                   <= lax.broadcasted_iota(jnp.int32, (rows, ATT_K), 0))
        for hh in range(2):
            s = _dot(q_ref[t * ATT_K:, heads[hh]], kt_ref[kt_idx, heads[hh], :])
            s = jnp.where(visible, s, -jnp.inf)
            s_scr[hh, kt_idx, t * ATT_K:, :] = s
            m_scr[hh, t * ATT_K:, :] = lane_max(m_scr[hh, t * ATT_K:, :], s)

    for hh in range(2):
        row_max = jnp.max(m_scr[hh], axis=-1, keepdims=True)
        m_scr[hh] = jnp.broadcast_to(row_max, (ATT_Q, LANES))
    l_scr[...] = jnp.zeros(l_scr.shape, f32)
    acc_scr[...] = jnp.zeros(acc_scr.shape, f32)

    def body_b(trip, c):
        for hh in range(2):
            mb = m_scr[hh]
            lsum, acc = l_scr[hh], acc_scr[hh]
            for t0 in range(0, tiles, PV_TILES):
                kt_idx = trip * tiles + t0
                row0 = pl.multiple_of(kt_idx * ATT_K, PV_TILES * ATT_K)
                vb = v_ref[pl.ds(row0, PV_TILES * ATT_K), :]
                parts = []
                for t in range(PV_TILES):
                    ps, psum = probs(s_scr[hh, kt_idx + t], mb)
                    parts += ps
                    lsum = lsum + psum
                acc = acc + _dot(jnp.concatenate(parts, axis=1).astype(bf16), vb)
            l_scr[hh] = lsum
            acc_scr[hh] = acc
        return c

    lax.fori_loop(0, qi, body_b, 0)
    for t in range(tiles):
        kt_idx = qi * tiles + t
        row0 = pl.multiple_of(kt_idx * ATT_K, ATT_K)
        vb = v_ref[pl.ds(row0, ATT_K), :]
        for hh in range(2):
            ps, psum = probs(s_scr[hh, kt_idx, t * ATT_K:, :], m_scr[hh, t * ATT_K:, :])
            l_scr[hh, t * ATT_K:, :] += psum
            acc_scr[hh, t * ATT_K:, :] += _dot(jnp.concatenate(ps, axis=1).astype(bf16), vb)

    lane = lax.broadcasted_iota(jnp.int32, (ATT_Q, 2 * HEAD_DIM), 1)
    o0 = acc_scr[0] / jnp.sum(l_scr[0], axis=-1, keepdims=True)
    o1 = acc_scr[1] / jnp.sum(l_scr[1], axis=-1, keepdims=True)
    o_ref[...] = jnp.where(lane < HEAD_DIM, o0, o1).astype(bf16)


def _attention(q, kt, v, batch, seq):
    T = q.shape[0]
    nq = seq // ATT_Q
    nk = seq // ATT_K
    pairs = N_HEADS // 2
    return pl.pallas_call(
        _attn_kernel,
        grid=(batch, pairs, nq),
        in_specs=[
            pl.BlockSpec((ATT_Q, 2 * HEAD_PAD), lambda b, p, i: (b * nq + i, p)),
            pl.BlockSpec((nk, 2 * HEAD_PAD, ATT_K), lambda b, p, i: (b, p, 0)),
            pl.BlockSpec((seq, 2 * HEAD_DIM), lambda b, p, i: (b, p)),
        ],
        out_specs=pl.BlockSpec((ATT_Q, 2 * HEAD_DIM), lambda b, p, i: (b * nq + i, p)),
        out_shape=jax.ShapeDtypeStruct((T, D_MODEL), bf16),
        scratch_shapes=[
            pltpu.VMEM((2, nk, ATT_Q, ATT_K), f32),
            pltpu.VMEM((2, ATT_Q, LANES), f32),
            pltpu.VMEM((2, ATT_Q, LANES), f32),
            pltpu.VMEM((2, ATT_Q, 2 * HEAD_DIM), f32),
        ],
        compiler_params=_params(("parallel", "parallel", "parallel")),
        name="attn",
    )(q, kt, v)


def _attn_out_kernel(x_ref, o_ref, sg_ref, wo_ref, mn_ref, wr_ref, x3_ref, lg_ref):
    for r0 in range(0, SGU_TILE, EPILOGUE_ROWS):
        rows = slice(r0, r0 + EPILOGUE_ROWS)
        gated = (o_ref[rows, :].astype(f32) * sg_ref[rows, :].astype(f32)).astype(bf16)
        x3 = x_ref[rows, :] + _dot(gated, wo_ref[...])
        x3_ref[rows, :] = x3
        lg_ref[:, rows] = _router_logits_t(x3, mn_ref[...], wr_ref[...])


def _attn_out(x2d, o, sg, wo, m_norm, w_router):
    T = x2d.shape[0]
    return pl.pallas_call(
        _attn_out_kernel,
        grid=(T // SGU_TILE,),
        in_specs=[
            pl.BlockSpec((SGU_TILE, D_MODEL), lambda i: (i, 0)),
            pl.BlockSpec((SGU_TILE, D_MODEL), lambda i: (i, 0)),
            pl.BlockSpec((SGU_TILE, D_MODEL), lambda i: (i, 0)),
            _const_spec((D_MODEL, D_MODEL)),
            _const_spec((1, D_MODEL)),
            _const_spec((2 * D_MODEL, 2 * LANES)),
        ],
        out_specs=[
            pl.BlockSpec((SGU_TILE, D_MODEL), lambda i: (i, 0)),
            pl.BlockSpec((ROUTER_ROWS, SGU_TILE), lambda i: (0, i)),
        ],
        out_shape=[
            jax.ShapeDtypeStruct((T, D_MODEL), f32),
            jax.ShapeDtypeStruct((ROUTER_ROWS, T), f32),
        ],
        compiler_params=_params(("parallel",)),
        name="attn_out",
    )(x2d, o, sg, wo, m_norm, w_router)


def _router_weight(w_group, w_expert):
    pad = jnp.zeros((D_MODEL, LANES - N_EXPERTS - N_GROUPS), f32)
    w = jnp.concatenate([w_expert, w_group, pad], axis=1)
    w_hi = w.astype(bf16)
    w_lo = (w - w_hi.astype(f32)).astype(bf16)
    top = jnp.concatenate([w_hi, w_lo], axis=1)
    bottom = jnp.concatenate([w_hi, jnp.zeros_like(w_hi)], axis=1)
    return jnp.concatenate([top, bottom], axis=0)


def kernel(x, a_norm, a_w_in, a_b_in, a_v_norm, a_w_s, a_b_s, a_w_out, kv_norm, kv_w, kv_b_f,
           k_norm, b_norm, b_w_qg, q_norm, b_w_out, m_norm, m_w_group, m_b_group, m_w_expert,
           m_b_expert, m_w1, m_w3, m_w2):
    batch, seq, _ = x.shape
    T = batch * seq
    x2d = x.reshape(T, D_MODEL)

    x1, lg0 = _sgu(
        x2d, a_norm[0].reshape(1, -1), a_w_in[0], a_b_in[0].reshape(1, -1),
        a_v_norm[0].reshape(1, -1), a_w_s[0], a_b_s[0].T, a_w_out[0].astype(bf16),
        m_norm[0].reshape(1, -1), _router_weight(m_w_group[0], m_w_expert[0]))
    x2 = _moe(0, x1, lg0, m_norm[0].reshape(1, -1), m_b_group[0], m_b_expert[0],
              m_w1, m_w3, m_w2)

    bfv = jnp.pad(kv_b_f, (0, LANES - N_HEADS)).reshape(1, LANES)
    gq = jnp.pad(q_norm[0], (0, HEAD_PAD - HEAD_DIM)).reshape(1, HEAD_PAD)
    gk = k_norm.reshape(HEAD_DIM, 1)
    q, kt, v, sg = _proj(x2, seq, kv_norm.reshape(1, -1), b_norm[0].reshape(1, -1),
                         kv_w.T, bfv, b_w_qg[0], gq, gk)
    o = _attention(q, kt, v, batch, seq)
    x3, lg1 = _attn_out(x2, o, sg, b_w_out[0].astype(bf16), m_norm[1].reshape(1, -1),
                        _router_weight(m_w_group[1], m_w_expert[1]))
    x4 = _moe(1, x3, lg1, m_norm[1].reshape(1, -1), m_b_group[1], m_b_expert[1],
              m_w1, m_w3, m_w2)
    return x4.reshape(batch, seq, D_MODEL)
```

```python
import functools
import math

import jax
import jax.numpy as jnp
import numpy as np
from jax import lax
from jax.experimental import pallas as pl
from jax.experimental.pallas import tpu as pltpu
from jax.experimental.pallas import tpu_sc as plsc

D_MODEL = 1024
EPS = 1e-6
SGU_BLOCK = 128
SGU_CHUNK = 64
CHUNK_SHIFT = SGU_CHUNK.bit_length() - 1
SGU_WIDTH = 2 * D_MODEL
SGU_GROUPS = 8
SGU_GROUP_DIM = SGU_WIDTH // SGU_GROUPS
N_HEADS = 16
HEAD_DIM = D_MODEL // N_HEADS
N_GROUPS = 4
EXPERTS_PER_GROUP = 8
N_EXPERTS = N_GROUPS * EXPERTS_PER_GROUP
GROUP_SHIFT = EXPERTS_PER_GROUP.bit_length() - 1
TOP_K = 2
D_EXPERT = D_MODEL // 2

LANES = 128
ROW_SUBLANES = D_MODEL // 2 // LANES
ISSUE_UNROLL = 8
VMEM_LIMIT_BYTES = 56 * 1024 * 1024

ROW_TILE = 256
SGU_TILE = 512
EPILOGUE_ROWS = 256
ROUTE_TILE = 1024
EXPERT_ROWS = 512
MOVE_TILE = 512
MOVE_CHUNK = 64
GATHER_WINDOW = 128
GATHER_WIDTH = 256
ATT_Q = 1024
ATT_K = 256
PV_TILES = 2
HEAD_PAD = 2 * HEAD_DIM
ROUTER_ROWS = 40
AUX_ONE0 = HEAD_DIM
AUX_CUM0 = HEAD_DIM + 3

LOG2E = math.log2(math.e)

bf16 = jnp.bfloat16
f32 = jnp.float32
u32 = jnp.uint32
HI_HALF = np.uint32(0xFFFF0000)


def _dot(a, b, precision=None):
    return jnp.dot(a, b, preferred_element_type=f32, precision=precision)


def _dot_nt(a, b, precision=None):
    return lax.dot_general(a, b, (((1,), (1,)), ((), ())), preferred_element_type=f32,
                           precision=precision)


def _rms(x, g):
    ms = jnp.mean(x * x, axis=-1, keepdims=True)
    return x * lax.rsqrt(ms + EPS) * g


def _gelu_tanh(z):
    c = math.sqrt(2.0 / math.pi)
    return z * (0.5 * (1.0 + jnp.tanh(c * (z + 0.044715 * (z * z * z)))))


def _split3(c):
    hi = c.astype(bf16).astype(f32)
    r = c - hi
    mid = r.astype(bf16).astype(f32)
    lo = r - mid
    return hi, mid, lo


def _pack_words(v):
    half = D_MODEL // 2
    lo = lax.bitcast_convert_type(v[:, :half].astype(bf16).astype(f32), u32) >> 16
    hi = lax.bitcast_convert_type(v[:, half:].astype(bf16).astype(f32), u32) & HI_HALF
    return hi | lo


def _unpack_words(w):
    lo = lax.bitcast_convert_type(w << 16, f32)
    hi = lax.bitcast_convert_type(w & HI_HALF, f32)
    return jnp.concatenate([lo, hi], axis=1)


def _pack_rows(v):
    return _pack_words(v).reshape(v.shape[0], ROW_SUBLANES, LANES)


def _unpack_rows(w):
    return _unpack_words(w.reshape(w.shape[0], D_MODEL // 2))


def _const_spec(shape, single=False):
    nd = len(shape)
    mode = pl.Buffered(1) if single else None
    return pl.BlockSpec(shape, lambda *_: (0,) * nd, pipeline_mode=mode)


def _params(sem):
    return pltpu.CompilerParams(dimension_semantics=sem, vmem_limit_bytes=VMEM_LIMIT_BYTES)


def _router_logits_t(x_new, mn, wr3):
    hm = _rms(x_new, mn)
    h_hi = hm.astype(bf16)
    h_lo = (hm - h_hi.astype(f32)).astype(bf16)
    both = _dot(jnp.concatenate([h_hi, h_lo], axis=1), wr3)
    lg = both[:, :LANES] + both[:, LANES:]
    return lg.T[:ROUTER_ROWS, :]


def _sgu_kernel(x_ref, an_ref, win_f32_ref, bin_ref, vn_ref, ws_ref, bst_ref, wout_ref, mn_ref,
                wr_ref, x1_ref, lg_ref, u_scr, v_scr, gated_scr, win_ref):
    @pl.when(pl.program_id(0) == 0)
    def _():
        win_ref[...] = win_f32_ref[...].astype(bf16)

    x = x_ref[...]
    h = _rms(x, an_ref[...]).astype(bf16)
    v = _gelu_tanh(_dot(h, win_ref[:, SGU_WIDTH:]) + bin_ref[:, SGU_WIDTH:])
    v_scr[...] = _rms(v, vn_ref[...]).astype(bf16)
    u_scr[...] = _gelu_tanh(_dot(h, win_ref[:, :SGU_WIDTH]) + bin_ref[:, :SGU_WIDTH])

    t_chunk = lax.broadcasted_iota(jnp.int32, (SGU_BLOCK, SGU_BLOCK), 0) >> CHUNK_SHIFT
    s_chunk = lax.broadcasted_iota(jnp.int32, (SGU_BLOCK, SGU_BLOCK), 1) >> CHUNK_SHIFT
    causal = t_chunk >= s_chunk
    wsm = [jnp.where(causal, ws_ref[g], 0.0).astype(bf16) for g in range(SGU_GROUPS)]
    for r0 in range(0, SGU_TILE, EPILOGUE_ROWS):
        for g in range(SGU_GROUPS):
            cols = slice(g * SGU_GROUP_DIM, (g + 1) * SGU_GROUP_DIM)
            for sb in range(r0 // SGU_BLOCK, (r0 + EPILOGUE_ROWS) // SGU_BLOCK):
                rows = slice(sb * SGU_BLOCK, (sb + 1) * SGU_BLOCK)
                mixed = _dot(wsm[g], v_scr[rows, cols]) + bst_ref[:, g:g + 1]
                gated_scr[rows, cols] = (u_scr[rows, cols] * mixed).astype(bf16)
        half = slice(r0, r0 + EPILOGUE_ROWS)
        x1 = x[half, :] + _dot(gated_scr[half, :], wout_ref[...])
        x1_ref[half, :] = x1
        lg_ref[:, half] = _router_logits_t(x1, mn_ref[...], wr_ref[...])


def _sgu(x2d, a_norm, w_in, b_in, v_norm, w_s, b_s_t, w_out, m_norm, w_router):
    T = x2d.shape[0]
    return pl.pallas_call(
        _sgu_kernel,
        grid=(T // SGU_TILE,),
        in_specs=[
            pl.BlockSpec((SGU_TILE, D_MODEL), lambda i: (i, 0)),
            _const_spec((1, D_MODEL)),
            _const_spec((D_MODEL, 2 * SGU_WIDTH), single=True),
            _const_spec((1, 2 * SGU_WIDTH)),
            _const_spec((1, SGU_WIDTH)),
            _const_spec((SGU_GROUPS, SGU_BLOCK, SGU_BLOCK)),
            _const_spec((SGU_BLOCK, SGU_GROUPS)),
            _const_spec((SGU_WIDTH, D_MODEL), single=True),
            _const_spec((1, D_MODEL)),
            _const_spec((2 * D_MODEL, 2 * LANES)),
        ],
        out_specs=[
            pl.BlockSpec((SGU_TILE, D_MODEL), lambda i: (i, 0)),
            pl.BlockSpec((ROUTER_ROWS, SGU_TILE), lambda i: (0, i)),
        ],
        out_shape=[
            jax.ShapeDtypeStruct((T, D_MODEL), f32),
            jax.ShapeDtypeStruct((ROUTER_ROWS, T), f32),
        ],
        scratch_shapes=[
            pltpu.VMEM((SGU_TILE, SGU_WIDTH), f32),
            pltpu.VMEM((SGU_TILE, SGU_WIDTH), bf16),
            pltpu.VMEM((SGU_TILE, SGU_WIDTH), bf16),
            pltpu.VMEM((D_MODEL, 2 * SGU_WIDTH), bf16),
        ],
        compiler_params=_params(("arbitrary",)),
        name="sgu",
    )(x2d, a_norm, w_in, b_in, v_norm, w_s, b_s_t, w_out, m_norm, w_router)


def _route_kernel(lg_ref, be_ref, bg_ref, oi_ref, ow_ref, cnt_ref, carry_scr, before_scr,
                  w_scr):
    step = pl.program_id(0)
    tt = lg_ref.shape[1]

    @pl.when(step == 0)
    def _():
        carry_scr[...] = jnp.zeros_like(carry_scr)
        before_scr[...] = (lax.broadcasted_iota(jnp.int32, (tt, tt), 0)
                           < lax.broadcasted_iota(jnp.int32, (tt, tt), 1)).astype(bf16)

    lg = lg_ref[...]
    e_l = lg[0:N_EXPERTS, :] + be_ref[...]
    g_l = lg[N_EXPERTS:N_EXPERTS + 8, :] + bg_ref[...]
    g_row = lax.broadcasted_iota(jnp.int32, g_l.shape, 0).astype(f32)
    g_l = jnp.where(g_row < N_GROUPS, g_l, -jnp.inf)
    g_max = jnp.max(g_l, axis=0, keepdims=True)
    g_sel = jnp.min(jnp.where(g_l == g_max, g_row, 8.0), axis=0, keepdims=True)
    g_den = jnp.sum(jnp.exp(g_l - g_max), axis=0, keepdims=True)
    g_w = 1.0 / g_den

    e_row_i = lax.broadcasted_iota(jnp.int32, e_l.shape, 0)
    e_row = e_row_i.astype(f32)
    e_grp = (e_row_i >> GROUP_SHIFT).astype(f32)
    e_in = jnp.where(e_grp == g_sel, e_l, -jnp.inf)
    m1 = jnp.max(e_in, axis=0, keepdims=True)
    i1 = jnp.min(jnp.where(e_in == m1, e_row, float(N_EXPERTS)), axis=0, keepdims=True)
    e_in2 = jnp.where(e_row == i1, -jnp.inf, e_in)
    m2 = jnp.max(e_in2, axis=0, keepdims=True)
    i2 = jnp.min(jnp.where(e_in2 == m2, e_row, float(N_EXPERTS)), axis=0, keepdims=True)
    t = jnp.exp(m2 - m1)
    p1 = 1.0 / (1.0 + t)
    p2 = t / (1.0 + t)

    hit1 = e_row == i1
    hit2 = e_row == i2
    member = jnp.logical_or(hit1, hit2)
    prefix = _dot(member.astype(bf16), before_scr[...])
    rank_full = prefix + carry_scr[...]
    rank1 = jnp.sum(jnp.where(hit1, rank_full, 0.0), axis=0, keepdims=True)
    rank2 = jnp.sum(jnp.where(hit2, rank_full, 0.0), axis=0, keepdims=True)
    carry_new = carry_scr[...] + jnp.sum(member.astype(f32), axis=1, keepdims=True)
    carry_scr[...] = carry_new

    oi_ref[...] = jnp.zeros(oi_ref.shape, jnp.int32)
    oi_ref[0:1, :] = i1.astype(jnp.int32)
    oi_ref[1:2, :] = i2.astype(jnp.int32)
    oi_ref[2:3, :] = rank1.astype(jnp.int32)
    oi_ref[3:4, :] = rank2.astype(jnp.int32)
    w_scr[...] = jnp.zeros(w_scr.shape, f32)
    w_scr[0:1, :] = g_w * p1
    w_scr[1:2, :] = g_w * p2
    ow_ref[...] = w_scr[...].T
    cnt_ref[...] = jnp.broadcast_to(carry_new, cnt_ref.shape)


def _route(logits_t, b_expert, b_group):
    T = logits_t.shape[1]
    return pl.pallas_call(
        _route_kernel,
        grid=(T // ROUTE_TILE,),
        in_specs=[
            pl.BlockSpec((ROUTER_ROWS, ROUTE_TILE), lambda i: (0, i)),
            _const_spec((N_EXPERTS, 1)),
            _const_spec((8, 1)),
        ],
        out_specs=[
            pl.BlockSpec((8, ROUTE_TILE), lambda i: (0, i)),
            pl.BlockSpec((ROUTE_TILE, 8), lambda i: (i, 0)),
            _const_spec((N_EXPERTS, LANES)),
        ],
        out_shape=[
            jax.ShapeDtypeStruct((8, T), jnp.int32),
            jax.ShapeDtypeStruct((T, 8), f32),
            jax.ShapeDtypeStruct((N_EXPERTS, LANES), f32),
        ],
        scratch_shapes=[pltpu.VMEM((N_EXPERTS, 1), f32),
                        pltpu.VMEM((ROUTE_TILE, ROUTE_TILE), bf16),
                        pltpu.VMEM((8, ROUTE_TILE), f32)],
        compiler_params=_params(("arbitrary",)),
        name="route",
    )(logits_t, b_expert, b_group)


def _row_copy(src, s, dst, d, sem):
    return pltpu.make_async_copy(src.at[s], dst.at[d], sem)


def _wait_rows(buf, sem):
    pltpu.make_async_copy(buf, buf, sem).wait()


def _dispatch_kernel(n_blocks, pend_ref, padded_ref, nu_ref, dest_ref, x_ref, mn_ref, xs_hbm,
                     zero_scr, x3_scr, zsem, sem):
    @pl.when(pl.program_id(0) == 0)
    def _():
        zero_scr[...] = jnp.zeros_like(zero_scr)

        def fill(row0):
            return pltpu.make_async_copy(zero_scr, xs_hbm.at[pl.ds(row0, EXPERT_ROWS)], zsem)

        for e in range(N_EXPERTS):
            @pl.when(padded_ref[e] > 0)
            def _():
                fill(pend_ref[e] - EXPERT_ROWS).start()

            @pl.when(nu_ref[0] + e < n_blocks)
            def _():
                fill((nu_ref[0] + e) * EXPERT_ROWS).start()

        for e in range(N_EXPERTS):
            @pl.when(padded_ref[e] > 0)
            def _():
                fill(0).wait()

            @pl.when(nu_ref[0] + e < n_blocks)
            def _():
                fill(0).wait()

    i = pl.program_id(0)
    n_tiles = pl.num_programs(0) - 1

    def staged(rows):
        return _pack_rows(_rms(x_ref[rows, :], mn_ref[...]))

    @pl.when(i == 0)
    def _():
        x3_scr[0] = staged(slice(None))

    @pl.when(i > 0)
    def _():
        cur = (i - 1) % 3
        nxt = i % 3

        def chunk(c, carry):
            r0 = pl.multiple_of(c * MOVE_CHUNK, MOVE_CHUNK)
            for j in range(MOVE_CHUNK):
                for k in range(TOP_K):
                    _row_copy(x3_scr.at[cur], r0 + j, xs_hbm,
                              dest_ref[0, k * MOVE_TILE + r0 + j], sem.at[cur]).start(priority=k)
            x3_scr[nxt, pl.ds(r0, MOVE_CHUNK)] = staged(pl.ds(r0, MOVE_CHUNK))
            return carry

        lax.fori_loop(0, MOVE_TILE // MOVE_CHUNK, chunk, 0)

        @pl.when(i > 1)
        def _():
            for k in range(TOP_K):
                _wait_rows(x3_scr.at[(i - 2) % 3], sem.at[(i - 2) % 3])

        @pl.when(i == n_tiles)
        def _():
            for k in range(TOP_K):
                _wait_rows(x3_scr.at[cur], sem.at[cur])


def _dispatch(pend, padded, n_used, dest_tiles, x2d, m_norm, n_rows):
    T = x2d.shape[0]
    n_blocks = n_rows // EXPERT_ROWS
    n_tiles = T // MOVE_TILE
    grid_spec = pltpu.PrefetchScalarGridSpec(
        num_scalar_prefetch=3,
        grid=(n_tiles + 1,),
        in_specs=[
            pl.BlockSpec((None, 1, TOP_K * MOVE_TILE),
                         lambda i, *_: (jnp.maximum(i - 1, 0), 0, 0), memory_space=pltpu.SMEM),
            pl.BlockSpec((MOVE_TILE, D_MODEL), lambda i, *_: (jnp.minimum(i, n_tiles - 1), 0)),
            pl.BlockSpec((1, D_MODEL), lambda i, *_: (0, 0)),
        ],
        out_specs=pl.BlockSpec(memory_space=pl.ANY),
        scratch_shapes=[
            pltpu.VMEM((EXPERT_ROWS, ROW_SUBLANES, LANES), u32),
            pltpu.VMEM((3, MOVE_TILE, ROW_SUBLANES, LANES), u32),
            pltpu.SemaphoreType.DMA(()),
            pltpu.SemaphoreType.DMA((3,)),
        ],
    )
    return pl.pallas_call(
        functools.partial(_dispatch_kernel, n_blocks),
        grid_spec=grid_spec,
        out_shape=jax.ShapeDtypeStruct((n_rows, ROW_SUBLANES, LANES), u32),
        compiler_params=_params(("arbitrary",)),
        name="dispatch",
    )(pend, padded, n_used, dest_tiles, x2d, m_norm)


def _expert_kernel(n_blocks, ps_ref, nb_ref, nu_ref, xs_hbm, w1_ref, w3_ref, w2_ref,
                   ys_hbm, xbuf, ybuf, h_scr, w1_scr, w3_scr, w2_scr, xsem, ysem):
    e = pl.program_id(0)
    nblk = nb_ref[e]
    g0 = ps_ref[e]
    n_used = nu_ref[0]

    def rows(g):
        return pl.ds(pl.multiple_of(g * EXPERT_ROWS, EXPERT_ROWS), EXPERT_ROWS)

    def x_copy(g):
        return pltpu.make_async_copy(xs_hbm.at[rows(g)], xbuf.at[g % 2], xsem.at[g % 2])

    def y_copy(g):
        return pltpu.make_async_copy(ybuf.at[g % 2], ys_hbm.at[rows(g)], ysem.at[g % 2])

    def normed(slot):
        return _unpack_rows(xbuf[slot]).astype(bf16)

    @pl.when(e == 0)
    def _():
        x_copy(0).start(priority=1)
        x_copy(1).start(priority=1)
        x_copy(0).wait()
        h_scr[...] = normed(0)

    @pl.when(nblk > 0)
    def _():
        w1_scr[...] = w1_ref[...].astype(bf16)
        w3_scr[...] = w3_ref[...].astype(bf16)
        w2_scr[...] = w2_ref[...].astype(bf16)

        def body(g, h):
            @pl.when(g + 2 < n_used)
            def _():
                x_copy(g + 2).start(priority=1)

            @pl.when(g + 1 < n_used)
            def _():
                x_copy(g + 1).wait()

            @pl.when(g >= 2)
            def _():
                y_copy(g - 2).wait()

            h_next = normed((g + 1) % 2)
            y = None
            for c0 in range(0, D_EXPERT, D_EXPERT // 2):
                cols = slice(c0, c0 + D_EXPERT // 2)
                a = _dot(h, w1_scr[:, cols])
                c = _dot(h, w3_scr[:, cols])
                hid = ((a * jax.nn.sigmoid(a)) * c).astype(bf16)
                part = _dot(hid, w2_scr[cols, :])
                y = part if y is None else y + part
            ybuf[g % 2] = _pack_words(y)
            y_copy(g).start(priority=1)
            return h_next

        h_scr[...] = lax.fori_loop(g0, g0 + nblk, body, h_scr[...])

    @pl.when(e == pl.num_programs(0) - 1)
    def _():
        y_copy(n_used - 2).wait()
        y_copy(n_used - 1).wait()
        ybuf[0] = jnp.zeros(ybuf.shape[1:], u32)

        def fill(b):
            r = pl.ds(pl.multiple_of(b * EXPERT_ROWS, EXPERT_ROWS), EXPERT_ROWS)
            return pltpu.make_async_copy(ybuf.at[0], ys_hbm.at[r], ysem.at[0])

        def start(b, c):
            fill(b).start()
            return c

        def wait(b, c):
            fill(b).wait()
            return c

        lax.fori_loop(nu_ref[0], n_blocks, start, 0)
        lax.fori_loop(nu_ref[0], n_blocks, wait, 0)


def _experts(layer, pstart, seg_blocks, n_used, xs, w1, w3, w2):
    n_rows = xs.shape[0]
    n_blocks = n_rows // EXPERT_ROWS

    def w_map(e, *_):
        return (layer, e, 0, 0)

    block_buf = pltpu.VMEM((2, EXPERT_ROWS, ROW_SUBLANES, LANES), u32)
    out_buf = pltpu.VMEM((2, EXPERT_ROWS, D_MODEL // 2), u32)
    grid_spec = pltpu.PrefetchScalarGridSpec(
        num_scalar_prefetch=3,
        grid=(N_EXPERTS,),
        in_specs=[
            pl.BlockSpec(memory_space=pl.ANY),
            pl.BlockSpec((None, None, D_MODEL, D_EXPERT), w_map),
            pl.BlockSpec((None, None, D_MODEL, D_EXPERT), w_map),
            pl.BlockSpec((None, None, D_EXPERT, D_MODEL), w_map),
        ],
        out_specs=pl.BlockSpec(memory_space=pl.ANY),
        scratch_shapes=[
            block_buf,
            out_buf,
            pltpu.VMEM((EXPERT_ROWS, D_MODEL), bf16),
            pltpu.VMEM((D_MODEL, D_EXPERT), bf16),
            pltpu.VMEM((D_MODEL, D_EXPERT), bf16),
            pltpu.VMEM((D_EXPERT, D_MODEL), bf16),
            pltpu.SemaphoreType.DMA((2,)),
            pltpu.SemaphoreType.DMA((2,)),
        ],
    )
    return pl.pallas_call(
        functools.partial(_expert_kernel, n_blocks),
        grid_spec=grid_spec,
        out_shape=jax.ShapeDtypeStruct((n_rows, D_MODEL // 2), u32),
        compiler_params=_params(("arbitrary",)),
        name="experts",
    )(pstart, seg_blocks, n_used, xs, w1, w3, w2)


def _gather_rows(ys, dest):
    n_idx = dest.size
    words = ys.shape[1]
    mesh = plsc.VectorSubcoreMesh(core_axis_name="core", subcore_axis_name="subcore")

    @pl.kernel(out_type=jax.ShapeDtypeStruct((n_idx, words), u32), mesh=mesh)
    def gather(src_hbm, idx_hbm, out_hbm):
        for slab in range(words // GATHER_WIDTH):
            def body(idx_vmem, out_vmem, slab=slab):
                cols = pl.ds(slab * GATHER_WIDTH, GATHER_WIDTH)
                pltpu.sync_copy(src_hbm.at[:, cols].at[idx_vmem.at[0]], out_vmem)

            pltpu.emit_pipeline(
                body,
                grid=(n_idx // GATHER_WINDOW,),
                in_specs=[pl.BlockSpec((1, GATHER_WINDOW), index_map=lambda i: (0, i))],
                out_specs=[pl.BlockSpec((GATHER_WINDOW, GATHER_WIDTH),
                                        index_map=lambda i, slab=slab: (i, slab))],
                core_axis_name=("core", "subcore"),
                dimension_semantics=(pltpu.PARALLEL,),
            )(idx_hbm, out_hbm)

    return gather(ys, dest.reshape(1, n_idx)).reshape(dest.shape + (words,))


def _combined(x_ref, w_ref, y0_ref, y1_ref):
    w = w_ref[...]
    y0 = _unpack_words(y0_ref[...])
    y1 = _unpack_words(y1_ref[...])
    return x_ref[...] + (w[:, 0:1] * y0 + w[:, 1:2] * y1)


def _combine_specs(rows):
    return [
        pl.BlockSpec((rows, D_MODEL), lambda i: (i, 0)),
        pl.BlockSpec((rows, 8), lambda i: (i, 0)),
        pl.BlockSpec((None, rows, D_MODEL // 2), lambda i: (0, i, 0)),
        pl.BlockSpec((None, rows, D_MODEL // 2), lambda i: (1, i, 0)),
    ]


def _combine_kernel(x_ref, w_ref, y0_ref, y1_ref, out_ref):
    out_ref[...] = _combined(x_ref, w_ref, y0_ref, y1_ref)


def _combine(x2d, w_cols, yg):
    T = x2d.shape[0]
    n_tiles = T // MOVE_TILE
    return pl.pallas_call(
        _combine_kernel,
        grid=(n_tiles,),
        in_specs=_combine_specs(MOVE_TILE),
        out_specs=pl.BlockSpec((MOVE_TILE, D_MODEL), lambda i: (i, 0)),
        out_shape=jax.ShapeDtypeStruct((T, D_MODEL), f32),
        compiler_params=_params(("parallel",)),
        name="combine",
    )(x2d, w_cols, yg, yg)


def _moe(layer, x2d, logits_t, m_norm, b_group, b_expert, w1, w3, w2):
    T = x2d.shape[0]
    n_blocks = (T * TOP_K) // EXPERT_ROWS + N_EXPERTS
    n_rows = n_blocks * EXPERT_ROWS

    be = b_expert.reshape(N_EXPERTS, 1)
    bg = jnp.concatenate([b_group, jnp.zeros((8 - N_GROUPS,), f32)]).reshape(8, 1)
    oi, w_cols, cnt = _route(logits_t, be, bg)

    counts = cnt[:, 0].astype(jnp.int32)
    padded = (counts + EXPERT_ROWS - 1) // EXPERT_ROWS * EXPERT_ROWS
    pend = jnp.cumsum(padded)
    pstart = pend - padded
    n_used = (pend[-1] // EXPERT_ROWS).astype(jnp.int32).reshape(1)
    e_ids = jnp.arange(N_EXPERTS, dtype=jnp.int32)[:, None, None]
    seg_start = jnp.sum(jnp.where(oi[None, 0:2] == e_ids, pstart[:, None, None], 0), axis=0)
    dest = seg_start + oi[2:4]
    dest_tiles = dest.reshape(TOP_K, T // MOVE_TILE, MOVE_TILE).transpose(1, 0, 2).reshape(
        T // MOVE_TILE, 1, TOP_K * MOVE_TILE)

    xs = _dispatch(pend.astype(jnp.int32), padded, n_used, dest_tiles, x2d, m_norm, n_rows)
    ys = _experts(layer, (pstart // EXPERT_ROWS).astype(jnp.int32), padded // EXPERT_ROWS,
                  n_used, xs, w1, w3, w2)
    return w_cols, _gather_rows(ys, dest)


def _log_sigmoid(x):
    return jnp.minimum(x, 0.0) - jnp.log1p(jnp.exp(-jnp.abs(x)))


def _proj_kernel(tiles_per_seq, x_ref, w_ref, y0_ref, y1_ref, kvn_ref, bn_ref, kvw_ref, bf_ref,
                 qgw_ref, gq_ref, gk_ref, x_out_ref, q_ref, kt_ref, v_ref, sg_ref, carry_scr,
                 wkt_ref, wv_ref, wf_ref, wq_ref, wg_ref):
    i = pl.program_id(0)

    @pl.when(i == 0)
    def _():
        wkt_ref[...] = kvw_ref[:D_MODEL, :].astype(bf16)
        wv_ref[...] = kvw_ref[D_MODEL:2 * D_MODEL, :].T.astype(bf16)
        wf_ref[...] = jnp.concatenate(
            [kvw_ref[2 * D_MODEL:, :].T, jnp.zeros((D_MODEL, LANES - N_HEADS), f32)],
            axis=1).astype(bf16)
        wg_ref[...] = qgw_ref[:, D_MODEL:].astype(bf16)
        zeros = jnp.zeros((D_MODEL, HEAD_PAD - HEAD_DIM), f32)
        for h in range(N_HEADS):
            wq_ref[:, h * HEAD_PAD:(h + 1) * HEAD_PAD] = jnp.concatenate(
                [qgw_ref[:, h * HEAD_DIM:(h + 1) * HEAD_DIM], zeros], axis=1).astype(bf16)

    x = _combined(x_ref, w_ref, y0_ref, y1_ref)
    x_out_ref[...] = x
    hkv = _rms(x, kvn_ref[...]).astype(bf16)
    hq = _rms(x, bn_ref[...]).astype(bf16)

    v_ref[...] = _dot(hkv, wv_ref[...]).astype(bf16)
    sg_ref[...] = jax.nn.sigmoid(_dot(hq, wg_ref[...])).astype(bf16)

    logf = _log_sigmoid(_dot(hkv, wf_ref[...]) + bf_ref[...])
    tm = x.shape[0]
    incl = (lax.broadcasted_iota(jnp.int32, (tm, tm), 0)
            >= lax.broadcasted_iota(jnp.int32, (tm, tm), 1)).astype(bf16)

    @pl.when(i % tiles_per_seq == 0)
    def _():
        carry_scr[...] = jnp.zeros_like(carry_scr)

    parts = _dot(incl, jnp.concatenate(_split3(logf), axis=1).astype(bf16))
    cum = (parts[:, :LANES] + parts[:, LANES:2 * LANES] + parts[:, 2 * LANES:]
           + carry_scr[...])
    carry_scr[...] = cum[tm - 1:tm, :]
    cum = cum * LOG2E
    cum_t = cum.T

    lane = lax.broadcasted_iota(jnp.int32, (tm, HEAD_PAD), 1)
    sub = lax.broadcasted_iota(jnp.int32, (HEAD_DIM, tm), 0)
    q_raw = _dot(hq, wq_ref[...])
    k_raw_t = _dot_nt(wkt_ref[...], hkv)
    scale = HEAD_DIM ** -0.5 * LOG2E

    for h in range(N_HEADS):
        sl = slice(h * HEAD_PAD, (h + 1) * HEAD_PAD)
        qb = q_raw[:, sl]
        q_ms = jnp.sum(qb * qb, axis=-1, keepdims=True) * (1.0 / HEAD_DIM)
        qn = qb * lax.rsqrt(q_ms + EPS) * gq_ref[...] * scale
        c_hi, c_mid, c_lo = _split3(cum[:, h:h + 1])
        qa = jnp.where(lane < HEAD_DIM, qn,
             jnp.where(lane < AUX_CUM0, 1.0,
             jnp.where(lane == AUX_CUM0, c_hi,
             jnp.where(lane == AUX_CUM0 + 1, c_mid,
             jnp.where(lane == AUX_CUM0 + 2, c_lo, 0.0)))))
        q_ref[:, sl] = qa.astype(bf16)

        kb = k_raw_t[h * HEAD_DIM:(h + 1) * HEAD_DIM, :]
        k_ms = jnp.sum(kb * kb, axis=0, keepdims=True) * (1.0 / HEAD_DIM)
        kn = kb * lax.rsqrt(k_ms + EPS) * gk_ref[...]
        t_hi, t_mid, t_lo = _split3(cum_t[h:h + 1, :])
        aux = jnp.where(sub == 0, -t_hi,
              jnp.where(sub == 1, -t_mid,
              jnp.where(sub == 2, -t_lo,
              jnp.where(sub < 6, 1.0, 0.0))))
        sl = slice(h * HEAD_PAD, (h + 1) * HEAD_PAD)
        ka = jnp.concatenate([kn, aux], axis=0).astype(bf16)
        for t in range(tm // ATT_K):
            kt_ref[t, sl, :] = ka[:, t * ATT_K:(t + 1) * ATT_K]


def _proj(x2d, w_cols, yg, seq, kv_norm, b_norm, kv_w, bfv, w_qg, gq, gk):
    T = x2d.shape[0]
    n_tiles = T // ROW_TILE
    qw = N_HEADS * HEAD_PAD
    return pl.pallas_call(
        functools.partial(_proj_kernel, seq // ROW_TILE),
        grid=(n_tiles,),
        in_specs=_combine_specs(ROW_TILE) + [
            _const_spec((1, D_MODEL)),
            _const_spec((1, D_MODEL)),
            _const_spec(kv_w.shape, single=True),
            _const_spec((1, LANES)),
            _const_spec(w_qg.shape, single=True),
            _const_spec((1, HEAD_PAD)),
            _const_spec((HEAD_DIM, 1)),
        ],
        out_specs=[
            pl.BlockSpec((ROW_TILE, D_MODEL), lambda i: (i, 0)),
            pl.BlockSpec((ROW_TILE, qw), lambda i: (i, 0)),
            pl.BlockSpec((ROW_TILE // ATT_K, qw, ATT_K), lambda i: (i, 0, 0)),
            pl.BlockSpec((ROW_TILE, D_MODEL), lambda i: (i, 0)),
            pl.BlockSpec((ROW_TILE, D_MODEL), lambda i: (i, 0)),
        ],
        out_shape=[
            jax.ShapeDtypeStruct((T, D_MODEL), f32),
            jax.ShapeDtypeStruct((T, qw), bf16),
            jax.ShapeDtypeStruct((T // ATT_K, qw, ATT_K), bf16),
            jax.ShapeDtypeStruct((T, D_MODEL), bf16),
            jax.ShapeDtypeStruct((T, D_MODEL), bf16),
        ],
        scratch_shapes=[
            pltpu.VMEM((1, LANES), f32),
            pltpu.VMEM((D_MODEL, D_MODEL), bf16),
            pltpu.VMEM((D_MODEL, D_MODEL), bf16),
            pltpu.VMEM((D_MODEL, LANES), bf16),
            pltpu.VMEM((D_MODEL, qw), bf16),
            pltpu.VMEM((D_MODEL, D_MODEL), bf16),
        ],
        compiler_params=_params(("arbitrary",)),
        name="proj",
    )(x2d, w_cols, yg, yg, kv_norm, b_norm, kv_w, bfv, w_qg, gq, gk)


def _attn_kernel(q_ref, kt_ref, v_ref, o_ref, s_scr, m_scr, l_scr, acc_scr):
    qi = pl.program_id(2)
    tiles = ATT_Q // ATT_K
    groups = ATT_K // LANES
    heads = [slice(hh * HEAD_PAD, (hh + 1) * HEAD_PAD) for hh in range(2)]
    m_scr[...] = jnp.full(m_scr.shape, -jnp.inf, f32)

    def lane_max(mx, s):
        for g in range(groups):
            mx = jnp.maximum(mx, s[:, g * LANES:(g + 1) * LANES])
        return mx

    def probs(s, mb):
        ps = [jnp.exp2(s[:, g * LANES:(g + 1) * LANES] - mb) for g in range(groups)]
        return ps, functools.reduce(lambda a, b: a + b, ps)

    def body_a(trip, c):
        for t in range(tiles):
            kt_idx = trip * tiles + t
            for hh in range(2):
                s = _dot(q_ref[:, heads[hh]], kt_ref[kt_idx, heads[hh], :])
                s_scr[hh, kt_idx] = s
                m_scr[hh] = lane_max(m_scr[hh], s)
        return c

    lax.fori_loop(0, qi, body_a, 0)
    for t in range(tiles):
        kt_idx = qi * tiles + t
        rows = ATT_Q - t * ATT_K
        visible = (lax.broadcasted_iota(jnp.int32, (rows, ATT_K), 1)
                   <= lax.broadcasted_iota(jnp.int32, (rows, ATT_K), 0))
        for hh in range(2):
            s = _dot(q_ref[t * ATT_K:, heads[hh]], kt_ref[kt_idx, heads[hh], :])
            s = jnp.where(visible, s, -jnp.inf)
            s_scr[hh, kt_idx, t * ATT_K:, :] = s
            m_scr[hh, t * ATT_K:, :] = lane_max(m_scr[hh, t * ATT_K:, :], s)

    for hh in range(2):
        row_max = jnp.max(m_scr[hh], axis=-1, keepdims=True)
        m_scr[hh] = jnp.broadcast_to(row_max, (ATT_Q, LANES))
    l_scr[...] = jnp.zeros(l_scr.shape, f32)
    acc_scr[...] = jnp.zeros(acc_scr.shape, f32)

    def body_b(trip, c):
        for hh in range(2):
            mb = m_scr[hh]
            lsum, acc = l_scr[hh], acc_scr[hh]
            for t0 in range(0, tiles, PV_TILES):
                kt_idx = trip * tiles + t0
                row0 = pl.multiple_of(kt_idx * ATT_K, PV_TILES * ATT_K)
                vb = v_ref[pl.ds(row0, PV_TILES * ATT_K), :]
                parts = []
                for t in range(PV_TILES):
                    ps, psum = probs(s_scr[hh, kt_idx + t], mb)
                    parts += ps
                    lsum = lsum + psum
                acc = acc + _dot(jnp.concatenate(parts, axis=1).astype(bf16), vb)
            l_scr[hh] = lsum
            acc_scr[hh] = acc
        return c

    lax.fori_loop(0, qi, body_b, 0)
    for t in range(tiles):
        kt_idx = qi * tiles + t
        row0 = pl.multiple_of(kt_idx * ATT_K, ATT_K)
        vb = v_ref[pl.ds(row0, ATT_K), :]
        for hh in range(2):
            ps, psum = probs(s_scr[hh, kt_idx, t * ATT_K:, :], m_scr[hh, t * ATT_K:, :])
            l_scr[hh, t * ATT_K:, :] += psum
            acc_scr[hh, t * ATT_K:, :] += _dot(jnp.concatenate(ps, axis=1).astype(bf16), vb)

    lane = lax.broadcasted_iota(jnp.int32, (ATT_Q, 2 * HEAD_DIM), 1)
    o0 = acc_scr[0] / jnp.sum(l_scr[0], axis=-1, keepdims=True)
    o1 = acc_scr[1] / jnp.sum(l_scr[1], axis=-1, keepdims=True)
    o_ref[...] = jnp.where(lane < HEAD_DIM, o0, o1).astype(bf16)


def _attention(q, kt, v, batch, seq):
    T = q.shape[0]
    nq = seq // ATT_Q
    nk = seq // ATT_K
    pairs = N_HEADS // 2
    return pl.pallas_call(
        _attn_kernel,
        grid=(batch, pairs, nq),
        in_specs=[
            pl.BlockSpec((ATT_Q, 2 * HEAD_PAD), lambda b, p, i: (b * nq + i, p)),
            pl.BlockSpec((nk, 2 * HEAD_PAD, ATT_K), lambda b, p, i: (b, p, 0)),
            pl.BlockSpec((seq, 2 * HEAD_DIM), lambda b, p, i: (b, p)),
        ],
        out_specs=pl.BlockSpec((ATT_Q, 2 * HEAD_DIM), lambda b, p, i: (b * nq + i, p)),
        out_shape=jax.ShapeDtypeStruct((T, D_MODEL), bf16),
        scratch_shapes=[
            pltpu.VMEM((2, nk, ATT_Q, ATT_K), f32),
            pltpu.VMEM((2, ATT_Q, LANES), f32),
            pltpu.VMEM((2, ATT_Q, LANES), f32),
            pltpu.VMEM((2, ATT_Q, 2 * HEAD_DIM), f32),
        ],
        compiler_params=_params(("parallel", "parallel", "parallel")),
        name="attn",
    )(q, kt, v)


def _attn_out_kernel(x_ref, o_ref, sg_ref, wo_ref, mn_ref, wr_ref, x3_ref, lg_ref):
    for r0 in range(0, SGU_TILE, EPILOGUE_ROWS):
        rows = slice(r0, r0 + EPILOGUE_ROWS)
        gated = (o_ref[rows, :].astype(f32) * sg_ref[rows, :].astype(f32)).astype(bf16)
        x3 = x_ref[rows, :] + _dot(gated, wo_ref[...])
        x3_ref[rows, :] = x3
        lg_ref[:, rows] = _router_logits_t(x3, mn_ref[...], wr_ref[...])


def _attn_out(x2d, o, sg, wo, m_norm, w_router):
    T = x2d.shape[0]
    return pl.pallas_call(
        _attn_out_kernel,
        grid=(T // SGU_TILE,),
        in_specs=[
            pl.BlockSpec((SGU_TILE, D_MODEL), lambda i: (i, 0)),
            pl.BlockSpec((SGU_TILE, D_MODEL), lambda i: (i, 0)),
            pl.BlockSpec((SGU_TILE, D_MODEL), lambda i: (i, 0)),
            _const_spec((D_MODEL, D_MODEL)),
            _const_spec((1, D_MODEL)),
            _const_spec((2 * D_MODEL, 2 * LANES)),
        ],
        out_specs=[
            pl.BlockSpec((SGU_TILE, D_MODEL), lambda i: (i, 0)),
            pl.BlockSpec((ROUTER_ROWS, SGU_TILE), lambda i: (0, i)),
        ],
        out_shape=[
            jax.ShapeDtypeStruct((T, D_MODEL), f32),
            jax.ShapeDtypeStruct((ROUTER_ROWS, T), f32),
        ],
        compiler_params=_params(("parallel",)),
        name="attn_out",
    )(x2d, o, sg, wo, m_norm, w_router)


def _router_weight(w_group, w_expert):
    pad = jnp.zeros((D_MODEL, LANES - N_EXPERTS - N_GROUPS), f32)
    w = jnp.concatenate([w_expert, w_group, pad], axis=1)
    w_hi = w.astype(bf16)
    w_lo = (w - w_hi.astype(f32)).astype(bf16)
    top = jnp.concatenate([w_hi, w_lo], axis=1)
    bottom = jnp.concatenate([w_hi, jnp.zeros_like(w_hi)], axis=1)
    return jnp.concatenate([top, bottom], axis=0)


def kernel(x, a_norm, a_w_in, a_b_in, a_v_norm, a_w_s, a_b_s, a_w_out, kv_norm, kv_w, kv_b_f,
           k_norm, b_norm, b_w_qg, q_norm, b_w_out, m_norm, m_w_group, m_b_group, m_w_expert,
           m_b_expert, m_w1, m_w3, m_w2):
    batch, seq, _ = x.shape
    T = batch * seq
    x2d = x.reshape(T, D_MODEL)

    x1, lg0 = _sgu(
        x2d, a_norm[0].reshape(1, -1), a_w_in[0], a_b_in[0].reshape(1, -1),
        a_v_norm[0].reshape(1, -1), a_w_s[0], a_b_s[0].T, a_w_out[0].astype(bf16),
        m_norm[0].reshape(1, -1), _router_weight(m_w_group[0], m_w_expert[0]))
    w_cols0, yg0 = _moe(0, x1, lg0, m_norm[0].reshape(1, -1), m_b_group[0], m_b_expert[0],
                        m_w1, m_w3, m_w2)

    bfv = jnp.pad(kv_b_f, (0, LANES - N_HEADS)).reshape(1, LANES)
    gq = jnp.pad(q_norm[0], (0, HEAD_PAD - HEAD_DIM)).reshape(1, HEAD_PAD)
    gk = k_norm.reshape(HEAD_DIM, 1)
    x2, q, kt, v, sg = _proj(x1, w_cols0, yg0, seq, kv_norm.reshape(1, -1),
                             b_norm[0].reshape(1, -1), kv_w.T, bfv, b_w_qg[0], gq, gk)
    o = _attention(q, kt, v, batch, seq)
    x3, lg1 = _attn_out(x2, o, sg, b_w_out[0].astype(bf16), m_norm[1].reshape(1, -1),
                        _router_weight(m_w_group[1], m_w_expert[1]))
    w_cols1, yg1 = _moe(1, x3, lg1, m_norm[1].reshape(1, -1), m_b_group[1], m_b_expert[1],
                        m_w1, m_w3, m_w2)
    x4 = _combine(x3, w_cols1, yg1)
    return x4.reshape(batch, seq, D_MODEL)
```

```python
import functools
import math

import jax
import jax.numpy as jnp
import numpy as np
from jax import lax
from jax.experimental import pallas as pl
from jax.experimental.pallas import tpu as pltpu
from jax.experimental.pallas import tpu_sc as plsc

D_MODEL = 1024
EPS = 1e-6
SGU_BLOCK = 128
SGU_CHUNK = 64
CHUNK_SHIFT = SGU_CHUNK.bit_length() - 1
SGU_WIDTH = 2 * D_MODEL
SGU_GROUPS = 8
SGU_GROUP_DIM = SGU_WIDTH // SGU_GROUPS
N_HEADS = 16
HEAD_DIM = D_MODEL // N_HEADS
N_GROUPS = 4
EXPERTS_PER_GROUP = 8
N_EXPERTS = N_GROUPS * EXPERTS_PER_GROUP
GROUP_SHIFT = EXPERTS_PER_GROUP.bit_length() - 1
TOP_K = 2
D_EXPERT = D_MODEL // 2

LANES = 128
ROW_SUBLANES = D_MODEL // 2 // LANES
ISSUE_UNROLL = 8
VMEM_LIMIT_BYTES = 56 * 1024 * 1024

ROW_TILE = 256
SGU_TILE = 512
EPILOGUE_ROWS = 256
ROUTE_TILE = 1024
EXPERT_ROWS = 512
MOVE_TILE = 512
MOVE_CHUNK = 64
GATHER_WINDOW = 128
GATHER_WIDTH = 256
COMBINE_CHUNKS = 2
ATT_Q = 1024
ATT_K = 256
PV_TILES = 2
HEAD_PAD = 2 * HEAD_DIM
ROUTER_ROWS = 40
AUX_ONE0 = HEAD_DIM
AUX_CUM0 = HEAD_DIM + 3

LOG2E = math.log2(math.e)

bf16 = jnp.bfloat16
f32 = jnp.float32
u32 = jnp.uint32
HI_HALF = np.uint32(0xFFFF0000)


def _dot(a, b, precision=None):
    return jnp.dot(a, b, preferred_element_type=f32, precision=precision)


def _dot_nt(a, b, precision=None):
    return lax.dot_general(a, b, (((1,), (1,)), ((), ())), preferred_element_type=f32,
                           precision=precision)


def _rms(x, g):
    ms = jnp.mean(x * x, axis=-1, keepdims=True)
    return x * lax.rsqrt(ms + EPS) * g


def _gelu_tanh(z):
    c = math.sqrt(2.0 / math.pi)
    return z * (0.5 * (1.0 + jnp.tanh(c * (z + 0.044715 * (z * z * z)))))


def _split3(c):
    hi = c.astype(bf16).astype(f32)
    r = c - hi
    mid = r.astype(bf16).astype(f32)
    lo = r - mid
    return hi, mid, lo


def _pack_words(v):
    half = D_MODEL // 2
    lo = lax.bitcast_convert_type(v[:, :half].astype(bf16).astype(f32), u32) >> 16
    hi = lax.bitcast_convert_type(v[:, half:].astype(bf16).astype(f32), u32) & HI_HALF
    return hi | lo


def _unpack_words(w):
    lo = lax.bitcast_convert_type(w << 16, f32)
    hi = lax.bitcast_convert_type(w & HI_HALF, f32)
    return jnp.concatenate([lo, hi], axis=1)


def _pack_rows(v):
    return _pack_words(v).reshape(v.shape[0], ROW_SUBLANES, LANES)


def _unpack_rows(w):
    return _unpack_words(w.reshape(w.shape[0], D_MODEL // 2))


def _const_spec(shape, single=False):
    nd = len(shape)
    mode = pl.Buffered(1) if single else None
    return pl.BlockSpec(shape, lambda *_: (0,) * nd, pipeline_mode=mode)


def _params(sem):
    return pltpu.CompilerParams(dimension_semantics=sem, vmem_limit_bytes=VMEM_LIMIT_BYTES)


def _router_logits_t(x_new, mn, wr3):
    hm = _rms(x_new, mn)
    h_hi = hm.astype(bf16)
    h_lo = (hm - h_hi.astype(f32)).astype(bf16)
    both = _dot(jnp.concatenate([h_hi, h_lo], axis=1), wr3)
    lg = both[:, :LANES] + both[:, LANES:]
    return lg.T[:ROUTER_ROWS, :]


def _sgu_kernel(x_ref, an_ref, win_f32_ref, bin_ref, vn_ref, ws_ref, bst_ref, wout_ref, mn_ref,
                wr_ref, x1_ref, lg_ref, u_scr, v_scr, gated_scr, win_ref):
    @pl.when(pl.program_id(0) == 0)
    def _():
        win_ref[...] = win_f32_ref[...].astype(bf16)

    x = x_ref[...]
    h = _rms(x, an_ref[...]).astype(bf16)
    v = _gelu_tanh(_dot(h, win_ref[:, SGU_WIDTH:]) + bin_ref[:, SGU_WIDTH:])
    v_scr[...] = _rms(v, vn_ref[...]).astype(bf16)
    u_scr[...] = _gelu_tanh(_dot(h, win_ref[:, :SGU_WIDTH]) + bin_ref[:, :SGU_WIDTH])

    t_chunk = lax.broadcasted_iota(jnp.int32, (SGU_BLOCK, SGU_BLOCK), 0) >> CHUNK_SHIFT
    s_chunk = lax.broadcasted_iota(jnp.int32, (SGU_BLOCK, SGU_BLOCK), 1) >> CHUNK_SHIFT
    causal = t_chunk >= s_chunk
    wsm = [jnp.where(causal, ws_ref[g], 0.0).astype(bf16) for g in range(SGU_GROUPS)]
    for r0 in range(0, SGU_TILE, EPILOGUE_ROWS):
        for g in range(SGU_GROUPS):
            cols = slice(g * SGU_GROUP_DIM, (g + 1) * SGU_GROUP_DIM)
            for sb in range(r0 // SGU_BLOCK, (r0 + EPILOGUE_ROWS) // SGU_BLOCK):
                rows = slice(sb * SGU_BLOCK, (sb + 1) * SGU_BLOCK)
                mixed = _dot(wsm[g], v_scr[rows, cols]) + bst_ref[:, g:g + 1]
                gated_scr[rows, cols] = (u_scr[rows, cols] * mixed).astype(bf16)
        half = slice(r0, r0 + EPILOGUE_ROWS)
        x1 = x[half, :] + _dot(gated_scr[half, :], wout_ref[...])
        x1_ref[half, :] = x1
        lg_ref[:, half] = _router_logits_t(x1, mn_ref[...], wr_ref[...])


def _sgu(x2d, a_norm, w_in, b_in, v_norm, w_s, b_s_t, w_out, m_norm, w_router):
    T = x2d.shape[0]
    return pl.pallas_call(
        _sgu_kernel,
        grid=(T // SGU_TILE,),
        in_specs=[
            pl.BlockSpec((SGU_TILE, D_MODEL), lambda i: (i, 0)),
            _const_spec((1, D_MODEL)),
            _const_spec((D_MODEL, 2 * SGU_WIDTH), single=True),
            _const_spec((1, 2 * SGU_WIDTH)),
            _const_spec((1, SGU_WIDTH)),
            _const_spec((SGU_GROUPS, SGU_BLOCK, SGU_BLOCK)),
            _const_spec((SGU_BLOCK, SGU_GROUPS)),
            _const_spec((SGU_WIDTH, D_MODEL), single=True),
            _const_spec((1, D_MODEL)),
            _const_spec((2 * D_MODEL, 2 * LANES)),
        ],
        out_specs=[
            pl.BlockSpec((SGU_TILE, D_MODEL), lambda i: (i, 0)),
            pl.BlockSpec((ROUTER_ROWS, SGU_TILE), lambda i: (0, i)),
        ],
        out_shape=[
            jax.ShapeDtypeStruct((T, D_MODEL), f32),
            jax.ShapeDtypeStruct((ROUTER_ROWS, T), f32),
        ],
        scratch_shapes=[
            pltpu.VMEM((SGU_TILE, SGU_WIDTH), f32),
            pltpu.VMEM((SGU_TILE, SGU_WIDTH), bf16),
            pltpu.VMEM((SGU_TILE, SGU_WIDTH), bf16),
            pltpu.VMEM((D_MODEL, 2 * SGU_WIDTH), bf16),
        ],
        compiler_params=_params(("arbitrary",)),
        name="sgu",
    )(x2d, a_norm, w_in, b_in, v_norm, w_s, b_s_t, w_out, m_norm, w_router)


def _route_kernel(lg_ref, be_ref, bg_ref, oi_ref, ow_ref, cnt_ref, carry_scr, before_scr,
                  w_scr):
    step = pl.program_id(0)
    tt = lg_ref.shape[1]

    @pl.when(step == 0)
    def _():
        carry_scr[...] = jnp.zeros_like(carry_scr)
        before_scr[...] = (lax.broadcasted_iota(jnp.int32, (tt, tt), 0)
                           < lax.broadcasted_iota(jnp.int32, (tt, tt), 1)).astype(bf16)

    lg = lg_ref[...]
    e_l = lg[0:N_EXPERTS, :] + be_ref[...]
    g_l = lg[N_EXPERTS:N_EXPERTS + 8, :] + bg_ref[...]
    g_row = lax.broadcasted_iota(jnp.int32, g_l.shape, 0).astype(f32)
    g_l = jnp.where(g_row < N_GROUPS, g_l, -jnp.inf)
    g_max = jnp.max(g_l, axis=0, keepdims=True)
    g_sel = jnp.min(jnp.where(g_l == g_max, g_row, 8.0), axis=0, keepdims=True)
    g_den = jnp.sum(jnp.exp(g_l - g_max), axis=0, keepdims=True)
    g_w = 1.0 / g_den

    e_row_i = lax.broadcasted_iota(jnp.int32, e_l.shape, 0)
    e_row = e_row_i.astype(f32)
    e_grp = (e_row_i >> GROUP_SHIFT).astype(f32)
    e_in = jnp.where(e_grp == g_sel, e_l, -jnp.inf)
    m1 = jnp.max(e_in, axis=0, keepdims=True)
    i1 = jnp.min(jnp.where(e_in == m1, e_row, float(N_EXPERTS)), axis=0, keepdims=True)
    e_in2 = jnp.where(e_row == i1, -jnp.inf, e_in)
    m2 = jnp.max(e_in2, axis=0, keepdims=True)
    i2 = jnp.min(jnp.where(e_in2 == m2, e_row, float(N_EXPERTS)), axis=0, keepdims=True)
    t = jnp.exp(m2 - m1)
    p1 = 1.0 / (1.0 + t)
    p2 = t / (1.0 + t)

    hit1 = e_row == i1
    hit2 = e_row == i2
    member = jnp.logical_or(hit1, hit2)
    prefix = _dot(member.astype(bf16), before_scr[...])
    rank_full = prefix + carry_scr[...]
    rank1 = jnp.sum(jnp.where(hit1, rank_full, 0.0), axis=0, keepdims=True)
    rank2 = jnp.sum(jnp.where(hit2, rank_full, 0.0), axis=0, keepdims=True)
    carry_new = carry_scr[...] + jnp.sum(member.astype(f32), axis=1, keepdims=True)
    carry_scr[...] = carry_new

    oi_ref[...] = jnp.zeros(oi_ref.shape, jnp.int32)
    oi_ref[0:1, :] = i1.astype(jnp.int32)
    oi_ref[1:2, :] = i2.astype(jnp.int32)
    oi_ref[2:3, :] = rank1.astype(jnp.int32)
    oi_ref[3:4, :] = rank2.astype(jnp.int32)
    w_scr[...] = jnp.zeros(w_scr.shape, f32)
    w_scr[0:1, :] = g_w * p1
    w_scr[1:2, :] = g_w * p2
    ow_ref[...] = w_scr[...].T
    cnt_ref[...] = jnp.broadcast_to(carry_new, cnt_ref.shape)


def _route(logits_t, b_expert, b_group):
    T = logits_t.shape[1]
    return pl.pallas_call(
        _route_kernel,
        grid=(T // ROUTE_TILE,),
        in_specs=[
            pl.BlockSpec((ROUTER_ROWS, ROUTE_TILE), lambda i: (0, i)),
            _const_spec((N_EXPERTS, 1)),
            _const_spec((8, 1)),
        ],
        out_specs=[
            pl.BlockSpec((8, ROUTE_TILE), lambda i: (0, i)),
            pl.BlockSpec((ROUTE_TILE, 8), lambda i: (i, 0)),
            _const_spec((N_EXPERTS, LANES)),
        ],
        out_shape=[
            jax.ShapeDtypeStruct((8, T), jnp.int32),
            jax.ShapeDtypeStruct((T, 8), f32),
            jax.ShapeDtypeStruct((N_EXPERTS, LANES), f32),
        ],
        scratch_shapes=[pltpu.VMEM((N_EXPERTS, 1), f32),
                        pltpu.VMEM((ROUTE_TILE, ROUTE_TILE), bf16),
                        pltpu.VMEM((8, ROUTE_TILE), f32)],
        compiler_params=_params(("arbitrary",)),
        name="route",
    )(logits_t, b_expert, b_group)


def _row_copy(src, s, dst, d, sem):
    return pltpu.make_async_copy(src.at[s], dst.at[d], sem)


def _wait_rows(buf, sem):
    pltpu.make_async_copy(buf, buf, sem).wait()


def _dispatch_kernel(n_blocks, pend_ref, padded_ref, nu_ref, dest_ref, x_ref, mn_ref, xs_hbm,
                     zero_scr, x3_scr, zsem, sem):
    @pl.when(pl.program_id(0) == 0)
    def _():
        zero_scr[...] = jnp.zeros_like(zero_scr)

        def fill(row0):
            return pltpu.make_async_copy(zero_scr, xs_hbm.at[pl.ds(row0, EXPERT_ROWS)], zsem)

        for e in range(N_EXPERTS):
            @pl.when(padded_ref[e] > 0)
            def _():
                fill(pend_ref[e] - EXPERT_ROWS).start()

            @pl.when(nu_ref[0] + e < n_blocks)
            def _():
                fill((nu_ref[0] + e) * EXPERT_ROWS).start()

        for e in range(N_EXPERTS):
            @pl.when(padded_ref[e] > 0)
            def _():
                fill(0).wait()

            @pl.when(nu_ref[0] + e < n_blocks)
            def _():
                fill(0).wait()

    i = pl.program_id(0)
    n_tiles = pl.num_programs(0) - 1

    def staged(rows):
        return _pack_rows(_rms(x_ref[rows, :], mn_ref[...]))

    @pl.when(i == 0)
    def _():
        x3_scr[0] = staged(slice(None))

    @pl.when(i > 0)
    def _():
        cur = (i - 1) % 3
        nxt = i % 3

        def chunk(c, carry):
            r0 = pl.multiple_of(c * MOVE_CHUNK, MOVE_CHUNK)
            for j in range(MOVE_CHUNK):
                for k in range(TOP_K):
                    _row_copy(x3_scr.at[cur], r0 + j, xs_hbm,
                              dest_ref[0, k * MOVE_TILE + r0 + j], sem.at[cur]).start(priority=k)
            x3_scr[nxt, pl.ds(r0, MOVE_CHUNK)] = staged(pl.ds(r0, MOVE_CHUNK))
            return carry

        lax.fori_loop(0, MOVE_TILE // MOVE_CHUNK, chunk, 0)

        @pl.when(i > 1)
        def _():
            for k in range(TOP_K):
                _wait_rows(x3_scr.at[(i - 2) % 3], sem.at[(i - 2) % 3])

        @pl.when(i == n_tiles)
        def _():
            for k in range(TOP_K):
                _wait_rows(x3_scr.at[cur], sem.at[cur])


def _dispatch(pend, padded, n_used, dest_tiles, x2d, m_norm, n_rows):
    T = x2d.shape[0]
    n_blocks = n_rows // EXPERT_ROWS
    n_tiles = T // MOVE_TILE
    grid_spec = pltpu.PrefetchScalarGridSpec(
        num_scalar_prefetch=3,
        grid=(n_tiles + 1,),
        in_specs=[
            pl.BlockSpec((None, 1, TOP_K * MOVE_TILE),
                         lambda i, *_: (jnp.maximum(i - 1, 0), 0, 0), memory_space=pltpu.SMEM),
            pl.BlockSpec((MOVE_TILE, D_MODEL), lambda i, *_: (jnp.minimum(i, n_tiles - 1), 0)),
            pl.BlockSpec((1, D_MODEL), lambda i, *_: (0, 0)),
        ],
        out_specs=pl.BlockSpec(memory_space=pl.ANY),
        scratch_shapes=[
            pltpu.VMEM((EXPERT_ROWS, ROW_SUBLANES, LANES), u32),
            pltpu.VMEM((3, MOVE_TILE, ROW_SUBLANES, LANES), u32),
            pltpu.SemaphoreType.DMA(()),
            pltpu.SemaphoreType.DMA((3,)),
        ],
    )
    return pl.pallas_call(
        functools.partial(_dispatch_kernel, n_blocks),
        grid_spec=grid_spec,
        out_shape=jax.ShapeDtypeStruct((n_rows, ROW_SUBLANES, LANES), u32),
        compiler_params=_params(("arbitrary",)),
        name="dispatch",
    )(pend, padded, n_used, dest_tiles, x2d, m_norm)


def _expert_kernel(n_blocks, ps_ref, nb_ref, nu_ref, xs_hbm, w1_ref, w3_ref, w2_ref,
                   ys_hbm, xbuf, ybuf, h_scr, w1_scr, w3_scr, w2_scr, xsem, ysem):
    e = pl.program_id(0)
    nblk = nb_ref[e]
    g0 = ps_ref[e]
    n_used = nu_ref[0]

    def rows(g):
        return pl.ds(pl.multiple_of(g * EXPERT_ROWS, EXPERT_ROWS), EXPERT_ROWS)

    def x_copy(g):
        return pltpu.make_async_copy(xs_hbm.at[rows(g)], xbuf.at[g % 2], xsem.at[g % 2])

    def y_copy(g):
        return pltpu.make_async_copy(ybuf.at[g % 2], ys_hbm.at[rows(g)], ysem.at[g % 2])

    def normed(slot):
        return _unpack_rows(xbuf[slot]).astype(bf16)

    @pl.when(e == 0)
    def _():
        x_copy(0).start(priority=1)
        x_copy(1).start(priority=1)
        x_copy(0).wait()
        h_scr[...] = normed(0)

    @pl.when(nblk > 0)
    def _():
        w1_scr[...] = w1_ref[...].astype(bf16)
        w3_scr[...] = w3_ref[...].astype(bf16)
        w2_scr[...] = w2_ref[...].astype(bf16)

        def body(g, h):
            @pl.when(g + 2 < n_used)
            def _():
                x_copy(g + 2).start(priority=1)

            @pl.when(g + 1 < n_used)
            def _():
                x_copy(g + 1).wait()

            @pl.when(g >= 2)
            def _():
                y_copy(g - 2).wait()

            h_next = normed((g + 1) % 2)
            y = None
            for c0 in range(0, D_EXPERT, D_EXPERT // 2):
                cols = slice(c0, c0 + D_EXPERT // 2)
                a = _dot(h, w1_scr[:, cols])
                c = _dot(h, w3_scr[:, cols])
                hid = ((a * jax.nn.sigmoid(a)) * c).astype(bf16)
                part = _dot(hid, w2_scr[cols, :])
                y = part if y is None else y + part
            ybuf[g % 2] = _pack_words(y)
            y_copy(g).start(priority=1)
            return h_next

        h_scr[...] = lax.fori_loop(g0, g0 + nblk, body, h_scr[...])

    @pl.when(e == pl.num_programs(0) - 1)
    def _():
        y_copy(n_used - 2).wait()
        y_copy(n_used - 1).wait()
        ybuf[0] = jnp.zeros(ybuf.shape[1:], u32)

        def fill(b):
            r = pl.ds(pl.multiple_of(b * EXPERT_ROWS, EXPERT_ROWS), EXPERT_ROWS)
            return pltpu.make_async_copy(ybuf.at[0], ys_hbm.at[r], ysem.at[0])

        def start(b, c):
            fill(b).start()
            return c

        def wait(b, c):
            fill(b).wait()
            return c

        lax.fori_loop(nu_ref[0], n_blocks, start, 0)
        lax.fori_loop(nu_ref[0], n_blocks, wait, 0)


def _experts(layer, pstart, seg_blocks, n_used, xs, w1, w3, w2):
    n_rows = xs.shape[0]
    n_blocks = n_rows // EXPERT_ROWS

    def w_map(e, *_):
        return (layer, e, 0, 0)

    block_buf = pltpu.VMEM((2, EXPERT_ROWS, ROW_SUBLANES, LANES), u32)
    out_buf = pltpu.VMEM((2, EXPERT_ROWS, D_MODEL // 2), u32)
    grid_spec = pltpu.PrefetchScalarGridSpec(
        num_scalar_prefetch=3,
        grid=(N_EXPERTS,),
        in_specs=[
            pl.BlockSpec(memory_space=pl.ANY),
            pl.BlockSpec((None, None, D_MODEL, D_EXPERT), w_map),
            pl.BlockSpec((None, None, D_MODEL, D_EXPERT), w_map),
            pl.BlockSpec((None, None, D_EXPERT, D_MODEL), w_map),
        ],
        out_specs=pl.BlockSpec(memory_space=pl.ANY),
        scratch_shapes=[
            block_buf,
            out_buf,
            pltpu.VMEM((EXPERT_ROWS, D_MODEL), bf16),
            pltpu.VMEM((D_MODEL, D_EXPERT), bf16),
            pltpu.VMEM((D_MODEL, D_EXPERT), bf16),
            pltpu.VMEM((D_EXPERT, D_MODEL), bf16),
            pltpu.SemaphoreType.DMA((2,)),
            pltpu.SemaphoreType.DMA((2,)),
        ],
    )
    return pl.pallas_call(
        functools.partial(_expert_kernel, n_blocks),
        grid_spec=grid_spec,
        out_shape=jax.ShapeDtypeStruct((n_rows, D_MODEL // 2), u32),
        compiler_params=_params(("arbitrary",)),
        name="experts",
    )(pstart, seg_blocks, n_used, xs, w1, w3, w2)


def _gather_rows(ys, dest):
    n_idx = dest.size
    words = ys.shape[1]
    mesh = plsc.VectorSubcoreMesh(core_axis_name="core", subcore_axis_name="subcore")

    @pl.kernel(out_type=jax.ShapeDtypeStruct((n_idx, words), u32), mesh=mesh)
    def gather(src_hbm, idx_hbm, out_hbm):
        for slab in range(words // GATHER_WIDTH):
            def body(idx_vmem, out_vmem, slab=slab):
                cols = pl.ds(slab * GATHER_WIDTH, GATHER_WIDTH)
                pltpu.sync_copy(src_hbm.at[:, cols].at[idx_vmem.at[0]], out_vmem)

            pltpu.emit_pipeline(
                body,
                grid=(n_idx // GATHER_WINDOW,),
                in_specs=[pl.BlockSpec((1, GATHER_WINDOW), index_map=lambda i: (0, i))],
                out_specs=[pl.BlockSpec((GATHER_WINDOW, GATHER_WIDTH),
                                        index_map=lambda i, slab=slab: (i, slab))],
                core_axis_name=("core", "subcore"),
                dimension_semantics=(pltpu.PARALLEL,),
            )(idx_hbm, out_hbm)

    return gather(ys, dest.reshape(1, n_idx)).reshape(dest.shape + (words,))


def _combined(x_ref, w_ref, y0_ref, y1_ref):
    w = w_ref[...]
    y0 = _unpack_words(y0_ref[...])
    y1 = _unpack_words(y1_ref[...])
    return x_ref[...] + (w[:, 0:1] * y0 + w[:, 1:2] * y1)


def _combine_specs(rows):
    return [
        pl.BlockSpec((rows, D_MODEL), lambda i: (i, 0)),
        pl.BlockSpec((rows, 8), lambda i: (i, 0)),
        pl.BlockSpec((None, rows, D_MODEL // 2), lambda i: (0, i, 0)),
        pl.BlockSpec((None, rows, D_MODEL // 2), lambda i: (1, i, 0)),
    ]


def _combine_kernel(x_ref, w_ref, y0_ref, y1_ref, *rest):
    rest[-1][...] = _combined(x_ref, w_ref, y0_ref, y1_ref)


def _combine(x2d, w_cols, ys, dest):
    T = x2d.shape[0]
    chunk = T // COMBINE_CHUNKS
    tiles = chunk // MOVE_TILE
    out = None
    for c in range(COMBINE_CHUNKS):
        yg = _gather_rows(ys, dest[:, c * chunk:(c + 1) * chunk])
        base = c * tiles
        specs = [
            pl.BlockSpec((MOVE_TILE, D_MODEL), lambda i, base=base: (base + i, 0)),
            pl.BlockSpec((MOVE_TILE, 8), lambda i, base=base: (base + i, 0)),
            pl.BlockSpec((None, MOVE_TILE, D_MODEL // 2), lambda i: (0, i, 0)),
            pl.BlockSpec((None, MOVE_TILE, D_MODEL // 2), lambda i: (1, i, 0)),
        ]
        args = [x2d, w_cols, yg, yg]
        if out is not None:
            specs.append(pl.BlockSpec(memory_space=pl.ANY))
            args.append(out)
        out = pl.pallas_call(
            _combine_kernel,
            grid=(tiles,),
            in_specs=specs,
            out_specs=pl.BlockSpec((MOVE_TILE, D_MODEL), lambda i, base=base: (base + i, 0)),
            out_shape=jax.ShapeDtypeStruct((T, D_MODEL), f32),
            input_output_aliases={} if c == 0 else {4: 0},
            compiler_params=_params(("parallel",)),
            name="combine",
        )(*args)
    return out


def _moe(layer, x2d, logits_t, m_norm, b_group, b_expert, w1, w3, w2):
    T = x2d.shape[0]
    n_blocks = (T * TOP_K) // EXPERT_ROWS + N_EXPERTS
    n_rows = n_blocks * EXPERT_ROWS

    be = b_expert.reshape(N_EXPERTS, 1)
    bg = jnp.concatenate([b_group, jnp.zeros((8 - N_GROUPS,), f32)]).reshape(8, 1)
    oi, w_cols, cnt = _route(logits_t, be, bg)

    counts = cnt[:, 0].astype(jnp.int32)
    padded = (counts + EXPERT_ROWS - 1) // EXPERT_ROWS * EXPERT_ROWS
    pend = jnp.cumsum(padded)
    pstart = pend - padded
    n_used = (pend[-1] // EXPERT_ROWS).astype(jnp.int32).reshape(1)
    e_ids = jnp.arange(N_EXPERTS, dtype=jnp.int32)[:, None, None]
    seg_start = jnp.sum(jnp.where(oi[None, 0:2] == e_ids, pstart[:, None, None], 0), axis=0)
    dest = seg_start + oi[2:4]
    dest_tiles = dest.reshape(TOP_K, T // MOVE_TILE, MOVE_TILE).transpose(1, 0, 2).reshape(
        T // MOVE_TILE, 1, TOP_K * MOVE_TILE)

    xs = _dispatch(pend.astype(jnp.int32), padded, n_used, dest_tiles, x2d, m_norm, n_rows)
    ys = _experts(layer, (pstart // EXPERT_ROWS).astype(jnp.int32), padded // EXPERT_ROWS,
                  n_used, xs, w1, w3, w2)
    return w_cols, ys, dest


def _log_sigmoid(x):
    return jnp.minimum(x, 0.0) - jnp.log1p(jnp.exp(-jnp.abs(x)))


def _proj_kernel(tiles_per_seq, x_ref, w_ref, y0_ref, y1_ref, kvn_ref, bn_ref, kvw_ref, bf_ref,
                 qgw_ref, gq_ref, gk_ref, x_out_ref, q_ref, kt_ref, v_ref, sg_ref, carry_scr,
                 wkt_ref, wv_ref, wf_ref, wq_ref, wg_ref):
    i = pl.program_id(0)

    @pl.when(i == 0)
    def _():
        wkt_ref[...] = kvw_ref[:D_MODEL, :].astype(bf16)
        wv_ref[...] = kvw_ref[D_MODEL:2 * D_MODEL, :].T.astype(bf16)
        wf_ref[...] = jnp.concatenate(
            [kvw_ref[2 * D_MODEL:, :].T, jnp.zeros((D_MODEL, LANES - N_HEADS), f32)],
            axis=1).astype(bf16)
        wg_ref[...] = qgw_ref[:, D_MODEL:].astype(bf16)
        zeros = jnp.zeros((D_MODEL, HEAD_PAD - HEAD_DIM), f32)
        for h in range(N_HEADS):
            wq_ref[:, h * HEAD_PAD:(h + 1) * HEAD_PAD] = jnp.concatenate(
                [qgw_ref[:, h * HEAD_DIM:(h + 1) * HEAD_DIM], zeros], axis=1).astype(bf16)

    x = _combined(x_ref, w_ref, y0_ref, y1_ref)
    x_out_ref[...] = x
    hkv = _rms(x, kvn_ref[...]).astype(bf16)
    hq = _rms(x, bn_ref[...]).astype(bf16)

    v_ref[...] = _dot(hkv, wv_ref[...]).astype(bf16)
    sg_ref[...] = jax.nn.sigmoid(_dot(hq, wg_ref[...])).astype(bf16)

    logf = _log_sigmoid(_dot(hkv, wf_ref[...]) + bf_ref[...])
    tm = x.shape[0]
    incl = (lax.broadcasted_iota(jnp.int32, (tm, tm), 0)
            >= lax.broadcasted_iota(jnp.int32, (tm, tm), 1)).astype(bf16)

    @pl.when(i % tiles_per_seq == 0)
    def _():
        carry_scr[...] = jnp.zeros_like(carry_scr)

    parts = _dot(incl, jnp.concatenate(_split3(logf), axis=1).astype(bf16))
    cum = (parts[:, :LANES] + parts[:, LANES:2 * LANES] + parts[:, 2 * LANES:]
           + carry_scr[...])
    carry_scr[...] = cum[tm - 1:tm, :]
    cum = cum * LOG2E
    cum_t = cum.T

    lane = lax.broadcasted_iota(jnp.int32, (tm, HEAD_PAD), 1)
    sub = lax.broadcasted_iota(jnp.int32, (HEAD_DIM, tm), 0)
    q_raw = _dot(hq, wq_ref[...])
    k_raw_t = _dot_nt(wkt_ref[...], hkv)
    scale = HEAD_DIM ** -0.5 * LOG2E

    for h in range(N_HEADS):
        sl = slice(h * HEAD_PAD, (h + 1) * HEAD_PAD)
        qb = q_raw[:, sl]
        q_ms = jnp.sum(qb * qb, axis=-1, keepdims=True) * (1.0 / HEAD_DIM)
        qn = qb * lax.rsqrt(q_ms + EPS) * gq_ref[...] * scale
        c_hi, c_mid, c_lo = _split3(cum[:, h:h + 1])
        qa = jnp.where(lane < HEAD_DIM, qn,
             jnp.where(lane < AUX_CUM0, 1.0,
             jnp.where(lane == AUX_CUM0, c_hi,
             jnp.where(lane == AUX_CUM0 + 1, c_mid,
             jnp.where(lane == AUX_CUM0 + 2, c_lo, 0.0)))))
        q_ref[:, sl] = qa.astype(bf16)

        kb = k_raw_t[h * HEAD_DIM:(h + 1) * HEAD_DIM, :]
        k_ms = jnp.sum(kb * kb, axis=0, keepdims=True) * (1.0 / HEAD_DIM)
        kn = kb * lax.rsqrt(k_ms + EPS) * gk_ref[...]
        t_hi, t_mid, t_lo = _split3(cum_t[h:h + 1, :])
        aux = jnp.where(sub == 0, -t_hi,
              jnp.where(sub == 1, -t_mid,
              jnp.where(sub == 2, -t_lo,
              jnp.where(sub < 6, 1.0, 0.0))))
        sl = slice(h * HEAD_PAD, (h + 1) * HEAD_PAD)
        ka = jnp.concatenate([kn, aux], axis=0).astype(bf16)
        for t in range(tm // ATT_K):
            kt_ref[t, sl, :] = ka[:, t * ATT_K:(t + 1) * ATT_K]


def _proj(x2d, w_cols, yg, seq, kv_norm, b_norm, kv_w, bfv, w_qg, gq, gk):
    T = x2d.shape[0]
    n_tiles = T // ROW_TILE
    qw = N_HEADS * HEAD_PAD
    return pl.pallas_call(
        functools.partial(_proj_kernel, seq // ROW_TILE),
        grid=(n_tiles,),
        in_specs=_combine_specs(ROW_TILE) + [
            _const_spec((1, D_MODEL)),
            _const_spec((1, D_MODEL)),
            _const_spec(kv_w.shape, single=True),
            _const_spec((1, LANES)),
            _const_spec(w_qg.shape, single=True),
            _const_spec((1, HEAD_PAD)),
            _const_spec((HEAD_DIM, 1)),
        ],
        out_specs=[
            pl.BlockSpec((ROW_TILE, D_MODEL), lambda i: (i, 0)),
            pl.BlockSpec((ROW_TILE, qw), lambda i: (i, 0)),
            pl.BlockSpec((ROW_TILE // ATT_K, qw, ATT_K), lambda i: (i, 0, 0)),
            pl.BlockSpec((ROW_TILE, D_MODEL), lambda i: (i, 0)),
            pl.BlockSpec((ROW_TILE, D_MODEL), lambda i: (i, 0)),
        ],
        out_shape=[
            jax.ShapeDtypeStruct((T, D_MODEL), f32),
            jax.ShapeDtypeStruct((T, qw), bf16),
            jax.ShapeDtypeStruct((T // ATT_K, qw, ATT_K), bf16),
            jax.ShapeDtypeStruct((T, D_MODEL), bf16),
            jax.ShapeDtypeStruct((T, D_MODEL), bf16),
        ],
        scratch_shapes=[
            pltpu.VMEM((1, LANES), f32),
            pltpu.VMEM((D_MODEL, D_MODEL), bf16),
            pltpu.VMEM((D_MODEL, D_MODEL), bf16),
            pltpu.VMEM((D_MODEL, LANES), bf16),
            pltpu.VMEM((D_MODEL, qw), bf16),
            pltpu.VMEM((D_MODEL, D_MODEL), bf16),
        ],
        compiler_params=_params(("arbitrary",)),
        name="proj",
    )(x2d, w_cols, yg, yg, kv_norm, b_norm, kv_w, bfv, w_qg, gq, gk)


def _attn_kernel(q_ref, kt_ref, v_ref, o_ref, s_scr, m_scr, l_scr, acc_scr):
    qi = pl.program_id(2)
    tiles = ATT_Q // ATT_K
    groups = ATT_K // LANES
    heads = [slice(hh * HEAD_PAD, (hh + 1) * HEAD_PAD) for hh in range(2)]
    m_scr[...] = jnp.full(m_scr.shape, -jnp.inf, f32)

    def lane_max(mx, s):
        for g in range(groups):
            mx = jnp.maximum(mx, s[:, g * LANES:(g + 1) * LANES])
        return mx

    def probs(s, mb):
        ps = [jnp.exp2(s[:, g * LANES:(g + 1) * LANES] - mb) for g in range(groups)]
        return ps, functools.reduce(lambda a, b: a + b, ps)

    def body_a(trip, c):
        for t in range(tiles):
            kt_idx = trip * tiles + t
            for hh in range(2):
                s = _dot(q_ref[:, heads[hh]], kt_ref[kt_idx, heads[hh], :])
                s_scr[hh, kt_idx] = s
                m_scr[hh] = lane_max(m_scr[hh], s)
        return c

    lax.fori_loop(0, qi, body_a, 0)
    for t in range(tiles):
        kt_idx = qi * tiles + t
        rows = ATT_Q - t * ATT_K
        visible = (lax.broadcasted_iota(jnp.int32, (rows, ATT_K), 1)
                   <= lax.broadcasted_iota(jnp.int32, (rows, ATT_K), 0))
        for hh in range(2):
            s = _dot(q_ref[t * ATT_K:, heads[hh]], kt_ref[kt_idx, heads[hh], :])
            s = jnp.where(visible, s, -jnp.inf)
            s_scr[hh, kt_idx, t * ATT_K:, :] = s
            m_scr[hh, t * ATT_K:, :] = lane_max(m_scr[hh, t * ATT_K:, :], s)

    for hh in range(2):
        row_max = jnp.max(m_scr[hh], axis=-1, keepdims=True)
        m_scr[hh] = jnp.broadcast_to(row_max, (ATT_Q, LANES))
    l_scr[...] = jnp.zeros(l_scr.shape, f32)
    acc_scr[...] = jnp.zeros(acc_scr.shape, f32)

    def body_b(trip, c):
        for hh in range(2):
            mb = m_scr[hh]
            lsum, acc = l_scr[hh], acc_scr[hh]
            for t0 in range(0, tiles, PV_TILES):
                kt_idx = trip * tiles + t0
                row0 = pl.multiple_of(kt_idx * ATT_K, PV_TILES * ATT_K)
                vb = v_ref[pl.ds(row0, PV_TILES * ATT_K), :]
                parts = []
                for t in range(PV_TILES):
                    ps, psum = probs(s_scr[hh, kt_idx + t], mb)
                    parts += ps
                    lsum = lsum + psum
                acc = acc + _dot(jnp.concatenate(parts, axis=1).astype(bf16), vb)
            l_scr[hh] = lsum
            acc_scr[hh] = acc
        return c

    lax.fori_loop(0, qi, body_b, 0)
    for t in range(tiles):
        kt_idx = qi * tiles + t
        row0 = pl.multiple_of(kt_idx * ATT_K, ATT_K)
        vb = v_ref[pl.ds(row0, ATT_K), :]
        for hh in range(2):
            ps, psum = probs(s_scr[hh, kt_idx, t * ATT_K:, :], m_scr[hh, t * ATT_K:, :])
            l_scr[hh, t * ATT_K:, :] += psum
            acc_scr[hh, t * ATT_K:, :] += _dot(jnp.concatenate(ps, axis=1).astype(bf16), vb)

    lane = lax.broadcasted_iota(jnp.int32, (ATT_Q, 2 * HEAD_DIM), 1)
    o0 = acc_scr[0] / jnp.sum(l_scr[0], axis=-1, keepdims=True)
    o1 = acc_scr[1] / jnp.sum(l_scr[1], axis=-1, keepdims=True)
    o_ref[...] = jnp.where(lane < HEAD_DIM, o0, o1).astype(bf16)


def _attention(q, kt, v, batch, seq):
    T = q.shape[0]
    nq = seq // ATT_Q
    nk = seq // ATT_K
    pairs = N_HEADS // 2
    return pl.pallas_call(
        _attn_kernel,
        grid=(batch, pairs, nq),
        in_specs=[
            pl.BlockSpec((ATT_Q, 2 * HEAD_PAD), lambda b, p, i: (b * nq + i, p)),
            pl.BlockSpec((nk, 2 * HEAD_PAD, ATT_K), lambda b, p, i: (b, p, 0)),
            pl.BlockSpec((seq, 2 * HEAD_DIM), lambda b, p, i: (b, p)),
        ],
        out_specs=pl.BlockSpec((ATT_Q, 2 * HEAD_DIM), lambda b, p, i: (b * nq + i, p)),
        out_shape=jax.ShapeDtypeStruct((T, D_MODEL), bf16),
        scratch_shapes=[
            pltpu.VMEM((2, nk, ATT_Q, ATT_K), f32),
            pltpu.VMEM((2, ATT_Q, LANES), f32),
            pltpu.VMEM((2, ATT_Q, LANES), f32),
            pltpu.VMEM((2, ATT_Q, 2 * HEAD_DIM), f32),
        ],
        compiler_params=_params(("parallel", "parallel", "parallel")),
        name="attn",
    )(q, kt, v)


def _attn_out_kernel(x_ref, o_ref, sg_ref, wo_ref, mn_ref, wr_ref, x3_ref, lg_ref):
    for r0 in range(0, SGU_TILE, EPILOGUE_ROWS):
        rows = slice(r0, r0 + EPILOGUE_ROWS)
        gated = (o_ref[rows, :].astype(f32) * sg_ref[rows, :].astype(f32)).astype(bf16)
        x3 = x_ref[rows, :] + _dot(gated, wo_ref[...])
        x3_ref[rows, :] = x3
        lg_ref[:, rows] = _router_logits_t(x3, mn_ref[...], wr_ref[...])


def _attn_out(x2d, o, sg, wo, m_norm, w_router):
    T = x2d.shape[0]
    return pl.pallas_call(
        _attn_out_kernel,
        grid=(T // SGU_TILE,),
        in_specs=[
            pl.BlockSpec((SGU_TILE, D_MODEL), lambda i: (i, 0)),
            pl.BlockSpec((SGU_TILE, D_MODEL), lambda i: (i, 0)),
            pl.BlockSpec((SGU_TILE, D_MODEL), lambda i: (i, 0)),
            _const_spec((D_MODEL, D_MODEL)),
            _const_spec((1, D_MODEL)),
            _const_spec((2 * D_MODEL, 2 * LANES)),
        ],
        out_specs=[
            pl.BlockSpec((SGU_TILE, D_MODEL), lambda i: (i, 0)),
            pl.BlockSpec((ROUTER_ROWS, SGU_TILE), lambda i: (0, i)),
        ],
        out_shape=[
            jax.ShapeDtypeStruct((T, D_MODEL), f32),
            jax.ShapeDtypeStruct((ROUTER_ROWS, T), f32),
        ],
        compiler_params=_params(("parallel",)),
        name="attn_out",
    )(x2d, o, sg, wo, m_norm, w_router)


def _router_weight(w_group, w_expert):
    pad = jnp.zeros((D_MODEL, LANES - N_EXPERTS - N_GROUPS), f32)
    w = jnp.concatenate([w_expert, w_group, pad], axis=1)
    w_hi = w.astype(bf16)
    w_lo = (w - w_hi.astype(f32)).astype(bf16)
    top = jnp.concatenate([w_hi, w_lo], axis=1)
    bottom = jnp.concatenate([w_hi, jnp.zeros_like(w_hi)], axis=1)
    return jnp.concatenate([top, bottom], axis=0)


def kernel(x, a_norm, a_w_in, a_b_in, a_v_norm, a_w_s, a_b_s, a_w_out, kv_norm, kv_w, kv_b_f,
           k_norm, b_norm, b_w_qg, q_norm, b_w_out, m_norm, m_w_group, m_b_group, m_w_expert,
           m_b_expert, m_w1, m_w3, m_w2):
    batch, seq, _ = x.shape
    T = batch * seq
    x2d = x.reshape(T, D_MODEL)

    x1, lg0 = _sgu(
        x2d, a_norm[0].reshape(1, -1), a_w_in[0], a_b_in[0].reshape(1, -1),
        a_v_norm[0].reshape(1, -1), a_w_s[0], a_b_s[0].T, a_w_out[0].astype(bf16),
        m_norm[0].reshape(1, -1), _router_weight(m_w_group[0], m_w_expert[0]))
    w_cols0, ys0, dest0 = _moe(0, x1, lg0, m_norm[0].reshape(1, -1), m_b_group[0],
                               m_b_expert[0], m_w1, m_w3, m_w2)
    yg0 = _gather_rows(ys0, dest0)

    bfv = jnp.pad(kv_b_f, (0, LANES - N_HEADS)).reshape(1, LANES)
    gq = jnp.pad(q_norm[0], (0, HEAD_PAD - HEAD_DIM)).reshape(1, HEAD_PAD)
    gk = k_norm.reshape(HEAD_DIM, 1)
    x2, q, kt, v, sg = _proj(x1, w_cols0, yg0, seq, kv_norm.reshape(1, -1),
                             b_norm[0].reshape(1, -1), kv_w.T, bfv, b_w_qg[0], gq, gk)
    o = _attention(q, kt, v, batch, seq)
    x3, lg1 = _attn_out(x2, o, sg, b_w_out[0].astype(bf16), m_norm[1].reshape(1, -1),
                        _router_weight(m_w_group[1], m_w_expert[1]))
    w_cols1, ys1, dest1 = _moe(1, x3, lg1, m_norm[1].reshape(1, -1), m_b_group[1],
                               m_b_expert[1], m_w1, m_w3, m_w2)
    x4 = _combine(x3, w_cols1, ys1, dest1)
    return x4.reshape(batch, seq, D_MODEL)
```
